```python
import math
import jax, jax.numpy as jnp
from jax import lax
import numpy as np

D_MODEL = 1024
BATCH = 32
SEQ = 2048
DEPTH = 1

MIX_W = D_MODEL
POOL_W = (3 * D_MODEL) // 8
SSM_W = (3 * D_MODEL) // 8
ATT_W = D_MODEL - POOL_W - SSM_W
IN_W = 2 * MIX_W
POOL_WINDOWS = (2, 4, 8, 16)
POOL_GROUPS = len(POOL_WINDOWS)
POOL_GW = POOL_W // POOL_GROUPS
SSM_GROUP = 16
SSM_NG = SSM_W // SSM_GROUP
SSM_N = 64
DT_MIN = 1e-3
DT_MAX = 1e-1
N_MEM = 256
MEM_HEADS = 4
MEM_HD = ATT_W // MEM_HEADS
EPS = 1e-6

kernel_name = "hybrid_pool_s5_memattn_layer"


def rmsnorm(x, g):
    xf = x.astype(jnp.float32)
    xf = xf * lax.rsqrt(jnp.mean(xf * xf, axis=-1, keepdims=True) + EPS)
    return (xf * g.astype(jnp.float32)).astype(x.dtype)


def pool_mixer(u, w_pool, pool_scale):
    b, l, _ = u.shape
    uf = u.astype(jnp.float32)
    cs0 = jnp.concatenate([jnp.zeros((b, 1, POOL_W), jnp.float32), jnp.cumsum(uf, axis=1)], axis=1)
    pos = jnp.arange(1, l + 1, dtype=jnp.float32)[None, :, None]
    outs = []
    for gi, w in enumerate(POOL_WINDOWS):
        sl = slice(gi * POOL_GW, (gi + 1) * POOL_GW)
        c = cs0[..., sl]
        lower = jnp.concatenate([jnp.zeros((b, w - 1, POOL_GW), jnp.float32), c[:, :l - w + 1]], axis=1)
        mean = (c[:, 1:] - lower) / jnp.minimum(pos, float(w))
        outs.append(jnp.einsum('blc,cd->bld', mean - uf[..., sl], w_pool[gi].astype(jnp.float32)))
    y = jnp.concatenate(outs, axis=-1) * pool_scale.astype(jnp.float32)
    return y.astype(u.dtype)


def _ssm_combine(e1, e2):
    a1, b1 = e1
    a2, b2 = e2
    return a1 * a2, a2 * b1 + b2


def s5_mixer(u, a_re, a_im, log_dt, b_re, b_im, c_re, c_im, d_skip, w_glu):
    bsz, l, _ = u.shape
    f32 = jnp.float32
    uf = u.astype(f32).reshape(bsz, l, SSM_NG, SSM_GROUP)
    lam = lax.complex(a_re.astype(f32), a_im.astype(f32))
    dt = jnp.exp(log_dt.astype(f32))[:, None]
    lam_bar = jnp.exp(lam * dt)
    b_mat = lax.complex(b_re.astype(f32), b_im.astype(f32))
    c_mat = lax.complex(c_re.astype(f32), c_im.astype(f32))
    b_bar = ((lam_bar - 1.0) / lam)[..., None] * b_mat
    bu = jnp.einsum('blgc,gnc->blgn', uf.astype(jnp.complex64), b_bar)
    lam_all = jnp.broadcast_to(lam_bar, bu.shape)
    _, hs = lax.associative_scan(_ssm_combine, (lam_all, bu), axis=1)
    y = jnp.einsum('blgn,gcn->blgc', hs, c_mat).real + d_skip.astype(f32).reshape(SSM_NG, SSM_GROUP) * uf
    y = jax.nn.gelu(y.reshape(bsz, l, SSM_W))
    z = y @ w_glu.astype(f32)
    out = z[..., :SSM_W] * jax.nn.sigmoid(z[..., SSM_W:])
    return out.astype(u.dtype)


def memory_attention(q, mem, g_mem, w_kv):
    bsz, l, _ = q.shape
    m = rmsnorm(mem, g_mem)
    kv = m @ w_kv
    k = kv[..., :ATT_W].reshape(bsz, -1, MEM_HEADS, MEM_HD).astype(jnp.float32)
    v = kv[..., ATT_W:].reshape(bsz, -1, MEM_HEADS, MEM_HD).astype(jnp.float32)
    qh = q.reshape(bsz, l, MEM_HEADS, MEM_HD).astype(jnp.float32)
    s = jnp.einsum('blhd,bmhd->bhlm', qh, k) * (MEM_HD ** -0.5)
    p = jax.nn.softmax(s, axis=-1)
    o = jnp.einsum('bhlm,bmhd->blhd', p, v).reshape(bsz, l, ATT_W)
    return o.astype(q.dtype)


def _fwd_setup_inputs(seed: int = 0) -> dict:
    key = jax.random.key(seed)
    ks = jax.random.split(key, 20)
    f32 = jnp.float32
    nrm = lambda k, shape, s: jax.random.normal(k, shape, f32) * s
    n_idx = jnp.arange(SSM_N, dtype=f32)
    a_re = -0.5 + nrm(ks[5], (DEPTH, SSM_NG, SSM_N), 1e-2)
    a_im = math.pi * n_idx[None, None, :] + nrm(ks[6], (DEPTH, SSM_NG, SSM_N), 1e-2)
    log_dt = jax.random.uniform(ks[7], (DEPTH, SSM_NG), f32, math.log(DT_MIN), math.log(DT_MAX))
    return {
        "x": nrm(ks[0], (BATCH, SEQ, D_MODEL), 1.0),
        "mem": nrm(ks[1], (BATCH, N_MEM, D_MODEL), 1.0),
        "g_pre": 1.0 + nrm(ks[2], (DEPTH, D_MODEL), 0.02),
        "w_in": nrm(ks[3], (DEPTH, D_MODEL, IN_W), D_MODEL ** -0.5),
        "w_pool": nrm(ks[4], (DEPTH, POOL_GROUPS, POOL_GW, POOL_GW), POOL_GW ** -0.5),
        "pool_scale": 1.0 + nrm(ks[8], (DEPTH, POOL_W), 0.02),
        "a_re": a_re,
        "a_im": a_im,
        "log_dt": log_dt,
        "b_re": nrm(ks[9], (DEPTH, SSM_NG, SSM_N, SSM_GROUP), (2 * SSM_GROUP) ** -0.5),
        "b_im": nrm(ks[10], (DEPTH, SSM_NG, SSM_N, SSM_GROUP), (2 * SSM_GROUP) ** -0.5),
        "c_re": nrm(ks[11], (DEPTH, SSM_NG, SSM_GROUP, SSM_N), (2 * SSM_N) ** -0.5),
        "c_im": nrm(ks[12], (DEPTH, SSM_NG, SSM_GROUP, SSM_N), (2 * SSM_N) ** -0.5),
        "d_skip": nrm(ks[13], (DEPTH, SSM_W), 1.0),
        "w_glu": nrm(ks[14], (DEPTH, SSM_W, 2 * SSM_W), SSM_W ** -0.5),
        "g_mem": 1.0 + nrm(ks[15], (DEPTH, D_MODEL), 0.02),
        "w_kv": nrm(ks[16], (DEPTH, D_MODEL, 2 * ATT_W), D_MODEL ** -0.5),
        "w_out": nrm(ks[17], (DEPTH, MIX_W, D_MODEL), MIX_W ** -0.5),
        "g_post": 1.0 + nrm(ks[18], (DEPTH, D_MODEL), 0.02),
    }


def _fwd_reference(x, mem, g_pre, w_in, w_pool, pool_scale, a_re, a_im, log_dt, b_re, b_im,
              c_re, c_im, d_skip, w_glu, g_mem, w_kv, w_out, g_post):
    for i in range(DEPTH):
        h = rmsnorm(x, g_pre[i])
        proj = h @ w_in[i]
        val, gate = proj[..., :MIX_W], proj[..., MIX_W:]
        u_pool = val[..., :POOL_W]
        u_ssm = val[..., POOL_W:POOL_W + SSM_W]
        q = val[..., POOL_W + SSM_W:]
        y_pool = pool_mixer(u_pool, w_pool[i], pool_scale[i])
        y_ssm = s5_mixer(u_ssm, a_re[i], a_im[i], log_dt[i], b_re[i], b_im[i],
                         c_re[i], c_im[i], d_skip[i], w_glu[i])
        y_att = memory_attention(q, mem, g_mem[i], w_kv[i])
        y = jnp.concatenate([y_pool, y_ssm, y_att], axis=-1) * jax.nn.silu(gate)
        out = y @ w_out[i]
        x = x + rmsnorm(out, g_post[i])
    return x


import jax as _jax
import jax.numpy as _jnp

TWIN_FORMAT = 'train_step'
FWD_PARAMS = ['x', 'mem', 'g_pre', 'w_in', 'w_pool', 'pool_scale', 'a_re', 'a_im', 'log_dt', 'b_re', 'b_im', 'c_re', 'c_im', 'd_skip', 'w_glu', 'g_mem', 'w_kv', 'w_out', 'g_post']
TWIN_WEIGHTS = ['g_pre', 'w_in', 'w_pool', 'pool_scale', 'a_re', 'a_im', 'log_dt', 'b_re', 'b_im', 'c_re', 'c_im', 'd_skip', 'w_glu', 'g_mem', 'w_kv', 'w_out', 'g_post']
TWIN_DIFF_INPUT = 'x'
TWIN_INPUTS = ['x', 'mem', 'g_pre', 'w_in', 'w_pool', 'pool_scale', 'a_re', 'a_im', 'log_dt', 'b_re', 'b_im', 'c_re', 'c_im', 'd_skip', 'w_glu', 'g_mem', 'w_kv', 'w_out', 'g_post', 'loss_target', 'm_g_pre', 'm_w_in', 'm_w_pool', 'm_pool_scale', 'm_a_re', 'm_a_im', 'm_log_dt', 'm_b_re', 'm_b_im', 'm_c_re', 'm_c_im', 'm_d_skip', 'm_w_glu', 'm_g_mem', 'm_w_kv', 'm_w_out', 'm_g_post', 'v_g_pre', 'v_w_in', 'v_w_pool', 'v_pool_scale', 'v_a_re', 'v_a_im', 'v_log_dt', 'v_b_re', 'v_b_im', 'v_c_re', 'v_c_im', 'v_d_skip', 'v_w_glu', 'v_g_mem', 'v_w_kv', 'v_w_out', 'v_g_post']
TWIN_OUTPUTS = ['loss', 'grad_x', 'grad_g_pre', 'grad_w_in', 'grad_w_pool', 'grad_pool_scale', 'grad_a_re', 'grad_a_im', 'grad_log_dt', 'grad_b_re', 'grad_b_im', 'grad_c_re', 'grad_c_im', 'grad_d_skip', 'grad_w_glu', 'grad_g_mem', 'grad_w_kv', 'grad_w_out', 'grad_g_post', 'delta_g_pre', 'delta_w_in', 'delta_w_pool', 'delta_pool_scale', 'delta_a_re', 'delta_a_im', 'delta_log_dt', 'delta_b_re', 'delta_b_im', 'delta_c_re', 'delta_c_im', 'delta_d_skip', 'delta_w_glu', 'delta_g_mem', 'delta_w_kv', 'delta_w_out', 'delta_g_post', 'new_m_g_pre', 'new_m_w_in', 'new_m_w_pool', 'new_m_pool_scale', 'new_m_a_re', 'new_m_a_im', 'new_m_log_dt', 'new_m_b_re', 'new_m_b_im', 'new_m_c_re', 'new_m_c_im', 'new_m_d_skip', 'new_m_w_glu', 'new_m_g_mem', 'new_m_w_kv', 'new_m_w_out', 'new_m_g_post', 'new_v_g_pre', 'new_v_w_in', 'new_v_w_pool', 'new_v_pool_scale', 'new_v_a_re', 'new_v_a_im', 'new_v_log_dt', 'new_v_b_re', 'new_v_b_im', 'new_v_c_re', 'new_v_c_im', 'new_v_d_skip', 'new_v_w_glu', 'new_v_g_mem', 'new_v_w_kv', 'new_v_w_out', 'new_v_g_post']
TWIN_LEAF_KINDS = {'loss': 'loss', 'grad_x': 'grad_x', 'grad_g_pre': 'grad_w', 'grad_w_in': 'grad_w', 'grad_w_pool': 'grad_w', 'grad_pool_scale': 'grad_w', 'grad_a_re': 'grad_w', 'grad_a_im': 'grad_w', 'grad_log_dt': 'grad_w', 'grad_b_re': 'grad_w', 'grad_b_im': 'grad_w', 'grad_c_re': 'grad_w', 'grad_c_im': 'grad_w', 'grad_d_skip': 'grad_w', 'grad_w_glu': 'grad_w', 'grad_g_mem': 'grad_w', 'grad_w_kv': 'grad_w', 'grad_w_out': 'grad_w', 'grad_g_post': 'grad_w', 'delta_g_pre': 'delta_w', 'delta_w_in': 'delta_w', 'delta_w_pool': 'delta_w', 'delta_pool_scale': 'delta_w', 'delta_a_re': 'delta_w', 'delta_a_im': 'delta_w', 'delta_log_dt': 'delta_w', 'delta_b_re': 'delta_w', 'delta_b_im': 'delta_w', 'delta_c_re': 'delta_w', 'delta_c_im': 'delta_w', 'delta_d_skip': 'delta_w', 'delta_w_glu': 'delta_w', 'delta_g_mem': 'delta_w', 'delta_w_kv': 'delta_w', 'delta_w_out': 'delta_w', 'delta_g_post': 'delta_w', 'new_m_g_pre': 'new_m', 'new_m_w_in': 'new_m', 'new_m_w_pool': 'new_m', 'new_m_pool_scale': 'new_m', 'new_m_a_re': 'new_m', 'new_m_a_im': 'new_m', 'new_m_log_dt': 'new_m', 'new_m_b_re': 'new_m', 'new_m_b_im': 'new_m', 'new_m_c_re': 'new_m', 'new_m_c_im': 'new_m', 'new_m_d_skip': 'new_m', 'new_m_w_glu': 'new_m', 'new_m_g_mem': 'new_m', 'new_m_w_kv': 'new_m', 'new_m_w_out': 'new_m', 'new_m_g_post': 'new_m', 'new_v_g_pre': 'new_v', 'new_v_w_in': 'new_v', 'new_v_w_pool': 'new_v', 'new_v_pool_scale': 'new_v', 'new_v_a_re': 'new_v', 'new_v_a_im': 'new_v', 'new_v_log_dt': 'new_v', 'new_v_b_re': 'new_v', 'new_v_b_im': 'new_v', 'new_v_c_re': 'new_v', 'new_v_c_im': 'new_v', 'new_v_d_skip': 'new_v', 'new_v_w_glu': 'new_v', 'new_v_g_mem': 'new_v', 'new_v_w_kv': 'new_v', 'new_v_w_out': 'new_v', 'new_v_g_post': 'new_v'}


def _forward(args):
    return _fwd_reference(*[args[k] for k in FWD_PARAMS])


def _output_shape():
    out = _jax.eval_shape(lambda: _forward(_fwd_setup_inputs(0)))
    return out.shape, out.dtype

N_MICROBATCH = 1
ADAM_LR = 0.001
ADAM_B1 = 0.9
ADAM_B2 = 0.999
ADAM_EPS = 1e-08
ADAM_WD = 0.01
ADAM_STEP = 10
PER_EXAMPLE_BATCH_AXIS = {'x': 0, 'mem': 0, 'loss_target': 0}
SHARED_INPUTS = []
_WEIGHT_DTYPES = {'g_pre': _jnp.float32, 'w_in': _jnp.float32, 'w_pool': _jnp.float32, 'pool_scale': _jnp.float32, 'a_re': _jnp.float32, 'a_im': _jnp.float32, 'log_dt': _jnp.float32, 'b_re': _jnp.float32, 'b_im': _jnp.float32, 'c_re': _jnp.float32, 'c_im': _jnp.float32, 'd_skip': _jnp.float32, 'w_glu': _jnp.float32, 'g_mem': _jnp.float32, 'w_kv': _jnp.float32, 'w_out': _jnp.float32, 'g_post': _jnp.float32}
MOMENT_SCALE = {'g_pre': 5.378959e-01, 'w_in': 4.163477e-01, 'w_pool': 6.550591e-01, 'pool_scale': 6.725785e-01, 'a_re': 1.712915e-02, 'a_im': 1.270657e-02, 'log_dt': 5.411026e+00, 'b_re': 7.910957e-03, 'b_im': 7.896149e-03, 'c_re': 1.586874e-02, 'c_im': 1.649363e-02, 'd_skip': 3.472279e-01, 'w_glu': 2.515013e-01, 'g_mem': 5.272802e-02, 'w_kv': 7.602123e-02, 'w_out': 4.552483e-01, 'g_post': 6.403409e+01}


def _to_microbatches(a, axis):
    t = _jnp.moveaxis(a, axis, 0)
    t = t.reshape((N_MICROBATCH, t.shape[0] // N_MICROBATCH) + t.shape[1:])
    return _jnp.moveaxis(t, 1, axis + 1)


def setup_inputs(seed: int = 0) -> dict:
    inp = _fwd_setup_inputs(seed)
    key = _jax.random.fold_in(_jax.random.key(seed), 7919)
    shape, _ = _output_shape()
    out = dict(inp)
    out["loss_target"] = _jax.random.normal(_jax.random.fold_in(key, 0), shape, _jnp.float32)
    for i, name in enumerate(TWIN_WEIGHTS):
        w = inp[name].astype(_jnp.float32)
        if MOMENT_SCALE is None:
            s = _jnp.sqrt(_jnp.mean(_jnp.square(w)) + 1e-30)
        else:
            s = MOMENT_SCALE[name]
        km, kv = _jax.random.split(_jax.random.fold_in(key, i + 1))
        out[name] = w
        out["m_" + name] = s * _jax.random.normal(km, w.shape, _jnp.float32)
        out["v_" + name] = (s * s) * _jax.random.uniform(kv, w.shape, _jnp.float32, 0.5, 1.5)
    if N_MICROBATCH > 1:
        for name, axis in PER_EXAMPLE_BATCH_AXIS.items():
            out[name] = _to_microbatches(out[name], axis)
    return {'x': out['x'], 'mem': out['mem'], 'g_pre': out['g_pre'], 'w_in': out['w_in'], 'w_pool': out['w_pool'], 'pool_scale': out['pool_scale'], 'a_re': out['a_re'], 'a_im': out['a_im'], 'log_dt': out['log_dt'], 'b_re': out['b_re'], 'b_im': out['b_im'], 'c_re': out['c_re'], 'c_im': out['c_im'], 'd_skip': out['d_skip'], 'w_glu': out['w_glu'], 'g_mem': out['g_mem'], 'w_kv': out['w_kv'], 'w_out': out['w_out'], 'g_post': out['g_post'], 'loss_target': out['loss_target'], 'm_g_pre': out['m_g_pre'], 'm_w_in': out['m_w_in'], 'm_w_pool': out['m_w_pool'], 'm_pool_scale': out['m_pool_scale'], 'm_a_re': out['m_a_re'], 'm_a_im': out['m_a_im'], 'm_log_dt': out['m_log_dt'], 'm_b_re': out['m_b_re'], 'm_b_im': out['m_b_im'], 'm_c_re': out['m_c_re'], 'm_c_im': out['m_c_im'], 'm_d_skip': out['m_d_skip'], 'm_w_glu': out['m_w_glu'], 'm_g_mem': out['m_g_mem'], 'm_w_kv': out['m_w_kv'], 'm_w_out': out['m_w_out'], 'm_g_post': out['m_g_post'], 'v_g_pre': out['v_g_pre'], 'v_w_in': out['v_w_in'], 'v_w_pool': out['v_w_pool'], 'v_pool_scale': out['v_pool_scale'], 'v_a_re': out['v_a_re'], 'v_a_im': out['v_a_im'], 'v_log_dt': out['v_log_dt'], 'v_b_re': out['v_b_re'], 'v_b_im': out['v_b_im'], 'v_c_re': out['v_c_re'], 'v_c_im': out['v_c_im'], 'v_d_skip': out['v_d_skip'], 'v_w_glu': out['v_w_glu'], 'v_g_mem': out['v_g_mem'], 'v_w_kv': out['v_w_kv'], 'v_w_out': out['v_w_out'], 'v_g_post': out['v_g_post']}


def _loss(weights, diff, rest, loss_target):
    with _jax.named_scope("forward"):
        args = {**rest, TWIN_DIFF_INPUT: diff, **{k: w.astype(_WEIGHT_DTYPES[k]) for k, w in weights.items()}}
        y = _forward(args)
    with _jax.named_scope("loss_head"):
        err = _jnp.square(y.astype(_jnp.float32) - loss_target)
        return 0.5 * _jnp.sum(_jnp.mean(err, axis=-1)) if err.ndim else 0.5 * err


def _adamw(w, g, m, v):
    m = ADAM_B1 * m + (1.0 - ADAM_B1) * g
    v = ADAM_B2 * v + (1.0 - ADAM_B2) * _jnp.square(g)
    m_hat = m / (1.0 - ADAM_B1 ** ADAM_STEP)
    v_hat = v / (1.0 - ADAM_B2 ** ADAM_STEP)
    delta = -ADAM_LR * (m_hat / (_jnp.sqrt(v_hat) + ADAM_EPS) + ADAM_WD * w)
    return delta, m, v


def reference(x, mem, g_pre, w_in, w_pool, pool_scale, a_re, a_im, log_dt, b_re, b_im, c_re, c_im, d_skip, w_glu, g_mem, w_kv, w_out, g_post, loss_target, m_g_pre, m_w_in, m_w_pool, m_pool_scale, m_a_re, m_a_im, m_log_dt, m_b_re, m_b_im, m_c_re, m_c_im, m_d_skip, m_w_glu, m_g_mem, m_w_kv, m_w_out, m_g_post, v_g_pre, v_w_in, v_w_pool, v_pool_scale, v_a_re, v_a_im, v_log_dt, v_b_re, v_b_im, v_c_re, v_c_im, v_d_skip, v_w_glu, v_g_mem, v_w_kv, v_w_out, v_g_post):
    given = dict(x=x, mem=mem, g_pre=g_pre, w_in=w_in, w_pool=w_pool, pool_scale=pool_scale, a_re=a_re, a_im=a_im, log_dt=log_dt, b_re=b_re, b_im=b_im, c_re=c_re, c_im=c_im, d_skip=d_skip, w_glu=w_glu, g_mem=g_mem, w_kv=w_kv, w_out=w_out, g_post=g_post, loss_target=loss_target, m_g_pre=m_g_pre, m_w_in=m_w_in, m_w_pool=m_w_pool, m_pool_scale=m_pool_scale, m_a_re=m_a_re, m_a_im=m_a_im, m_log_dt=m_log_dt, m_b_re=m_b_re, m_b_im=m_b_im, m_c_re=m_c_re, m_c_im=m_c_im, m_d_skip=m_d_skip, m_w_glu=m_w_glu, m_g_mem=m_g_mem, m_w_kv=m_w_kv, m_w_out=m_w_out, m_g_post=m_g_post, v_g_pre=v_g_pre, v_w_in=v_w_in, v_w_pool=v_w_pool, v_pool_scale=v_pool_scale, v_a_re=v_a_re, v_a_im=v_a_im, v_log_dt=v_log_dt, v_b_re=v_b_re, v_b_im=v_b_im, v_c_re=v_c_re, v_c_im=v_c_im, v_d_skip=v_d_skip, v_w_glu=v_w_glu, v_g_mem=v_g_mem, v_w_kv=v_w_kv, v_w_out=v_w_out, v_g_post=v_g_post)
    weights = {n: given[n] for n in TWIN_WEIGHTS}
    shared = {n: given[n] for n in SHARED_INPUTS}
    per_example = {n: given[n] for n in ['x', 'mem']}
    grad_fn = _jax.value_and_grad(_loss, argnums=(0, 1))

    def one_microbatch(ex, loss_target):
        ex = dict(ex)
        diff = ex.pop(TWIN_DIFF_INPUT)
        return grad_fn(weights, diff, {**shared, **ex}, loss_target)

    if N_MICROBATCH == 1:
        loss, (grad_w, grad_x) = one_microbatch(per_example, given["loss_target"])
    else:
        def body(carry, xs):
            loss_sum, grad_sum = carry
            l_k, (gw_k, gx_k) = one_microbatch(xs[0], xs[1])
            with _jax.named_scope("update"):
                return (loss_sum + l_k, _jax.tree.map(_jnp.add, grad_sum, gw_k)), gx_k

        init = (_jnp.zeros((), _jnp.float32), _jax.tree.map(_jnp.zeros_like, weights))
        (loss, grad_w), grad_x = _jax.lax.scan(body, init, (per_example, given["loss_target"]))
    with _jax.named_scope("update"):
        delta_w, new_m, new_v = {}, {}, {}
        for n in TWIN_WEIGHTS:
            delta_w[n], new_m[n], new_v[n] = _adamw(weights[n], grad_w[n], given["m_" + n], given["v_" + n])
    return (loss, grad_x, *[grad_w[n] for n in TWIN_WEIGHTS], *[delta_w[n] for n in TWIN_WEIGHTS],
            *[new_m[n] for n in TWIN_WEIGHTS], *[new_v[n] for n in TWIN_WEIGHTS])
```

```python
import functools
import math

import jax
import jax.numpy as jnp
from jax import lax
from jax.experimental import pallas as pl
from jax.experimental.pallas import tpu as pltpu

F32 = jnp.float32
BF16 = jnp.bfloat16

D_MODEL = 1024
POOL_W = 384
SSM_W = 384
ATT_W = 256
POOL_GW = 96
POOL_WINDOWS = (2, 4, 8, 16)
POOL_PAD = 16
SSM_NG = 24
SSM_N = 64
SSM_GC = 16
NST = SSM_NG * SSM_N
N_MEM = 256
MEM_HEADS = 4
MEM_HD = 64
EPS = 1e-6

ADAM_LR = 0.001
ADAM_B1 = 0.9
ADAM_B2 = 0.999
ADAM_EPS = 1e-08
ADAM_WD = 0.01
ADAM_STEP = 10

SUBLANES = 8
LANES = 128
V7X_VMEM_BYTES = 64 * 2**20
VMEM_LIMIT = V7X_VMEM_BYTES - 8 * 2**20
SCAN_COLS = 512
MESH = pl.DeviceIdType.MESH

NT = (((1,), (1,)), ((), ()))
TN = (((0,), (0,)), ((), ()))


def _params(*sem):
    return pltpu.CompilerParams(dimension_semantics=sem, vmem_limit_bytes=VMEM_LIMIT)


def _dot(a, b):
    return jnp.dot(a, b, preferred_element_type=F32)


def _dot_nt(a, b):
    return lax.dot_general(a, b, NT, preferred_element_type=F32)


def _dot_tn(a, b):
    return lax.dot_general(a, b, TN, preferred_element_type=F32)


def _rms_scale(v):
    return lax.rsqrt(jnp.mean(v * v, axis=-1, keepdims=True) + EPS)


def _adamw(w, g, m, v):
    m = ADAM_B1 * m + (1.0 - ADAM_B1) * g
    v = ADAM_B2 * v + (1.0 - ADAM_B2) * (g * g)
    m_hat = m / (1.0 - ADAM_B1 ** ADAM_STEP)
    v_hat = v / (1.0 - ADAM_B2 ** ADAM_STEP)
    delta = -ADAM_LR * (m_hat / (jnp.sqrt(v_hat) + ADAM_EPS) + ADAM_WD * w)
    return delta, m, v


def in_proj(x2, g_pre, w_in):
    t = x2.shape[0]
    tm = min(512, t)

    def body(x_ref, g_ref, w_ref, o_ref):
        x = x_ref[...]
        h = (x * _rms_scale(x) * g_ref[...]).astype(BF16)
        o_ref[...] = _dot(h, w_ref[...])

    return pl.pallas_call(
        body, grid=(t // tm,),
        in_specs=[pl.BlockSpec((tm, D_MODEL), lambda i: (i, 0)),
                  pl.BlockSpec((1, D_MODEL), lambda i: (0, 0)),
                  pl.BlockSpec((D_MODEL, 2 * D_MODEL), lambda i: (0, 0))],
        out_specs=pl.BlockSpec((tm, 2 * D_MODEL), lambda i: (i, 0)),
        out_shape=jax.ShapeDtypeStruct((t, 2 * D_MODEL), F32),
        compiler_params=_params("arbitrary"), name="in_proj")(x2, g_pre, w_in)


def _pool_lane_window(shape):
    ch = lax.broadcasted_iota(jnp.int32, shape, 1)
    return jnp.where(ch < POOL_GW, 2.0, jnp.where(ch < 2 * POOL_GW, 4.0, jnp.where(ch < 3 * POOL_GW, 8.0, 16.0)))


def _pool_select(win, s2, s4, s8, s16):
    return jnp.where(win == 2.0, s2, jnp.where(win == 4.0, s4, jnp.where(win == 8.0, s8, s16)))


def _pool_counts(l, win):
    pos = (lax.broadcasted_iota(jnp.int32, (l, POOL_W), 0) + 1).astype(F32)
    return jnp.minimum(pos, win)


def _pool_diff(u, pad_ref, l):
    lo = POOL_PAD
    pad_ref[pl.ds(lo, l), :] = u
    s2 = u + pad_ref[pl.ds(lo - 1, l), :]
    pad_ref[pl.ds(lo, l), :] = s2
    s4 = s2 + pad_ref[pl.ds(lo - 2, l), :]
    pad_ref[pl.ds(lo, l), :] = s4
    s8 = s4 + pad_ref[pl.ds(lo - 4, l), :]
    pad_ref[pl.ds(lo, l), :] = s8
    s16 = s8 + pad_ref[pl.ds(lo - 8, l), :]
    win = _pool_lane_window((1, POOL_W))
    cnt = _pool_counts(l, win)
    return _pool_select(win, s2, s4, s8, s16) / cnt - u, cnt, win


def pool_fwd(proj, w_blk, pool_scale, nb, l):
    def body(u_ref, w_ref, ps_ref, y_ref, pad_ref):
        pad_ref[pl.ds(0, POOL_PAD), :] = jnp.zeros((POOL_PAD, POOL_W), F32)
        d, _, _ = _pool_diff(u_ref[...], pad_ref, l)
        y_ref[...] = _dot(d.astype(BF16), w_ref[...]) * ps_ref[...]

    return pl.pallas_call(
        body, grid=(nb,),
        in_specs=[pl.BlockSpec((l, POOL_W), lambda b: (b, 0)),
                  pl.BlockSpec((POOL_W, POOL_W), lambda b: (0, 0)),
                  pl.BlockSpec((1, POOL_W), lambda b: (0, 0))],
        out_specs=pl.BlockSpec((l, POOL_W), lambda b: (b, 0)),
        out_shape=jax.ShapeDtypeStruct((nb * l, POOL_W), F32),
        scratch_shapes=[pltpu.VMEM((POOL_PAD + l, POOL_W), F32)],
        compiler_params=_params("arbitrary"), name="pool_fwd")(proj, w_blk, pool_scale)


def pool_bwd(proj, d_ycat, w_blk, pool_scale, nb, l):
    def body(u_ref, dy_ref, w_ref, ps_ref, du_ref, dw_ref, dps_ref, pad_ref, padb_ref):
        b = pl.program_id(0)

        @pl.when(b == 0)
        def _():
            dw_ref[...] = jnp.zeros_like(dw_ref)
            dps_ref[...] = jnp.zeros_like(dps_ref)

        pad_ref[pl.ds(0, POOL_PAD), :] = jnp.zeros((POOL_PAD, POOL_W), F32)
        padb_ref[pl.ds(l, POOL_PAD), :] = jnp.zeros((POOL_PAD, POOL_W), F32)
        d, cnt, win = _pool_diff(u_ref[...], pad_ref, l)
        db = d.astype(BF16)
        w = w_ref[...]
        dy = dy_ref[...]
        dps_ref[...] += jnp.sum(dy * _dot(db, w), axis=0, keepdims=True)
        dyo = (dy * ps_ref[...]).astype(BF16)
        dw_ref[...] += _dot_tn(db, dyo)
        dd = _dot_nt(dyo, w)
        e = dd / cnt
        padb_ref[pl.ds(0, l), :] = e
        f2 = e + padb_ref[pl.ds(1, l), :]
        padb_ref[pl.ds(0, l), :] = f2
        f4 = f2 + padb_ref[pl.ds(2, l), :]
        padb_ref[pl.ds(0, l), :] = f4
        f8 = f4 + padb_ref[pl.ds(4, l), :]
        padb_ref[pl.ds(0, l), :] = f8
        f16 = f8 + padb_ref[pl.ds(8, l), :]
        du_ref[...] = _pool_select(win, f2, f4, f8, f16) - dd

    return pl.pallas_call(
        body, grid=(nb,),
        in_specs=[pl.BlockSpec((l, POOL_W), lambda b: (b, 0)),
                  pl.BlockSpec((l, POOL_W), lambda b: (b, 0)),
                  pl.BlockSpec((POOL_W, POOL_W), lambda b: (0, 0)),
                  pl.BlockSpec((1, POOL_W), lambda b: (0, 0))],
        out_specs=[pl.BlockSpec((l, POOL_W), lambda b: (b, 0)),
                   pl.BlockSpec((POOL_W, POOL_W), lambda b: (0, 0)),
                   pl.BlockSpec((1, POOL_W), lambda b: (0, 0))],
        out_shape=[jax.ShapeDtypeStruct((nb * l, POOL_W), F32),
                   jax.ShapeDtypeStruct((POOL_W, POOL_W), F32),
                   jax.ShapeDtypeStruct((1, POOL_W), F32)],
        scratch_shapes=[pltpu.VMEM((POOL_PAD + l, POOL_W), F32), pltpu.VMEM((POOL_PAD + l, POOL_W), F32)],
        compiler_params=_params("arbitrary"), name="pool_bwd")(proj, d_ycat, w_blk, pool_scale)


def _discretise(a_re, a_im, ldt, b_re, b_im):
    dt = jnp.exp(ldt)
    mag = jnp.exp(a_re * dt)
    th = a_im * dt
    lr = mag * jnp.cos(th)
    li = mag * jnp.sin(th)
    nr = lr - 1.0
    den = a_re * a_re + a_im * a_im
    fr = (nr * a_re + li * a_im) / den
    fi = (li * a_re - nr * a_im) / den
    return lr, li, fr * b_re - fi * b_im, fr * b_im + fi * b_re


def _cmul(ar, ai, br, bi):
    return ar * br - ai * bi, ar * bi + ai * br


def s5_scan_consts(a_re_row, a_im_row, ldt_row):
    def body(ar_ref, ai_ref, ld_ref, f_ref, b_ref):
        dt = jnp.exp(ld_ref[...])
        mag = jnp.exp(ar_ref[...] * dt)
        th = ai_ref[...] * dt
        lr = mag * jnp.cos(th)
        li = mag * jnp.sin(th)
        r = lax.broadcasted_iota(jnp.int32, (SUBLANES, NST), 0)
        for out_ref, sign, rev in ((f_ref, 1.0, False), (b_ref, -1.0, True)):
            p = [(lr, sign * li)]
            for _ in range(SUBLANES - 1):
                p.append(_cmul(p[-1][0], p[-1][1], lr, sign * li))
            for idx, s in enumerate((1, 2, 4)):
                inside = (r < SUBLANES - s) if rev else (r >= s)
                out_ref[pl.ds(2 * idx * SUBLANES, SUBLANES), :] = jnp.where(inside, p[s - 1][0], 0.0)
                out_ref[pl.ds((2 * idx + 1) * SUBLANES, SUBLANES), :] = jnp.where(inside, p[s - 1][1], 0.0)
            pr = jnp.zeros((SUBLANES, NST), F32)
            pi = jnp.zeros((SUBLANES, NST), F32)
            for row in range(SUBLANES):
                power = (SUBLANES - row) if rev else (row + 1)
                pr = jnp.where(r == row, p[power - 1][0], pr)
                pi = jnp.where(r == row, p[power - 1][1], pi)
            out_ref[pl.ds(6 * SUBLANES, SUBLANES), :] = pr
            out_ref[pl.ds(7 * SUBLANES, SUBLANES), :] = pi

    shape = jax.ShapeDtypeStruct((8 * SUBLANES, NST), F32)
    return pl.pallas_call(body, out_shape=[shape, shape], name="s5_scan_consts")(a_re_row, a_im_row, ldt_row)


def s5_bbar(a_re_c, a_im_c, ldt_c, b_re2, b_im2):
    def body(ar, ai, ld, br, bi, o_re, o_im):
        _, _, bbr, bbi = _discretise(ar[...], ai[...], ld[...], br[...], bi[...])
        o_re[...] = bbr
        o_im[...] = bbi

    shape = jax.ShapeDtypeStruct((NST, SSM_GC), F32)
    return pl.pallas_call(body, out_shape=[shape, shape], name="s5_bbar")(a_re_c, a_im_c, ldt_c, b_re2, b_im2)


def s5_param_bwd(a_re_c, a_im_c, ldt_c, b_re2, b_im2, d_lr, d_li, d_bbr, d_bbi):
    def body(ar, ai, ld, br, bi, g_lr, g_li, g_br, g_bi, o_ar, o_ai, o_ld, o_br, o_bi):
        _, vjp = jax.vjp(_discretise, ar[...], ai[...], ld[...], br[...], bi[...])
        d_ar, d_ai, d_ld, d_br, d_bi = vjp((g_lr[...], g_li[...], g_br[...], g_bi[...]))
        o_ar[...] = d_ar
        o_ai[...] = d_ai
        o_ld[...] = jnp.sum(d_ld.reshape(SSM_NG, SSM_N, 1), axis=1)
        o_br[...] = d_br
        o_bi[...] = d_bi

    col = jax.ShapeDtypeStruct((NST, 1), F32)
    mat = jax.ShapeDtypeStruct((NST, SSM_GC), F32)
    grp = jax.ShapeDtypeStruct((SSM_NG, 1), F32)
    return pl.pallas_call(body, out_shape=[col, col, grp, mat, mat], name="s5_param_bwd")(
        a_re_c, a_im_c, ldt_c, b_re2, b_im2, d_lr, d_li, d_bbr, d_bbi)


def _scan_tiles(buf_ref, cst_ref, carry_ref, rows, reverse, h_ref=None, acc_ref=None):
    n_tiles = rows // SUBLANES
    shifts = ((1, 0), (2, 2), (4, 4))
    row_id = lax.broadcasted_iota(jnp.int32, (SUBLANES, SCAN_COLS), 0)
    for j in range(NST // SCAN_COLS):
        c_re = pl.ds(j * SCAN_COLS, SCAN_COLS)
        c_im = pl.ds(NST + j * SCAN_COLS, SCAN_COLS)

        def step(i, carry, c_re=c_re, c_im=c_im):
            tile = (n_tiles - 1 - i) if reverse else i
            rws = pl.ds(pl.multiple_of(tile * SUBLANES, SUBLANES), SUBLANES)
            xr = buf_ref[rws, c_re]
            xi = buf_ref[rws, c_im]
            for s, base in shifts:
                amount = (SUBLANES - s) if reverse else s
                sr = pltpu.roll(xr, amount, 0)
                si = pltpu.roll(xi, amount, 0)
                mr = cst_ref[pl.ds(base * SUBLANES, SUBLANES), c_re]
                mi = cst_ref[pl.ds((base + 1) * SUBLANES, SUBLANES), c_re]
                xr, xi = xr + (mr * sr - mi * si), xi + (mr * si + mi * sr)
            pr = cst_ref[pl.ds(6 * SUBLANES, SUBLANES), c_re]
            pi = cst_ref[pl.ds(7 * SUBLANES, SUBLANES), c_re]
            cr, ci = carry[0], carry[1]
            xr, xi = xr + (pr * cr - pi * ci), xi + (pr * ci + pi * cr)
            buf_ref[rws, c_re] = xr
            buf_ref[rws, c_im] = xi
            edge = 0 if reverse else SUBLANES - 1
            ncr = jnp.broadcast_to(xr[edge:edge + 1, :], (SUBLANES, SCAN_COLS))
            nci = jnp.broadcast_to(xi[edge:edge + 1, :], (SUBLANES, SCAN_COLS))
            if not reverse:
                return ncr, nci
            gnr = jnp.where(row_id == SUBLANES - 1, cr, pltpu.roll(xr, SUBLANES - 1, 0))
            gni = jnp.where(row_id == SUBLANES - 1, ci, pltpu.roll(xi, SUBLANES - 1, 0))
            hr = h_ref[rws, c_re]
            hi = h_ref[rws, c_im]
            return ncr, nci, carry[2] + (gnr * hr + gni * hi), carry[3] + (gni * hr - gnr * hi)

        init = (carry_ref[:, c_re], carry_ref[:, c_im])
        if reverse:
            init = init + (acc_ref[:, c_re], acc_ref[:, c_im])
        out = lax.fori_loop(0, n_tiles, step, init)
        carry_ref[:, c_re] = out[0]
        carry_ref[:, c_im] = out[1]
        if reverse:
            acc_ref[:, c_re] = out[2]
            acc_ref[:, c_im] = out[3]


def s5_fwd(proj, b_blk, c_blk, cst, d_skip, w_glu, nb, l):
    tt = min(256, l)
    nt = l // tt

    def body(u_ref, b_ref, c_ref, cst_ref, ds_ref, wg_ref, y_ref, h_ref, yp_ref, z_ref, buf_ref, carry_ref):
        @pl.when(pl.program_id(1) == 0)
        def _():
            carry_ref[...] = jnp.zeros_like(carry_ref)

        u = u_ref[...]
        buf_ref[...] = _dot(u.astype(BF16), b_ref[...])
        _scan_tiles(buf_ref, cst_ref, carry_ref, tt, reverse=False)
        h = buf_ref[...].astype(BF16)
        h_ref[...] = h
        ypre = _dot(h, c_ref[...]) + ds_ref[...] * u
        yp_ref[...] = ypre
        z = _dot(jax.nn.gelu(ypre).astype(BF16), wg_ref[...])
        z_ref[...] = z
        y_ref[...] = z[:, :SSM_W] * jax.nn.sigmoid(z[:, SSM_W:])

    row = lambda b, t: (b * nt + t, 0)
    const = lambda b, t: (0, 0)
    return pl.pallas_call(
        body, grid=(nb, nt),
        in_specs=[pl.BlockSpec((tt, SSM_W), lambda b, t: (b * nt + t, 1)),
                  pl.BlockSpec((SSM_W, 2 * NST), const),
                  pl.BlockSpec((2 * NST, SSM_W), const),
                  pl.BlockSpec((8 * SUBLANES, NST), const),
                  pl.BlockSpec((1, SSM_W), const),
                  pl.BlockSpec((SSM_W, 2 * SSM_W), const)],
        out_specs=[pl.BlockSpec((tt, SSM_W), row),
                   pl.BlockSpec((tt, 2 * NST), row),
                   pl.BlockSpec((tt, SSM_W), row),
                   pl.BlockSpec((tt, 2 * SSM_W), row)],
        out_shape=[jax.ShapeDtypeStruct((nb * l, SSM_W), F32),
                   jax.ShapeDtypeStruct((nb * l, 2 * NST), BF16),
                   jax.ShapeDtypeStruct((nb * l, SSM_W), F32),
                   jax.ShapeDtypeStruct((nb * l, 2 * SSM_W), F32)],
        scratch_shapes=[pltpu.VMEM((tt, 2 * NST), F32), pltpu.VMEM((SUBLANES, 2 * NST), F32)],
        compiler_params=_params("arbitrary", "arbitrary"), name="s5_fwd")(proj, b_blk, c_blk, cst, d_skip, w_glu)


def s5_bwd(proj, d_ycat, h, ypre, z, b_blk, c_blk, cst_rev, d_skip, w_glu, nb, l):
    tt = min(256, l)
    nt = l // tt

    def body(u_ref, dy_ref, h_ref, yp_ref, z_ref, b_ref, c_ref, cst_ref, ds_ref, wg_ref,
             du_ref, dwg_ref, dds_ref, dc_ref, db_ref, dlam_ref, buf_ref, hbuf_ref, carry_ref):
        b = pl.program_id(0)
        t = pl.program_id(1)

        @pl.when((b == 0) & (t == 0))
        def _():
            dwg_ref[...] = jnp.zeros_like(dwg_ref)
            dds_ref[...] = jnp.zeros_like(dds_ref)
            dc_ref[...] = jnp.zeros_like(dc_ref)
            db_ref[...] = jnp.zeros_like(db_ref)
            dlam_ref[...] = jnp.zeros_like(dlam_ref)

        @pl.when(t == 0)
        def _():
            carry_ref[...] = jnp.zeros_like(carry_ref)

        u = u_ref[...]
        ypre = yp_ref[...]
        z = z_ref[...]
        z1 = z[:, :SSM_W]
        sg = jax.nn.sigmoid(z[:, SSM_W:])
        dy = dy_ref[...]
        dz = jnp.concatenate([dy * sg, dy * z1 * sg * (1.0 - sg)], axis=1).astype(BF16)
        yg, gelu_vjp = jax.vjp(jax.nn.gelu, ypre)
        dwg_ref[...] += _dot_tn(yg.astype(BF16), dz)
        dypre = gelu_vjp(_dot_nt(dz, wg_ref[...]))[0]
        dds_ref[...] += jnp.sum(dypre * u, axis=0, keepdims=True)
        dyb = dypre.astype(BF16)
        hb = h_ref[...]
        dc_ref[...] += _dot_tn(hb, dyb)
        hbuf_ref[...] = hb.astype(F32)
        buf_ref[...] = _dot_nt(dyb, c_ref[...])
        _scan_tiles(buf_ref, cst_ref, carry_ref, tt, reverse=True, h_ref=hbuf_ref, acc_ref=dlam_ref)
        g = buf_ref[...].astype(BF16)
        db_ref[...] += _dot_tn(g, u.astype(BF16))
        du_ref[...] = _dot_nt(g, b_ref[...]) + ds_ref[...] * dypre

        @pl.when((b == nb - 1) & (t == nt - 1))
        def _():
            dlam_ref[...] = jnp.broadcast_to(jnp.sum(dlam_ref[...], axis=0, keepdims=True), dlam_ref.shape)

    def rrow(col):
        return lambda b, t: (b * nt + (nt - 1 - t), col)

    const = lambda b, t: (0, 0)
    return pl.pallas_call(
        body, grid=(nb, nt),
        in_specs=[pl.BlockSpec((tt, SSM_W), rrow(1)),
                  pl.BlockSpec((tt, SSM_W), rrow(1)),
                  pl.BlockSpec((tt, 2 * NST), rrow(0)),
                  pl.BlockSpec((tt, SSM_W), rrow(0)),
                  pl.BlockSpec((tt, 2 * SSM_W), rrow(0)),
                  pl.BlockSpec((SSM_W, 2 * NST), const),
                  pl.BlockSpec((2 * NST, SSM_W), const),
                  pl.BlockSpec((8 * SUBLANES, NST), const),
                  pl.BlockSpec((1, SSM_W), const),
                  pl.BlockSpec((SSM_W, 2 * SSM_W), const)],
        out_specs=[pl.BlockSpec((tt, SSM_W), rrow(0)),
                   pl.BlockSpec((SSM_W, 2 * SSM_W), const),
                   pl.BlockSpec((1, SSM_W), const),
                   pl.BlockSpec((2 * NST, SSM_W), const),
                   pl.BlockSpec((2 * NST, SSM_W), const),
                   pl.BlockSpec((SUBLANES, 2 * NST), const)],
        out_shape=[jax.ShapeDtypeStruct((nb * l, SSM_W), F32),
                   jax.ShapeDtypeStruct((SSM_W, 2 * SSM_W), F32),
                   jax.ShapeDtypeStruct((1, SSM_W), F32),
                   jax.ShapeDtypeStruct((2 * NST, SSM_W), F32),
                   jax.ShapeDtypeStruct((2 * NST, SSM_W), F32),
                   jax.ShapeDtypeStruct((SUBLANES, 2 * NST), F32)],
        scratch_shapes=[pltpu.VMEM((tt, 2 * NST), F32), pltpu.VMEM((tt, 2 * NST), F32),
                        pltpu.VMEM((SUBLANES, 2 * NST), F32)],
        compiler_params=_params("arbitrary", "arbitrary"), name="s5_bwd")(
            proj, d_ycat, h, ypre, z, b_blk, c_blk, cst_rev, d_skip, w_glu)


def _head_mask(hh):
    lane = lax.broadcasted_iota(jnp.int32, (1, ATT_W), 1)
    return (lane >= hh * MEM_HD) & (lane < (hh + 1) * MEM_HD)


def _mem_kv(mem_ref, gm_ref, wkv_ref):
    m = mem_ref[0]
    mh = m * _rms_scale(m)
    mb = (mh * gm_ref[...]).astype(BF16)
    kv = _dot(mb, wkv_ref[...])
    return mh, mb, kv[:, :ATT_W].astype(BF16), kv[:, ATT_W:].astype(BF16)


def _softmax_rows(s):
    e = jnp.exp(s - jnp.max(s, axis=-1, keepdims=True))
    return e / jnp.sum(e, axis=-1, keepdims=True)


def attn_fwd(proj, mem, g_mem, w_kv, nb, l):
    tq = min(512, l)
    nq = l // tq
    scale = MEM_HD ** -0.5

    def body(q_ref, mem_ref, gm_ref, wkv_ref, o_ref, k_s, v_s):
        @pl.when(pl.program_id(1) == 0)
        def _():
            _, _, k, v = _mem_kv(mem_ref, gm_ref, wkv_ref)
            k_s[...] = k
            v_s[...] = v

        q = q_ref[...].astype(BF16)
        k = k_s[...]
        v = v_s[...]
        o = jnp.zeros((tq, ATT_W), F32)
        for hh in range(MEM_HEADS):
            msk = _head_mask(hh)
            p = _softmax_rows(_dot_nt(q, jnp.where(msk, k, 0)) * scale)
            o = o + _dot(p.astype(BF16), jnp.where(msk, v, 0))
        o_ref[...] = o

    const = lambda b, t: (0, 0)
    return pl.pallas_call(
        body, grid=(nb, nq),
        in_specs=[pl.BlockSpec((tq, ATT_W), lambda b, t: (b * nq + t, 3)),
                  pl.BlockSpec((1, N_MEM, D_MODEL), lambda b, t: (b, 0, 0)),
                  pl.BlockSpec((1, D_MODEL), const),
                  pl.BlockSpec((D_MODEL, 2 * ATT_W), const)],
        out_specs=pl.BlockSpec((tq, ATT_W), lambda b, t: (b * nq + t, 0)),
        out_shape=jax.ShapeDtypeStruct((nb * l, ATT_W), F32),
        scratch_shapes=[pltpu.VMEM((N_MEM, ATT_W), BF16), pltpu.VMEM((N_MEM, ATT_W), BF16)],
        compiler_params=_params("arbitrary", "arbitrary"), name="attn_fwd")(proj, mem, g_mem, w_kv)


def attn_bwd(proj, d_ycat, mem, g_mem, w_kv, nb, l):
    tq = min(512, l)
    nq = l // tq
    scale = MEM_HD ** -0.5

    def body(q_ref, do_ref, mem_ref, gm_ref, wkv_ref, dq_ref, dwkv_ref, dgm_ref, k_s, v_s, mb_s, dk_s, dv_s):
        b = pl.program_id(0)
        t = pl.program_id(1)

        @pl.when((b == 0) & (t == 0))
        def _():
            dwkv_ref[...] = jnp.zeros_like(dwkv_ref)
            dgm_ref[...] = jnp.zeros_like(dgm_ref)

        @pl.when(t == 0)
        def _():
            _, mb, k, v = _mem_kv(mem_ref, gm_ref, wkv_ref)
            k_s[...] = k
            v_s[...] = v
            mb_s[...] = mb
            dk_s[...] = jnp.zeros_like(dk_s)
            dv_s[...] = jnp.zeros_like(dv_s)

        q = q_ref[...].astype(BF16)
        do = do_ref[...].astype(BF16)
        k = k_s[...]
        v = v_s[...]
        dq = jnp.zeros((tq, ATT_W), F32)
        dk = jnp.zeros((N_MEM, ATT_W), F32)
        dv = jnp.zeros((N_MEM, ATT_W), F32)
        for hh in range(MEM_HEADS):
            msk = _head_mask(hh)
            kh = jnp.where(msk, k, 0)
            p = _softmax_rows(_dot_nt(q, kh) * scale)
            dp = _dot_nt(do, jnp.where(msk, v, 0))
            ds = (p * (dp - jnp.sum(dp * p, axis=-1, keepdims=True)) * scale).astype(BF16)
            dq = dq + _dot(ds, kh)
            dk = dk + jnp.where(msk, _dot_tn(ds, q), 0.0)
            dv = dv + jnp.where(msk, _dot_tn(p.astype(BF16), do), 0.0)
        dq_ref[...] = dq
        dk_s[...] += dk
        dv_s[...] += dv

        @pl.when(t == nq - 1)
        def _():
            dkv = jnp.concatenate([dk_s[...], dv_s[...]], axis=1).astype(BF16)
            dwkv_ref[...] += _dot_tn(mb_s[...], dkv)
            m = mem_ref[0]
            dgm_ref[...] += jnp.sum(_dot_nt(dkv, wkv_ref[...]) * (m * _rms_scale(m)), axis=0, keepdims=True)

    const = lambda b, t: (0, 0)
    return pl.pallas_call(
        body, grid=(nb, nq),
        in_specs=[pl.BlockSpec((tq, ATT_W), lambda b, t: (b * nq + t, 3)),
                  pl.BlockSpec((tq, ATT_W), lambda b, t: (b * nq + t, 3)),
                  pl.BlockSpec((1, N_MEM, D_MODEL), lambda b, t: (b, 0, 0)),
                  pl.BlockSpec((1, D_MODEL), const),
                  pl.BlockSpec((D_MODEL, 2 * ATT_W), const)],
        out_specs=[pl.BlockSpec((tq, ATT_W), lambda b, t: (b * nq + t, 0)),
                   pl.BlockSpec((D_MODEL, 2 * ATT_W), const),
                   pl.BlockSpec((1, D_MODEL), const)],
        out_shape=[jax.ShapeDtypeStruct((nb * l, ATT_W), F32),
                   jax.ShapeDtypeStruct((D_MODEL, 2 * ATT_W), F32),
                   jax.ShapeDtypeStruct((1, D_MODEL), F32)],
        scratch_shapes=[pltpu.VMEM((N_MEM, ATT_W), BF16), pltpu.VMEM((N_MEM, ATT_W), BF16),
                        pltpu.VMEM((N_MEM, D_MODEL), BF16),
                        pltpu.VMEM((N_MEM, ATT_W), F32), pltpu.VMEM((N_MEM, ATT_W), F32)],
        compiler_params=_params("arbitrary", "arbitrary"), name="attn_bwd")(proj, d_ycat, mem, g_mem, w_kv)


def tail(y_pool, y_ssm, y_att, proj, x2, target, w_out, g_post):
    t = x2.shape[0]
    tm = min(256, t)

    def body(yp_ref, ys_ref, ya_ref, gate_ref, x_ref, tg_ref, w_ref, gp_ref,
             dz_ref, dyc_ref, dgate_ref, dw_ref, dgp_ref, loss_ref):
        @pl.when(pl.program_id(0) == 0)
        def _():
            dw_ref[...] = jnp.zeros_like(dw_ref)
            dgp_ref[...] = jnp.zeros_like(dgp_ref)
            loss_ref[...] = jnp.zeros_like(loss_ref)

        ycat = jnp.concatenate([yp_ref[...], ys_ref[...], ya_ref[...]], axis=1)
        gate = gate_ref[...]
        sg = jax.nn.sigmoid(gate)
        silu = gate * sg
        yb = (ycat * silu).astype(BF16)
        w = w_ref[...]
        out = _dot(yb, w)
        r2 = _rms_scale(out)
        oh = out * r2
        gp = gp_ref[...]
        err = (x_ref[...] + oh * gp) - tg_ref[...]
        loss_ref[...] += 0.5 * jnp.sum(jnp.mean(err * err, axis=-1, keepdims=True), axis=0, keepdims=True)
        dz = err * (1.0 / D_MODEL)
        dz_ref[...] = dz
        dgp_ref[...] += jnp.sum(dz * oh, axis=0, keepdims=True)
        dn = dz * gp
        dout = (r2 * (dn - oh * jnp.mean(dn * oh, axis=-1, keepdims=True))).astype(BF16)
        dw_ref[...] += _dot_tn(yb, dout)
        dy = _dot_nt(dout, w)
        dyc_ref[...] = dy * silu
        dgate_ref[...] = dy * ycat * (sg * (1.0 + gate * (1.0 - sg)))

    row = lambda i: (i, 0)
    const = lambda i: (0, 0)
    full = jax.ShapeDtypeStruct((t, D_MODEL), F32)
    return pl.pallas_call(
        body, grid=(t // tm,),
        in_specs=[pl.BlockSpec((tm, POOL_W), row),
                  pl.BlockSpec((tm, SSM_W), row),
                  pl.BlockSpec((tm, ATT_W), row),
                  pl.BlockSpec((tm, D_MODEL), lambda i: (i, 1)),
                  pl.BlockSpec((tm, D_MODEL), row),
                  pl.BlockSpec((tm, D_MODEL), row),
                  pl.BlockSpec((D_MODEL, D_MODEL), const),
                  pl.BlockSpec((1, D_MODEL), const)],
        out_specs=[pl.BlockSpec((tm, D_MODEL), row),
                   pl.BlockSpec((tm, D_MODEL), row),
                   pl.BlockSpec((tm, D_MODEL), row),
                   pl.BlockSpec((D_MODEL, D_MODEL), const),
                   pl.BlockSpec((1, D_MODEL), const),
                   pl.BlockSpec((1, LANES), const)],
        out_shape=[full, full, full,
                   jax.ShapeDtypeStruct((D_MODEL, D_MODEL), F32),
                   jax.ShapeDtypeStruct((1, D_MODEL), F32),
                   jax.ShapeDtypeStruct((1, LANES), F32)],
        compiler_params=_params("arbitrary"), name="tail")(y_pool, y_ssm, y_att, proj, x2, target, w_out, g_post)


def in_proj_bwd(du_pool, du_ssm, dq, d_gate, x2, dz, g_pre, w_in):
    t = x2.shape[0]
    tm = min(256, t)
    shard = 2 * D_MODEL // 4

    def body(dup_ref, dus_ref, dq_ref, dg_ref, x_ref, dz_ref, g_ref, w_ref, gx_ref, dw_ref, dgp_ref):
        @pl.when(pl.program_id(0) == 0)
        def _():
            dw_ref[...] = jnp.zeros_like(dw_ref)
            dgp_ref[...] = jnp.zeros_like(dgp_ref)

        dproj = jnp.concatenate([dup_ref[...], dus_ref[...], dq_ref[...], dg_ref[...]], axis=1).astype(BF16)
        x = x_ref[...]
        r1 = _rms_scale(x)
        xh = x * r1
        g = g_ref[...]
        hb = (xh * g).astype(BF16)
        for k in range(4):
            dw_ref[k] += _dot_tn(hb, dproj[:, k * shard:(k + 1) * shard])
        dh = _dot_nt(dproj, w_ref[...])
        dgp_ref[...] += jnp.sum(dh * xh, axis=0, keepdims=True)
        dn = dh * g
        gx_ref[...] = dz_ref[...] + r1 * (dn - xh * jnp.mean(dn * xh, axis=-1, keepdims=True))

    row = lambda i: (i, 0)
    const = lambda i: (0, 0)
    return pl.pallas_call(
        body, grid=(t // tm,),
        in_specs=[pl.BlockSpec((tm, POOL_W), row),
                  pl.BlockSpec((tm, SSM_W), row),
                  pl.BlockSpec((tm, ATT_W), row),
                  pl.BlockSpec((tm, D_MODEL), row),
                  pl.BlockSpec((tm, D_MODEL), row),
                  pl.BlockSpec((tm, D_MODEL), row),
                  pl.BlockSpec((1, D_MODEL), const),
                  pl.BlockSpec((D_MODEL, 2 * D_MODEL), const)],
        out_specs=[pl.BlockSpec((tm, D_MODEL), row),
                   pl.BlockSpec((4, D_MODEL, shard), lambda i: (0, 0, 0)),
                   pl.BlockSpec((1, D_MODEL), const)],
        out_shape=[jax.ShapeDtypeStruct((t, D_MODEL), F32),
                   jax.ShapeDtypeStruct((4, D_MODEL, shard), F32),
                   jax.ShapeDtypeStruct((1, D_MODEL), F32)],
        compiler_params=_params("arbitrary"), name="in_proj_bwd")(du_pool, du_ssm, dq, d_gate, x2, dz, g_pre, w_in)


def _block_diag(blocks):
    g, r, c = blocks.shape
    eye = jnp.eye(g, dtype=blocks.dtype)
    return jnp.einsum("grc,gh->grhc", blocks, eye).reshape(g * r, g * c)


def _block_diag_extract(mat, g, r, c):
    eye = jnp.eye(g, dtype=mat.dtype)
    return jnp.einsum("grhc,gh->grc", mat.reshape(g, r, g, c), eye)


def local_step(x, mem, target, g_pre, w_in, w_pool, pool_scale, a_re, a_im, log_dt, b_re, b_im, c_re, c_im, d_skip,
               w_glu, g_mem, w_kv, w_out, g_post):
    nb, l, _ = x.shape
    x2 = x.reshape(nb * l, D_MODEL)
    tg2 = target.reshape(nb * l, D_MODEL)

    rowv = lambda a: a.reshape(1, NST)
    colv = lambda a: a.reshape(NST, 1)
    ldt_full = jnp.broadcast_to(log_dt.reshape(SSM_NG, 1), (SSM_NG, SSM_N))
    b_re2 = b_re.reshape(NST, SSM_GC)
    b_im2 = b_im.reshape(NST, SSM_GC)
    cst_f, cst_b = s5_scan_consts(rowv(a_re), rowv(a_im), rowv(ldt_full))
    bbr, bbi = s5_bbar(colv(a_re), colv(a_im), colv(ldt_full), b_re2, b_im2)
    to_b_blk = lambda bb: _block_diag(bb.reshape(SSM_NG, SSM_N, SSM_GC).transpose(0, 2, 1))
    b_blk = jnp.concatenate([to_b_blk(bbr), to_b_blk(bbi)], axis=1).astype(BF16)
    to_c_blk = lambda cc: _block_diag(cc.reshape(SSM_NG, SSM_GC, SSM_N).transpose(0, 2, 1))
    c_blk = jnp.concatenate([to_c_blk(c_re), -to_c_blk(c_im)], axis=0).astype(BF16)
    w_pool_blk = _block_diag(w_pool.reshape(4, POOL_GW, POOL_GW)).astype(BF16)

    proj = in_proj(x2, g_pre, w_in)
    y_pool = pool_fwd(proj, w_pool_blk, pool_scale, nb, l)
    y_ssm, h, ypre, z = s5_fwd(proj, b_blk, c_blk, cst_f, d_skip, w_glu, nb, l)
    y_att = attn_fwd(proj, mem, g_mem, w_kv, nb, l)
    dz, d_ycat, d_gate, d_w_out, d_g_post, loss = tail(y_pool, y_ssm, y_att, proj, x2, tg2, w_out, g_post)
    du_pool, d_wp_blk, d_pool_scale = pool_bwd(proj, d_ycat, w_pool_blk, pool_scale, nb, l)
    du_ssm, d_w_glu, d_d_skip, d_c_full, d_b_full, d_lam = s5_bwd(
        proj, d_ycat, h, ypre, z, b_blk, c_blk, cst_b, d_skip, w_glu, nb, l)
    dq, d_w_kv, d_g_mem = attn_bwd(proj, d_ycat, mem, g_mem, w_kv, nb, l)
    grad_x, d_w_in, d_g_pre = in_proj_bwd(du_pool, du_ssm, dq, d_gate, x2, dz, g_pre, w_in)

    d_c = _block_diag_extract(d_c_full.reshape(2 * SSM_NG * SSM_N, SSM_W).reshape(2, NST, SSM_W)[0], SSM_NG, SSM_N, SSM_GC)
    d_ci = _block_diag_extract(d_c_full.reshape(2, NST, SSM_W)[1], SSM_NG, SSM_N, SSM_GC)
    d_bb = d_b_full.reshape(2, NST, SSM_W)
    d_bbr = _block_diag_extract(d_bb[0], SSM_NG, SSM_N, SSM_GC).reshape(NST, SSM_GC)
    d_bbi = _block_diag_extract(d_bb[1], SSM_NG, SSM_N, SSM_GC).reshape(NST, SSM_GC)
    d_lam_row = d_lam[0]
    d_ar, d_ai, d_ld, d_br, d_bi = s5_param_bwd(
        colv(a_re), colv(a_im), colv(ldt_full), b_re2, b_im2,
        d_lam_row[:NST].reshape(NST, 1), d_lam_row[NST:].reshape(NST, 1), d_bbr, d_bbi)

    grads = {
        "g_pre": d_g_pre,
        "w_in": d_w_in,
        "w_pool": _block_diag_extract(d_wp_blk, 4, POOL_GW, POOL_GW).reshape(1, 4, POOL_GW, POOL_GW),
        "pool_scale": d_pool_scale,
        "a_re": d_ar.reshape(1, SSM_NG, SSM_N),
        "a_im": d_ai.reshape(1, SSM_NG, SSM_N),
        "log_dt": d_ld.reshape(1, SSM_NG),
        "b_re": d_br.reshape(1, SSM_NG, SSM_N, SSM_GC),
        "b_im": d_bi.reshape(1, SSM_NG, SSM_N, SSM_GC),
        "c_re": d_c.transpose(0, 2, 1).reshape(1, SSM_NG, SSM_GC, SSM_N),
        "c_im": (-d_ci).transpose(0, 2, 1).reshape(1, SSM_NG, SSM_GC, SSM_N),
        "d_skip": d_d_skip,
        "w_glu": d_w_glu,
        "g_mem": d_g_mem,
        "w_kv": d_w_kv,
        "w_out": d_w_out,
        "g_post": d_g_post,
    }
    return loss, grad_x.reshape(nb, l, D_MODEL), grads


VMEM_SPEC = pl.BlockSpec(memory_space=pltpu.VMEM)
ANY_SPEC = pl.BlockSpec(memory_space=pl.ANY)
N_CHIPS = 4
N_DEV = 8


def _place():
    return lax.axis_index("x"), lax.axis_index("y"), lax.axis_index("c")


def _other_chips(x, y):
    return [(1 - x, y), (x, 1 - y), (1 - x, 1 - y)]


def gather_weights(shards):
    n = len(shards)

    def body(*refs):
        ins, outs = refs[:n], refs[n:2 * n]
        send_sems, recv_sems = refs[2 * n:]
        x, y, c = _place()
        k = 2 * x + y
        for i in range(n):
            outs[i][k] = ins[i][...].astype(BF16)
        chips = _other_chips(x, y)

        def copy(i, j, block):
            px, py = chips[j]
            return pltpu.make_async_remote_copy(
                src_ref=outs[i].at[block], dst_ref=outs[i].at[block],
                send_sem=send_sems.at[i * 3 + j], recv_sem=recv_sems.at[i * 3 + j],
                device_id=(px, py, c), device_id_type=MESH)

        sends = [copy(i, j, k) for i in range(n) for j in range(3)]
        for cp in sends:
            cp.start()
        for i in range(n):
            for j in range(3):
                px, py = chips[j]
                copy(i, j, 2 * px + py).wait_recv()
        for cp in sends:
            cp.wait_send()

    return pl.pallas_call(
        body,
        out_shape=[jax.ShapeDtypeStruct((N_CHIPS,) + s.shape, BF16) for s in shards],
        in_specs=[VMEM_SPEC] * n, out_specs=[VMEM_SPEC] * n,
        scratch_shapes=[pltpu.SemaphoreType.DMA((3 * n,)), pltpu.SemaphoreType.DMA((3 * n,))],
        compiler_params=pltpu.CompilerParams(vmem_limit_bytes=VMEM_LIMIT),
        name="gather_weights")(*shards)


def reduce_sharded(g4, w, m, v, name):
    _, r, cdim = g4.shape

    def body(g_ref, w_ref, m_ref, v_ref, og, od, om, ov, own, recv, part, sib, lsem, send_sems, recv_sems):
        x, y, c = _place()
        chips = _other_chips(x, y)
        own_cp = pltpu.make_async_copy(g_ref.at[2 * x + y], own, lsem)
        own_cp.start()
        sends = []
        for j, (px, py) in enumerate(chips):
            cp = pltpu.make_async_remote_copy(
                src_ref=g_ref.at[2 * px + py], dst_ref=recv.at[j], send_sem=send_sems.at[j], recv_sem=recv_sems.at[j],
                device_id=(px, py, c), device_id_type=MESH)
            cp.start()
            sends.append(cp)
        own_cp.wait()
        for cp in sends:
            cp.wait_recv()
        part[...] = ((own[...] + recv[0]) + recv[1]) + recv[2]
        swap = pltpu.make_async_remote_copy(
            src_ref=part, dst_ref=sib, send_sem=send_sems.at[3], recv_sem=recv_sems.at[3],
            device_id=(x, y, 1 - c), device_id_type=MESH)
        swap.start()
        swap.wait_recv()
        g = part[...] + sib[...]
        delta, new_m, new_v = _adamw(w_ref[...], g, m_ref[...], v_ref[...])
        og[...] = g
        od[...] = delta
        om[...] = new_m
        ov[...] = new_v
        for cp in sends:
            cp.wait_send()
        swap.wait_send()

    blk = jax.ShapeDtypeStruct((r, cdim), F32)
    return pl.pallas_call(
        body, out_shape=[blk] * 4,
        in_specs=[ANY_SPEC, VMEM_SPEC, VMEM_SPEC, VMEM_SPEC], out_specs=[VMEM_SPEC] * 4,
        scratch_shapes=[pltpu.VMEM((r, cdim), F32), pltpu.VMEM((3, r, cdim), F32), pltpu.VMEM((r, cdim), F32),
                        pltpu.VMEM((r, cdim), F32), pltpu.SemaphoreType.DMA,
                        pltpu.SemaphoreType.DMA((4,)), pltpu.SemaphoreType.DMA((4,))],
        compiler_params=pltpu.CompilerParams(vmem_limit_bytes=VMEM_LIMIT),
        name=name)(g4, w, m, v)


def reduce_replicated(p, w, m, v):
    r = p.shape[0]
    rels = [(dx, dy, dc) for dx in (0, 1) for dy in (0, 1) for dc in (0, 1)][1:]

    def body(p_ref, w_ref, m_ref, v_ref, og, od, om, ov, gat, send_sems, recv_sems):
        x, y, c = _place()
        me = 4 * x + 2 * y + c
        gat[me] = p_ref[...]

        def peer_of(j):
            dx, dy, dc = rels[j]
            return (1 - x if dx else x, 1 - y if dy else y, 1 - c if dc else c)

        def copy(j, block):
            return pltpu.make_async_remote_copy(
                src_ref=gat.at[block], dst_ref=gat.at[block], send_sem=send_sems.at[j], recv_sem=recv_sems.at[j],
                device_id=peer_of(j), device_id_type=MESH)

        sends = [copy(j, me) for j in range(len(rels))]
        for cp in sends:
            cp.start()
        for j in range(len(rels)):
            px, py, pc = peer_of(j)
            copy(j, 4 * px + 2 * py + pc).wait_recv()
        g = gat[0]
        for i in range(1, N_DEV):
            g = g + gat[i]
        delta, new_m, new_v = _adamw(w_ref[...], g, m_ref[...], v_ref[...])
        og[...] = g
        od[...] = delta
        om[...] = new_m
        ov[...] = new_v
        for cp in sends:
            cp.wait_send()

    blk = jax.ShapeDtypeStruct((r, LANES), F32)
    return pl.pallas_call(
        body, out_shape=[blk] * 4, in_specs=[VMEM_SPEC] * 4, out_specs=[VMEM_SPEC] * 4,
        scratch_shapes=[pltpu.VMEM((N_DEV, r, LANES), F32), pltpu.SemaphoreType.DMA((N_DEV - 1,)),
                        pltpu.SemaphoreType.DMA((N_DEV - 1,))],
        compiler_params=pltpu.CompilerParams(vmem_limit_bytes=VMEM_LIMIT),
        name="reduce_replicated")(p, w, m, v)


WEIGHTS = ("g_pre", "w_in", "w_pool", "pool_scale", "a_re", "a_im", "log_dt", "b_re", "b_im", "c_re", "c_im", "d_skip",
           "w_glu", "g_mem", "w_kv", "w_out", "g_post")
SHARDED = ("w_in", "w_glu", "w_kv", "w_out")
REPLICATED = tuple(n for n in WEIGHTS if n not in SHARDED)
PACK_TILE = SUBLANES * LANES


def _pack(arrays):
    parts = []
    for a in arrays:
        flat = a.reshape(-1)
        parts.append(jnp.pad(flat, (0, -flat.shape[0] % PACK_TILE)).reshape(-1, LANES))
    return jnp.concatenate(parts, axis=0)


def _unpack(packed, shapes):
    out, row = [], 0
    for shp in shapes:
        size = math.prod(shp)
        rows = -(-size // PACK_TILE) * SUBLANES
        out.append(packed[row:row + rows].reshape(-1)[:size].reshape(shp))
        row += rows
    return out


def _chip_major(a, n_cols):
    return a.reshape(a.shape[0], N_CHIPS, n_cols).transpose(1, 0, 2)


def _from_chip_major(a):
    return a.transpose(1, 0, 2).reshape(a.shape[1], N_CHIPS * a.shape[2])


def kernel(x, mem, g_pre, w_in, w_pool, pool_scale, a_re, a_im, log_dt, b_re, b_im, c_re, c_im, d_skip, w_glu, g_mem, w_kv, w_out, g_post, loss_target, m_g_pre, m_w_in, m_w_pool, m_pool_scale, m_a_re, m_a_im, m_log_dt, m_b_re, m_b_im, m_c_re, m_c_im, m_d_skip, m_w_glu, m_g_mem, m_w_kv, m_w_out, m_g_post, v_g_pre, v_w_in, v_w_pool, v_pool_scale, v_a_re, v_a_im, v_log_dt, v_b_re, v_b_im, v_c_re, v_c_im, v_d_skip, v_w_glu, v_g_mem, v_w_kv, v_w_out, v_g_post):
    given = dict(locals())
    wts = {n: given[n] for n in WEIGHTS}
    mom = {n: given["m_" + n] for n in WEIGHTS}
    var = {n: given["v_" + n] for n in WEIGHTS}

    g_in, g_glu, g_kv, g_out = gather_weights([wts[n][0] for n in SHARDED])
    w_in_full = _from_chip_major(g_in)
    w_glu_full = _from_chip_major(g_glu)
    w_kv_full = g_kv.reshape(D_MODEL, 2 * ATT_W)
    w_out_full = g_out.reshape(D_MODEL, D_MODEL)

    loss_part, grad_x, grads = local_step(
        x, mem, loss_target, g_pre, w_in_full, w_pool[0], pool_scale, a_re[0], a_im[0], log_dt[0], b_re[0], b_im[0],
        c_re[0], c_im[0], d_skip, w_glu_full, g_mem, w_kv_full, w_out_full, g_post)

    results = {}
    partial4 = {
        "w_in": grads["w_in"],
        "w_glu": _chip_major(grads["w_glu"], 2 * SSM_W // N_CHIPS),
        "w_kv": grads["w_kv"].reshape(N_CHIPS, D_MODEL // N_CHIPS, 2 * ATT_W),
        "w_out": grads["w_out"].reshape(N_CHIPS, D_MODEL // N_CHIPS, D_MODEL),
    }
    for n in SHARDED:
        outs = reduce_sharded(partial4[n], wts[n][0], mom[n][0], var[n][0], "reduce_" + n)
        results[n] = [o[None] for o in outs]

    loss_tile = jnp.broadcast_to(loss_part, (SUBLANES, LANES))
    packed = reduce_replicated(
        _pack([grads[n] for n in REPLICATED] + [loss_tile]),
        _pack([wts[n] for n in REPLICATED] + [jnp.zeros((SUBLANES, LANES), F32)]),
        _pack([mom[n] for n in REPLICATED] + [jnp.zeros((SUBLANES, LANES), F32)]),
        _pack([var[n] for n in REPLICATED] + [jnp.zeros((SUBLANES, LANES), F32)]))
    shapes = [wts[n].shape for n in REPLICATED] + [(SUBLANES, LANES)]
    unpacked = [_unpack(p, shapes) for p in packed]
    for i, n in enumerate(REPLICATED):
        results[n] = [u[i] for u in unpacked]
    loss = unpacked[0][-1][0, 0]

    return (loss, grad_x, *[results[n][0] for n in WEIGHTS], *[results[n][1] for n in WEIGHTS],
            *[results[n][2] for n in WEIGHTS], *[results[n][3] for n in WEIGHTS])
```

```python
import functools
import math

import jax
import jax.numpy as jnp
from jax import lax
from jax.experimental import pallas as pl
from jax.experimental.pallas import tpu as pltpu

F32 = jnp.float32
BF16 = jnp.bfloat16

D_MODEL = 1024
POOL_W = 384
SSM_W = 384
ATT_W = 256
POOL_GW = 96
POOL_WINDOWS = (2, 4, 8, 16)
POOL_PAD = 16
SSM_NG = 24
SSM_N = 64
SSM_GC = 16
NST = SSM_NG * SSM_N
N_MEM = 256
MEM_HEADS = 4
MEM_HD = 64
EPS = 1e-6

ADAM_LR = 0.001
ADAM_B1 = 0.9
ADAM_B2 = 0.999
ADAM_EPS = 1e-08
ADAM_WD = 0.01
ADAM_STEP = 10

SUBLANES = 8
LANES = 128
V7X_VMEM_BYTES = 64 * 2**20
VMEM_LIMIT = V7X_VMEM_BYTES - 8 * 2**20
SCAN_COLS = 512
MESH = pl.DeviceIdType.MESH

NT = (((1,), (1,)), ((), ()))
TN = (((0,), (0,)), ((), ()))


def _params(*sem):
    return pltpu.CompilerParams(dimension_semantics=sem, vmem_limit_bytes=VMEM_LIMIT)


def _dot(a, b):
    return jnp.dot(a, b, preferred_element_type=F32)


def _dot_nt(a, b):
    return lax.dot_general(a, b, NT, preferred_element_type=F32)


def _dot_tn(a, b):
    return lax.dot_general(a, b, TN, preferred_element_type=F32)


def _rms_scale(v):
    return lax.rsqrt(jnp.mean(v * v, axis=-1, keepdims=True) + EPS)


def _adamw(w, g, m, v):
    m = ADAM_B1 * m + (1.0 - ADAM_B1) * g
    v = ADAM_B2 * v + (1.0 - ADAM_B2) * (g * g)
    m_hat = m / (1.0 - ADAM_B1 ** ADAM_STEP)
    v_hat = v / (1.0 - ADAM_B2 ** ADAM_STEP)
    delta = -ADAM_LR * (m_hat / (jnp.sqrt(v_hat) + ADAM_EPS) + ADAM_WD * w)
    return delta, m, v


def in_proj(x2, g_pre, w_in):
    t = x2.shape[0]
    tm = min(512, t)

    def body(x_ref, g_ref, w_ref, o_ref):
        x = x_ref[...]
        h = (x * _rms_scale(x) * g_ref[...]).astype(BF16)
        o_ref[...] = _dot(h, w_ref[...])

    return pl.pallas_call(
        body, grid=(t // tm,),
        in_specs=[pl.BlockSpec((tm, D_MODEL), lambda i: (i, 0)),
                  pl.BlockSpec((1, D_MODEL), lambda i: (0, 0)),
                  pl.BlockSpec((D_MODEL, 2 * D_MODEL), lambda i: (0, 0))],
        out_specs=pl.BlockSpec((tm, 2 * D_MODEL), lambda i: (i, 0)),
        out_shape=jax.ShapeDtypeStruct((t, 2 * D_MODEL), F32),
        compiler_params=_params("arbitrary"), name="in_proj")(x2, g_pre, w_in)


def _pool_lane_window(shape):
    ch = lax.broadcasted_iota(jnp.int32, shape, 1)
    return jnp.where(ch < POOL_GW, 2.0, jnp.where(ch < 2 * POOL_GW, 4.0, jnp.where(ch < 3 * POOL_GW, 8.0, 16.0)))


def _pool_select(win, s2, s4, s8, s16):
    return jnp.where(win == 2.0, s2, jnp.where(win == 4.0, s4, jnp.where(win == 8.0, s8, s16)))


def _pool_counts(l, win):
    pos = (lax.broadcasted_iota(jnp.int32, (l, POOL_W), 0) + 1).astype(F32)
    return jnp.minimum(pos, win)


def _pool_diff(u, pad_ref, l):
    lo = POOL_PAD
    pad_ref[pl.ds(lo, l), :] = u
    s2 = u + pad_ref[pl.ds(lo - 1, l), :]
    pad_ref[pl.ds(lo, l), :] = s2
    s4 = s2 + pad_ref[pl.ds(lo - 2, l), :]
    pad_ref[pl.ds(lo, l), :] = s4
    s8 = s4 + pad_ref[pl.ds(lo - 4, l), :]
    pad_ref[pl.ds(lo, l), :] = s8
    s16 = s8 + pad_ref[pl.ds(lo - 8, l), :]
    win = _pool_lane_window((1, POOL_W))
    cnt = _pool_counts(l, win)
    return _pool_select(win, s2, s4, s8, s16) / cnt - u, cnt, win


def pool_fwd(proj, w_blk, pool_scale, nb, l):
    def body(u_ref, w_ref, ps_ref, y_ref, pad_ref):
        pad_ref[pl.ds(0, POOL_PAD), :] = jnp.zeros((POOL_PAD, POOL_W), F32)
        d, _, _ = _pool_diff(u_ref[...], pad_ref, l)
        y_ref[...] = _dot(d.astype(BF16), w_ref[...]) * ps_ref[...]

    return pl.pallas_call(
        body, grid=(nb,),
        in_specs=[pl.BlockSpec((l, POOL_W), lambda b: (b, 0)),
                  pl.BlockSpec((POOL_W, POOL_W), lambda b: (0, 0)),
                  pl.BlockSpec((1, POOL_W), lambda b: (0, 0))],
        out_specs=pl.BlockSpec((l, POOL_W), lambda b: (b, 0)),
        out_shape=jax.ShapeDtypeStruct((nb * l, POOL_W), F32),
        scratch_shapes=[pltpu.VMEM((POOL_PAD + l, POOL_W), F32)],
        compiler_params=_params("arbitrary"), name="pool_fwd")(proj, w_blk, pool_scale)


def pool_bwd(proj, d_ycat, w_blk, pool_scale, nb, l):
    def body(u_ref, dy_ref, w_ref, ps_ref, du_ref, dw_ref, dps_ref, pad_ref, padb_ref):
        b = pl.program_id(0)

        @pl.when(b == 0)
        def _():
            dw_ref[...] = jnp.zeros_like(dw_ref)
            dps_ref[...] = jnp.zeros_like(dps_ref)

        pad_ref[pl.ds(0, POOL_PAD), :] = jnp.zeros((POOL_PAD, POOL_W), F32)
        padb_ref[pl.ds(l, POOL_PAD), :] = jnp.zeros((POOL_PAD, POOL_W), F32)
        d, cnt, win = _pool_diff(u_ref[...], pad_ref, l)
        db = d.astype(BF16)
        w = w_ref[...]
        dy = dy_ref[...]
        dps_ref[...] += jnp.sum(dy * _dot(db, w), axis=0, keepdims=True)
        dyo = (dy * ps_ref[...]).astype(BF16)
        dw_ref[...] += _dot_tn(db, dyo)
        dd = _dot_nt(dyo, w)
        e = dd / cnt
        padb_ref[pl.ds(0, l), :] = e
        f2 = e + padb_ref[pl.ds(1, l), :]
        padb_ref[pl.ds(0, l), :] = f2
        f4 = f2 + padb_ref[pl.ds(2, l), :]
        padb_ref[pl.ds(0, l), :] = f4
        f8 = f4 + padb_ref[pl.ds(4, l), :]
        padb_ref[pl.ds(0, l), :] = f8
        f16 = f8 + padb_ref[pl.ds(8, l), :]
        du_ref[...] = _pool_select(win, f2, f4, f8, f16) - dd

    return pl.pallas_call(
        body, grid=(nb,),
        in_specs=[pl.BlockSpec((l, POOL_W), lambda b: (b, 0)),
                  pl.BlockSpec((l, POOL_W), lambda b: (b, 0)),
                  pl.BlockSpec((POOL_W, POOL_W), lambda b: (0, 0)),
                  pl.BlockSpec((1, POOL_W), lambda b: (0, 0))],
        out_specs=[pl.BlockSpec((l, POOL_W), lambda b: (b, 0)),
                   pl.BlockSpec((POOL_W, POOL_W), lambda b: (0, 0)),
                   pl.BlockSpec((1, POOL_W), lambda b: (0, 0))],
        out_shape=[jax.ShapeDtypeStruct((nb * l, POOL_W), F32),
                   jax.ShapeDtypeStruct((POOL_W, POOL_W), F32),
                   jax.ShapeDtypeStruct((1, POOL_W), F32)],
        scratch_shapes=[pltpu.VMEM((POOL_PAD + l, POOL_W), F32), pltpu.VMEM((POOL_PAD + l, POOL_W), F32)],
        compiler_params=_params("arbitrary"), name="pool_bwd")(proj, d_ycat, w_blk, pool_scale)


def _discretise(a_re, a_im, ldt, b_re, b_im):
    dt = jnp.exp(ldt)
    mag = jnp.exp(a_re * dt)
    th = a_im * dt
    lr = mag * jnp.cos(th)
    li = mag * jnp.sin(th)
    nr = lr - 1.0
    den = a_re * a_re + a_im * a_im
    fr = (nr * a_re + li * a_im) / den
    fi = (li * a_re - nr * a_im) / den
    return lr, li, fr * b_re - fi * b_im, fr * b_im + fi * b_re


def _cmul(ar, ai, br, bi):
    return ar * br - ai * bi, ar * bi + ai * br


def s5_scan_consts(a_re_row, a_im_row, ldt_row):
    def body(ar_ref, ai_ref, ld_ref, f_ref, b_ref):
        dt = jnp.exp(ld_ref[...])
        mag = jnp.exp(ar_ref[...] * dt)
        th = ai_ref[...] * dt
        lr = mag * jnp.cos(th)
        li = mag * jnp.sin(th)
        r = lax.broadcasted_iota(jnp.int32, (SUBLANES, NST), 0)
        for out_ref, sign, rev in ((f_ref, 1.0, False), (b_ref, -1.0, True)):
            p = [(lr, sign * li)]
            for _ in range(SUBLANES - 1):
                p.append(_cmul(p[-1][0], p[-1][1], lr, sign * li))
            for idx, s in enumerate((1, 2, 4)):
                inside = (r < SUBLANES - s) if rev else (r >= s)
                out_ref[pl.ds(2 * idx * SUBLANES, SUBLANES), :] = jnp.where(inside, p[s - 1][0], 0.0)
                out_ref[pl.ds((2 * idx + 1) * SUBLANES, SUBLANES), :] = jnp.where(inside, p[s - 1][1], 0.0)
            pr = jnp.zeros((SUBLANES, NST), F32)
            pi = jnp.zeros((SUBLANES, NST), F32)
            for row in range(SUBLANES):
                power = (SUBLANES - row) if rev else (row + 1)
                pr = jnp.where(r == row, p[power - 1][0], pr)
                pi = jnp.where(r == row, p[power - 1][1], pi)
            out_ref[pl.ds(6 * SUBLANES, SUBLANES), :] = pr
            out_ref[pl.ds(7 * SUBLANES, SUBLANES), :] = pi

    shape = jax.ShapeDtypeStruct((8 * SUBLANES, NST), F32)
    return pl.pallas_call(body, out_shape=[shape, shape], name="s5_scan_consts")(a_re_row, a_im_row, ldt_row)


def s5_bbar(a_re_c, a_im_c, ldt_c, b_re2, b_im2):
    def body(ar, ai, ld, br, bi, o_re, o_im):
        _, _, bbr, bbi = _discretise(ar[...], ai[...], ld[...], br[...], bi[...])
        o_re[...] = bbr
        o_im[...] = bbi

    shape = jax.ShapeDtypeStruct((NST, SSM_GC), F32)
    return pl.pallas_call(body, out_shape=[shape, shape], name="s5_bbar")(a_re_c, a_im_c, ldt_c, b_re2, b_im2)


def s5_param_bwd(a_re_c, a_im_c, ldt_c, b_re2, b_im2, d_lr, d_li, d_bbr, d_bbi):
    def body(ar, ai, ld, br, bi, g_lr, g_li, g_br, g_bi, o_ar, o_ai, o_ld, o_br, o_bi):
        _, vjp = jax.vjp(_discretise, ar[...], ai[...], ld[...], br[...], bi[...])
        d_ar, d_ai, d_ld, d_br, d_bi = vjp((g_lr[...], g_li[...], g_br[...], g_bi[...]))
        o_ar[...] = d_ar
        o_ai[...] = d_ai
        o_ld[...] = jnp.sum(d_ld.reshape(SSM_NG, SSM_N, 1), axis=1)
        o_br[...] = d_br
        o_bi[...] = d_bi

    col = jax.ShapeDtypeStruct((NST, 1), F32)
    mat = jax.ShapeDtypeStruct((NST, SSM_GC), F32)
    grp = jax.ShapeDtypeStruct((SSM_NG, 1), F32)
    return pl.pallas_call(body, out_shape=[col, col, grp, mat, mat], name="s5_param_bwd")(
        a_re_c, a_im_c, ldt_c, b_re2, b_im2, d_lr, d_li, d_bbr, d_bbi)


def _scan_tiles(buf_ref, cst_ref, carry_ref, rows, reverse, h_ref=None, acc_ref=None):
    n_tiles = rows // SUBLANES
    shifts = ((1, 0), (2, 2), (4, 4))
    row_id = lax.broadcasted_iota(jnp.int32, (SUBLANES, SCAN_COLS), 0)
    for j in range(NST // SCAN_COLS):
        c_re = pl.ds(j * SCAN_COLS, SCAN_COLS)
        c_im = pl.ds(NST + j * SCAN_COLS, SCAN_COLS)

        def step(i, carry, c_re=c_re, c_im=c_im):
            tile = (n_tiles - 1 - i) if reverse else i
            rws = pl.ds(pl.multiple_of(tile * SUBLANES, SUBLANES), SUBLANES)
            xr = buf_ref[rws, c_re]
            xi = buf_ref[rws, c_im]
            for s, base in shifts:
                amount = (SUBLANES - s) if reverse else s
                sr = pltpu.roll(xr, amount, 0)
                si = pltpu.roll(xi, amount, 0)
                mr = cst_ref[pl.ds(base * SUBLANES, SUBLANES), c_re]
                mi = cst_ref[pl.ds((base + 1) * SUBLANES, SUBLANES), c_re]
                xr, xi = xr + (mr * sr - mi * si), xi + (mr * si + mi * sr)
            pr = cst_ref[pl.ds(6 * SUBLANES, SUBLANES), c_re]
            pi = cst_ref[pl.ds(7 * SUBLANES, SUBLANES), c_re]
            cr, ci = carry[0], carry[1]
            xr, xi = xr + (pr * cr - pi * ci), xi + (pr * ci + pi * cr)
            buf_ref[rws, c_re] = xr
            buf_ref[rws, c_im] = xi
            edge = 0 if reverse else SUBLANES - 1
            ncr = jnp.broadcast_to(xr[edge:edge + 1, :], (SUBLANES, SCAN_COLS))
            nci = jnp.broadcast_to(xi[edge:edge + 1, :], (SUBLANES, SCAN_COLS))
            if not reverse:
                return ncr, nci
            gnr = jnp.where(row_id == SUBLANES - 1, cr, pltpu.roll(xr, SUBLANES - 1, 0))
            gni = jnp.where(row_id == SUBLANES - 1, ci, pltpu.roll(xi, SUBLANES - 1, 0))
            hr = h_ref[rws, c_re]
            hi = h_ref[rws, c_im]
            return ncr, nci, carry[2] + (gnr * hr + gni * hi), carry[3] + (gni * hr - gnr * hi)

        init = (carry_ref[:, c_re], carry_ref[:, c_im])
        if reverse:
            init = init + (acc_ref[:, c_re], acc_ref[:, c_im])
        out = lax.fori_loop(0, n_tiles, step, init)
        carry_ref[:, c_re] = out[0]
        carry_ref[:, c_im] = out[1]
        if reverse:
            acc_ref[:, c_re] = out[2]
            acc_ref[:, c_im] = out[3]


def s5_fwd(proj, b_blk, c_blk, cst, d_skip, w_glu, nb, l):
    tt = min(256, l)
    nt = l // tt

    def body(u_ref, b_ref, c_ref, cst_ref, ds_ref, wg_ref, y_ref, h_ref, yp_ref, z_ref, buf_ref, carry_ref):
        @pl.when(pl.program_id(1) == 0)
        def _():
            carry_ref[...] = jnp.zeros_like(carry_ref)

        u = u_ref[...]
        buf_ref[...] = _dot(u.astype(BF16), b_ref[...])
        _scan_tiles(buf_ref, cst_ref, carry_ref, tt, reverse=False)
        h = buf_ref[...].astype(BF16)
        h_ref[...] = h
        ypre = _dot(h, c_ref[...]) + ds_ref[...] * u
        yp_ref[...] = ypre
        z = _dot(jax.nn.gelu(ypre).astype(BF16), wg_ref[...])
        z_ref[...] = z
        y_ref[...] = z[:, :SSM_W] * jax.nn.sigmoid(z[:, SSM_W:])

    row = lambda b, t: (b * nt + t, 0)
    const = lambda b, t: (0, 0)
    return pl.pallas_call(
        body, grid=(nb, nt),
        in_specs=[pl.BlockSpec((tt, SSM_W), lambda b, t: (b * nt + t, 1)),
                  pl.BlockSpec((SSM_W, 2 * NST), const),
                  pl.BlockSpec((2 * NST, SSM_W), const),
                  pl.BlockSpec((8 * SUBLANES, NST), const),
                  pl.BlockSpec((1, SSM_W), const),
                  pl.BlockSpec((SSM_W, 2 * SSM_W), const)],
        out_specs=[pl.BlockSpec((tt, SSM_W), row),
                   pl.BlockSpec((tt, 2 * NST), row),
                   pl.BlockSpec((tt, SSM_W), row),
                   pl.BlockSpec((tt, 2 * SSM_W), row)],
        out_shape=[jax.ShapeDtypeStruct((nb * l, SSM_W), F32),
                   jax.ShapeDtypeStruct((nb * l, 2 * NST), BF16),
                   jax.ShapeDtypeStruct((nb * l, SSM_W), F32),
                   jax.ShapeDtypeStruct((nb * l, 2 * SSM_W), F32)],
        scratch_shapes=[pltpu.VMEM((tt, 2 * NST), F32), pltpu.VMEM((SUBLANES, 2 * NST), F32)],
        compiler_params=_params("arbitrary", "arbitrary"), name="s5_fwd")(proj, b_blk, c_blk, cst, d_skip, w_glu)


def s5_bwd(proj, d_ycat, h, ypre, z, b_blk, c_blk, cst_rev, d_skip, w_glu, nb, l):
    tt = min(256, l)
    nt = l // tt

    def body(u_ref, dy_ref, h_ref, yp_ref, z_ref, b_ref, c_ref, cst_ref, ds_ref, wg_ref,
             du_ref, dwg_ref, dds_ref, dc_ref, db_ref, dlam_ref, buf_ref, hbuf_ref, carry_ref):
        b = pl.program_id(0)
        t = pl.program_id(1)

        @pl.when((b == 0) & (t == 0))
        def _():
            dwg_ref[...] = jnp.zeros_like(dwg_ref)
            dds_ref[...] = jnp.zeros_like(dds_ref)
            dc_ref[...] = jnp.zeros_like(dc_ref)
            db_ref[...] = jnp.zeros_like(db_ref)
            dlam_ref[...] = jnp.zeros_like(dlam_ref)

        @pl.when(t == 0)
        def _():
            carry_ref[...] = jnp.zeros_like(carry_ref)

        u = u_ref[...]
        ypre = yp_ref[...]
        z = z_ref[...]
        z1 = z[:, :SSM_W]
        sg = jax.nn.sigmoid(z[:, SSM_W:])
        dy = dy_ref[...]
        dz = jnp.concatenate([dy * sg, dy * z1 * sg * (1.0 - sg)], axis=1).astype(BF16)
        yg, gelu_vjp = jax.vjp(jax.nn.gelu, ypre)
        dwg_ref[...] += _dot_tn(yg.astype(BF16), dz)
        dypre = gelu_vjp(_dot_nt(dz, wg_ref[...]))[0]
        dds_ref[...] += jnp.sum(dypre * u, axis=0, keepdims=True)
        dyb = dypre.astype(BF16)
        hb = h_ref[...]
        dc_ref[...] += _dot_tn(hb, dyb)
        hbuf_ref[...] = hb.astype(F32)
        buf_ref[...] = _dot_nt(dyb, c_ref[...])
        _scan_tiles(buf_ref, cst_ref, carry_ref, tt, reverse=True, h_ref=hbuf_ref, acc_ref=dlam_ref)
        g = buf_ref[...].astype(BF16)
        db_ref[...] += _dot_tn(g, u.astype(BF16))
        du_ref[...] = _dot_nt(g, b_ref[...]) + ds_ref[...] * dypre

        @pl.when((b == nb - 1) & (t == nt - 1))
        def _():
            dlam_ref[...] = jnp.broadcast_to(jnp.sum(dlam_ref[...], axis=0, keepdims=True), dlam_ref.shape)

    def rrow(col):
        return lambda b, t: (b * nt + (nt - 1 - t), col)

    const = lambda b, t: (0, 0)
    return pl.pallas_call(
        body, grid=(nb, nt),
        in_specs=[pl.BlockSpec((tt, SSM_W), rrow(1)),
                  pl.BlockSpec((tt, SSM_W), rrow(1)),
                  pl.BlockSpec((tt, 2 * NST), rrow(0)),
                  pl.BlockSpec((tt, SSM_W), rrow(0)),
                  pl.BlockSpec((tt, 2 * SSM_W), rrow(0)),
                  pl.BlockSpec((SSM_W, 2 * NST), const),
                  pl.BlockSpec((2 * NST, SSM_W), const),
                  pl.BlockSpec((8 * SUBLANES, NST), const),
                  pl.BlockSpec((1, SSM_W), const),
                  pl.BlockSpec((SSM_W, 2 * SSM_W), const)],
        out_specs=[pl.BlockSpec((tt, SSM_W), rrow(0)),
                   pl.BlockSpec((SSM_W, 2 * SSM_W), const),
                   pl.BlockSpec((1, SSM_W), const),
                   pl.BlockSpec((2 * NST, SSM_W), const),
                   pl.BlockSpec((2 * NST, SSM_W), const),
                   pl.BlockSpec((SUBLANES, 2 * NST), const)],
        out_shape=[jax.ShapeDtypeStruct((nb * l, SSM_W), F32),
                   jax.ShapeDtypeStruct((SSM_W, 2 * SSM_W), F32),
                   jax.ShapeDtypeStruct((1, SSM_W), F32),
                   jax.ShapeDtypeStruct((2 * NST, SSM_W), F32),
                   jax.ShapeDtypeStruct((2 * NST, SSM_W), F32),
                   jax.ShapeDtypeStruct((SUBLANES, 2 * NST), F32)],
        scratch_shapes=[pltpu.VMEM((tt, 2 * NST), F32), pltpu.VMEM((tt, 2 * NST), F32),
                        pltpu.VMEM((SUBLANES, 2 * NST), F32)],
        compiler_params=_params("arbitrary", "arbitrary"), name="s5_bwd")(
            proj, d_ycat, h, ypre, z, b_blk, c_blk, cst_rev, d_skip, w_glu)


def _head_mask(hh):
    lane = lax.broadcasted_iota(jnp.int32, (1, ATT_W), 1)
    return (lane >= hh * MEM_HD) & (lane < (hh + 1) * MEM_HD)


def _mem_kv(mem_ref, gm_ref, wkv_ref):
    m = mem_ref[0]
    mh = m * _rms_scale(m)
    mb = (mh * gm_ref[...]).astype(BF16)
    kv = _dot(mb, wkv_ref[...])
    return mh, mb, kv[:, :ATT_W].astype(BF16), kv[:, ATT_W:].astype(BF16)


def _softmax_rows(s):
    e = jnp.exp(s - jnp.max(s, axis=-1, keepdims=True))
    return e / jnp.sum(e, axis=-1, keepdims=True)


def attn_fwd(proj, mem, g_mem, w_kv, nb, l):
    tq = min(512, l)
    nq = l // tq
    scale = MEM_HD ** -0.5

    def body(q_ref, mem_ref, gm_ref, wkv_ref, o_ref, k_s, v_s):
        @pl.when(pl.program_id(1) == 0)
        def _():
            _, _, k, v = _mem_kv(mem_ref, gm_ref, wkv_ref)
            k_s[...] = k
            v_s[...] = v

        q = q_ref[...].astype(BF16)
        k = k_s[...]
        v = v_s[...]
        o = jnp.zeros((tq, ATT_W), F32)
        for hh in range(MEM_HEADS):
            msk = _head_mask(hh)
            p = _softmax_rows(_dot_nt(q, jnp.where(msk, k, 0)) * scale)
            o = o + _dot(p.astype(BF16), jnp.where(msk, v, 0))
        o_ref[...] = o

    const = lambda b, t: (0, 0)
    return pl.pallas_call(
        body, grid=(nb, nq),
        in_specs=[pl.BlockSpec((tq, ATT_W), lambda b, t: (b * nq + t, 3)),
                  pl.BlockSpec((1, N_MEM, D_MODEL), lambda b, t: (b, 0, 0)),
                  pl.BlockSpec((1, D_MODEL), const),
                  pl.BlockSpec((D_MODEL, 2 * ATT_W), const)],
        out_specs=pl.BlockSpec((tq, ATT_W), lambda b, t: (b * nq + t, 0)),
        out_shape=jax.ShapeDtypeStruct((nb * l, ATT_W), F32),
        scratch_shapes=[pltpu.VMEM((N_MEM, ATT_W), BF16), pltpu.VMEM((N_MEM, ATT_W), BF16)],
        compiler_params=_params("arbitrary", "arbitrary"), name="attn_fwd")(proj, mem, g_mem, w_kv)


def attn_bwd(proj, d_ycat, mem, g_mem, w_kv, nb, l):
    tq = min(512, l)
    nq = l // tq
    scale = MEM_HD ** -0.5

    def body(q_ref, do_ref, mem_ref, gm_ref, wkv_ref, dq_ref, dwkv_ref, dgm_ref, k_s, v_s, mb_s, dk_s, dv_s):
        b = pl.program_id(0)
        t = pl.program_id(1)

        @pl.when((b == 0) & (t == 0))
        def _():
            dwkv_ref[...] = jnp.zeros_like(dwkv_ref)
            dgm_ref[...] = jnp.zeros_like(dgm_ref)

        @pl.when(t == 0)
        def _():
            _, mb, k, v = _mem_kv(mem_ref, gm_ref, wkv_ref)
            k_s[...] = k
            v_s[...] = v
            mb_s[...] = mb
            dk_s[...] = jnp.zeros_like(dk_s)
            dv_s[...] = jnp.zeros_like(dv_s)

        q = q_ref[...].astype(BF16)
        do = do_ref[...].astype(BF16)
        k = k_s[...]
        v = v_s[...]
        dq = jnp.zeros((tq, ATT_W), F32)
        dk = jnp.zeros((N_MEM, ATT_W), F32)
        dv = jnp.zeros((N_MEM, ATT_W), F32)
        for hh in range(MEM_HEADS):
            msk = _head_mask(hh)
            kh = jnp.where(msk, k, 0)
            p = _softmax_rows(_dot_nt(q, kh) * scale)
            dp = _dot_nt(do, jnp.where(msk, v, 0))
            ds = (p * (dp - jnp.sum(dp * p, axis=-1, keepdims=True)) * scale).astype(BF16)
            dq = dq + _dot(ds, kh)
            dk = dk + jnp.where(msk, _dot_tn(ds, q), 0.0)
            dv = dv + jnp.where(msk, _dot_tn(p.astype(BF16), do), 0.0)
        dq_ref[...] = dq
        dk_s[...] += dk
        dv_s[...] += dv

        @pl.when(t == nq - 1)
        def _():
            dkv = jnp.concatenate([dk_s[...], dv_s[...]], axis=1).astype(BF16)
            dwkv_ref[...] += _dot_tn(mb_s[...], dkv)
            m = mem_ref[0]
            dgm_ref[...] += jnp.sum(_dot_nt(dkv, wkv_ref[...]) * (m * _rms_scale(m)), axis=0, keepdims=True)

    const = lambda b, t: (0, 0)
    return pl.pallas_call(
        body, grid=(nb, nq),
        in_specs=[pl.BlockSpec((tq, ATT_W), lambda b, t: (b * nq + t, 3)),
                  pl.BlockSpec((tq, ATT_W), lambda b, t: (b * nq + t, 3)),
                  pl.BlockSpec((1, N_MEM, D_MODEL), lambda b, t: (b, 0, 0)),
                  pl.BlockSpec((1, D_MODEL), const),
                  pl.BlockSpec((D_MODEL, 2 * ATT_W), const)],
        out_specs=[pl.BlockSpec((tq, ATT_W), lambda b, t: (b * nq + t, 0)),
                   pl.BlockSpec((D_MODEL, 2 * ATT_W), const),
                   pl.BlockSpec((1, D_MODEL), const)],
        out_shape=[jax.ShapeDtypeStruct((nb * l, ATT_W), F32),
                   jax.ShapeDtypeStruct((D_MODEL, 2 * ATT_W), F32),
                   jax.ShapeDtypeStruct((1, D_MODEL), F32)],
        scratch_shapes=[pltpu.VMEM((N_MEM, ATT_W), BF16), pltpu.VMEM((N_MEM, ATT_W), BF16),
                        pltpu.VMEM((N_MEM, D_MODEL), BF16),
                        pltpu.VMEM((N_MEM, ATT_W), F32), pltpu.VMEM((N_MEM, ATT_W), F32)],
        compiler_params=_params("arbitrary", "arbitrary"), name="attn_bwd")(proj, d_ycat, mem, g_mem, w_kv)


def tail(y_pool, y_ssm, y_att, proj, x2, target, w_out, g_post):
    t = x2.shape[0]
    tm = min(256, t)

    def body(yp_ref, ys_ref, ya_ref, gate_ref, x_ref, tg_ref, w_ref, gp_ref,
             dz_ref, dyc_ref, dgate_ref, dw_ref, dgp_ref, loss_ref):
        @pl.when(pl.program_id(0) == 0)
        def _():
            dw_ref[...] = jnp.zeros_like(dw_ref)
            dgp_ref[...] = jnp.zeros_like(dgp_ref)
            loss_ref[...] = jnp.zeros_like(loss_ref)

        ycat = jnp.concatenate([yp_ref[...], ys_ref[...], ya_ref[...]], axis=1)
        gate = gate_ref[...]
        sg = jax.nn.sigmoid(gate)
        silu = gate * sg
        yb = (ycat * silu).astype(BF16)
        w = w_ref[...]
        out = _dot(yb, w)
        r2 = _rms_scale(out)
        oh = out * r2
        gp = gp_ref[...]
        err = (x_ref[...] + oh * gp) - tg_ref[...]
        loss_ref[...] += 0.5 * jnp.sum(jnp.mean(err * err, axis=-1, keepdims=True), axis=0, keepdims=True)
        dz = err * (1.0 / D_MODEL)
        dz_ref[...] = dz
        dgp_ref[...] += jnp.sum(dz * oh, axis=0, keepdims=True)
        dn = dz * gp
        dout = (r2 * (dn - oh * jnp.mean(dn * oh, axis=-1, keepdims=True))).astype(BF16)
        dw_ref[...] += _dot_tn(yb, dout)
        dy = _dot_nt(dout, w)
        dyc_ref[...] = dy * silu
        dgate_ref[...] = dy * ycat * (sg * (1.0 + gate * (1.0 - sg)))

    row = lambda i: (i, 0)
    const = lambda i: (0, 0)
    full = jax.ShapeDtypeStruct((t, D_MODEL), F32)
    return pl.pallas_call(
        body, grid=(t // tm,),
        in_specs=[pl.BlockSpec((tm, POOL_W), row),
                  pl.BlockSpec((tm, SSM_W), row),
                  pl.BlockSpec((tm, ATT_W), row),
                  pl.BlockSpec((tm, D_MODEL), lambda i: (i, 1)),
                  pl.BlockSpec((tm, D_MODEL), row),
                  pl.BlockSpec((tm, D_MODEL), row),
                  pl.BlockSpec((D_MODEL, D_MODEL), const),
                  pl.BlockSpec((1, D_MODEL), const)],
        out_specs=[pl.BlockSpec((tm, D_MODEL), row),
                   pl.BlockSpec((tm, D_MODEL), row),
                   pl.BlockSpec((tm, D_MODEL), row),
                   pl.BlockSpec((D_MODEL, D_MODEL), const),
                   pl.BlockSpec((1, D_MODEL), const),
                   pl.BlockSpec((1, LANES), const)],
        out_shape=[full, full, full,
                   jax.ShapeDtypeStruct((D_MODEL, D_MODEL), F32),
                   jax.ShapeDtypeStruct((1, D_MODEL), F32),
                   jax.ShapeDtypeStruct((1, LANES), F32)],
        compiler_params=_params("arbitrary"), name="tail")(y_pool, y_ssm, y_att, proj, x2, target, w_out, g_post)


def in_proj_bwd(du_pool, du_ssm, dq, d_gate, x2, dz, g_pre, w_in):
    t = x2.shape[0]
    tm = min(256, t)
    shard = 2 * D_MODEL // 4

    def body(dup_ref, dus_ref, dq_ref, dg_ref, x_ref, dz_ref, g_ref, w_ref, gx_ref, dw_ref, dgp_ref):
        @pl.when(pl.program_id(0) == 0)
        def _():
            dw_ref[...] = jnp.zeros_like(dw_ref)
            dgp_ref[...] = jnp.zeros_like(dgp_ref)

        dproj = jnp.concatenate([dup_ref[...], dus_ref[...], dq_ref[...], dg_ref[...]], axis=1).astype(BF16)
        x = x_ref[...]
        r1 = _rms_scale(x)
        xh = x * r1
        g = g_ref[...]
        hb = (xh * g).astype(BF16)
        for k in range(4):
            dw_ref[k] += _dot_tn(hb, dproj[:, k * shard:(k + 1) * shard])
        dh = _dot_nt(dproj, w_ref[...])
        dgp_ref[...] += jnp.sum(dh * xh, axis=0, keepdims=True)
        dn = dh * g
        gx_ref[...] = dz_ref[...] + r1 * (dn - xh * jnp.mean(dn * xh, axis=-1, keepdims=True))

    row = lambda i: (i, 0)
    const = lambda i: (0, 0)
    return pl.pallas_call(
        body, grid=(t // tm,),
        in_specs=[pl.BlockSpec((tm, POOL_W), row),
                  pl.BlockSpec((tm, SSM_W), row),
                  pl.BlockSpec((tm, ATT_W), row),
                  pl.BlockSpec((tm, D_MODEL), row),
                  pl.BlockSpec((tm, D_MODEL), row),
                  pl.BlockSpec((tm, D_MODEL), row),
                  pl.BlockSpec((1, D_MODEL), const),
                  pl.BlockSpec((D_MODEL, 2 * D_MODEL), const)],
        out_specs=[pl.BlockSpec((tm, D_MODEL), row),
                   pl.BlockSpec((4, D_MODEL, shard), lambda i: (0, 0, 0)),
                   pl.BlockSpec((1, D_MODEL), const)],
        out_shape=[jax.ShapeDtypeStruct((t, D_MODEL), F32),
                   jax.ShapeDtypeStruct((4, D_MODEL, shard), F32),
                   jax.ShapeDtypeStruct((1, D_MODEL), F32)],
        compiler_params=_params("arbitrary"), name="in_proj_bwd")(du_pool, du_ssm, dq, d_gate, x2, dz, g_pre, w_in)


def _block_diag(blocks):
    g, r, c = blocks.shape
    eye = jnp.eye(g, dtype=blocks.dtype)
    return jnp.einsum("grc,gh->grhc", blocks, eye).reshape(g * r, g * c)


def _block_diag_extract(mat, g, r, c):
    eye = jnp.eye(g, dtype=mat.dtype)
    return jnp.einsum("grhc,gh->grc", mat.reshape(g, r, g, c), eye)


def local_step(x, mem, target, g_pre, w_in, w_pool, pool_scale, a_re, a_im, log_dt, b_re, b_im, c_re, c_im, d_skip,
               w_glu, g_mem, w_kv, w_out, g_post):
    nb, l, _ = x.shape
    x2 = x.reshape(nb * l, D_MODEL)
    tg2 = target.reshape(nb * l, D_MODEL)

    rowv = lambda a: a.reshape(1, NST)
    colv = lambda a: a.reshape(NST, 1)
    ldt_full = jnp.broadcast_to(log_dt.reshape(SSM_NG, 1), (SSM_NG, SSM_N))
    b_re2 = b_re.reshape(NST, SSM_GC)
    b_im2 = b_im.reshape(NST, SSM_GC)
    cst_f, cst_b = s5_scan_consts(rowv(a_re), rowv(a_im), rowv(ldt_full))
    bbr, bbi = s5_bbar(colv(a_re), colv(a_im), colv(ldt_full), b_re2, b_im2)
    to_b_blk = lambda bb: _block_diag(bb.reshape(SSM_NG, SSM_N, SSM_GC).transpose(0, 2, 1))
    b_blk = jnp.concatenate([to_b_blk(bbr), to_b_blk(bbi)], axis=1).astype(BF16)
    to_c_blk = lambda cc: _block_diag(cc.reshape(SSM_NG, SSM_GC, SSM_N).transpose(0, 2, 1))
    c_blk = jnp.concatenate([to_c_blk(c_re), -to_c_blk(c_im)], axis=0).astype(BF16)
    w_pool_blk = _block_diag(w_pool.reshape(4, POOL_GW, POOL_GW)).astype(BF16)

    proj = in_proj(x2, g_pre, w_in)
    y_pool = pool_fwd(proj, w_pool_blk, pool_scale, nb, l)
    y_ssm, h, ypre, z = s5_fwd(proj, b_blk, c_blk, cst_f, d_skip, w_glu, nb, l)
    y_att = attn_fwd(proj, mem, g_mem, w_kv, nb, l)
    dz, d_ycat, d_gate, d_w_out, d_g_post, loss = tail(y_pool, y_ssm, y_att, proj, x2, tg2, w_out, g_post)
    du_pool, d_wp_blk, d_pool_scale = pool_bwd(proj, d_ycat, w_pool_blk, pool_scale, nb, l)
    du_ssm, d_w_glu, d_d_skip, d_c_full, d_b_full, d_lam = s5_bwd(
        proj, d_ycat, h, ypre, z, b_blk, c_blk, cst_b, d_skip, w_glu, nb, l)
    dq, d_w_kv, d_g_mem = attn_bwd(proj, d_ycat, mem, g_mem, w_kv, nb, l)
    grad_x, d_w_in, d_g_pre = in_proj_bwd(du_pool, du_ssm, dq, d_gate, x2, dz, g_pre, w_in)

    d_c = _block_diag_extract(d_c_full.reshape(2 * SSM_NG * SSM_N, SSM_W).reshape(2, NST, SSM_W)[0], SSM_NG, SSM_N, SSM_GC)
    d_ci = _block_diag_extract(d_c_full.reshape(2, NST, SSM_W)[1], SSM_NG, SSM_N, SSM_GC)
    d_bb = d_b_full.reshape(2, NST, SSM_W)
    d_bbr = _block_diag_extract(d_bb[0], SSM_NG, SSM_N, SSM_GC).reshape(NST, SSM_GC)
    d_bbi = _block_diag_extract(d_bb[1], SSM_NG, SSM_N, SSM_GC).reshape(NST, SSM_GC)
    d_lam_row = d_lam[0]
    d_ar, d_ai, d_ld, d_br, d_bi = s5_param_bwd(
        colv(a_re), colv(a_im), colv(ldt_full), b_re2, b_im2,
        d_lam_row[:NST].reshape(NST, 1), d_lam_row[NST:].reshape(NST, 1), d_bbr, d_bbi)

    grads = {
        "g_pre": d_g_pre,
        "w_in": d_w_in,
        "w_pool": _block_diag_extract(d_wp_blk, 4, POOL_GW, POOL_GW).reshape(1, 4, POOL_GW, POOL_GW),
        "pool_scale": d_pool_scale,
        "a_re": d_ar.reshape(1, SSM_NG, SSM_N),
        "a_im": d_ai.reshape(1, SSM_NG, SSM_N),
        "log_dt": d_ld.reshape(1, SSM_NG),
        "b_re": d_br.reshape(1, SSM_NG, SSM_N, SSM_GC),
        "b_im": d_bi.reshape(1, SSM_NG, SSM_N, SSM_GC),
        "c_re": d_c.transpose(0, 2, 1).reshape(1, SSM_NG, SSM_GC, SSM_N),
        "c_im": (-d_ci).transpose(0, 2, 1).reshape(1, SSM_NG, SSM_GC, SSM_N),
        "d_skip": d_d_skip,
        "w_glu": d_w_glu,
        "g_mem": d_g_mem,
        "w_kv": d_w_kv,
        "w_out": d_w_out,
        "g_post": d_g_post,
    }
    return loss, grad_x.reshape(nb, l, D_MODEL), grads


VMEM_SPEC = pl.BlockSpec(memory_space=pltpu.VMEM)
ANY_SPEC = pl.BlockSpec(memory_space=pl.ANY)
N_CHIPS = 4
N_DEV = 8


def _place():
    return lax.axis_index("x"), lax.axis_index("y"), lax.axis_index("c")


def _other_chips(x, y):
    return [(1 - x, y), (x, 1 - y), (1 - x, 1 - y)]


def gather_weights(shards):
    n = len(shards)

    def body(*refs):
        ins, outs = refs[:n], refs[n:2 * n]
        send_sems, recv_sems = refs[2 * n:]
        x, y, c = _place()
        k = 2 * x + y
        chips = _other_chips(x, y)
        for i in range(n):
            outs[i][k] = ins[i][...].astype(BF16)

        def copy(i, block, core_half, to, sem):
            hr = shards[i].shape[0] // 2
            rows = outs[i].at[block, pl.ds(pl.multiple_of(core_half * hr, 2 * SUBLANES), hr)]
            return pltpu.make_async_remote_copy(
                src_ref=rows, dst_ref=rows, send_sem=send_sems.at[sem], recv_sem=recv_sems.at[sem],
                device_id=to, device_id_type=MESH)

        pairs = [(i, j) for i in range(n) for j in range(3)]
        first = [copy(i, k, c, (*chips[j], c), i * 3 + j) for i, j in pairs]
        for cp in first:
            cp.start()
        passed = []
        for i, j in pairs:
            px, py = chips[j]
            copy(i, 2 * px + py, c, (px, py, c), i * 3 + j).wait_recv()
            fw = copy(i, 2 * px + py, c, (x, y, 1 - c), 3 * n + i * 3 + j)
            fw.start()
            passed.append(fw)
        for i, j in pairs:
            px, py = chips[j]
            copy(i, 2 * px + py, 1 - c, (x, y, 1 - c), 3 * n + i * 3 + j).wait_recv()
        for cp in first + passed:
            cp.wait_send()

    return pl.pallas_call(
        body,
        out_shape=[jax.ShapeDtypeStruct((N_CHIPS,) + s.shape, BF16) for s in shards],
        in_specs=[VMEM_SPEC] * n, out_specs=[VMEM_SPEC] * n,
        scratch_shapes=[pltpu.SemaphoreType.DMA((6 * n,)), pltpu.SemaphoreType.DMA((6 * n,))],
        compiler_params=pltpu.CompilerParams(vmem_limit_bytes=VMEM_LIMIT),
        name="gather_weights")(*shards)


def reduce_sharded(g4, w, m, v, name):
    _, r, cdim = g4.shape
    hr = r // 2

    def body(g_ref, w_ref, m_ref, v_ref, og, od, om, ov, mine, theirs, sendb, recvb, lsem, send_sems, recv_sems):
        x, y, c = _place()
        k = 2 * x + y
        chips = _other_chips(x, y)
        sib = (x, y, 1 - c)
        my_rows = pl.ds(pl.multiple_of(c * hr, 2 * SUBLANES), hr)
        sib_rows = pl.ds(pl.multiple_of((1 - c) * hr, 2 * SUBLANES), hr)

        load = pltpu.make_async_copy(g_ref.at[:, my_rows, :], mine, lsem)
        load.start()
        swap = pltpu.make_async_remote_copy(
            src_ref=g_ref.at[:, sib_rows, :], dst_ref=theirs, send_sem=send_sems.at[0], recv_sem=recv_sems.at[0],
            device_id=sib, device_id_type=MESH)
        swap.start()
        load.wait()
        swap.wait_recv()
        sends = [swap]
        for j, (px, py) in enumerate(chips):
            kp = 2 * px + py
            sendb[j] = (mine[kp] + theirs[kp]).astype(BF16)
            cp = pltpu.make_async_remote_copy(
                src_ref=sendb.at[j], dst_ref=recvb.at[j], send_sem=send_sems.at[1 + j], recv_sem=recv_sems.at[1 + j],
                device_id=(px, py, c), device_id_type=MESH)
            cp.start()
            sends.append(cp)
        half = mine[k] + theirs[k]
        for cp in sends[1:]:
            cp.wait_recv()
        half = ((half + recvb[0].astype(F32)) + recvb[1].astype(F32)) + recvb[2].astype(F32)
        og[my_rows, :] = half
        last = pltpu.make_async_remote_copy(
            src_ref=og.at[my_rows, :], dst_ref=og.at[my_rows, :], send_sem=send_sems.at[4], recv_sem=recv_sems.at[4],
            device_id=sib, device_id_type=MESH)
        last.start()
        pltpu.make_async_remote_copy(
            src_ref=og.at[sib_rows, :], dst_ref=og.at[sib_rows, :], send_sem=send_sems.at[4], recv_sem=recv_sems.at[4],
            device_id=sib, device_id_type=MESH).wait_recv()
        delta, new_m, new_v = _adamw(w_ref[...], og[...], m_ref[...], v_ref[...])
        od[...] = delta
        om[...] = new_m
        ov[...] = new_v
        for cp in sends + [last]:
            cp.wait_send()

    blk = jax.ShapeDtypeStruct((r, cdim), F32)
    return pl.pallas_call(
        body, out_shape=[blk] * 4,
        in_specs=[ANY_SPEC, VMEM_SPEC, VMEM_SPEC, VMEM_SPEC], out_specs=[VMEM_SPEC] * 4,
        scratch_shapes=[pltpu.VMEM((N_CHIPS, hr, cdim), F32), pltpu.VMEM((N_CHIPS, hr, cdim), F32),
                        pltpu.VMEM((3, hr, cdim), BF16), pltpu.VMEM((3, hr, cdim), BF16), pltpu.SemaphoreType.DMA,
                        pltpu.SemaphoreType.DMA((5,)), pltpu.SemaphoreType.DMA((5,))],
        compiler_params=pltpu.CompilerParams(vmem_limit_bytes=VMEM_LIMIT),
        name=name)(g4, w, m, v)


def reduce_replicated(p, w, m, v):
    r = p.shape[0]
    hr = r // 2

    def body(p_ref, w_ref, m_ref, v_ref, og, od, om, ov, theirs, gat, send_sems, recv_sems):
        x, y, c = _place()
        k = 2 * x + y
        chips = _other_chips(x, y)
        sib = (x, y, 1 - c)
        my_rows = pl.ds(pl.multiple_of(c * hr, SUBLANES), hr)
        sib_rows = pl.ds(pl.multiple_of((1 - c) * hr, SUBLANES), hr)

        swap = pltpu.make_async_remote_copy(
            src_ref=p_ref.at[sib_rows, :], dst_ref=theirs, send_sem=send_sems.at[0], recv_sem=recv_sems.at[0],
            device_id=sib, device_id_type=MESH)
        swap.start()
        swap.wait_recv()
        gat[k] = p_ref[my_rows, :] + theirs[...]
        sends = [swap]
        for j, (px, py) in enumerate(chips):
            cp = pltpu.make_async_remote_copy(
                src_ref=gat.at[k], dst_ref=gat.at[k], send_sem=send_sems.at[1 + j], recv_sem=recv_sems.at[1 + j],
                device_id=(px, py, c), device_id_type=MESH)
            cp.start()
            sends.append(cp)
        for j, (px, py) in enumerate(chips):
            kp = 2 * px + py
            pltpu.make_async_remote_copy(
                src_ref=gat.at[kp], dst_ref=gat.at[kp], send_sem=send_sems.at[1 + j], recv_sem=recv_sems.at[1 + j],
                device_id=(px, py, c), device_id_type=MESH).wait_recv()
        og[my_rows, :] = ((gat[0] + gat[1]) + gat[2]) + gat[3]
        last = pltpu.make_async_remote_copy(
            src_ref=og.at[my_rows, :], dst_ref=og.at[my_rows, :], send_sem=send_sems.at[4], recv_sem=recv_sems.at[4],
            device_id=sib, device_id_type=MESH)
        last.start()
        pltpu.make_async_remote_copy(
            src_ref=og.at[sib_rows, :], dst_ref=og.at[sib_rows, :], send_sem=send_sems.at[4], recv_sem=recv_sems.at[4],
            device_id=sib, device_id_type=MESH).wait_recv()
        delta, new_m, new_v = _adamw(w_ref[...], og[...], m_ref[...], v_ref[...])
        od[...] = delta
        om[...] = new_m
        ov[...] = new_v
        for cp in sends + [last]:
            cp.wait_send()

    blk = jax.ShapeDtypeStruct((r, LANES), F32)
    return pl.pallas_call(
        body, out_shape=[blk] * 4, in_specs=[VMEM_SPEC] * 4, out_specs=[VMEM_SPEC] * 4,
        scratch_shapes=[pltpu.VMEM((hr, LANES), F32), pltpu.VMEM((N_CHIPS, hr, LANES), F32),
                        pltpu.SemaphoreType.DMA((5,)), pltpu.SemaphoreType.DMA((5,))],
        compiler_params=pltpu.CompilerParams(vmem_limit_bytes=VMEM_LIMIT),
        name="reduce_replicated")(p, w, m, v)


WEIGHTS = ("g_pre", "w_in", "w_pool", "pool_scale", "a_re", "a_im", "log_dt", "b_re", "b_im", "c_re", "c_im", "d_skip",
           "w_glu", "g_mem", "w_kv", "w_out", "g_post")
SHARDED = ("w_in", "w_glu", "w_kv", "w_out")
REPLICATED = tuple(n for n in WEIGHTS if n not in SHARDED)
PACK_TILE = SUBLANES * LANES


def _pack(arrays):
    parts = []
    for a in arrays:
        flat = a.reshape(-1)
        parts.append(jnp.pad(flat, (0, -flat.shape[0] % PACK_TILE)).reshape(-1, LANES))
    rows = sum(p.shape[0] for p in parts)
    if rows % (2 * SUBLANES):
        parts.append(jnp.zeros((SUBLANES, LANES), F32))
    return jnp.concatenate(parts, axis=0)


def _unpack(packed, shapes):
    out, row = [], 0
    for shp in shapes:
        size = math.prod(shp)
        rows = -(-size // PACK_TILE) * SUBLANES
        out.append(packed[row:row + rows].reshape(-1)[:size].reshape(shp))
        row += rows
    return out


def _chip_major(a, n_cols):
    return a.reshape(a.shape[0], N_CHIPS, n_cols).transpose(1, 0, 2)


def _from_chip_major(a):
    return a.transpose(1, 0, 2).reshape(a.shape[1], N_CHIPS * a.shape[2])


def kernel(x, mem, g_pre, w_in, w_pool, pool_scale, a_re, a_im, log_dt, b_re, b_im, c_re, c_im, d_skip, w_glu, g_mem, w_kv, w_out, g_post, loss_target, m_g_pre, m_w_in, m_w_pool, m_pool_scale, m_a_re, m_a_im, m_log_dt, m_b_re, m_b_im, m_c_re, m_c_im, m_d_skip, m_w_glu, m_g_mem, m_w_kv, m_w_out, m_g_post, v_g_pre, v_w_in, v_w_pool, v_pool_scale, v_a_re, v_a_im, v_log_dt, v_b_re, v_b_im, v_c_re, v_c_im, v_d_skip, v_w_glu, v_g_mem, v_w_kv, v_w_out, v_g_post):
    given = dict(locals())
    wts = {n: given[n] for n in WEIGHTS}
    mom = {n: given["m_" + n] for n in WEIGHTS}
    var = {n: given["v_" + n] for n in WEIGHTS}

    g_in, g_glu, g_kv, g_out = gather_weights([wts[n][0] for n in SHARDED])
    w_in_full = _from_chip_major(g_in)
    w_glu_full = _from_chip_major(g_glu)
    w_kv_full = g_kv.reshape(D_MODEL, 2 * ATT_W)
    w_out_full = g_out.reshape(D_MODEL, D_MODEL)

    loss_part, grad_x, grads = local_step(
        x, mem, loss_target, g_pre, w_in_full, w_pool[0], pool_scale, a_re[0], a_im[0], log_dt[0], b_re[0], b_im[0],
        c_re[0], c_im[0], d_skip, w_glu_full, g_mem, w_kv_full, w_out_full, g_post)

    results = {}
    partial4 = {
        "w_in": grads["w_in"],
        "w_glu": _chip_major(grads["w_glu"], 2 * SSM_W // N_CHIPS),
        "w_kv": grads["w_kv"].reshape(N_CHIPS, D_MODEL // N_CHIPS, 2 * ATT_W),
        "w_out": grads["w_out"].reshape(N_CHIPS, D_MODEL // N_CHIPS, D_MODEL),
    }
    for n in SHARDED:
        outs = reduce_sharded(partial4[n], wts[n][0], mom[n][0], var[n][0], "reduce_" + n)
        results[n] = [o[None] for o in outs]

    loss_tile = jnp.broadcast_to(loss_part, (SUBLANES, LANES))
    packed = reduce_replicated(
        _pack([grads[n] for n in REPLICATED] + [loss_tile]),
        _pack([wts[n] for n in REPLICATED] + [jnp.zeros((SUBLANES, LANES), F32)]),
        _pack([mom[n] for n in REPLICATED] + [jnp.zeros((SUBLANES, LANES), F32)]),
        _pack([var[n] for n in REPLICATED] + [jnp.zeros((SUBLANES, LANES), F32)]))
    shapes = [wts[n].shape for n in REPLICATED] + [(SUBLANES, LANES)]
    unpacked = [_unpack(p, shapes) for p in packed]
    for i, n in enumerate(REPLICATED):
        results[n] = [u[i] for u in unpacked]
    loss = unpacked[0][-1][0, 0]

    return (loss, grad_x, *[results[n][0] for n in WEIGHTS], *[results[n][1] for n in WEIGHTS],
            *[results[n][2] for n in WEIGHTS], *[results[n][3] for n in WEIGHTS])
```

```python
import functools
import math

import jax
import jax.numpy as jnp
from jax import lax
from jax.experimental import pallas as pl
from jax.experimental.pallas import tpu as pltpu

F32 = jnp.float32
BF16 = jnp.bfloat16

D_MODEL = 1024
POOL_W = 384
SSM_W = 384
ATT_W = 256
POOL_GW = 96
POOL_WINDOWS = (2, 4, 8, 16)
POOL_PAD = 16
SSM_NG = 24
SSM_N = 64
SSM_GC = 16
NST = SSM_NG * SSM_N
N_MEM = 256
MEM_HEADS = 4
MEM_HD = 64
EPS = 1e-6

ADAM_LR = 0.001
ADAM_B1 = 0.9
ADAM_B2 = 0.999
ADAM_EPS = 1e-08
ADAM_WD = 0.01
ADAM_STEP = 10

SUBLANES = 8
LANES = 128
V7X_VMEM_BYTES = 64 * 2**20
VMEM_LIMIT = V7X_VMEM_BYTES - 8 * 2**20
SCAN_COLS = 512
MESH = pl.DeviceIdType.MESH

NT = (((1,), (1,)), ((), ()))
TN = (((0,), (0,)), ((), ()))


def _params(*sem):
    return pltpu.CompilerParams(dimension_semantics=sem, vmem_limit_bytes=VMEM_LIMIT)


def _dot(a, b):
    return jnp.dot(a, b, preferred_element_type=F32)


def _dot_nt(a, b):
    return lax.dot_general(a, b, NT, preferred_element_type=F32)


def _dot_tn(a, b):
    return lax.dot_general(a, b, TN, preferred_element_type=F32)


def _rms_scale(v):
    return lax.rsqrt(jnp.mean(v * v, axis=-1, keepdims=True) + EPS)


def _adamw(w, g, m, v):
    m = ADAM_B1 * m + (1.0 - ADAM_B1) * g
    v = ADAM_B2 * v + (1.0 - ADAM_B2) * (g * g)
    m_hat = m / (1.0 - ADAM_B1 ** ADAM_STEP)
    v_hat = v / (1.0 - ADAM_B2 ** ADAM_STEP)
    delta = -ADAM_LR * (m_hat / (jnp.sqrt(v_hat) + ADAM_EPS) + ADAM_WD * w)
    return delta, m, v


def in_proj(x2, g_pre, w_in):
    t = x2.shape[0]
    tm = min(512, t)

    def body(x_ref, g_ref, w_ref, o_ref):
        x = x_ref[...]
        h = (x * _rms_scale(x) * g_ref[...]).astype(BF16)
        o_ref[...] = _dot(h, w_ref[...])

    return pl.pallas_call(
        body, grid=(t // tm,),
        in_specs=[pl.BlockSpec((tm, D_MODEL), lambda i: (i, 0)),
                  pl.BlockSpec((1, D_MODEL), lambda i: (0, 0)),
                  pl.BlockSpec((D_MODEL, 2 * D_MODEL), lambda i: (0, 0))],
        out_specs=pl.BlockSpec((tm, 2 * D_MODEL), lambda i: (i, 0)),
        out_shape=jax.ShapeDtypeStruct((t, 2 * D_MODEL), F32),
        compiler_params=_params("arbitrary"), name="in_proj")(x2, g_pre, w_in)


def _pool_lane_window(shape):
    ch = lax.broadcasted_iota(jnp.int32, shape, 1)
    return jnp.where(ch < POOL_GW, 2.0, jnp.where(ch < 2 * POOL_GW, 4.0, jnp.where(ch < 3 * POOL_GW, 8.0, 16.0)))


def _pool_select(win, s2, s4, s8, s16):
    return jnp.where(win == 2.0, s2, jnp.where(win == 4.0, s4, jnp.where(win == 8.0, s8, s16)))


def _pool_counts(l, win):
    pos = (lax.broadcasted_iota(jnp.int32, (l, POOL_W), 0) + 1).astype(F32)
    return jnp.minimum(pos, win)


def _pool_diff(u, pad_ref, l):
    lo = POOL_PAD
    pad_ref[pl.ds(lo, l), :] = u
    s2 = u + pad_ref[pl.ds(lo - 1, l), :]
    pad_ref[pl.ds(lo, l), :] = s2
    s4 = s2 + pad_ref[pl.ds(lo - 2, l), :]
    pad_ref[pl.ds(lo, l), :] = s4
    s8 = s4 + pad_ref[pl.ds(lo - 4, l), :]
    pad_ref[pl.ds(lo, l), :] = s8
    s16 = s8 + pad_ref[pl.ds(lo - 8, l), :]
    win = _pool_lane_window((1, POOL_W))
    cnt = _pool_counts(l, win)
    return _pool_select(win, s2, s4, s8, s16) / cnt - u, cnt, win


def pool_fwd(proj, w_blk, pool_scale, nb, l):
    def body(u_ref, w_ref, ps_ref, y_ref, pad_ref):
        pad_ref[pl.ds(0, POOL_PAD), :] = jnp.zeros((POOL_PAD, POOL_W), F32)
        d, _, _ = _pool_diff(u_ref[...], pad_ref, l)
        y_ref[...] = _dot(d.astype(BF16), w_ref[...]) * ps_ref[...]

    return pl.pallas_call(
        body, grid=(nb,),
        in_specs=[pl.BlockSpec((l, POOL_W), lambda b: (b, 0)),
                  pl.BlockSpec((POOL_W, POOL_W), lambda b: (0, 0)),
                  pl.BlockSpec((1, POOL_W), lambda b: (0, 0))],
        out_specs=pl.BlockSpec((l, POOL_W), lambda b: (b, 0)),
        out_shape=jax.ShapeDtypeStruct((nb * l, POOL_W), F32),
        scratch_shapes=[pltpu.VMEM((POOL_PAD + l, POOL_W), F32)],
        compiler_params=_params("arbitrary"), name="pool_fwd")(proj, w_blk, pool_scale)


def pool_bwd(proj, d_ycat, w_blk, pool_scale, nb, l):
    def body(u_ref, dy_ref, w_ref, ps_ref, du_ref, dw_ref, dps_ref, pad_ref, padb_ref):
        b = pl.program_id(0)

        @pl.when(b == 0)
        def _():
            dw_ref[...] = jnp.zeros_like(dw_ref)
            dps_ref[...] = jnp.zeros_like(dps_ref)

        pad_ref[pl.ds(0, POOL_PAD), :] = jnp.zeros((POOL_PAD, POOL_W), F32)
        padb_ref[pl.ds(l, POOL_PAD), :] = jnp.zeros((POOL_PAD, POOL_W), F32)
        d, cnt, win = _pool_diff(u_ref[...], pad_ref, l)
        db = d.astype(BF16)
        w = w_ref[...]
        dy = dy_ref[...]
        dps_ref[...] += jnp.sum(dy * _dot(db, w), axis=0, keepdims=True)
        dyo = (dy * ps_ref[...]).astype(BF16)
        dw_ref[...] += _dot_tn(db, dyo)
        dd = _dot_nt(dyo, w)
        e = dd / cnt
        padb_ref[pl.ds(0, l), :] = e
        f2 = e + padb_ref[pl.ds(1, l), :]
        padb_ref[pl.ds(0, l), :] = f2
        f4 = f2 + padb_ref[pl.ds(2, l), :]
        padb_ref[pl.ds(0, l), :] = f4
        f8 = f4 + padb_ref[pl.ds(4, l), :]
        padb_ref[pl.ds(0, l), :] = f8
        f16 = f8 + padb_ref[pl.ds(8, l), :]
        du_ref[...] = _pool_select(win, f2, f4, f8, f16) - dd

    return pl.pallas_call(
        body, grid=(nb,),
        in_specs=[pl.BlockSpec((l, POOL_W), lambda b: (b, 0)),
                  pl.BlockSpec((l, POOL_W), lambda b: (b, 0)),
                  pl.BlockSpec((POOL_W, POOL_W), lambda b: (0, 0)),
                  pl.BlockSpec((1, POOL_W), lambda b: (0, 0))],
        out_specs=[pl.BlockSpec((l, POOL_W), lambda b: (b, 0)),
                   pl.BlockSpec((POOL_W, POOL_W), lambda b: (0, 0)),
                   pl.BlockSpec((1, POOL_W), lambda b: (0, 0))],
        out_shape=[jax.ShapeDtypeStruct((nb * l, POOL_W), F32),
                   jax.ShapeDtypeStruct((POOL_W, POOL_W), F32),
                   jax.ShapeDtypeStruct((1, POOL_W), F32)],
        scratch_shapes=[pltpu.VMEM((POOL_PAD + l, POOL_W), F32), pltpu.VMEM((POOL_PAD + l, POOL_W), F32)],
        compiler_params=_params("arbitrary"), name="pool_bwd")(proj, d_ycat, w_blk, pool_scale)


def _discretise(a_re, a_im, ldt, b_re, b_im):
    dt = jnp.exp(ldt)
    mag = jnp.exp(a_re * dt)
    th = a_im * dt
    lr = mag * jnp.cos(th)
    li = mag * jnp.sin(th)
    nr = lr - 1.0
    den = a_re * a_re + a_im * a_im
    fr = (nr * a_re + li * a_im) / den
    fi = (li * a_re - nr * a_im) / den
    return lr, li, fr * b_re - fi * b_im, fr * b_im + fi * b_re


def _cmul(ar, ai, br, bi):
    return ar * br - ai * bi, ar * bi + ai * br


def s5_scan_consts(a_re_row, a_im_row, ldt_row):
    def body(ar_ref, ai_ref, ld_ref, f_ref, b_ref):
        dt = jnp.exp(ld_ref[...])
        mag = jnp.exp(ar_ref[...] * dt)
        th = ai_ref[...] * dt
        lr = mag * jnp.cos(th)
        li = mag * jnp.sin(th)
        r = lax.broadcasted_iota(jnp.int32, (SUBLANES, NST), 0)
        for out_ref, sign, rev in ((f_ref, 1.0, False), (b_ref, -1.0, True)):
            p = [(lr, sign * li)]
            for _ in range(SUBLANES - 1):
                p.append(_cmul(p[-1][0], p[-1][1], lr, sign * li))
            for idx, s in enumerate((1, 2, 4)):
                inside = (r < SUBLANES - s) if rev else (r >= s)
                out_ref[pl.ds(2 * idx * SUBLANES, SUBLANES), :] = jnp.where(inside, p[s - 1][0], 0.0)
                out_ref[pl.ds((2 * idx + 1) * SUBLANES, SUBLANES), :] = jnp.where(inside, p[s - 1][1], 0.0)
            pr = jnp.zeros((SUBLANES, NST), F32)
            pi = jnp.zeros((SUBLANES, NST), F32)
            for row in range(SUBLANES):
                power = (SUBLANES - row) if rev else (row + 1)
                pr = jnp.where(r == row, p[power - 1][0], pr)
                pi = jnp.where(r == row, p[power - 1][1], pi)
            out_ref[pl.ds(6 * SUBLANES, SUBLANES), :] = pr
            out_ref[pl.ds(7 * SUBLANES, SUBLANES), :] = pi

    shape = jax.ShapeDtypeStruct((8 * SUBLANES, NST), F32)
    return pl.pallas_call(body, out_shape=[shape, shape], name="s5_scan_consts")(a_re_row, a_im_row, ldt_row)


def s5_bbar(a_re_c, a_im_c, ldt_c, b_re2, b_im2):
    def body(ar, ai, ld, br, bi, o_re, o_im):
        _, _, bbr, bbi = _discretise(ar[...], ai[...], ld[...], br[...], bi[...])
        o_re[...] = bbr
        o_im[...] = bbi

    shape = jax.ShapeDtypeStruct((NST, SSM_GC), F32)
    return pl.pallas_call(body, out_shape=[shape, shape], name="s5_bbar")(a_re_c, a_im_c, ldt_c, b_re2, b_im2)


def s5_param_bwd(a_re_c, a_im_c, ldt_c, b_re2, b_im2, d_lr, d_li, d_bbr, d_bbi):
    def body(ar, ai, ld, br, bi, g_lr, g_li, g_br, g_bi, o_ar, o_ai, o_ld, o_br, o_bi):
        _, vjp = jax.vjp(_discretise, ar[...], ai[...], ld[...], br[...], bi[...])
        d_ar, d_ai, d_ld, d_br, d_bi = vjp((g_lr[...], g_li[...], g_br[...], g_bi[...]))
        o_ar[...] = d_ar
        o_ai[...] = d_ai
        o_ld[...] = jnp.sum(d_ld.reshape(SSM_NG, SSM_N, 1), axis=1)
        o_br[...] = d_br
        o_bi[...] = d_bi

    col = jax.ShapeDtypeStruct((NST, 1), F32)
    mat = jax.ShapeDtypeStruct((NST, SSM_GC), F32)
    grp = jax.ShapeDtypeStruct((SSM_NG, 1), F32)
    return pl.pallas_call(body, out_shape=[col, col, grp, mat, mat], name="s5_param_bwd")(
        a_re_c, a_im_c, ldt_c, b_re2, b_im2, d_lr, d_li, d_bbr, d_bbi)


def _scan_tiles(buf_ref, cst_ref, carry_ref, rows, reverse, h_ref=None, acc_ref=None):
    n_tiles = rows // SUBLANES
    shifts = ((1, 0), (2, 2), (4, 4))
    row_id = lax.broadcasted_iota(jnp.int32, (SUBLANES, SCAN_COLS), 0)
    for j in range(NST // SCAN_COLS):
        c_re = pl.ds(j * SCAN_COLS, SCAN_COLS)
        c_im = pl.ds(NST + j * SCAN_COLS, SCAN_COLS)

        def step(i, carry, c_re=c_re, c_im=c_im):
            tile = (n_tiles - 1 - i) if reverse else i
            rws = pl.ds(pl.multiple_of(tile * SUBLANES, SUBLANES), SUBLANES)
            xr = buf_ref[rws, c_re]
            xi = buf_ref[rws, c_im]
            for s, base in shifts:
                amount = (SUBLANES - s) if reverse else s
                sr = pltpu.roll(xr, amount, 0)
                si = pltpu.roll(xi, amount, 0)
                mr = cst_ref[pl.ds(base * SUBLANES, SUBLANES), c_re]
                mi = cst_ref[pl.ds((base + 1) * SUBLANES, SUBLANES), c_re]
                xr, xi = xr + (mr * sr - mi * si), xi + (mr * si + mi * sr)
            pr = cst_ref[pl.ds(6 * SUBLANES, SUBLANES), c_re]
            pi = cst_ref[pl.ds(7 * SUBLANES, SUBLANES), c_re]
            cr, ci = carry[0], carry[1]
            xr, xi = xr + (pr * cr - pi * ci), xi + (pr * ci + pi * cr)
            buf_ref[rws, c_re] = xr
            buf_ref[rws, c_im] = xi
            edge = 0 if reverse else SUBLANES - 1
            ncr = jnp.broadcast_to(xr[edge:edge + 1, :], (SUBLANES, SCAN_COLS))
            nci = jnp.broadcast_to(xi[edge:edge + 1, :], (SUBLANES, SCAN_COLS))
            if not reverse:
                return ncr, nci
            gnr = jnp.where(row_id == SUBLANES - 1, cr, pltpu.roll(xr, SUBLANES - 1, 0))
            gni = jnp.where(row_id == SUBLANES - 1, ci, pltpu.roll(xi, SUBLANES - 1, 0))
            hr = h_ref[rws, c_re]
            hi = h_ref[rws, c_im]
            return ncr, nci, carry[2] + (gnr * hr + gni * hi), carry[3] + (gni * hr - gnr * hi)

        init = (carry_ref[:, c_re], carry_ref[:, c_im])
        if reverse:
            init = init + (acc_ref[:, c_re], acc_ref[:, c_im])
        out = lax.fori_loop(0, n_tiles, step, init)
        carry_ref[:, c_re] = out[0]
        carry_ref[:, c_im] = out[1]
        if reverse:
            acc_ref[:, c_re] = out[2]
            acc_ref[:, c_im] = out[3]


def s5_fwd(proj, b_blk, c_blk, cst, d_skip, w_glu, nb, l):
    tt = min(256, l)
    nt = l // tt

    def body(u_ref, b_ref, c_ref, cst_ref, ds_ref, wg_ref, y_ref, h_ref, yp_ref, z_ref, buf_ref, carry_ref):
        @pl.when(pl.program_id(1) == 0)
        def _():
            carry_ref[...] = jnp.zeros_like(carry_ref)

        u = u_ref[...]
        buf_ref[...] = _dot(u.astype(BF16), b_ref[...])
        _scan_tiles(buf_ref, cst_ref, carry_ref, tt, reverse=False)
        h = buf_ref[...].astype(BF16)
        h_ref[...] = h
        ypre = _dot(h, c_ref[...]) + ds_ref[...] * u
        yp_ref[...] = ypre
        z = _dot(jax.nn.gelu(ypre).astype(BF16), wg_ref[...])
        z_ref[...] = z
        y_ref[...] = z[:, :SSM_W] * jax.nn.sigmoid(z[:, SSM_W:])

    row = lambda b, t: (b * nt + t, 0)
    const = lambda b, t: (0, 0)
    return pl.pallas_call(
        body, grid=(nb, nt),
        in_specs=[pl.BlockSpec((tt, SSM_W), lambda b, t: (b * nt + t, 1)),
                  pl.BlockSpec((SSM_W, 2 * NST), const),
                  pl.BlockSpec((2 * NST, SSM_W), const),
                  pl.BlockSpec((8 * SUBLANES, NST), const),
                  pl.BlockSpec((1, SSM_W), const),
                  pl.BlockSpec((SSM_W, 2 * SSM_W), const)],
        out_specs=[pl.BlockSpec((tt, SSM_W), row),
                   pl.BlockSpec((tt, 2 * NST), row),
                   pl.BlockSpec((tt, SSM_W), row),
                   pl.BlockSpec((tt, 2 * SSM_W), row)],
        out_shape=[jax.ShapeDtypeStruct((nb * l, SSM_W), F32),
                   jax.ShapeDtypeStruct((nb * l, 2 * NST), BF16),
                   jax.ShapeDtypeStruct((nb * l, SSM_W), F32),
                   jax.ShapeDtypeStruct((nb * l, 2 * SSM_W), F32)],
        scratch_shapes=[pltpu.VMEM((tt, 2 * NST), F32), pltpu.VMEM((SUBLANES, 2 * NST), F32)],
        compiler_params=_params("arbitrary", "arbitrary"), name="s5_fwd")(proj, b_blk, c_blk, cst, d_skip, w_glu)


def s5_bwd(proj, d_ycat, h, ypre, z, b_blk, c_blk, cst_rev, d_skip, w_glu, nb, l):
    tt = min(256, l)
    nt = l // tt

    def body(u_ref, dy_ref, h_ref, yp_ref, z_ref, b_ref, c_ref, cst_ref, ds_ref, wg_ref,
             du_ref, dwg_ref, dds_ref, dc_ref, db_ref, dlam_ref, buf_ref, hbuf_ref, carry_ref):
        b = pl.program_id(0)
        t = pl.program_id(1)

        @pl.when((b == 0) & (t == 0))
        def _():
            dwg_ref[...] = jnp.zeros_like(dwg_ref)
            dds_ref[...] = jnp.zeros_like(dds_ref)
            dc_ref[...] = jnp.zeros_like(dc_ref)
            db_ref[...] = jnp.zeros_like(db_ref)
            dlam_ref[...] = jnp.zeros_like(dlam_ref)

        @pl.when(t == 0)
        def _():
            carry_ref[...] = jnp.zeros_like(carry_ref)

        u = u_ref[...]
        ypre = yp_ref[...]
        z = z_ref[...]
        z1 = z[:, :SSM_W]
        sg = jax.nn.sigmoid(z[:, SSM_W:])
        dy = dy_ref[...]
        dz = jnp.concatenate([dy * sg, dy * z1 * sg * (1.0 - sg)], axis=1).astype(BF16)
        yg, gelu_vjp = jax.vjp(jax.nn.gelu, ypre)
        dwg_ref[...] += _dot_tn(yg.astype(BF16), dz)
        dypre = gelu_vjp(_dot_nt(dz, wg_ref[...]))[0]
        dds_ref[...] += jnp.sum(dypre * u, axis=0, keepdims=True)
        dyb = dypre.astype(BF16)
        hb = h_ref[...]
        dc_ref[...] += _dot_tn(hb, dyb)
        hbuf_ref[...] = hb.astype(F32)
        buf_ref[...] = _dot_nt(dyb, c_ref[...])
        _scan_tiles(buf_ref, cst_ref, carry_ref, tt, reverse=True, h_ref=hbuf_ref, acc_ref=dlam_ref)
        g = buf_ref[...].astype(BF16)
        db_ref[...] += _dot_tn(g, u.astype(BF16))
        du_ref[...] = _dot_nt(g, b_ref[...]) + ds_ref[...] * dypre

        @pl.when((b == nb - 1) & (t == nt - 1))
        def _():
            dlam_ref[...] = jnp.broadcast_to(jnp.sum(dlam_ref[...], axis=0, keepdims=True), dlam_ref.shape)

    def rrow(col):
        return lambda b, t: (b * nt + (nt - 1 - t), col)

    const = lambda b, t: (0, 0)
    return pl.pallas_call(
        body, grid=(nb, nt),
        in_specs=[pl.BlockSpec((tt, SSM_W), rrow(1)),
                  pl.BlockSpec((tt, SSM_W), rrow(1)),
                  pl.BlockSpec((tt, 2 * NST), rrow(0)),
                  pl.BlockSpec((tt, SSM_W), rrow(0)),
                  pl.BlockSpec((tt, 2 * SSM_W), rrow(0)),
                  pl.BlockSpec((SSM_W, 2 * NST), const),
                  pl.BlockSpec((2 * NST, SSM_W), const),
                  pl.BlockSpec((8 * SUBLANES, NST), const),
                  pl.BlockSpec((1, SSM_W), const),
                  pl.BlockSpec((SSM_W, 2 * SSM_W), const)],
        out_specs=[pl.BlockSpec((tt, SSM_W), rrow(0)),
                   pl.BlockSpec((SSM_W, 2 * SSM_W), const),
                   pl.BlockSpec((1, SSM_W), const),
                   pl.BlockSpec((2 * NST, SSM_W), const),
                   pl.BlockSpec((2 * NST, SSM_W), const),
                   pl.BlockSpec((SUBLANES, 2 * NST), const)],
        out_shape=[jax.ShapeDtypeStruct((nb * l, SSM_W), F32),
                   jax.ShapeDtypeStruct((SSM_W, 2 * SSM_W), F32),
                   jax.ShapeDtypeStruct((1, SSM_W), F32),
                   jax.ShapeDtypeStruct((2 * NST, SSM_W), F32),
                   jax.ShapeDtypeStruct((2 * NST, SSM_W), F32),
                   jax.ShapeDtypeStruct((SUBLANES, 2 * NST), F32)],
        scratch_shapes=[pltpu.VMEM((tt, 2 * NST), F32), pltpu.VMEM((tt, 2 * NST), F32),
                        pltpu.VMEM((SUBLANES, 2 * NST), F32)],
        compiler_params=_params("arbitrary", "arbitrary"), name="s5_bwd")(
            proj, d_ycat, h, ypre, z, b_blk, c_blk, cst_rev, d_skip, w_glu)


def _head_mask(hh):
    lane = lax.broadcasted_iota(jnp.int32, (1, ATT_W), 1)
    return (lane >= hh * MEM_HD) & (lane < (hh + 1) * MEM_HD)


def _mem_kv(mem_ref, gm_ref, wkv_ref):
    m = mem_ref[0]
    mh = m * _rms_scale(m)
    mb = (mh * gm_ref[...]).astype(BF16)
    kv = _dot(mb, wkv_ref[...])
    return mh, mb, kv[:, :ATT_W].astype(BF16), kv[:, ATT_W:].astype(BF16)


def _softmax_rows(s):
    e = jnp.exp(s - jnp.max(s, axis=-1, keepdims=True))
    return e / jnp.sum(e, axis=-1, keepdims=True)


def attn_fwd(proj, mem, g_mem, w_kv, nb, l):
    tq = min(512, l)
    nq = l // tq
    scale = MEM_HD ** -0.5

    def body(q_ref, mem_ref, gm_ref, wkv_ref, o_ref, k_s, v_s):
        @pl.when(pl.program_id(1) == 0)
        def _():
            _, _, k, v = _mem_kv(mem_ref, gm_ref, wkv_ref)
            k_s[...] = k
            v_s[...] = v

        q = q_ref[...].astype(BF16)
        k = k_s[...]
        v = v_s[...]
        o = jnp.zeros((tq, ATT_W), F32)
        for hh in range(MEM_HEADS):
            msk = _head_mask(hh)
            p = _softmax_rows(_dot_nt(q, jnp.where(msk, k, 0)) * scale)
            o = o + _dot(p.astype(BF16), jnp.where(msk, v, 0))
        o_ref[...] = o

    const = lambda b, t: (0, 0)
    return pl.pallas_call(
        body, grid=(nb, nq),
        in_specs=[pl.BlockSpec((tq, ATT_W), lambda b, t: (b * nq + t, 3)),
                  pl.BlockSpec((1, N_MEM, D_MODEL), lambda b, t: (b, 0, 0)),
                  pl.BlockSpec((1, D_MODEL), const),
                  pl.BlockSpec((D_MODEL, 2 * ATT_W), const)],
        out_specs=pl.BlockSpec((tq, ATT_W), lambda b, t: (b * nq + t, 0)),
        out_shape=jax.ShapeDtypeStruct((nb * l, ATT_W), F32),
        scratch_shapes=[pltpu.VMEM((N_MEM, ATT_W), BF16), pltpu.VMEM((N_MEM, ATT_W), BF16)],
        compiler_params=_params("arbitrary", "arbitrary"), name="attn_fwd")(proj, mem, g_mem, w_kv)


def attn_bwd(proj, d_ycat, mem, g_mem, w_kv, nb, l):
    tq = min(512, l)
    nq = l // tq
    scale = MEM_HD ** -0.5

    def body(q_ref, do_ref, mem_ref, gm_ref, wkv_ref, dq_ref, dwkv_ref, dgm_ref, k_s, v_s, mb_s, dk_s, dv_s):
        b = pl.program_id(0)
        t = pl.program_id(1)

        @pl.when((b == 0) & (t == 0))
        def _():
            dwkv_ref[...] = jnp.zeros_like(dwkv_ref)
            dgm_ref[...] = jnp.zeros_like(dgm_ref)

        @pl.when(t == 0)
        def _():
            _, mb, k, v = _mem_kv(mem_ref, gm_ref, wkv_ref)
            k_s[...] = k
            v_s[...] = v
            mb_s[...] = mb
            dk_s[...] = jnp.zeros_like(dk_s)
            dv_s[...] = jnp.zeros_like(dv_s)

        q = q_ref[...].astype(BF16)
        do = do_ref[...].astype(BF16)
        k = k_s[...]
        v = v_s[...]
        dq = jnp.zeros((tq, ATT_W), F32)
        dk = jnp.zeros((N_MEM, ATT_W), F32)
        dv = jnp.zeros((N_MEM, ATT_W), F32)
        for hh in range(MEM_HEADS):
            msk = _head_mask(hh)
            kh = jnp.where(msk, k, 0)
            p = _softmax_rows(_dot_nt(q, kh) * scale)
            dp = _dot_nt(do, jnp.where(msk, v, 0))
            ds = (p * (dp - jnp.sum(dp * p, axis=-1, keepdims=True)) * scale).astype(BF16)
            dq = dq + _dot(ds, kh)
            dk = dk + jnp.where(msk, _dot_tn(ds, q), 0.0)
            dv = dv + jnp.where(msk, _dot_tn(p.astype(BF16), do), 0.0)
        dq_ref[...] = dq
        dk_s[...] += dk
        dv_s[...] += dv

        @pl.when(t == nq - 1)
        def _():
            dkv = jnp.concatenate([dk_s[...], dv_s[...]], axis=1).astype(BF16)
            dwkv_ref[...] += _dot_tn(mb_s[...], dkv)
            m = mem_ref[0]
            dgm_ref[...] += jnp.sum(_dot_nt(dkv, wkv_ref[...]) * (m * _rms_scale(m)), axis=0, keepdims=True)

    const = lambda b, t: (0, 0)
    return pl.pallas_call(
        body, grid=(nb, nq),
        in_specs=[pl.BlockSpec((tq, ATT_W), lambda b, t: (b * nq + t, 3)),
                  pl.BlockSpec((tq, ATT_W), lambda b, t: (b * nq + t, 3)),
                  pl.BlockSpec((1, N_MEM, D_MODEL), lambda b, t: (b, 0, 0)),
                  pl.BlockSpec((1, D_MODEL), const),
                  pl.BlockSpec((D_MODEL, 2 * ATT_W), const)],
        out_specs=[pl.BlockSpec((tq, ATT_W), lambda b, t: (b * nq + t, 0)),
                   pl.BlockSpec((D_MODEL, 2 * ATT_W), const),
                   pl.BlockSpec((1, D_MODEL), const)],
        out_shape=[jax.ShapeDtypeStruct((nb * l, ATT_W), F32),
                   jax.ShapeDtypeStruct((D_MODEL, 2 * ATT_W), F32),
                   jax.ShapeDtypeStruct((1, D_MODEL), F32)],
        scratch_shapes=[pltpu.VMEM((N_MEM, ATT_W), BF16), pltpu.VMEM((N_MEM, ATT_W), BF16),
                        pltpu.VMEM((N_MEM, D_MODEL), BF16),
                        pltpu.VMEM((N_MEM, ATT_W), F32), pltpu.VMEM((N_MEM, ATT_W), F32)],
        compiler_params=_params("arbitrary", "arbitrary"), name="attn_bwd")(proj, d_ycat, mem, g_mem, w_kv)


def tail(y_pool, y_ssm, y_att, proj, x2, target, w_out, g_post):
    t = x2.shape[0]
    tm = min(256, t)

    def body(yp_ref, ys_ref, ya_ref, gate_ref, x_ref, tg_ref, w_ref, gp_ref,
             dz_ref, dyc_ref, dgate_ref, dw_ref, dgp_ref, loss_ref):
        @pl.when(pl.program_id(0) == 0)
        def _():
            dw_ref[...] = jnp.zeros_like(dw_ref)
            dgp_ref[...] = jnp.zeros_like(dgp_ref)
            loss_ref[...] = jnp.zeros_like(loss_ref)

        ycat = jnp.concatenate([yp_ref[...], ys_ref[...], ya_ref[...]], axis=1)
        gate = gate_ref[...]
        sg = jax.nn.sigmoid(gate)
        silu = gate * sg
        yb = (ycat * silu).astype(BF16)
        w = w_ref[...]
        out = _dot(yb, w)
        r2 = _rms_scale(out)
        oh = out * r2
        gp = gp_ref[...]
        err = (x_ref[...] + oh * gp) - tg_ref[...]
        loss_ref[...] += 0.5 * jnp.sum(jnp.mean(err * err, axis=-1, keepdims=True), axis=0, keepdims=True)
        dz = err * (1.0 / D_MODEL)
        dz_ref[...] = dz
        dgp_ref[...] += jnp.sum(dz * oh, axis=0, keepdims=True)
        dn = dz * gp
        dout = (r2 * (dn - oh * jnp.mean(dn * oh, axis=-1, keepdims=True))).astype(BF16)
        dw_ref[...] += _dot_tn(yb, dout)
        dy = _dot_nt(dout, w)
        dyc_ref[...] = dy * silu
        dgate_ref[...] = dy * ycat * (sg * (1.0 + gate * (1.0 - sg)))

    row = lambda i: (i, 0)
    const = lambda i: (0, 0)
    full = jax.ShapeDtypeStruct((t, D_MODEL), F32)
    return pl.pallas_call(
        body, grid=(t // tm,),
        in_specs=[pl.BlockSpec((tm, POOL_W), row),
                  pl.BlockSpec((tm, SSM_W), row),
                  pl.BlockSpec((tm, ATT_W), row),
                  pl.BlockSpec((tm, D_MODEL), lambda i: (i, 1)),
                  pl.BlockSpec((tm, D_MODEL), row),
                  pl.BlockSpec((tm, D_MODEL), row),
                  pl.BlockSpec((D_MODEL, D_MODEL), const),
                  pl.BlockSpec((1, D_MODEL), const)],
        out_specs=[pl.BlockSpec((tm, D_MODEL), row),
                   pl.BlockSpec((tm, D_MODEL), row),
                   pl.BlockSpec((tm, D_MODEL), row),
                   pl.BlockSpec((D_MODEL, D_MODEL), const),
                   pl.BlockSpec((1, D_MODEL), const),
                   pl.BlockSpec((1, LANES), const)],
        out_shape=[full, full, full,
                   jax.ShapeDtypeStruct((D_MODEL, D_MODEL), F32),
                   jax.ShapeDtypeStruct((1, D_MODEL), F32),
                   jax.ShapeDtypeStruct((1, LANES), F32)],
        compiler_params=_params("arbitrary"), name="tail")(y_pool, y_ssm, y_att, proj, x2, target, w_out, g_post)


def in_proj_bwd(du_pool, du_ssm, dq, d_gate, x2, dz, g_pre, w_in):
    t = x2.shape[0]
    tm = min(256, t)
    shard = 2 * D_MODEL // 4

    def body(dup_ref, dus_ref, dq_ref, dg_ref, x_ref, dz_ref, g_ref, w_ref, gx_ref, dw_ref, dgp_ref):
        @pl.when(pl.program_id(0) == 0)
        def _():
            dw_ref[...] = jnp.zeros_like(dw_ref)
            dgp_ref[...] = jnp.zeros_like(dgp_ref)

        dproj = jnp.concatenate([dup_ref[...], dus_ref[...], dq_ref[...], dg_ref[...]], axis=1).astype(BF16)
        x = x_ref[...]
        r1 = _rms_scale(x)
        xh = x * r1
        g = g_ref[...]
        hb = (xh * g).astype(BF16)
        for k in range(4):
            dw_ref[k] += _dot_tn(hb, dproj[:, k * shard:(k + 1) * shard])
        dh = _dot_nt(dproj, w_ref[...])
        dgp_ref[...] += jnp.sum(dh * xh, axis=0, keepdims=True)
        dn = dh * g
        gx_ref[...] = dz_ref[...] + r1 * (dn - xh * jnp.mean(dn * xh, axis=-1, keepdims=True))

    row = lambda i: (i, 0)
    const = lambda i: (0, 0)
    return pl.pallas_call(
        body, grid=(t // tm,),
        in_specs=[pl.BlockSpec((tm, POOL_W), row),
                  pl.BlockSpec((tm, SSM_W), row),
                  pl.BlockSpec((tm, ATT_W), row),
                  pl.BlockSpec((tm, D_MODEL), row),
                  pl.BlockSpec((tm, D_MODEL), row),
                  pl.BlockSpec((tm, D_MODEL), row),
                  pl.BlockSpec((1, D_MODEL), const),
                  pl.BlockSpec((D_MODEL, 2 * D_MODEL), const)],
        out_specs=[pl.BlockSpec((tm, D_MODEL), row),
                   pl.BlockSpec((4, D_MODEL, shard), lambda i: (0, 0, 0)),
                   pl.BlockSpec((1, D_MODEL), const)],
        out_shape=[jax.ShapeDtypeStruct((t, D_MODEL), F32),
                   jax.ShapeDtypeStruct((4, D_MODEL, shard), F32),
                   jax.ShapeDtypeStruct((1, D_MODEL), F32)],
        compiler_params=_params("arbitrary"), name="in_proj_bwd")(du_pool, du_ssm, dq, d_gate, x2, dz, g_pre, w_in)


def _block_diag(blocks):
    g, r, c = blocks.shape
    eye = jnp.eye(g, dtype=blocks.dtype)
    return jnp.einsum("grc,gh->grhc", blocks, eye).reshape(g * r, g * c)


def _block_diag_extract(mat, g, r, c):
    eye = jnp.eye(g, dtype=mat.dtype)
    return jnp.einsum("grhc,gh->grc", mat.reshape(g, r, g, c), eye)


def local_step(x, mem, target, g_pre, w_in, w_pool, pool_scale, a_re, a_im, log_dt, b_re, b_im, c_re, c_im, d_skip,
               w_glu, g_mem, w_kv, w_out, g_post):
    nb, l, _ = x.shape
    x2 = x.reshape(nb * l, D_MODEL)
    tg2 = target.reshape(nb * l, D_MODEL)

    rowv = lambda a: a.reshape(1, NST)
    colv = lambda a: a.reshape(NST, 1)
    ldt_full = jnp.broadcast_to(log_dt.reshape(SSM_NG, 1), (SSM_NG, SSM_N))
    b_re2 = b_re.reshape(NST, SSM_GC)
    b_im2 = b_im.reshape(NST, SSM_GC)
    cst_f, cst_b = s5_scan_consts(rowv(a_re), rowv(a_im), rowv(ldt_full))
    bbr, bbi = s5_bbar(colv(a_re), colv(a_im), colv(ldt_full), b_re2, b_im2)
    to_b_blk = lambda bb: _block_diag(bb.reshape(SSM_NG, SSM_N, SSM_GC).transpose(0, 2, 1))
    b_blk = jnp.concatenate([to_b_blk(bbr), to_b_blk(bbi)], axis=1).astype(BF16)
    to_c_blk = lambda cc: _block_diag(cc.reshape(SSM_NG, SSM_GC, SSM_N).transpose(0, 2, 1))
    c_blk = jnp.concatenate([to_c_blk(c_re), -to_c_blk(c_im)], axis=0).astype(BF16)
    w_pool_blk = _block_diag(w_pool.reshape(4, POOL_GW, POOL_GW)).astype(BF16)

    proj = in_proj(x2, g_pre, w_in)
    y_pool = pool_fwd(proj, w_pool_blk, pool_scale, nb, l)
    y_ssm, h, ypre, z = s5_fwd(proj, b_blk, c_blk, cst_f, d_skip, w_glu, nb, l)
    y_att = attn_fwd(proj, mem, g_mem, w_kv, nb, l)
    dz, d_ycat, d_gate, d_w_out, d_g_post, loss = tail(y_pool, y_ssm, y_att, proj, x2, tg2, w_out, g_post)
    du_pool, d_wp_blk, d_pool_scale = pool_bwd(proj, d_ycat, w_pool_blk, pool_scale, nb, l)
    du_ssm, d_w_glu, d_d_skip, d_c_full, d_b_full, d_lam = s5_bwd(
        proj, d_ycat, h, ypre, z, b_blk, c_blk, cst_b, d_skip, w_glu, nb, l)
    dq, d_w_kv, d_g_mem = attn_bwd(proj, d_ycat, mem, g_mem, w_kv, nb, l)
    grad_x, d_w_in, d_g_pre = in_proj_bwd(du_pool, du_ssm, dq, d_gate, x2, dz, g_pre, w_in)

    d_c = _block_diag_extract(d_c_full.reshape(2 * SSM_NG * SSM_N, SSM_W).reshape(2, NST, SSM_W)[0], SSM_NG, SSM_N, SSM_GC)
    d_ci = _block_diag_extract(d_c_full.reshape(2, NST, SSM_W)[1], SSM_NG, SSM_N, SSM_GC)
    d_bb = d_b_full.reshape(2, NST, SSM_W)
    d_bbr = _block_diag_extract(d_bb[0], SSM_NG, SSM_N, SSM_GC).reshape(NST, SSM_GC)
    d_bbi = _block_diag_extract(d_bb[1], SSM_NG, SSM_N, SSM_GC).reshape(NST, SSM_GC)
    d_lam_row = d_lam[0]
    d_ar, d_ai, d_ld, d_br, d_bi = s5_param_bwd(
        colv(a_re), colv(a_im), colv(ldt_full), b_re2, b_im2,
        d_lam_row[:NST].reshape(NST, 1), d_lam_row[NST:].reshape(NST, 1), d_bbr, d_bbi)

    grads = {
        "g_pre": d_g_pre,
        "w_in": d_w_in,
        "w_pool": _block_diag_extract(d_wp_blk, 4, POOL_GW, POOL_GW).reshape(1, 4, POOL_GW, POOL_GW),
        "pool_scale": d_pool_scale,
        "a_re": d_ar.reshape(1, SSM_NG, SSM_N),
        "a_im": d_ai.reshape(1, SSM_NG, SSM_N),
        "log_dt": d_ld.reshape(1, SSM_NG),
        "b_re": d_br.reshape(1, SSM_NG, SSM_N, SSM_GC),
        "b_im": d_bi.reshape(1, SSM_NG, SSM_N, SSM_GC),
        "c_re": d_c.transpose(0, 2, 1).reshape(1, SSM_NG, SSM_GC, SSM_N),
        "c_im": (-d_ci).transpose(0, 2, 1).reshape(1, SSM_NG, SSM_GC, SSM_N),
        "d_skip": d_d_skip,
        "w_glu": d_w_glu,
        "g_mem": d_g_mem,
        "w_kv": d_w_kv,
        "w_out": d_w_out,
        "g_post": d_g_post,
    }
    return loss, grad_x.reshape(nb, l, D_MODEL), grads


VMEM_SPEC = pl.BlockSpec(memory_space=pltpu.VMEM)
ANY_SPEC = pl.BlockSpec(memory_space=pl.ANY)
N_CHIPS = 4
N_DEV = 8


def _place():
    return lax.axis_index("x"), lax.axis_index("y"), lax.axis_index("c")


def _other_chips(x, y):
    return [(1 - x, y), (x, 1 - y), (1 - x, 1 - y)]


def gather_weights(shards):
    n = len(shards)

    def body(*refs):
        ins, outs = refs[:n], refs[n:2 * n]
        send_sems, recv_sems = refs[2 * n:]
        x, y, c = _place()
        k = 2 * x + y
        chips = _other_chips(x, y)
        for i in range(n):
            outs[i][k] = ins[i][...].astype(BF16)

        def copy(i, block, core_half, to, sem):
            hr = shards[i].shape[0] // 2
            rows = outs[i].at[block, pl.ds(pl.multiple_of(core_half * hr, 2 * SUBLANES), hr)]
            return pltpu.make_async_remote_copy(
                src_ref=rows, dst_ref=rows, send_sem=send_sems.at[sem], recv_sem=recv_sems.at[sem],
                device_id=to, device_id_type=MESH)

        pairs = [(i, j) for i in range(n) for j in range(3)]
        first = [copy(i, k, c, (*chips[j], c), i * 3 + j) for i, j in pairs]
        for cp in first:
            cp.start()
        passed = []
        for i, j in pairs:
            px, py = chips[j]
            copy(i, 2 * px + py, c, (px, py, c), i * 3 + j).wait_recv()
            fw = copy(i, 2 * px + py, c, (x, y, 1 - c), 3 * n + i * 3 + j)
            fw.start()
            passed.append(fw)
        for i, j in pairs:
            px, py = chips[j]
            copy(i, 2 * px + py, 1 - c, (x, y, 1 - c), 3 * n + i * 3 + j).wait_recv()
        for cp in first + passed:
            cp.wait_send()

    return pl.pallas_call(
        body,
        out_shape=[jax.ShapeDtypeStruct((N_CHIPS,) + s.shape, BF16) for s in shards],
        in_specs=[VMEM_SPEC] * n, out_specs=[VMEM_SPEC] * n,
        scratch_shapes=[pltpu.SemaphoreType.DMA((6 * n,)), pltpu.SemaphoreType.DMA((6 * n,))],
        compiler_params=pltpu.CompilerParams(vmem_limit_bytes=VMEM_LIMIT),
        name="gather_weights")(*shards)


SEMS_PER_ITEM = 5


def reduce_all(g4s, packed):
    n = len(g4s)
    dims = [g.shape[1:] for g in g4s]
    pr = packed.shape[0]

    def body(*refs):
        g_refs, p_ref = refs[:n], refs[n]
        outs, op = refs[n + 1:2 * n + 1], refs[2 * n + 1]
        scr = refs[2 * n + 2:]
        mine, theirs, sendb, recvb = scr[0:n], scr[n:2 * n], scr[2 * n:3 * n], scr[3 * n:4 * n]
        p_theirs, gat, lsems, send_sems, recv_sems = scr[4 * n:]
        x, y, c = _place()
        k = 2 * x + y
        chips = _other_chips(x, y)
        sib = (x, y, 1 - c)

        def halves(rows):
            hr = rows // 2
            return (pl.ds(pl.multiple_of(c * hr, SUBLANES), hr), pl.ds(pl.multiple_of((1 - c) * hr, SUBLANES), hr))

        def remote(src, dst, sem, to):
            return pltpu.make_async_remote_copy(
                src_ref=src, dst_ref=dst, send_sem=send_sems.at[sem], recv_sem=recv_sems.at[sem],
                device_id=to, device_id_type=MESH)

        loads, started = [], []
        for i in range(n):
            my_rows, sib_rows = halves(dims[i][0])
            ld = pltpu.make_async_copy(g_refs[i].at[:, my_rows, :], mine[i], lsems.at[i])
            ld.start()
            loads.append(ld)
            sw = remote(g_refs[i].at[:, sib_rows, :], theirs[i], SEMS_PER_ITEM * i, sib)
            sw.start()
            started.append(sw)
        p_my, p_sib = halves(pr)
        p_swap = remote(p_ref.at[p_sib, :], p_theirs, SEMS_PER_ITEM * n, sib)
        p_swap.start()
        started.append(p_swap)

        p_swap.wait_recv()
        gat[k] = p_ref[p_my, :] + p_theirs[...]
        for j, (px, py) in enumerate(chips):
            cp = remote(gat.at[k], gat.at[k], SEMS_PER_ITEM * n + 1 + j, (px, py, c))
            cp.start()
            started.append(cp)
        ici = []
        for i in range(n):
            loads[i].wait()
            started[i].wait_recv()
            for j, (px, py) in enumerate(chips):
                kp = 2 * px + py
                sendb[i][j] = (mine[i][kp] + theirs[i][kp]).astype(BF16)
                cp = remote(sendb[i].at[j], recvb[i].at[j], SEMS_PER_ITEM * i + 1 + j, (px, py, c))
                cp.start()
                ici.append(cp)

        for j, (px, py) in enumerate(chips):
            kp = 2 * px + py
            remote(gat.at[kp], gat.at[kp], SEMS_PER_ITEM * n + 1 + j, (px, py, c)).wait_recv()
        op[p_my, :] = ((gat[0] + gat[1]) + gat[2]) + gat[3]
        last = remote(op.at[p_my, :], op.at[p_my, :], SEMS_PER_ITEM * n + 4, sib)
        last.start()
        started.append(last)
        for i in range(n):
            my_rows, _ = halves(dims[i][0])
            half = mine[i][k] + theirs[i][k]
            for j in range(3):
                ici[3 * i + j].wait_recv()
                half = half + recvb[i][j].astype(F32)
            outs[i][my_rows, :] = half
            last = remote(outs[i].at[my_rows, :], outs[i].at[my_rows, :], SEMS_PER_ITEM * i + 4, sib)
            last.start()
            started.append(last)
        remote(op.at[p_sib, :], op.at[p_sib, :], SEMS_PER_ITEM * n + 4, sib).wait_recv()
        for i in range(n):
            _, sib_rows = halves(dims[i][0])
            remote(outs[i].at[sib_rows, :], outs[i].at[sib_rows, :], SEMS_PER_ITEM * i + 4, sib).wait_recv()
        for cp in started + ici:
            cp.wait_send()

    n_sems = SEMS_PER_ITEM * (n + 1)
    scratch = ([pltpu.VMEM((N_CHIPS, r // 2, cd), F32) for r, cd in dims] * 2
               + [pltpu.VMEM((3, r // 2, cd), BF16) for r, cd in dims] * 2
               + [pltpu.VMEM((pr // 2, LANES), F32), pltpu.VMEM((N_CHIPS, pr // 2, LANES), F32),
                  pltpu.SemaphoreType.DMA((n,)), pltpu.SemaphoreType.DMA((n_sems,)), pltpu.SemaphoreType.DMA((n_sems,))])
    res = pl.pallas_call(
        body,
        out_shape=[jax.ShapeDtypeStruct(d, F32) for d in dims] + [jax.ShapeDtypeStruct(packed.shape, F32)],
        in_specs=[ANY_SPEC] * n + [VMEM_SPEC], out_specs=[VMEM_SPEC] * (n + 1),
        scratch_shapes=scratch,
        compiler_params=pltpu.CompilerParams(vmem_limit_bytes=VMEM_LIMIT),
        name="reduce_all")(*g4s, packed)
    return res[:n], res[n]


def adamw_all(ws, gs, ms, vs):
    n = len(ws)

    def body(*refs):
        w, g, m, v = refs[:n], refs[n:2 * n], refs[2 * n:3 * n], refs[3 * n:4 * n]
        od, om, ov = refs[4 * n:5 * n], refs[5 * n:6 * n], refs[6 * n:]
        for i in range(n):
            od[i][...], om[i][...], ov[i][...] = _adamw(w[i][...], g[i][...], m[i][...], v[i][...])

    shapes = [jax.ShapeDtypeStruct(a.shape, F32) for a in ws]
    res = pl.pallas_call(
        body, out_shape=shapes * 3, in_specs=[VMEM_SPEC] * (4 * n), out_specs=[VMEM_SPEC] * (3 * n),
        compiler_params=pltpu.CompilerParams(vmem_limit_bytes=VMEM_LIMIT),
        name="adamw_all")(*ws, *gs, *ms, *vs)
    return res[:n], res[n:2 * n], res[2 * n:]


WEIGHTS = ("g_pre", "w_in", "w_pool", "pool_scale", "a_re", "a_im", "log_dt", "b_re", "b_im", "c_re", "c_im", "d_skip",
           "w_glu", "g_mem", "w_kv", "w_out", "g_post")
SHARDED = ("w_in", "w_glu", "w_kv", "w_out")
REPLICATED = tuple(n for n in WEIGHTS if n not in SHARDED)
PACK_TILE = SUBLANES * LANES


def _pack(arrays):
    parts = []
    for a in arrays:
        flat = a.reshape(-1)
        parts.append(jnp.pad(flat, (0, -flat.shape[0] % PACK_TILE)).reshape(-1, LANES))
    rows = sum(p.shape[0] for p in parts)
    if rows % (2 * SUBLANES):
        parts.append(jnp.zeros((SUBLANES, LANES), F32))
    return jnp.concatenate(parts, axis=0)


def _unpack(packed, shapes):
    out, row = [], 0
    for shp in shapes:
        size = math.prod(shp)
        rows = -(-size // PACK_TILE) * SUBLANES
        out.append(packed[row:row + rows].reshape(-1)[:size].reshape(shp))
        row += rows
    return out


def _chip_major(a, n_cols):
    return a.reshape(a.shape[0], N_CHIPS, n_cols).transpose(1, 0, 2)


def _from_chip_major(a):
    return a.transpose(1, 0, 2).reshape(a.shape[1], N_CHIPS * a.shape[2])


def kernel(x, mem, g_pre, w_in, w_pool, pool_scale, a_re, a_im, log_dt, b_re, b_im, c_re, c_im, d_skip, w_glu, g_mem, w_kv, w_out, g_post, loss_target, m_g_pre, m_w_in, m_w_pool, m_pool_scale, m_a_re, m_a_im, m_log_dt, m_b_re, m_b_im, m_c_re, m_c_im, m_d_skip, m_w_glu, m_g_mem, m_w_kv, m_w_out, m_g_post, v_g_pre, v_w_in, v_w_pool, v_pool_scale, v_a_re, v_a_im, v_log_dt, v_b_re, v_b_im, v_c_re, v_c_im, v_d_skip, v_w_glu, v_g_mem, v_w_kv, v_w_out, v_g_post):
    given = dict(locals())
    wts = {n: given[n] for n in WEIGHTS}
    mom = {n: given["m_" + n] for n in WEIGHTS}
    var = {n: given["v_" + n] for n in WEIGHTS}

    g_in, g_glu, g_kv, g_out = gather_weights([wts[n][0] for n in SHARDED])
    w_in_full = _from_chip_major(g_in)
    w_glu_full = _from_chip_major(g_glu)
    w_kv_full = g_kv.reshape(D_MODEL, 2 * ATT_W)
    w_out_full = g_out.reshape(D_MODEL, D_MODEL)

    loss_part, grad_x, grads = local_step(
        x, mem, loss_target, g_pre, w_in_full, w_pool[0], pool_scale, a_re[0], a_im[0], log_dt[0], b_re[0], b_im[0],
        c_re[0], c_im[0], d_skip, w_glu_full, g_mem, w_kv_full, w_out_full, g_post)

    partial4 = [
        grads["w_in"],
        _chip_major(grads["w_glu"], 2 * SSM_W // N_CHIPS),
        grads["w_kv"].reshape(N_CHIPS, D_MODEL // N_CHIPS, 2 * ATT_W),
        grads["w_out"].reshape(N_CHIPS, D_MODEL // N_CHIPS, D_MODEL),
    ]
    loss_tile = jnp.broadcast_to(loss_part, (SUBLANES, LANES))
    sharded_g, packed_g = reduce_all(partial4, _pack([grads[n] for n in REPLICATED] + [loss_tile]))
    small_g = _unpack(packed_g, [wts[n].shape for n in REPLICATED] + [(SUBLANES, LANES)])
    loss = small_g[-1][0, 0]
    grad = dict(zip(SHARDED, [g[None] for g in sharded_g]))
    grad.update(zip(REPLICATED, small_g[:-1]))

    two_d = lambda a: a.reshape(-1, a.shape[-1])
    deltas, new_ms, new_vs = adamw_all([two_d(wts[n]) for n in WEIGHTS], [two_d(grad[n]) for n in WEIGHTS],
                                       [two_d(mom[n]) for n in WEIGHTS], [two_d(var[n]) for n in WEIGHTS])
    back = lambda arrs: [a.reshape(wts[n].shape) for a, n in zip(arrs, WEIGHTS)]
    return (loss, grad_x, *[grad[n] for n in WEIGHTS], *back(deltas), *back(new_ms), *back(new_vs))
```

```python
import functools
import math

import jax
import jax.numpy as jnp
from jax import lax
from jax.experimental import pallas as pl
from jax.experimental.pallas import tpu as pltpu

F32 = jnp.float32
BF16 = jnp.bfloat16

D_MODEL = 1024
POOL_W = 384
SSM_W = 384
ATT_W = 256
POOL_GW = 96
POOL_WINDOWS = (2, 4, 8, 16)
POOL_PAD = 16
SSM_NG = 24
SSM_N = 64
SSM_GC = 16
N_CHIPS = 4
N_DEV = 8
W_IN_SHARD = 2 * D_MODEL // N_CHIPS
NST = SSM_NG * SSM_N
BLK_CH = 128
BLK_GROUPS = BLK_CH // SSM_GC
BLK_ST = BLK_GROUPS * SSM_N
N_BLK = SSM_W // BLK_CH
N_MEM = 256
MEM_HEADS = 4
MEM_HD = 64
EPS = 1e-6

ADAM_LR = 0.001
ADAM_B1 = 0.9
ADAM_B2 = 0.999
ADAM_EPS = 1e-08
ADAM_WD = 0.01
ADAM_STEP = 10

SUBLANES = 8
LANES = 128
V7X_VMEM_BYTES = 64 * 2**20
VMEM_LIMIT = V7X_VMEM_BYTES - 8 * 2**20
SCAN_COLS = 512
MESH = pl.DeviceIdType.MESH

NT = (((1,), (1,)), ((), ()))
TN = (((0,), (0,)), ((), ()))


def _params(*sem):
    return pltpu.CompilerParams(dimension_semantics=sem, vmem_limit_bytes=VMEM_LIMIT)


def _dot(a, b):
    return jnp.dot(a, b, preferred_element_type=F32)


def _dot_nt(a, b):
    return lax.dot_general(a, b, NT, preferred_element_type=F32)


def _dot_tn(a, b):
    return lax.dot_general(a, b, TN, preferred_element_type=F32)


def _rms_scale(v):
    return lax.rsqrt(jnp.mean(v * v, axis=-1, keepdims=True) + EPS)


def _adamw(w, g, m, v):
    m = ADAM_B1 * m + (1.0 - ADAM_B1) * g
    v = ADAM_B2 * v + (1.0 - ADAM_B2) * (g * g)
    m_hat = m / (1.0 - ADAM_B1 ** ADAM_STEP)
    v_hat = v / (1.0 - ADAM_B2 ** ADAM_STEP)
    delta = -ADAM_LR * (m_hat / (jnp.sqrt(v_hat) + ADAM_EPS) + ADAM_WD * w)
    return delta, m, v


def in_proj(x2, g_pre, w_in):
    t = x2.shape[0]
    tm = min(512, t)

    def body(x_ref, g_ref, w_ref, o_ref):
        x = x_ref[...]
        h = (x * _rms_scale(x) * g_ref[...]).astype(BF16)
        for k in range(N_CHIPS):
            o_ref[:, pl.ds(k * W_IN_SHARD, W_IN_SHARD)] = _dot(h, w_ref[k])

    return pl.pallas_call(
        body, grid=(t // tm,),
        in_specs=[pl.BlockSpec((tm, D_MODEL), lambda i: (i, 0)),
                  pl.BlockSpec((1, D_MODEL), lambda i: (0, 0)),
                  pl.BlockSpec((N_CHIPS, D_MODEL, W_IN_SHARD), lambda i: (0, 0, 0))],
        out_specs=pl.BlockSpec((tm, 2 * D_MODEL), lambda i: (i, 0)),
        out_shape=jax.ShapeDtypeStruct((t, 2 * D_MODEL), F32),
        compiler_params=_params("arbitrary"), name="in_proj")(x2, g_pre, w_in)


def _pool_lane_window(shape):
    ch = lax.broadcasted_iota(jnp.int32, shape, 1)
    return jnp.where(ch < POOL_GW, 2.0, jnp.where(ch < 2 * POOL_GW, 4.0, jnp.where(ch < 3 * POOL_GW, 8.0, 16.0)))


def _pool_select(win, s2, s4, s8, s16):
    return jnp.where(win == 2.0, s2, jnp.where(win == 4.0, s4, jnp.where(win == 8.0, s8, s16)))


def _pool_div_count(v, win):
    pos = (lax.broadcasted_iota(jnp.int32, (POOL_PAD, POOL_W), 0) + 1).astype(F32)
    head = v[:POOL_PAD] / jnp.minimum(pos, win)
    return jnp.concatenate([head, v[POOL_PAD:] * (1.0 / win)], axis=0)


def _pool_diff(u, pad_ref, l):
    lo = POOL_PAD
    pad_ref[pl.ds(lo, l), :] = u
    s2 = u + pad_ref[pl.ds(lo - 1, l), :]
    pad_ref[pl.ds(lo, l), :] = s2
    s4 = s2 + pad_ref[pl.ds(lo - 2, l), :]
    pad_ref[pl.ds(lo, l), :] = s4
    s8 = s4 + pad_ref[pl.ds(lo - 4, l), :]
    pad_ref[pl.ds(lo, l), :] = s8
    s16 = s8 + pad_ref[pl.ds(lo - 8, l), :]
    win = _pool_lane_window((1, POOL_W))
    return _pool_div_count(_pool_select(win, s2, s4, s8, s16), win) - u, win


def pool_fwd(proj, w_blk, pool_scale, nb, l):
    def body(u_ref, w_ref, ps_ref, y_ref, pad_ref):
        pad_ref[pl.ds(0, POOL_PAD), :] = jnp.zeros((POOL_PAD, POOL_W), F32)
        d, _ = _pool_diff(u_ref[...], pad_ref, l)
        y_ref[...] = _dot(d.astype(BF16), w_ref[...]) * ps_ref[...]

    return pl.pallas_call(
        body, grid=(nb,),
        in_specs=[pl.BlockSpec((l, POOL_W), lambda b: (b, 0)),
                  pl.BlockSpec((POOL_W, POOL_W), lambda b: (0, 0)),
                  pl.BlockSpec((1, POOL_W), lambda b: (0, 0))],
        out_specs=pl.BlockSpec((l, POOL_W), lambda b: (b, 0)),
        out_shape=jax.ShapeDtypeStruct((nb * l, POOL_W), F32),
        scratch_shapes=[pltpu.VMEM((POOL_PAD + l, POOL_W), F32)],
        compiler_params=_params("arbitrary"), name="pool_fwd")(proj, w_blk, pool_scale)


def pool_bwd(proj, d_ycat, w_blk, pool_scale, nb, l):
    def body(u_ref, dy_ref, w_ref, ps_ref, du_ref, dw_ref, dps_ref, pad_ref, padb_ref):
        b = pl.program_id(0)

        @pl.when(b == 0)
        def _():
            dw_ref[...] = jnp.zeros_like(dw_ref)
            dps_ref[...] = jnp.zeros_like(dps_ref)

        pad_ref[pl.ds(0, POOL_PAD), :] = jnp.zeros((POOL_PAD, POOL_W), F32)
        padb_ref[pl.ds(l, POOL_PAD), :] = jnp.zeros((POOL_PAD, POOL_W), F32)
        d, win = _pool_diff(u_ref[...], pad_ref, l)
        db = d.astype(BF16)
        w = w_ref[...]
        dy = dy_ref[...]
        dps_ref[...] += jnp.sum(dy * _dot(db, w), axis=0, keepdims=True)
        dyo = (dy * ps_ref[...]).astype(BF16)
        dw_ref[...] += _dot_tn(db, dyo)
        dd = _dot_nt(dyo, w)
        e = _pool_div_count(dd, win)
        padb_ref[pl.ds(0, l), :] = e
        f2 = e + padb_ref[pl.ds(1, l), :]
        padb_ref[pl.ds(0, l), :] = f2
        f4 = f2 + padb_ref[pl.ds(2, l), :]
        padb_ref[pl.ds(0, l), :] = f4
        f8 = f4 + padb_ref[pl.ds(4, l), :]
        padb_ref[pl.ds(0, l), :] = f8
        f16 = f8 + padb_ref[pl.ds(8, l), :]
        du_ref[...] = _pool_select(win, f2, f4, f8, f16) - dd

    return pl.pallas_call(
        body, grid=(nb,),
        in_specs=[pl.BlockSpec((l, POOL_W), lambda b: (b, 0)),
                  pl.BlockSpec((l, POOL_W), lambda b: (b, 0)),
                  pl.BlockSpec((POOL_W, POOL_W), lambda b: (0, 0)),
                  pl.BlockSpec((1, POOL_W), lambda b: (0, 0))],
        out_specs=[pl.BlockSpec((l, POOL_W), lambda b: (b, 0)),
                   pl.BlockSpec((POOL_W, POOL_W), lambda b: (0, 0)),
                   pl.BlockSpec((1, POOL_W), lambda b: (0, 0))],
        out_shape=[jax.ShapeDtypeStruct((nb * l, POOL_W), F32),
                   jax.ShapeDtypeStruct((POOL_W, POOL_W), F32),
                   jax.ShapeDtypeStruct((1, POOL_W), F32)],
        scratch_shapes=[pltpu.VMEM((POOL_PAD + l, POOL_W), F32), pltpu.VMEM((POOL_PAD + l, POOL_W), F32)],
        compiler_params=_params("arbitrary"), name="pool_bwd")(proj, d_ycat, w_blk, pool_scale)


def _discretise(a_re, a_im, ldt, b_re, b_im):
    dt = jnp.exp(ldt)
    mag = jnp.exp(a_re * dt)
    th = a_im * dt
    lr = mag * jnp.cos(th)
    li = mag * jnp.sin(th)
    nr = lr - 1.0
    den = a_re * a_re + a_im * a_im
    fr = (nr * a_re + li * a_im) / den
    fi = (li * a_re - nr * a_im) / den
    return lr, li, fr * b_re - fi * b_im, fr * b_im + fi * b_re


def _cmul(ar, ai, br, bi):
    return ar * br - ai * bi, ar * bi + ai * br


def s5_scan_consts(a_re_row, a_im_row, ldt_row):
    def body(ar_ref, ai_ref, ld_ref, f_ref, b_ref):
        dt = jnp.exp(ld_ref[...])
        mag = jnp.exp(ar_ref[...] * dt)
        th = ai_ref[...] * dt
        lr = mag * jnp.cos(th)
        li = mag * jnp.sin(th)
        r = lax.broadcasted_iota(jnp.int32, (SUBLANES, NST), 0)
        for out_ref, sign, rev in ((f_ref, 1.0, False), (b_ref, -1.0, True)):
            p = [(lr, sign * li)]
            for _ in range(SUBLANES - 1):
                p.append(_cmul(p[-1][0], p[-1][1], lr, sign * li))
            for idx, s in enumerate((1, 2, 4)):
                inside = (r < SUBLANES - s) if rev else (r >= s)
                out_ref[pl.ds(2 * idx * SUBLANES, SUBLANES), :] = jnp.where(inside, p[s - 1][0], 0.0)
                out_ref[pl.ds((2 * idx + 1) * SUBLANES, SUBLANES), :] = jnp.where(inside, p[s - 1][1], 0.0)
            pr = jnp.zeros((SUBLANES, NST), F32)
            pi = jnp.zeros((SUBLANES, NST), F32)
            for row in range(SUBLANES):
                power = (SUBLANES - row) if rev else (row + 1)
                pr = jnp.where(r == row, p[power - 1][0], pr)
                pi = jnp.where(r == row, p[power - 1][1], pi)
            out_ref[pl.ds(6 * SUBLANES, SUBLANES), :] = pr
            out_ref[pl.ds(7 * SUBLANES, SUBLANES), :] = pi

    shape = jax.ShapeDtypeStruct((8 * SUBLANES, NST), F32)
    return pl.pallas_call(body, out_shape=[shape, shape], name="s5_scan_consts")(a_re_row, a_im_row, ldt_row)


def s5_bbar(a_re_c, a_im_c, ldt_c, b_re2, b_im2):
    def body(ar, ai, ld, br, bi, o_re, o_im):
        _, _, bbr, bbi = _discretise(ar[...], ai[...], ld[...], br[...], bi[...])
        o_re[...] = bbr
        o_im[...] = bbi

    shape = jax.ShapeDtypeStruct((NST, SSM_GC), F32)
    return pl.pallas_call(body, out_shape=[shape, shape], name="s5_bbar")(a_re_c, a_im_c, ldt_c, b_re2, b_im2)


def s5_param_bwd(a_re_c, a_im_c, ldt_c, b_re2, b_im2, d_lr, d_li, d_bbr, d_bbi):
    def body(ar, ai, ld, br, bi, g_lr, g_li, g_br, g_bi, o_ar, o_ai, o_ld, o_br, o_bi):
        _, vjp = jax.vjp(_discretise, ar[...], ai[...], ld[...], br[...], bi[...])
        d_ar, d_ai, d_ld, d_br, d_bi = vjp((g_lr[...], g_li[...], g_br[...], g_bi[...]))
        o_ar[...] = d_ar
        o_ai[...] = d_ai
        o_ld[...] = jnp.sum(d_ld.reshape(SSM_NG, SSM_N, 1), axis=1)
        o_br[...] = d_br
        o_bi[...] = d_bi

    col = jax.ShapeDtypeStruct((NST, 1), F32)
    mat = jax.ShapeDtypeStruct((NST, SSM_GC), F32)
    grp = jax.ShapeDtypeStruct((SSM_NG, 1), F32)
    return pl.pallas_call(body, out_shape=[col, col, grp, mat, mat], name="s5_param_bwd")(
        a_re_c, a_im_c, ldt_c, b_re2, b_im2, d_lr, d_li, d_bbr, d_bbi)


def _scan_tiles(buf_ref, cst_ref, carry_ref, rows, reverse, h_ref=None, acc_ref=None):
    n_tiles = rows // SUBLANES
    shifts = ((1, 0), (2, 2), (4, 4))
    row_id = lax.broadcasted_iota(jnp.int32, (SUBLANES, SCAN_COLS), 0)
    for j in range(NST // SCAN_COLS):
        c_re = pl.ds(j * SCAN_COLS, SCAN_COLS)
        c_im = pl.ds(NST + j * SCAN_COLS, SCAN_COLS)

        def step(i, carry, c_re=c_re, c_im=c_im):
            tile = (n_tiles - 1 - i) if reverse else i
            rws = pl.ds(pl.multiple_of(tile * SUBLANES, SUBLANES), SUBLANES)
            xr = buf_ref[rws, c_re]
            xi = buf_ref[rws, c_im]
            for s, base in shifts:
                amount = (SUBLANES - s) if reverse else s
                sr = pltpu.roll(xr, amount, 0)
                si = pltpu.roll(xi, amount, 0)
                mr = cst_ref[pl.ds(base * SUBLANES, SUBLANES), c_re]
                mi = cst_ref[pl.ds((base + 1) * SUBLANES, SUBLANES), c_re]
                xr, xi = xr + (mr * sr - mi * si), xi + (mr * si + mi * sr)
            pr = cst_ref[pl.ds(6 * SUBLANES, SUBLANES), c_re]
            pi = cst_ref[pl.ds(7 * SUBLANES, SUBLANES), c_re]
            cr, ci = carry[0], carry[1]
            xr, xi = xr + (pr * cr - pi * ci), xi + (pr * ci + pi * cr)
            buf_ref[rws, c_re] = xr
            buf_ref[rws, c_im] = xi
            edge = 0 if reverse else SUBLANES - 1
            ncr = jnp.broadcast_to(xr[edge:edge + 1, :], (SUBLANES, SCAN_COLS))
            nci = jnp.broadcast_to(xi[edge:edge + 1, :], (SUBLANES, SCAN_COLS))
            if not reverse:
                return ncr, nci
            gnr = jnp.where(row_id == SUBLANES - 1, cr, pltpu.roll(xr, SUBLANES - 1, 0))
            gni = jnp.where(row_id == SUBLANES - 1, ci, pltpu.roll(xi, SUBLANES - 1, 0))
            hr = h_ref[rws, c_re]
            hi = h_ref[rws, c_im]
            return ncr, nci, carry[2] + (gnr * hr + gni * hi), carry[3] + (gni * hr - gnr * hi)

        init = (carry_ref[:, c_re], carry_ref[:, c_im])
        if reverse:
            init = init + (acc_ref[:, c_re], acc_ref[:, c_im])
        out = lax.fori_loop(0, n_tiles, step, init)
        carry_ref[:, c_re] = out[0]
        carry_ref[:, c_im] = out[1]
        if reverse:
            acc_ref[:, c_re] = out[2]
            acc_ref[:, c_im] = out[3]


def _state_block(v, m):
    return jnp.concatenate([v[:, m * BLK_ST:(m + 1) * BLK_ST], v[:, NST + m * BLK_ST:NST + (m + 1) * BLK_ST]], axis=1)


def _put_state_block(buf_ref, m, val):
    buf_ref[:, pl.ds(m * BLK_ST, BLK_ST)] = val[:, :BLK_ST]
    buf_ref[:, pl.ds(NST + m * BLK_ST, BLK_ST)] = val[:, BLK_ST:]


def s5_fwd(proj, b_m, c_m, cst, d_skip, w_glu, nb, l):
    tt = min(256, l)
    nt = l // tt

    def body(u_ref, b_ref, c_ref, cst_ref, ds_ref, wg_ref, y_ref, h_ref, yp_ref, z_ref, buf_ref, carry_ref):
        @pl.when(pl.program_id(1) == 0)
        def _():
            carry_ref[...] = jnp.zeros_like(carry_ref)

        u = u_ref[...]
        ub = u.astype(BF16)
        for m in range(N_BLK):
            _put_state_block(buf_ref, m, _dot(ub[:, m * BLK_CH:(m + 1) * BLK_CH], b_ref[m]))
        _scan_tiles(buf_ref, cst_ref, carry_ref, tt, reverse=False)
        h = buf_ref[...].astype(BF16)
        h_ref[...] = h
        ypre = jnp.concatenate([_dot(_state_block(h, m), c_ref[m]) for m in range(N_BLK)], axis=1) + ds_ref[...] * u
        yp_ref[...] = ypre
        z = _dot(jax.nn.gelu(ypre).astype(BF16), wg_ref[...])
        z_ref[...] = z
        y_ref[...] = z[:, :SSM_W] * jax.nn.sigmoid(z[:, SSM_W:])

    row = lambda b, t: (b * nt + t, 0)
    const = lambda b, t: (0, 0)
    const3 = lambda b, t: (0, 0, 0)
    return pl.pallas_call(
        body, grid=(nb, nt),
        in_specs=[pl.BlockSpec((tt, SSM_W), lambda b, t: (b * nt + t, 1)),
                  pl.BlockSpec((N_BLK, BLK_CH, 2 * BLK_ST), const3),
                  pl.BlockSpec((N_BLK, 2 * BLK_ST, BLK_CH), const3),
                  pl.BlockSpec((8 * SUBLANES, NST), const),
                  pl.BlockSpec((1, SSM_W), const),
                  pl.BlockSpec((SSM_W, 2 * SSM_W), const)],
        out_specs=[pl.BlockSpec((tt, SSM_W), row),
                   pl.BlockSpec((tt, 2 * NST), row),
                   pl.BlockSpec((tt, SSM_W), row),
                   pl.BlockSpec((tt, 2 * SSM_W), row)],
        out_shape=[jax.ShapeDtypeStruct((nb * l, SSM_W), F32),
                   jax.ShapeDtypeStruct((nb * l, 2 * NST), BF16),
                   jax.ShapeDtypeStruct((nb * l, SSM_W), F32),
                   jax.ShapeDtypeStruct((nb * l, 2 * SSM_W), F32)],
        scratch_shapes=[pltpu.VMEM((tt, 2 * NST), F32), pltpu.VMEM((SUBLANES, 2 * NST), F32)],
        compiler_params=_params("arbitrary", "arbitrary"), name="s5_fwd")(proj, b_m, c_m, cst, d_skip, w_glu)


def s5_bwd(proj, d_ycat, h, ypre, z, b_m, c_m, cst_rev, d_skip, w_glu, nb, l):
    tt = min(256, l)
    nt = l // tt

    def body(u_ref, dy_ref, h_ref, yp_ref, z_ref, b_ref, c_ref, cst_ref, ds_ref, wg_ref,
             du_ref, dwg_ref, dds_ref, dcc_ref, dbb_ref, dlam_ref, buf_ref, hbuf_ref, carry_ref, dc_ref, db_ref):
        b = pl.program_id(0)
        t = pl.program_id(1)

        @pl.when((b == 0) & (t == 0))
        def _():
            dwg_ref[...] = jnp.zeros_like(dwg_ref)
            dds_ref[...] = jnp.zeros_like(dds_ref)
            dc_ref[...] = jnp.zeros_like(dc_ref)
            db_ref[...] = jnp.zeros_like(db_ref)
            dlam_ref[...] = jnp.zeros_like(dlam_ref)

        @pl.when(t == 0)
        def _():
            carry_ref[...] = jnp.zeros_like(carry_ref)

        u = u_ref[...]
        ypre = yp_ref[...]
        z = z_ref[...]
        z1 = z[:, :SSM_W]
        sg = jax.nn.sigmoid(z[:, SSM_W:])
        dy = dy_ref[...]
        dz = jnp.concatenate([dy * sg, dy * z1 * sg * (1.0 - sg)], axis=1).astype(BF16)
        yg, gelu_vjp = jax.vjp(jax.nn.gelu, ypre)
        dwg_ref[...] += _dot_tn(yg.astype(BF16), dz)
        dypre = gelu_vjp(_dot_nt(dz, wg_ref[...]))[0]
        dds_ref[...] += jnp.sum(dypre * u, axis=0, keepdims=True)
        dyb = dypre.astype(BF16)
        ub = u.astype(BF16)
        hb = h_ref[...]
        hbuf_ref[...] = hb.astype(F32)
        for m in range(N_BLK):
            dy_m = dyb[:, m * BLK_CH:(m + 1) * BLK_CH]
            dc_ref[m] += _dot_tn(_state_block(hb, m), dy_m)
            _put_state_block(buf_ref, m, _dot_nt(dy_m, c_ref[m]))
        _scan_tiles(buf_ref, cst_ref, carry_ref, tt, reverse=True, h_ref=hbuf_ref, acc_ref=dlam_ref)
        g = buf_ref[...].astype(BF16)
        du = []
        for m in range(N_BLK):
            g_m = _state_block(g, m)
            db_ref[m] += _dot_tn(g_m, ub[:, m * BLK_CH:(m + 1) * BLK_CH])
            du.append(_dot_nt(g_m, b_ref[m]))
        du_ref[...] = jnp.concatenate(du, axis=1) + ds_ref[...] * dypre

        @pl.when((b == nb - 1) & (t == nt - 1))
        def _():
            dlam_ref[...] = jnp.broadcast_to(jnp.sum(dlam_ref[...], axis=0, keepdims=True), dlam_ref.shape)
            for acc_ref, out_ref in ((dc_ref, dcc_ref), (db_ref, dbb_ref)):
                for m in range(N_BLK):
                    for ri in range(2):
                        for gl in range(BLK_CH // SSM_GC):
                            out_ref[ri, pl.ds((m * (BLK_CH // SSM_GC) + gl) * SSM_N, SSM_N), :] = acc_ref[
                                m, pl.ds(ri * BLK_ST + gl * SSM_N, SSM_N), pl.ds(gl * SSM_GC, SSM_GC)]

    def rrow(col):
        return lambda b, t: (b * nt + (nt - 1 - t), col)

    const = lambda b, t: (0, 0)
    const3 = lambda b, t: (0, 0, 0)
    return pl.pallas_call(
        body, grid=(nb, nt),
        in_specs=[pl.BlockSpec((tt, SSM_W), rrow(1)),
                  pl.BlockSpec((tt, SSM_W), rrow(1)),
                  pl.BlockSpec((tt, 2 * NST), rrow(0)),
                  pl.BlockSpec((tt, SSM_W), rrow(0)),
                  pl.BlockSpec((tt, 2 * SSM_W), rrow(0)),
                  pl.BlockSpec((N_BLK, BLK_CH, 2 * BLK_ST), const3),
                  pl.BlockSpec((N_BLK, 2 * BLK_ST, BLK_CH), const3),
                  pl.BlockSpec((8 * SUBLANES, NST), const),
                  pl.BlockSpec((1, SSM_W), const),
                  pl.BlockSpec((SSM_W, 2 * SSM_W), const)],
        out_specs=[pl.BlockSpec((tt, SSM_W), rrow(0)),
                   pl.BlockSpec((SSM_W, 2 * SSM_W), const),
                   pl.BlockSpec((1, SSM_W), const),
                   pl.BlockSpec((2, NST, SSM_GC), const3),
                   pl.BlockSpec((2, NST, SSM_GC), const3),
                   pl.BlockSpec((SUBLANES, 2 * NST), const)],
        out_shape=[jax.ShapeDtypeStruct((nb * l, SSM_W), F32),
                   jax.ShapeDtypeStruct((SSM_W, 2 * SSM_W), F32),
                   jax.ShapeDtypeStruct((1, SSM_W), F32),
                   jax.ShapeDtypeStruct((2, NST, SSM_GC), F32),
                   jax.ShapeDtypeStruct((2, NST, SSM_GC), F32),
                   jax.ShapeDtypeStruct((SUBLANES, 2 * NST), F32)],
        scratch_shapes=[pltpu.VMEM((tt, 2 * NST), F32), pltpu.VMEM((tt, 2 * NST), F32),
                        pltpu.VMEM((SUBLANES, 2 * NST), F32),
                        pltpu.VMEM((N_BLK, 2 * BLK_ST, BLK_CH), F32), pltpu.VMEM((N_BLK, 2 * BLK_ST, BLK_CH), F32)],
        compiler_params=_params("arbitrary", "arbitrary"), name="s5_bwd")(
            proj, d_ycat, h, ypre, z, b_m, c_m, cst_rev, d_skip, w_glu)


def _head_mask(hh):
    lane = lax.broadcasted_iota(jnp.int32, (1, ATT_W), 1)
    return (lane >= hh * MEM_HD) & (lane < (hh + 1) * MEM_HD)


def _mem_kv(mem_ref, gm_ref, wkv_ref):
    m = mem_ref[0]
    mh = m * _rms_scale(m)
    mb = (mh * gm_ref[...]).astype(BF16)
    kv = _dot(mb, wkv_ref[...])
    return mh, mb, kv[:, :ATT_W].astype(BF16), kv[:, ATT_W:].astype(BF16)


def _softmax_rows(s):
    e = jnp.exp(s - jnp.max(s, axis=-1, keepdims=True))
    return e / jnp.sum(e, axis=-1, keepdims=True)


def attn_fwd(proj, mem, g_mem, w_kv, nb, l):
    tq = min(512, l)
    nq = l // tq
    scale = MEM_HD ** -0.5

    def body(q_ref, mem_ref, gm_ref, wkv_ref, o_ref, k_s, v_s):
        @pl.when(pl.program_id(1) == 0)
        def _():
            _, _, k, v = _mem_kv(mem_ref, gm_ref, wkv_ref)
            k_s[...] = k
            v_s[...] = v

        q = q_ref[...].astype(BF16)
        k = k_s[...]
        v = v_s[...]
        o = jnp.zeros((tq, ATT_W), F32)
        for hh in range(MEM_HEADS):
            msk = _head_mask(hh)
            p = _softmax_rows(_dot_nt(q, jnp.where(msk, k, 0)) * scale)
            o = o + _dot(p.astype(BF16), jnp.where(msk, v, 0))
        o_ref[...] = o

    const = lambda b, t: (0, 0)
    return pl.pallas_call(
        body, grid=(nb, nq),
        in_specs=[pl.BlockSpec((tq, ATT_W), lambda b, t: (b * nq + t, 3)),
                  pl.BlockSpec((1, N_MEM, D_MODEL), lambda b, t: (b, 0, 0)),
                  pl.BlockSpec((1, D_MODEL), const),
                  pl.BlockSpec((D_MODEL, 2 * ATT_W), const)],
        out_specs=pl.BlockSpec((tq, ATT_W), lambda b, t: (b * nq + t, 0)),
        out_shape=jax.ShapeDtypeStruct((nb * l, ATT_W), F32),
        scratch_shapes=[pltpu.VMEM((N_MEM, ATT_W), BF16), pltpu.VMEM((N_MEM, ATT_W), BF16)],
        compiler_params=_params("arbitrary", "arbitrary"), name="attn_fwd")(proj, mem, g_mem, w_kv)


def attn_bwd(proj, d_ycat, mem, g_mem, w_kv, nb, l):
    tq = min(512, l)
    nq = l // tq
    scale = MEM_HD ** -0.5

    def body(q_ref, do_ref, mem_ref, gm_ref, wkv_ref, dq_ref, dwkv_ref, dgm_ref, k_s, v_s, mb_s, dk_s, dv_s):
        b = pl.program_id(0)
        t = pl.program_id(1)

        @pl.when((b == 0) & (t == 0))
        def _():
            dwkv_ref[...] = jnp.zeros_like(dwkv_ref)
            dgm_ref[...] = jnp.zeros_like(dgm_ref)

        @pl.when(t == 0)
        def _():
            _, mb, k, v = _mem_kv(mem_ref, gm_ref, wkv_ref)
            k_s[...] = k
            v_s[...] = v
            mb_s[...] = mb
            dk_s[...] = jnp.zeros_like(dk_s)
            dv_s[...] = jnp.zeros_like(dv_s)

        q = q_ref[...].astype(BF16)
        do = do_ref[...].astype(BF16)
        k = k_s[...]
        v = v_s[...]
        dq = jnp.zeros((tq, ATT_W), F32)
        dk = jnp.zeros((N_MEM, ATT_W), F32)
        dv = jnp.zeros((N_MEM, ATT_W), F32)
        for hh in range(MEM_HEADS):
            msk = _head_mask(hh)
            kh = jnp.where(msk, k, 0)
            p = _softmax_rows(_dot_nt(q, kh) * scale)
            dp = _dot_nt(do, jnp.where(msk, v, 0))
            ds = (p * (dp - jnp.sum(dp * p, axis=-1, keepdims=True)) * scale).astype(BF16)
            dq = dq + _dot(ds, kh)
            dk = dk + jnp.where(msk, _dot_tn(ds, q), 0.0)
            dv = dv + jnp.where(msk, _dot_tn(p.astype(BF16), do), 0.0)
        dq_ref[...] = dq
        dk_s[...] += dk
        dv_s[...] += dv

        @pl.when(t == nq - 1)
        def _():
            dkv = jnp.concatenate([dk_s[...], dv_s[...]], axis=1).astype(BF16)
            dwkv_ref[...] += _dot_tn(mb_s[...], dkv)
            m = mem_ref[0]
            dgm_ref[...] += jnp.sum(_dot_nt(dkv, wkv_ref[...]) * (m * _rms_scale(m)), axis=0, keepdims=True)

    const = lambda b, t: (0, 0)
    return pl.pallas_call(
        body, grid=(nb, nq),
        in_specs=[pl.BlockSpec((tq, ATT_W), lambda b, t: (b * nq + t, 3)),
                  pl.BlockSpec((tq, ATT_W), lambda b, t: (b * nq + t, 3)),
                  pl.BlockSpec((1, N_MEM, D_MODEL), lambda b, t: (b, 0, 0)),
                  pl.BlockSpec((1, D_MODEL), const),
                  pl.BlockSpec((D_MODEL, 2 * ATT_W), const)],
        out_specs=[pl.BlockSpec((tq, ATT_W), lambda b, t: (b * nq + t, 0)),
                   pl.BlockSpec((D_MODEL, 2 * ATT_W), const),
                   pl.BlockSpec((1, D_MODEL), const)],
        out_shape=[jax.ShapeDtypeStruct((nb * l, ATT_W), F32),
                   jax.ShapeDtypeStruct((D_MODEL, 2 * ATT_W), F32),
                   jax.ShapeDtypeStruct((1, D_MODEL), F32)],
        scratch_shapes=[pltpu.VMEM((N_MEM, ATT_W), BF16), pltpu.VMEM((N_MEM, ATT_W), BF16),
                        pltpu.VMEM((N_MEM, D_MODEL), BF16),
                        pltpu.VMEM((N_MEM, ATT_W), F32), pltpu.VMEM((N_MEM, ATT_W), F32)],
        compiler_params=_params("arbitrary", "arbitrary"), name="attn_bwd")(proj, d_ycat, mem, g_mem, w_kv)


def tail(y_pool, y_ssm, y_att, proj, x2, target, w_out, g_post):
    t = x2.shape[0]
    tm = min(256, t)

    def body(yp_ref, ys_ref, ya_ref, gate_ref, x_ref, tg_ref, w_ref, gp_ref,
             dz_ref, dyc_ref, dgate_ref, dw_ref, dgp_ref, loss_ref):
        @pl.when(pl.program_id(0) == 0)
        def _():
            dw_ref[...] = jnp.zeros_like(dw_ref)
            dgp_ref[...] = jnp.zeros_like(dgp_ref)
            loss_ref[...] = jnp.zeros_like(loss_ref)

        ycat = jnp.concatenate([yp_ref[...], ys_ref[...], ya_ref[...]], axis=1)
        gate = gate_ref[...]
        sg = jax.nn.sigmoid(gate)
        silu = gate * sg
        yb = (ycat * silu).astype(BF16)
        w = w_ref[...]
        out = _dot(yb, w)
        r2 = _rms_scale(out)
        oh = out * r2
        gp = gp_ref[...]
        err = (x_ref[...] + oh * gp) - tg_ref[...]
        loss_ref[...] += 0.5 * jnp.sum(jnp.mean(err * err, axis=-1, keepdims=True), axis=0, keepdims=True)
        dz = err * (1.0 / D_MODEL)
        dz_ref[...] = dz
        dgp_ref[...] += jnp.sum(dz * oh, axis=0, keepdims=True)
        dn = dz * gp
        dout = (r2 * (dn - oh * jnp.mean(dn * oh, axis=-1, keepdims=True))).astype(BF16)
        dw_ref[...] += _dot_tn(yb, dout)
        dy = _dot_nt(dout, w)
        dyc_ref[...] = dy * silu
        dgate_ref[...] = dy * ycat * (sg * (1.0 + gate * (1.0 - sg)))

    row = lambda i: (i, 0)
    const = lambda i: (0, 0)
    full = jax.ShapeDtypeStruct((t, D_MODEL), F32)
    return pl.pallas_call(
        body, grid=(t // tm,),
        in_specs=[pl.BlockSpec((tm, POOL_W), row),
                  pl.BlockSpec((tm, SSM_W), row),
                  pl.BlockSpec((tm, ATT_W), row),
                  pl.BlockSpec((tm, D_MODEL), lambda i: (i, 1)),
                  pl.BlockSpec((tm, D_MODEL), row),
                  pl.BlockSpec((tm, D_MODEL), row),
                  pl.BlockSpec((D_MODEL, D_MODEL), const),
                  pl.BlockSpec((1, D_MODEL), const)],
        out_specs=[pl.BlockSpec((tm, D_MODEL), row),
                   pl.BlockSpec((tm, D_MODEL), row),
                   pl.BlockSpec((tm, D_MODEL), row),
                   pl.BlockSpec((D_MODEL, D_MODEL), const),
                   pl.BlockSpec((1, D_MODEL), const),
                   pl.BlockSpec((1, LANES), const)],
        out_shape=[full, full, full,
                   jax.ShapeDtypeStruct((D_MODEL, D_MODEL), F32),
                   jax.ShapeDtypeStruct((1, D_MODEL), F32),
                   jax.ShapeDtypeStruct((1, LANES), F32)],
        compiler_params=_params("arbitrary"), name="tail")(y_pool, y_ssm, y_att, proj, x2, target, w_out, g_post)


def in_proj_bwd(du_pool, du_ssm, dq, d_gate, x2, dz, g_pre, w_in):
    t = x2.shape[0]
    tm = min(256, t)
    shard = W_IN_SHARD

    def body(dup_ref, dus_ref, dq_ref, dg_ref, x_ref, dz_ref, g_ref, w_ref, gx_ref, dw_ref, dgp_ref):
        @pl.when(pl.program_id(0) == 0)
        def _():
            dw_ref[...] = jnp.zeros_like(dw_ref)
            dgp_ref[...] = jnp.zeros_like(dgp_ref)

        dproj = jnp.concatenate([dup_ref[...], dus_ref[...], dq_ref[...], dg_ref[...]], axis=1).astype(BF16)
        x = x_ref[...]
        r1 = _rms_scale(x)
        xh = x * r1
        g = g_ref[...]
        hb = (xh * g).astype(BF16)
        dh = jnp.zeros((tm, D_MODEL), F32)
        for k in range(N_CHIPS):
            dp_k = dproj[:, k * shard:(k + 1) * shard]
            dw_ref[k] += _dot_tn(hb, dp_k)
            dh = dh + _dot_nt(dp_k, w_ref[k])
        dgp_ref[...] += jnp.sum(dh * xh, axis=0, keepdims=True)
        dn = dh * g
        gx_ref[...] = dz_ref[...] + r1 * (dn - xh * jnp.mean(dn * xh, axis=-1, keepdims=True))

    row = lambda i: (i, 0)
    const = lambda i: (0, 0)
    return pl.pallas_call(
        body, grid=(t // tm,),
        in_specs=[pl.BlockSpec((tm, POOL_W), row),
                  pl.BlockSpec((tm, SSM_W), row),
                  pl.BlockSpec((tm, ATT_W), row),
                  pl.BlockSpec((tm, D_MODEL), row),
                  pl.BlockSpec((tm, D_MODEL), row),
                  pl.BlockSpec((tm, D_MODEL), row),
                  pl.BlockSpec((1, D_MODEL), const),
                  pl.BlockSpec((N_CHIPS, D_MODEL, shard), lambda i: (0, 0, 0))],
        out_specs=[pl.BlockSpec((tm, D_MODEL), row),
                   pl.BlockSpec((N_CHIPS, D_MODEL, shard), lambda i: (0, 0, 0)),
                   pl.BlockSpec((1, D_MODEL), const)],
        out_shape=[jax.ShapeDtypeStruct((t, D_MODEL), F32),
                   jax.ShapeDtypeStruct((N_CHIPS, D_MODEL, shard), F32),
                   jax.ShapeDtypeStruct((1, D_MODEL), F32)],
        compiler_params=_params("arbitrary"), name="in_proj_bwd")(du_pool, du_ssm, dq, d_gate, x2, dz, g_pre, w_in)


def _block_diag(blocks):
    g, r, c = blocks.shape
    eye = jnp.eye(g, dtype=blocks.dtype)
    return jnp.einsum("grc,gh->grhc", blocks, eye).reshape(g * r, g * c)


def _block_diag_extract(mat, g, r, c):
    eye = jnp.eye(g, dtype=mat.dtype)
    return jnp.einsum("grhc,gh->grc", mat.reshape(g, r, g, c), eye)


def local_step(x, mem, target, g_pre, w_in, w_pool, pool_scale, a_re, a_im, log_dt, b_re, b_im, c_re, c_im, d_skip,
               w_glu, g_mem, w_kv, w_out, g_post):
    nb, l, _ = x.shape
    x2 = x.reshape(nb * l, D_MODEL)
    tg2 = target.reshape(nb * l, D_MODEL)

    rowv = lambda a: a.reshape(1, NST)
    colv = lambda a: a.reshape(NST, 1)
    ldt_full = jnp.broadcast_to(log_dt.reshape(SSM_NG, 1), (SSM_NG, SSM_N))
    b_re2 = b_re.reshape(NST, SSM_GC)
    b_im2 = b_im.reshape(NST, SSM_GC)
    cst_f, cst_b = s5_scan_consts(rowv(a_re), rowv(a_im), rowv(ldt_full))
    bbr, bbi = s5_bbar(colv(a_re), colv(a_im), colv(ldt_full), b_re2, b_im2)
    eye = jnp.eye(BLK_GROUPS, dtype=F32)
    blk_in = lambda bb: jnp.einsum("mgnc,gh->mhcgn", bb.reshape(N_BLK, BLK_GROUPS, SSM_N, SSM_GC), eye).reshape(
        N_BLK, BLK_CH, BLK_ST)
    blk_out = lambda cc: jnp.einsum("mgcn,gh->mgnhc", cc.reshape(N_BLK, BLK_GROUPS, SSM_GC, SSM_N), eye).reshape(
        N_BLK, BLK_ST, BLK_CH)
    b_m = jnp.concatenate([blk_in(bbr), blk_in(bbi)], axis=2).astype(BF16)
    c_m = jnp.concatenate([blk_out(c_re), -blk_out(c_im)], axis=1).astype(BF16)
    w_pool_blk = _block_diag(w_pool.reshape(4, POOL_GW, POOL_GW)).astype(BF16)

    proj = in_proj(x2, g_pre, w_in)
    y_pool = pool_fwd(proj, w_pool_blk, pool_scale, nb, l)
    y_ssm, h, ypre, z = s5_fwd(proj, b_m, c_m, cst_f, d_skip, w_glu, nb, l)
    y_att = attn_fwd(proj, mem, g_mem, w_kv, nb, l)
    dz, d_ycat, d_gate, d_w_out, d_g_post, loss = tail(y_pool, y_ssm, y_att, proj, x2, tg2, w_out, g_post)
    du_pool, d_wp_blk, d_pool_scale = pool_bwd(proj, d_ycat, w_pool_blk, pool_scale, nb, l)
    du_ssm, d_w_glu, d_d_skip, d_cc, d_bb, d_lam = s5_bwd(
        proj, d_ycat, h, ypre, z, b_m, c_m, cst_b, d_skip, w_glu, nb, l)
    dq, d_w_kv, d_g_mem = attn_bwd(proj, d_ycat, mem, g_mem, w_kv, nb, l)
    grad_x, d_w_in, d_g_pre = in_proj_bwd(du_pool, du_ssm, dq, d_gate, x2, dz, g_pre, w_in)

    d_c = d_cc[0].reshape(SSM_NG, SSM_N, SSM_GC)
    d_ci = d_cc[1].reshape(SSM_NG, SSM_N, SSM_GC)
    d_lam_row = d_lam[0]
    d_ar, d_ai, d_ld, d_br, d_bi = s5_param_bwd(
        colv(a_re), colv(a_im), colv(ldt_full), b_re2, b_im2,
        d_lam_row[:NST].reshape(NST, 1), d_lam_row[NST:].reshape(NST, 1), d_bb[0], d_bb[1])

    grads = {
        "g_pre": d_g_pre,
        "w_in": d_w_in,
        "w_pool": _block_diag_extract(d_wp_blk, 4, POOL_GW, POOL_GW).reshape(1, 4, POOL_GW, POOL_GW),
        "pool_scale": d_pool_scale,
        "a_re": d_ar.reshape(1, SSM_NG, SSM_N),
        "a_im": d_ai.reshape(1, SSM_NG, SSM_N),
        "log_dt": d_ld.reshape(1, SSM_NG),
        "b_re": d_br.reshape(1, SSM_NG, SSM_N, SSM_GC),
        "b_im": d_bi.reshape(1, SSM_NG, SSM_N, SSM_GC),
        "c_re": d_c.transpose(0, 2, 1).reshape(1, SSM_NG, SSM_GC, SSM_N),
        "c_im": (-d_ci).transpose(0, 2, 1).reshape(1, SSM_NG, SSM_GC, SSM_N),
        "d_skip": d_d_skip,
        "w_glu": d_w_glu,
        "g_mem": d_g_mem,
        "w_kv": d_w_kv,
        "w_out": d_w_out,
        "g_post": d_g_post,
    }
    return loss, grad_x.reshape(nb, l, D_MODEL), grads


VMEM_SPEC = pl.BlockSpec(memory_space=pltpu.VMEM)
ANY_SPEC = pl.BlockSpec(memory_space=pl.ANY)


def _place():
    return lax.axis_index("x"), lax.axis_index("y"), lax.axis_index("c")


def _other_chips(x, y):
    return [(1 - x, y), (x, 1 - y), (1 - x, 1 - y)]


def gather_weights(shards):
    n = len(shards)

    def body(*refs):
        ins, outs = refs[:n], refs[n:2 * n]
        send_sems, recv_sems = refs[2 * n:]
        x, y, c = _place()
        k = 2 * x + y
        chips = _other_chips(x, y)
        for i in range(n):
            outs[i][k] = ins[i][...].astype(BF16)

        def copy(i, block, core_half, to, sem):
            hr = shards[i].shape[0] // 2
            rows = outs[i].at[block, pl.ds(pl.multiple_of(core_half * hr, 2 * SUBLANES), hr)]
            return pltpu.make_async_remote_copy(
                src_ref=rows, dst_ref=rows, send_sem=send_sems.at[sem], recv_sem=recv_sems.at[sem],
                device_id=to, device_id_type=MESH)

        pairs = [(i, j) for i in range(n) for j in range(3)]
        first = [copy(i, k, c, (*chips[j], c), i * 3 + j) for i, j in pairs]
        for cp in first:
            cp.start()
        passed = []
        for i, j in pairs:
            px, py = chips[j]
            copy(i, 2 * px + py, c, (px, py, c), i * 3 + j).wait_recv()
            fw = copy(i, 2 * px + py, c, (x, y, 1 - c), 3 * n + i * 3 + j)
            fw.start()
            passed.append(fw)
        for i, j in pairs:
            px, py = chips[j]
            copy(i, 2 * px + py, 1 - c, (x, y, 1 - c), 3 * n + i * 3 + j).wait_recv()
        for cp in first + passed:
            cp.wait_send()

    return pl.pallas_call(
        body,
        out_shape=[jax.ShapeDtypeStruct((N_CHIPS,) + s.shape, BF16) for s in shards],
        in_specs=[VMEM_SPEC] * n, out_specs=[VMEM_SPEC] * n,
        scratch_shapes=[pltpu.SemaphoreType.DMA((6 * n,)), pltpu.SemaphoreType.DMA((6 * n,))],
        compiler_params=pltpu.CompilerParams(vmem_limit_bytes=VMEM_LIMIT),
        name="gather_weights")(*shards)


SEMS_PER_ITEM = 5


def reduce_all(g4s, packed):
    n = len(g4s)
    dims = [g.shape[1:] for g in g4s]
    pr = packed.shape[0]

    def body(*refs):
        g_refs, p_ref = refs[:n], refs[n]
        outs, op = refs[n + 1:2 * n + 1], refs[2 * n + 1]
        scr = refs[2 * n + 2:]
        mine, theirs, sendb, recvb = scr[0:n], scr[n:2 * n], scr[2 * n:3 * n], scr[3 * n:4 * n]
        p_theirs, gat, lsems, send_sems, recv_sems = scr[4 * n:]
        x, y, c = _place()
        k = 2 * x + y
        chips = _other_chips(x, y)
        sib = (x, y, 1 - c)

        def halves(rows):
            hr = rows // 2
            return (pl.ds(pl.multiple_of(c * hr, SUBLANES), hr), pl.ds(pl.multiple_of((1 - c) * hr, SUBLANES), hr))

        def remote(src, dst, sem, to):
            return pltpu.make_async_remote_copy(
                src_ref=src, dst_ref=dst, send_sem=send_sems.at[sem], recv_sem=recv_sems.at[sem],
                device_id=to, device_id_type=MESH)

        loads, started = [], []
        for i in range(n):
            my_rows, sib_rows = halves(dims[i][0])
            ld = pltpu.make_async_copy(g_refs[i].at[:, my_rows, :], mine[i], lsems.at[i])
            ld.start()
            loads.append(ld)
            sw = remote(g_refs[i].at[:, sib_rows, :], theirs[i], SEMS_PER_ITEM * i, sib)
            sw.start()
            started.append(sw)
        p_my, p_sib = halves(pr)
        p_swap = remote(p_ref.at[p_sib, :], p_theirs, SEMS_PER_ITEM * n, sib)
        p_swap.start()
        started.append(p_swap)

        p_swap.wait_recv()
        gat[k] = p_ref[p_my, :] + p_theirs[...]
        for j, (px, py) in enumerate(chips):
            cp = remote(gat.at[k], gat.at[k], SEMS_PER_ITEM * n + 1 + j, (px, py, c))
            cp.start()
            started.append(cp)
        ici = []
        for i in range(n):
            loads[i].wait()
            started[i].wait_recv()
            for j, (px, py) in enumerate(chips):
                kp = 2 * px + py
                sendb[i][j] = (mine[i][kp] + theirs[i][kp]).astype(BF16)
                cp = remote(sendb[i].at[j], recvb[i].at[j], SEMS_PER_ITEM * i + 1 + j, (px, py, c))
                cp.start()
                ici.append(cp)

        for j, (px, py) in enumerate(chips):
            kp = 2 * px + py
            remote(gat.at[kp], gat.at[kp], SEMS_PER_ITEM * n + 1 + j, (px, py, c)).wait_recv()
        op[p_my, :] = ((gat[0] + gat[1]) + gat[2]) + gat[3]
        last = remote(op.at[p_my, :], op.at[p_my, :], SEMS_PER_ITEM * n + 4, sib)
        last.start()
        started.append(last)
        for i in range(n):
            my_rows, _ = halves(dims[i][0])
            half = mine[i][k] + theirs[i][k]
            for j in range(3):
                ici[3 * i + j].wait_recv()
                half = half + recvb[i][j].astype(F32)
            outs[i][my_rows, :] = half
            last = remote(outs[i].at[my_rows, :], outs[i].at[my_rows, :], SEMS_PER_ITEM * i + 4, sib)
            last.start()
            started.append(last)
        remote(op.at[p_sib, :], op.at[p_sib, :], SEMS_PER_ITEM * n + 4, sib).wait_recv()
        for i in range(n):
            _, sib_rows = halves(dims[i][0])
            remote(outs[i].at[sib_rows, :], outs[i].at[sib_rows, :], SEMS_PER_ITEM * i + 4, sib).wait_recv()
        for cp in started + ici:
            cp.wait_send()

    n_sems = SEMS_PER_ITEM * (n + 1)
    scratch = ([pltpu.VMEM((N_CHIPS, r // 2, cd), F32) for r, cd in dims] * 2
               + [pltpu.VMEM((3, r // 2, cd), BF16) for r, cd in dims] * 2
               + [pltpu.VMEM((pr // 2, LANES), F32), pltpu.VMEM((N_CHIPS, pr // 2, LANES), F32),
                  pltpu.SemaphoreType.DMA((n,)), pltpu.SemaphoreType.DMA((n_sems,)), pltpu.SemaphoreType.DMA((n_sems,))])
    res = pl.pallas_call(
        body,
        out_shape=[jax.ShapeDtypeStruct(d, F32) for d in dims] + [jax.ShapeDtypeStruct(packed.shape, F32)],
        in_specs=[ANY_SPEC] * n + [VMEM_SPEC], out_specs=[VMEM_SPEC] * (n + 1),
        scratch_shapes=scratch,
        compiler_params=pltpu.CompilerParams(vmem_limit_bytes=VMEM_LIMIT),
        name="reduce_all")(*g4s, packed)
    return res[:n], res[n]


def adamw_all(ws, gs, ms, vs):
    n = len(ws)

    def body(*refs):
        w, g, m, v = refs[:n], refs[n:2 * n], refs[2 * n:3 * n], refs[3 * n:4 * n]
        od, om, ov = refs[4 * n:5 * n], refs[5 * n:6 * n], refs[6 * n:]
        for i in range(n):
            od[i][...], om[i][...], ov[i][...] = _adamw(w[i][...], g[i][...], m[i][...], v[i][...])

    shapes = [jax.ShapeDtypeStruct(a.shape, F32) for a in ws]
    res = pl.pallas_call(
        body, out_shape=shapes * 3, in_specs=[VMEM_SPEC] * (4 * n), out_specs=[VMEM_SPEC] * (3 * n),
        compiler_params=pltpu.CompilerParams(vmem_limit_bytes=VMEM_LIMIT),
        name="adamw_all")(*ws, *gs, *ms, *vs)
    return res[:n], res[n:2 * n], res[2 * n:]


WEIGHTS = ("g_pre", "w_in", "w_pool", "pool_scale", "a_re", "a_im", "log_dt", "b_re", "b_im", "c_re", "c_im", "d_skip",
           "w_glu", "g_mem", "w_kv", "w_out", "g_post")
SHARDED = ("w_in", "w_glu", "w_kv", "w_out")
REPLICATED = tuple(n for n in WEIGHTS if n not in SHARDED)
PACK_TILE = SUBLANES * LANES


def _pack(arrays):
    parts = []
    for a in arrays:
        flat = a.reshape(-1)
        parts.append(jnp.pad(flat, (0, -flat.shape[0] % PACK_TILE)).reshape(-1, LANES))
    rows = sum(p.shape[0] for p in parts)
    if rows % (2 * SUBLANES):
        parts.append(jnp.zeros((SUBLANES, LANES), F32))
    return jnp.concatenate(parts, axis=0)


def _unpack(packed, shapes):
    out, row = [], 0
    for shp in shapes:
        size = math.prod(shp)
        rows = -(-size // PACK_TILE) * SUBLANES
        out.append(packed[row:row + rows].reshape(-1)[:size].reshape(shp))
        row += rows
    return out


def _chip_major(a, n_cols):
    return a.reshape(a.shape[0], N_CHIPS, n_cols).transpose(1, 0, 2)


def _from_chip_major(a):
    return a.transpose(1, 0, 2).reshape(a.shape[1], N_CHIPS * a.shape[2])


def kernel(x, mem, g_pre, w_in, w_pool, pool_scale, a_re, a_im, log_dt, b_re, b_im, c_re, c_im, d_skip, w_glu, g_mem, w_kv, w_out, g_post, loss_target, m_g_pre, m_w_in, m_w_pool, m_pool_scale, m_a_re, m_a_im, m_log_dt, m_b_re, m_b_im, m_c_re, m_c_im, m_d_skip, m_w_glu, m_g_mem, m_w_kv, m_w_out, m_g_post, v_g_pre, v_w_in, v_w_pool, v_pool_scale, v_a_re, v_a_im, v_log_dt, v_b_re, v_b_im, v_c_re, v_c_im, v_d_skip, v_w_glu, v_g_mem, v_w_kv, v_w_out, v_g_post):
    given = dict(locals())
    wts = {n: given[n] for n in WEIGHTS}
    mom = {n: given["m_" + n] for n in WEIGHTS}
    var = {n: given["v_" + n] for n in WEIGHTS}

    g_in, g_glu, g_kv, g_out = gather_weights([wts[n][0] for n in SHARDED])
    w_in_full = g_in
    w_glu_full = _from_chip_major(g_glu)
    w_kv_full = g_kv.reshape(D_MODEL, 2 * ATT_W)
    w_out_full = g_out.reshape(D_MODEL, D_MODEL)

    loss_part, grad_x, grads = local_step(
        x, mem, loss_target, g_pre, w_in_full, w_pool[0], pool_scale, a_re[0], a_im[0], log_dt[0], b_re[0], b_im[0],
        c_re[0], c_im[0], d_skip, w_glu_full, g_mem, w_kv_full, w_out_full, g_post)

    partial4 = [
        grads["w_in"],
        _chip_major(grads["w_glu"], 2 * SSM_W // N_CHIPS),
        grads["w_kv"].reshape(N_CHIPS, D_MODEL // N_CHIPS, 2 * ATT_W),
        grads["w_out"].reshape(N_CHIPS, D_MODEL // N_CHIPS, D_MODEL),
    ]
    loss_tile = jnp.broadcast_to(loss_part, (SUBLANES, LANES))
    sharded_g, packed_g = reduce_all(partial4, _pack([grads[n] for n in REPLICATED] + [loss_tile]))
    small_g = _unpack(packed_g, [wts[n].shape for n in REPLICATED] + [(SUBLANES, LANES)])
    loss = small_g[-1][0, 0]
    grad = dict(zip(SHARDED, [g[None] for g in sharded_g]))
    grad.update(zip(REPLICATED, small_g[:-1]))

    two_d = lambda a: a.reshape(-1, a.shape[-1])
    deltas, new_ms, new_vs = adamw_all([two_d(wts[n]) for n in WEIGHTS], [two_d(grad[n]) for n in WEIGHTS],
                                       [two_d(mom[n]) for n in WEIGHTS], [two_d(var[n]) for n in WEIGHTS])
    back = lambda arrs: [a.reshape(wts[n].shape) for a, n in zip(arrs, WEIGHTS)]
    return (loss, grad_x, *[grad[n] for n in WEIGHTS], *back(deltas), *back(new_ms), *back(new_vs))
```

```python
import functools
import math

import jax
import jax.numpy as jnp
from jax import lax
from jax.experimental import pallas as pl
from jax.experimental.pallas import tpu as pltpu

F32 = jnp.float32
BF16 = jnp.bfloat16

D_MODEL = 1024
POOL_W = 384
SSM_W = 384
ATT_W = 256
POOL_GW = 96
POOL_WINDOWS = (2, 4, 8, 16)
POOL_PAD = 16
SSM_NG = 24
SSM_N = 64
SSM_GC = 16
N_CHIPS = 4
N_DEV = 8
W_IN_SHARD = 2 * D_MODEL // N_CHIPS
NST = SSM_NG * SSM_N
BLK_CH = 128
BLK_GROUPS = BLK_CH // SSM_GC
BLK_ST = BLK_GROUPS * SSM_N
N_BLK = SSM_W // BLK_CH
N_MEM = 256
MEM_HEADS = 4
MEM_HD = 64
EPS = 1e-6

ADAM_LR = 0.001
ADAM_B1 = 0.9
ADAM_B2 = 0.999
ADAM_EPS = 1e-08
ADAM_WD = 0.01
ADAM_STEP = 10

SUBLANES = 8
LANES = 128
V7X_VMEM_BYTES = 64 * 2**20
VMEM_LIMIT = V7X_VMEM_BYTES - 8 * 2**20
SCAN_COLS = 512
MESH = pl.DeviceIdType.MESH

NT = (((1,), (1,)), ((), ()))
TN = (((0,), (0,)), ((), ()))


def _params(*sem):
    return pltpu.CompilerParams(dimension_semantics=sem, vmem_limit_bytes=VMEM_LIMIT)


def _dot(a, b):
    return jnp.dot(a, b, preferred_element_type=F32)


def _dot_nt(a, b):
    return lax.dot_general(a, b, NT, preferred_element_type=F32)


def _dot_tn(a, b):
    return lax.dot_general(a, b, TN, preferred_element_type=F32)


def _rms_scale(v):
    return lax.rsqrt(jnp.mean(v * v, axis=-1, keepdims=True) + EPS)


def _adamw(w, g, m, v):
    m = ADAM_B1 * m + (1.0 - ADAM_B1) * g
    v = ADAM_B2 * v + (1.0 - ADAM_B2) * (g * g)
    m_hat = m / (1.0 - ADAM_B1 ** ADAM_STEP)
    v_hat = v / (1.0 - ADAM_B2 ** ADAM_STEP)
    delta = -ADAM_LR * (m_hat / (jnp.sqrt(v_hat) + ADAM_EPS) + ADAM_WD * w)
    return delta, m, v


def in_proj(x2, g_pre, w_in):
    t = x2.shape[0]
    tm = min(512, t)

    def body(x_ref, g_ref, w_ref, o_ref):
        x = x_ref[...]
        h = (x * _rms_scale(x) * g_ref[...]).astype(BF16)
        for k in range(N_CHIPS):
            o_ref[:, pl.ds(k * W_IN_SHARD, W_IN_SHARD)] = _dot(h, w_ref[k])

    return pl.pallas_call(
        body, grid=(t // tm,),
        in_specs=[pl.BlockSpec((tm, D_MODEL), lambda i: (i, 0)),
                  pl.BlockSpec((1, D_MODEL), lambda i: (0, 0)),
                  pl.BlockSpec((N_CHIPS, D_MODEL, W_IN_SHARD), lambda i: (0, 0, 0))],
        out_specs=pl.BlockSpec((tm, 2 * D_MODEL), lambda i: (i, 0)),
        out_shape=jax.ShapeDtypeStruct((t, 2 * D_MODEL), F32),
        compiler_params=_params("arbitrary"), name="in_proj")(x2, g_pre, w_in)


def _pool_lane_window(shape):
    ch = lax.broadcasted_iota(jnp.int32, shape, 1)
    return jnp.where(ch < POOL_GW, 2.0, jnp.where(ch < 2 * POOL_GW, 4.0, jnp.where(ch < 3 * POOL_GW, 8.0, 16.0)))


def _pool_select(win, s2, s4, s8, s16):
    return jnp.where(win == 2.0, s2, jnp.where(win == 4.0, s4, jnp.where(win == 8.0, s8, s16)))


def _pool_div_count(v, win):
    pos = (lax.broadcasted_iota(jnp.int32, (POOL_PAD, POOL_W), 0) + 1).astype(F32)
    head = v[:POOL_PAD] / jnp.minimum(pos, win)
    return jnp.concatenate([head, v[POOL_PAD:] * (1.0 / win)], axis=0)


def _pool_diff(u, pad_ref, l):
    lo = POOL_PAD
    pad_ref[pl.ds(lo, l), :] = u
    s2 = u + pad_ref[pl.ds(lo - 1, l), :]
    pad_ref[pl.ds(lo, l), :] = s2
    s4 = s2 + pad_ref[pl.ds(lo - 2, l), :]
    pad_ref[pl.ds(lo, l), :] = s4
    s8 = s4 + pad_ref[pl.ds(lo - 4, l), :]
    pad_ref[pl.ds(lo, l), :] = s8
    s16 = s8 + pad_ref[pl.ds(lo - 8, l), :]
    win = _pool_lane_window((1, POOL_W))
    return _pool_div_count(_pool_select(win, s2, s4, s8, s16), win) - u, win


def pool_fwd(proj, w_blk, pool_scale, nb, l):
    def body(u_ref, w_ref, ps_ref, y_ref, pad_ref):
        pad_ref[pl.ds(0, POOL_PAD), :] = jnp.zeros((POOL_PAD, POOL_W), F32)
        d, _ = _pool_diff(u_ref[...], pad_ref, l)
        y_ref[...] = _dot(d.astype(BF16), w_ref[...]) * ps_ref[...]

    return pl.pallas_call(
        body, grid=(nb,),
        in_specs=[pl.BlockSpec((l, POOL_W), lambda b: (b, 0)),
                  pl.BlockSpec((POOL_W, POOL_W), lambda b: (0, 0)),
                  pl.BlockSpec((1, POOL_W), lambda b: (0, 0))],
        out_specs=pl.BlockSpec((l, POOL_W), lambda b: (b, 0)),
        out_shape=jax.ShapeDtypeStruct((nb * l, POOL_W), F32),
        scratch_shapes=[pltpu.VMEM((POOL_PAD + l, POOL_W), F32)],
        compiler_params=_params("arbitrary"), name="pool_fwd")(proj, w_blk, pool_scale)


def pool_bwd(proj, d_ycat, w_blk, pool_scale, nb, l):
    def body(u_ref, dy_ref, w_ref, ps_ref, du_ref, dw_ref, dps_ref, pad_ref, padb_ref):
        b = pl.program_id(0)

        @pl.when(b == 0)
        def _():
            dw_ref[...] = jnp.zeros_like(dw_ref)
            dps_ref[...] = jnp.zeros_like(dps_ref)

        pad_ref[pl.ds(0, POOL_PAD), :] = jnp.zeros((POOL_PAD, POOL_W), F32)
        padb_ref[pl.ds(l, POOL_PAD), :] = jnp.zeros((POOL_PAD, POOL_W), F32)
        d, win = _pool_diff(u_ref[...], pad_ref, l)
        db = d.astype(BF16)
        w = w_ref[...]
        dy = dy_ref[...]
        dps_ref[...] += jnp.sum(dy * _dot(db, w), axis=0, keepdims=True)
        dyo = (dy * ps_ref[...]).astype(BF16)
        dw_ref[...] += _dot_tn(db, dyo)
        dd = _dot_nt(dyo, w)
        e = _pool_div_count(dd, win)
        padb_ref[pl.ds(0, l), :] = e
        f2 = e + padb_ref[pl.ds(1, l), :]
        padb_ref[pl.ds(0, l), :] = f2
        f4 = f2 + padb_ref[pl.ds(2, l), :]
        padb_ref[pl.ds(0, l), :] = f4
        f8 = f4 + padb_ref[pl.ds(4, l), :]
        padb_ref[pl.ds(0, l), :] = f8
        f16 = f8 + padb_ref[pl.ds(8, l), :]
        du_ref[...] = _pool_select(win, f2, f4, f8, f16) - dd

    return pl.pallas_call(
        body, grid=(nb,),
        in_specs=[pl.BlockSpec((l, POOL_W), lambda b: (b, 0)),
                  pl.BlockSpec((l, POOL_W), lambda b: (b, 0)),
                  pl.BlockSpec((POOL_W, POOL_W), lambda b: (0, 0)),
                  pl.BlockSpec((1, POOL_W), lambda b: (0, 0))],
        out_specs=[pl.BlockSpec((l, POOL_W), lambda b: (b, 0)),
                   pl.BlockSpec((POOL_W, POOL_W), lambda b: (0, 0)),
                   pl.BlockSpec((1, POOL_W), lambda b: (0, 0))],
        out_shape=[jax.ShapeDtypeStruct((nb * l, POOL_W), F32),
                   jax.ShapeDtypeStruct((POOL_W, POOL_W), F32),
                   jax.ShapeDtypeStruct((1, POOL_W), F32)],
        scratch_shapes=[pltpu.VMEM((POOL_PAD + l, POOL_W), F32), pltpu.VMEM((POOL_PAD + l, POOL_W), F32)],
        compiler_params=_params("arbitrary"), name="pool_bwd")(proj, d_ycat, w_blk, pool_scale)


def _discretise(a_re, a_im, ldt, b_re, b_im):
    dt = jnp.exp(ldt)
    mag = jnp.exp(a_re * dt)
    th = a_im * dt
    lr = mag * jnp.cos(th)
    li = mag * jnp.sin(th)
    nr = lr - 1.0
    den = a_re * a_re + a_im * a_im
    fr = (nr * a_re + li * a_im) / den
    fi = (li * a_re - nr * a_im) / den
    return lr, li, fr * b_re - fi * b_im, fr * b_im + fi * b_re


def _cmul(ar, ai, br, bi):
    return ar * br - ai * bi, ar * bi + ai * br


def s5_scan_consts(a_re_row, a_im_row, ldt_row):
    def body(ar_ref, ai_ref, ld_ref, f_ref, b_ref):
        dt = jnp.exp(ld_ref[...])
        mag = jnp.exp(ar_ref[...] * dt)
        th = ai_ref[...] * dt
        lr = mag * jnp.cos(th)
        li = mag * jnp.sin(th)
        r = lax.broadcasted_iota(jnp.int32, (SUBLANES, NST), 0)
        for out_ref, sign, rev in ((f_ref, 1.0, False), (b_ref, -1.0, True)):
            p = [(lr, sign * li)]
            for _ in range(SUBLANES - 1):
                p.append(_cmul(p[-1][0], p[-1][1], lr, sign * li))
            for idx, s in enumerate((1, 2, 4)):
                inside = (r < SUBLANES - s) if rev else (r >= s)
                out_ref[pl.ds(2 * idx * SUBLANES, SUBLANES), :] = jnp.where(inside, p[s - 1][0], 0.0)
                out_ref[pl.ds((2 * idx + 1) * SUBLANES, SUBLANES), :] = jnp.where(inside, p[s - 1][1], 0.0)
            pr = jnp.zeros((SUBLANES, NST), F32)
            pi = jnp.zeros((SUBLANES, NST), F32)
            for row in range(SUBLANES):
                power = (SUBLANES - row) if rev else (row + 1)
                pr = jnp.where(r == row, p[power - 1][0], pr)
                pi = jnp.where(r == row, p[power - 1][1], pi)
            out_ref[pl.ds(6 * SUBLANES, SUBLANES), :] = pr
            out_ref[pl.ds(7 * SUBLANES, SUBLANES), :] = pi

    shape = jax.ShapeDtypeStruct((8 * SUBLANES, NST), F32)
    return pl.pallas_call(body, out_shape=[shape, shape], name="s5_scan_consts")(a_re_row, a_im_row, ldt_row)


def s5_bbar(a_re_c, a_im_c, ldt_c, b_re2, b_im2):
    def body(ar, ai, ld, br, bi, o_re, o_im):
        _, _, bbr, bbi = _discretise(ar[...], ai[...], ld[...], br[...], bi[...])
        o_re[...] = bbr
        o_im[...] = bbi

    shape = jax.ShapeDtypeStruct((NST, SSM_GC), F32)
    return pl.pallas_call(body, out_shape=[shape, shape], name="s5_bbar")(a_re_c, a_im_c, ldt_c, b_re2, b_im2)


def s5_param_bwd(a_re_c, a_im_c, ldt_c, b_re2, b_im2, d_lr, d_li, d_bbr, d_bbi):
    def body(ar, ai, ld, br, bi, g_lr, g_li, g_br, g_bi, o_ar, o_ai, o_ld, o_br, o_bi):
        _, vjp = jax.vjp(_discretise, ar[...], ai[...], ld[...], br[...], bi[...])
        d_ar, d_ai, d_ld, d_br, d_bi = vjp((g_lr[...], g_li[...], g_br[...], g_bi[...]))
        o_ar[...] = d_ar
        o_ai[...] = d_ai
        o_ld[...] = jnp.sum(d_ld.reshape(SSM_NG, SSM_N, 1), axis=1)
        o_br[...] = d_br
        o_bi[...] = d_bi

    col = jax.ShapeDtypeStruct((NST, 1), F32)
    mat = jax.ShapeDtypeStruct((NST, SSM_GC), F32)
    grp = jax.ShapeDtypeStruct((SSM_NG, 1), F32)
    return pl.pallas_call(body, out_shape=[col, col, grp, mat, mat], name="s5_param_bwd")(
        a_re_c, a_im_c, ldt_c, b_re2, b_im2, d_lr, d_li, d_bbr, d_bbi)


def _scan_tiles(buf_ref, cst_ref, carry_ref, rows, reverse, h_ref=None, acc_ref=None, reset=None, unrolled=False):
    n_tiles = rows // SUBLANES
    shifts = ((1, 0), (2, 2), (4, 4))
    row_id = lax.broadcasted_iota(jnp.int32, (SUBLANES, SCAN_COLS), 0)
    for j in range(NST // SCAN_COLS):
        c_re = pl.ds(j * SCAN_COLS, SCAN_COLS)
        c_im = pl.ds(NST + j * SCAN_COLS, SCAN_COLS)

        def step(i, carry, c_re=c_re, c_im=c_im):
            tile = (n_tiles - 1 - i) if reverse else i
            start = tile * SUBLANES
            rws = pl.ds(start if unrolled else pl.multiple_of(start, SUBLANES), SUBLANES)
            xr = buf_ref[rws, c_re]
            xi = buf_ref[rws, c_im]
            for s, base in shifts:
                amount = (SUBLANES - s) if reverse else s
                sr = pltpu.roll(xr, amount, 0)
                si = pltpu.roll(xi, amount, 0)
                mr = cst_ref[pl.ds(base * SUBLANES, SUBLANES), c_re]
                mi = cst_ref[pl.ds((base + 1) * SUBLANES, SUBLANES), c_re]
                xr, xi = xr + (mr * sr - mi * si), xi + (mr * si + mi * sr)
            pr = cst_ref[pl.ds(6 * SUBLANES, SUBLANES), c_re]
            pi = cst_ref[pl.ds(7 * SUBLANES, SUBLANES), c_re]
            cr, ci = carry[0], carry[1]
            xr, xi = xr + (pr * cr - pi * ci), xi + (pr * ci + pi * cr)
            buf_ref[rws, c_re] = xr
            buf_ref[rws, c_im] = xi
            edge = 0 if reverse else SUBLANES - 1
            ncr = jnp.broadcast_to(xr[edge:edge + 1, :], (SUBLANES, SCAN_COLS))
            nci = jnp.broadcast_to(xi[edge:edge + 1, :], (SUBLANES, SCAN_COLS))
            if not reverse:
                return ncr, nci
            gnr = jnp.where(row_id == SUBLANES - 1, cr, pltpu.roll(xr, SUBLANES - 1, 0))
            gni = jnp.where(row_id == SUBLANES - 1, ci, pltpu.roll(xi, SUBLANES - 1, 0))
            hr = h_ref[rws, c_re]
            hi = h_ref[rws, c_im]
            return ncr, nci, carry[2] + (gnr * hr + gni * hi), carry[3] + (gni * hr - gnr * hi)

        init = (carry_ref[:, c_re], carry_ref[:, c_im])
        if reset is not None:
            init = tuple(jnp.where(reset, 0.0, v) for v in init)
        if reverse:
            init = init + (acc_ref[:, c_re], acc_ref[:, c_im])
        if unrolled:
            out = init
            for i in range(n_tiles):
                out = step(i, out)
        else:
            out = lax.fori_loop(0, n_tiles, step, init)
        carry_ref[:, c_re] = out[0]
        carry_ref[:, c_im] = out[1]
        if reverse:
            acc_ref[:, c_re] = out[2]
            acc_ref[:, c_im] = out[3]


def _state_block(v, m):
    return jnp.concatenate([v[:, m * BLK_ST:(m + 1) * BLK_ST], v[:, NST + m * BLK_ST:NST + (m + 1) * BLK_ST]], axis=1)


def _put_state_block(buf_ref, m, val):
    buf_ref[:, pl.ds(m * BLK_ST, BLK_ST)] = val[:, :BLK_ST]
    buf_ref[:, pl.ds(NST + m * BLK_ST, BLK_ST)] = val[:, BLK_ST:]


def s5_fwd(proj, b_m, c_m, cst, d_skip, w_glu, nb, l):
    tt = min(256, l)
    nt = l // tt
    n_chunks = nb * nt

    def body(u_ref, b_ref, c_ref, cst_ref, ds_ref, wg_ref, y_ref, h_ref, yp_ref, z_ref,
             buf0, buf1, ub0, ub1, carry_ref):
        s = pl.program_id(0)

        @pl.when(s == 0)
        def _():
            for r in (buf0, buf1, ub0, ub1, carry_ref):
                r[...] = jnp.zeros_like(r)

        def step(p_buf, p_u, q_buf):
            h = p_buf[...].astype(BF16)
            h_ref[...] = h
            ypre = (jnp.concatenate([_dot(_state_block(h, m), c_ref[m]) for m in range(N_BLK)], axis=1)
                    + ds_ref[...] * p_u[...])
            yp_ref[...] = ypre
            z = _dot(jax.nn.gelu(ypre).astype(BF16), wg_ref[...])
            z_ref[...] = z
            y_ref[...] = z[:, :SSM_W] * jax.nn.sigmoid(z[:, SSM_W:])
            u = u_ref[...]
            p_u[...] = u
            ub = u.astype(BF16)
            for m in range(N_BLK):
                _put_state_block(p_buf, m, _dot(ub[:, m * BLK_CH:(m + 1) * BLK_CH], b_ref[m]))
            _scan_tiles(q_buf, cst_ref, carry_ref, tt, reverse=False, reset=lax.rem(s + nt - 1, nt) == 0, unrolled=True)

        @pl.when(lax.rem(s, 2) == 0)
        def _():
            step(buf0, ub0, buf1)

        @pl.when(lax.rem(s, 2) == 1)
        def _():
            step(buf1, ub1, buf0)

    row_in = lambda s: (jnp.minimum(s, n_chunks - 1), 1)
    row_out = lambda s: (jnp.maximum(s - 2, 0), 0)
    const = lambda s: (0, 0)
    const3 = lambda s: (0, 0, 0)
    return pl.pallas_call(
        body, grid=(n_chunks + 2,),
        in_specs=[pl.BlockSpec((tt, SSM_W), row_in),
                  pl.BlockSpec((N_BLK, BLK_CH, 2 * BLK_ST), const3),
                  pl.BlockSpec((N_BLK, 2 * BLK_ST, BLK_CH), const3),
                  pl.BlockSpec((8 * SUBLANES, NST), const),
                  pl.BlockSpec((1, SSM_W), const),
                  pl.BlockSpec((SSM_W, 2 * SSM_W), const)],
        out_specs=[pl.BlockSpec((tt, SSM_W), row_out),
                   pl.BlockSpec((tt, 2 * NST), row_out),
                   pl.BlockSpec((tt, SSM_W), row_out),
                   pl.BlockSpec((tt, 2 * SSM_W), row_out)],
        out_shape=[jax.ShapeDtypeStruct((nb * l, SSM_W), F32),
                   jax.ShapeDtypeStruct((nb * l, 2 * NST), BF16),
                   jax.ShapeDtypeStruct((nb * l, SSM_W), F32),
                   jax.ShapeDtypeStruct((nb * l, 2 * SSM_W), F32)],
        scratch_shapes=[pltpu.VMEM((tt, 2 * NST), F32), pltpu.VMEM((tt, 2 * NST), F32),
                        pltpu.VMEM((tt, SSM_W), F32), pltpu.VMEM((tt, SSM_W), F32),
                        pltpu.VMEM((SUBLANES, 2 * NST), F32)],
        compiler_params=_params("arbitrary"), name="s5_fwd")(proj, b_m, c_m, cst, d_skip, w_glu)


def s5_bwd(proj, d_ycat, h, ypre, z, b_m, c_m, cst_rev, d_skip, w_glu, nb, l):
    tt = min(256, l)
    nt = l // tt
    n_chunks = nb * nt

    def body(u_ref, dy_ref, h_ref, yp_ref, z_ref, b_ref, c_ref, cst_ref, ds_ref, wg_ref,
             du_ref, dwg_ref, dds_ref, dcc_ref, dbb_ref, dlam_ref,
             buf0, buf1, hb0, hb1, ub0, ub1, dyp0, dyp1, carry_ref, dc_ref, db_ref):
        s = pl.program_id(0)

        @pl.when(s == 0)
        def _():
            for r in (buf0, buf1, hb0, hb1, ub0, ub1, dyp0, dyp1, carry_ref, dc_ref, db_ref, dwg_ref, dds_ref, dlam_ref):
                r[...] = jnp.zeros_like(r)

        def step(p_buf, p_h, p_u, p_dyp, q_buf, q_h):
            g = p_buf[...].astype(BF16)
            ub_done = p_u[...].astype(BF16)
            du = []
            for m in range(N_BLK):
                g_m = _state_block(g, m)
                db_ref[m] += _dot_tn(g_m, ub_done[:, m * BLK_CH:(m + 1) * BLK_CH])
                du.append(_dot_nt(g_m, b_ref[m]))
            du_ref[...] = jnp.concatenate(du, axis=1) + ds_ref[...] * p_dyp[...]
            u = u_ref[...]
            ypre = yp_ref[...]
            z = z_ref[...]
            z1 = z[:, :SSM_W]
            sg = jax.nn.sigmoid(z[:, SSM_W:])
            dy = dy_ref[...] * jnp.where(s < n_chunks, 1.0, 0.0)
            dz = jnp.concatenate([dy * sg, dy * z1 * sg * (1.0 - sg)], axis=1).astype(BF16)
            yg, gelu_vjp = jax.vjp(jax.nn.gelu, ypre)
            dwg_ref[...] += _dot_tn(yg.astype(BF16), dz)
            dypre = gelu_vjp(_dot_nt(dz, wg_ref[...]))[0]
            dds_ref[...] += jnp.sum(dypre * u, axis=0, keepdims=True)
            dyb = dypre.astype(BF16)
            hb = h_ref[...]
            p_h[...] = hb.astype(F32)
            p_u[...] = u
            p_dyp[...] = dypre
            for m in range(N_BLK):
                dy_m = dyb[:, m * BLK_CH:(m + 1) * BLK_CH]
                dc_ref[m] += _dot_tn(_state_block(hb, m), dy_m)
                _put_state_block(p_buf, m, _dot_nt(dy_m, c_ref[m]))
            _scan_tiles(q_buf, cst_ref, carry_ref, tt, reverse=True, h_ref=q_h, acc_ref=dlam_ref,
                        reset=lax.rem(s + nt - 1, nt) == 0, unrolled=True)

        @pl.when(lax.rem(s, 2) == 0)
        def _():
            step(buf0, hb0, ub0, dyp0, buf1, hb1)

        @pl.when(lax.rem(s, 2) == 1)
        def _():
            step(buf1, hb1, ub1, dyp1, buf0, hb0)

        @pl.when(s == n_chunks + 1)
        def _():
            dlam_ref[...] = jnp.broadcast_to(jnp.sum(dlam_ref[...], axis=0, keepdims=True), dlam_ref.shape)
            for acc_ref, out_ref in ((dc_ref, dcc_ref), (db_ref, dbb_ref)):
                for m in range(N_BLK):
                    for ri in range(2):
                        for gl in range(BLK_GROUPS):
                            out_ref[ri, pl.ds((m * BLK_GROUPS + gl) * SSM_N, SSM_N), :] = acc_ref[
                                m, pl.ds(ri * BLK_ST + gl * SSM_N, SSM_N), pl.ds(gl * SSM_GC, SSM_GC)]

    def chunk_rows(o):
        return lax.div(o, nt) * nt + (nt - 1 - lax.rem(o, nt))

    def rrow(col):
        return lambda s: (chunk_rows(jnp.minimum(s, n_chunks - 1)), col)

    const = lambda s: (0, 0)
    const3 = lambda s: (0, 0, 0)
    state_buf = pltpu.VMEM((tt, 2 * NST), F32)
    chan_buf = pltpu.VMEM((tt, SSM_W), F32)
    return pl.pallas_call(
        body, grid=(n_chunks + 2,),
        in_specs=[pl.BlockSpec((tt, SSM_W), rrow(1)),
                  pl.BlockSpec((tt, SSM_W), rrow(1)),
                  pl.BlockSpec((tt, 2 * NST), rrow(0)),
                  pl.BlockSpec((tt, SSM_W), rrow(0)),
                  pl.BlockSpec((tt, 2 * SSM_W), rrow(0)),
                  pl.BlockSpec((N_BLK, BLK_CH, 2 * BLK_ST), const3),
                  pl.BlockSpec((N_BLK, 2 * BLK_ST, BLK_CH), const3),
                  pl.BlockSpec((8 * SUBLANES, NST), const),
                  pl.BlockSpec((1, SSM_W), const),
                  pl.BlockSpec((SSM_W, 2 * SSM_W), const)],
        out_specs=[pl.BlockSpec((tt, SSM_W), lambda s: (chunk_rows(jnp.maximum(s - 2, 0)), 0)),
                   pl.BlockSpec((SSM_W, 2 * SSM_W), const),
                   pl.BlockSpec((1, SSM_W), const),
                   pl.BlockSpec((2, NST, SSM_GC), const3),
                   pl.BlockSpec((2, NST, SSM_GC), const3),
                   pl.BlockSpec((SUBLANES, 2 * NST), const)],
        out_shape=[jax.ShapeDtypeStruct((nb * l, SSM_W), F32),
                   jax.ShapeDtypeStruct((SSM_W, 2 * SSM_W), F32),
                   jax.ShapeDtypeStruct((1, SSM_W), F32),
                   jax.ShapeDtypeStruct((2, NST, SSM_GC), F32),
                   jax.ShapeDtypeStruct((2, NST, SSM_GC), F32),
                   jax.ShapeDtypeStruct((SUBLANES, 2 * NST), F32)],
        scratch_shapes=[state_buf, state_buf, state_buf, state_buf, chan_buf, chan_buf, chan_buf, chan_buf,
                        pltpu.VMEM((SUBLANES, 2 * NST), F32),
                        pltpu.VMEM((N_BLK, 2 * BLK_ST, BLK_CH), F32), pltpu.VMEM((N_BLK, 2 * BLK_ST, BLK_CH), F32)],
        compiler_params=_params("arbitrary"), name="s5_bwd")(
            proj, d_ycat, h, ypre, z, b_m, c_m, cst_rev, d_skip, w_glu)


def _head_mask(hh):
    lane = lax.broadcasted_iota(jnp.int32, (1, ATT_W), 1)
    return (lane >= hh * MEM_HD) & (lane < (hh + 1) * MEM_HD)


def _mem_kv(mem_ref, gm_ref, wkv_ref):
    m = mem_ref[0]
    mh = m * _rms_scale(m)
    mb = (mh * gm_ref[...]).astype(BF16)
    kv = _dot(mb, wkv_ref[...])
    return mh, mb, kv[:, :ATT_W].astype(BF16), kv[:, ATT_W:].astype(BF16)


def _softmax_rows(s):
    e = jnp.exp(s - jnp.max(s, axis=-1, keepdims=True))
    return e / jnp.sum(e, axis=-1, keepdims=True)


def attn_fwd(proj, mem, g_mem, w_kv, nb, l):
    tq = min(512, l)
    nq = l // tq
    scale = MEM_HD ** -0.5

    def body(q_ref, mem_ref, gm_ref, wkv_ref, o_ref, k_s, v_s):
        @pl.when(pl.program_id(1) == 0)
        def _():
            _, _, k, v = _mem_kv(mem_ref, gm_ref, wkv_ref)
            k_s[...] = k
            v_s[...] = v

        q = q_ref[...].astype(BF16)
        k = k_s[...]
        v = v_s[...]
        o = jnp.zeros((tq, ATT_W), F32)
        for hh in range(MEM_HEADS):
            msk = _head_mask(hh)
            p = _softmax_rows(_dot_nt(q, jnp.where(msk, k, 0)) * scale)
            o = o + _dot(p.astype(BF16), jnp.where(msk, v, 0))
        o_ref[...] = o

    const = lambda b, t: (0, 0)
    return pl.pallas_call(
        body, grid=(nb, nq),
        in_specs=[pl.BlockSpec((tq, ATT_W), lambda b, t: (b * nq + t, 3)),
                  pl.BlockSpec((1, N_MEM, D_MODEL), lambda b, t: (b, 0, 0)),
                  pl.BlockSpec((1, D_MODEL), const),
                  pl.BlockSpec((D_MODEL, 2 * ATT_W), const)],
        out_specs=pl.BlockSpec((tq, ATT_W), lambda b, t: (b * nq + t, 0)),
        out_shape=jax.ShapeDtypeStruct((nb * l, ATT_W), F32),
        scratch_shapes=[pltpu.VMEM((N_MEM, ATT_W), BF16), pltpu.VMEM((N_MEM, ATT_W), BF16)],
        compiler_params=_params("arbitrary", "arbitrary"), name="attn_fwd")(proj, mem, g_mem, w_kv)


def attn_bwd(proj, d_ycat, mem, g_mem, w_kv, nb, l):
    tq = min(512, l)
    nq = l // tq
    scale = MEM_HD ** -0.5

    def body(q_ref, do_ref, mem_ref, gm_ref, wkv_ref, dq_ref, dwkv_ref, dgm_ref, k_s, v_s, mb_s, dk_s, dv_s):
        b = pl.program_id(0)
        t = pl.program_id(1)

        @pl.when((b == 0) & (t == 0))
        def _():
            dwkv_ref[...] = jnp.zeros_like(dwkv_ref)
            dgm_ref[...] = jnp.zeros_like(dgm_ref)

        @pl.when(t == 0)
        def _():
            _, mb, k, v = _mem_kv(mem_ref, gm_ref, wkv_ref)
            k_s[...] = k
            v_s[...] = v
            mb_s[...] = mb
            dk_s[...] = jnp.zeros_like(dk_s)
            dv_s[...] = jnp.zeros_like(dv_s)

        q = q_ref[...].astype(BF16)
        do = do_ref[...].astype(BF16)
        k = k_s[...]
        v = v_s[...]
        dq = jnp.zeros((tq, ATT_W), F32)
        dk = jnp.zeros((N_MEM, ATT_W), F32)
        dv = jnp.zeros((N_MEM, ATT_W), F32)
        for hh in range(MEM_HEADS):
            msk = _head_mask(hh)
            kh = jnp.where(msk, k, 0)
            p = _softmax_rows(_dot_nt(q, kh) * scale)
            dp = _dot_nt(do, jnp.where(msk, v, 0))
            ds = (p * (dp - jnp.sum(dp * p, axis=-1, keepdims=True)) * scale).astype(BF16)
            dq = dq + _dot(ds, kh)
            dk = dk + jnp.where(msk, _dot_tn(ds, q), 0.0)
            dv = dv + jnp.where(msk, _dot_tn(p.astype(BF16), do), 0.0)
        dq_ref[...] = dq
        dk_s[...] += dk
        dv_s[...] += dv

        @pl.when(t == nq - 1)
        def _():
            dkv = jnp.concatenate([dk_s[...], dv_s[...]], axis=1).astype(BF16)
            dwkv_ref[...] += _dot_tn(mb_s[...], dkv)
            m = mem_ref[0]
            dgm_ref[...] += jnp.sum(_dot_nt(dkv, wkv_ref[...]) * (m * _rms_scale(m)), axis=0, keepdims=True)

    const = lambda b, t: (0, 0)
    return pl.pallas_call(
        body, grid=(nb, nq),
        in_specs=[pl.BlockSpec((tq, ATT_W), lambda b, t: (b * nq + t, 3)),
                  pl.BlockSpec((tq, ATT_W), lambda b, t: (b * nq + t, 3)),
                  pl.BlockSpec((1, N_MEM, D_MODEL), lambda b, t: (b, 0, 0)),
                  pl.BlockSpec((1, D_MODEL), const),
                  pl.BlockSpec((D_MODEL, 2 * ATT_W), const)],
        out_specs=[pl.BlockSpec((tq, ATT_W), lambda b, t: (b * nq + t, 0)),
                   pl.BlockSpec((D_MODEL, 2 * ATT_W), const),
                   pl.BlockSpec((1, D_MODEL), const)],
        out_shape=[jax.ShapeDtypeStruct((nb * l, ATT_W), F32),
                   jax.ShapeDtypeStruct((D_MODEL, 2 * ATT_W), F32),
                   jax.ShapeDtypeStruct((1, D_MODEL), F32)],
        scratch_shapes=[pltpu.VMEM((N_MEM, ATT_W), BF16), pltpu.VMEM((N_MEM, ATT_W), BF16),
                        pltpu.VMEM((N_MEM, D_MODEL), BF16),
                        pltpu.VMEM((N_MEM, ATT_W), F32), pltpu.VMEM((N_MEM, ATT_W), F32)],
        compiler_params=_params("arbitrary", "arbitrary"), name="attn_bwd")(proj, d_ycat, mem, g_mem, w_kv)


def tail(y_pool, y_ssm, y_att, proj, x2, target, w_out, g_post):
    t = x2.shape[0]
    tm = min(256, t)

    def body(yp_ref, ys_ref, ya_ref, gate_ref, x_ref, tg_ref, w_ref, gp_ref,
             dz_ref, dyc_ref, dgate_ref, dw_ref, dgp_ref, loss_ref):
        @pl.when(pl.program_id(0) == 0)
        def _():
            dw_ref[...] = jnp.zeros_like(dw_ref)
            dgp_ref[...] = jnp.zeros_like(dgp_ref)
            loss_ref[...] = jnp.zeros_like(loss_ref)

        ycat = jnp.concatenate([yp_ref[...], ys_ref[...], ya_ref[...]], axis=1)
        gate = gate_ref[...]
        sg = jax.nn.sigmoid(gate)
        silu = gate * sg
        yb = (ycat * silu).astype(BF16)
        w = w_ref[...]
        out = _dot(yb, w)
        r2 = _rms_scale(out)
        oh = out * r2
        gp = gp_ref[...]
        err = (x_ref[...] + oh * gp) - tg_ref[...]
        loss_ref[...] += 0.5 * jnp.sum(jnp.mean(err * err, axis=-1, keepdims=True), axis=0, keepdims=True)
        dz = err * (1.0 / D_MODEL)
        dz_ref[...] = dz
        dgp_ref[...] += jnp.sum(dz * oh, axis=0, keepdims=True)
        dn = dz * gp
        dout = (r2 * (dn - oh * jnp.mean(dn * oh, axis=-1, keepdims=True))).astype(BF16)
        dw_ref[...] += _dot_tn(yb, dout)
        dy = _dot_nt(dout, w)
        dyc_ref[...] = dy * silu
        dgate_ref[...] = dy * ycat * (sg * (1.0 + gate * (1.0 - sg)))

    row = lambda i: (i, 0)
    const = lambda i: (0, 0)
    full = jax.ShapeDtypeStruct((t, D_MODEL), F32)
    return pl.pallas_call(
        body, grid=(t // tm,),
        in_specs=[pl.BlockSpec((tm, POOL_W), row),
                  pl.BlockSpec((tm, SSM_W), row),
                  pl.BlockSpec((tm, ATT_W), row),
                  pl.BlockSpec((tm, D_MODEL), lambda i: (i, 1)),
                  pl.BlockSpec((tm, D_MODEL), row),
                  pl.BlockSpec((tm, D_MODEL), row),
                  pl.BlockSpec((D_MODEL, D_MODEL), const),
                  pl.BlockSpec((1, D_MODEL), const)],
        out_specs=[pl.BlockSpec((tm, D_MODEL), row),
                   pl.BlockSpec((tm, D_MODEL), row),
                   pl.BlockSpec((tm, D_MODEL), row),
                   pl.BlockSpec((D_MODEL, D_MODEL), const),
                   pl.BlockSpec((1, D_MODEL), const),
                   pl.BlockSpec((1, LANES), const)],
        out_shape=[full, full, full,
                   jax.ShapeDtypeStruct((D_MODEL, D_MODEL), F32),
                   jax.ShapeDtypeStruct((1, D_MODEL), F32),
                   jax.ShapeDtypeStruct((1, LANES), F32)],
        compiler_params=_params("arbitrary"), name="tail")(y_pool, y_ssm, y_att, proj, x2, target, w_out, g_post)


def in_proj_bwd(du_pool, du_ssm, dq, d_gate, x2, dz, g_pre, w_in):
    t = x2.shape[0]
    tm = min(256, t)
    shard = W_IN_SHARD

    def body(dup_ref, dus_ref, dq_ref, dg_ref, x_ref, dz_ref, g_ref, w_ref, gx_ref, dw_ref, dgp_ref):
        @pl.when(pl.program_id(0) == 0)
        def _():
            dw_ref[...] = jnp.zeros_like(dw_ref)
            dgp_ref[...] = jnp.zeros_like(dgp_ref)

        dproj = jnp.concatenate([dup_ref[...], dus_ref[...], dq_ref[...], dg_ref[...]], axis=1).astype(BF16)
        x = x_ref[...]
        r1 = _rms_scale(x)
        xh = x * r1
        g = g_ref[...]
        hb = (xh * g).astype(BF16)
        dh = jnp.zeros((tm, D_MODEL), F32)
        for k in range(N_CHIPS):
            dp_k = dproj[:, k * shard:(k + 1) * shard]
            dw_ref[k] += _dot_tn(hb, dp_k)
            dh = dh + _dot_nt(dp_k, w_ref[k])
        dgp_ref[...] += jnp.sum(dh * xh, axis=0, keepdims=True)
        dn = dh * g
        gx_ref[...] = dz_ref[...] + r1 * (dn - xh * jnp.mean(dn * xh, axis=-1, keepdims=True))

    row = lambda i: (i, 0)
    const = lambda i: (0, 0)
    return pl.pallas_call(
        body, grid=(t // tm,),
        in_specs=[pl.BlockSpec((tm, POOL_W), row),
                  pl.BlockSpec((tm, SSM_W), row),
                  pl.BlockSpec((tm, ATT_W), row),
                  pl.BlockSpec((tm, D_MODEL), row),
                  pl.BlockSpec((tm, D_MODEL), row),
                  pl.BlockSpec((tm, D_MODEL), row),
                  pl.BlockSpec((1, D_MODEL), const),
                  pl.BlockSpec((N_CHIPS, D_MODEL, shard), lambda i: (0, 0, 0))],
        out_specs=[pl.BlockSpec((tm, D_MODEL), row),
                   pl.BlockSpec((N_CHIPS, D_MODEL, shard), lambda i: (0, 0, 0)),
                   pl.BlockSpec((1, D_MODEL), const)],
        out_shape=[jax.ShapeDtypeStruct((t, D_MODEL), F32),
                   jax.ShapeDtypeStruct((N_CHIPS, D_MODEL, shard), F32),
                   jax.ShapeDtypeStruct((1, D_MODEL), F32)],
        compiler_params=_params("arbitrary"), name="in_proj_bwd")(du_pool, du_ssm, dq, d_gate, x2, dz, g_pre, w_in)


def _block_diag(blocks):
    g, r, c = blocks.shape
    eye = jnp.eye(g, dtype=blocks.dtype)
    return jnp.einsum("grc,gh->grhc", blocks, eye).reshape(g * r, g * c)


def _block_diag_extract(mat, g, r, c):
    eye = jnp.eye(g, dtype=mat.dtype)
    return jnp.einsum("grhc,gh->grc", mat.reshape(g, r, g, c), eye)


def local_step(x, mem, target, g_pre, w_in, w_pool, pool_scale, a_re, a_im, log_dt, b_re, b_im, c_re, c_im, d_skip,
               w_glu, g_mem, w_kv, w_out, g_post):
    nb, l, _ = x.shape
    x2 = x.reshape(nb * l, D_MODEL)
    tg2 = target.reshape(nb * l, D_MODEL)

    rowv = lambda a: a.reshape(1, NST)
    colv = lambda a: a.reshape(NST, 1)
    ldt_full = jnp.broadcast_to(log_dt.reshape(SSM_NG, 1), (SSM_NG, SSM_N))
    b_re2 = b_re.reshape(NST, SSM_GC)
    b_im2 = b_im.reshape(NST, SSM_GC)
    cst_f, cst_b = s5_scan_consts(rowv(a_re), rowv(a_im), rowv(ldt_full))
    bbr, bbi = s5_bbar(colv(a_re), colv(a_im), colv(ldt_full), b_re2, b_im2)
    eye = jnp.eye(BLK_GROUPS, dtype=F32)
    blk_in = lambda bb: jnp.einsum("mgnc,gh->mhcgn", bb.reshape(N_BLK, BLK_GROUPS, SSM_N, SSM_GC), eye).reshape(
        N_BLK, BLK_CH, BLK_ST)
    blk_out = lambda cc: jnp.einsum("mgcn,gh->mgnhc", cc.reshape(N_BLK, BLK_GROUPS, SSM_GC, SSM_N), eye).reshape(
        N_BLK, BLK_ST, BLK_CH)
    b_m = jnp.concatenate([blk_in(bbr), blk_in(bbi)], axis=2).astype(BF16)
    c_m = jnp.concatenate([blk_out(c_re), -blk_out(c_im)], axis=1).astype(BF16)
    w_pool_blk = _block_diag(w_pool.reshape(4, POOL_GW, POOL_GW)).astype(BF16)

    proj = in_proj(x2, g_pre, w_in)
    y_pool = pool_fwd(proj, w_pool_blk, pool_scale, nb, l)
    y_ssm, h, ypre, z = s5_fwd(proj, b_m, c_m, cst_f, d_skip, w_glu, nb, l)
    y_att = attn_fwd(proj, mem, g_mem, w_kv, nb, l)
    dz, d_ycat, d_gate, d_w_out, d_g_post, loss = tail(y_pool, y_ssm, y_att, proj, x2, tg2, w_out, g_post)
    du_pool, d_wp_blk, d_pool_scale = pool_bwd(proj, d_ycat, w_pool_blk, pool_scale, nb, l)
    du_ssm, d_w_glu, d_d_skip, d_cc, d_bb, d_lam = s5_bwd(
        proj, d_ycat, h, ypre, z, b_m, c_m, cst_b, d_skip, w_glu, nb, l)
    dq, d_w_kv, d_g_mem = attn_bwd(proj, d_ycat, mem, g_mem, w_kv, nb, l)
    grad_x, d_w_in, d_g_pre = in_proj_bwd(du_pool, du_ssm, dq, d_gate, x2, dz, g_pre, w_in)

    d_c = d_cc[0].reshape(SSM_NG, SSM_N, SSM_GC)
    d_ci = d_cc[1].reshape(SSM_NG, SSM_N, SSM_GC)
    d_lam_row = d_lam[0]
    d_ar, d_ai, d_ld, d_br, d_bi = s5_param_bwd(
        colv(a_re), colv(a_im), colv(ldt_full), b_re2, b_im2,
        d_lam_row[:NST].reshape(NST, 1), d_lam_row[NST:].reshape(NST, 1), d_bb[0], d_bb[1])

    grads = {
        "g_pre": d_g_pre,
        "w_in": d_w_in,
        "w_pool": _block_diag_extract(d_wp_blk, 4, POOL_GW, POOL_GW).reshape(1, 4, POOL_GW, POOL_GW),
        "pool_scale": d_pool_scale,
        "a_re": d_ar.reshape(1, SSM_NG, SSM_N),
        "a_im": d_ai.reshape(1, SSM_NG, SSM_N),
        "log_dt": d_ld.reshape(1, SSM_NG),
        "b_re": d_br.reshape(1, SSM_NG, SSM_N, SSM_GC),
        "b_im": d_bi.reshape(1, SSM_NG, SSM_N, SSM_GC),
        "c_re": d_c.transpose(0, 2, 1).reshape(1, SSM_NG, SSM_GC, SSM_N),
        "c_im": (-d_ci).transpose(0, 2, 1).reshape(1, SSM_NG, SSM_GC, SSM_N),
        "d_skip": d_d_skip,
        "w_glu": d_w_glu,
        "g_mem": d_g_mem,
        "w_kv": d_w_kv,
        "w_out": d_w_out,
        "g_post": d_g_post,
    }
    return loss, grad_x.reshape(nb, l, D_MODEL), grads


VMEM_SPEC = pl.BlockSpec(memory_space=pltpu.VMEM)
ANY_SPEC = pl.BlockSpec(memory_space=pl.ANY)


def _place():
    return lax.axis_index("x"), lax.axis_index("y"), lax.axis_index("c")


def _other_chips(x, y):
    return [(1 - x, y), (x, 1 - y), (1 - x, 1 - y)]


def gather_weights(shards):
    n = len(shards)

    def body(*refs):
        ins, outs = refs[:n], refs[n:2 * n]
        send_sems, recv_sems = refs[2 * n:]
        x, y, c = _place()
        k = 2 * x + y
        chips = _other_chips(x, y)
        for i in range(n):
            outs[i][k] = ins[i][...].astype(BF16)

        def copy(i, block, core_half, to, sem):
            hr = shards[i].shape[0] // 2
            rows = outs[i].at[block, pl.ds(pl.multiple_of(core_half * hr, 2 * SUBLANES), hr)]
            return pltpu.make_async_remote_copy(
                src_ref=rows, dst_ref=rows, send_sem=send_sems.at[sem], recv_sem=recv_sems.at[sem],
                device_id=to, device_id_type=MESH)

        pairs = [(i, j) for i in range(n) for j in range(3)]
        first = [copy(i, k, c, (*chips[j], c), i * 3 + j) for i, j in pairs]
        for cp in first:
            cp.start()
        passed = []
        for i, j in pairs:
            px, py = chips[j]
            copy(i, 2 * px + py, c, (px, py, c), i * 3 + j).wait_recv()
            fw = copy(i, 2 * px + py, c, (x, y, 1 - c), 3 * n + i * 3 + j)
            fw.start()
            passed.append(fw)
        for i, j in pairs:
            px, py = chips[j]
            copy(i, 2 * px + py, 1 - c, (x, y, 1 - c), 3 * n + i * 3 + j).wait_recv()
        for cp in first + passed:
            cp.wait_send()

    return pl.pallas_call(
        body,
        out_shape=[jax.ShapeDtypeStruct((N_CHIPS,) + s.shape, BF16) for s in shards],
        in_specs=[VMEM_SPEC] * n, out_specs=[VMEM_SPEC] * n,
        scratch_shapes=[pltpu.SemaphoreType.DMA((6 * n,)), pltpu.SemaphoreType.DMA((6 * n,))],
        compiler_params=pltpu.CompilerParams(vmem_limit_bytes=VMEM_LIMIT),
        name="gather_weights")(*shards)


SEMS_PER_ITEM = 5


def reduce_all(g4s, packed):
    n = len(g4s)
    dims = [g.shape[1:] for g in g4s]
    pr = packed.shape[0]

    def body(*refs):
        g_refs, p_ref = refs[:n], refs[n]
        outs, op = refs[n + 1:2 * n + 1], refs[2 * n + 1]
        scr = refs[2 * n + 2:]
        mine, theirs, sendb, recvb = scr[0:n], scr[n:2 * n], scr[2 * n:3 * n], scr[3 * n:4 * n]
        p_theirs, gat, lsems, send_sems, recv_sems = scr[4 * n:]
        x, y, c = _place()
        k = 2 * x + y
        chips = _other_chips(x, y)
        sib = (x, y, 1 - c)

        def halves(rows):
            hr = rows // 2
            return (pl.ds(pl.multiple_of(c * hr, SUBLANES), hr), pl.ds(pl.multiple_of((1 - c) * hr, SUBLANES), hr))

        def remote(src, dst, sem, to):
            return pltpu.make_async_remote_copy(
                src_ref=src, dst_ref=dst, send_sem=send_sems.at[sem], recv_sem=recv_sems.at[sem],
                device_id=to, device_id_type=MESH)

        loads, started = [], []
        for i in range(n):
            my_rows, sib_rows = halves(dims[i][0])
            ld = pltpu.make_async_copy(g_refs[i].at[:, my_rows, :], mine[i], lsems.at[i])
            ld.start()
            loads.append(ld)
            sw = remote(g_refs[i].at[:, sib_rows, :], theirs[i], SEMS_PER_ITEM * i, sib)
            sw.start()
            started.append(sw)
        p_my, p_sib = halves(pr)
        p_swap = remote(p_ref.at[p_sib, :], p_theirs, SEMS_PER_ITEM * n, sib)
        p_swap.start()
        started.append(p_swap)

        p_swap.wait_recv()
        gat[k] = p_ref[p_my, :] + p_theirs[...]
        for j, (px, py) in enumerate(chips):
            cp = remote(gat.at[k], gat.at[k], SEMS_PER_ITEM * n + 1 + j, (px, py, c))
            cp.start()
            started.append(cp)
        ici = []
        for i in range(n):
            loads[i].wait()
            started[i].wait_recv()
            for j, (px, py) in enumerate(chips):
                kp = 2 * px + py
                sendb[i][j] = (mine[i][kp] + theirs[i][kp]).astype(BF16)
                cp = remote(sendb[i].at[j], recvb[i].at[j], SEMS_PER_ITEM * i + 1 + j, (px, py, c))
                cp.start()
                ici.append(cp)

        for j, (px, py) in enumerate(chips):
            kp = 2 * px + py
            remote(gat.at[kp], gat.at[kp], SEMS_PER_ITEM * n + 1 + j, (px, py, c)).wait_recv()
        op[p_my, :] = ((gat[0] + gat[1]) + gat[2]) + gat[3]
        last = remote(op.at[p_my, :], op.at[p_my, :], SEMS_PER_ITEM * n + 4, sib)
        last.start()
        started.append(last)
        for i in range(n):
            my_rows, _ = halves(dims[i][0])
            half = mine[i][k] + theirs[i][k]
            for j in range(3):
                ici[3 * i + j].wait_recv()
                half = half + recvb[i][j].astype(F32)
            outs[i][my_rows, :] = half
            last = remote(outs[i].at[my_rows, :], outs[i].at[my_rows, :], SEMS_PER_ITEM * i + 4, sib)
            last.start()
            started.append(last)
        remote(op.at[p_sib, :], op.at[p_sib, :], SEMS_PER_ITEM * n + 4, sib).wait_recv()
        for i in range(n):
            _, sib_rows = halves(dims[i][0])
            remote(outs[i].at[sib_rows, :], outs[i].at[sib_rows, :], SEMS_PER_ITEM * i + 4, sib).wait_recv()
        for cp in started + ici:
            cp.wait_send()

    n_sems = SEMS_PER_ITEM * (n + 1)
    scratch = ([pltpu.VMEM((N_CHIPS, r // 2, cd), F32) for r, cd in dims] * 2
               + [pltpu.VMEM((3, r // 2, cd), BF16) for r, cd in dims] * 2
               + [pltpu.VMEM((pr // 2, LANES), F32), pltpu.VMEM((N_CHIPS, pr // 2, LANES), F32),
                  pltpu.SemaphoreType.DMA((n,)), pltpu.SemaphoreType.DMA((n_sems,)), pltpu.SemaphoreType.DMA((n_sems,))])
    res = pl.pallas_call(
        body,
        out_shape=[jax.ShapeDtypeStruct(d, F32) for d in dims] + [jax.ShapeDtypeStruct(packed.shape, F32)],
        in_specs=[ANY_SPEC] * n + [VMEM_SPEC], out_specs=[VMEM_SPEC] * (n + 1),
        scratch_shapes=scratch,
        compiler_params=pltpu.CompilerParams(vmem_limit_bytes=VMEM_LIMIT),
        name="reduce_all")(*g4s, packed)
    return res[:n], res[n]


def adamw_all(ws, gs, ms, vs):
    n = len(ws)

    def body(*refs):
        w, g, m, v = refs[:n], refs[n:2 * n], refs[2 * n:3 * n], refs[3 * n:4 * n]
        od, om, ov = refs[4 * n:5 * n], refs[5 * n:6 * n], refs[6 * n:]
        for i in range(n):
            od[i][...], om[i][...], ov[i][...] = _adamw(w[i][...], g[i][...], m[i][...], v[i][...])

    shapes = [jax.ShapeDtypeStruct(a.shape, F32) for a in ws]
    res = pl.pallas_call(
        body, out_shape=shapes * 3, in_specs=[VMEM_SPEC] * (4 * n), out_specs=[VMEM_SPEC] * (3 * n),
        compiler_params=pltpu.CompilerParams(vmem_limit_bytes=VMEM_LIMIT),
        name="adamw_all")(*ws, *gs, *ms, *vs)
    return res[:n], res[n:2 * n], res[2 * n:]


WEIGHTS = ("g_pre", "w_in", "w_pool", "pool_scale", "a_re", "a_im", "log_dt", "b_re", "b_im", "c_re", "c_im", "d_skip",
           "w_glu", "g_mem", "w_kv", "w_out", "g_post")
SHARDED = ("w_in", "w_glu", "w_kv", "w_out")
REPLICATED = tuple(n for n in WEIGHTS if n not in SHARDED)
PACK_TILE = SUBLANES * LANES


def _pack(arrays):
    parts = []
    for a in arrays:
        flat = a.reshape(-1)
        parts.append(jnp.pad(flat, (0, -flat.shape[0] % PACK_TILE)).reshape(-1, LANES))
    rows = sum(p.shape[0] for p in parts)
    if rows % (2 * SUBLANES):
        parts.append(jnp.zeros((SUBLANES, LANES), F32))
    return jnp.concatenate(parts, axis=0)


def _unpack(packed, shapes):
    out, row = [], 0
    for shp in shapes:
        size = math.prod(shp)
        rows = -(-size // PACK_TILE) * SUBLANES
        out.append(packed[row:row + rows].reshape(-1)[:size].reshape(shp))
        row += rows
    return out


def _chip_major(a, n_cols):
    return a.reshape(a.shape[0], N_CHIPS, n_cols).transpose(1, 0, 2)


def _from_chip_major(a):
    return a.transpose(1, 0, 2).reshape(a.shape[1], N_CHIPS * a.shape[2])


def kernel(x, mem, g_pre, w_in, w_pool, pool_scale, a_re, a_im, log_dt, b_re, b_im, c_re, c_im, d_skip, w_glu, g_mem, w_kv, w_out, g_post, loss_target, m_g_pre, m_w_in, m_w_pool, m_pool_scale, m_a_re, m_a_im, m_log_dt, m_b_re, m_b_im, m_c_re, m_c_im, m_d_skip, m_w_glu, m_g_mem, m_w_kv, m_w_out, m_g_post, v_g_pre, v_w_in, v_w_pool, v_pool_scale, v_a_re, v_a_im, v_log_dt, v_b_re, v_b_im, v_c_re, v_c_im, v_d_skip, v_w_glu, v_g_mem, v_w_kv, v_w_out, v_g_post):
    given = dict(locals())
    wts = {n: given[n] for n in WEIGHTS}
    mom = {n: given["m_" + n] for n in WEIGHTS}
    var = {n: given["v_" + n] for n in WEIGHTS}

    g_in, g_glu, g_kv, g_out = gather_weights([wts[n][0] for n in SHARDED])
    w_in_full = g_in
    w_glu_full = _from_chip_major(g_glu)
    w_kv_full = g_kv.reshape(D_MODEL, 2 * ATT_W)
    w_out_full = g_out.reshape(D_MODEL, D_MODEL)

    loss_part, grad_x, grads = local_step(
        x, mem, loss_target, g_pre, w_in_full, w_pool[0], pool_scale, a_re[0], a_im[0], log_dt[0], b_re[0], b_im[0],
        c_re[0], c_im[0], d_skip, w_glu_full, g_mem, w_kv_full, w_out_full, g_post)

    partial4 = [
        grads["w_in"],
        _chip_major(grads["w_glu"], 2 * SSM_W // N_CHIPS),
        grads["w_kv"].reshape(N_CHIPS, D_MODEL // N_CHIPS, 2 * ATT_W),
        grads["w_out"].reshape(N_CHIPS, D_MODEL // N_CHIPS, D_MODEL),
    ]
    loss_tile = jnp.broadcast_to(loss_part, (SUBLANES, LANES))
    sharded_g, packed_g = reduce_all(partial4, _pack([grads[n] for n in REPLICATED] + [loss_tile]))
    small_g = _unpack(packed_g, [wts[n].shape for n in REPLICATED] + [(SUBLANES, LANES)])
    loss = small_g[-1][0, 0]
    grad = dict(zip(SHARDED, [g[None] for g in sharded_g]))
    grad.update(zip(REPLICATED, small_g[:-1]))

    deltas, new_ms, new_vs = adamw_all([wts[n] for n in WEIGHTS], [grad[n] for n in WEIGHTS],
                                       [mom[n] for n in WEIGHTS], [var[n] for n in WEIGHTS])
    return (loss, grad_x, *[grad[n] for n in WEIGHTS], *deltas, *new_ms, *new_vs)
```

```python
import functools
import math

import jax
import jax.numpy as jnp
from jax import lax
from jax.experimental import pallas as pl
from jax.experimental.pallas import tpu as pltpu

F32 = jnp.float32
BF16 = jnp.bfloat16

D_MODEL = 1024
POOL_W = 384
SSM_W = 384
ATT_W = 256
POOL_GW = 96
POOL_WINDOWS = (2, 4, 8, 16)
POOL_PAD = 16
SSM_NG = 24
SSM_N = 64
SSM_GC = 16
N_CHIPS = 4
N_DEV = 8
W_IN_SHARD = 2 * D_MODEL // N_CHIPS
NST = SSM_NG * SSM_N
BLK_CH = 128
BLK_GROUPS = BLK_CH // SSM_GC
BLK_ST = BLK_GROUPS * SSM_N
N_BLK = SSM_W // BLK_CH
N_MEM = 256
MEM_HEADS = 4
MEM_HD = 64
EPS = 1e-6

ADAM_LR = 0.001
ADAM_B1 = 0.9
ADAM_B2 = 0.999
ADAM_EPS = 1e-08
ADAM_WD = 0.01
ADAM_STEP = 10

SUBLANES = 8
LANES = 128
V7X_VMEM_BYTES = 64 * 2**20
VMEM_LIMIT = V7X_VMEM_BYTES - 8 * 2**20
SCAN_COLS = 512
ROW_TILE = 256
MESH = pl.DeviceIdType.MESH

NT = (((1,), (1,)), ((), ()))
TN = (((0,), (0,)), ((), ()))


def _params(*sem):
    return pltpu.CompilerParams(dimension_semantics=sem, vmem_limit_bytes=VMEM_LIMIT)


def _dot(a, b):
    return jnp.dot(a, b, preferred_element_type=F32)


def _dot_nt(a, b):
    return lax.dot_general(a, b, NT, preferred_element_type=F32)


def _dot_tn(a, b):
    return lax.dot_general(a, b, TN, preferred_element_type=F32)


def _rms_scale(v):
    return lax.rsqrt(jnp.mean(v * v, axis=-1, keepdims=True) + EPS)


def _adamw(w, g, m, v):
    m = ADAM_B1 * m + (1.0 - ADAM_B1) * g
    v = ADAM_B2 * v + (1.0 - ADAM_B2) * (g * g)
    m_hat = m / (1.0 - ADAM_B1 ** ADAM_STEP)
    v_hat = v / (1.0 - ADAM_B2 ** ADAM_STEP)
    delta = -ADAM_LR * (m_hat / (jnp.sqrt(v_hat) + ADAM_EPS) + ADAM_WD * w)
    return delta, m, v


def in_proj(x2, g_pre, w_in):
    t = x2.shape[0]
    tm = min(512, t)

    def body(x_ref, g_ref, w_ref, o_ref):
        x = x_ref[...]
        h = (x * _rms_scale(x) * g_ref[...]).astype(BF16)
        for k in range(N_CHIPS):
            o_ref[:, pl.ds(k * W_IN_SHARD, W_IN_SHARD)] = _dot(h, w_ref[k])

    return pl.pallas_call(
        body, grid=(t // tm,),
        in_specs=[pl.BlockSpec((tm, D_MODEL), lambda i: (i, 0)),
                  pl.BlockSpec((1, D_MODEL), lambda i: (0, 0)),
                  pl.BlockSpec((N_CHIPS, D_MODEL, W_IN_SHARD), lambda i: (0, 0, 0))],
        out_specs=pl.BlockSpec((tm, 2 * D_MODEL), lambda i: (i, 0)),
        out_shape=jax.ShapeDtypeStruct((t, 2 * D_MODEL), F32),
        compiler_params=_params("arbitrary"), name="in_proj")(x2, g_pre, w_in)


def _pool_lane_window(shape):
    ch = lax.broadcasted_iota(jnp.int32, shape, 1)
    return jnp.where(ch < POOL_GW, 2.0, jnp.where(ch < 2 * POOL_GW, 4.0, jnp.where(ch < 3 * POOL_GW, 8.0, 16.0)))


def _pool_select(win, s2, s4, s8, s16):
    return jnp.where(win == 2.0, s2, jnp.where(win == 4.0, s4, jnp.where(win == 8.0, s8, s16)))


def _pool_div_count(v, win):
    pos = (lax.broadcasted_iota(jnp.int32, (POOL_PAD, POOL_W), 0) + 1).astype(F32)
    head = v[:POOL_PAD] / jnp.minimum(pos, win)
    return jnp.concatenate([head, v[POOL_PAD:] * (1.0 / win)], axis=0)


def _pool_diff(u, pad_ref, l):
    lo = POOL_PAD
    pad_ref[pl.ds(lo, l), :] = u
    s2 = u + pad_ref[pl.ds(lo - 1, l), :]
    pad_ref[pl.ds(lo, l), :] = s2
    s4 = s2 + pad_ref[pl.ds(lo - 2, l), :]
    pad_ref[pl.ds(lo, l), :] = s4
    s8 = s4 + pad_ref[pl.ds(lo - 4, l), :]
    pad_ref[pl.ds(lo, l), :] = s8
    s16 = s8 + pad_ref[pl.ds(lo - 8, l), :]
    win = _pool_lane_window((1, POOL_W))
    return _pool_div_count(_pool_select(win, s2, s4, s8, s16), win) - u, win


def pool_fwd(proj, w_blk, pool_scale, nb, l):
    def body(u_ref, w_ref, ps_ref, y_ref, pad_ref):
        pad_ref[pl.ds(0, POOL_PAD), :] = jnp.zeros((POOL_PAD, POOL_W), F32)
        d, _ = _pool_diff(u_ref[...], pad_ref, l)
        y_ref[...] = _dot(d.astype(BF16), w_ref[...]) * ps_ref[...]

    return pl.pallas_call(
        body, grid=(nb,),
        in_specs=[pl.BlockSpec((l, POOL_W), lambda b: (b, 0)),
                  pl.BlockSpec((POOL_W, POOL_W), lambda b: (0, 0)),
                  pl.BlockSpec((1, POOL_W), lambda b: (0, 0))],
        out_specs=pl.BlockSpec((l, POOL_W), lambda b: (b, 0)),
        out_shape=jax.ShapeDtypeStruct((nb * l, POOL_W), F32),
        scratch_shapes=[pltpu.VMEM((POOL_PAD + l, POOL_W), F32)],
        compiler_params=_params("arbitrary"), name="pool_fwd")(proj, w_blk, pool_scale)


def pool_bwd(proj, d_ycat, w_blk, pool_scale, nb, l):
    def body(u_ref, dy_ref, w_ref, ps_ref, du_ref, dw_ref, dps_ref, pad_ref, padb_ref):
        b = pl.program_id(0)

        @pl.when(b == 0)
        def _():
            dw_ref[...] = jnp.zeros_like(dw_ref)
            dps_ref[...] = jnp.zeros_like(dps_ref)

        pad_ref[pl.ds(0, POOL_PAD), :] = jnp.zeros((POOL_PAD, POOL_W), F32)
        padb_ref[pl.ds(l, POOL_PAD), :] = jnp.zeros((POOL_PAD, POOL_W), F32)
        d, win = _pool_diff(u_ref[...], pad_ref, l)
        db = d.astype(BF16)
        w = w_ref[...]
        dy = dy_ref[...]
        dps_ref[...] += jnp.sum(dy * _dot(db, w), axis=0, keepdims=True)
        dyo = (dy * ps_ref[...]).astype(BF16)
        dw_ref[...] += _dot_tn(db, dyo)
        dd = _dot_nt(dyo, w)
        e = _pool_div_count(dd, win)
        padb_ref[pl.ds(0, l), :] = e
        f2 = e + padb_ref[pl.ds(1, l), :]
        padb_ref[pl.ds(0, l), :] = f2
        f4 = f2 + padb_ref[pl.ds(2, l), :]
        padb_ref[pl.ds(0, l), :] = f4
        f8 = f4 + padb_ref[pl.ds(4, l), :]
        padb_ref[pl.ds(0, l), :] = f8
        f16 = f8 + padb_ref[pl.ds(8, l), :]
        du_ref[...] = (_pool_select(win, f2, f4, f8, f16) - dd).astype(BF16)

    return pl.pallas_call(
        body, grid=(nb,),
        in_specs=[pl.BlockSpec((l, POOL_W), lambda b: (b, 0)),
                  pl.BlockSpec((l, POOL_W), lambda b: (b, 0)),
                  pl.BlockSpec((POOL_W, POOL_W), lambda b: (0, 0)),
                  pl.BlockSpec((1, POOL_W), lambda b: (0, 0))],
        out_specs=[pl.BlockSpec((l, POOL_W), lambda b: (b, 0)),
                   pl.BlockSpec((POOL_W, POOL_W), lambda b: (0, 0)),
                   pl.BlockSpec((1, POOL_W), lambda b: (0, 0))],
        out_shape=[jax.ShapeDtypeStruct((nb * l, POOL_W), BF16),
                   jax.ShapeDtypeStruct((POOL_W, POOL_W), F32),
                   jax.ShapeDtypeStruct((1, POOL_W), F32)],
        scratch_shapes=[pltpu.VMEM((POOL_PAD + l, POOL_W), F32), pltpu.VMEM((POOL_PAD + l, POOL_W), F32)],
        compiler_params=_params("arbitrary"), name="pool_bwd")(proj, d_ycat, w_blk, pool_scale)


def _discretise(a_re, a_im, ldt, b_re, b_im):
    dt = jnp.exp(ldt)
    mag = jnp.exp(a_re * dt)
    th = a_im * dt
    lr = mag * jnp.cos(th)
    li = mag * jnp.sin(th)
    nr = lr - 1.0
    den = a_re * a_re + a_im * a_im
    fr = (nr * a_re + li * a_im) / den
    fi = (li * a_re - nr * a_im) / den
    return lr, li, fr * b_re - fi * b_im, fr * b_im + fi * b_re


def _cmul(ar, ai, br, bi):
    return ar * br - ai * bi, ar * bi + ai * br


def s5_scan_consts(a_re_row, a_im_row, ldt_row):
    def body(ar_ref, ai_ref, ld_ref, f_ref, b_ref):
        dt = jnp.exp(ld_ref[...])
        mag = jnp.exp(ar_ref[...] * dt)
        th = ai_ref[...] * dt
        lr = mag * jnp.cos(th)
        li = mag * jnp.sin(th)
        r = lax.broadcasted_iota(jnp.int32, (SUBLANES, NST), 0)
        for out_ref, sign, rev in ((f_ref, 1.0, False), (b_ref, -1.0, True)):
            p = [(lr, sign * li)]
            for _ in range(SUBLANES - 1):
                p.append(_cmul(p[-1][0], p[-1][1], lr, sign * li))
            for idx, s in enumerate((1, 2, 4)):
                inside = (r < SUBLANES - s) if rev else (r >= s)
                out_ref[pl.ds(2 * idx * SUBLANES, SUBLANES), :] = jnp.where(inside, p[s - 1][0], 0.0)
                out_ref[pl.ds((2 * idx + 1) * SUBLANES, SUBLANES), :] = jnp.where(inside, p[s - 1][1], 0.0)
            pr = jnp.zeros((SUBLANES, NST), F32)
            pi = jnp.zeros((SUBLANES, NST), F32)
            for row in range(SUBLANES):
                power = (SUBLANES - row) if rev else (row + 1)
                pr = jnp.where(r == row, p[power - 1][0], pr)
                pi = jnp.where(r == row, p[power - 1][1], pi)
            out_ref[pl.ds(6 * SUBLANES, SUBLANES), :] = pr
            out_ref[pl.ds(7 * SUBLANES, SUBLANES), :] = pi

    shape = jax.ShapeDtypeStruct((8 * SUBLANES, NST), F32)
    return pl.pallas_call(body, out_shape=[shape, shape], name="s5_scan_consts")(a_re_row, a_im_row, ldt_row)


def s5_bbar(a_re_c, a_im_c, ldt_c, b_re2, b_im2):
    def body(ar, ai, ld, br, bi, o_re, o_im):
        _, _, bbr, bbi = _discretise(ar[...], ai[...], ld[...], br[...], bi[...])
        o_re[...] = bbr
        o_im[...] = bbi

    shape = jax.ShapeDtypeStruct((NST, SSM_GC), F32)
    return pl.pallas_call(body, out_shape=[shape, shape], name="s5_bbar")(a_re_c, a_im_c, ldt_c, b_re2, b_im2)


def s5_param_bwd(a_re_c, a_im_c, ldt_c, b_re2, b_im2, d_lr, d_li, d_bbr, d_bbi):
    def body(ar, ai, ld, br, bi, g_lr, g_li, g_br, g_bi, o_ar, o_ai, o_ld, o_br, o_bi):
        _, vjp = jax.vjp(_discretise, ar[...], ai[...], ld[...], br[...], bi[...])
        d_ar, d_ai, d_ld, d_br, d_bi = vjp((g_lr[...], g_li[...], g_br[...], g_bi[...]))
        o_ar[...] = d_ar
        o_ai[...] = d_ai
        o_ld[...] = jnp.sum(d_ld.reshape(SSM_NG, SSM_N, 1), axis=1)
        o_br[...] = d_br
        o_bi[...] = d_bi

    col = jax.ShapeDtypeStruct((NST, 1), F32)
    mat = jax.ShapeDtypeStruct((NST, SSM_GC), F32)
    grp = jax.ShapeDtypeStruct((SSM_NG, 1), F32)
    return pl.pallas_call(body, out_shape=[col, col, grp, mat, mat], name="s5_param_bwd")(
        a_re_c, a_im_c, ldt_c, b_re2, b_im2, d_lr, d_li, d_bbr, d_bbi)


def _scan_tiles(buf_ref, cst_ref, carry_ref, rows, reverse, h_ref=None, acc_ref=None, reset=None, unrolled=False):
    n_tiles = rows // SUBLANES
    shifts = ((1, 0), (2, 2), (4, 4))
    row_id = lax.broadcasted_iota(jnp.int32, (SUBLANES, SCAN_COLS), 0)
    for j in range(NST // SCAN_COLS):
        c_re = pl.ds(j * SCAN_COLS, SCAN_COLS)
        c_im = pl.ds(NST + j * SCAN_COLS, SCAN_COLS)

        def step(i, carry, c_re=c_re, c_im=c_im):
            tile = (n_tiles - 1 - i) if reverse else i
            start = tile * SUBLANES
            rws = pl.ds(start if unrolled else pl.multiple_of(start, SUBLANES), SUBLANES)
            xr = buf_ref[rws, c_re]
            xi = buf_ref[rws, c_im]
            for s, base in shifts:
                amount = (SUBLANES - s) if reverse else s
                sr = pltpu.roll(xr, amount, 0)
                si = pltpu.roll(xi, amount, 0)
                mr = cst_ref[pl.ds(base * SUBLANES, SUBLANES), c_re]
                mi = cst_ref[pl.ds((base + 1) * SUBLANES, SUBLANES), c_re]
                xr, xi = xr + (mr * sr - mi * si), xi + (mr * si + mi * sr)
            pr = cst_ref[pl.ds(6 * SUBLANES, SUBLANES), c_re]
            pi = cst_ref[pl.ds(7 * SUBLANES, SUBLANES), c_re]
            cr, ci = carry[0], carry[1]
            xr, xi = xr + (pr * cr - pi * ci), xi + (pr * ci + pi * cr)
            buf_ref[rws, c_re] = xr
            buf_ref[rws, c_im] = xi
            edge = 0 if reverse else SUBLANES - 1
            ncr = jnp.broadcast_to(xr[edge:edge + 1, :], (SUBLANES, SCAN_COLS))
            nci = jnp.broadcast_to(xi[edge:edge + 1, :], (SUBLANES, SCAN_COLS))
            if not reverse:
                return ncr, nci
            gnr = jnp.where(row_id == SUBLANES - 1, cr, pltpu.roll(xr, SUBLANES - 1, 0))
            gni = jnp.where(row_id == SUBLANES - 1, ci, pltpu.roll(xi, SUBLANES - 1, 0))
            hr = h_ref[rws, c_re]
            hi = h_ref[rws, c_im]
            return ncr, nci, carry[2] + (gnr * hr + gni * hi), carry[3] + (gni * hr - gnr * hi)

        init = (carry_ref[:, c_re], carry_ref[:, c_im])
        if reset is not None:
            init = tuple(jnp.where(reset, 0.0, v) for v in init)
        if reverse:
            init = init + (acc_ref[:, c_re], acc_ref[:, c_im])
        if unrolled:
            out = init
            for i in range(n_tiles):
                out = step(i, out)
        else:
            out = lax.fori_loop(0, n_tiles, step, init)
        carry_ref[:, c_re] = out[0]
        carry_ref[:, c_im] = out[1]
        if reverse:
            acc_ref[:, c_re] = out[2]
            acc_ref[:, c_im] = out[3]


def _state_block(v, m):
    return jnp.concatenate([v[:, m * BLK_ST:(m + 1) * BLK_ST], v[:, NST + m * BLK_ST:NST + (m + 1) * BLK_ST]], axis=1)


def _put_state_block(buf_ref, m, val):
    buf_ref[:, pl.ds(m * BLK_ST, BLK_ST)] = val[:, :BLK_ST]
    buf_ref[:, pl.ds(NST + m * BLK_ST, BLK_ST)] = val[:, BLK_ST:]


def s5_fwd(proj, b_m, c_m, cst, d_skip, w_glu, nb, l):
    tt = min(256, l)
    nt = l // tt
    n_chunks = nb * nt

    def body(u_ref, b_ref, c_ref, cst_ref, ds_ref, wg_ref, y_ref, h_ref, yp_ref, z_ref,
             buf0, buf1, ub0, ub1, carry_ref):
        s = pl.program_id(0)

        @pl.when(s == 0)
        def _():
            for r in (buf0, buf1, ub0, ub1, carry_ref):
                r[...] = jnp.zeros_like(r)

        def step(p_buf, p_u, q_buf):
            h = p_buf[...].astype(BF16)
            h_ref[...] = h
            ypre = (jnp.concatenate([_dot(_state_block(h, m), c_ref[m]) for m in range(N_BLK)], axis=1)
                    + ds_ref[...] * p_u[...])
            yp_ref[...] = ypre
            z = _dot(jax.nn.gelu(ypre).astype(BF16), wg_ref[...])
            z_ref[...] = z
            y_ref[...] = z[:, :SSM_W] * jax.nn.sigmoid(z[:, SSM_W:])
            u = u_ref[...]
            p_u[...] = u
            ub = u.astype(BF16)
            for m in range(N_BLK):
                _put_state_block(p_buf, m, _dot(ub[:, m * BLK_CH:(m + 1) * BLK_CH], b_ref[m]))
            _scan_tiles(q_buf, cst_ref, carry_ref, tt, reverse=False, reset=lax.rem(s + nt - 1, nt) == 0, unrolled=True)

        @pl.when(lax.rem(s, 2) == 0)
        def _():
            step(buf0, ub0, buf1)

        @pl.when(lax.rem(s, 2) == 1)
        def _():
            step(buf1, ub1, buf0)

    row_in = lambda s: (jnp.minimum(s, n_chunks - 1), 1)
    row_out = lambda s: (jnp.maximum(s - 2, 0), 0)
    const = lambda s: (0, 0)
    const3 = lambda s: (0, 0, 0)
    return pl.pallas_call(
        body, grid=(n_chunks + 2,),
        in_specs=[pl.BlockSpec((tt, SSM_W), row_in),
                  pl.BlockSpec((N_BLK, BLK_CH, 2 * BLK_ST), const3),
                  pl.BlockSpec((N_BLK, 2 * BLK_ST, BLK_CH), const3),
                  pl.BlockSpec((8 * SUBLANES, NST), const),
                  pl.BlockSpec((1, SSM_W), const),
                  pl.BlockSpec((SSM_W, 2 * SSM_W), const)],
        out_specs=[pl.BlockSpec((tt, SSM_W), row_out),
                   pl.BlockSpec((tt, 2 * NST), row_out),
                   pl.BlockSpec((tt, SSM_W), row_out),
                   pl.BlockSpec((tt, 2 * SSM_W), row_out)],
        out_shape=[jax.ShapeDtypeStruct((nb * l, SSM_W), F32),
                   jax.ShapeDtypeStruct((nb * l, 2 * NST), BF16),
                   jax.ShapeDtypeStruct((nb * l, SSM_W), F32),
                   jax.ShapeDtypeStruct((nb * l, 2 * SSM_W), F32)],
        scratch_shapes=[pltpu.VMEM((tt, 2 * NST), F32), pltpu.VMEM((tt, 2 * NST), F32),
                        pltpu.VMEM((tt, SSM_W), F32), pltpu.VMEM((tt, SSM_W), F32),
                        pltpu.VMEM((SUBLANES, 2 * NST), F32)],
        compiler_params=_params("arbitrary"), name="s5_fwd")(proj, b_m, c_m, cst, d_skip, w_glu)


def s5_bwd(proj, d_ycat, h, ypre, z, b_m, c_m, cst_rev, d_skip, w_glu, nb, l):
    tt = min(256, l)
    nt = l // tt
    n_chunks = nb * nt

    def body(u_ref, dy_ref, h_ref, yp_ref, z_ref, b_ref, c_ref, cst_ref, ds_ref, wg_ref,
             du_ref, dwg_ref, dds_ref, dcc_ref, dbb_ref, dlam_ref,
             buf0, buf1, hb0, hb1, ub0, ub1, dyp0, dyp1, carry_ref, dc_ref, db_ref):
        s = pl.program_id(0)

        @pl.when(s == 0)
        def _():
            for r in (buf0, buf1, hb0, hb1, ub0, ub1, dyp0, dyp1, carry_ref, dc_ref, db_ref, dwg_ref, dds_ref, dlam_ref):
                r[...] = jnp.zeros_like(r)

        def step(p_buf, p_h, p_u, p_dyp, q_buf, q_h):
            g = p_buf[...].astype(BF16)
            ub_done = p_u[...].astype(BF16)
            du = []
            for m in range(N_BLK):
                g_m = _state_block(g, m)
                db_ref[m] += _dot_tn(g_m, ub_done[:, m * BLK_CH:(m + 1) * BLK_CH])
                du.append(_dot_nt(g_m, b_ref[m]))
            du_ref[...] = (jnp.concatenate(du, axis=1) + ds_ref[...] * p_dyp[...]).astype(BF16)
            u = u_ref[...]
            ypre = yp_ref[...]
            z = z_ref[...]
            z1 = z[:, :SSM_W]
            sg = jax.nn.sigmoid(z[:, SSM_W:])
            dy = dy_ref[...] * jnp.where(s < n_chunks, 1.0, 0.0)
            dz = jnp.concatenate([dy * sg, dy * z1 * sg * (1.0 - sg)], axis=1).astype(BF16)
            yg, gelu_vjp = jax.vjp(jax.nn.gelu, ypre)
            dwg_ref[...] += _dot_tn(yg.astype(BF16), dz)
            dypre = gelu_vjp(_dot_nt(dz, wg_ref[...]))[0]
            dds_ref[...] += jnp.sum(dypre * u, axis=0, keepdims=True)
            dyb = dypre.astype(BF16)
            hb = h_ref[...]
            p_h[...] = hb.astype(F32)
            p_u[...] = u
            p_dyp[...] = dypre
            for m in range(N_BLK):
                dy_m = dyb[:, m * BLK_CH:(m + 1) * BLK_CH]
                dc_ref[m] += _dot_tn(_state_block(hb, m), dy_m)
                _put_state_block(p_buf, m, _dot_nt(dy_m, c_ref[m]))
            _scan_tiles(q_buf, cst_ref, carry_ref, tt, reverse=True, h_ref=q_h, acc_ref=dlam_ref,
                        reset=lax.rem(s + nt - 1, nt) == 0, unrolled=True)

        @pl.when(lax.rem(s, 2) == 0)
        def _():
            step(buf0, hb0, ub0, dyp0, buf1, hb1)

        @pl.when(lax.rem(s, 2) == 1)
        def _():
            step(buf1, hb1, ub1, dyp1, buf0, hb0)

        @pl.when(s == n_chunks + 1)
        def _():
            dlam_ref[...] = jnp.broadcast_to(jnp.sum(dlam_ref[...], axis=0, keepdims=True), dlam_ref.shape)
            for acc_ref, out_ref in ((dc_ref, dcc_ref), (db_ref, dbb_ref)):
                for m in range(N_BLK):
                    for ri in range(2):
                        for gl in range(BLK_GROUPS):
                            out_ref[ri, pl.ds((m * BLK_GROUPS + gl) * SSM_N, SSM_N), :] = acc_ref[
                                m, pl.ds(ri * BLK_ST + gl * SSM_N, SSM_N), pl.ds(gl * SSM_GC, SSM_GC)]

    def chunk_rows(o):
        return lax.div(o, nt) * nt + (nt - 1 - lax.rem(o, nt))

    def rrow(col):
        return lambda s: (chunk_rows(jnp.minimum(s, n_chunks - 1)), col)

    const = lambda s: (0, 0)
    const3 = lambda s: (0, 0, 0)
    state_buf = pltpu.VMEM((tt, 2 * NST), F32)
    chan_buf = pltpu.VMEM((tt, SSM_W), F32)
    return pl.pallas_call(
        body, grid=(n_chunks + 2,),
        in_specs=[pl.BlockSpec((tt, SSM_W), rrow(1)),
                  pl.BlockSpec((tt, SSM_W), rrow(1)),
                  pl.BlockSpec((tt, 2 * NST), rrow(0)),
                  pl.BlockSpec((tt, SSM_W), rrow(0)),
                  pl.BlockSpec((tt, 2 * SSM_W), rrow(0)),
                  pl.BlockSpec((N_BLK, BLK_CH, 2 * BLK_ST), const3),
                  pl.BlockSpec((N_BLK, 2 * BLK_ST, BLK_CH), const3),
                  pl.BlockSpec((8 * SUBLANES, NST), const),
                  pl.BlockSpec((1, SSM_W), const),
                  pl.BlockSpec((SSM_W, 2 * SSM_W), const)],
        out_specs=[pl.BlockSpec((tt, SSM_W), lambda s: (chunk_rows(jnp.maximum(s - 2, 0)), 0)),
                   pl.BlockSpec((SSM_W, 2 * SSM_W), const),
                   pl.BlockSpec((1, SSM_W), const),
                   pl.BlockSpec((2, NST, SSM_GC), const3),
                   pl.BlockSpec((2, NST, SSM_GC), const3),
                   pl.BlockSpec((SUBLANES, 2 * NST), const)],
        out_shape=[jax.ShapeDtypeStruct((nb * l, SSM_W), BF16),
                   jax.ShapeDtypeStruct((SSM_W, 2 * SSM_W), F32),
                   jax.ShapeDtypeStruct((1, SSM_W), F32),
                   jax.ShapeDtypeStruct((2, NST, SSM_GC), F32),
                   jax.ShapeDtypeStruct((2, NST, SSM_GC), F32),
                   jax.ShapeDtypeStruct((SUBLANES, 2 * NST), F32)],
        scratch_shapes=[state_buf, state_buf, state_buf, state_buf, chan_buf, chan_buf, chan_buf, chan_buf,
                        pltpu.VMEM((SUBLANES, 2 * NST), F32),
                        pltpu.VMEM((N_BLK, 2 * BLK_ST, BLK_CH), F32), pltpu.VMEM((N_BLK, 2 * BLK_ST, BLK_CH), F32)],
        compiler_params=_params("arbitrary"), name="s5_bwd")(
            proj, d_ycat, h, ypre, z, b_m, c_m, cst_rev, d_skip, w_glu)


def _head_mask(hh):
    lane = lax.broadcasted_iota(jnp.int32, (1, ATT_W), 1)
    return (lane >= hh * MEM_HD) & (lane < (hh + 1) * MEM_HD)


def _mem_kv(mem_ref, gm_ref, wkv_ref):
    m = mem_ref[0]
    mh = m * _rms_scale(m)
    mb = (mh * gm_ref[...]).astype(BF16)
    kv = _dot(mb, wkv_ref[...])
    return mh, mb, kv[:, :ATT_W].astype(BF16), kv[:, ATT_W:].astype(BF16)


def _softmax_rows(s):
    e = jnp.exp(s - jnp.max(s, axis=-1, keepdims=True))
    return e / jnp.sum(e, axis=-1, keepdims=True)


def attn_fwd(proj, mem, g_mem, w_kv, nb, l):
    tq = min(512, l)
    nq = l // tq
    scale = MEM_HD ** -0.5

    def body(q_ref, mem_ref, gm_ref, wkv_ref, o_ref, k_s, v_s):
        @pl.when(pl.program_id(1) == 0)
        def _():
            _, _, k, v = _mem_kv(mem_ref, gm_ref, wkv_ref)
            k_s[...] = k
            v_s[...] = v

        q = q_ref[...].astype(BF16)
        k = k_s[...]
        v = v_s[...]
        o = jnp.zeros((tq, ATT_W), F32)
        for hh in range(MEM_HEADS):
            msk = _head_mask(hh)
            p = _softmax_rows(_dot_nt(q, jnp.where(msk, k, 0)) * scale)
            o = o + _dot(p.astype(BF16), jnp.where(msk, v, 0))
        o_ref[...] = o

    const = lambda b, t: (0, 0)
    return pl.pallas_call(
        body, grid=(nb, nq),
        in_specs=[pl.BlockSpec((tq, ATT_W), lambda b, t: (b * nq + t, 3)),
                  pl.BlockSpec((1, N_MEM, D_MODEL), lambda b, t: (b, 0, 0)),
                  pl.BlockSpec((1, D_MODEL), const),
                  pl.BlockSpec((D_MODEL, 2 * ATT_W), const)],
        out_specs=pl.BlockSpec((tq, ATT_W), lambda b, t: (b * nq + t, 0)),
        out_shape=jax.ShapeDtypeStruct((nb * l, ATT_W), F32),
        scratch_shapes=[pltpu.VMEM((N_MEM, ATT_W), BF16), pltpu.VMEM((N_MEM, ATT_W), BF16)],
        compiler_params=_params("arbitrary", "arbitrary"), name="attn_fwd")(proj, mem, g_mem, w_kv)


def attn_bwd(proj, d_ycat, mem, g_mem, w_kv, nb, l):
    tq = min(512, l)
    nq = l // tq
    scale = MEM_HD ** -0.5

    def body(q_ref, do_ref, mem_ref, gm_ref, wkv_ref, dq_ref, dwkv_ref, dgm_ref, k_s, v_s, mb_s, dk_s, dv_s):
        b = pl.program_id(0)
        t = pl.program_id(1)

        @pl.when((b == 0) & (t == 0))
        def _():
            dwkv_ref[...] = jnp.zeros_like(dwkv_ref)
            dgm_ref[...] = jnp.zeros_like(dgm_ref)

        @pl.when(t == 0)
        def _():
            _, mb, k, v = _mem_kv(mem_ref, gm_ref, wkv_ref)
            k_s[...] = k
            v_s[...] = v
            mb_s[...] = mb
            dk_s[...] = jnp.zeros_like(dk_s)
            dv_s[...] = jnp.zeros_like(dv_s)

        q = q_ref[...].astype(BF16)
        do = do_ref[...].astype(BF16)
        k = k_s[...]
        v = v_s[...]
        dq = jnp.zeros((tq, ATT_W), F32)
        dk = jnp.zeros((N_MEM, ATT_W), F32)
        dv = jnp.zeros((N_MEM, ATT_W), F32)
        for hh in range(MEM_HEADS):
            msk = _head_mask(hh)
            kh = jnp.where(msk, k, 0)
            p = _softmax_rows(_dot_nt(q, kh) * scale)
            dp = _dot_nt(do, jnp.where(msk, v, 0))
            ds = (p * (dp - jnp.sum(dp * p, axis=-1, keepdims=True)) * scale).astype(BF16)
            dq = dq + _dot(ds, kh)
            dk = dk + jnp.where(msk, _dot_tn(ds, q), 0.0)
            dv = dv + jnp.where(msk, _dot_tn(p.astype(BF16), do), 0.0)
        dq_ref[...] = dq.astype(BF16)
        dk_s[...] += dk
        dv_s[...] += dv

        @pl.when(t == nq - 1)
        def _():
            dkv = jnp.concatenate([dk_s[...], dv_s[...]], axis=1).astype(BF16)
            dwkv_ref[...] += _dot_tn(mb_s[...], dkv)
            m = mem_ref[0]
            dgm_ref[...] += jnp.sum(_dot_nt(dkv, wkv_ref[...]) * (m * _rms_scale(m)), axis=0, keepdims=True)

    const = lambda b, t: (0, 0)
    return pl.pallas_call(
        body, grid=(nb, nq),
        in_specs=[pl.BlockSpec((tq, ATT_W), lambda b, t: (b * nq + t, 3)),
                  pl.BlockSpec((tq, ATT_W), lambda b, t: (b * nq + t, 3)),
                  pl.BlockSpec((1, N_MEM, D_MODEL), lambda b, t: (b, 0, 0)),
                  pl.BlockSpec((1, D_MODEL), const),
                  pl.BlockSpec((D_MODEL, 2 * ATT_W), const)],
        out_specs=[pl.BlockSpec((tq, ATT_W), lambda b, t: (b * nq + t, 0)),
                   pl.BlockSpec((D_MODEL, 2 * ATT_W), const),
                   pl.BlockSpec((1, D_MODEL), const)],
        out_shape=[jax.ShapeDtypeStruct((nb * l, ATT_W), BF16),
                   jax.ShapeDtypeStruct((D_MODEL, 2 * ATT_W), F32),
                   jax.ShapeDtypeStruct((1, D_MODEL), F32)],
        scratch_shapes=[pltpu.VMEM((N_MEM, ATT_W), BF16), pltpu.VMEM((N_MEM, ATT_W), BF16),
                        pltpu.VMEM((N_MEM, D_MODEL), BF16),
                        pltpu.VMEM((N_MEM, ATT_W), F32), pltpu.VMEM((N_MEM, ATT_W), F32)],
        compiler_params=_params("arbitrary", "arbitrary"), name="attn_bwd")(proj, d_ycat, mem, g_mem, w_kv)


def tail(y_pool, y_ssm, y_att, proj, x2, target, w_out, g_post):
    t = x2.shape[0]
    tm = min(ROW_TILE, t)
    n = t // tm

    def body(yp_ref, ys_ref, ya_ref, gate_ref, x_ref, tg_ref, w_ref, gp_ref,
             dz_ref, dyc_ref, dgate_ref, dw_ref, dgp_ref, loss_ref,
             out0, out1, dout0, dout1, yb0, yb1, silu0, silu1, yds0, yds1):
        s = pl.program_id(0)

        @pl.when(s == 0)
        def _():
            for r in (out0, out1, dout0, dout1, yb0, yb1, silu0, silu1, yds0, yds1, dw_ref, dgp_ref, loss_ref):
                r[...] = jnp.zeros_like(r)

        def step(p_out, p_dout, p_yb, p_silu, p_yds, q_out, q_dout):
            w = w_ref[...]
            dout_done = p_dout[...]
            dw_ref[...] += _dot_tn(p_yb[...], dout_done)
            dy = _dot_nt(dout_done, w)
            dyc_ref[...] = dy * p_silu[...]
            dgate_ref[...] = (dy * p_yds[...]).astype(BF16)
            ycat = jnp.concatenate([yp_ref[...], ys_ref[...], ya_ref[...]], axis=1)
            gate = gate_ref[...]
            sg = jax.nn.sigmoid(gate)
            silu = gate * sg
            yb = (ycat * silu).astype(BF16)
            p_yb[...] = yb
            p_silu[...] = silu
            p_yds[...] = ycat * (sg * (1.0 + gate * (1.0 - sg)))
            p_out[...] = _dot(yb, w)
            live = jnp.where((s >= 1) & (s <= n), 1.0, 0.0)
            out = q_out[...]
            r2 = _rms_scale(out)
            oh = out * r2
            gp = gp_ref[...]
            err = (x_ref[...] + oh * gp) - tg_ref[...]
            loss_ref[...] += (0.5 * live) * jnp.sum(jnp.mean(err * err, axis=-1, keepdims=True), axis=0, keepdims=True)
            dz = err * (1.0 / D_MODEL)
            dz_ref[...] = dz
            dgp_ref[...] += live * jnp.sum(dz * oh, axis=0, keepdims=True)
            dn = dz * gp
            q_dout[...] = ((r2 * live) * (dn - oh * jnp.mean(dn * oh, axis=-1, keepdims=True))).astype(BF16)

        @pl.when(lax.rem(s, 2) == 0)
        def _():
            step(out0, dout0, yb0, silu0, yds0, out1, dout1)

        @pl.when(lax.rem(s, 2) == 1)
        def _():
            step(out1, dout1, yb1, silu1, yds1, out0, dout0)

    clamp = lambda v: jnp.clip(v, 0, n - 1)
    row_a = lambda s: (clamp(s), 0)
    row_b = lambda s: (clamp(s - 1), 0)
    row_c = lambda s: (clamp(s - 2), 0)
    const = lambda s: (0, 0)
    full = jax.ShapeDtypeStruct((t, D_MODEL), F32)
    f32_buf = pltpu.VMEM((tm, D_MODEL), F32)
    bf16_buf = pltpu.VMEM((tm, D_MODEL), BF16)
    return pl.pallas_call(
        body, grid=(n + 2,),
        in_specs=[pl.BlockSpec((tm, POOL_W), row_a),
                  pl.BlockSpec((tm, SSM_W), row_a),
                  pl.BlockSpec((tm, ATT_W), row_a),
                  pl.BlockSpec((tm, D_MODEL), lambda s: (clamp(s), 1)),
                  pl.BlockSpec((tm, D_MODEL), row_b),
                  pl.BlockSpec((tm, D_MODEL), row_b),
                  pl.BlockSpec((D_MODEL, D_MODEL), const),
                  pl.BlockSpec((1, D_MODEL), const)],
        out_specs=[pl.BlockSpec((tm, D_MODEL), row_b),
                   pl.BlockSpec((tm, D_MODEL), row_c),
                   pl.BlockSpec((tm, D_MODEL), row_c),
                   pl.BlockSpec((D_MODEL, D_MODEL), const),
                   pl.BlockSpec((1, D_MODEL), const),
                   pl.BlockSpec((1, LANES), const)],
        out_shape=[full, full, jax.ShapeDtypeStruct((t, D_MODEL), BF16),
                   jax.ShapeDtypeStruct((D_MODEL, D_MODEL), F32),
                   jax.ShapeDtypeStruct((1, D_MODEL), F32),
                   jax.ShapeDtypeStruct((1, LANES), F32)],
        scratch_shapes=[f32_buf, f32_buf, bf16_buf, bf16_buf, bf16_buf, bf16_buf, f32_buf, f32_buf, f32_buf, f32_buf],
        compiler_params=_params("arbitrary"), name="tail")(y_pool, y_ssm, y_att, proj, x2, target, w_out, g_post)


def in_proj_bwd(du_pool, du_ssm, dq, d_gate, x2, dz, g_pre, w_in):
    t = x2.shape[0]
    tm = min(ROW_TILE, t)
    n = t // tm
    shard = W_IN_SHARD

    def body(dup_ref, dus_ref, dq_ref, dg_ref, x_ref, dz_ref, g_ref, w_ref, gx_ref, dw_ref, dgp_ref,
             dh0, dh1, xh0, xh1, r0, r1):
        s = pl.program_id(0)

        @pl.when(s == 0)
        def _():
            for r in (dh0, dh1, xh0, xh1, r0, r1, dw_ref, dgp_ref):
                r[...] = jnp.zeros_like(r)

        def step(p_dh, p_xh, p_r, q_dh, q_xh, q_r):
            g = g_ref[...]
            dproj = jnp.concatenate([dup_ref[...], dus_ref[...], dq_ref[...], dg_ref[...]], axis=1)
            dproj = jnp.where(s < n, dproj, jnp.zeros_like(dproj))
            x = x_ref[...]
            rs = _rms_scale(x)
            xh = x * rs
            p_xh[...] = xh
            p_r[...] = jnp.broadcast_to(rs, p_r.shape)
            hb = (xh * g).astype(BF16)
            dh = jnp.zeros((tm, D_MODEL), F32)
            for k in range(N_CHIPS):
                dp_k = dproj[:, k * shard:(k + 1) * shard]
                dw_ref[k] += _dot_tn(hb, dp_k)
                dh = dh + _dot_nt(dp_k, w_ref[k])
            p_dh[...] = dh
            dh_q = q_dh[...]
            xh_q = q_xh[...]
            dgp_ref[...] += jnp.where(s >= 1, 1.0, 0.0) * jnp.sum(dh_q * xh_q, axis=0, keepdims=True)
            dn = dh_q * g
            gx_ref[...] = dz_ref[...] + q_r[:, :1] * (dn - xh_q * jnp.mean(dn * xh_q, axis=-1, keepdims=True))

        @pl.when(lax.rem(s, 2) == 0)
        def _():
            step(dh0, xh0, r0, dh1, xh1, r1)

        @pl.when(lax.rem(s, 2) == 1)
        def _():
            step(dh1, xh1, r1, dh0, xh0, r0)

    clamp = lambda v: jnp.clip(v, 0, n - 1)
    row_a = lambda s: (clamp(s), 0)
    row_b = lambda s: (clamp(s - 1), 0)
    const = lambda s: (0, 0)
    f32_buf = pltpu.VMEM((tm, D_MODEL), F32)
    col_buf = pltpu.VMEM((tm, LANES), F32)
    return pl.pallas_call(
        body, grid=(n + 1,),
        in_specs=[pl.BlockSpec((tm, POOL_W), row_a),
                  pl.BlockSpec((tm, SSM_W), row_a),
                  pl.BlockSpec((tm, ATT_W), row_a),
                  pl.BlockSpec((tm, D_MODEL), row_a),
                  pl.BlockSpec((tm, D_MODEL), row_a),
                  pl.BlockSpec((tm, D_MODEL), row_b),
                  pl.BlockSpec((1, D_MODEL), const),
                  pl.BlockSpec((N_CHIPS, D_MODEL, shard), lambda s: (0, 0, 0))],
        out_specs=[pl.BlockSpec((tm, D_MODEL), row_b),
                   pl.BlockSpec((N_CHIPS, D_MODEL, shard), lambda s: (0, 0, 0)),
                   pl.BlockSpec((1, D_MODEL), const)],
        out_shape=[jax.ShapeDtypeStruct((t, D_MODEL), F32),
                   jax.ShapeDtypeStruct((N_CHIPS, D_MODEL, shard), F32),
                   jax.ShapeDtypeStruct((1, D_MODEL), F32)],
        scratch_shapes=[f32_buf, f32_buf, f32_buf, f32_buf, col_buf, col_buf],
        compiler_params=_params("arbitrary"), name="in_proj_bwd")(du_pool, du_ssm, dq, d_gate, x2, dz, g_pre, w_in)


def _block_diag(blocks):
    g, r, c = blocks.shape
    eye = jnp.eye(g, dtype=blocks.dtype)
    return jnp.einsum("grc,gh->grhc", blocks, eye).reshape(g * r, g * c)


def _block_diag_extract(mat, g, r, c):
    eye = jnp.eye(g, dtype=mat.dtype)
    return jnp.einsum("grhc,gh->grc", mat.reshape(g, r, g, c), eye)


def local_step(x, mem, target, g_pre, w_in, w_pool, pool_scale, a_re, a_im, log_dt, b_re, b_im, c_re, c_im, d_skip,
               w_glu, g_mem, w_kv, w_out, g_post):
    nb, l, _ = x.shape
    x2 = x.reshape(nb * l, D_MODEL)
    tg2 = target.reshape(nb * l, D_MODEL)

    rowv = lambda a: a.reshape(1, NST)
    colv = lambda a: a.reshape(NST, 1)
    ldt_full = jnp.broadcast_to(log_dt.reshape(SSM_NG, 1), (SSM_NG, SSM_N))
    b_re2 = b_re.reshape(NST, SSM_GC)
    b_im2 = b_im.reshape(NST, SSM_GC)
    cst_f, cst_b = s5_scan_consts(rowv(a_re), rowv(a_im), rowv(ldt_full))
    bbr, bbi = s5_bbar(colv(a_re), colv(a_im), colv(ldt_full), b_re2, b_im2)
    eye = jnp.eye(BLK_GROUPS, dtype=F32)
    blk_in = lambda bb: jnp.einsum("mgnc,gh->mhcgn", bb.reshape(N_BLK, BLK_GROUPS, SSM_N, SSM_GC), eye).reshape(
        N_BLK, BLK_CH, BLK_ST)
    blk_out = lambda cc: jnp.einsum("mgcn,gh->mgnhc", cc.reshape(N_BLK, BLK_GROUPS, SSM_GC, SSM_N), eye).reshape(
        N_BLK, BLK_ST, BLK_CH)
    b_m = jnp.concatenate([blk_in(bbr), blk_in(bbi)], axis=2).astype(BF16)
    c_m = jnp.concatenate([blk_out(c_re), -blk_out(c_im)], axis=1).astype(BF16)
    w_pool_blk = _block_diag(w_pool.reshape(4, POOL_GW, POOL_GW)).astype(BF16)

    proj = in_proj(x2, g_pre, w_in)
    y_pool = pool_fwd(proj, w_pool_blk, pool_scale, nb, l)
    y_ssm, h, ypre, z = s5_fwd(proj, b_m, c_m, cst_f, d_skip, w_glu, nb, l)
    y_att = attn_fwd(proj, mem, g_mem, w_kv, nb, l)
    dz, d_ycat, d_gate, d_w_out, d_g_post, loss = tail(y_pool, y_ssm, y_att, proj, x2, tg2, w_out, g_post)
    du_pool, d_wp_blk, d_pool_scale = pool_bwd(proj, d_ycat, w_pool_blk, pool_scale, nb, l)
    du_ssm, d_w_glu, d_d_skip, d_cc, d_bb, d_lam = s5_bwd(
        proj, d_ycat, h, ypre, z, b_m, c_m, cst_b, d_skip, w_glu, nb, l)
    dq, d_w_kv, d_g_mem = attn_bwd(proj, d_ycat, mem, g_mem, w_kv, nb, l)
    grad_x, d_w_in, d_g_pre = in_proj_bwd(du_pool, du_ssm, dq, d_gate, x2, dz, g_pre, w_in)

    d_c = d_cc[0].reshape(SSM_NG, SSM_N, SSM_GC)
    d_ci = d_cc[1].reshape(SSM_NG, SSM_N, SSM_GC)
    d_lam_row = d_lam[0]
    d_ar, d_ai, d_ld, d_br, d_bi = s5_param_bwd(
        colv(a_re), colv(a_im), colv(ldt_full), b_re2, b_im2,
        d_lam_row[:NST].reshape(NST, 1), d_lam_row[NST:].reshape(NST, 1), d_bb[0], d_bb[1])

    grads = {
        "g_pre": d_g_pre,
        "w_in": d_w_in,
        "w_pool": _block_diag_extract(d_wp_blk, 4, POOL_GW, POOL_GW).reshape(1, 4, POOL_GW, POOL_GW),
        "pool_scale": d_pool_scale,
        "a_re": d_ar.reshape(1, SSM_NG, SSM_N),
        "a_im": d_ai.reshape(1, SSM_NG, SSM_N),
        "log_dt": d_ld.reshape(1, SSM_NG),
        "b_re": d_br.reshape(1, SSM_NG, SSM_N, SSM_GC),
        "b_im": d_bi.reshape(1, SSM_NG, SSM_N, SSM_GC),
        "c_re": d_c.transpose(0, 2, 1).reshape(1, SSM_NG, SSM_GC, SSM_N),
        "c_im": (-d_ci).transpose(0, 2, 1).reshape(1, SSM_NG, SSM_GC, SSM_N),
        "d_skip": d_d_skip,
        "w_glu": d_w_glu,
        "g_mem": d_g_mem,
        "w_kv": d_w_kv,
        "w_out": d_w_out,
        "g_post": d_g_post,
    }
    return loss, grad_x.reshape(nb, l, D_MODEL), grads


VMEM_SPEC = pl.BlockSpec(memory_space=pltpu.VMEM)
ANY_SPEC = pl.BlockSpec(memory_space=pl.ANY)


def _place():
    return lax.axis_index("x"), lax.axis_index("y"), lax.axis_index("c")


def _other_chips(x, y):
    return [(1 - x, y), (x, 1 - y), (1 - x, 1 - y)]


def gather_weights(shards):
    n = len(shards)

    def body(*refs):
        ins, outs = refs[:n], refs[n:2 * n]
        send_sems, recv_sems = refs[2 * n:]
        x, y, c = _place()
        k = 2 * x + y
        chips = _other_chips(x, y)
        for i in range(n):
            outs[i][k] = ins[i][...].astype(BF16)

        def copy(i, block, core_half, to, sem):
            hr = shards[i].shape[0] // 2
            rows = outs[i].at[block, pl.ds(pl.multiple_of(core_half * hr, 2 * SUBLANES), hr)]
            return pltpu.make_async_remote_copy(
                src_ref=rows, dst_ref=rows, send_sem=send_sems.at[sem], recv_sem=recv_sems.at[sem],
                device_id=to, device_id_type=MESH)

        pairs = [(i, j) for i in range(n) for j in range(3)]
        first = [copy(i, k, c, (*chips[j], c), i * 3 + j) for i, j in pairs]
        for cp in first:
            cp.start()
        passed = []
        for i, j in pairs:
            px, py = chips[j]
            copy(i, 2 * px + py, c, (px, py, c), i * 3 + j).wait_recv()
            fw = copy(i, 2 * px + py, c, (x, y, 1 - c), 3 * n + i * 3 + j)
            fw.start()
            passed.append(fw)
        for i, j in pairs:
            px, py = chips[j]
            copy(i, 2 * px + py, 1 - c, (x, y, 1 - c), 3 * n + i * 3 + j).wait_recv()
        for cp in first + passed:
            cp.wait_send()

    return pl.pallas_call(
        body,
        out_shape=[jax.ShapeDtypeStruct((N_CHIPS,) + s.shape, BF16) for s in shards],
        in_specs=[VMEM_SPEC] * n, out_specs=[VMEM_SPEC] * n,
        scratch_shapes=[pltpu.SemaphoreType.DMA((6 * n,)), pltpu.SemaphoreType.DMA((6 * n,))],
        compiler_params=pltpu.CompilerParams(vmem_limit_bytes=VMEM_LIMIT),
        name="gather_weights")(*shards)


SEMS_PER_ITEM = 5


def reduce_all(g4s, packed):
    n = len(g4s)
    dims = [g.shape[1:] for g in g4s]
    pr = packed.shape[0]

    def body(*refs):
        g_refs, p_ref = refs[:n], refs[n]
        outs, op = refs[n + 1:2 * n + 1], refs[2 * n + 1]
        scr = refs[2 * n + 2:]
        mine, theirs, sendb, recvb = scr[0:n], scr[n:2 * n], scr[2 * n:3 * n], scr[3 * n:4 * n]
        p_theirs, gat, lsems, send_sems, recv_sems = scr[4 * n:]
        x, y, c = _place()
        k = 2 * x + y
        chips = _other_chips(x, y)
        sib = (x, y, 1 - c)

        def halves(rows):
            hr = rows // 2
            return (pl.ds(pl.multiple_of(c * hr, SUBLANES), hr), pl.ds(pl.multiple_of((1 - c) * hr, SUBLANES), hr))

        def remote(src, dst, sem, to):
            return pltpu.make_async_remote_copy(
                src_ref=src, dst_ref=dst, send_sem=send_sems.at[sem], recv_sem=recv_sems.at[sem],
                device_id=to, device_id_type=MESH)

        loads, started = [], []
        for i in range(n):
            my_rows, sib_rows = halves(dims[i][0])
            ld = pltpu.make_async_copy(g_refs[i].at[:, my_rows, :], mine[i], lsems.at[i])
            ld.start()
            loads.append(ld)
            sw = remote(g_refs[i].at[:, sib_rows, :], theirs[i], SEMS_PER_ITEM * i, sib)
            sw.start()
            started.append(sw)
        p_my, p_sib = halves(pr)
        p_swap = remote(p_ref.at[p_sib, :], p_theirs, SEMS_PER_ITEM * n, sib)
        p_swap.start()
        started.append(p_swap)

        p_swap.wait_recv()
        gat[k] = p_ref[p_my, :] + p_theirs[...]
        for j, (px, py) in enumerate(chips):
            cp = remote(gat.at[k], gat.at[k], SEMS_PER_ITEM * n + 1 + j, (px, py, c))
            cp.start()
            started.append(cp)
        ici = []
        for i in range(n):
            loads[i].wait()
            started[i].wait_recv()
            for j, (px, py) in enumerate(chips):
                kp = 2 * px + py
                sendb[i][j] = (mine[i][kp] + theirs[i][kp]).astype(BF16)
                cp = remote(sendb[i].at[j], recvb[i].at[j], SEMS_PER_ITEM * i + 1 + j, (px, py, c))
                cp.start()
                ici.append(cp)

        for j, (px, py) in enumerate(chips):
            kp = 2 * px + py
            remote(gat.at[kp], gat.at[kp], SEMS_PER_ITEM * n + 1 + j, (px, py, c)).wait_recv()
        op[p_my, :] = ((gat[0] + gat[1]) + gat[2]) + gat[3]
        last = remote(op.at[p_my, :], op.at[p_my, :], SEMS_PER_ITEM * n + 4, sib)
        last.start()
        started.append(last)
        for i in range(n):
            my_rows, _ = halves(dims[i][0])
            half = mine[i][k] + theirs[i][k]
            for j in range(3):
                ici[3 * i + j].wait_recv()
                half = half + recvb[i][j].astype(F32)
            outs[i][my_rows, :] = half
            last = remote(outs[i].at[my_rows, :], outs[i].at[my_rows, :], SEMS_PER_ITEM * i + 4, sib)
            last.start()
            started.append(last)
        remote(op.at[p_sib, :], op.at[p_sib, :], SEMS_PER_ITEM * n + 4, sib).wait_recv()
        for i in range(n):
            _, sib_rows = halves(dims[i][0])
            remote(outs[i].at[sib_rows, :], outs[i].at[sib_rows, :], SEMS_PER_ITEM * i + 4, sib).wait_recv()
        for cp in started + ici:
            cp.wait_send()

    n_sems = SEMS_PER_ITEM * (n + 1)
    scratch = ([pltpu.VMEM((N_CHIPS, r // 2, cd), F32) for r, cd in dims] * 2
               + [pltpu.VMEM((3, r // 2, cd), BF16) for r, cd in dims] * 2
               + [pltpu.VMEM((pr // 2, LANES), F32), pltpu.VMEM((N_CHIPS, pr // 2, LANES), F32),
                  pltpu.SemaphoreType.DMA((n,)), pltpu.SemaphoreType.DMA((n_sems,)), pltpu.SemaphoreType.DMA((n_sems,))])
    res = pl.pallas_call(
        body,
        out_shape=[jax.ShapeDtypeStruct(d, F32) for d in dims] + [jax.ShapeDtypeStruct(packed.shape, F32)],
        in_specs=[ANY_SPEC] * n + [VMEM_SPEC], out_specs=[VMEM_SPEC] * (n + 1),
        scratch_shapes=scratch,
        compiler_params=pltpu.CompilerParams(vmem_limit_bytes=VMEM_LIMIT),
        name="reduce_all")(*g4s, packed)
    return res[:n], res[n]


def adamw_all(ws, gs, ms, vs):
    n = len(ws)

    def body(*refs):
        w, g, m, v = refs[:n], refs[n:2 * n], refs[2 * n:3 * n], refs[3 * n:4 * n]
        od, om, ov = refs[4 * n:5 * n], refs[5 * n:6 * n], refs[6 * n:]
        for i in range(n):
            od[i][...], om[i][...], ov[i][...] = _adamw(w[i][...], g[i][...], m[i][...], v[i][...])

    shapes = [jax.ShapeDtypeStruct(a.shape, F32) for a in ws]
    res = pl.pallas_call(
        body, out_shape=shapes * 3, in_specs=[VMEM_SPEC] * (4 * n), out_specs=[VMEM_SPEC] * (3 * n),
        compiler_params=pltpu.CompilerParams(vmem_limit_bytes=VMEM_LIMIT),
        name="adamw_all")(*ws, *gs, *ms, *vs)
    return res[:n], res[n:2 * n], res[2 * n:]


WEIGHTS = ("g_pre", "w_in", "w_pool", "pool_scale", "a_re", "a_im", "log_dt", "b_re", "b_im", "c_re", "c_im", "d_skip",
           "w_glu", "g_mem", "w_kv", "w_out", "g_post")
SHARDED = ("w_in", "w_glu", "w_kv", "w_out")
REPLICATED = tuple(n for n in WEIGHTS if n not in SHARDED)
PACK_TILE = SUBLANES * LANES


def _pack(arrays):
    parts = []
    for a in arrays:
        flat = a.reshape(-1)
        parts.append(jnp.pad(flat, (0, -flat.shape[0] % PACK_TILE)).reshape(-1, LANES))
    rows = sum(p.shape[0] for p in parts)
    if rows % (2 * SUBLANES):
        parts.append(jnp.zeros((SUBLANES, LANES), F32))
    return jnp.concatenate(parts, axis=0)


def _unpack(packed, shapes):
    out, row = [], 0
    for shp in shapes:
        size = math.prod(shp)
        rows = -(-size // PACK_TILE) * SUBLANES
        out.append(packed[row:row + rows].reshape(-1)[:size].reshape(shp))
        row += rows
    return out


def _chip_major(a, n_cols):
    return a.reshape(a.shape[0], N_CHIPS, n_cols).transpose(1, 0, 2)


def _from_chip_major(a):
    return a.transpose(1, 0, 2).reshape(a.shape[1], N_CHIPS * a.shape[2])


def kernel(x, mem, g_pre, w_in, w_pool, pool_scale, a_re, a_im, log_dt, b_re, b_im, c_re, c_im, d_skip, w_glu, g_mem, w_kv, w_out, g_post, loss_target, m_g_pre, m_w_in, m_w_pool, m_pool_scale, m_a_re, m_a_im, m_log_dt, m_b_re, m_b_im, m_c_re, m_c_im, m_d_skip, m_w_glu, m_g_mem, m_w_kv, m_w_out, m_g_post, v_g_pre, v_w_in, v_w_pool, v_pool_scale, v_a_re, v_a_im, v_log_dt, v_b_re, v_b_im, v_c_re, v_c_im, v_d_skip, v_w_glu, v_g_mem, v_w_kv, v_w_out, v_g_post):
    given = dict(locals())
    wts = {n: given[n] for n in WEIGHTS}
    mom = {n: given["m_" + n] for n in WEIGHTS}
    var = {n: given["v_" + n] for n in WEIGHTS}

    g_in, g_glu, g_kv, g_out = gather_weights([wts[n][0] for n in SHARDED])
    w_in_full = g_in
    w_glu_full = _from_chip_major(g_glu)
    w_kv_full = g_kv.reshape(D_MODEL, 2 * ATT_W)
    w_out_full = g_out.reshape(D_MODEL, D_MODEL)

    loss_part, grad_x, grads = local_step(
        x, mem, loss_target, g_pre, w_in_full, w_pool[0], pool_scale, a_re[0], a_im[0], log_dt[0], b_re[0], b_im[0],
        c_re[0], c_im[0], d_skip, w_glu_full, g_mem, w_kv_full, w_out_full, g_post)

    partial4 = [
        grads["w_in"],
        _chip_major(grads["w_glu"], 2 * SSM_W // N_CHIPS),
        grads["w_kv"].reshape(N_CHIPS, D_MODEL // N_CHIPS, 2 * ATT_W),
        grads["w_out"].reshape(N_CHIPS, D_MODEL // N_CHIPS, D_MODEL),
    ]
    loss_tile = jnp.broadcast_to(loss_part, (SUBLANES, LANES))
    sharded_g, packed_g = reduce_all(partial4, _pack([grads[n] for n in REPLICATED] + [loss_tile]))
    small_g = _unpack(packed_g, [wts[n].shape for n in REPLICATED] + [(SUBLANES, LANES)])
    loss = small_g[-1][0, 0]
    grad = dict(zip(SHARDED, [g[None] for g in sharded_g]))
    grad.update(zip(REPLICATED, small_g[:-1]))

    deltas, new_ms, new_vs = adamw_all([wts[n] for n in WEIGHTS], [grad[n] for n in WEIGHTS],
                                       [mom[n] for n in WEIGHTS], [var[n] for n in WEIGHTS])
    return (loss, grad_x, *[grad[n] for n in WEIGHTS], *deltas, *new_ms, *new_vs)
```

```python
import functools
import math

import jax
import jax.numpy as jnp
from jax import lax
from jax.experimental import pallas as pl
from jax.experimental.pallas import tpu as pltpu

F32 = jnp.float32
BF16 = jnp.bfloat16

D_MODEL = 1024
POOL_W = 384
SSM_W = 384
ATT_W = 256
POOL_GW = 96
POOL_WINDOWS = (2, 4, 8, 16)
POOL_PAD = 16
SSM_NG = 24
SSM_N = 64
SSM_GC = 16
N_CHIPS = 4
N_DEV = 8
W_IN_SHARD = 2 * D_MODEL // N_CHIPS
NST = SSM_NG * SSM_N
BLK_CH = 128
BLK_GROUPS = BLK_CH // SSM_GC
BLK_ST = BLK_GROUPS * SSM_N
N_BLK = SSM_W // BLK_CH
N_MEM = 256
MEM_HEADS = 4
MEM_HD = 64
EPS = 1e-6

ADAM_LR = 0.001
ADAM_B1 = 0.9
ADAM_B2 = 0.999
ADAM_EPS = 1e-08
ADAM_WD = 0.01
ADAM_STEP = 10

SUBLANES = 8
LANES = 128
V7X_VMEM_BYTES = 64 * 2**20
VMEM_LIMIT = V7X_VMEM_BYTES - 8 * 2**20
SCAN_COLS = 512
ROW_TILE = 256
MESH = pl.DeviceIdType.MESH

NT = (((1,), (1,)), ((), ()))
TN = (((0,), (0,)), ((), ()))


def _params(*sem):
    return pltpu.CompilerParams(dimension_semantics=sem, vmem_limit_bytes=VMEM_LIMIT)


def _dot(a, b):
    return jnp.dot(a, b, preferred_element_type=F32)


def _dot_nt(a, b):
    return lax.dot_general(a, b, NT, preferred_element_type=F32)


def _dot_tn(a, b):
    return lax.dot_general(a, b, TN, preferred_element_type=F32)


def _rms_scale(v):
    return lax.rsqrt(jnp.mean(v * v, axis=-1, keepdims=True) + EPS)


def _adamw(w, g, m, v):
    m = ADAM_B1 * m + (1.0 - ADAM_B1) * g
    v = ADAM_B2 * v + (1.0 - ADAM_B2) * (g * g)
    m_hat = m / (1.0 - ADAM_B1 ** ADAM_STEP)
    v_hat = v / (1.0 - ADAM_B2 ** ADAM_STEP)
    delta = -ADAM_LR * (m_hat / (jnp.sqrt(v_hat) + ADAM_EPS) + ADAM_WD * w)
    return delta, m, v


def in_proj(x2, g_pre, w_in):
    t = x2.shape[0]
    tm = min(512, t)

    def body(x_ref, g_ref, w_ref, o_ref):
        x = x_ref[...]
        h = (x * _rms_scale(x) * g_ref[...]).astype(BF16)
        for k in range(N_CHIPS):
            o_ref[:, pl.ds(k * W_IN_SHARD, W_IN_SHARD)] = _dot(h, w_ref[k])

    return pl.pallas_call(
        body, grid=(t // tm,),
        in_specs=[pl.BlockSpec((tm, D_MODEL), lambda i: (i, 0)),
                  pl.BlockSpec((1, D_MODEL), lambda i: (0, 0)),
                  pl.BlockSpec((N_CHIPS, D_MODEL, W_IN_SHARD), lambda i: (0, 0, 0))],
        out_specs=pl.BlockSpec((tm, 2 * D_MODEL), lambda i: (i, 0)),
        out_shape=jax.ShapeDtypeStruct((t, 2 * D_MODEL), F32),
        compiler_params=_params("arbitrary"), name="in_proj")(x2, g_pre, w_in)


def _pool_lane_window(shape):
    ch = lax.broadcasted_iota(jnp.int32, shape, 1)
    return jnp.where(ch < POOL_GW, 2.0, jnp.where(ch < 2 * POOL_GW, 4.0, jnp.where(ch < 3 * POOL_GW, 8.0, 16.0)))


def _pool_select(win, s2, s4, s8, s16):
    return jnp.where(win == 2.0, s2, jnp.where(win == 4.0, s4, jnp.where(win == 8.0, s8, s16)))


def _pool_div_count(v, win):
    pos = (lax.broadcasted_iota(jnp.int32, (POOL_PAD, POOL_W), 0) + 1).astype(F32)
    head = v[:POOL_PAD] / jnp.minimum(pos, win)
    return jnp.concatenate([head, v[POOL_PAD:] * (1.0 / win)], axis=0)


def _pool_diff(u, pad_ref, l):
    lo = POOL_PAD
    pad_ref[pl.ds(lo, l), :] = u
    s2 = u + pad_ref[pl.ds(lo - 1, l), :]
    pad_ref[pl.ds(lo, l), :] = s2
    s4 = s2 + pad_ref[pl.ds(lo - 2, l), :]
    pad_ref[pl.ds(lo, l), :] = s4
    s8 = s4 + pad_ref[pl.ds(lo - 4, l), :]
    pad_ref[pl.ds(lo, l), :] = s8
    s16 = s8 + pad_ref[pl.ds(lo - 8, l), :]
    win = _pool_lane_window((1, POOL_W))
    return _pool_div_count(_pool_select(win, s2, s4, s8, s16), win) - u, win


def pool_fwd(proj, w_blk, pool_scale, nb, l):
    def body(u_ref, w_ref, ps_ref, y_ref, pad_ref):
        pad_ref[pl.ds(0, POOL_PAD), :] = jnp.zeros((POOL_PAD, POOL_W), F32)
        d, _ = _pool_diff(u_ref[...], pad_ref, l)
        y_ref[...] = _dot(d.astype(BF16), w_ref[...]) * ps_ref[...]

    return pl.pallas_call(
        body, grid=(nb,),
        in_specs=[pl.BlockSpec((l, POOL_W), lambda b: (b, 0)),
                  pl.BlockSpec((POOL_W, POOL_W), lambda b: (0, 0)),
                  pl.BlockSpec((1, POOL_W), lambda b: (0, 0))],
        out_specs=pl.BlockSpec((l, POOL_W), lambda b: (b, 0)),
        out_shape=jax.ShapeDtypeStruct((nb * l, POOL_W), F32),
        scratch_shapes=[pltpu.VMEM((POOL_PAD + l, POOL_W), F32)],
        compiler_params=_params("arbitrary"), name="pool_fwd")(proj, w_blk, pool_scale)


def pool_bwd(proj, d_ycat, w_blk, pool_scale, nb, l):
    def body(u_ref, dy_ref, w_ref, ps_ref, du_ref, dw_ref, dps_ref, pad_ref, padb_ref):
        b = pl.program_id(0)

        @pl.when(b == 0)
        def _():
            dw_ref[...] = jnp.zeros_like(dw_ref)
            dps_ref[...] = jnp.zeros_like(dps_ref)

        pad_ref[pl.ds(0, POOL_PAD), :] = jnp.zeros((POOL_PAD, POOL_W), F32)
        padb_ref[pl.ds(l, POOL_PAD), :] = jnp.zeros((POOL_PAD, POOL_W), F32)
        d, win = _pool_diff(u_ref[...], pad_ref, l)
        db = d.astype(BF16)
        w = w_ref[...]
        dy = dy_ref[...]
        dps_ref[...] += jnp.sum(dy * _dot(db, w), axis=0, keepdims=True)
        dyo = (dy * ps_ref[...]).astype(BF16)
        dw_ref[...] += _dot_tn(db, dyo)
        dd = _dot_nt(dyo, w)
        e = _pool_div_count(dd, win)
        padb_ref[pl.ds(0, l), :] = e
        f2 = e + padb_ref[pl.ds(1, l), :]
        padb_ref[pl.ds(0, l), :] = f2
        f4 = f2 + padb_ref[pl.ds(2, l), :]
        padb_ref[pl.ds(0, l), :] = f4
        f8 = f4 + padb_ref[pl.ds(4, l), :]
        padb_ref[pl.ds(0, l), :] = f8
        f16 = f8 + padb_ref[pl.ds(8, l), :]
        du_ref[...] = (_pool_select(win, f2, f4, f8, f16) - dd).astype(BF16)

    return pl.pallas_call(
        body, grid=(nb,),
        in_specs=[pl.BlockSpec((l, POOL_W), lambda b: (b, 0)),
                  pl.BlockSpec((l, POOL_W), lambda b: (b, 0)),
                  pl.BlockSpec((POOL_W, POOL_W), lambda b: (0, 0)),
                  pl.BlockSpec((1, POOL_W), lambda b: (0, 0))],
        out_specs=[pl.BlockSpec((l, POOL_W), lambda b: (b, 0)),
                   pl.BlockSpec((POOL_W, POOL_W), lambda b: (0, 0)),
                   pl.BlockSpec((1, POOL_W), lambda b: (0, 0))],
        out_shape=[jax.ShapeDtypeStruct((nb * l, POOL_W), BF16),
                   jax.ShapeDtypeStruct((POOL_W, POOL_W), F32),
                   jax.ShapeDtypeStruct((1, POOL_W), F32)],
        scratch_shapes=[pltpu.VMEM((POOL_PAD + l, POOL_W), F32), pltpu.VMEM((POOL_PAD + l, POOL_W), F32)],
        compiler_params=_params("arbitrary"), name="pool_bwd")(proj, d_ycat, w_blk, pool_scale)


def _discretise(a_re, a_im, ldt, b_re, b_im):
    dt = jnp.exp(ldt)
    mag = jnp.exp(a_re * dt)
    th = a_im * dt
    lr = mag * jnp.cos(th)
    li = mag * jnp.sin(th)
    nr = lr - 1.0
    den = a_re * a_re + a_im * a_im
    fr = (nr * a_re + li * a_im) / den
    fi = (li * a_re - nr * a_im) / den
    return lr, li, fr * b_re - fi * b_im, fr * b_im + fi * b_re


def _cmul(ar, ai, br, bi):
    return ar * br - ai * bi, ar * bi + ai * br


def s5_scan_consts(a_re_row, a_im_row, ldt_row):
    def body(ar_ref, ai_ref, ld_ref, f_ref, b_ref):
        dt = jnp.exp(ld_ref[...])
        mag = jnp.exp(ar_ref[...] * dt)
        th = ai_ref[...] * dt
        lr = mag * jnp.cos(th)
        li = mag * jnp.sin(th)
        r = lax.broadcasted_iota(jnp.int32, (SUBLANES, NST), 0)
        for out_ref, sign, rev in ((f_ref, 1.0, False), (b_ref, -1.0, True)):
            p = [(lr, sign * li)]
            for _ in range(SUBLANES - 1):
                p.append(_cmul(p[-1][0], p[-1][1], lr, sign * li))
            for idx, s in enumerate((1, 2, 4)):
                inside = (r < SUBLANES - s) if rev else (r >= s)
                out_ref[pl.ds(2 * idx * SUBLANES, SUBLANES), :] = jnp.where(inside, p[s - 1][0], 0.0)
                out_ref[pl.ds((2 * idx + 1) * SUBLANES, SUBLANES), :] = jnp.where(inside, p[s - 1][1], 0.0)
            pr = jnp.zeros((SUBLANES, NST), F32)
            pi = jnp.zeros((SUBLANES, NST), F32)
            for row in range(SUBLANES):
                power = (SUBLANES - row) if rev else (row + 1)
                pr = jnp.where(r == row, p[power - 1][0], pr)
                pi = jnp.where(r == row, p[power - 1][1], pi)
            out_ref[pl.ds(6 * SUBLANES, SUBLANES), :] = pr
            out_ref[pl.ds(7 * SUBLANES, SUBLANES), :] = pi

    shape = jax.ShapeDtypeStruct((8 * SUBLANES, NST), F32)
    return pl.pallas_call(body, out_shape=[shape, shape], name="s5_scan_consts")(a_re_row, a_im_row, ldt_row)


def s5_bbar(a_re_c, a_im_c, ldt_c, b_re2, b_im2):
    def body(ar, ai, ld, br, bi, o_re, o_im):
        _, _, bbr, bbi = _discretise(ar[...], ai[...], ld[...], br[...], bi[...])
        o_re[...] = bbr
        o_im[...] = bbi

    shape = jax.ShapeDtypeStruct((NST, SSM_GC), F32)
    return pl.pallas_call(body, out_shape=[shape, shape], name="s5_bbar")(a_re_c, a_im_c, ldt_c, b_re2, b_im2)


def s5_param_bwd(a_re_c, a_im_c, ldt_c, b_re2, b_im2, d_lr, d_li, d_bbr, d_bbi):
    def body(ar, ai, ld, br, bi, g_lr, g_li, g_br, g_bi, o_ar, o_ai, o_ld, o_br, o_bi):
        _, vjp = jax.vjp(_discretise, ar[...], ai[...], ld[...], br[...], bi[...])
        d_ar, d_ai, d_ld, d_br, d_bi = vjp((g_lr[...], g_li[...], g_br[...], g_bi[...]))
        o_ar[...] = d_ar
        o_ai[...] = d_ai
        o_ld[...] = jnp.sum(d_ld.reshape(SSM_NG, SSM_N, 1), axis=1)
        o_br[...] = d_br
        o_bi[...] = d_bi

    col = jax.ShapeDtypeStruct((NST, 1), F32)
    mat = jax.ShapeDtypeStruct((NST, SSM_GC), F32)
    grp = jax.ShapeDtypeStruct((SSM_NG, 1), F32)
    return pl.pallas_call(body, out_shape=[col, col, grp, mat, mat], name="s5_param_bwd")(
        a_re_c, a_im_c, ldt_c, b_re2, b_im2, d_lr, d_li, d_bbr, d_bbi)


def _scan_tiles(buf_ref, cst_ref, carry_ref, rows, reverse, h_ref=None, acc_ref=None, reset=None, unrolled=False):
    n_tiles = rows // SUBLANES
    shifts = ((1, 0), (2, 2), (4, 4))
    row_id = lax.broadcasted_iota(jnp.int32, (SUBLANES, SCAN_COLS), 0)
    for j in range(NST // SCAN_COLS):
        c_re = pl.ds(j * SCAN_COLS, SCAN_COLS)
        c_im = pl.ds(NST + j * SCAN_COLS, SCAN_COLS)

        def step(i, carry, c_re=c_re, c_im=c_im):
            tile = (n_tiles - 1 - i) if reverse else i
            start = tile * SUBLANES
            rws = pl.ds(start if unrolled else pl.multiple_of(start, SUBLANES), SUBLANES)
            xr = buf_ref[rws, c_re]
            xi = buf_ref[rws, c_im]
            for s, base in shifts:
                amount = (SUBLANES - s) if reverse else s
                sr = pltpu.roll(xr, amount, 0)
                si = pltpu.roll(xi, amount, 0)
                mr = cst_ref[pl.ds(base * SUBLANES, SUBLANES), c_re]
                mi = cst_ref[pl.ds((base + 1) * SUBLANES, SUBLANES), c_re]
                xr, xi = xr + (mr * sr - mi * si), xi + (mr * si + mi * sr)
            pr = cst_ref[pl.ds(6 * SUBLANES, SUBLANES), c_re]
            pi = cst_ref[pl.ds(7 * SUBLANES, SUBLANES), c_re]
            cr, ci = carry[0], carry[1]
            xr, xi = xr + (pr * cr - pi * ci), xi + (pr * ci + pi * cr)
            buf_ref[rws, c_re] = xr
            buf_ref[rws, c_im] = xi
            edge = 0 if reverse else SUBLANES - 1
            ncr = jnp.broadcast_to(xr[edge:edge + 1, :], (SUBLANES, SCAN_COLS))
            nci = jnp.broadcast_to(xi[edge:edge + 1, :], (SUBLANES, SCAN_COLS))
            if not reverse:
                return ncr, nci
            gnr = jnp.where(row_id == SUBLANES - 1, cr, pltpu.roll(xr, SUBLANES - 1, 0))
            gni = jnp.where(row_id == SUBLANES - 1, ci, pltpu.roll(xi, SUBLANES - 1, 0))
            hr = h_ref[rws, c_re]
            hi = h_ref[rws, c_im]
            return ncr, nci, carry[2] + (gnr * hr + gni * hi), carry[3] + (gni * hr - gnr * hi)

        init = (carry_ref[:, c_re], carry_ref[:, c_im])
        if reset is not None:
            init = tuple(jnp.where(reset, 0.0, v) for v in init)
        if reverse:
            init = init + (acc_ref[:, c_re], acc_ref[:, c_im])
        if unrolled:
            out = init
            for i in range(n_tiles):
                out = step(i, out)
        else:
            out = lax.fori_loop(0, n_tiles, step, init)
        carry_ref[:, c_re] = out[0]
        carry_ref[:, c_im] = out[1]
        if reverse:
            acc_ref[:, c_re] = out[2]
            acc_ref[:, c_im] = out[3]


def _state_block(v, m):
    return jnp.concatenate([v[:, m * BLK_ST:(m + 1) * BLK_ST], v[:, NST + m * BLK_ST:NST + (m + 1) * BLK_ST]], axis=1)


def _put_state_block(buf_ref, m, val):
    buf_ref[:, pl.ds(m * BLK_ST, BLK_ST)] = val[:, :BLK_ST]
    buf_ref[:, pl.ds(NST + m * BLK_ST, BLK_ST)] = val[:, BLK_ST:]


def s5_fwd(proj, b_m, c_m, cst, d_skip, w_glu, nb, l):
    tt = min(256, l)
    nt = l // tt
    n_chunks = nb * nt

    def body(u_ref, b_ref, c_ref, cst_ref, ds_ref, wg_ref, y_ref, h_ref, yp_ref, z_ref,
             buf0, buf1, ub0, ub1, carry_ref):
        s = pl.program_id(0)

        @pl.when(s == 0)
        def _():
            for r in (buf0, buf1, ub0, ub1, carry_ref):
                r[...] = jnp.zeros_like(r)

        def step(p_buf, p_u, q_buf):
            h = p_buf[...].astype(BF16)
            h_ref[...] = h
            ypre = (jnp.concatenate([_dot(_state_block(h, m), c_ref[m]) for m in range(N_BLK)], axis=1)
                    + ds_ref[...] * p_u[...])
            yp_ref[...] = ypre
            z = _dot(jax.nn.gelu(ypre).astype(BF16), wg_ref[...])
            z_ref[...] = z
            y_ref[...] = z[:, :SSM_W] * jax.nn.sigmoid(z[:, SSM_W:])
            u = u_ref[...]
            p_u[...] = u
            ub = u.astype(BF16)
            for m in range(N_BLK):
                _put_state_block(p_buf, m, _dot(ub[:, m * BLK_CH:(m + 1) * BLK_CH], b_ref[m]))
            _scan_tiles(q_buf, cst_ref, carry_ref, tt, reverse=False, reset=lax.rem(s + nt - 1, nt) == 0, unrolled=True)

        @pl.when(lax.rem(s, 2) == 0)
        def _():
            step(buf0, ub0, buf1)

        @pl.when(lax.rem(s, 2) == 1)
        def _():
            step(buf1, ub1, buf0)

    row_in = lambda s: (jnp.minimum(s, n_chunks - 1), 1)
    row_out = lambda s: (jnp.maximum(s - 2, 0), 0)
    const = lambda s: (0, 0)
    const3 = lambda s: (0, 0, 0)
    return pl.pallas_call(
        body, grid=(n_chunks + 2,),
        in_specs=[pl.BlockSpec((tt, SSM_W), row_in),
                  pl.BlockSpec((N_BLK, BLK_CH, 2 * BLK_ST), const3),
                  pl.BlockSpec((N_BLK, 2 * BLK_ST, BLK_CH), const3),
                  pl.BlockSpec((8 * SUBLANES, NST), const),
                  pl.BlockSpec((1, SSM_W), const),
                  pl.BlockSpec((SSM_W, 2 * SSM_W), const)],
        out_specs=[pl.BlockSpec((tt, SSM_W), row_out),
                   pl.BlockSpec((tt, 2 * NST), row_out),
                   pl.BlockSpec((tt, SSM_W), row_out),
                   pl.BlockSpec((tt, 2 * SSM_W), row_out)],
        out_shape=[jax.ShapeDtypeStruct((nb * l, SSM_W), F32),
                   jax.ShapeDtypeStruct((nb * l, 2 * NST), BF16),
                   jax.ShapeDtypeStruct((nb * l, SSM_W), F32),
                   jax.ShapeDtypeStruct((nb * l, 2 * SSM_W), F32)],
        scratch_shapes=[pltpu.VMEM((tt, 2 * NST), F32), pltpu.VMEM((tt, 2 * NST), F32),
                        pltpu.VMEM((tt, SSM_W), F32), pltpu.VMEM((tt, SSM_W), F32),
                        pltpu.VMEM((SUBLANES, 2 * NST), F32)],
        compiler_params=_params("arbitrary"), name="s5_fwd")(proj, b_m, c_m, cst, d_skip, w_glu)


def s5_bwd(proj, d_ycat, h, ypre, z, b_m, c_m, cst_rev, d_skip, w_glu, nb, l):
    tt = min(256, l)
    nt = l // tt
    n_chunks = nb * nt

    def body(u_ref, dy_ref, h_ref, yp_ref, z_ref, b_ref, c_ref, cst_ref, ds_ref, wg_ref,
             du_ref, dwg_ref, dds_ref, dcc_ref, dbb_ref, dlam_ref,
             buf0, buf1, hb0, hb1, ub0, ub1, dyp0, dyp1, carry_ref, dc_ref, db_ref):
        s = pl.program_id(0)

        @pl.when(s == 0)
        def _():
            for r in (buf0, buf1, hb0, hb1, ub0, ub1, dyp0, dyp1, carry_ref, dc_ref, db_ref, dwg_ref, dds_ref, dlam_ref):
                r[...] = jnp.zeros_like(r)

        def step(p_buf, p_h, p_u, p_dyp, q_buf, q_h):
            g = p_buf[...].astype(BF16)
            ub_done = p_u[...].astype(BF16)
            du = []
            for m in range(N_BLK):
                g_m = _state_block(g, m)
                db_ref[m] += _dot_tn(g_m, ub_done[:, m * BLK_CH:(m + 1) * BLK_CH])
                du.append(_dot_nt(g_m, b_ref[m]))
            du_ref[...] = (jnp.concatenate(du, axis=1) + ds_ref[...] * p_dyp[...]).astype(BF16)
            u = u_ref[...]
            ypre = yp_ref[...]
            z = z_ref[...]
            z1 = z[:, :SSM_W]
            sg = jax.nn.sigmoid(z[:, SSM_W:])
            dy = dy_ref[...] * jnp.where(s < n_chunks, 1.0, 0.0)
            dz = jnp.concatenate([dy * sg, dy * z1 * sg * (1.0 - sg)], axis=1).astype(BF16)
            yg, gelu_vjp = jax.vjp(jax.nn.gelu, ypre)
            dwg_ref[...] += _dot_tn(yg.astype(BF16), dz)
            dypre = gelu_vjp(_dot_nt(dz, wg_ref[...]))[0]
            dds_ref[...] += jnp.sum(dypre * u, axis=0, keepdims=True)
            dyb = dypre.astype(BF16)
            hb = h_ref[...]
            p_h[...] = hb.astype(F32)
            p_u[...] = u
            p_dyp[...] = dypre
            for m in range(N_BLK):
                dy_m = dyb[:, m * BLK_CH:(m + 1) * BLK_CH]
                dc_ref[m] += _dot_tn(_state_block(hb, m), dy_m)
                _put_state_block(p_buf, m, _dot_nt(dy_m, c_ref[m]))
            _scan_tiles(q_buf, cst_ref, carry_ref, tt, reverse=True, h_ref=q_h, acc_ref=dlam_ref,
                        reset=lax.rem(s + nt - 1, nt) == 0, unrolled=True)

        @pl.when(lax.rem(s, 2) == 0)
        def _():
            step(buf0, hb0, ub0, dyp0, buf1, hb1)

        @pl.when(lax.rem(s, 2) == 1)
        def _():
            step(buf1, hb1, ub1, dyp1, buf0, hb0)

        @pl.when(s == n_chunks + 1)
        def _():
            dlam_ref[...] = jnp.broadcast_to(jnp.sum(dlam_ref[...], axis=0, keepdims=True), dlam_ref.shape)
            for acc_ref, out_ref in ((dc_ref, dcc_ref), (db_ref, dbb_ref)):
                for m in range(N_BLK):
                    for ri in range(2):
                        for gl in range(BLK_GROUPS):
                            out_ref[ri, pl.ds((m * BLK_GROUPS + gl) * SSM_N, SSM_N), :] = acc_ref[
                                m, pl.ds(ri * BLK_ST + gl * SSM_N, SSM_N), pl.ds(gl * SSM_GC, SSM_GC)]

    def chunk_rows(o):
        return lax.div(o, nt) * nt + (nt - 1 - lax.rem(o, nt))

    def rrow(col):
        return lambda s: (chunk_rows(jnp.minimum(s, n_chunks - 1)), col)

    const = lambda s: (0, 0)
    const3 = lambda s: (0, 0, 0)
    state_buf = pltpu.VMEM((tt, 2 * NST), F32)
    chan_buf = pltpu.VMEM((tt, SSM_W), F32)
    return pl.pallas_call(
        body, grid=(n_chunks + 2,),
        in_specs=[pl.BlockSpec((tt, SSM_W), rrow(1)),
                  pl.BlockSpec((tt, SSM_W), rrow(1)),
                  pl.BlockSpec((tt, 2 * NST), rrow(0)),
                  pl.BlockSpec((tt, SSM_W), rrow(0)),
                  pl.BlockSpec((tt, 2 * SSM_W), rrow(0)),
                  pl.BlockSpec((N_BLK, BLK_CH, 2 * BLK_ST), const3),
                  pl.BlockSpec((N_BLK, 2 * BLK_ST, BLK_CH), const3),
                  pl.BlockSpec((8 * SUBLANES, NST), const),
                  pl.BlockSpec((1, SSM_W), const),
                  pl.BlockSpec((SSM_W, 2 * SSM_W), const)],
        out_specs=[pl.BlockSpec((tt, SSM_W), lambda s: (chunk_rows(jnp.maximum(s - 2, 0)), 0)),
                   pl.BlockSpec((SSM_W, 2 * SSM_W), const),
                   pl.BlockSpec((1, SSM_W), const),
                   pl.BlockSpec((2, NST, SSM_GC), const3),
                   pl.BlockSpec((2, NST, SSM_GC), const3),
                   pl.BlockSpec((SUBLANES, 2 * NST), const)],
        out_shape=[jax.ShapeDtypeStruct((nb * l, SSM_W), BF16),
                   jax.ShapeDtypeStruct((SSM_W, 2 * SSM_W), F32),
                   jax.ShapeDtypeStruct((1, SSM_W), F32),
                   jax.ShapeDtypeStruct((2, NST, SSM_GC), F32),
                   jax.ShapeDtypeStruct((2, NST, SSM_GC), F32),
                   jax.ShapeDtypeStruct((SUBLANES, 2 * NST), F32)],
        scratch_shapes=[state_buf, state_buf, state_buf, state_buf, chan_buf, chan_buf, chan_buf, chan_buf,
                        pltpu.VMEM((SUBLANES, 2 * NST), F32),
                        pltpu.VMEM((N_BLK, 2 * BLK_ST, BLK_CH), F32), pltpu.VMEM((N_BLK, 2 * BLK_ST, BLK_CH), F32)],
        compiler_params=_params("arbitrary"), name="s5_bwd")(
            proj, d_ycat, h, ypre, z, b_m, c_m, cst_rev, d_skip, w_glu)


def _head_mask(hh):
    lane = lax.broadcasted_iota(jnp.int32, (1, ATT_W), 1)
    return (lane >= hh * MEM_HD) & (lane < (hh + 1) * MEM_HD)


def _mem_kv(mem_ref, gm_ref, wkv_ref):
    m = mem_ref[0]
    mh = m * _rms_scale(m)
    mb = (mh * gm_ref[...]).astype(BF16)
    kv = _dot(mb, wkv_ref[...])
    return mh, mb, kv[:, :ATT_W].astype(BF16), kv[:, ATT_W:].astype(BF16)


def _stack_heads(dst_ref, a):
    for hh in range(MEM_HEADS):
        dst_ref[pl.ds(hh * N_MEM, N_MEM), :] = jnp.where(_head_mask(hh), a, jnp.zeros_like(a))


def _fold_heads(a):
    out = jnp.zeros((N_MEM, ATT_W), a.dtype)
    for hh in range(MEM_HEADS):
        out = out + jnp.where(_head_mask(hh), a[hh * N_MEM:(hh + 1) * N_MEM], 0.0)
    return out


def _softmax_heads(s):
    parts = []
    for hh in range(MEM_HEADS):
        sh = s[:, hh * N_MEM:(hh + 1) * N_MEM]
        e = jnp.exp(sh - jnp.max(sh, axis=-1, keepdims=True))
        parts.append(e / jnp.sum(e, axis=-1, keepdims=True))
    return jnp.concatenate(parts, axis=1)


def attn_fwd(proj, mem, g_mem, w_kv, nb, l):
    tq = min(512, l)
    nq = l // tq
    scale = MEM_HD ** -0.5

    def body(q_ref, mem_ref, gm_ref, wkv_ref, o_ref, k_s, v_s):
        @pl.when(pl.program_id(1) == 0)
        def _():
            _, _, k, v = _mem_kv(mem_ref, gm_ref, wkv_ref)
            _stack_heads(k_s, k)
            _stack_heads(v_s, v)

        p = _softmax_heads(_dot_nt(q_ref[...].astype(BF16), k_s[...]) * scale)
        o_ref[...] = _dot(p.astype(BF16), v_s[...])

    const = lambda b, t: (0, 0)
    stacked = pltpu.VMEM((MEM_HEADS * N_MEM, ATT_W), BF16)
    return pl.pallas_call(
        body, grid=(nb, nq),
        in_specs=[pl.BlockSpec((tq, ATT_W), lambda b, t: (b * nq + t, 3)),
                  pl.BlockSpec((1, N_MEM, D_MODEL), lambda b, t: (b, 0, 0)),
                  pl.BlockSpec((1, D_MODEL), const),
                  pl.BlockSpec((D_MODEL, 2 * ATT_W), const)],
        out_specs=pl.BlockSpec((tq, ATT_W), lambda b, t: (b * nq + t, 0)),
        out_shape=jax.ShapeDtypeStruct((nb * l, ATT_W), F32),
        scratch_shapes=[stacked, stacked],
        compiler_params=_params("arbitrary", "arbitrary"), name="attn_fwd")(proj, mem, g_mem, w_kv)


def attn_bwd(proj, d_ycat, mem, g_mem, w_kv, nb, l):
    tq = min(512, l)
    nq = l // tq
    scale = MEM_HD ** -0.5

    def body(q_ref, do_ref, mem_ref, gm_ref, wkv_ref, dq_ref, dwkv_ref, dgm_ref, k_s, v_s, mb_s, dk_s, dv_s):
        b = pl.program_id(0)
        t = pl.program_id(1)

        @pl.when((b == 0) & (t == 0))
        def _():
            dwkv_ref[...] = jnp.zeros_like(dwkv_ref)
            dgm_ref[...] = jnp.zeros_like(dgm_ref)

        @pl.when(t == 0)
        def _():
            _, mb, k, v = _mem_kv(mem_ref, gm_ref, wkv_ref)
            _stack_heads(k_s, k)
            _stack_heads(v_s, v)
            mb_s[...] = mb
            dk_s[...] = jnp.zeros_like(dk_s)
            dv_s[...] = jnp.zeros_like(dv_s)

        q = q_ref[...].astype(BF16)
        do = do_ref[...].astype(BF16)
        k = k_s[...]
        p = _softmax_heads(_dot_nt(q, k) * scale)
        dp = _dot_nt(do, v_s[...])
        ds = []
        for hh in range(MEM_HEADS):
            blk = slice(hh * N_MEM, (hh + 1) * N_MEM)
            ds.append(p[:, blk] * (dp[:, blk] - jnp.sum(dp[:, blk] * p[:, blk], axis=-1, keepdims=True)) * scale)
        ds = jnp.concatenate(ds, axis=1).astype(BF16)
        dq_ref[...] = _dot(ds, k).astype(BF16)
        dk_s[...] += _dot_tn(ds, q)
        dv_s[...] += _dot_tn(p.astype(BF16), do)

        @pl.when(t == nq - 1)
        def _():
            dkv = jnp.concatenate([_fold_heads(dk_s[...]), _fold_heads(dv_s[...])], axis=1).astype(BF16)
            dwkv_ref[...] += _dot_tn(mb_s[...], dkv)
            m = mem_ref[0]
            dgm_ref[...] += jnp.sum(_dot_nt(dkv, wkv_ref[...]) * (m * _rms_scale(m)), axis=0, keepdims=True)

    const = lambda b, t: (0, 0)
    return pl.pallas_call(
        body, grid=(nb, nq),
        in_specs=[pl.BlockSpec((tq, ATT_W), lambda b, t: (b * nq + t, 3)),
                  pl.BlockSpec((tq, ATT_W), lambda b, t: (b * nq + t, 3)),
                  pl.BlockSpec((1, N_MEM, D_MODEL), lambda b, t: (b, 0, 0)),
                  pl.BlockSpec((1, D_MODEL), const),
                  pl.BlockSpec((D_MODEL, 2 * ATT_W), const)],
        out_specs=[pl.BlockSpec((tq, ATT_W), lambda b, t: (b * nq + t, 0)),
                   pl.BlockSpec((D_MODEL, 2 * ATT_W), const),
                   pl.BlockSpec((1, D_MODEL), const)],
        out_shape=[jax.ShapeDtypeStruct((nb * l, ATT_W), BF16),
                   jax.ShapeDtypeStruct((D_MODEL, 2 * ATT_W), F32),
                   jax.ShapeDtypeStruct((1, D_MODEL), F32)],
        scratch_shapes=[pltpu.VMEM((MEM_HEADS * N_MEM, ATT_W), BF16), pltpu.VMEM((MEM_HEADS * N_MEM, ATT_W), BF16),
                        pltpu.VMEM((N_MEM, D_MODEL), BF16),
                        pltpu.VMEM((MEM_HEADS * N_MEM, ATT_W), F32), pltpu.VMEM((MEM_HEADS * N_MEM, ATT_W), F32)],
        compiler_params=_params("arbitrary", "arbitrary"), name="attn_bwd")(proj, d_ycat, mem, g_mem, w_kv)


def tail(y_pool, y_ssm, y_att, proj, x2, target, w_out, g_post):
    t = x2.shape[0]
    tm = min(ROW_TILE, t)

    def body(yp_ref, ys_ref, ya_ref, gate_ref, x_ref, tg_ref, w_ref, gp_ref,
             dz_ref, dyc_ref, dgate_ref, dw_ref, dgp_ref, loss_ref):
        @pl.when(pl.program_id(0) == 0)
        def _():
            dw_ref[...] = jnp.zeros_like(dw_ref)
            dgp_ref[...] = jnp.zeros_like(dgp_ref)
            loss_ref[...] = jnp.zeros_like(loss_ref)

        ycat = jnp.concatenate([yp_ref[...], ys_ref[...], ya_ref[...]], axis=1)
        gate = gate_ref[...]
        sg = jax.nn.sigmoid(gate)
        silu = gate * sg
        yb = (ycat * silu).astype(BF16)
        w = w_ref[...]
        out = _dot(yb, w)
        r2 = _rms_scale(out)
        oh = out * r2
        gp = gp_ref[...]
        err = (x_ref[...] + oh * gp) - tg_ref[...]
        loss_ref[...] += 0.5 * jnp.sum(jnp.mean(err * err, axis=-1, keepdims=True), axis=0, keepdims=True)
        dz = err * (1.0 / D_MODEL)
        dz_ref[...] = dz
        dgp_ref[...] += jnp.sum(dz * oh, axis=0, keepdims=True)
        dn = dz * gp
        dout = (r2 * (dn - oh * jnp.mean(dn * oh, axis=-1, keepdims=True))).astype(BF16)
        dw_ref[...] += _dot_tn(yb, dout)
        dy = _dot_nt(dout, w)
        dyc_ref[...] = dy * silu
        dgate_ref[...] = (dy * ycat * (sg * (1.0 + gate * (1.0 - sg)))).astype(BF16)

    row = lambda i: (i, 0)
    const = lambda i: (0, 0)
    full = jax.ShapeDtypeStruct((t, D_MODEL), F32)
    return pl.pallas_call(
        body, grid=(t // tm,),
        in_specs=[pl.BlockSpec((tm, POOL_W), row),
                  pl.BlockSpec((tm, SSM_W), row),
                  pl.BlockSpec((tm, ATT_W), row),
                  pl.BlockSpec((tm, D_MODEL), lambda i: (i, 1)),
                  pl.BlockSpec((tm, D_MODEL), row),
                  pl.BlockSpec((tm, D_MODEL), row),
                  pl.BlockSpec((D_MODEL, D_MODEL), const),
                  pl.BlockSpec((1, D_MODEL), const)],
        out_specs=[pl.BlockSpec((tm, D_MODEL), row),
                   pl.BlockSpec((tm, D_MODEL), row),
                   pl.BlockSpec((tm, D_MODEL), row),
                   pl.BlockSpec((D_MODEL, D_MODEL), const),
                   pl.BlockSpec((1, D_MODEL), const),
                   pl.BlockSpec((1, LANES), const)],
        out_shape=[full, full, jax.ShapeDtypeStruct((t, D_MODEL), BF16),
                   jax.ShapeDtypeStruct((D_MODEL, D_MODEL), F32),
                   jax.ShapeDtypeStruct((1, D_MODEL), F32),
                   jax.ShapeDtypeStruct((1, LANES), F32)],
        compiler_params=_params("arbitrary"), name="tail")(y_pool, y_ssm, y_att, proj, x2, target, w_out, g_post)


def in_proj_bwd(du_pool, du_ssm, dq, d_gate, x2, dz, g_pre, w_in):
    t = x2.shape[0]
    tm = min(ROW_TILE, t)
    shard = W_IN_SHARD

    def body(dup_ref, dus_ref, dq_ref, dg_ref, x_ref, dz_ref, g_ref, w_ref, gx_ref, dw_ref, dgp_ref):
        @pl.when(pl.program_id(0) == 0)
        def _():
            dw_ref[...] = jnp.zeros_like(dw_ref)
            dgp_ref[...] = jnp.zeros_like(dgp_ref)

        dproj = jnp.concatenate([dup_ref[...], dus_ref[...], dq_ref[...], dg_ref[...]], axis=1)
        x = x_ref[...]
        r1 = _rms_scale(x)
        xh = x * r1
        g = g_ref[...]
        hb = (xh * g).astype(BF16)
        dh = jnp.zeros((tm, D_MODEL), F32)
        for k in range(N_CHIPS):
            dp_k = dproj[:, k * shard:(k + 1) * shard]
            dw_ref[k] += _dot_tn(hb, dp_k)
            dh = dh + _dot_nt(dp_k, w_ref[k])
        dgp_ref[...] += jnp.sum(dh * xh, axis=0, keepdims=True)
        dn = dh * g
        gx_ref[...] = dz_ref[...] + r1 * (dn - xh * jnp.mean(dn * xh, axis=-1, keepdims=True))

    row = lambda i: (i, 0)
    const = lambda i: (0, 0)
    return pl.pallas_call(
        body, grid=(t // tm,),
        in_specs=[pl.BlockSpec((tm, POOL_W), row),
                  pl.BlockSpec((tm, SSM_W), row),
                  pl.BlockSpec((tm, ATT_W), row),
                  pl.BlockSpec((tm, D_MODEL), row),
                  pl.BlockSpec((tm, D_MODEL), row),
                  pl.BlockSpec((tm, D_MODEL), row),
                  pl.BlockSpec((1, D_MODEL), const),
                  pl.BlockSpec((N_CHIPS, D_MODEL, shard), lambda i: (0, 0, 0))],
        out_specs=[pl.BlockSpec((tm, D_MODEL), row),
                   pl.BlockSpec((N_CHIPS, D_MODEL, shard), lambda i: (0, 0, 0)),
                   pl.BlockSpec((1, D_MODEL), const)],
        out_shape=[jax.ShapeDtypeStruct((t, D_MODEL), F32),
                   jax.ShapeDtypeStruct((N_CHIPS, D_MODEL, shard), F32),
                   jax.ShapeDtypeStruct((1, D_MODEL), F32)],
        compiler_params=_params("arbitrary"), name="in_proj_bwd")(du_pool, du_ssm, dq, d_gate, x2, dz, g_pre, w_in)


def _block_diag(blocks):
    g, r, c = blocks.shape
    eye = jnp.eye(g, dtype=blocks.dtype)
    return jnp.einsum("grc,gh->grhc", blocks, eye).reshape(g * r, g * c)


def _block_diag_extract(mat, g, r, c):
    eye = jnp.eye(g, dtype=mat.dtype)
    return jnp.einsum("grhc,gh->grc", mat.reshape(g, r, g, c), eye)


def local_step(x, mem, target, g_pre, w_in, w_pool, pool_scale, a_re, a_im, log_dt, b_re, b_im, c_re, c_im, d_skip,
               w_glu, g_mem, w_kv, w_out, g_post):
    nb, l, _ = x.shape
    x2 = x.reshape(nb * l, D_MODEL)
    tg2 = target.reshape(nb * l, D_MODEL)

    rowv = lambda a: a.reshape(1, NST)
    colv = lambda a: a.reshape(NST, 1)
    ldt_full = jnp.broadcast_to(log_dt.reshape(SSM_NG, 1), (SSM_NG, SSM_N))
    b_re2 = b_re.reshape(NST, SSM_GC)
    b_im2 = b_im.reshape(NST, SSM_GC)
    cst_f, cst_b = s5_scan_consts(rowv(a_re), rowv(a_im), rowv(ldt_full))
    bbr, bbi = s5_bbar(colv(a_re), colv(a_im), colv(ldt_full), b_re2, b_im2)
    eye = jnp.eye(BLK_GROUPS, dtype=F32)
    blk_in = lambda bb: jnp.einsum("mgnc,gh->mhcgn", bb.reshape(N_BLK, BLK_GROUPS, SSM_N, SSM_GC), eye).reshape(
        N_BLK, BLK_CH, BLK_ST)
    blk_out = lambda cc: jnp.einsum("mgcn,gh->mgnhc", cc.reshape(N_BLK, BLK_GROUPS, SSM_GC, SSM_N), eye).reshape(
        N_BLK, BLK_ST, BLK_CH)
    b_m = jnp.concatenate([blk_in(bbr), blk_in(bbi)], axis=2).astype(BF16)
    c_m = jnp.concatenate([blk_out(c_re), -blk_out(c_im)], axis=1).astype(BF16)
    w_pool_blk = _block_diag(w_pool.reshape(4, POOL_GW, POOL_GW)).astype(BF16)

    proj = in_proj(x2, g_pre, w_in)
    y_pool = pool_fwd(proj, w_pool_blk, pool_scale, nb, l)
    y_ssm, h, ypre, z = s5_fwd(proj, b_m, c_m, cst_f, d_skip, w_glu, nb, l)
    y_att = attn_fwd(proj, mem, g_mem, w_kv, nb, l)
    dz, d_ycat, d_gate, d_w_out, d_g_post, loss = tail(y_pool, y_ssm, y_att, proj, x2, tg2, w_out, g_post)
    du_pool, d_wp_blk, d_pool_scale = pool_bwd(proj, d_ycat, w_pool_blk, pool_scale, nb, l)
    du_ssm, d_w_glu, d_d_skip, d_cc, d_bb, d_lam = s5_bwd(
        proj, d_ycat, h, ypre, z, b_m, c_m, cst_b, d_skip, w_glu, nb, l)
    dq, d_w_kv, d_g_mem = attn_bwd(proj, d_ycat, mem, g_mem, w_kv, nb, l)
    grad_x, d_w_in, d_g_pre = in_proj_bwd(du_pool, du_ssm, dq, d_gate, x2, dz, g_pre, w_in)

    d_c = d_cc[0].reshape(SSM_NG, SSM_N, SSM_GC)
    d_ci = d_cc[1].reshape(SSM_NG, SSM_N, SSM_GC)
    d_lam_row = d_lam[0]
    d_ar, d_ai, d_ld, d_br, d_bi = s5_param_bwd(
        colv(a_re), colv(a_im), colv(ldt_full), b_re2, b_im2,
        d_lam_row[:NST].reshape(NST, 1), d_lam_row[NST:].reshape(NST, 1), d_bb[0], d_bb[1])

    grads = {
        "g_pre": d_g_pre,
        "w_in": d_w_in,
        "w_pool": _block_diag_extract(d_wp_blk, 4, POOL_GW, POOL_GW).reshape(1, 4, POOL_GW, POOL_GW),
        "pool_scale": d_pool_scale,
        "a_re": d_ar.reshape(1, SSM_NG, SSM_N),
        "a_im": d_ai.reshape(1, SSM_NG, SSM_N),
        "log_dt": d_ld.reshape(1, SSM_NG),
        "b_re": d_br.reshape(1, SSM_NG, SSM_N, SSM_GC),
        "b_im": d_bi.reshape(1, SSM_NG, SSM_N, SSM_GC),
        "c_re": d_c.transpose(0, 2, 1).reshape(1, SSM_NG, SSM_GC, SSM_N),
        "c_im": (-d_ci).transpose(0, 2, 1).reshape(1, SSM_NG, SSM_GC, SSM_N),
        "d_skip": d_d_skip,
        "w_glu": d_w_glu,
        "g_mem": d_g_mem,
        "w_kv": d_w_kv,
        "w_out": d_w_out,
        "g_post": d_g_post,
    }
    return loss, grad_x.reshape(nb, l, D_MODEL), grads


VMEM_SPEC = pl.BlockSpec(memory_space=pltpu.VMEM)
ANY_SPEC = pl.BlockSpec(memory_space=pl.ANY)


def _place():
    return lax.axis_index("x"), lax.axis_index("y"), lax.axis_index("c")


def _other_chips(x, y):
    return [(1 - x, y), (x, 1 - y), (1 - x, 1 - y)]


def gather_weights(shards):
    n = len(shards)

    def body(*refs):
        ins, outs = refs[:n], refs[n:2 * n]
        send_sems, recv_sems = refs[2 * n:]
        x, y, c = _place()
        k = 2 * x + y
        chips = _other_chips(x, y)
        for i in range(n):
            outs[i][k] = ins[i][...].astype(BF16)

        def copy(i, block, core_half, to, sem):
            hr = shards[i].shape[0] // 2
            rows = outs[i].at[block, pl.ds(pl.multiple_of(core_half * hr, 2 * SUBLANES), hr)]
            return pltpu.make_async_remote_copy(
                src_ref=rows, dst_ref=rows, send_sem=send_sems.at[sem], recv_sem=recv_sems.at[sem],
                device_id=to, device_id_type=MESH)

        pairs = [(i, j) for i in range(n) for j in range(3)]
        first = [copy(i, k, c, (*chips[j], c), i * 3 + j) for i, j in pairs]
        for cp in first:
            cp.start()
        passed = []
        for i, j in pairs:
            px, py = chips[j]
            copy(i, 2 * px + py, c, (px, py, c), i * 3 + j).wait_recv()
            fw = copy(i, 2 * px + py, c, (x, y, 1 - c), 3 * n + i * 3 + j)
            fw.start()
            passed.append(fw)
        for i, j in pairs:
            px, py = chips[j]
            copy(i, 2 * px + py, 1 - c, (x, y, 1 - c), 3 * n + i * 3 + j).wait_recv()
        for cp in first + passed:
            cp.wait_send()

    return pl.pallas_call(
        body,
        out_shape=[jax.ShapeDtypeStruct((N_CHIPS,) + s.shape, BF16) for s in shards],
        in_specs=[VMEM_SPEC] * n, out_specs=[VMEM_SPEC] * n,
        scratch_shapes=[pltpu.SemaphoreType.DMA((6 * n,)), pltpu.SemaphoreType.DMA((6 * n,))],
        compiler_params=pltpu.CompilerParams(vmem_limit_bytes=VMEM_LIMIT),
        name="gather_weights")(*shards)


SEMS_PER_ITEM = 5


def reduce_all(g4s, packed):
    n = len(g4s)
    dims = [g.shape[1:] for g in g4s]
    pr = packed.shape[0]

    def body(*refs):
        g_refs, p_ref = refs[:n], refs[n]
        outs, op = refs[n + 1:2 * n + 1], refs[2 * n + 1]
        scr = refs[2 * n + 2:]
        mine, theirs, sendb, recvb = scr[0:n], scr[n:2 * n], scr[2 * n:3 * n], scr[3 * n:4 * n]
        p_theirs, gat, lsems, send_sems, recv_sems = scr[4 * n:]
        x, y, c = _place()
        k = 2 * x + y
        chips = _other_chips(x, y)
        sib = (x, y, 1 - c)

        def halves(rows):
            hr = rows // 2
            return (pl.ds(pl.multiple_of(c * hr, SUBLANES), hr), pl.ds(pl.multiple_of((1 - c) * hr, SUBLANES), hr))

        def remote(src, dst, sem, to):
            return pltpu.make_async_remote_copy(
                src_ref=src, dst_ref=dst, send_sem=send_sems.at[sem], recv_sem=recv_sems.at[sem],
                device_id=to, device_id_type=MESH)

        loads, started = [], []
        for i in range(n):
            my_rows, sib_rows = halves(dims[i][0])
            ld = pltpu.make_async_copy(g_refs[i].at[:, my_rows, :], mine[i], lsems.at[i])
            ld.start()
            loads.append(ld)
            sw = remote(g_refs[i].at[:, sib_rows, :], theirs[i], SEMS_PER_ITEM * i, sib)
            sw.start()
            started.append(sw)
        p_my, p_sib = halves(pr)
        p_swap = remote(p_ref.at[p_sib, :], p_theirs, SEMS_PER_ITEM * n, sib)
        p_swap.start()
        started.append(p_swap)

        p_swap.wait_recv()
        gat[k] = p_ref[p_my, :] + p_theirs[...]
        for j, (px, py) in enumerate(chips):
            cp = remote(gat.at[k], gat.at[k], SEMS_PER_ITEM * n + 1 + j, (px, py, c))
            cp.start()
            started.append(cp)
        ici = []
        for i in range(n):
            loads[i].wait()
            started[i].wait_recv()
            for j, (px, py) in enumerate(chips):
                kp = 2 * px + py
                sendb[i][j] = (mine[i][kp] + theirs[i][kp]).astype(BF16)
                cp = remote(sendb[i].at[j], recvb[i].at[j], SEMS_PER_ITEM * i + 1 + j, (px, py, c))
                cp.start()
                ici.append(cp)

        for j, (px, py) in enumerate(chips):
            kp = 2 * px + py
            remote(gat.at[kp], gat.at[kp], SEMS_PER_ITEM * n + 1 + j, (px, py, c)).wait_recv()
        op[p_my, :] = ((gat[0] + gat[1]) + gat[2]) + gat[3]
        last = remote(op.at[p_my, :], op.at[p_my, :], SEMS_PER_ITEM * n + 4, sib)
        last.start()
        started.append(last)
        for i in range(n):
            my_rows, _ = halves(dims[i][0])
            half = mine[i][k] + theirs[i][k]
            for j in range(3):
                ici[3 * i + j].wait_recv()
                half = half + recvb[i][j].astype(F32)
            outs[i][my_rows, :] = half
            last = remote(outs[i].at[my_rows, :], outs[i].at[my_rows, :], SEMS_PER_ITEM * i + 4, sib)
            last.start()
            started.append(last)
        remote(op.at[p_sib, :], op.at[p_sib, :], SEMS_PER_ITEM * n + 4, sib).wait_recv()
        for i in range(n):
            _, sib_rows = halves(dims[i][0])
            remote(outs[i].at[sib_rows, :], outs[i].at[sib_rows, :], SEMS_PER_ITEM * i + 4, sib).wait_recv()
        for cp in started + ici:
            cp.wait_send()

    n_sems = SEMS_PER_ITEM * (n + 1)
    scratch = ([pltpu.VMEM((N_CHIPS, r // 2, cd), F32) for r, cd in dims] * 2
               + [pltpu.VMEM((3, r // 2, cd), BF16) for r, cd in dims] * 2
               + [pltpu.VMEM((pr // 2, LANES), F32), pltpu.VMEM((N_CHIPS, pr // 2, LANES), F32),
                  pltpu.SemaphoreType.DMA((n,)), pltpu.SemaphoreType.DMA((n_sems,)), pltpu.SemaphoreType.DMA((n_sems,))])
    res = pl.pallas_call(
        body,
        out_shape=[jax.ShapeDtypeStruct(d, F32) for d in dims] + [jax.ShapeDtypeStruct(packed.shape, F32)],
        in_specs=[ANY_SPEC] * n + [VMEM_SPEC], out_specs=[VMEM_SPEC] * (n + 1),
        scratch_shapes=scratch,
        compiler_params=pltpu.CompilerParams(vmem_limit_bytes=VMEM_LIMIT),
        name="reduce_all")(*g4s, packed)
    return res[:n], res[n]


def adamw_all(ws, gs, ms, vs):
    n = len(ws)

    def body(*refs):
        w, g, m, v = refs[:n], refs[n:2 * n], refs[2 * n:3 * n], refs[3 * n:4 * n]
        od, om, ov = refs[4 * n:5 * n], refs[5 * n:6 * n], refs[6 * n:]
        for i in range(n):
            od[i][...], om[i][...], ov[i][...] = _adamw(w[i][...], g[i][...], m[i][...], v[i][...])

    shapes = [jax.ShapeDtypeStruct(a.shape, F32) for a in ws]
    res = pl.pallas_call(
        body, out_shape=shapes * 3, in_specs=[VMEM_SPEC] * (4 * n), out_specs=[VMEM_SPEC] * (3 * n),
        compiler_params=pltpu.CompilerParams(vmem_limit_bytes=VMEM_LIMIT),
        name="adamw_all")(*ws, *gs, *ms, *vs)
    return res[:n], res[n:2 * n], res[2 * n:]


WEIGHTS = ("g_pre", "w_in", "w_pool", "pool_scale", "a_re", "a_im", "log_dt", "b_re", "b_im", "c_re", "c_im", "d_skip",
           "w_glu", "g_mem", "w_kv", "w_out", "g_post")
SHARDED = ("w_in", "w_glu", "w_kv", "w_out")
REPLICATED = tuple(n for n in WEIGHTS if n not in SHARDED)
PACK_TILE = SUBLANES * LANES


def _pack(arrays):
    parts = []
    for a in arrays:
        flat = a.reshape(-1)
        parts.append(jnp.pad(flat, (0, -flat.shape[0] % PACK_TILE)).reshape(-1, LANES))
    rows = sum(p.shape[0] for p in parts)
    if rows % (2 * SUBLANES):
        parts.append(jnp.zeros((SUBLANES, LANES), F32))
    return jnp.concatenate(parts, axis=0)


def _unpack(packed, shapes):
    out, row = [], 0
    for shp in shapes:
        size = math.prod(shp)
        rows = -(-size // PACK_TILE) * SUBLANES
        out.append(packed[row:row + rows].reshape(-1)[:size].reshape(shp))
        row += rows
    return out


def _chip_major(a, n_cols):
    return a.reshape(a.shape[0], N_CHIPS, n_cols).transpose(1, 0, 2)


def _from_chip_major(a):
    return a.transpose(1, 0, 2).reshape(a.shape[1], N_CHIPS * a.shape[2])


def kernel(x, mem, g_pre, w_in, w_pool, pool_scale, a_re, a_im, log_dt, b_re, b_im, c_re, c_im, d_skip, w_glu, g_mem, w_kv, w_out, g_post, loss_target, m_g_pre, m_w_in, m_w_pool, m_pool_scale, m_a_re, m_a_im, m_log_dt, m_b_re, m_b_im, m_c_re, m_c_im, m_d_skip, m_w_glu, m_g_mem, m_w_kv, m_w_out, m_g_post, v_g_pre, v_w_in, v_w_pool, v_pool_scale, v_a_re, v_a_im, v_log_dt, v_b_re, v_b_im, v_c_re, v_c_im, v_d_skip, v_w_glu, v_g_mem, v_w_kv, v_w_out, v_g_post):
    given = dict(locals())
    wts = {n: given[n] for n in WEIGHTS}
    mom = {n: given["m_" + n] for n in WEIGHTS}
    var = {n: given["v_" + n] for n in WEIGHTS}

    g_in, g_glu, g_kv, g_out = gather_weights([wts[n][0] for n in SHARDED])
    w_in_full = g_in
    w_glu_full = _from_chip_major(g_glu)
    w_kv_full = g_kv.reshape(D_MODEL, 2 * ATT_W)
    w_out_full = g_out.reshape(D_MODEL, D_MODEL)

    loss_part, grad_x, grads = local_step(
        x, mem, loss_target, g_pre, w_in_full, w_pool[0], pool_scale, a_re[0], a_im[0], log_dt[0], b_re[0], b_im[0],
        c_re[0], c_im[0], d_skip, w_glu_full, g_mem, w_kv_full, w_out_full, g_post)

    partial4 = [
        grads["w_in"],
        _chip_major(grads["w_glu"], 2 * SSM_W // N_CHIPS),
        grads["w_kv"].reshape(N_CHIPS, D_MODEL // N_CHIPS, 2 * ATT_W),
        grads["w_out"].reshape(N_CHIPS, D_MODEL // N_CHIPS, D_MODEL),
    ]
    loss_tile = jnp.broadcast_to(loss_part, (SUBLANES, LANES))
    sharded_g, packed_g = reduce_all(partial4, _pack([grads[n] for n in REPLICATED] + [loss_tile]))
    small_g = _unpack(packed_g, [wts[n].shape for n in REPLICATED] + [(SUBLANES, LANES)])
    loss = small_g[-1][0, 0]
    grad = dict(zip(SHARDED, [g[None] for g in sharded_g]))
    grad.update(zip(REPLICATED, small_g[:-1]))

    deltas, new_ms, new_vs = adamw_all([wts[n] for n in WEIGHTS], [grad[n] for n in WEIGHTS],
                                       [mom[n] for n in WEIGHTS], [var[n] for n in WEIGHTS])
    return (loss, grad_x, *[grad[n] for n in WEIGHTS], *deltas, *new_ms, *new_vs)
```

```python
import functools
import math

import jax
import jax.numpy as jnp
from jax import lax
from jax.experimental import pallas as pl
from jax.experimental.pallas import tpu as pltpu

F32 = jnp.float32
BF16 = jnp.bfloat16

D_MODEL = 1024
POOL_W = 384
SSM_W = 384
ATT_W = 256
POOL_GW = 96
POOL_WINDOWS = (2, 4, 8, 16)
POOL_PAD = 16
SSM_NG = 24
SSM_N = 64
SSM_GC = 16
N_CHIPS = 4
N_DEV = 8
W_IN_SHARD = 2 * D_MODEL // N_CHIPS
NST = SSM_NG * SSM_N
BLK_CH = 128
BLK_GROUPS = BLK_CH // SSM_GC
BLK_ST = BLK_GROUPS * SSM_N
N_BLK = SSM_W // BLK_CH
N_MEM = 256
MEM_HEADS = 4
MEM_HD = 64
EPS = 1e-6

ADAM_LR = 0.001
ADAM_B1 = 0.9
ADAM_B2 = 0.999
ADAM_EPS = 1e-08
ADAM_WD = 0.01
ADAM_STEP = 10

SUBLANES = 8
LANES = 128
V7X_VMEM_BYTES = 64 * 2**20
VMEM_LIMIT = V7X_VMEM_BYTES - 8 * 2**20
SCAN_COLS = 512
ROW_TILE = 256
MESH = pl.DeviceIdType.MESH

NT = (((1,), (1,)), ((), ()))
TN = (((0,), (0,)), ((), ()))


def _params(*sem):
    return pltpu.CompilerParams(dimension_semantics=sem, vmem_limit_bytes=VMEM_LIMIT)


def _dot(a, b):
    return jnp.dot(a, b, preferred_element_type=F32)


def _dot_nt(a, b):
    return lax.dot_general(a, b, NT, preferred_element_type=F32)


def _dot_tn(a, b):
    return lax.dot_general(a, b, TN, preferred_element_type=F32)


def _rms_scale(v):
    return lax.rsqrt(jnp.mean(v * v, axis=-1, keepdims=True) + EPS)


def _adamw(w, g, m, v):
    m = ADAM_B1 * m + (1.0 - ADAM_B1) * g
    v = ADAM_B2 * v + (1.0 - ADAM_B2) * (g * g)
    m_hat = m / (1.0 - ADAM_B1 ** ADAM_STEP)
    v_hat = v / (1.0 - ADAM_B2 ** ADAM_STEP)
    delta = -ADAM_LR * (m_hat / (jnp.sqrt(v_hat) + ADAM_EPS) + ADAM_WD * w)
    return delta, m, v


def in_proj(x2, g_pre, w_in):
    t = x2.shape[0]
    tm = min(512, t)

    def body(x_ref, g_ref, w_ref, o_ref):
        x = x_ref[...]
        h = (x * _rms_scale(x) * g_ref[...]).astype(BF16)
        for k in range(N_CHIPS):
            o_ref[:, pl.ds(k * W_IN_SHARD, W_IN_SHARD)] = _dot(h, w_ref[k]).astype(BF16)

    return pl.pallas_call(
        body, grid=(t // tm,),
        in_specs=[pl.BlockSpec((tm, D_MODEL), lambda i: (i, 0)),
                  pl.BlockSpec((1, D_MODEL), lambda i: (0, 0)),
                  pl.BlockSpec((N_CHIPS, D_MODEL, W_IN_SHARD), lambda i: (0, 0, 0))],
        out_specs=pl.BlockSpec((tm, 2 * D_MODEL), lambda i: (i, 0)),
        out_shape=jax.ShapeDtypeStruct((t, 2 * D_MODEL), BF16),
        compiler_params=_params("arbitrary"), name="in_proj")(x2, g_pre, w_in)


def _pool_lane_window(shape):
    ch = lax.broadcasted_iota(jnp.int32, shape, 1)
    return jnp.where(ch < POOL_GW, 2.0, jnp.where(ch < 2 * POOL_GW, 4.0, jnp.where(ch < 3 * POOL_GW, 8.0, 16.0)))


def _pool_select(win, s2, s4, s8, s16):
    return jnp.where(win == 2.0, s2, jnp.where(win == 4.0, s4, jnp.where(win == 8.0, s8, s16)))


def _pool_div_count(v, win):
    pos = (lax.broadcasted_iota(jnp.int32, (POOL_PAD, POOL_W), 0) + 1).astype(F32)
    head = v[:POOL_PAD] / jnp.minimum(pos, win)
    return jnp.concatenate([head, v[POOL_PAD:] * (1.0 / win)], axis=0)


def _pool_diff(u, pad_ref, l):
    lo = POOL_PAD
    pad_ref[pl.ds(lo, l), :] = u
    s2 = u + pad_ref[pl.ds(lo - 1, l), :]
    pad_ref[pl.ds(lo, l), :] = s2
    s4 = s2 + pad_ref[pl.ds(lo - 2, l), :]
    pad_ref[pl.ds(lo, l), :] = s4
    s8 = s4 + pad_ref[pl.ds(lo - 4, l), :]
    pad_ref[pl.ds(lo, l), :] = s8
    s16 = s8 + pad_ref[pl.ds(lo - 8, l), :]
    win = _pool_lane_window((1, POOL_W))
    return _pool_div_count(_pool_select(win, s2, s4, s8, s16), win) - u, win


def pool_fwd(proj, w_blk, pool_scale, nb, l):
    def body(u_ref, w_ref, ps_ref, y_ref, pad_ref):
        pad_ref[pl.ds(0, POOL_PAD), :] = jnp.zeros((POOL_PAD, POOL_W), F32)
        d, _ = _pool_diff(u_ref[...].astype(F32), pad_ref, l)
        y_ref[...] = (_dot(d.astype(BF16), w_ref[...]) * ps_ref[...]).astype(BF16)

    return pl.pallas_call(
        body, grid=(nb,),
        in_specs=[pl.BlockSpec((l, POOL_W), lambda b: (b, 0)),
                  pl.BlockSpec((POOL_W, POOL_W), lambda b: (0, 0)),
                  pl.BlockSpec((1, POOL_W), lambda b: (0, 0))],
        out_specs=pl.BlockSpec((l, POOL_W), lambda b: (b, 0)),
        out_shape=jax.ShapeDtypeStruct((nb * l, POOL_W), BF16),
        scratch_shapes=[pltpu.VMEM((POOL_PAD + l, POOL_W), F32)],
        compiler_params=_params("arbitrary"), name="pool_fwd")(proj, w_blk, pool_scale)


def pool_bwd(proj, d_ycat, w_blk, pool_scale, nb, l):
    def body(u_ref, dy_ref, w_ref, ps_ref, du_ref, dw_ref, dps_ref, pad_ref, padb_ref):
        b = pl.program_id(0)

        @pl.when(b == 0)
        def _():
            dw_ref[...] = jnp.zeros_like(dw_ref)
            dps_ref[...] = jnp.zeros_like(dps_ref)

        pad_ref[pl.ds(0, POOL_PAD), :] = jnp.zeros((POOL_PAD, POOL_W), F32)
        padb_ref[pl.ds(l, POOL_PAD), :] = jnp.zeros((POOL_PAD, POOL_W), F32)
        d, win = _pool_diff(u_ref[...].astype(F32), pad_ref, l)
        db = d.astype(BF16)
        w = w_ref[...]
        dy = dy_ref[...].astype(F32)
        dps_ref[...] += jnp.sum(dy * _dot(db, w), axis=0, keepdims=True)
        dyo = (dy * ps_ref[...]).astype(BF16)
        dw_ref[...] += _dot_tn(db, dyo)
        dd = _dot_nt(dyo, w)
        e = _pool_div_count(dd, win)
        padb_ref[pl.ds(0, l), :] = e
        f2 = e + padb_ref[pl.ds(1, l), :]
        padb_ref[pl.ds(0, l), :] = f2
        f4 = f2 + padb_ref[pl.ds(2, l), :]
        padb_ref[pl.ds(0, l), :] = f4
        f8 = f4 + padb_ref[pl.ds(4, l), :]
        padb_ref[pl.ds(0, l), :] = f8
        f16 = f8 + padb_ref[pl.ds(8, l), :]
        du_ref[...] = (_pool_select(win, f2, f4, f8, f16) - dd).astype(BF16)

    return pl.pallas_call(
        body, grid=(nb,),
        in_specs=[pl.BlockSpec((l, POOL_W), lambda b: (b, 0)),
                  pl.BlockSpec((l, POOL_W), lambda b: (b, 0)),
                  pl.BlockSpec((POOL_W, POOL_W), lambda b: (0, 0)),
                  pl.BlockSpec((1, POOL_W), lambda b: (0, 0))],
        out_specs=[pl.BlockSpec((l, POOL_W), lambda b: (b, 0)),
                   pl.BlockSpec((POOL_W, POOL_W), lambda b: (0, 0)),
                   pl.BlockSpec((1, POOL_W), lambda b: (0, 0))],
        out_shape=[jax.ShapeDtypeStruct((nb * l, POOL_W), BF16),
                   jax.ShapeDtypeStruct((POOL_W, POOL_W), F32),
                   jax.ShapeDtypeStruct((1, POOL_W), F32)],
        scratch_shapes=[pltpu.VMEM((POOL_PAD + l, POOL_W), F32), pltpu.VMEM((POOL_PAD + l, POOL_W), F32)],
        compiler_params=_params("arbitrary"), name="pool_bwd")(proj, d_ycat, w_blk, pool_scale)


def _discretise(a_re, a_im, ldt, b_re, b_im):
    dt = jnp.exp(ldt)
    mag = jnp.exp(a_re * dt)
    th = a_im * dt
    lr = mag * jnp.cos(th)
    li = mag * jnp.sin(th)
    nr = lr - 1.0
    den = a_re * a_re + a_im * a_im
    fr = (nr * a_re + li * a_im) / den
    fi = (li * a_re - nr * a_im) / den
    return lr, li, fr * b_re - fi * b_im, fr * b_im + fi * b_re


def _cmul(ar, ai, br, bi):
    return ar * br - ai * bi, ar * bi + ai * br


def s5_scan_consts(a_re_row, a_im_row, ldt_row):
    def body(ar_ref, ai_ref, ld_ref, f_ref, b_ref):
        dt = jnp.exp(ld_ref[...])
        mag = jnp.exp(ar_ref[...] * dt)
        th = ai_ref[...] * dt
        lr = mag * jnp.cos(th)
        li = mag * jnp.sin(th)
        r = lax.broadcasted_iota(jnp.int32, (SUBLANES, NST), 0)
        for out_ref, sign, rev in ((f_ref, 1.0, False), (b_ref, -1.0, True)):
            p = [(lr, sign * li)]
            for _ in range(SUBLANES - 1):
                p.append(_cmul(p[-1][0], p[-1][1], lr, sign * li))
            for idx, s in enumerate((1, 2, 4)):
                inside = (r < SUBLANES - s) if rev else (r >= s)
                out_ref[pl.ds(2 * idx * SUBLANES, SUBLANES), :] = jnp.where(inside, p[s - 1][0], 0.0)
                out_ref[pl.ds((2 * idx + 1) * SUBLANES, SUBLANES), :] = jnp.where(inside, p[s - 1][1], 0.0)
            pr = jnp.zeros((SUBLANES, NST), F32)
            pi = jnp.zeros((SUBLANES, NST), F32)
            for row in range(SUBLANES):
                power = (SUBLANES - row) if rev else (row + 1)
                pr = jnp.where(r == row, p[power - 1][0], pr)
                pi = jnp.where(r == row, p[power - 1][1], pi)
            out_ref[pl.ds(6 * SUBLANES, SUBLANES), :] = pr
            out_ref[pl.ds(7 * SUBLANES, SUBLANES), :] = pi

    shape = jax.ShapeDtypeStruct((8 * SUBLANES, NST), F32)
    return pl.pallas_call(body, out_shape=[shape, shape], name="s5_scan_consts")(a_re_row, a_im_row, ldt_row)


def s5_bbar(a_re_c, a_im_c, ldt_c, b_re2, b_im2):
    def body(ar, ai, ld, br, bi, o_re, o_im):
        _, _, bbr, bbi = _discretise(ar[...], ai[...], ld[...], br[...], bi[...])
        o_re[...] = bbr
        o_im[...] = bbi

    shape = jax.ShapeDtypeStruct((NST, SSM_GC), F32)
    return pl.pallas_call(body, out_shape=[shape, shape], name="s5_bbar")(a_re_c, a_im_c, ldt_c, b_re2, b_im2)


def s5_param_bwd(a_re_c, a_im_c, ldt_c, b_re2, b_im2, d_lr, d_li, d_bbr, d_bbi):
    def body(ar, ai, ld, br, bi, g_lr, g_li, g_br, g_bi, o_ar, o_ai, o_ld, o_br, o_bi):
        _, vjp = jax.vjp(_discretise, ar[...], ai[...], ld[...], br[...], bi[...])
        d_ar, d_ai, d_ld, d_br, d_bi = vjp((g_lr[...], g_li[...], g_br[...], g_bi[...]))
        o_ar[...] = d_ar
        o_ai[...] = d_ai
        o_ld[...] = jnp.sum(d_ld.reshape(SSM_NG, SSM_N, 1), axis=1)
        o_br[...] = d_br
        o_bi[...] = d_bi

    col = jax.ShapeDtypeStruct((NST, 1), F32)
    mat = jax.ShapeDtypeStruct((NST, SSM_GC), F32)
    grp = jax.ShapeDtypeStruct((SSM_NG, 1), F32)
    return pl.pallas_call(body, out_shape=[col, col, grp, mat, mat], name="s5_param_bwd")(
        a_re_c, a_im_c, ldt_c, b_re2, b_im2, d_lr, d_li, d_bbr, d_bbi)


def _scan_tiles(buf_ref, cst_ref, carry_ref, rows, reverse, h_ref=None, acc_ref=None, reset=None, unrolled=False):
    n_tiles = rows // SUBLANES
    shifts = ((1, 0), (2, 2), (4, 4))
    row_id = lax.broadcasted_iota(jnp.int32, (SUBLANES, SCAN_COLS), 0)
    for j in range(NST // SCAN_COLS):
        c_re = pl.ds(j * SCAN_COLS, SCAN_COLS)
        c_im = pl.ds(NST + j * SCAN_COLS, SCAN_COLS)

        def step(i, carry, c_re=c_re, c_im=c_im):
            tile = (n_tiles - 1 - i) if reverse else i
            start = tile * SUBLANES
            rws = pl.ds(start if unrolled else pl.multiple_of(start, SUBLANES), SUBLANES)
            xr = buf_ref[rws, c_re]
            xi = buf_ref[rws, c_im]
            for s, base in shifts:
                amount = (SUBLANES - s) if reverse else s
                sr = pltpu.roll(xr, amount, 0)
                si = pltpu.roll(xi, amount, 0)
                mr = cst_ref[pl.ds(base * SUBLANES, SUBLANES), c_re]
                mi = cst_ref[pl.ds((base + 1) * SUBLANES, SUBLANES), c_re]
                xr, xi = xr + (mr * sr - mi * si), xi + (mr * si + mi * sr)
            pr = cst_ref[pl.ds(6 * SUBLANES, SUBLANES), c_re]
            pi = cst_ref[pl.ds(7 * SUBLANES, SUBLANES), c_re]
            cr, ci = carry[0], carry[1]
            xr, xi = xr + (pr * cr - pi * ci), xi + (pr * ci + pi * cr)
            buf_ref[rws, c_re] = xr
            buf_ref[rws, c_im] = xi
            edge = 0 if reverse else SUBLANES - 1
            ncr = jnp.broadcast_to(xr[edge:edge + 1, :], (SUBLANES, SCAN_COLS))
            nci = jnp.broadcast_to(xi[edge:edge + 1, :], (SUBLANES, SCAN_COLS))
            if not reverse:
                return ncr, nci
            gnr = jnp.where(row_id == SUBLANES - 1, cr, pltpu.roll(xr, SUBLANES - 1, 0))
            gni = jnp.where(row_id == SUBLANES - 1, ci, pltpu.roll(xi, SUBLANES - 1, 0))
            hr = h_ref[rws, c_re]
            hi = h_ref[rws, c_im]
            return ncr, nci, carry[2] + (gnr * hr + gni * hi), carry[3] + (gni * hr - gnr * hi)

        init = (carry_ref[:, c_re], carry_ref[:, c_im])
        if reset is not None:
            init = tuple(jnp.where(reset, 0.0, v) for v in init)
        if reverse:
            init = init + (acc_ref[:, c_re], acc_ref[:, c_im])
        if unrolled:
            out = init
            for i in range(n_tiles):
                out = step(i, out)
        else:
            out = lax.fori_loop(0, n_tiles, step, init)
        carry_ref[:, c_re] = out[0]
        carry_ref[:, c_im] = out[1]
        if reverse:
            acc_ref[:, c_re] = out[2]
            acc_ref[:, c_im] = out[3]


def _state_block(v, m):
    return jnp.concatenate([v[:, m * BLK_ST:(m + 1) * BLK_ST], v[:, NST + m * BLK_ST:NST + (m + 1) * BLK_ST]], axis=1)


def _put_state_block(buf_ref, m, val):
    buf_ref[:, pl.ds(m * BLK_ST, BLK_ST)] = val[:, :BLK_ST]
    buf_ref[:, pl.ds(NST + m * BLK_ST, BLK_ST)] = val[:, BLK_ST:]


def s5_fwd(proj, b_m, c_m, cst, d_skip, w_glu, nb, l):
    tt = min(256, l)
    nt = l // tt
    n_chunks = nb * nt

    def body(u_ref, b_ref, c_ref, cst_ref, ds_ref, wg_ref, y_ref, h_ref, yp_ref, z_ref,
             buf0, buf1, ub0, ub1, carry_ref):
        s = pl.program_id(0)

        @pl.when(s == 0)
        def _():
            for r in (buf0, buf1, ub0, ub1, carry_ref):
                r[...] = jnp.zeros_like(r)

        def step(p_buf, p_u, q_buf):
            h = p_buf[...].astype(BF16)
            h_ref[...] = h
            ypre = (jnp.concatenate([_dot(_state_block(h, m), c_ref[m]) for m in range(N_BLK)], axis=1)
                    + ds_ref[...] * p_u[...])
            yp_ref[...] = ypre
            z = _dot(jax.nn.gelu(ypre).astype(BF16), wg_ref[...])
            z_ref[...] = z
            y_ref[...] = (z[:, :SSM_W] * jax.nn.sigmoid(z[:, SSM_W:])).astype(BF16)
            ub = u_ref[...]
            p_u[...] = ub.astype(F32)
            for m in range(N_BLK):
                _put_state_block(p_buf, m, _dot(ub[:, m * BLK_CH:(m + 1) * BLK_CH], b_ref[m]))
            _scan_tiles(q_buf, cst_ref, carry_ref, tt, reverse=False, reset=lax.rem(s + nt - 1, nt) == 0, unrolled=True)

        @pl.when(lax.rem(s, 2) == 0)
        def _():
            step(buf0, ub0, buf1)

        @pl.when(lax.rem(s, 2) == 1)
        def _():
            step(buf1, ub1, buf0)

    row_in = lambda s: (jnp.minimum(s, n_chunks - 1), 1)
    row_out = lambda s: (jnp.maximum(s - 2, 0), 0)
    const = lambda s: (0, 0)
    const3 = lambda s: (0, 0, 0)
    return pl.pallas_call(
        body, grid=(n_chunks + 2,),
        in_specs=[pl.BlockSpec((tt, SSM_W), row_in),
                  pl.BlockSpec((N_BLK, BLK_CH, 2 * BLK_ST), const3),
                  pl.BlockSpec((N_BLK, 2 * BLK_ST, BLK_CH), const3),
                  pl.BlockSpec((8 * SUBLANES, NST), const),
                  pl.BlockSpec((1, SSM_W), const),
                  pl.BlockSpec((SSM_W, 2 * SSM_W), const)],
        out_specs=[pl.BlockSpec((tt, SSM_W), row_out),
                   pl.BlockSpec((tt, 2 * NST), row_out),
                   pl.BlockSpec((tt, SSM_W), row_out),
                   pl.BlockSpec((tt, 2 * SSM_W), row_out)],
        out_shape=[jax.ShapeDtypeStruct((nb * l, SSM_W), BF16),
                   jax.ShapeDtypeStruct((nb * l, 2 * NST), BF16),
                   jax.ShapeDtypeStruct((nb * l, SSM_W), F32),
                   jax.ShapeDtypeStruct((nb * l, 2 * SSM_W), F32)],
        scratch_shapes=[pltpu.VMEM((tt, 2 * NST), F32), pltpu.VMEM((tt, 2 * NST), F32),
                        pltpu.VMEM((tt, SSM_W), F32), pltpu.VMEM((tt, SSM_W), F32),
                        pltpu.VMEM((SUBLANES, 2 * NST), F32)],
        compiler_params=_params("arbitrary"), name="s5_fwd")(proj, b_m, c_m, cst, d_skip, w_glu)


def s5_bwd(proj, d_ycat, h, ypre, z, b_m, c_m, cst_rev, d_skip, w_glu, nb, l):
    tt = min(256, l)
    nt = l // tt
    n_chunks = nb * nt

    def body(u_ref, dy_ref, h_ref, yp_ref, z_ref, b_ref, c_ref, cst_ref, ds_ref, wg_ref,
             du_ref, dwg_ref, dds_ref, dcc_ref, dbb_ref, dlam_ref,
             buf0, buf1, hb0, hb1, ub0, ub1, dyp0, dyp1, carry_ref, dc_ref, db_ref):
        s = pl.program_id(0)

        @pl.when(s == 0)
        def _():
            for r in (buf0, buf1, hb0, hb1, ub0, ub1, dyp0, dyp1, carry_ref, dc_ref, db_ref, dwg_ref, dds_ref, dlam_ref):
                r[...] = jnp.zeros_like(r)

        def step(p_buf, p_h, p_u, p_dyp, q_buf, q_h):
            g = p_buf[...].astype(BF16)
            ub_done = p_u[...].astype(BF16)
            du = []
            for m in range(N_BLK):
                g_m = _state_block(g, m)
                db_ref[m] += _dot_tn(g_m, ub_done[:, m * BLK_CH:(m + 1) * BLK_CH])
                du.append(_dot_nt(g_m, b_ref[m]))
            du_ref[...] = (jnp.concatenate(du, axis=1) + ds_ref[...] * p_dyp[...]).astype(BF16)
            u = u_ref[...].astype(F32)
            ypre = yp_ref[...]
            z = z_ref[...]
            z1 = z[:, :SSM_W]
            sg = jax.nn.sigmoid(z[:, SSM_W:])
            dy = dy_ref[...].astype(F32) * jnp.where(s < n_chunks, 1.0, 0.0)
            dz = jnp.concatenate([dy * sg, dy * z1 * sg * (1.0 - sg)], axis=1).astype(BF16)
            yg, gelu_vjp = jax.vjp(jax.nn.gelu, ypre)
            dwg_ref[...] += _dot_tn(yg.astype(BF16), dz)
            dypre = gelu_vjp(_dot_nt(dz, wg_ref[...]))[0]
            dds_ref[...] += jnp.sum(dypre * u, axis=0, keepdims=True)
            dyb = dypre.astype(BF16)
            hb = h_ref[...]
            p_h[...] = hb.astype(F32)
            p_u[...] = u
            p_dyp[...] = dypre
            for m in range(N_BLK):
                dy_m = dyb[:, m * BLK_CH:(m + 1) * BLK_CH]
                dc_ref[m] += _dot_tn(_state_block(hb, m), dy_m)
                _put_state_block(p_buf, m, _dot_nt(dy_m, c_ref[m]))
            _scan_tiles(q_buf, cst_ref, carry_ref, tt, reverse=True, h_ref=q_h, acc_ref=dlam_ref,
                        reset=lax.rem(s + nt - 1, nt) == 0, unrolled=True)

        @pl.when(lax.rem(s, 2) == 0)
        def _():
            step(buf0, hb0, ub0, dyp0, buf1, hb1)

        @pl.when(lax.rem(s, 2) == 1)
        def _():
            step(buf1, hb1, ub1, dyp1, buf0, hb0)

        @pl.when(s == n_chunks + 1)
        def _():
            dlam_ref[...] = jnp.broadcast_to(jnp.sum(dlam_ref[...], axis=0, keepdims=True), dlam_ref.shape)
            for acc_ref, out_ref in ((dc_ref, dcc_ref), (db_ref, dbb_ref)):
                for m in range(N_BLK):
                    for ri in range(2):
                        for gl in range(BLK_GROUPS):
                            out_ref[ri, pl.ds((m * BLK_GROUPS + gl) * SSM_N, SSM_N), :] = acc_ref[
                                m, pl.ds(ri * BLK_ST + gl * SSM_N, SSM_N), pl.ds(gl * SSM_GC, SSM_GC)]

    def chunk_rows(o):
        return lax.div(o, nt) * nt + (nt - 1 - lax.rem(o, nt))

    def rrow(col):
        return lambda s: (chunk_rows(jnp.minimum(s, n_chunks - 1)), col)

    const = lambda s: (0, 0)
    const3 = lambda s: (0, 0, 0)
    state_buf = pltpu.VMEM((tt, 2 * NST), F32)
    chan_buf = pltpu.VMEM((tt, SSM_W), F32)
    return pl.pallas_call(
        body, grid=(n_chunks + 2,),
        in_specs=[pl.BlockSpec((tt, SSM_W), rrow(1)),
                  pl.BlockSpec((tt, SSM_W), rrow(1)),
                  pl.BlockSpec((tt, 2 * NST), rrow(0)),
                  pl.BlockSpec((tt, SSM_W), rrow(0)),
                  pl.BlockSpec((tt, 2 * SSM_W), rrow(0)),
                  pl.BlockSpec((N_BLK, BLK_CH, 2 * BLK_ST), const3),
                  pl.BlockSpec((N_BLK, 2 * BLK_ST, BLK_CH), const3),
                  pl.BlockSpec((8 * SUBLANES, NST), const),
                  pl.BlockSpec((1, SSM_W), const),
                  pl.BlockSpec((SSM_W, 2 * SSM_W), const)],
        out_specs=[pl.BlockSpec((tt, SSM_W), lambda s: (chunk_rows(jnp.maximum(s - 2, 0)), 0)),
                   pl.BlockSpec((SSM_W, 2 * SSM_W), const),
                   pl.BlockSpec((1, SSM_W), const),
                   pl.BlockSpec((2, NST, SSM_GC), const3),
                   pl.BlockSpec((2, NST, SSM_GC), const3),
                   pl.BlockSpec((SUBLANES, 2 * NST), const)],
        out_shape=[jax.ShapeDtypeStruct((nb * l, SSM_W), BF16),
                   jax.ShapeDtypeStruct((SSM_W, 2 * SSM_W), F32),
                   jax.ShapeDtypeStruct((1, SSM_W), F32),
                   jax.ShapeDtypeStruct((2, NST, SSM_GC), F32),
                   jax.ShapeDtypeStruct((2, NST, SSM_GC), F32),
                   jax.ShapeDtypeStruct((SUBLANES, 2 * NST), F32)],
        scratch_shapes=[state_buf, state_buf, state_buf, state_buf, chan_buf, chan_buf, chan_buf, chan_buf,
                        pltpu.VMEM((SUBLANES, 2 * NST), F32),
                        pltpu.VMEM((N_BLK, 2 * BLK_ST, BLK_CH), F32), pltpu.VMEM((N_BLK, 2 * BLK_ST, BLK_CH), F32)],
        compiler_params=_params("arbitrary"), name="s5_bwd")(
            proj, d_ycat, h, ypre, z, b_m, c_m, cst_rev, d_skip, w_glu)


def _head_mask(hh):
    lane = lax.broadcasted_iota(jnp.int32, (1, ATT_W), 1)
    return (lane >= hh * MEM_HD) & (lane < (hh + 1) * MEM_HD)


def _mem_kv(mem_ref, gm_ref, wkv_ref):
    m = mem_ref[0]
    mh = m * _rms_scale(m)
    mb = (mh * gm_ref[...]).astype(BF16)
    kv = _dot(mb, wkv_ref[...])
    return mh, mb, kv[:, :ATT_W].astype(BF16), kv[:, ATT_W:].astype(BF16)


def _stack_heads(dst_ref, a):
    for hh in range(MEM_HEADS):
        dst_ref[pl.ds(hh * N_MEM, N_MEM), :] = jnp.where(_head_mask(hh), a, jnp.zeros_like(a))


def _fold_heads(a):
    out = jnp.zeros((N_MEM, ATT_W), a.dtype)
    for hh in range(MEM_HEADS):
        out = out + jnp.where(_head_mask(hh), a[hh * N_MEM:(hh + 1) * N_MEM], 0.0)
    return out


def _softmax_heads(s):
    parts = []
    for hh in range(MEM_HEADS):
        sh = s[:, hh * N_MEM:(hh + 1) * N_MEM]
        e = jnp.exp(sh - jnp.max(sh, axis=-1, keepdims=True))
        parts.append(e / jnp.sum(e, axis=-1, keepdims=True))
    return jnp.concatenate(parts, axis=1)


def attn_fwd(proj, mem, g_mem, w_kv, nb, l):
    tq = min(512, l)
    nq = l // tq
    scale = MEM_HD ** -0.5

    def body(q_ref, mem_ref, gm_ref, wkv_ref, o_ref, k_s, v_s):
        @pl.when(pl.program_id(1) == 0)
        def _():
            _, _, k, v = _mem_kv(mem_ref, gm_ref, wkv_ref)
            _stack_heads(k_s, k)
            _stack_heads(v_s, v)

        p = _softmax_heads(_dot_nt(q_ref[...], k_s[...]) * scale)
        o_ref[...] = _dot(p.astype(BF16), v_s[...]).astype(BF16)

    const = lambda b, t: (0, 0)
    stacked = pltpu.VMEM((MEM_HEADS * N_MEM, ATT_W), BF16)
    return pl.pallas_call(
        body, grid=(nb, nq),
        in_specs=[pl.BlockSpec((tq, ATT_W), lambda b, t: (b * nq + t, 3)),
                  pl.BlockSpec((1, N_MEM, D_MODEL), lambda b, t: (b, 0, 0)),
                  pl.BlockSpec((1, D_MODEL), const),
                  pl.BlockSpec((D_MODEL, 2 * ATT_W), const)],
        out_specs=pl.BlockSpec((tq, ATT_W), lambda b, t: (b * nq + t, 0)),
        out_shape=jax.ShapeDtypeStruct((nb * l, ATT_W), BF16),
        scratch_shapes=[stacked, stacked],
        compiler_params=_params("arbitrary", "arbitrary"), name="attn_fwd")(proj, mem, g_mem, w_kv)


def attn_bwd(proj, d_ycat, mem, g_mem, w_kv, nb, l):
    tq = min(512, l)
    nq = l // tq
    scale = MEM_HD ** -0.5

    def body(q_ref, do_ref, mem_ref, gm_ref, wkv_ref, dq_ref, dwkv_ref, dgm_ref, k_s, v_s, mb_s, dk_s, dv_s):
        b = pl.program_id(0)
        t = pl.program_id(1)

        @pl.when((b == 0) & (t == 0))
        def _():
            dwkv_ref[...] = jnp.zeros_like(dwkv_ref)
            dgm_ref[...] = jnp.zeros_like(dgm_ref)

        @pl.when(t == 0)
        def _():
            _, mb, k, v = _mem_kv(mem_ref, gm_ref, wkv_ref)
            _stack_heads(k_s, k)
            _stack_heads(v_s, v)
            mb_s[...] = mb
            dk_s[...] = jnp.zeros_like(dk_s)
            dv_s[...] = jnp.zeros_like(dv_s)

        q = q_ref[...]
        do = do_ref[...]
        k = k_s[...]
        p = _softmax_heads(_dot_nt(q, k) * scale)
        dp = _dot_nt(do, v_s[...])
        ds = []
        for hh in range(MEM_HEADS):
            blk = slice(hh * N_MEM, (hh + 1) * N_MEM)
            ds.append(p[:, blk] * (dp[:, blk] - jnp.sum(dp[:, blk] * p[:, blk], axis=-1, keepdims=True)) * scale)
        ds = jnp.concatenate(ds, axis=1).astype(BF16)
        dq_ref[...] = _dot(ds, k).astype(BF16)
        dk_s[...] += _dot_tn(ds, q)
        dv_s[...] += _dot_tn(p.astype(BF16), do)

        @pl.when(t == nq - 1)
        def _():
            dkv = jnp.concatenate([_fold_heads(dk_s[...]), _fold_heads(dv_s[...])], axis=1).astype(BF16)
            dwkv_ref[...] += _dot_tn(mb_s[...], dkv)
            m = mem_ref[0]
            dgm_ref[...] += jnp.sum(_dot_nt(dkv, wkv_ref[...]) * (m * _rms_scale(m)), axis=0, keepdims=True)

    const = lambda b, t: (0, 0)
    return pl.pallas_call(
        body, grid=(nb, nq),
        in_specs=[pl.BlockSpec((tq, ATT_W), lambda b, t: (b * nq + t, 3)),
                  pl.BlockSpec((tq, ATT_W), lambda b, t: (b * nq + t, 3)),
                  pl.BlockSpec((1, N_MEM, D_MODEL), lambda b, t: (b, 0, 0)),
                  pl.BlockSpec((1, D_MODEL), const),
                  pl.BlockSpec((D_MODEL, 2 * ATT_W), const)],
        out_specs=[pl.BlockSpec((tq, ATT_W), lambda b, t: (b * nq + t, 0)),
                   pl.BlockSpec((D_MODEL, 2 * ATT_W), const),
                   pl.BlockSpec((1, D_MODEL), const)],
        out_shape=[jax.ShapeDtypeStruct((nb * l, ATT_W), BF16),
                   jax.ShapeDtypeStruct((D_MODEL, 2 * ATT_W), F32),
                   jax.ShapeDtypeStruct((1, D_MODEL), F32)],
        scratch_shapes=[pltpu.VMEM((MEM_HEADS * N_MEM, ATT_W), BF16), pltpu.VMEM((MEM_HEADS * N_MEM, ATT_W), BF16),
                        pltpu.VMEM((N_MEM, D_MODEL), BF16),
                        pltpu.VMEM((MEM_HEADS * N_MEM, ATT_W), F32), pltpu.VMEM((MEM_HEADS * N_MEM, ATT_W), F32)],
        compiler_params=_params("arbitrary", "arbitrary"), name="attn_bwd")(proj, d_ycat, mem, g_mem, w_kv)


def tail(y_pool, y_ssm, y_att, proj, x2, target, w_out, g_post):
    t = x2.shape[0]
    tm = min(ROW_TILE, t)

    def body(yp_ref, ys_ref, ya_ref, gate_ref, x_ref, tg_ref, w_ref, gp_ref,
             dz_ref, dyc_ref, dgate_ref, dw_ref, dgp_ref, loss_ref):
        @pl.when(pl.program_id(0) == 0)
        def _():
            dw_ref[...] = jnp.zeros_like(dw_ref)
            dgp_ref[...] = jnp.zeros_like(dgp_ref)
            loss_ref[...] = jnp.zeros_like(loss_ref)

        ycat = jnp.concatenate([yp_ref[...], ys_ref[...], ya_ref[...]], axis=1).astype(F32)
        gate = gate_ref[...].astype(F32)
        sg = jax.nn.sigmoid(gate)
        silu = gate * sg
        yb = (ycat * silu).astype(BF16)
        w = w_ref[...]
        out = _dot(yb, w)
        r2 = _rms_scale(out)
        oh = out * r2
        gp = gp_ref[...]
        err = (x_ref[...] + oh * gp) - tg_ref[...]
        loss_ref[...] += 0.5 * jnp.sum(jnp.mean(err * err, axis=-1, keepdims=True), axis=0, keepdims=True)
        dz = err * (1.0 / D_MODEL)
        dz_ref[...] = dz.astype(BF16)
        dgp_ref[...] += jnp.sum(dz * oh, axis=0, keepdims=True)
        dn = dz * gp
        dout = (r2 * (dn - oh * jnp.mean(dn * oh, axis=-1, keepdims=True))).astype(BF16)
        dw_ref[...] += _dot_tn(yb, dout)
        dy = _dot_nt(dout, w)
        dyc_ref[...] = (dy * silu).astype(BF16)
        dgate_ref[...] = (dy * ycat * (sg * (1.0 + gate * (1.0 - sg)))).astype(BF16)

    row = lambda i: (i, 0)
    const = lambda i: (0, 0)
    full = jax.ShapeDtypeStruct((t, D_MODEL), BF16)
    return pl.pallas_call(
        body, grid=(t // tm,),
        in_specs=[pl.BlockSpec((tm, POOL_W), row),
                  pl.BlockSpec((tm, SSM_W), row),
                  pl.BlockSpec((tm, ATT_W), row),
                  pl.BlockSpec((tm, D_MODEL), lambda i: (i, 1)),
                  pl.BlockSpec((tm, D_MODEL), row),
                  pl.BlockSpec((tm, D_MODEL), row),
                  pl.BlockSpec((D_MODEL, D_MODEL), const),
                  pl.BlockSpec((1, D_MODEL), const)],
        out_specs=[pl.BlockSpec((tm, D_MODEL), row),
                   pl.BlockSpec((tm, D_MODEL), row),
                   pl.BlockSpec((tm, D_MODEL), row),
                   pl.BlockSpec((D_MODEL, D_MODEL), const),
                   pl.BlockSpec((1, D_MODEL), const),
                   pl.BlockSpec((1, LANES), const)],
        out_shape=[full, full, full,
                   jax.ShapeDtypeStruct((D_MODEL, D_MODEL), F32),
                   jax.ShapeDtypeStruct((1, D_MODEL), F32),
                   jax.ShapeDtypeStruct((1, LANES), F32)],
        compiler_params=_params("arbitrary"), name="tail")(y_pool, y_ssm, y_att, proj, x2, target, w_out, g_post)


def in_proj_bwd(du_pool, du_ssm, dq, d_gate, x2, dz, g_pre, w_in):
    t = x2.shape[0]
    tm = min(ROW_TILE, t)
    shard = W_IN_SHARD

    def body(dup_ref, dus_ref, dq_ref, dg_ref, x_ref, dz_ref, g_ref, w_ref, gx_ref, dw_ref, dgp_ref):
        @pl.when(pl.program_id(0) == 0)
        def _():
            dw_ref[...] = jnp.zeros_like(dw_ref)
            dgp_ref[...] = jnp.zeros_like(dgp_ref)

        dproj = jnp.concatenate([dup_ref[...], dus_ref[...], dq_ref[...], dg_ref[...]], axis=1)
        x = x_ref[...]
        r1 = _rms_scale(x)
        xh = x * r1
        g = g_ref[...]
        hb = (xh * g).astype(BF16)
        dh = jnp.zeros((tm, D_MODEL), F32)
        for k in range(N_CHIPS):
            dp_k = dproj[:, k * shard:(k + 1) * shard]
            dw_ref[k] += _dot_tn(hb, dp_k)
            dh = dh + _dot_nt(dp_k, w_ref[k])
        dgp_ref[...] += jnp.sum(dh * xh, axis=0, keepdims=True)
        dn = dh * g
        gx_ref[...] = dz_ref[...].astype(F32) + r1 * (dn - xh * jnp.mean(dn * xh, axis=-1, keepdims=True))

    row = lambda i: (i, 0)
    const = lambda i: (0, 0)
    return pl.pallas_call(
        body, grid=(t // tm,),
        in_specs=[pl.BlockSpec((tm, POOL_W), row),
                  pl.BlockSpec((tm, SSM_W), row),
                  pl.BlockSpec((tm, ATT_W), row),
                  pl.BlockSpec((tm, D_MODEL), row),
                  pl.BlockSpec((tm, D_MODEL), row),
                  pl.BlockSpec((tm, D_MODEL), row),
                  pl.BlockSpec((1, D_MODEL), const),
                  pl.BlockSpec((N_CHIPS, D_MODEL, shard), lambda i: (0, 0, 0))],
        out_specs=[pl.BlockSpec((tm, D_MODEL), row),
                   pl.BlockSpec((N_CHIPS, D_MODEL, shard), lambda i: (0, 0, 0)),
                   pl.BlockSpec((1, D_MODEL), const)],
        out_shape=[jax.ShapeDtypeStruct((t, D_MODEL), F32),
                   jax.ShapeDtypeStruct((N_CHIPS, D_MODEL, shard), F32),
                   jax.ShapeDtypeStruct((1, D_MODEL), F32)],
        compiler_params=_params("arbitrary"), name="in_proj_bwd")(du_pool, du_ssm, dq, d_gate, x2, dz, g_pre, w_in)


def _block_diag(blocks):
    g, r, c = blocks.shape
    eye = jnp.eye(g, dtype=blocks.dtype)
    return jnp.einsum("grc,gh->grhc", blocks, eye).reshape(g * r, g * c)


def _block_diag_extract(mat, g, r, c):
    eye = jnp.eye(g, dtype=mat.dtype)
    return jnp.einsum("grhc,gh->grc", mat.reshape(g, r, g, c), eye)


def local_step(x, mem, target, g_pre, w_in, w_pool, pool_scale, a_re, a_im, log_dt, b_re, b_im, c_re, c_im, d_skip,
               w_glu, g_mem, w_kv, w_out, g_post):
    nb, l, _ = x.shape
    x2 = x.reshape(nb * l, D_MODEL)
    tg2 = target.reshape(nb * l, D_MODEL)

    rowv = lambda a: a.reshape(1, NST)
    colv = lambda a: a.reshape(NST, 1)
    ldt_full = jnp.broadcast_to(log_dt.reshape(SSM_NG, 1), (SSM_NG, SSM_N))
    b_re2 = b_re.reshape(NST, SSM_GC)
    b_im2 = b_im.reshape(NST, SSM_GC)
    cst_f, cst_b = s5_scan_consts(rowv(a_re), rowv(a_im), rowv(ldt_full))
    bbr, bbi = s5_bbar(colv(a_re), colv(a_im), colv(ldt_full), b_re2, b_im2)
    eye = jnp.eye(BLK_GROUPS, dtype=F32)
    blk_in = lambda bb: jnp.einsum("mgnc,gh->mhcgn", bb.reshape(N_BLK, BLK_GROUPS, SSM_N, SSM_GC), eye).reshape(
        N_BLK, BLK_CH, BLK_ST)
    blk_out = lambda cc: jnp.einsum("mgcn,gh->mgnhc", cc.reshape(N_BLK, BLK_GROUPS, SSM_GC, SSM_N), eye).reshape(
        N_BLK, BLK_ST, BLK_CH)
    b_m = jnp.concatenate([blk_in(bbr), blk_in(bbi)], axis=2).astype(BF16)
    c_m = jnp.concatenate([blk_out(c_re), -blk_out(c_im)], axis=1).astype(BF16)
    w_pool_blk = _block_diag(w_pool.reshape(4, POOL_GW, POOL_GW)).astype(BF16)

    proj = in_proj(x2, g_pre, w_in)
    y_pool = pool_fwd(proj, w_pool_blk, pool_scale, nb, l)
    y_ssm, h, ypre, z = s5_fwd(proj, b_m, c_m, cst_f, d_skip, w_glu, nb, l)
    y_att = attn_fwd(proj, mem, g_mem, w_kv, nb, l)
    dz, d_ycat, d_gate, d_w_out, d_g_post, loss = tail(y_pool, y_ssm, y_att, proj, x2, tg2, w_out, g_post)
    du_pool, d_wp_blk, d_pool_scale = pool_bwd(proj, d_ycat, w_pool_blk, pool_scale, nb, l)
    du_ssm, d_w_glu, d_d_skip, d_cc, d_bb, d_lam = s5_bwd(
        proj, d_ycat, h, ypre, z, b_m, c_m, cst_b, d_skip, w_glu, nb, l)
    dq, d_w_kv, d_g_mem = attn_bwd(proj, d_ycat, mem, g_mem, w_kv, nb, l)
    grad_x, d_w_in, d_g_pre = in_proj_bwd(du_pool, du_ssm, dq, d_gate, x2, dz, g_pre, w_in)

    d_c = d_cc[0].reshape(SSM_NG, SSM_N, SSM_GC)
    d_ci = d_cc[1].reshape(SSM_NG, SSM_N, SSM_GC)
    d_lam_row = d_lam[0]
    d_ar, d_ai, d_ld, d_br, d_bi = s5_param_bwd(
        colv(a_re), colv(a_im), colv(ldt_full), b_re2, b_im2,
        d_lam_row[:NST].reshape(NST, 1), d_lam_row[NST:].reshape(NST, 1), d_bb[0], d_bb[1])

    grads = {
        "g_pre": d_g_pre,
        "w_in": d_w_in,
        "w_pool": _block_diag_extract(d_wp_blk, 4, POOL_GW, POOL_GW).reshape(1, 4, POOL_GW, POOL_GW),
        "pool_scale": d_pool_scale,
        "a_re": d_ar.reshape(1, SSM_NG, SSM_N),
        "a_im": d_ai.reshape(1, SSM_NG, SSM_N),
        "log_dt": d_ld.reshape(1, SSM_NG),
        "b_re": d_br.reshape(1, SSM_NG, SSM_N, SSM_GC),
        "b_im": d_bi.reshape(1, SSM_NG, SSM_N, SSM_GC),
        "c_re": d_c.transpose(0, 2, 1).reshape(1, SSM_NG, SSM_GC, SSM_N),
        "c_im": (-d_ci).transpose(0, 2, 1).reshape(1, SSM_NG, SSM_GC, SSM_N),
        "d_skip": d_d_skip,
        "w_glu": d_w_glu,
        "g_mem": d_g_mem,
        "w_kv": d_w_kv,
        "w_out": d_w_out,
        "g_post": d_g_post,
    }
    return loss, grad_x.reshape(nb, l, D_MODEL), grads


VMEM_SPEC = pl.BlockSpec(memory_space=pltpu.VMEM)
ANY_SPEC = pl.BlockSpec(memory_space=pl.ANY)


def _place():
    return lax.axis_index("x"), lax.axis_index("y"), lax.axis_index("c")


def _other_chips(x, y):
    return [(1 - x, y), (x, 1 - y), (1 - x, 1 - y)]


def gather_weights(shards):
    n = len(shards)

    def body(*refs):
        ins, outs = refs[:n], refs[n:2 * n]
        send_sems, recv_sems = refs[2 * n:]
        x, y, c = _place()
        k = 2 * x + y
        chips = _other_chips(x, y)
        for i in range(n):
            outs[i][k] = ins[i][...].astype(BF16)

        def copy(i, block, core_half, to, sem):
            hr = shards[i].shape[0] // 2
            rows = outs[i].at[block, pl.ds(pl.multiple_of(core_half * hr, 2 * SUBLANES), hr)]
            return pltpu.make_async_remote_copy(
                src_ref=rows, dst_ref=rows, send_sem=send_sems.at[sem], recv_sem=recv_sems.at[sem],
                device_id=to, device_id_type=MESH)

        pairs = [(i, j) for i in range(n) for j in range(3)]
        first = [copy(i, k, c, (*chips[j], c), i * 3 + j) for i, j in pairs]
        for cp in first:
            cp.start()
        passed = []
        for i, j in pairs:
            px, py = chips[j]
            copy(i, 2 * px + py, c, (px, py, c), i * 3 + j).wait_recv()
            fw = copy(i, 2 * px + py, c, (x, y, 1 - c), 3 * n + i * 3 + j)
            fw.start()
            passed.append(fw)
        for i, j in pairs:
            px, py = chips[j]
            copy(i, 2 * px + py, 1 - c, (x, y, 1 - c), 3 * n + i * 3 + j).wait_recv()
        for cp in first + passed:
            cp.wait_send()

    return pl.pallas_call(
        body,
        out_shape=[jax.ShapeDtypeStruct((N_CHIPS,) + s.shape, BF16) for s in shards],
        in_specs=[VMEM_SPEC] * n, out_specs=[VMEM_SPEC] * n,
        scratch_shapes=[pltpu.SemaphoreType.DMA((6 * n,)), pltpu.SemaphoreType.DMA((6 * n,))],
        compiler_params=pltpu.CompilerParams(vmem_limit_bytes=VMEM_LIMIT),
        name="gather_weights")(*shards)


SEMS_PER_ITEM = 5


def reduce_all(g4s, packed):
    n = len(g4s)
    dims = [g.shape[1:] for g in g4s]
    pr = packed.shape[0]

    def body(*refs):
        g_refs, p_ref = refs[:n], refs[n]
        outs, op = refs[n + 1:2 * n + 1], refs[2 * n + 1]
        scr = refs[2 * n + 2:]
        mine, theirs, sendb, recvb = scr[0:n], scr[n:2 * n], scr[2 * n:3 * n], scr[3 * n:4 * n]
        p_theirs, gat, lsems, send_sems, recv_sems = scr[4 * n:]
        x, y, c = _place()
        k = 2 * x + y
        chips = _other_chips(x, y)
        sib = (x, y, 1 - c)

        def halves(rows):
            hr = rows // 2
            return (pl.ds(pl.multiple_of(c * hr, SUBLANES), hr), pl.ds(pl.multiple_of((1 - c) * hr, SUBLANES), hr))

        def remote(src, dst, sem, to):
            return pltpu.make_async_remote_copy(
                src_ref=src, dst_ref=dst, send_sem=send_sems.at[sem], recv_sem=recv_sems.at[sem],
                device_id=to, device_id_type=MESH)

        loads, started = [], []
        for i in range(n):
            my_rows, sib_rows = halves(dims[i][0])
            ld = pltpu.make_async_copy(g_refs[i].at[:, my_rows, :], mine[i], lsems.at[i])
            ld.start()
            loads.append(ld)
            sw = remote(g_refs[i].at[:, sib_rows, :], theirs[i], SEMS_PER_ITEM * i, sib)
            sw.start()
            started.append(sw)
        p_my, p_sib = halves(pr)
        p_swap = remote(p_ref.at[p_sib, :], p_theirs, SEMS_PER_ITEM * n, sib)
        p_swap.start()
        started.append(p_swap)

        p_swap.wait_recv()
        gat[k] = p_ref[p_my, :] + p_theirs[...]
        for j, (px, py) in enumerate(chips):
            cp = remote(gat.at[k], gat.at[k], SEMS_PER_ITEM * n + 1 + j, (px, py, c))
            cp.start()
            started.append(cp)
        ici = []
        for i in range(n):
            loads[i].wait()
            started[i].wait_recv()
            for j, (px, py) in enumerate(chips):
                kp = 2 * px + py
                sendb[i][j] = (mine[i][kp] + theirs[i][kp]).astype(BF16)
                cp = remote(sendb[i].at[j], recvb[i].at[j], SEMS_PER_ITEM * i + 1 + j, (px, py, c))
                cp.start()
                ici.append(cp)

        for j, (px, py) in enumerate(chips):
            kp = 2 * px + py
            remote(gat.at[kp], gat.at[kp], SEMS_PER_ITEM * n + 1 + j, (px, py, c)).wait_recv()
        op[p_my, :] = ((gat[0] + gat[1]) + gat[2]) + gat[3]
        last = remote(op.at[p_my, :], op.at[p_my, :], SEMS_PER_ITEM * n + 4, sib)
        last.start()
        started.append(last)
        for i in range(n):
            my_rows, _ = halves(dims[i][0])
            half = mine[i][k] + theirs[i][k]
            for j in range(3):
                ici[3 * i + j].wait_recv()
                half = half + recvb[i][j].astype(F32)
            outs[i][my_rows, :] = half
            last = remote(outs[i].at[my_rows, :], outs[i].at[my_rows, :], SEMS_PER_ITEM * i + 4, sib)
            last.start()
            started.append(last)
        remote(op.at[p_sib, :], op.at[p_sib, :], SEMS_PER_ITEM * n + 4, sib).wait_recv()
        for i in range(n):
            _, sib_rows = halves(dims[i][0])
            remote(outs[i].at[sib_rows, :], outs[i].at[sib_rows, :], SEMS_PER_ITEM * i + 4, sib).wait_recv()
        for cp in started + ici:
            cp.wait_send()

    n_sems = SEMS_PER_ITEM * (n + 1)
    scratch = ([pltpu.VMEM((N_CHIPS, r // 2, cd), F32) for r, cd in dims] * 2
               + [pltpu.VMEM((3, r // 2, cd), BF16) for r, cd in dims] * 2
               + [pltpu.VMEM((pr // 2, LANES), F32), pltpu.VMEM((N_CHIPS, pr // 2, LANES), F32),
                  pltpu.SemaphoreType.DMA((n,)), pltpu.SemaphoreType.DMA((n_sems,)), pltpu.SemaphoreType.DMA((n_sems,))])
    res = pl.pallas_call(
        body,
        out_shape=[jax.ShapeDtypeStruct(d, F32) for d in dims] + [jax.ShapeDtypeStruct(packed.shape, F32)],
        in_specs=[ANY_SPEC] * n + [VMEM_SPEC], out_specs=[VMEM_SPEC] * (n + 1),
        scratch_shapes=scratch,
        compiler_params=pltpu.CompilerParams(vmem_limit_bytes=VMEM_LIMIT),
        name="reduce_all")(*g4s, packed)
    return res[:n], res[n]


def adamw_all(ws, gs, ms, vs):
    n = len(ws)

    def body(*refs):
        w, g, m, v = refs[:n], refs[n:2 * n], refs[2 * n:3 * n], refs[3 * n:4 * n]
        od, om, ov = refs[4 * n:5 * n], refs[5 * n:6 * n], refs[6 * n:]
        for i in range(n):
            od[i][...], om[i][...], ov[i][...] = _adamw(w[i][...], g[i][...], m[i][...], v[i][...])

    shapes = [jax.ShapeDtypeStruct(a.shape, F32) for a in ws]
    res = pl.pallas_call(
        body, out_shape=shapes * 3, in_specs=[VMEM_SPEC] * (4 * n), out_specs=[VMEM_SPEC] * (3 * n),
        compiler_params=pltpu.CompilerParams(vmem_limit_bytes=VMEM_LIMIT),
        name="adamw_all")(*ws, *gs, *ms, *vs)
    return res[:n], res[n:2 * n], res[2 * n:]


WEIGHTS = ("g_pre", "w_in", "w_pool", "pool_scale", "a_re", "a_im", "log_dt", "b_re", "b_im", "c_re", "c_im", "d_skip",
           "w_glu", "g_mem", "w_kv", "w_out", "g_post")
SHARDED = ("w_in", "w_glu", "w_kv", "w_out")
REPLICATED = tuple(n for n in WEIGHTS if n not in SHARDED)
PACK_TILE = SUBLANES * LANES


def _pack(arrays):
    parts = []
    for a in arrays:
        flat = a.reshape(-1)
        parts.append(jnp.pad(flat, (0, -flat.shape[0] % PACK_TILE)).reshape(-1, LANES))
    rows = sum(p.shape[0] for p in parts)
    if rows % (2 * SUBLANES):
        parts.append(jnp.zeros((SUBLANES, LANES), F32))
    return jnp.concatenate(parts, axis=0)


def _unpack(packed, shapes):
    out, row = [], 0
    for shp in shapes:
        size = math.prod(shp)
        rows = -(-size // PACK_TILE) * SUBLANES
        out.append(packed[row:row + rows].reshape(-1)[:size].reshape(shp))
        row += rows
    return out


def _chip_major(a, n_cols):
    return a.reshape(a.shape[0], N_CHIPS, n_cols).transpose(1, 0, 2)


def _from_chip_major(a):
    return a.transpose(1, 0, 2).reshape(a.shape[1], N_CHIPS * a.shape[2])


def kernel(x, mem, g_pre, w_in, w_pool, pool_scale, a_re, a_im, log_dt, b_re, b_im, c_re, c_im, d_skip, w_glu, g_mem, w_kv, w_out, g_post, loss_target, m_g_pre, m_w_in, m_w_pool, m_pool_scale, m_a_re, m_a_im, m_log_dt, m_b_re, m_b_im, m_c_re, m_c_im, m_d_skip, m_w_glu, m_g_mem, m_w_kv, m_w_out, m_g_post, v_g_pre, v_w_in, v_w_pool, v_pool_scale, v_a_re, v_a_im, v_log_dt, v_b_re, v_b_im, v_c_re, v_c_im, v_d_skip, v_w_glu, v_g_mem, v_w_kv, v_w_out, v_g_post):
    given = dict(locals())
    wts = {n: given[n] for n in WEIGHTS}
    mom = {n: given["m_" + n] for n in WEIGHTS}
    var = {n: given["v_" + n] for n in WEIGHTS}

    g_in, g_glu, g_kv, g_out = gather_weights([wts[n][0] for n in SHARDED])
    w_in_full = g_in
    w_glu_full = _from_chip_major(g_glu)
    w_kv_full = g_kv.reshape(D_MODEL, 2 * ATT_W)
    w_out_full = g_out.reshape(D_MODEL, D_MODEL)

    loss_part, grad_x, grads = local_step(
        x, mem, loss_target, g_pre, w_in_full, w_pool[0], pool_scale, a_re[0], a_im[0], log_dt[0], b_re[0], b_im[0],
        c_re[0], c_im[0], d_skip, w_glu_full, g_mem, w_kv_full, w_out_full, g_post)

    partial4 = [
        grads["w_in"],
        _chip_major(grads["w_glu"], 2 * SSM_W // N_CHIPS),
        grads["w_kv"].reshape(N_CHIPS, D_MODEL // N_CHIPS, 2 * ATT_W),
        grads["w_out"].reshape(N_CHIPS, D_MODEL // N_CHIPS, D_MODEL),
    ]
    loss_tile = jnp.broadcast_to(loss_part, (SUBLANES, LANES))
    sharded_g, packed_g = reduce_all(partial4, _pack([grads[n] for n in REPLICATED] + [loss_tile]))
    small_g = _unpack(packed_g, [wts[n].shape for n in REPLICATED] + [(SUBLANES, LANES)])
    loss = small_g[-1][0, 0]
    grad = dict(zip(SHARDED, [g[None] for g in sharded_g]))
    grad.update(zip(REPLICATED, small_g[:-1]))

    deltas, new_ms, new_vs = adamw_all([wts[n] for n in WEIGHTS], [grad[n] for n in WEIGHTS],
                                       [mom[n] for n in WEIGHTS], [var[n] for n in WEIGHTS])
    return (loss, grad_x, *[grad[n] for n in WEIGHTS], *deltas, *new_ms, *new_vs)
```

```python
import functools
import math

import jax
import jax.numpy as jnp
from jax import lax
from jax.experimental import pallas as pl
from jax.experimental.pallas import tpu as pltpu

F32 = jnp.float32
BF16 = jnp.bfloat16

D_MODEL = 1024
POOL_W = 384
SSM_W = 384
ATT_W = 256
POOL_GW = 96
POOL_WINDOWS = (2, 4, 8, 16)
POOL_PAD = 16
SSM_NG = 24
SSM_N = 64
SSM_GC = 16
N_CHIPS = 4
N_DEV = 8
W_IN_SHARD = 2 * D_MODEL // N_CHIPS
NST = SSM_NG * SSM_N
BLK_CH = 128
BLK_GROUPS = BLK_CH // SSM_GC
BLK_ST = BLK_GROUPS * SSM_N
N_BLK = SSM_W // BLK_CH
N_MEM = 256
MEM_HEADS = 4
MEM_HD = 64
EPS = 1e-6

ADAM_LR = 0.001
ADAM_B1 = 0.9
ADAM_B2 = 0.999
ADAM_EPS = 1e-08
ADAM_WD = 0.01
ADAM_STEP = 10

SUBLANES = 8
LANES = 128
V7X_VMEM_BYTES = 64 * 2**20
VMEM_LIMIT = V7X_VMEM_BYTES - 8 * 2**20
SCAN_COLS = 512
ROW_TILE = 256
MESH = pl.DeviceIdType.MESH

NT = (((1,), (1,)), ((), ()))
TN = (((0,), (0,)), ((), ()))


def _params(*sem):
    return pltpu.CompilerParams(dimension_semantics=sem, vmem_limit_bytes=VMEM_LIMIT)


def _dot(a, b):
    return jnp.dot(a, b, preferred_element_type=F32)


def _dot_nt(a, b):
    return lax.dot_general(a, b, NT, preferred_element_type=F32)


def _dot_tn(a, b):
    return lax.dot_general(a, b, TN, preferred_element_type=F32)


def _rms_scale(v):
    return lax.rsqrt(jnp.mean(v * v, axis=-1, keepdims=True) + EPS)


def _adamw(w, g, m, v):
    m = ADAM_B1 * m + (1.0 - ADAM_B1) * g
    v = ADAM_B2 * v + (1.0 - ADAM_B2) * (g * g)
    m_hat = m / (1.0 - ADAM_B1 ** ADAM_STEP)
    v_hat = v / (1.0 - ADAM_B2 ** ADAM_STEP)
    delta = -ADAM_LR * (m_hat / (jnp.sqrt(v_hat) + ADAM_EPS) + ADAM_WD * w)
    return delta, m, v


def in_proj(x2, g_pre, w_in, later_shards):
    t = x2.shape[0]
    tm = min(512, t)
    n_steps = t // tm
    n = len(later_shards)

    def body(x_ref, g_ref, w_ref, *rest):
        shards, o_ref, gathered = rest[:n], rest[n], rest[n + 1:2 * n + 1]
        local_sems, send_sems, recv_sems = rest[2 * n + 1:]
        i = pl.program_id(0)
        cx, cy, cc = _place()
        k = 2 * cx + cy
        chips = _other_chips(cx, cy)

        def own(j):
            return pltpu.make_async_copy(shards[j], gathered[j].at[k], local_sems.at[j])

        def remote(j, p, block):
            px, py = chips[p]
            return pltpu.make_async_remote_copy(
                src_ref=shards[j], dst_ref=gathered[j].at[block], send_sem=send_sems.at[3 * j + p],
                recv_sem=recv_sems.at[3 * j + p], device_id=(px, py, cc), device_id_type=MESH)

        @pl.when(i == 0)
        def _():
            for j in range(n):
                own(j).start()
                for p in range(3):
                    remote(j, p, k).start()

        x = x_ref[...]
        h = (x * _rms_scale(x) * g_ref[...]).astype(BF16)
        for kk in range(N_CHIPS):
            o_ref[:, pl.ds(kk * W_IN_SHARD, W_IN_SHARD)] = _dot(h, w_ref[kk]).astype(BF16)

        @pl.when(i == n_steps - 1)
        def _():
            for j in range(n):
                own(j).wait()
                for p in range(3):
                    px, py = chips[p]
                    remote(j, p, 2 * px + py).wait_recv()
            for j in range(n):
                for p in range(3):
                    remote(j, p, k).wait_send()

    res = pl.pallas_call(
        body, grid=(n_steps,),
        in_specs=[pl.BlockSpec((tm, D_MODEL), lambda i: (i, 0)),
                  pl.BlockSpec((1, D_MODEL), lambda i: (0, 0)),
                  pl.BlockSpec((N_CHIPS, D_MODEL, W_IN_SHARD), lambda i: (0, 0, 0))] + [ANY_SPEC] * n,
        out_specs=[pl.BlockSpec((tm, 2 * D_MODEL), lambda i: (i, 0))] + [ANY_SPEC] * n,
        out_shape=([jax.ShapeDtypeStruct((t, 2 * D_MODEL), BF16)]
                   + [jax.ShapeDtypeStruct((N_CHIPS,) + s.shape, BF16) for s in later_shards]),
        scratch_shapes=[pltpu.SemaphoreType.DMA((n,)), pltpu.SemaphoreType.DMA((3 * n,)),
                        pltpu.SemaphoreType.DMA((3 * n,))],
        compiler_params=_params("arbitrary"), name="in_proj")(x2, g_pre, w_in, *later_shards)
    return res[0], res[1:]


def _pool_lane_window(shape):
    ch = lax.broadcasted_iota(jnp.int32, shape, 1)
    return jnp.where(ch < POOL_GW, 2.0, jnp.where(ch < 2 * POOL_GW, 4.0, jnp.where(ch < 3 * POOL_GW, 8.0, 16.0)))


def _pool_select(win, s2, s4, s8, s16):
    return jnp.where(win == 2.0, s2, jnp.where(win == 4.0, s4, jnp.where(win == 8.0, s8, s16)))


def _pool_div_count(v, win):
    pos = (lax.broadcasted_iota(jnp.int32, (POOL_PAD, POOL_W), 0) + 1).astype(F32)
    head = v[:POOL_PAD] / jnp.minimum(pos, win)
    return jnp.concatenate([head, v[POOL_PAD:] * (1.0 / win)], axis=0)


def _pool_diff(u, pad_ref, l):
    lo = POOL_PAD
    pad_ref[pl.ds(lo, l), :] = u
    s2 = u + pad_ref[pl.ds(lo - 1, l), :]
    pad_ref[pl.ds(lo, l), :] = s2
    s4 = s2 + pad_ref[pl.ds(lo - 2, l), :]
    pad_ref[pl.ds(lo, l), :] = s4
    s8 = s4 + pad_ref[pl.ds(lo - 4, l), :]
    pad_ref[pl.ds(lo, l), :] = s8
    s16 = s8 + pad_ref[pl.ds(lo - 8, l), :]
    win = _pool_lane_window((1, POOL_W))
    return _pool_div_count(_pool_select(win, s2, s4, s8, s16), win) - u, win


def pool_fwd(proj, w_blk, pool_scale, nb, l):
    def body(u_ref, w_ref, ps_ref, y_ref, pad_ref):
        pad_ref[pl.ds(0, POOL_PAD), :] = jnp.zeros((POOL_PAD, POOL_W), F32)
        d, _ = _pool_diff(u_ref[...].astype(F32), pad_ref, l)
        y_ref[...] = (_dot(d.astype(BF16), w_ref[...]) * ps_ref[...]).astype(BF16)

    return pl.pallas_call(
        body, grid=(nb,),
        in_specs=[pl.BlockSpec((l, POOL_W), lambda b: (b, 0)),
                  pl.BlockSpec((POOL_W, POOL_W), lambda b: (0, 0)),
                  pl.BlockSpec((1, POOL_W), lambda b: (0, 0))],
        out_specs=pl.BlockSpec((l, POOL_W), lambda b: (b, 0)),
        out_shape=jax.ShapeDtypeStruct((nb * l, POOL_W), BF16),
        scratch_shapes=[pltpu.VMEM((POOL_PAD + l, POOL_W), F32)],
        compiler_params=_params("arbitrary"), name="pool_fwd")(proj, w_blk, pool_scale)


def pool_bwd(proj, d_ycat, w_blk, pool_scale, nb, l):
    def body(u_ref, dy_ref, w_ref, ps_ref, du_ref, dw_ref, dps_ref, pad_ref, padb_ref):
        b = pl.program_id(0)

        @pl.when(b == 0)
        def _():
            dw_ref[...] = jnp.zeros_like(dw_ref)
            dps_ref[...] = jnp.zeros_like(dps_ref)

        pad_ref[pl.ds(0, POOL_PAD), :] = jnp.zeros((POOL_PAD, POOL_W), F32)
        padb_ref[pl.ds(l, POOL_PAD), :] = jnp.zeros((POOL_PAD, POOL_W), F32)
        d, win = _pool_diff(u_ref[...].astype(F32), pad_ref, l)
        db = d.astype(BF16)
        w = w_ref[...]
        dy = dy_ref[...].astype(F32)
        dps_ref[...] += jnp.sum(dy * _dot(db, w), axis=0, keepdims=True)
        dyo = (dy * ps_ref[...]).astype(BF16)
        dw_ref[...] += _dot_tn(db, dyo)
        dd = _dot_nt(dyo, w)
        e = _pool_div_count(dd, win)
        padb_ref[pl.ds(0, l), :] = e
        f2 = e + padb_ref[pl.ds(1, l), :]
        padb_ref[pl.ds(0, l), :] = f2
        f4 = f2 + padb_ref[pl.ds(2, l), :]
        padb_ref[pl.ds(0, l), :] = f4
        f8 = f4 + padb_ref[pl.ds(4, l), :]
        padb_ref[pl.ds(0, l), :] = f8
        f16 = f8 + padb_ref[pl.ds(8, l), :]
        du_ref[...] = (_pool_select(win, f2, f4, f8, f16) - dd).astype(BF16)

    return pl.pallas_call(
        body, grid=(nb,),
        in_specs=[pl.BlockSpec((l, POOL_W), lambda b: (b, 0)),
                  pl.BlockSpec((l, POOL_W), lambda b: (b, 0)),
                  pl.BlockSpec((POOL_W, POOL_W), lambda b: (0, 0)),
                  pl.BlockSpec((1, POOL_W), lambda b: (0, 0))],
        out_specs=[pl.BlockSpec((l, POOL_W), lambda b: (b, 0)),
                   pl.BlockSpec((POOL_W, POOL_W), lambda b: (0, 0)),
                   pl.BlockSpec((1, POOL_W), lambda b: (0, 0))],
        out_shape=[jax.ShapeDtypeStruct((nb * l, POOL_W), BF16),
                   jax.ShapeDtypeStruct((POOL_W, POOL_W), F32),
                   jax.ShapeDtypeStruct((1, POOL_W), F32)],
        scratch_shapes=[pltpu.VMEM((POOL_PAD + l, POOL_W), F32), pltpu.VMEM((POOL_PAD + l, POOL_W), F32)],
        compiler_params=_params("arbitrary"), name="pool_bwd")(proj, d_ycat, w_blk, pool_scale)


def _discretise(a_re, a_im, ldt, b_re, b_im):
    dt = jnp.exp(ldt)
    mag = jnp.exp(a_re * dt)
    th = a_im * dt
    lr = mag * jnp.cos(th)
    li = mag * jnp.sin(th)
    nr = lr - 1.0
    den = a_re * a_re + a_im * a_im
    fr = (nr * a_re + li * a_im) / den
    fi = (li * a_re - nr * a_im) / den
    return lr, li, fr * b_re - fi * b_im, fr * b_im + fi * b_re


def _cmul(ar, ai, br, bi):
    return ar * br - ai * bi, ar * bi + ai * br


def s5_scan_consts(a_re_row, a_im_row, ldt_row):
    def body(ar_ref, ai_ref, ld_ref, f_ref, b_ref):
        dt = jnp.exp(ld_ref[...])
        mag = jnp.exp(ar_ref[...] * dt)
        th = ai_ref[...] * dt
        lr = mag * jnp.cos(th)
        li = mag * jnp.sin(th)
        r = lax.broadcasted_iota(jnp.int32, (SUBLANES, NST), 0)
        for out_ref, sign, rev in ((f_ref, 1.0, False), (b_ref, -1.0, True)):
            p = [(lr, sign * li)]
            for _ in range(SUBLANES - 1):
                p.append(_cmul(p[-1][0], p[-1][1], lr, sign * li))
            for idx, s in enumerate((1, 2, 4)):
                inside = (r < SUBLANES - s) if rev else (r >= s)
                out_ref[pl.ds(2 * idx * SUBLANES, SUBLANES), :] = jnp.where(inside, p[s - 1][0], 0.0)
                out_ref[pl.ds((2 * idx + 1) * SUBLANES, SUBLANES), :] = jnp.where(inside, p[s - 1][1], 0.0)
            pr = jnp.zeros((SUBLANES, NST), F32)
            pi = jnp.zeros((SUBLANES, NST), F32)
            for row in range(SUBLANES):
                power = (SUBLANES - row) if rev else (row + 1)
                pr = jnp.where(r == row, p[power - 1][0], pr)
                pi = jnp.where(r == row, p[power - 1][1], pi)
            out_ref[pl.ds(6 * SUBLANES, SUBLANES), :] = pr
            out_ref[pl.ds(7 * SUBLANES, SUBLANES), :] = pi

    shape = jax.ShapeDtypeStruct((8 * SUBLANES, NST), F32)
    return pl.pallas_call(body, out_shape=[shape, shape], name="s5_scan_consts")(a_re_row, a_im_row, ldt_row)


def s5_bbar(a_re_c, a_im_c, ldt_c, b_re2, b_im2):
    def body(ar, ai, ld, br, bi, o_re, o_im):
        _, _, bbr, bbi = _discretise(ar[...], ai[...], ld[...], br[...], bi[...])
        o_re[...] = bbr
        o_im[...] = bbi

    shape = jax.ShapeDtypeStruct((NST, SSM_GC), F32)
    return pl.pallas_call(body, out_shape=[shape, shape], name="s5_bbar")(a_re_c, a_im_c, ldt_c, b_re2, b_im2)


def s5_param_bwd(a_re_c, a_im_c, ldt_c, b_re2, b_im2, d_lr, d_li, d_bbr, d_bbi):
    def body(ar, ai, ld, br, bi, g_lr, g_li, g_br, g_bi, o_ar, o_ai, o_ld, o_br, o_bi):
        _, vjp = jax.vjp(_discretise, ar[...], ai[...], ld[...], br[...], bi[...])
        d_ar, d_ai, d_ld, d_br, d_bi = vjp((g_lr[...], g_li[...], g_br[...], g_bi[...]))
        o_ar[...] = d_ar
        o_ai[...] = d_ai
        o_ld[...] = jnp.sum(d_ld.reshape(SSM_NG, SSM_N, 1), axis=1)
        o_br[...] = d_br
        o_bi[...] = d_bi

    col = jax.ShapeDtypeStruct((NST, 1), F32)
    mat = jax.ShapeDtypeStruct((NST, SSM_GC), F32)
    grp = jax.ShapeDtypeStruct((SSM_NG, 1), F32)
    return pl.pallas_call(body, out_shape=[col, col, grp, mat, mat], name="s5_param_bwd")(
        a_re_c, a_im_c, ldt_c, b_re2, b_im2, d_lr, d_li, d_bbr, d_bbi)


def _scan_tiles(buf_ref, cst_ref, carry_ref, rows, reverse, h_ref=None, acc_ref=None, reset=None, unrolled=False):
    n_tiles = rows // SUBLANES
    shifts = ((1, 0), (2, 2), (4, 4))
    row_id = lax.broadcasted_iota(jnp.int32, (SUBLANES, SCAN_COLS), 0)
    for j in range(NST // SCAN_COLS):
        c_re = pl.ds(j * SCAN_COLS, SCAN_COLS)
        c_im = pl.ds(NST + j * SCAN_COLS, SCAN_COLS)

        def step(i, carry, c_re=c_re, c_im=c_im):
            tile = (n_tiles - 1 - i) if reverse else i
            start = tile * SUBLANES
            rws = pl.ds(start if unrolled else pl.multiple_of(start, SUBLANES), SUBLANES)
            xr = buf_ref[rws, c_re]
            xi = buf_ref[rws, c_im]
            for s, base in shifts:
                amount = (SUBLANES - s) if reverse else s
                sr = pltpu.roll(xr, amount, 0)
                si = pltpu.roll(xi, amount, 0)
                mr = cst_ref[pl.ds(base * SUBLANES, SUBLANES), c_re]
                mi = cst_ref[pl.ds((base + 1) * SUBLANES, SUBLANES), c_re]
                xr, xi = xr + (mr * sr - mi * si), xi + (mr * si + mi * sr)
            pr = cst_ref[pl.ds(6 * SUBLANES, SUBLANES), c_re]
            pi = cst_ref[pl.ds(7 * SUBLANES, SUBLANES), c_re]
            cr, ci = carry[0], carry[1]
            xr, xi = xr + (pr * cr - pi * ci), xi + (pr * ci + pi * cr)
            buf_ref[rws, c_re] = xr
            buf_ref[rws, c_im] = xi
            edge = 0 if reverse else SUBLANES - 1
            ncr = jnp.broadcast_to(xr[edge:edge + 1, :], (SUBLANES, SCAN_COLS))
            nci = jnp.broadcast_to(xi[edge:edge + 1, :], (SUBLANES, SCAN_COLS))
            if not reverse:
                return ncr, nci
            gnr = jnp.where(row_id == SUBLANES - 1, cr, pltpu.roll(xr, SUBLANES - 1, 0))
            gni = jnp.where(row_id == SUBLANES - 1, ci, pltpu.roll(xi, SUBLANES - 1, 0))
            hr = h_ref[rws, c_re]
            hi = h_ref[rws, c_im]
            return ncr, nci, carry[2] + (gnr * hr + gni * hi), carry[3] + (gni * hr - gnr * hi)

        init = (carry_ref[:, c_re], carry_ref[:, c_im])
        if reset is not None:
            init = tuple(jnp.where(reset, 0.0, v) for v in init)
        if reverse:
            init = init + (acc_ref[:, c_re], acc_ref[:, c_im])
        if unrolled:
            out = init
            for i in range(n_tiles):
                out = step(i, out)
        else:
            out = lax.fori_loop(0, n_tiles, step, init)
        carry_ref[:, c_re] = out[0]
        carry_ref[:, c_im] = out[1]
        if reverse:
            acc_ref[:, c_re] = out[2]
            acc_ref[:, c_im] = out[3]


def _state_block(v, m):
    return jnp.concatenate([v[:, m * BLK_ST:(m + 1) * BLK_ST], v[:, NST + m * BLK_ST:NST + (m + 1) * BLK_ST]], axis=1)


def _put_state_block(buf_ref, m, val):
    buf_ref[:, pl.ds(m * BLK_ST, BLK_ST)] = val[:, :BLK_ST]
    buf_ref[:, pl.ds(NST + m * BLK_ST, BLK_ST)] = val[:, BLK_ST:]


def s5_fwd(proj, b_m, c_m, cst, d_skip, w_glu, nb, l):
    tt = min(256, l)
    nt = l // tt
    n_chunks = nb * nt

    def body(u_ref, b_ref, c_ref, cst_ref, ds_ref, wg_ref, y_ref, h_ref, yp_ref, z_ref,
             buf0, buf1, ub0, ub1, carry_ref):
        s = pl.program_id(0)

        @pl.when(s == 0)
        def _():
            for r in (buf0, buf1, ub0, ub1, carry_ref):
                r[...] = jnp.zeros_like(r)

        def step(p_buf, p_u, q_buf):
            h = p_buf[...].astype(BF16)
            h_ref[...] = h
            ypre = (jnp.concatenate([_dot(_state_block(h, m), c_ref[m]) for m in range(N_BLK)], axis=1)
                    + ds_ref[...] * p_u[...])
            yp_ref[...] = ypre
            z = _dot(jax.nn.gelu(ypre).astype(BF16), wg_ref[...])
            z_ref[...] = z
            y_ref[...] = (z[:, :SSM_W] * jax.nn.sigmoid(z[:, SSM_W:])).astype(BF16)
            ub = u_ref[...]
            p_u[...] = ub.astype(F32)
            for m in range(N_BLK):
                _put_state_block(p_buf, m, _dot(ub[:, m * BLK_CH:(m + 1) * BLK_CH], b_ref[m]))
            _scan_tiles(q_buf, cst_ref, carry_ref, tt, reverse=False, reset=lax.rem(s + nt - 1, nt) == 0, unrolled=True)

        @pl.when(lax.rem(s, 2) == 0)
        def _():
            step(buf0, ub0, buf1)

        @pl.when(lax.rem(s, 2) == 1)
        def _():
            step(buf1, ub1, buf0)

    row_in = lambda s: (jnp.minimum(s, n_chunks - 1), 1)
    row_out = lambda s: (jnp.maximum(s - 2, 0), 0)
    const = lambda s: (0, 0)
    const3 = lambda s: (0, 0, 0)
    return pl.pallas_call(
        body, grid=(n_chunks + 2,),
        in_specs=[pl.BlockSpec((tt, SSM_W), row_in),
                  pl.BlockSpec((N_BLK, BLK_CH, 2 * BLK_ST), const3),
                  pl.BlockSpec((N_BLK, 2 * BLK_ST, BLK_CH), const3),
                  pl.BlockSpec((8 * SUBLANES, NST), const),
                  pl.BlockSpec((1, SSM_W), const),
                  pl.BlockSpec((SSM_W, 2 * SSM_W), const)],
        out_specs=[pl.BlockSpec((tt, SSM_W), row_out),
                   pl.BlockSpec((tt, 2 * NST), row_out),
                   pl.BlockSpec((tt, SSM_W), row_out),
                   pl.BlockSpec((tt, 2 * SSM_W), row_out)],
        out_shape=[jax.ShapeDtypeStruct((nb * l, SSM_W), BF16),
                   jax.ShapeDtypeStruct((nb * l, 2 * NST), BF16),
                   jax.ShapeDtypeStruct((nb * l, SSM_W), F32),
                   jax.ShapeDtypeStruct((nb * l, 2 * SSM_W), F32)],
        scratch_shapes=[pltpu.VMEM((tt, 2 * NST), F32), pltpu.VMEM((tt, 2 * NST), F32),
                        pltpu.VMEM((tt, SSM_W), F32), pltpu.VMEM((tt, SSM_W), F32),
                        pltpu.VMEM((SUBLANES, 2 * NST), F32)],
        compiler_params=_params("arbitrary"), name="s5_fwd")(proj, b_m, c_m, cst, d_skip, w_glu)


def s5_bwd(proj, d_ycat, h, ypre, z, b_m, c_m, cst_rev, d_skip, w_glu, nb, l):
    tt = min(256, l)
    nt = l // tt
    n_chunks = nb * nt

    def body(u_ref, dy_ref, h_ref, yp_ref, z_ref, b_ref, c_ref, cst_ref, ds_ref, wg_ref,
             du_ref, dwg_ref, dds_ref, dcc_ref, dbb_ref, dlam_ref,
             buf0, buf1, hb0, hb1, ub0, ub1, dyp0, dyp1, carry_ref, dc_ref, db_ref):
        s = pl.program_id(0)

        @pl.when(s == 0)
        def _():
            for r in (buf0, buf1, hb0, hb1, ub0, ub1, dyp0, dyp1, carry_ref, dc_ref, db_ref, dwg_ref, dds_ref, dlam_ref):
                r[...] = jnp.zeros_like(r)

        def step(p_buf, p_h, p_u, p_dyp, q_buf, q_h):
            g = p_buf[...].astype(BF16)
            ub_done = p_u[...].astype(BF16)
            du = []
            for m in range(N_BLK):
                g_m = _state_block(g, m)
                db_ref[m] += _dot_tn(g_m, ub_done[:, m * BLK_CH:(m + 1) * BLK_CH])
                du.append(_dot_nt(g_m, b_ref[m]))
            du_ref[...] = (jnp.concatenate(du, axis=1) + ds_ref[...] * p_dyp[...]).astype(BF16)
            u = u_ref[...].astype(F32)
            ypre = yp_ref[...]
            z = z_ref[...]
            z1 = z[:, :SSM_W]
            sg = jax.nn.sigmoid(z[:, SSM_W:])
            dy = dy_ref[...].astype(F32) * jnp.where(s < n_chunks, 1.0, 0.0)
            dz = jnp.concatenate([dy * sg, dy * z1 * sg * (1.0 - sg)], axis=1).astype(BF16)
            yg, gelu_vjp = jax.vjp(jax.nn.gelu, ypre)
            dwg_ref[...] += _dot_tn(yg.astype(BF16), dz)
            dypre = gelu_vjp(_dot_nt(dz, wg_ref[...]))[0]
            dds_ref[...] += jnp.sum(dypre * u, axis=0, keepdims=True)
            dyb = dypre.astype(BF16)
            hb = h_ref[...]
            p_h[...] = hb.astype(F32)
            p_u[...] = u
            p_dyp[...] = dypre
            for m in range(N_BLK):
                dy_m = dyb[:, m * BLK_CH:(m + 1) * BLK_CH]
                dc_ref[m] += _dot_tn(_state_block(hb, m), dy_m)
                _put_state_block(p_buf, m, _dot_nt(dy_m, c_ref[m]))
            _scan_tiles(q_buf, cst_ref, carry_ref, tt, reverse=True, h_ref=q_h, acc_ref=dlam_ref,
                        reset=lax.rem(s + nt - 1, nt) == 0, unrolled=True)

        @pl.when(lax.rem(s, 2) == 0)
        def _():
            step(buf0, hb0, ub0, dyp0, buf1, hb1)

        @pl.when(lax.rem(s, 2) == 1)
        def _():
            step(buf1, hb1, ub1, dyp1, buf0, hb0)

        @pl.when(s == n_chunks + 1)
        def _():
            dlam_ref[...] = jnp.broadcast_to(jnp.sum(dlam_ref[...], axis=0, keepdims=True), dlam_ref.shape)
            for acc_ref, out_ref in ((dc_ref, dcc_ref), (db_ref, dbb_ref)):
                for m in range(N_BLK):
                    for ri in range(2):
                        for gl in range(BLK_GROUPS):
                            out_ref[ri, pl.ds((m * BLK_GROUPS + gl) * SSM_N, SSM_N), :] = acc_ref[
                                m, pl.ds(ri * BLK_ST + gl * SSM_N, SSM_N), pl.ds(gl * SSM_GC, SSM_GC)]

    def chunk_rows(o):
        return lax.div(o, nt) * nt + (nt - 1 - lax.rem(o, nt))

    def rrow(col):
        return lambda s: (chunk_rows(jnp.minimum(s, n_chunks - 1)), col)

    const = lambda s: (0, 0)
    const3 = lambda s: (0, 0, 0)
    state_buf = pltpu.VMEM((tt, 2 * NST), F32)
    chan_buf = pltpu.VMEM((tt, SSM_W), F32)
    return pl.pallas_call(
        body, grid=(n_chunks + 2,),
        in_specs=[pl.BlockSpec((tt, SSM_W), rrow(1)),
                  pl.BlockSpec((tt, SSM_W), rrow(1)),
                  pl.BlockSpec((tt, 2 * NST), rrow(0)),
                  pl.BlockSpec((tt, SSM_W), rrow(0)),
                  pl.BlockSpec((tt, 2 * SSM_W), rrow(0)),
                  pl.BlockSpec((N_BLK, BLK_CH, 2 * BLK_ST), const3),
                  pl.BlockSpec((N_BLK, 2 * BLK_ST, BLK_CH), const3),
                  pl.BlockSpec((8 * SUBLANES, NST), const),
                  pl.BlockSpec((1, SSM_W), const),
                  pl.BlockSpec((SSM_W, 2 * SSM_W), const)],
        out_specs=[pl.BlockSpec((tt, SSM_W), lambda s: (chunk_rows(jnp.maximum(s - 2, 0)), 0)),
                   pl.BlockSpec((SSM_W, 2 * SSM_W), const),
                   pl.BlockSpec((1, SSM_W), const),
                   pl.BlockSpec((2, NST, SSM_GC), const3),
                   pl.BlockSpec((2, NST, SSM_GC), const3),
                   pl.BlockSpec((SUBLANES, 2 * NST), const)],
        out_shape=[jax.ShapeDtypeStruct((nb * l, SSM_W), BF16),
                   jax.ShapeDtypeStruct((SSM_W, 2 * SSM_W), F32),
                   jax.ShapeDtypeStruct((1, SSM_W), F32),
                   jax.ShapeDtypeStruct((2, NST, SSM_GC), F32),
                   jax.ShapeDtypeStruct((2, NST, SSM_GC), F32),
                   jax.ShapeDtypeStruct((SUBLANES, 2 * NST), F32)],
        scratch_shapes=[state_buf, state_buf, state_buf, state_buf, chan_buf, chan_buf, chan_buf, chan_buf,
                        pltpu.VMEM((SUBLANES, 2 * NST), F32),
                        pltpu.VMEM((N_BLK, 2 * BLK_ST, BLK_CH), F32), pltpu.VMEM((N_BLK, 2 * BLK_ST, BLK_CH), F32)],
        compiler_params=_params("arbitrary"), name="s5_bwd")(
            proj, d_ycat, h, ypre, z, b_m, c_m, cst_rev, d_skip, w_glu)


def _head_mask(hh):
    lane = lax.broadcasted_iota(jnp.int32, (1, ATT_W), 1)
    return (lane >= hh * MEM_HD) & (lane < (hh + 1) * MEM_HD)


def _mem_kv(mem_ref, gm_ref, wkv_ref):
    m = mem_ref[0]
    mh = m * _rms_scale(m)
    mb = (mh * gm_ref[...]).astype(BF16)
    kv = _dot(mb, wkv_ref[...])
    return mh, mb, kv[:, :ATT_W].astype(BF16), kv[:, ATT_W:].astype(BF16)


def _stack_heads(dst_ref, a):
    for hh in range(MEM_HEADS):
        dst_ref[pl.ds(hh * N_MEM, N_MEM), :] = jnp.where(_head_mask(hh), a, jnp.zeros_like(a))


def _fold_heads(a):
    out = jnp.zeros((N_MEM, ATT_W), a.dtype)
    for hh in range(MEM_HEADS):
        out = out + jnp.where(_head_mask(hh), a[hh * N_MEM:(hh + 1) * N_MEM], 0.0)
    return out


def _softmax_heads(s):
    parts = []
    for hh in range(MEM_HEADS):
        sh = s[:, hh * N_MEM:(hh + 1) * N_MEM]
        e = jnp.exp(sh - jnp.max(sh, axis=-1, keepdims=True))
        parts.append(e / jnp.sum(e, axis=-1, keepdims=True))
    return jnp.concatenate(parts, axis=1)


def attn_fwd(proj, mem, g_mem, w_kv, nb, l):
    tq = min(512, l)
    nq = l // tq
    scale = MEM_HD ** -0.5

    def body(q_ref, mem_ref, gm_ref, wkv_ref, o_ref, k_s, v_s):
        @pl.when(pl.program_id(1) == 0)
        def _():
            _, _, k, v = _mem_kv(mem_ref, gm_ref, wkv_ref)
            _stack_heads(k_s, k)
            _stack_heads(v_s, v)

        p = _softmax_heads(_dot_nt(q_ref[...], k_s[...]) * scale)
        o_ref[...] = _dot(p.astype(BF16), v_s[...]).astype(BF16)

    const = lambda b, t: (0, 0)
    stacked = pltpu.VMEM((MEM_HEADS * N_MEM, ATT_W), BF16)
    return pl.pallas_call(
        body, grid=(nb, nq),
        in_specs=[pl.BlockSpec((tq, ATT_W), lambda b, t: (b * nq + t, 3)),
                  pl.BlockSpec((1, N_MEM, D_MODEL), lambda b, t: (b, 0, 0)),
                  pl.BlockSpec((1, D_MODEL), const),
                  pl.BlockSpec((D_MODEL, 2 * ATT_W), const)],
        out_specs=pl.BlockSpec((tq, ATT_W), lambda b, t: (b * nq + t, 0)),
        out_shape=jax.ShapeDtypeStruct((nb * l, ATT_W), BF16),
        scratch_shapes=[stacked, stacked],
        compiler_params=_params("arbitrary", "arbitrary"), name="attn_fwd")(proj, mem, g_mem, w_kv)


def attn_bwd(proj, d_ycat, mem, g_mem, w_kv, nb, l):
    tq = min(512, l)
    nq = l // tq
    scale = MEM_HD ** -0.5

    def body(q_ref, do_ref, mem_ref, gm_ref, wkv_ref, dq_ref, dwkv_ref, dgm_ref, k_s, v_s, mb_s, dk_s, dv_s):
        b = pl.program_id(0)
        t = pl.program_id(1)

        @pl.when((b == 0) & (t == 0))
        def _():
            dwkv_ref[...] = jnp.zeros_like(dwkv_ref)
            dgm_ref[...] = jnp.zeros_like(dgm_ref)

        @pl.when(t == 0)
        def _():
            _, mb, k, v = _mem_kv(mem_ref, gm_ref, wkv_ref)
            _stack_heads(k_s, k)
            _stack_heads(v_s, v)
            mb_s[...] = mb
            dk_s[...] = jnp.zeros_like(dk_s)
            dv_s[...] = jnp.zeros_like(dv_s)

        q = q_ref[...]
        do = do_ref[...]
        k = k_s[...]
        p = _softmax_heads(_dot_nt(q, k) * scale)
        dp = _dot_nt(do, v_s[...])
        ds = []
        for hh in range(MEM_HEADS):
            blk = slice(hh * N_MEM, (hh + 1) * N_MEM)
            ds.append(p[:, blk] * (dp[:, blk] - jnp.sum(dp[:, blk] * p[:, blk], axis=-1, keepdims=True)) * scale)
        ds = jnp.concatenate(ds, axis=1).astype(BF16)
        dq_ref[...] = _dot(ds, k).astype(BF16)
        dk_s[...] += _dot_tn(ds, q)
        dv_s[...] += _dot_tn(p.astype(BF16), do)

        @pl.when(t == nq - 1)
        def _():
            dkv = jnp.concatenate([_fold_heads(dk_s[...]), _fold_heads(dv_s[...])], axis=1).astype(BF16)
            dwkv_ref[...] += _dot_tn(mb_s[...], dkv)
            m = mem_ref[0]
            dgm_ref[...] += jnp.sum(_dot_nt(dkv, wkv_ref[...]) * (m * _rms_scale(m)), axis=0, keepdims=True)

    const = lambda b, t: (0, 0)
    return pl.pallas_call(
        body, grid=(nb, nq),
        in_specs=[pl.BlockSpec((tq, ATT_W), lambda b, t: (b * nq + t, 3)),
                  pl.BlockSpec((tq, ATT_W), lambda b, t: (b * nq + t, 3)),
                  pl.BlockSpec((1, N_MEM, D_MODEL), lambda b, t: (b, 0, 0)),
                  pl.BlockSpec((1, D_MODEL), const),
                  pl.BlockSpec((D_MODEL, 2 * ATT_W), const)],
        out_specs=[pl.BlockSpec((tq, ATT_W), lambda b, t: (b * nq + t, 0)),
                   pl.BlockSpec((D_MODEL, 2 * ATT_W), const),
                   pl.BlockSpec((1, D_MODEL), const)],
        out_shape=[jax.ShapeDtypeStruct((nb * l, ATT_W), BF16),
                   jax.ShapeDtypeStruct((D_MODEL, 2 * ATT_W), F32),
                   jax.ShapeDtypeStruct((1, D_MODEL), F32)],
        scratch_shapes=[pltpu.VMEM((MEM_HEADS * N_MEM, ATT_W), BF16), pltpu.VMEM((MEM_HEADS * N_MEM, ATT_W), BF16),
                        pltpu.VMEM((N_MEM, D_MODEL), BF16),
                        pltpu.VMEM((MEM_HEADS * N_MEM, ATT_W), F32), pltpu.VMEM((MEM_HEADS * N_MEM, ATT_W), F32)],
        compiler_params=_params("arbitrary", "arbitrary"), name="attn_bwd")(proj, d_ycat, mem, g_mem, w_kv)


def tail(y_pool, y_ssm, y_att, proj, x2, target, w_out, g_post):
    t = x2.shape[0]
    tm = min(ROW_TILE, t)

    def body(yp_ref, ys_ref, ya_ref, gate_ref, x_ref, tg_ref, w_ref, gp_ref,
             dz_ref, dyc_ref, dgate_ref, dw_ref, dgp_ref, loss_ref):
        @pl.when(pl.program_id(0) == 0)
        def _():
            dw_ref[...] = jnp.zeros_like(dw_ref)
            dgp_ref[...] = jnp.zeros_like(dgp_ref)
            loss_ref[...] = jnp.zeros_like(loss_ref)

        ycat = jnp.concatenate([yp_ref[...], ys_ref[...], ya_ref[...]], axis=1).astype(F32)
        gate = gate_ref[...].astype(F32)
        sg = jax.nn.sigmoid(gate)
        silu = gate * sg
        yb = (ycat * silu).astype(BF16)
        w = w_ref[...]
        out = _dot(yb, w)
        r2 = _rms_scale(out)
        oh = out * r2
        gp = gp_ref[...]
        err = (x_ref[...] + oh * gp) - tg_ref[...]
        loss_ref[...] += 0.5 * jnp.sum(jnp.mean(err * err, axis=-1, keepdims=True), axis=0, keepdims=True)
        dz = err * (1.0 / D_MODEL)
        dz_ref[...] = dz.astype(BF16)
        dgp_ref[...] += jnp.sum(dz * oh, axis=0, keepdims=True)
        dn = dz * gp
        dout = (r2 * (dn - oh * jnp.mean(dn * oh, axis=-1, keepdims=True))).astype(BF16)
        dw_ref[...] += _dot_tn(yb, dout)
        dy = _dot_nt(dout, w)
        dyc_ref[...] = (dy * silu).astype(BF16)
        dgate_ref[...] = (dy * ycat * (sg * (1.0 + gate * (1.0 - sg)))).astype(BF16)

    row = lambda i: (i, 0)
    const = lambda i: (0, 0)
    full = jax.ShapeDtypeStruct((t, D_MODEL), BF16)
    return pl.pallas_call(
        body, grid=(t // tm,),
        in_specs=[pl.BlockSpec((tm, POOL_W), row),
                  pl.BlockSpec((tm, SSM_W), row),
                  pl.BlockSpec((tm, ATT_W), row),
                  pl.BlockSpec((tm, D_MODEL), lambda i: (i, 1)),
                  pl.BlockSpec((tm, D_MODEL), row),
                  pl.BlockSpec((tm, D_MODEL), row),
                  pl.BlockSpec((D_MODEL, D_MODEL), const),
                  pl.BlockSpec((1, D_MODEL), const)],
        out_specs=[pl.BlockSpec((tm, D_MODEL), row),
                   pl.BlockSpec((tm, D_MODEL), row),
                   pl.BlockSpec((tm, D_MODEL), row),
                   pl.BlockSpec((D_MODEL, D_MODEL), const),
                   pl.BlockSpec((1, D_MODEL), const),
                   pl.BlockSpec((1, LANES), const)],
        out_shape=[full, full, full,
                   jax.ShapeDtypeStruct((D_MODEL, D_MODEL), F32),
                   jax.ShapeDtypeStruct((1, D_MODEL), F32),
                   jax.ShapeDtypeStruct((1, LANES), F32)],
        compiler_params=_params("arbitrary"), name="tail")(y_pool, y_ssm, y_att, proj, x2, target, w_out, g_post)


def in_proj_bwd(du_pool, du_ssm, dq, d_gate, x2, dz, g_pre, w_in):
    t = x2.shape[0]
    tm = min(ROW_TILE, t)
    shard = W_IN_SHARD

    def body(dup_ref, dus_ref, dq_ref, dg_ref, x_ref, dz_ref, g_ref, w_ref, gx_ref, dw_ref, dgp_ref):
        @pl.when(pl.program_id(0) == 0)
        def _():
            dw_ref[...] = jnp.zeros_like(dw_ref)
            dgp_ref[...] = jnp.zeros_like(dgp_ref)

        dproj = jnp.concatenate([dup_ref[...], dus_ref[...], dq_ref[...], dg_ref[...]], axis=1)
        x = x_ref[...]
        r1 = _rms_scale(x)
        xh = x * r1
        g = g_ref[...]
        hb = (xh * g).astype(BF16)
        dh = jnp.zeros((tm, D_MODEL), F32)
        for k in range(N_CHIPS):
            dp_k = dproj[:, k * shard:(k + 1) * shard]
            dw_ref[k] += _dot_tn(hb, dp_k)
            dh = dh + _dot_nt(dp_k, w_ref[k])
        dgp_ref[...] += jnp.sum(dh * xh, axis=0, keepdims=True)
        dn = dh * g
        gx_ref[...] = dz_ref[...].astype(F32) + r1 * (dn - xh * jnp.mean(dn * xh, axis=-1, keepdims=True))

    row = lambda i: (i, 0)
    const = lambda i: (0, 0)
    return pl.pallas_call(
        body, grid=(t // tm,),
        in_specs=[pl.BlockSpec((tm, POOL_W), row),
                  pl.BlockSpec((tm, SSM_W), row),
                  pl.BlockSpec((tm, ATT_W), row),
                  pl.BlockSpec((tm, D_MODEL), row),
                  pl.BlockSpec((tm, D_MODEL), row),
                  pl.BlockSpec((tm, D_MODEL), row),
                  pl.BlockSpec((1, D_MODEL), const),
                  pl.BlockSpec((N_CHIPS, D_MODEL, shard), lambda i: (0, 0, 0))],
        out_specs=[pl.BlockSpec((tm, D_MODEL), row),
                   pl.BlockSpec((N_CHIPS, D_MODEL, shard), lambda i: (0, 0, 0)),
                   pl.BlockSpec((1, D_MODEL), const)],
        out_shape=[jax.ShapeDtypeStruct((t, D_MODEL), F32),
                   jax.ShapeDtypeStruct((N_CHIPS, D_MODEL, shard), F32),
                   jax.ShapeDtypeStruct((1, D_MODEL), F32)],
        compiler_params=_params("arbitrary"), name="in_proj_bwd")(du_pool, du_ssm, dq, d_gate, x2, dz, g_pre, w_in)


def _block_diag(blocks):
    g, r, c = blocks.shape
    eye = jnp.eye(g, dtype=blocks.dtype)
    return jnp.einsum("grc,gh->grhc", blocks, eye).reshape(g * r, g * c)


def _block_diag_extract(mat, g, r, c):
    eye = jnp.eye(g, dtype=mat.dtype)
    return jnp.einsum("grhc,gh->grc", mat.reshape(g, r, g, c), eye)


def local_step(x, mem, target, g_pre, w_in, w_pool, pool_scale, a_re, a_im, log_dt, b_re, b_im, c_re, c_im, d_skip,
               later_shards, g_mem, g_post):
    nb, l, _ = x.shape
    x2 = x.reshape(nb * l, D_MODEL)
    tg2 = target.reshape(nb * l, D_MODEL)

    rowv = lambda a: a.reshape(1, NST)
    colv = lambda a: a.reshape(NST, 1)
    ldt_full = jnp.broadcast_to(log_dt.reshape(SSM_NG, 1), (SSM_NG, SSM_N))
    b_re2 = b_re.reshape(NST, SSM_GC)
    b_im2 = b_im.reshape(NST, SSM_GC)
    cst_f, cst_b = s5_scan_consts(rowv(a_re), rowv(a_im), rowv(ldt_full))
    bbr, bbi = s5_bbar(colv(a_re), colv(a_im), colv(ldt_full), b_re2, b_im2)
    eye = jnp.eye(BLK_GROUPS, dtype=F32)
    blk_in = lambda bb: jnp.einsum("mgnc,gh->mhcgn", bb.reshape(N_BLK, BLK_GROUPS, SSM_N, SSM_GC), eye).reshape(
        N_BLK, BLK_CH, BLK_ST)
    blk_out = lambda cc: jnp.einsum("mgcn,gh->mgnhc", cc.reshape(N_BLK, BLK_GROUPS, SSM_GC, SSM_N), eye).reshape(
        N_BLK, BLK_ST, BLK_CH)
    b_m = jnp.concatenate([blk_in(bbr), blk_in(bbi)], axis=2).astype(BF16)
    c_m = jnp.concatenate([blk_out(c_re), -blk_out(c_im)], axis=1).astype(BF16)
    w_pool_blk = _block_diag(w_pool.reshape(4, POOL_GW, POOL_GW)).astype(BF16)

    proj, (g_glu, g_kv, g_out) = in_proj(x2, g_pre, w_in, later_shards)
    w_glu = _from_chip_major(g_glu)
    w_kv = g_kv.reshape(D_MODEL, 2 * ATT_W)
    w_out = g_out.reshape(D_MODEL, D_MODEL)
    y_pool = pool_fwd(proj, w_pool_blk, pool_scale, nb, l)
    y_ssm, h, ypre, z = s5_fwd(proj, b_m, c_m, cst_f, d_skip, w_glu, nb, l)
    y_att = attn_fwd(proj, mem, g_mem, w_kv, nb, l)
    dz, d_ycat, d_gate, d_w_out, d_g_post, loss = tail(y_pool, y_ssm, y_att, proj, x2, tg2, w_out, g_post)
    du_pool, d_wp_blk, d_pool_scale = pool_bwd(proj, d_ycat, w_pool_blk, pool_scale, nb, l)
    du_ssm, d_w_glu, d_d_skip, d_cc, d_bb, d_lam = s5_bwd(
        proj, d_ycat, h, ypre, z, b_m, c_m, cst_b, d_skip, w_glu, nb, l)
    dq, d_w_kv, d_g_mem = attn_bwd(proj, d_ycat, mem, g_mem, w_kv, nb, l)
    grad_x, d_w_in, d_g_pre = in_proj_bwd(du_pool, du_ssm, dq, d_gate, x2, dz, g_pre, w_in)

    d_c = d_cc[0].reshape(SSM_NG, SSM_N, SSM_GC)
    d_ci = d_cc[1].reshape(SSM_NG, SSM_N, SSM_GC)
    d_lam_row = d_lam[0]
    d_ar, d_ai, d_ld, d_br, d_bi = s5_param_bwd(
        colv(a_re), colv(a_im), colv(ldt_full), b_re2, b_im2,
        d_lam_row[:NST].reshape(NST, 1), d_lam_row[NST:].reshape(NST, 1), d_bb[0], d_bb[1])

    grads = {
        "g_pre": d_g_pre,
        "w_in": d_w_in,
        "w_pool": _block_diag_extract(d_wp_blk, 4, POOL_GW, POOL_GW).reshape(1, 4, POOL_GW, POOL_GW),
        "pool_scale": d_pool_scale,
        "a_re": d_ar.reshape(1, SSM_NG, SSM_N),
        "a_im": d_ai.reshape(1, SSM_NG, SSM_N),
        "log_dt": d_ld.reshape(1, SSM_NG),
        "b_re": d_br.reshape(1, SSM_NG, SSM_N, SSM_GC),
        "b_im": d_bi.reshape(1, SSM_NG, SSM_N, SSM_GC),
        "c_re": d_c.transpose(0, 2, 1).reshape(1, SSM_NG, SSM_GC, SSM_N),
        "c_im": (-d_ci).transpose(0, 2, 1).reshape(1, SSM_NG, SSM_GC, SSM_N),
        "d_skip": d_d_skip,
        "w_glu": d_w_glu,
        "g_mem": d_g_mem,
        "w_kv": d_w_kv,
        "w_out": d_w_out,
        "g_post": d_g_post,
    }
    return loss, grad_x.reshape(nb, l, D_MODEL), grads


VMEM_SPEC = pl.BlockSpec(memory_space=pltpu.VMEM)
ANY_SPEC = pl.BlockSpec(memory_space=pl.ANY)


def _place():
    return lax.axis_index("x"), lax.axis_index("y"), lax.axis_index("c")


def _other_chips(x, y):
    return [(1 - x, y), (x, 1 - y), (1 - x, 1 - y)]


def gather_weights(shards, casts):
    n = len(shards)
    nc = len(casts)

    def body(*refs):
        ins, cast_ins = refs[:n], refs[n:n + nc]
        outs, cast_outs = refs[n + nc:2 * n + nc], refs[2 * n + nc:2 * (n + nc)]
        send_sems, recv_sems = refs[2 * (n + nc):]
        x, y, c = _place()
        k = 2 * x + y
        chips = _other_chips(x, y)
        for i in range(n):
            outs[i][k] = ins[i][...].astype(BF16)
        for i in range(nc):
            cast_outs[i][...] = cast_ins[i][...].astype(BF16)

        def copy(i, block, core_half, to, sem):
            hr = shards[i].shape[0] // 2
            rows = outs[i].at[block, pl.ds(pl.multiple_of(core_half * hr, 2 * SUBLANES), hr)]
            return pltpu.make_async_remote_copy(
                src_ref=rows, dst_ref=rows, send_sem=send_sems.at[sem], recv_sem=recv_sems.at[sem],
                device_id=to, device_id_type=MESH)

        pairs = [(i, j) for i in range(n) for j in range(3)]
        first = [copy(i, k, c, (*chips[j], c), i * 3 + j) for i, j in pairs]
        for cp in first:
            cp.start()
        passed = []
        for i, j in pairs:
            px, py = chips[j]
            copy(i, 2 * px + py, c, (px, py, c), i * 3 + j).wait_recv()
            fw = copy(i, 2 * px + py, c, (x, y, 1 - c), 3 * n + i * 3 + j)
            fw.start()
            passed.append(fw)
        for i, j in pairs:
            px, py = chips[j]
            copy(i, 2 * px + py, 1 - c, (x, y, 1 - c), 3 * n + i * 3 + j).wait_recv()
        for cp in first + passed:
            cp.wait_send()

    res = pl.pallas_call(
        body,
        out_shape=([jax.ShapeDtypeStruct((N_CHIPS,) + s.shape, BF16) for s in shards]
                   + [jax.ShapeDtypeStruct(s.shape, BF16) for s in casts]),
        in_specs=[VMEM_SPEC] * (n + nc), out_specs=[VMEM_SPEC] * (n + nc),
        scratch_shapes=[pltpu.SemaphoreType.DMA((6 * n,)), pltpu.SemaphoreType.DMA((6 * n,))],
        compiler_params=pltpu.CompilerParams(vmem_limit_bytes=VMEM_LIMIT),
        name="gather_weights")(*shards, *casts)
    return res[:n], res[n:]


SEMS_PER_ITEM = 5


def reduce_all(g4s, packed):
    n = len(g4s)
    dims = [g.shape[1:] for g in g4s]
    pr = packed.shape[0]

    def body(*refs):
        g_refs, p_ref = refs[:n], refs[n]
        outs, op = refs[n + 1:2 * n + 1], refs[2 * n + 1]
        scr = refs[2 * n + 2:]
        mine, theirs, sendb, recvb = scr[0:n], scr[n:2 * n], scr[2 * n:3 * n], scr[3 * n:4 * n]
        p_theirs, gat, lsems, send_sems, recv_sems = scr[4 * n:]
        x, y, c = _place()
        k = 2 * x + y
        chips = _other_chips(x, y)
        sib = (x, y, 1 - c)

        def halves(rows):
            hr = rows // 2
            return (pl.ds(pl.multiple_of(c * hr, SUBLANES), hr), pl.ds(pl.multiple_of((1 - c) * hr, SUBLANES), hr))

        def remote(src, dst, sem, to):
            return pltpu.make_async_remote_copy(
                src_ref=src, dst_ref=dst, send_sem=send_sems.at[sem], recv_sem=recv_sems.at[sem],
                device_id=to, device_id_type=MESH)

        loads, started = [], []
        for i in range(n):
            my_rows, sib_rows = halves(dims[i][0])
            ld = pltpu.make_async_copy(g_refs[i].at[:, my_rows, :], mine[i], lsems.at[i])
            ld.start()
            loads.append(ld)
            sw = remote(g_refs[i].at[:, sib_rows, :], theirs[i], SEMS_PER_ITEM * i, sib)
            sw.start()
            started.append(sw)
        p_my, p_sib = halves(pr)
        p_swap = remote(p_ref.at[p_sib, :], p_theirs, SEMS_PER_ITEM * n, sib)
        p_swap.start()
        started.append(p_swap)

        p_swap.wait_recv()
        gat[k] = p_ref[p_my, :] + p_theirs[...]
        for j, (px, py) in enumerate(chips):
            cp = remote(gat.at[k], gat.at[k], SEMS_PER_ITEM * n + 1 + j, (px, py, c))
            cp.start()
            started.append(cp)
        ici = []
        for i in range(n):
            loads[i].wait()
            started[i].wait_recv()
            for j, (px, py) in enumerate(chips):
                kp = 2 * px + py
                sendb[i][j] = (mine[i][kp] + theirs[i][kp]).astype(BF16)
                cp = remote(sendb[i].at[j], recvb[i].at[j], SEMS_PER_ITEM * i + 1 + j, (px, py, c))
                cp.start()
                ici.append(cp)

        for j, (px, py) in enumerate(chips):
            kp = 2 * px + py
            remote(gat.at[kp], gat.at[kp], SEMS_PER_ITEM * n + 1 + j, (px, py, c)).wait_recv()
        op[p_my, :] = ((gat[0] + gat[1]) + gat[2]) + gat[3]
        last = remote(op.at[p_my, :], op.at[p_my, :], SEMS_PER_ITEM * n + 4, sib)
        last.start()
        started.append(last)
        for i in range(n):
            my_rows, _ = halves(dims[i][0])
            half = mine[i][k] + theirs[i][k]
            for j in range(3):
                ici[3 * i + j].wait_recv()
                half = half + recvb[i][j].astype(F32)
            outs[i][my_rows, :] = half
            last = remote(outs[i].at[my_rows, :], outs[i].at[my_rows, :], SEMS_PER_ITEM * i + 4, sib)
            last.start()
            started.append(last)
        remote(op.at[p_sib, :], op.at[p_sib, :], SEMS_PER_ITEM * n + 4, sib).wait_recv()
        for i in range(n):
            _, sib_rows = halves(dims[i][0])
            remote(outs[i].at[sib_rows, :], outs[i].at[sib_rows, :], SEMS_PER_ITEM * i + 4, sib).wait_recv()
        for cp in started + ici:
            cp.wait_send()

    n_sems = SEMS_PER_ITEM * (n + 1)
    scratch = ([pltpu.VMEM((N_CHIPS, r // 2, cd), F32) for r, cd in dims] * 2
               + [pltpu.VMEM((3, r // 2, cd), BF16) for r, cd in dims] * 2
               + [pltpu.VMEM((pr // 2, LANES), F32), pltpu.VMEM((N_CHIPS, pr // 2, LANES), F32),
                  pltpu.SemaphoreType.DMA((n,)), pltpu.SemaphoreType.DMA((n_sems,)), pltpu.SemaphoreType.DMA((n_sems,))])
    res = pl.pallas_call(
        body,
        out_shape=[jax.ShapeDtypeStruct(d, F32) for d in dims] + [jax.ShapeDtypeStruct(packed.shape, F32)],
        in_specs=[ANY_SPEC] * n + [VMEM_SPEC], out_specs=[VMEM_SPEC] * (n + 1),
        scratch_shapes=scratch,
        compiler_params=pltpu.CompilerParams(vmem_limit_bytes=VMEM_LIMIT),
        name="reduce_all")(*g4s, packed)
    return res[:n], res[n]


def adamw_all(ws, gs, ms, vs):
    n = len(ws)

    def body(*refs):
        w, g, m, v = refs[:n], refs[n:2 * n], refs[2 * n:3 * n], refs[3 * n:4 * n]
        od, om, ov = refs[4 * n:5 * n], refs[5 * n:6 * n], refs[6 * n:]
        for i in range(n):
            od[i][...], om[i][...], ov[i][...] = _adamw(w[i][...], g[i][...], m[i][...], v[i][...])

    shapes = [jax.ShapeDtypeStruct(a.shape, F32) for a in ws]
    res = pl.pallas_call(
        body, out_shape=shapes * 3, in_specs=[VMEM_SPEC] * (4 * n), out_specs=[VMEM_SPEC] * (3 * n),
        compiler_params=pltpu.CompilerParams(vmem_limit_bytes=VMEM_LIMIT),
        name="adamw_all")(*ws, *gs, *ms, *vs)
    return res[:n], res[n:2 * n], res[2 * n:]


WEIGHTS = ("g_pre", "w_in", "w_pool", "pool_scale", "a_re", "a_im", "log_dt", "b_re", "b_im", "c_re", "c_im", "d_skip",
           "w_glu", "g_mem", "w_kv", "w_out", "g_post")
SHARDED = ("w_in", "w_glu", "w_kv", "w_out")
REPLICATED = tuple(n for n in WEIGHTS if n not in SHARDED)
PACK_TILE = SUBLANES * LANES


def _pack(arrays):
    parts = []
    for a in arrays:
        flat = a.reshape(-1)
        parts.append(jnp.pad(flat, (0, -flat.shape[0] % PACK_TILE)).reshape(-1, LANES))
    rows = sum(p.shape[0] for p in parts)
    if rows % (2 * SUBLANES):
        parts.append(jnp.zeros((SUBLANES, LANES), F32))
    return jnp.concatenate(parts, axis=0)


def _unpack(packed, shapes):
    out, row = [], 0
    for shp in shapes:
        size = math.prod(shp)
        rows = -(-size // PACK_TILE) * SUBLANES
        out.append(packed[row:row + rows].reshape(-1)[:size].reshape(shp))
        row += rows
    return out


def _chip_major(a, n_cols):
    return a.reshape(a.shape[0], N_CHIPS, n_cols).transpose(1, 0, 2)


def _from_chip_major(a):
    return a.transpose(1, 0, 2).reshape(a.shape[1], N_CHIPS * a.shape[2])


def kernel(x, mem, g_pre, w_in, w_pool, pool_scale, a_re, a_im, log_dt, b_re, b_im, c_re, c_im, d_skip, w_glu, g_mem, w_kv, w_out, g_post, loss_target, m_g_pre, m_w_in, m_w_pool, m_pool_scale, m_a_re, m_a_im, m_log_dt, m_b_re, m_b_im, m_c_re, m_c_im, m_d_skip, m_w_glu, m_g_mem, m_w_kv, m_w_out, m_g_post, v_g_pre, v_w_in, v_w_pool, v_pool_scale, v_a_re, v_a_im, v_log_dt, v_b_re, v_b_im, v_c_re, v_c_im, v_d_skip, v_w_glu, v_g_mem, v_w_kv, v_w_out, v_g_post):
    given = dict(locals())
    wts = {n: given[n] for n in WEIGHTS}
    mom = {n: given["m_" + n] for n in WEIGHTS}
    var = {n: given["v_" + n] for n in WEIGHTS}

    (g_in,), later_shards = gather_weights([w_in[0]], [w_glu[0], w_kv[0], w_out[0]])
    loss_part, grad_x, grads = local_step(
        x, mem, loss_target, g_pre, g_in, w_pool[0], pool_scale, a_re[0], a_im[0], log_dt[0], b_re[0], b_im[0],
        c_re[0], c_im[0], d_skip, later_shards, g_mem, g_post)

    partial4 = [
        grads["w_in"],
        _chip_major(grads["w_glu"], 2 * SSM_W // N_CHIPS),
        grads["w_kv"].reshape(N_CHIPS, D_MODEL // N_CHIPS, 2 * ATT_W),
        grads["w_out"].reshape(N_CHIPS, D_MODEL // N_CHIPS, D_MODEL),
    ]
    loss_tile = jnp.broadcast_to(loss_part, (SUBLANES, LANES))
    sharded_g, packed_g = reduce_all(partial4, _pack([grads[n] for n in REPLICATED] + [loss_tile]))
    small_g = _unpack(packed_g, [wts[n].shape for n in REPLICATED] + [(SUBLANES, LANES)])
    loss = small_g[-1][0, 0]
    grad = dict(zip(SHARDED, [g[None] for g in sharded_g]))
    grad.update(zip(REPLICATED, small_g[:-1]))

    deltas, new_ms, new_vs = adamw_all([wts[n] for n in WEIGHTS], [grad[n] for n in WEIGHTS],
                                       [mom[n] for n in WEIGHTS], [var[n] for n in WEIGHTS])
    return (loss, grad_x, *[grad[n] for n in WEIGHTS], *deltas, *new_ms, *new_vs)
```

```python
import functools
import math

import jax
import jax.numpy as jnp
from jax import lax
from jax.experimental import pallas as pl
from jax.experimental.pallas import tpu as pltpu

F32 = jnp.float32
BF16 = jnp.bfloat16

D_MODEL = 1024
POOL_W = 384
SSM_W = 384
ATT_W = 256
POOL_GW = 96
POOL_WINDOWS = (2, 4, 8, 16)
POOL_PAD = 16
SSM_NG = 24
SSM_N = 64
SSM_GC = 16
N_CHIPS = 4
N_DEV = 8
W_IN_SHARD = 2 * D_MODEL // N_CHIPS
NST = SSM_NG * SSM_N
BLK_CH = 128
BLK_GROUPS = BLK_CH // SSM_GC
BLK_ST = BLK_GROUPS * SSM_N
N_BLK = SSM_W // BLK_CH
N_MEM = 256
MEM_HEADS = 4
MEM_HD = 64
EPS = 1e-6

ADAM_LR = 0.001
ADAM_B1 = 0.9
ADAM_B2 = 0.999
ADAM_EPS = 1e-08
ADAM_WD = 0.01
ADAM_STEP = 10

SUBLANES = 8
LANES = 128
V7X_VMEM_BYTES = 64 * 2**20
VMEM_LIMIT = V7X_VMEM_BYTES - 8 * 2**20
SCAN_COLS = 512
ROW_TILE = 256
MESH = pl.DeviceIdType.MESH

NT = (((1,), (1,)), ((), ()))
TN = (((0,), (0,)), ((), ()))


def _params(*sem):
    return pltpu.CompilerParams(dimension_semantics=sem, vmem_limit_bytes=VMEM_LIMIT)


def _dot(a, b):
    return jnp.dot(a, b, preferred_element_type=F32)


def _dot_nt(a, b):
    return lax.dot_general(a, b, NT, preferred_element_type=F32)


def _dot_tn(a, b):
    return lax.dot_general(a, b, TN, preferred_element_type=F32)


def _rms_scale(v):
    return lax.rsqrt(jnp.mean(v * v, axis=-1, keepdims=True) + EPS)


def _adamw(w, g, m, v):
    m = ADAM_B1 * m + (1.0 - ADAM_B1) * g
    v = ADAM_B2 * v + (1.0 - ADAM_B2) * (g * g)
    m_hat = m / (1.0 - ADAM_B1 ** ADAM_STEP)
    v_hat = v / (1.0 - ADAM_B2 ** ADAM_STEP)
    delta = -ADAM_LR * (m_hat / (jnp.sqrt(v_hat) + ADAM_EPS) + ADAM_WD * w)
    return delta, m, v


def in_proj(x2, g_pre, w_in, later_shards):
    t = x2.shape[0]
    tm = min(512, t)
    n_steps = t // tm
    n = len(later_shards)

    def body(x_ref, g_ref, w_ref, *rest):
        shards, o_ref, gathered = rest[:n], rest[n], rest[n + 1:2 * n + 1]
        local_sems, send_sems, recv_sems = rest[2 * n + 1:]
        i = pl.program_id(0)
        cx, cy, cc = _place()
        k = 2 * cx + cy
        chips = _other_chips(cx, cy)

        def own(j):
            return pltpu.make_async_copy(shards[j], gathered[j].at[k], local_sems.at[j])

        def remote(j, p, block):
            px, py = chips[p]
            return pltpu.make_async_remote_copy(
                src_ref=shards[j], dst_ref=gathered[j].at[block], send_sem=send_sems.at[3 * j + p],
                recv_sem=recv_sems.at[3 * j + p], device_id=(px, py, cc), device_id_type=MESH)

        @pl.when(i == 0)
        def _():
            for j in range(n):
                own(j).start()
                for p in range(3):
                    remote(j, p, k).start()

        x = x_ref[...]
        h = (x * _rms_scale(x) * g_ref[...]).astype(BF16)
        for kk in range(N_CHIPS):
            o_ref[:, pl.ds(kk * W_IN_SHARD, W_IN_SHARD)] = _dot(h, w_ref[kk]).astype(BF16)

        @pl.when(i == n_steps - 1)
        def _():
            for j in range(n):
                own(j).wait()
                for p in range(3):
                    px, py = chips[p]
                    remote(j, p, 2 * px + py).wait_recv()
            for j in range(n):
                for p in range(3):
                    remote(j, p, k).wait_send()

    res = pl.pallas_call(
        body, grid=(n_steps,),
        in_specs=[pl.BlockSpec((tm, D_MODEL), lambda i: (i, 0)),
                  pl.BlockSpec((1, D_MODEL), lambda i: (0, 0)),
                  pl.BlockSpec((N_CHIPS, D_MODEL, W_IN_SHARD), lambda i: (0, 0, 0))] + [ANY_SPEC] * n,
        out_specs=[pl.BlockSpec((tm, 2 * D_MODEL), lambda i: (i, 0))] + [ANY_SPEC] * n,
        out_shape=([jax.ShapeDtypeStruct((t, 2 * D_MODEL), BF16)]
                   + [jax.ShapeDtypeStruct((N_CHIPS,) + s.shape, BF16) for s in later_shards]),
        scratch_shapes=[pltpu.SemaphoreType.DMA((n,)), pltpu.SemaphoreType.DMA((3 * n,)),
                        pltpu.SemaphoreType.DMA((3 * n,))],
        compiler_params=_params("arbitrary"), name="in_proj")(x2, g_pre, w_in, *later_shards)
    return res[0], res[1:]


def _pool_lane_window(shape):
    ch = lax.broadcasted_iota(jnp.int32, shape, 1)
    return jnp.where(ch < POOL_GW, 2.0, jnp.where(ch < 2 * POOL_GW, 4.0, jnp.where(ch < 3 * POOL_GW, 8.0, 16.0)))


def _pool_select(win, s2, s4, s8, s16):
    return jnp.where(win == 2.0, s2, jnp.where(win == 4.0, s4, jnp.where(win == 8.0, s8, s16)))


def _pool_div_count(v, win):
    pos = (lax.broadcasted_iota(jnp.int32, (POOL_PAD, POOL_W), 0) + 1).astype(F32)
    head = v[:POOL_PAD] / jnp.minimum(pos, win)
    return jnp.concatenate([head, v[POOL_PAD:] * (1.0 / win)], axis=0)


def _pool_diff(u, pad_ref, l):
    lo = POOL_PAD
    pad_ref[pl.ds(lo, l), :] = u
    s2 = u + pad_ref[pl.ds(lo - 1, l), :]
    pad_ref[pl.ds(lo, l), :] = s2
    s4 = s2 + pad_ref[pl.ds(lo - 2, l), :]
    pad_ref[pl.ds(lo, l), :] = s4
    s8 = s4 + pad_ref[pl.ds(lo - 4, l), :]
    pad_ref[pl.ds(lo, l), :] = s8
    s16 = s8 + pad_ref[pl.ds(lo - 8, l), :]
    win = _pool_lane_window((1, POOL_W))
    return _pool_div_count(_pool_select(win, s2, s4, s8, s16), win) - u, win


def pool_fwd(proj, w_blk, pool_scale, nb, l):
    def body(u_ref, w_ref, ps_ref, y_ref, pad_ref):
        pad_ref[pl.ds(0, POOL_PAD), :] = jnp.zeros((POOL_PAD, POOL_W), F32)
        d, _ = _pool_diff(u_ref[...].astype(F32), pad_ref, l)
        y_ref[...] = (_dot(d.astype(BF16), w_ref[...]) * ps_ref[...]).astype(BF16)

    return pl.pallas_call(
        body, grid=(nb,),
        in_specs=[pl.BlockSpec((l, POOL_W), lambda b: (b, 0)),
                  pl.BlockSpec((POOL_W, POOL_W), lambda b: (0, 0)),
                  pl.BlockSpec((1, POOL_W), lambda b: (0, 0))],
        out_specs=pl.BlockSpec((l, POOL_W), lambda b: (b, 0)),
        out_shape=jax.ShapeDtypeStruct((nb * l, POOL_W), BF16),
        scratch_shapes=[pltpu.VMEM((POOL_PAD + l, POOL_W), F32)],
        compiler_params=_params("arbitrary"), name="pool_fwd")(proj, w_blk, pool_scale)


def pool_bwd(proj, d_ycat, w_blk, pool_scale, nb, l):
    def body(u_ref, dy_ref, w_ref, ps_ref, du_ref, dw_ref, dps_ref, pad_ref, padb_ref):
        b = pl.program_id(0)

        @pl.when(b == 0)
        def _():
            dw_ref[...] = jnp.zeros_like(dw_ref)
            dps_ref[...] = jnp.zeros_like(dps_ref)

        pad_ref[pl.ds(0, POOL_PAD), :] = jnp.zeros((POOL_PAD, POOL_W), F32)
        padb_ref[pl.ds(l, POOL_PAD), :] = jnp.zeros((POOL_PAD, POOL_W), F32)
        d, win = _pool_diff(u_ref[...].astype(F32), pad_ref, l)
        db = d.astype(BF16)
        w = w_ref[...]
        dy = dy_ref[...].astype(F32)
        dps_ref[...] += jnp.sum(dy * _dot(db, w), axis=0, keepdims=True)
        dyo = (dy * ps_ref[...]).astype(BF16)
        dw_ref[...] += _dot_tn(db, dyo)
        dd = _dot_nt(dyo, w)
        e = _pool_div_count(dd, win)
        padb_ref[pl.ds(0, l), :] = e
        f2 = e + padb_ref[pl.ds(1, l), :]
        padb_ref[pl.ds(0, l), :] = f2
        f4 = f2 + padb_ref[pl.ds(2, l), :]
        padb_ref[pl.ds(0, l), :] = f4
        f8 = f4 + padb_ref[pl.ds(4, l), :]
        padb_ref[pl.ds(0, l), :] = f8
        f16 = f8 + padb_ref[pl.ds(8, l), :]
        du_ref[...] = (_pool_select(win, f2, f4, f8, f16) - dd).astype(BF16)

    return pl.pallas_call(
        body, grid=(nb,),
        in_specs=[pl.BlockSpec((l, POOL_W), lambda b: (b, 0)),
                  pl.BlockSpec((l, POOL_W), lambda b: (b, 0)),
                  pl.BlockSpec((POOL_W, POOL_W), lambda b: (0, 0)),
                  pl.BlockSpec((1, POOL_W), lambda b: (0, 0))],
        out_specs=[pl.BlockSpec((l, POOL_W), lambda b: (b, 0)),
                   pl.BlockSpec((POOL_W, POOL_W), lambda b: (0, 0)),
                   pl.BlockSpec((1, POOL_W), lambda b: (0, 0))],
        out_shape=[jax.ShapeDtypeStruct((nb * l, POOL_W), BF16),
                   jax.ShapeDtypeStruct((POOL_W, POOL_W), F32),
                   jax.ShapeDtypeStruct((1, POOL_W), F32)],
        scratch_shapes=[pltpu.VMEM((POOL_PAD + l, POOL_W), F32), pltpu.VMEM((POOL_PAD + l, POOL_W), F32)],
        compiler_params=_params("arbitrary"), name="pool_bwd")(proj, d_ycat, w_blk, pool_scale)


def _discretise(a_re, a_im, ldt, b_re, b_im):
    dt = jnp.exp(ldt)
    mag = jnp.exp(a_re * dt)
    th = a_im * dt
    lr = mag * jnp.cos(th)
    li = mag * jnp.sin(th)
    nr = lr - 1.0
    den = a_re * a_re + a_im * a_im
    fr = (nr * a_re + li * a_im) / den
    fi = (li * a_re - nr * a_im) / den
    return lr, li, fr * b_re - fi * b_im, fr * b_im + fi * b_re


def _cmul(ar, ai, br, bi):
    return ar * br - ai * bi, ar * bi + ai * br


def s5_scan_consts(a_re_row, a_im_row, ldt_row):
    def body(ar_ref, ai_ref, ld_ref, f_ref, b_ref):
        dt = jnp.exp(ld_ref[...])
        mag = jnp.exp(ar_ref[...] * dt)
        th = ai_ref[...] * dt
        lr = mag * jnp.cos(th)
        li = mag * jnp.sin(th)
        r = lax.broadcasted_iota(jnp.int32, (SUBLANES, NST), 0)
        for out_ref, sign, rev in ((f_ref, 1.0, False), (b_ref, -1.0, True)):
            p = [(lr, sign * li)]
            for _ in range(SUBLANES - 1):
                p.append(_cmul(p[-1][0], p[-1][1], lr, sign * li))
            for idx, s in enumerate((1, 2, 4)):
                inside = (r < SUBLANES - s) if rev else (r >= s)
                out_ref[pl.ds(2 * idx * SUBLANES, SUBLANES), :] = jnp.where(inside, p[s - 1][0], 0.0)
                out_ref[pl.ds((2 * idx + 1) * SUBLANES, SUBLANES), :] = jnp.where(inside, p[s - 1][1], 0.0)
            pr = jnp.zeros((SUBLANES, NST), F32)
            pi = jnp.zeros((SUBLANES, NST), F32)
            for row in range(SUBLANES):
                power = (SUBLANES - row) if rev else (row + 1)
                pr = jnp.where(r == row, p[power - 1][0], pr)
                pi = jnp.where(r == row, p[power - 1][1], pi)
            out_ref[pl.ds(6 * SUBLANES, SUBLANES), :] = pr
            out_ref[pl.ds(7 * SUBLANES, SUBLANES), :] = pi

    shape = jax.ShapeDtypeStruct((8 * SUBLANES, NST), F32)
    return pl.pallas_call(body, out_shape=[shape, shape], name="s5_scan_consts")(a_re_row, a_im_row, ldt_row)


def s5_bbar(a_re_row, a_im_row, ldt_row, b_re_t, b_im_t):
    def body(ar, ai, ld, br, bi, o_re, o_im):
        _, _, bbr, bbi = _discretise(ar[...], ai[...], ld[...], br[...], bi[...])
        o_re[...] = bbr
        o_im[...] = bbi

    shape = jax.ShapeDtypeStruct((SSM_GC, NST), F32)
    return pl.pallas_call(body, out_shape=[shape, shape], name="s5_bbar")(a_re_row, a_im_row, ldt_row, b_re_t, b_im_t)


def s5_param_bwd(a_re_row, a_im_row, ldt_row, b_re_t, b_im_t, d_lr, d_li, d_bbr, d_bbi):
    def body(ar, ai, ld, br, bi, g_lr, g_li, g_br, g_bi, o_ar, o_ai, o_ld, o_br, o_bi):
        _, vjp = jax.vjp(_discretise, ar[...], ai[...], ld[...], br[...], bi[...])
        d_ar, d_ai, d_ld, d_br, d_bi = vjp((g_lr[...], g_li[...], g_br[...], g_bi[...]))
        o_ar[...] = d_ar
        o_ai[...] = d_ai
        state = lax.broadcasted_iota(jnp.int32, (NST, LANES), 0)
        lane = lax.broadcasted_iota(jnp.int32, (NST, LANES), 1)
        in_group = jnp.where((state >= lane * SSM_N) & (state < (lane + 1) * SSM_N), 1.0, 0.0)
        o_ld[...] = jnp.dot(d_ld, in_group, precision=lax.Precision.HIGHEST, preferred_element_type=F32)
        o_br[...] = d_br
        o_bi[...] = d_bi

    row = jax.ShapeDtypeStruct((1, NST), F32)
    mat = jax.ShapeDtypeStruct((SSM_GC, NST), F32)
    grp = jax.ShapeDtypeStruct((1, LANES), F32)
    return pl.pallas_call(body, out_shape=[row, row, grp, mat, mat], name="s5_param_bwd")(
        a_re_row, a_im_row, ldt_row, b_re_t, b_im_t, d_lr, d_li, d_bbr, d_bbi)


def _scan_tiles(buf_ref, cst_ref, carry_ref, rows, reverse, reset, h_ref=None, acc_ref=None):
    n_tiles = rows // SUBLANES
    shifts = ((1, 0), (2, 2), (4, 4))
    row_id = lax.broadcasted_iota(jnp.int32, (SUBLANES, SCAN_COLS), 0)
    cols = [(j * SCAN_COLS, NST + j * SCAN_COLS) for j in range(NST // SCAN_COLS)]
    edge = 0 if reverse else SUBLANES - 1
    carry = jnp.where(reset, 0.0, carry_ref[...])
    accs = [(acc_ref[:, pl.ds(o_re, SCAN_COLS)], acc_ref[:, pl.ds(o_im, SCAN_COLS)]) for o_re, o_im in cols] if reverse else None
    for i in range(n_tiles):
        start = ((n_tiles - 1 - i) if reverse else i) * SUBLANES
        rws = pl.ds(start, SUBLANES)
        for j, (o_re, o_im) in enumerate(cols):
            c_re = pl.ds(o_re, SCAN_COLS)
            c_im = pl.ds(o_im, SCAN_COLS)
            xr = buf_ref[rws, c_re]
            xi = buf_ref[rws, c_im]
            for s, base in shifts:
                amount = (SUBLANES - s) if reverse else s
                sr = pltpu.roll(xr, amount, 0)
                si = pltpu.roll(xi, amount, 0)
                mr = cst_ref[pl.ds(base * SUBLANES, SUBLANES), c_re]
                mi = cst_ref[pl.ds((base + 1) * SUBLANES, SUBLANES), c_re]
                xr, xi = xr + (mr * sr - mi * si), xi + (mr * si + mi * sr)
            pr = cst_ref[pl.ds(6 * SUBLANES, SUBLANES), c_re]
            pi = cst_ref[pl.ds(7 * SUBLANES, SUBLANES), c_re]
            cr = carry[:, o_re:o_re + SCAN_COLS]
            ci = carry[:, o_im:o_im + SCAN_COLS]
            xr, xi = xr + (pr * cr - pi * ci), xi + (pr * ci + pi * cr)
            buf_ref[rws, c_re] = xr
            buf_ref[rws, c_im] = xi
            if reverse:
                gnr = jnp.where(row_id == SUBLANES - 1, cr, pltpu.roll(xr, SUBLANES - 1, 0))
                gni = jnp.where(row_id == SUBLANES - 1, ci, pltpu.roll(xi, SUBLANES - 1, 0))
                hr = h_ref[rws, c_re]
                hi = h_ref[rws, c_im]
                accs[j] = (accs[j][0] + (gnr * hr + gni * hi), accs[j][1] + (gni * hr - gnr * hi))
        carry = jnp.broadcast_to(buf_ref[pl.ds(start + edge, 1), :], (SUBLANES, 2 * NST))
    carry_ref[...] = carry
    if reverse:
        for (o_re, o_im), (a_re_, a_im_) in zip(cols, accs):
            acc_ref[:, pl.ds(o_re, SCAN_COLS)] = a_re_
            acc_ref[:, pl.ds(o_im, SCAN_COLS)] = a_im_


def _state_block(v, m):
    return jnp.concatenate([v[:, m * BLK_ST:(m + 1) * BLK_ST], v[:, NST + m * BLK_ST:NST + (m + 1) * BLK_ST]], axis=1)


def _put_state_block(buf_ref, m, val):
    buf_ref[:, pl.ds(m * BLK_ST, BLK_ST)] = val[:, :BLK_ST]
    buf_ref[:, pl.ds(NST + m * BLK_ST, BLK_ST)] = val[:, BLK_ST:]


def s5_fwd(proj, b_m, c_m, cst, d_skip, w_glu, nb, l):
    tt = min(256, l)
    nt = l // tt
    n_chunks = nb * nt

    def body(u_ref, b_ref, c_ref, cst_ref, ds_ref, wg_ref, y_ref, h_ref, yp_ref, z_ref,
             buf0, buf1, ub0, ub1, carry_ref):
        s = pl.program_id(0)

        @pl.when(s == 0)
        def _():
            for r in (buf0, buf1, ub0, ub1, carry_ref):
                r[...] = jnp.zeros_like(r)

        def step(p_buf, p_u, q_buf):
            h = p_buf[...].astype(BF16)
            h_ref[...] = h
            ypre = (jnp.concatenate([_dot(_state_block(h, m), c_ref[m]) for m in range(N_BLK)], axis=1)
                    + ds_ref[...] * p_u[...])
            yp_ref[...] = ypre
            z = _dot(jax.nn.gelu(ypre).astype(BF16), wg_ref[...])
            z_ref[...] = z
            y_ref[...] = (z[:, :SSM_W] * jax.nn.sigmoid(z[:, SSM_W:])).astype(BF16)
            ub = u_ref[...]
            p_u[...] = ub.astype(F32)
            for m in range(N_BLK):
                _put_state_block(p_buf, m, _dot(ub[:, m * BLK_CH:(m + 1) * BLK_CH], b_ref[m]))
            _scan_tiles(q_buf, cst_ref, carry_ref, tt, False, lax.rem(s + nt - 1, nt) == 0)

        @pl.when(lax.rem(s, 2) == 0)
        def _():
            step(buf0, ub0, buf1)

        @pl.when(lax.rem(s, 2) == 1)
        def _():
            step(buf1, ub1, buf0)

    row_in = lambda s: (jnp.minimum(s, n_chunks - 1), 1)
    row_out = lambda s: (jnp.maximum(s - 2, 0), 0)
    const = lambda s: (0, 0)
    const3 = lambda s: (0, 0, 0)
    return pl.pallas_call(
        body, grid=(n_chunks + 2,),
        in_specs=[pl.BlockSpec((tt, SSM_W), row_in),
                  pl.BlockSpec((N_BLK, BLK_CH, 2 * BLK_ST), const3),
                  pl.BlockSpec((N_BLK, 2 * BLK_ST, BLK_CH), const3),
                  pl.BlockSpec((8 * SUBLANES, NST), const),
                  pl.BlockSpec((1, SSM_W), const),
                  pl.BlockSpec((SSM_W, 2 * SSM_W), const)],
        out_specs=[pl.BlockSpec((tt, SSM_W), row_out),
                   pl.BlockSpec((tt, 2 * NST), row_out),
                   pl.BlockSpec((tt, SSM_W), row_out),
                   pl.BlockSpec((tt, 2 * SSM_W), row_out)],
        out_shape=[jax.ShapeDtypeStruct((nb * l, SSM_W), BF16),
                   jax.ShapeDtypeStruct((nb * l, 2 * NST), BF16),
                   jax.ShapeDtypeStruct((nb * l, SSM_W), F32),
                   jax.ShapeDtypeStruct((nb * l, 2 * SSM_W), F32)],
        scratch_shapes=[pltpu.VMEM((tt, 2 * NST), F32), pltpu.VMEM((tt, 2 * NST), F32),
                        pltpu.VMEM((tt, SSM_W), F32), pltpu.VMEM((tt, SSM_W), F32),
                        pltpu.VMEM((SUBLANES, 2 * NST), F32)],
        compiler_params=_params("arbitrary"), name="s5_fwd")(proj, b_m, c_m, cst, d_skip, w_glu)


def s5_bwd(proj, d_ycat, h, ypre, z, b_m, c_m, cst_rev, d_skip, w_glu, nb, l):
    tt = min(256, l)
    nt = l // tt
    n_chunks = nb * nt

    def body(u_ref, dy_ref, h_ref, yp_ref, z_ref, b_ref, c_ref, cst_ref, ds_ref, wg_ref,
             du_ref, dwg_ref, dds_ref, dcc_ref, dbb_ref, dlam_ref,
             buf0, buf1, hb0, hb1, ub0, ub1, dyp0, dyp1, carry_ref, dc_ref, db_ref):
        s = pl.program_id(0)

        @pl.when(s == 0)
        def _():
            for r in (buf0, buf1, hb0, hb1, ub0, ub1, dyp0, dyp1, carry_ref, dc_ref, db_ref, dwg_ref, dds_ref, dlam_ref):
                r[...] = jnp.zeros_like(r)

        def step(p_buf, p_h, p_u, p_dyp, q_buf, q_h):
            g = p_buf[...].astype(BF16)
            ub_done = p_u[...].astype(BF16)
            du = []
            for m in range(N_BLK):
                g_m = _state_block(g, m)
                db_ref[m] += _dot_tn(g_m, ub_done[:, m * BLK_CH:(m + 1) * BLK_CH])
                du.append(_dot_nt(g_m, b_ref[m]))
            du_ref[...] = (jnp.concatenate(du, axis=1) + ds_ref[...] * p_dyp[...]).astype(BF16)
            u = u_ref[...].astype(F32)
            ypre = yp_ref[...]
            z = z_ref[...]
            z1 = z[:, :SSM_W]
            sg = jax.nn.sigmoid(z[:, SSM_W:])
            dy = dy_ref[...].astype(F32) * jnp.where(s < n_chunks, 1.0, 0.0)
            dz = jnp.concatenate([dy * sg, dy * z1 * sg * (1.0 - sg)], axis=1).astype(BF16)
            yg, gelu_vjp = jax.vjp(jax.nn.gelu, ypre)
            dwg_ref[...] += _dot_tn(yg.astype(BF16), dz)
            dypre = gelu_vjp(_dot_nt(dz, wg_ref[...]))[0]
            dds_ref[...] += jnp.sum(dypre * u, axis=0, keepdims=True)
            dyb = dypre.astype(BF16)
            hb = h_ref[...]
            p_h[...] = hb.astype(F32)
            p_u[...] = u
            p_dyp[...] = dypre
            for m in range(N_BLK):
                dy_m = dyb[:, m * BLK_CH:(m + 1) * BLK_CH]
                dc_ref[m] += _dot_tn(_state_block(hb, m), dy_m)
                _put_state_block(p_buf, m, _dot_nt(dy_m, c_ref[m]))
            _scan_tiles(q_buf, cst_ref, carry_ref, tt, True, lax.rem(s + nt - 1, nt) == 0, h_ref=q_h, acc_ref=dlam_ref)

        @pl.when(lax.rem(s, 2) == 0)
        def _():
            step(buf0, hb0, ub0, dyp0, buf1, hb1)

        @pl.when(lax.rem(s, 2) == 1)
        def _():
            step(buf1, hb1, ub1, dyp1, buf0, hb0)

        @pl.when(s == n_chunks + 1)
        def _():
            dlam_ref[...] = jnp.broadcast_to(jnp.sum(dlam_ref[...], axis=0, keepdims=True), dlam_ref.shape)
            for acc_ref, out_ref in ((dc_ref, dcc_ref), (db_ref, dbb_ref)):
                for m in range(N_BLK):
                    for ri in range(2):
                        for gl in range(BLK_GROUPS):
                            out_ref[ri, pl.ds((m * BLK_GROUPS + gl) * SSM_N, SSM_N), :] = acc_ref[
                                m, pl.ds(ri * BLK_ST + gl * SSM_N, SSM_N), pl.ds(gl * SSM_GC, SSM_GC)]

    def chunk_rows(o):
        return lax.div(o, nt) * nt + (nt - 1 - lax.rem(o, nt))

    def rrow(col):
        return lambda s: (chunk_rows(jnp.minimum(s, n_chunks - 1)), col)

    const = lambda s: (0, 0)
    const3 = lambda s: (0, 0, 0)
    state_buf = pltpu.VMEM((tt, 2 * NST), F32)
    chan_buf = pltpu.VMEM((tt, SSM_W), F32)
    return pl.pallas_call(
        body, grid=(n_chunks + 2,),
        in_specs=[pl.BlockSpec((tt, SSM_W), rrow(1)),
                  pl.BlockSpec((tt, SSM_W), rrow(1)),
                  pl.BlockSpec((tt, 2 * NST), rrow(0)),
                  pl.BlockSpec((tt, SSM_W), rrow(0)),
                  pl.BlockSpec((tt, 2 * SSM_W), rrow(0)),
                  pl.BlockSpec((N_BLK, BLK_CH, 2 * BLK_ST), const3),
                  pl.BlockSpec((N_BLK, 2 * BLK_ST, BLK_CH), const3),
                  pl.BlockSpec((8 * SUBLANES, NST), const),
                  pl.BlockSpec((1, SSM_W), const),
                  pl.BlockSpec((SSM_W, 2 * SSM_W), const)],
        out_specs=[pl.BlockSpec((tt, SSM_W), lambda s: (chunk_rows(jnp.maximum(s - 2, 0)), 0)),
                   pl.BlockSpec((SSM_W, 2 * SSM_W), const),
                   pl.BlockSpec((1, SSM_W), const),
                   pl.BlockSpec((2, NST, SSM_GC), const3),
                   pl.BlockSpec((2, NST, SSM_GC), const3),
                   pl.BlockSpec((SUBLANES, 2 * NST), const)],
        out_shape=[jax.ShapeDtypeStruct((nb * l, SSM_W), BF16),
                   jax.ShapeDtypeStruct((SSM_W, 2 * SSM_W), F32),
                   jax.ShapeDtypeStruct((1, SSM_W), F32),
                   jax.ShapeDtypeStruct((2, NST, SSM_GC), F32),
                   jax.ShapeDtypeStruct((2, NST, SSM_GC), F32),
                   jax.ShapeDtypeStruct((SUBLANES, 2 * NST), F32)],
        scratch_shapes=[state_buf, state_buf, state_buf, state_buf, chan_buf, chan_buf, chan_buf, chan_buf,
                        pltpu.VMEM((SUBLANES, 2 * NST), F32),
                        pltpu.VMEM((N_BLK, 2 * BLK_ST, BLK_CH), F32), pltpu.VMEM((N_BLK, 2 * BLK_ST, BLK_CH), F32)],
        compiler_params=_params("arbitrary"), name="s5_bwd")(
            proj, d_ycat, h, ypre, z, b_m, c_m, cst_rev, d_skip, w_glu)


def _head_mask(hh):
    lane = lax.broadcasted_iota(jnp.int32, (1, ATT_W), 1)
    return (lane >= hh * MEM_HD) & (lane < (hh + 1) * MEM_HD)


def _mem_kv(mem_ref, gm_ref, wkv_ref):
    m = mem_ref[0]
    mh = m * _rms_scale(m)
    mb = (mh * gm_ref[...]).astype(BF16)
    kv = _dot(mb, wkv_ref[...])
    return mh, mb, kv[:, :ATT_W].astype(BF16), kv[:, ATT_W:].astype(BF16)


def _stack_heads(dst_ref, a):
    for hh in range(MEM_HEADS):
        dst_ref[pl.ds(hh * N_MEM, N_MEM), :] = jnp.where(_head_mask(hh), a, jnp.zeros_like(a))


def _fold_heads(a):
    out = jnp.zeros((N_MEM, ATT_W), a.dtype)
    for hh in range(MEM_HEADS):
        out = out + jnp.where(_head_mask(hh), a[hh * N_MEM:(hh + 1) * N_MEM], 0.0)
    return out


def _softmax_heads(s):
    parts = []
    for hh in range(MEM_HEADS):
        sh = s[:, hh * N_MEM:(hh + 1) * N_MEM]
        e = jnp.exp(sh - jnp.max(sh, axis=-1, keepdims=True))
        parts.append(e / jnp.sum(e, axis=-1, keepdims=True))
    return jnp.concatenate(parts, axis=1)


def attn_fwd(proj, mem, g_mem, w_kv, nb, l):
    tq = min(512, l)
    nq = l // tq
    scale = MEM_HD ** -0.5

    def body(q_ref, mem_ref, gm_ref, wkv_ref, o_ref, k_s, v_s):
        @pl.when(pl.program_id(1) == 0)
        def _():
            _, _, k, v = _mem_kv(mem_ref, gm_ref, wkv_ref)
            _stack_heads(k_s, k)
            _stack_heads(v_s, v)

        p = _softmax_heads(_dot_nt(q_ref[...], k_s[...]) * scale)
        o_ref[...] = _dot(p.astype(BF16), v_s[...]).astype(BF16)

    const = lambda b, t: (0, 0)
    stacked = pltpu.VMEM((MEM_HEADS * N_MEM, ATT_W), BF16)
    return pl.pallas_call(
        body, grid=(nb, nq),
        in_specs=[pl.BlockSpec((tq, ATT_W), lambda b, t: (b * nq + t, 3)),
                  pl.BlockSpec((1, N_MEM, D_MODEL), lambda b, t: (b, 0, 0)),
                  pl.BlockSpec((1, D_MODEL), const),
                  pl.BlockSpec((D_MODEL, 2 * ATT_W), const)],
        out_specs=pl.BlockSpec((tq, ATT_W), lambda b, t: (b * nq + t, 0)),
        out_shape=jax.ShapeDtypeStruct((nb * l, ATT_W), BF16),
        scratch_shapes=[stacked, stacked],
        compiler_params=_params("arbitrary", "arbitrary"), name="attn_fwd")(proj, mem, g_mem, w_kv)


def attn_bwd(proj, d_ycat, mem, g_mem, w_kv, nb, l):
    tq = min(512, l)
    nq = l // tq
    scale = MEM_HD ** -0.5

    def body(q_ref, do_ref, mem_ref, gm_ref, wkv_ref, dq_ref, dwkv_ref, dgm_ref, k_s, v_s, mb_s, dk_s, dv_s):
        b = pl.program_id(0)
        t = pl.program_id(1)

        @pl.when((b == 0) & (t == 0))
        def _():
            dwkv_ref[...] = jnp.zeros_like(dwkv_ref)
            dgm_ref[...] = jnp.zeros_like(dgm_ref)

        @pl.when(t == 0)
        def _():
            _, mb, k, v = _mem_kv(mem_ref, gm_ref, wkv_ref)
            _stack_heads(k_s, k)
            _stack_heads(v_s, v)
            mb_s[...] = mb
            dk_s[...] = jnp.zeros_like(dk_s)
            dv_s[...] = jnp.zeros_like(dv_s)

        q = q_ref[...]
        do = do_ref[...]
        k = k_s[...]
        p = _softmax_heads(_dot_nt(q, k) * scale)
        dp = _dot_nt(do, v_s[...])
        ds = []
        for hh in range(MEM_HEADS):
            blk = slice(hh * N_MEM, (hh + 1) * N_MEM)
            ds.append(p[:, blk] * (dp[:, blk] - jnp.sum(dp[:, blk] * p[:, blk], axis=-1, keepdims=True)) * scale)
        ds = jnp.concatenate(ds, axis=1).astype(BF16)
        dq_ref[...] = _dot(ds, k).astype(BF16)
        dk_s[...] += _dot_tn(ds, q)
        dv_s[...] += _dot_tn(p.astype(BF16), do)

        @pl.when(t == nq - 1)
        def _():
            dkv = jnp.concatenate([_fold_heads(dk_s[...]), _fold_heads(dv_s[...])], axis=1).astype(BF16)
            dwkv_ref[...] += _dot_tn(mb_s[...], dkv)
            m = mem_ref[0]
            dgm_ref[...] += jnp.sum(_dot_nt(dkv, wkv_ref[...]) * (m * _rms_scale(m)), axis=0, keepdims=True)

    const = lambda b, t: (0, 0)
    return pl.pallas_call(
        body, grid=(nb, nq),
        in_specs=[pl.BlockSpec((tq, ATT_W), lambda b, t: (b * nq + t, 3)),
                  pl.BlockSpec((tq, ATT_W), lambda b, t: (b * nq + t, 3)),
                  pl.BlockSpec((1, N_MEM, D_MODEL), lambda b, t: (b, 0, 0)),
                  pl.BlockSpec((1, D_MODEL), const),
                  pl.BlockSpec((D_MODEL, 2 * ATT_W), const)],
        out_specs=[pl.BlockSpec((tq, ATT_W), lambda b, t: (b * nq + t, 0)),
                   pl.BlockSpec((D_MODEL, 2 * ATT_W), const),
                   pl.BlockSpec((1, D_MODEL), const)],
        out_shape=[jax.ShapeDtypeStruct((nb * l, ATT_W), BF16),
                   jax.ShapeDtypeStruct((D_MODEL, 2 * ATT_W), F32),
                   jax.ShapeDtypeStruct((1, D_MODEL), F32)],
        scratch_shapes=[pltpu.VMEM((MEM_HEADS * N_MEM, ATT_W), BF16), pltpu.VMEM((MEM_HEADS * N_MEM, ATT_W), BF16),
                        pltpu.VMEM((N_MEM, D_MODEL), BF16),
                        pltpu.VMEM((MEM_HEADS * N_MEM, ATT_W), F32), pltpu.VMEM((MEM_HEADS * N_MEM, ATT_W), F32)],
        compiler_params=_params("arbitrary", "arbitrary"), name="attn_bwd")(proj, d_ycat, mem, g_mem, w_kv)


def tail(y_pool, y_ssm, y_att, proj, x2, target, w_out, g_post):
    t = x2.shape[0]
    tm = min(ROW_TILE, t)

    def body(yp_ref, ys_ref, ya_ref, gate_ref, x_ref, tg_ref, w_ref, gp_ref,
             dz_ref, dyc_ref, dgate_ref, dw_ref, dgp_ref, loss_ref):
        @pl.when(pl.program_id(0) == 0)
        def _():
            dw_ref[...] = jnp.zeros_like(dw_ref)
            dgp_ref[...] = jnp.zeros_like(dgp_ref)
            loss_ref[...] = jnp.zeros_like(loss_ref)

        ycat = jnp.concatenate([yp_ref[...], ys_ref[...], ya_ref[...]], axis=1).astype(F32)
        gate = gate_ref[...].astype(F32)
        sg = jax.nn.sigmoid(gate)
        silu = gate * sg
        yb = (ycat * silu).astype(BF16)
        w = w_ref[...]
        out = _dot(yb, w)
        r2 = _rms_scale(out)
        oh = out * r2
        gp = gp_ref[...]
        err = (x_ref[...] + oh * gp) - tg_ref[...]
        loss_ref[...] += 0.5 * jnp.sum(jnp.mean(err * err, axis=-1, keepdims=True), axis=0, keepdims=True)
        dz = err * (1.0 / D_MODEL)
        dz_ref[...] = dz.astype(BF16)
        dgp_ref[...] += jnp.sum(dz * oh, axis=0, keepdims=True)
        dn = dz * gp
        dout = (r2 * (dn - oh * jnp.mean(dn * oh, axis=-1, keepdims=True))).astype(BF16)
        dw_ref[...] += _dot_tn(yb, dout)
        dy = _dot_nt(dout, w)
        dyc_ref[...] = (dy * silu).astype(BF16)
        dgate_ref[...] = (dy * ycat * (sg * (1.0 + gate * (1.0 - sg)))).astype(BF16)

    row = lambda i: (i, 0)
    const = lambda i: (0, 0)
    full = jax.ShapeDtypeStruct((t, D_MODEL), BF16)
    return pl.pallas_call(
        body, grid=(t // tm,),
        in_specs=[pl.BlockSpec((tm, POOL_W), row),
                  pl.BlockSpec((tm, SSM_W), row),
                  pl.BlockSpec((tm, ATT_W), row),
                  pl.BlockSpec((tm, D_MODEL), lambda i: (i, 1)),
                  pl.BlockSpec((tm, D_MODEL), row),
                  pl.BlockSpec((tm, D_MODEL), row),
                  pl.BlockSpec((D_MODEL, D_MODEL), const),
                  pl.BlockSpec((1, D_MODEL), const)],
        out_specs=[pl.BlockSpec((tm, D_MODEL), row),
                   pl.BlockSpec((tm, D_MODEL), row),
                   pl.BlockSpec((tm, D_MODEL), row),
                   pl.BlockSpec((D_MODEL, D_MODEL), const),
                   pl.BlockSpec((1, D_MODEL), const),
                   pl.BlockSpec((1, LANES), const)],
        out_shape=[full, full, full,
                   jax.ShapeDtypeStruct((D_MODEL, D_MODEL), F32),
                   jax.ShapeDtypeStruct((1, D_MODEL), F32),
                   jax.ShapeDtypeStruct((1, LANES), F32)],
        compiler_params=_params("arbitrary"), name="tail")(y_pool, y_ssm, y_att, proj, x2, target, w_out, g_post)


def in_proj_bwd(du_pool, du_ssm, dq, d_gate, x2, dz, g_pre, w_in):
    t = x2.shape[0]
    tm = min(ROW_TILE, t)
    shard = W_IN_SHARD

    def body(dup_ref, dus_ref, dq_ref, dg_ref, x_ref, dz_ref, g_ref, w_ref, gx_ref, dw_ref, dgp_ref):
        @pl.when(pl.program_id(0) == 0)
        def _():
            dw_ref[...] = jnp.zeros_like(dw_ref)
            dgp_ref[...] = jnp.zeros_like(dgp_ref)

        dproj = jnp.concatenate([dup_ref[...], dus_ref[...], dq_ref[...], dg_ref[...]], axis=1)
        x = x_ref[...]
        r1 = _rms_scale(x)
        xh = x * r1
        g = g_ref[...]
        hb = (xh * g).astype(BF16)
        dh = jnp.zeros((tm, D_MODEL), F32)
        for k in range(N_CHIPS):
            dp_k = dproj[:, k * shard:(k + 1) * shard]
            dw_ref[k] += _dot_tn(hb, dp_k)
            dh = dh + _dot_nt(dp_k, w_ref[k])
        dgp_ref[...] += jnp.sum(dh * xh, axis=0, keepdims=True)
        dn = dh * g
        gx_ref[...] = dz_ref[...].astype(F32) + r1 * (dn - xh * jnp.mean(dn * xh, axis=-1, keepdims=True))

    row = lambda i: (i, 0)
    const = lambda i: (0, 0)
    return pl.pallas_call(
        body, grid=(t // tm,),
        in_specs=[pl.BlockSpec((tm, POOL_W), row),
                  pl.BlockSpec((tm, SSM_W), row),
                  pl.BlockSpec((tm, ATT_W), row),
                  pl.BlockSpec((tm, D_MODEL), row),
                  pl.BlockSpec((tm, D_MODEL), row),
                  pl.BlockSpec((tm, D_MODEL), row),
                  pl.BlockSpec((1, D_MODEL), const),
                  pl.BlockSpec((N_CHIPS, D_MODEL, shard), lambda i: (0, 0, 0))],
        out_specs=[pl.BlockSpec((tm, D_MODEL), row),
                   pl.BlockSpec((N_CHIPS, D_MODEL, shard), lambda i: (0, 0, 0)),
                   pl.BlockSpec((1, D_MODEL), const)],
        out_shape=[jax.ShapeDtypeStruct((t, D_MODEL), F32),
                   jax.ShapeDtypeStruct((N_CHIPS, D_MODEL, shard), F32),
                   jax.ShapeDtypeStruct((1, D_MODEL), F32)],
        compiler_params=_params("arbitrary"), name="in_proj_bwd")(du_pool, du_ssm, dq, d_gate, x2, dz, g_pre, w_in)


def _block_diag(blocks):
    g, r, c = blocks.shape
    eye = jnp.eye(g, dtype=blocks.dtype)
    return jnp.einsum("grc,gh->grhc", blocks, eye).reshape(g * r, g * c)


def _block_diag_extract(mat, g, r, c):
    eye = jnp.eye(g, dtype=mat.dtype)
    return jnp.einsum("grhc,gh->grc", mat.reshape(g, r, g, c), eye)


def local_step(x, mem, target, g_pre, w_in, w_pool, pool_scale, a_re, a_im, log_dt, b_re, b_im, c_re, c_im, d_skip,
               later_shards, g_mem, g_post):
    nb, l, _ = x.shape
    x2 = x.reshape(nb * l, D_MODEL)
    tg2 = target.reshape(nb * l, D_MODEL)

    rowv = lambda a: a.reshape(1, NST)
    lanes_last = lambda b: b.transpose(2, 0, 1).reshape(SSM_GC, NST)
    ldt_row = rowv(jnp.broadcast_to(log_dt.reshape(SSM_NG, 1), (SSM_NG, SSM_N)))
    b_re_t = lanes_last(b_re)
    b_im_t = lanes_last(b_im)
    cst_f, cst_b = s5_scan_consts(rowv(a_re), rowv(a_im), ldt_row)
    bbr, bbi = s5_bbar(rowv(a_re), rowv(a_im), ldt_row, b_re_t, b_im_t)
    eye = jnp.eye(BLK_GROUPS, dtype=F32)
    blk_in = lambda bb: jnp.einsum("cmgn,gh->mhcgn", bb.reshape(SSM_GC, N_BLK, BLK_GROUPS, SSM_N), eye).reshape(
        N_BLK, BLK_CH, BLK_ST)
    blk_out = lambda cc: jnp.einsum("mgcn,gh->mgnhc", cc.reshape(N_BLK, BLK_GROUPS, SSM_GC, SSM_N), eye).reshape(
        N_BLK, BLK_ST, BLK_CH)
    b_m = jnp.concatenate([blk_in(bbr), blk_in(bbi)], axis=2).astype(BF16)
    c_m = jnp.concatenate([blk_out(c_re), -blk_out(c_im)], axis=1).astype(BF16)
    w_pool_blk = _block_diag(w_pool.reshape(4, POOL_GW, POOL_GW)).astype(BF16)

    proj, (g_glu, g_kv, g_out) = in_proj(x2, g_pre, w_in, later_shards)
    w_glu = _from_chip_major(g_glu)
    w_kv = g_kv.reshape(D_MODEL, 2 * ATT_W)
    w_out = g_out.reshape(D_MODEL, D_MODEL)
    y_pool = pool_fwd(proj, w_pool_blk, pool_scale, nb, l)
    y_ssm, h, ypre, z = s5_fwd(proj, b_m, c_m, cst_f, d_skip, w_glu, nb, l)
    y_att = attn_fwd(proj, mem, g_mem, w_kv, nb, l)
    dz, d_ycat, d_gate, d_w_out, d_g_post, loss = tail(y_pool, y_ssm, y_att, proj, x2, tg2, w_out, g_post)
    du_pool, d_wp_blk, d_pool_scale = pool_bwd(proj, d_ycat, w_pool_blk, pool_scale, nb, l)
    du_ssm, d_w_glu, d_d_skip, d_cc, d_bb, d_lam = s5_bwd(
        proj, d_ycat, h, ypre, z, b_m, c_m, cst_b, d_skip, w_glu, nb, l)
    dq, d_w_kv, d_g_mem = attn_bwd(proj, d_ycat, mem, g_mem, w_kv, nb, l)
    grad_x, d_w_in, d_g_pre = in_proj_bwd(du_pool, du_ssm, dq, d_gate, x2, dz, g_pre, w_in)

    d_c = d_cc[0].reshape(SSM_NG, SSM_N, SSM_GC)
    d_ci = d_cc[1].reshape(SSM_NG, SSM_N, SSM_GC)
    d_lam_row = d_lam[0]
    d_ar, d_ai, d_ld, d_br, d_bi = s5_param_bwd(
        rowv(a_re), rowv(a_im), ldt_row, b_re_t, b_im_t,
        d_lam_row[:NST].reshape(1, NST), d_lam_row[NST:].reshape(1, NST), d_bb[0].T, d_bb[1].T)
    from_lanes_last = lambda b: b.reshape(SSM_GC, SSM_NG, SSM_N).transpose(1, 2, 0).reshape(1, SSM_NG, SSM_N, SSM_GC)

    grads = {
        "g_pre": d_g_pre,
        "w_in": d_w_in,
        "w_pool": _block_diag_extract(d_wp_blk, 4, POOL_GW, POOL_GW).reshape(1, 4, POOL_GW, POOL_GW),
        "pool_scale": d_pool_scale,
        "a_re": d_ar.reshape(1, SSM_NG, SSM_N),
        "a_im": d_ai.reshape(1, SSM_NG, SSM_N),
        "log_dt": d_ld[:, :SSM_NG],
        "b_re": from_lanes_last(d_br),
        "b_im": from_lanes_last(d_bi),
        "c_re": d_c.transpose(0, 2, 1).reshape(1, SSM_NG, SSM_GC, SSM_N),
        "c_im": (-d_ci).transpose(0, 2, 1).reshape(1, SSM_NG, SSM_GC, SSM_N),
        "d_skip": d_d_skip,
        "w_glu": d_w_glu,
        "g_mem": d_g_mem,
        "w_kv": d_w_kv,
        "w_out": d_w_out,
        "g_post": d_g_post,
    }
    return loss, grad_x.reshape(nb, l, D_MODEL), grads


VMEM_SPEC = pl.BlockSpec(memory_space=pltpu.VMEM)
ANY_SPEC = pl.BlockSpec(memory_space=pl.ANY)


def _place():
    return lax.axis_index("x"), lax.axis_index("y"), lax.axis_index("c")


def _other_chips(x, y):
    return [(1 - x, y), (x, 1 - y), (1 - x, 1 - y)]


def gather_weights(shards, casts):
    n = len(shards)
    nc = len(casts)

    def body(*refs):
        ins, cast_ins = refs[:n], refs[n:n + nc]
        outs, cast_outs = refs[n + nc:2 * n + nc], refs[2 * n + nc:2 * (n + nc)]
        send_sems, recv_sems = refs[2 * (n + nc):]
        x, y, c = _place()
        k = 2 * x + y
        chips = _other_chips(x, y)
        for i in range(n):
            outs[i][k] = ins[i][...].astype(BF16)
        for i in range(nc):
            cast_outs[i][...] = cast_ins[i][...].astype(BF16)

        def copy(i, block, core_half, to, sem):
            hr = shards[i].shape[0] // 2
            rows = outs[i].at[block, pl.ds(pl.multiple_of(core_half * hr, 2 * SUBLANES), hr)]
            return pltpu.make_async_remote_copy(
                src_ref=rows, dst_ref=rows, send_sem=send_sems.at[sem], recv_sem=recv_sems.at[sem],
                device_id=to, device_id_type=MESH)

        pairs = [(i, j) for i in range(n) for j in range(3)]
        first = [copy(i, k, c, (*chips[j], c), i * 3 + j) for i, j in pairs]
        for cp in first:
            cp.start()
        passed = []
        for i, j in pairs:
            px, py = chips[j]
            copy(i, 2 * px + py, c, (px, py, c), i * 3 + j).wait_recv()
            fw = copy(i, 2 * px + py, c, (x, y, 1 - c), 3 * n + i * 3 + j)
            fw.start()
            passed.append(fw)
        for i, j in pairs:
            px, py = chips[j]
            copy(i, 2 * px + py, 1 - c, (x, y, 1 - c), 3 * n + i * 3 + j).wait_recv()
        for cp in first + passed:
            cp.wait_send()

    res = pl.pallas_call(
        body,
        out_shape=([jax.ShapeDtypeStruct((N_CHIPS,) + s.shape, BF16) for s in shards]
                   + [jax.ShapeDtypeStruct(s.shape, BF16) for s in casts]),
        in_specs=[VMEM_SPEC] * (n + nc), out_specs=[VMEM_SPEC] * (n + nc),
        scratch_shapes=[pltpu.SemaphoreType.DMA((6 * n,)), pltpu.SemaphoreType.DMA((6 * n,))],
        compiler_params=pltpu.CompilerParams(vmem_limit_bytes=VMEM_LIMIT),
        name="gather_weights")(*shards, *casts)
    return res[:n], res[n:]


SEMS_PER_ITEM = 5


def reduce_all(g4s, packed):
    n = len(g4s)
    dims = [g.shape[1:] for g in g4s]
    pr = packed.shape[0]

    def body(*refs):
        g_refs, p_ref = refs[:n], refs[n]
        outs, op = refs[n + 1:2 * n + 1], refs[2 * n + 1]
        scr = refs[2 * n + 2:]
        mine, theirs, sendb, recvb = scr[0:n], scr[n:2 * n], scr[2 * n:3 * n], scr[3 * n:4 * n]
        p_theirs, gat, lsems, send_sems, recv_sems = scr[4 * n:]
        x, y, c = _place()
        k = 2 * x + y
        chips = _other_chips(x, y)
        sib = (x, y, 1 - c)

        def halves(rows):
            hr = rows // 2
            return (pl.ds(pl.multiple_of(c * hr, SUBLANES), hr), pl.ds(pl.multiple_of((1 - c) * hr, SUBLANES), hr))

        def remote(src, dst, sem, to):
            return pltpu.make_async_remote_copy(
                src_ref=src, dst_ref=dst, send_sem=send_sems.at[sem], recv_sem=recv_sems.at[sem],
                device_id=to, device_id_type=MESH)

        loads, started = [], []
        for i in range(n):
            my_rows, sib_rows = halves(dims[i][0])
            ld = pltpu.make_async_copy(g_refs[i].at[:, my_rows, :], mine[i], lsems.at[i])
            ld.start()
            loads.append(ld)
            sw = remote(g_refs[i].at[:, sib_rows, :], theirs[i], SEMS_PER_ITEM * i, sib)
            sw.start()
            started.append(sw)
        p_my, p_sib = halves(pr)
        p_swap = remote(p_ref.at[p_sib, :], p_theirs, SEMS_PER_ITEM * n, sib)
        p_swap.start()
        started.append(p_swap)

        p_swap.wait_recv()
        gat[k] = p_ref[p_my, :] + p_theirs[...]
        for j, (px, py) in enumerate(chips):
            cp = remote(gat.at[k], gat.at[k], SEMS_PER_ITEM * n + 1 + j, (px, py, c))
            cp.start()
            started.append(cp)
        ici = []
        for i in range(n):
            loads[i].wait()
            started[i].wait_recv()
            for j, (px, py) in enumerate(chips):
                kp = 2 * px + py
                sendb[i][j] = (mine[i][kp] + theirs[i][kp]).astype(BF16)
                cp = remote(sendb[i].at[j], recvb[i].at[j], SEMS_PER_ITEM * i + 1 + j, (px, py, c))
                cp.start()
                ici.append(cp)

        for j, (px, py) in enumerate(chips):
            kp = 2 * px + py
            remote(gat.at[kp], gat.at[kp], SEMS_PER_ITEM * n + 1 + j, (px, py, c)).wait_recv()
        op[p_my, :] = ((gat[0] + gat[1]) + gat[2]) + gat[3]
        last = remote(op.at[p_my, :], op.at[p_my, :], SEMS_PER_ITEM * n + 4, sib)
        last.start()
        started.append(last)
        for i in range(n):
            my_rows, _ = halves(dims[i][0])
            half = mine[i][k] + theirs[i][k]
            for j in range(3):
                ici[3 * i + j].wait_recv()
                half = half + recvb[i][j].astype(F32)
            outs[i][my_rows, :] = half
            last = remote(outs[i].at[my_rows, :], outs[i].at[my_rows, :], SEMS_PER_ITEM * i + 4, sib)
            last.start()
            started.append(last)
        remote(op.at[p_sib, :], op.at[p_sib, :], SEMS_PER_ITEM * n + 4, sib).wait_recv()
        for i in range(n):
            _, sib_rows = halves(dims[i][0])
            remote(outs[i].at[sib_rows, :], outs[i].at[sib_rows, :], SEMS_PER_ITEM * i + 4, sib).wait_recv()
        for cp in started + ici:
            cp.wait_send()

    n_sems = SEMS_PER_ITEM * (n + 1)
    scratch = ([pltpu.VMEM((N_CHIPS, r // 2, cd), F32) for r, cd in dims] * 2
               + [pltpu.VMEM((3, r // 2, cd), BF16) for r, cd in dims] * 2
               + [pltpu.VMEM((pr // 2, LANES), F32), pltpu.VMEM((N_CHIPS, pr // 2, LANES), F32),
                  pltpu.SemaphoreType.DMA((n,)), pltpu.SemaphoreType.DMA((n_sems,)), pltpu.SemaphoreType.DMA((n_sems,))])
    res = pl.pallas_call(
        body,
        out_shape=[jax.ShapeDtypeStruct(d, F32) for d in dims] + [jax.ShapeDtypeStruct(packed.shape, F32)],
        in_specs=[ANY_SPEC] * n + [VMEM_SPEC], out_specs=[VMEM_SPEC] * (n + 1),
        scratch_shapes=scratch,
        compiler_params=pltpu.CompilerParams(vmem_limit_bytes=VMEM_LIMIT),
        name="reduce_all")(*g4s, packed)
    return res[:n], res[n]


def adamw_all(ws, gs, ms, vs):
    n = len(ws)

    def body(*refs):
        w, g, m, v = refs[:n], refs[n:2 * n], refs[2 * n:3 * n], refs[3 * n:4 * n]
        od, om, ov = refs[4 * n:5 * n], refs[5 * n:6 * n], refs[6 * n:]
        for i in range(n):
            od[i][...], om[i][...], ov[i][...] = _adamw(w[i][...], g[i][...], m[i][...], v[i][...])

    shapes = [jax.ShapeDtypeStruct(a.shape, F32) for a in ws]
    res = pl.pallas_call(
        body, out_shape=shapes * 3, in_specs=[VMEM_SPEC] * (4 * n), out_specs=[VMEM_SPEC] * (3 * n),
        compiler_params=pltpu.CompilerParams(vmem_limit_bytes=VMEM_LIMIT),
        name="adamw_all")(*ws, *gs, *ms, *vs)
    return res[:n], res[n:2 * n], res[2 * n:]


WEIGHTS = ("g_pre", "w_in", "w_pool", "pool_scale", "a_re", "a_im", "log_dt", "b_re", "b_im", "c_re", "c_im", "d_skip",
           "w_glu", "g_mem", "w_kv", "w_out", "g_post")
SHARDED = ("w_in", "w_glu", "w_kv", "w_out")
REPLICATED = tuple(n for n in WEIGHTS if n not in SHARDED)
PACK_TILE = SUBLANES * LANES


def _pack(arrays):
    parts = []
    for a in arrays:
        flat = a.reshape(-1)
        parts.append(jnp.pad(flat, (0, -flat.shape[0] % PACK_TILE)).reshape(-1, LANES))
    rows = sum(p.shape[0] for p in parts)
    if rows % (2 * SUBLANES):
        parts.append(jnp.zeros((SUBLANES, LANES), F32))
    return jnp.concatenate(parts, axis=0)


def _unpack(packed, shapes):
    out, row = [], 0
    for shp in shapes:
        size = math.prod(shp)
        rows = -(-size // PACK_TILE) * SUBLANES
        out.append(packed[row:row + rows].reshape(-1)[:size].reshape(shp))
        row += rows
    return out


def _chip_major(a, n_cols):
    return a.reshape(a.shape[0], N_CHIPS, n_cols).transpose(1, 0, 2)


def _from_chip_major(a):
    return a.transpose(1, 0, 2).reshape(a.shape[1], N_CHIPS * a.shape[2])


def kernel(x, mem, g_pre, w_in, w_pool, pool_scale, a_re, a_im, log_dt, b_re, b_im, c_re, c_im, d_skip, w_glu, g_mem, w_kv, w_out, g_post, loss_target, m_g_pre, m_w_in, m_w_pool, m_pool_scale, m_a_re, m_a_im, m_log_dt, m_b_re, m_b_im, m_c_re, m_c_im, m_d_skip, m_w_glu, m_g_mem, m_w_kv, m_w_out, m_g_post, v_g_pre, v_w_in, v_w_pool, v_pool_scale, v_a_re, v_a_im, v_log_dt, v_b_re, v_b_im, v_c_re, v_c_im, v_d_skip, v_w_glu, v_g_mem, v_w_kv, v_w_out, v_g_post):
    given = dict(locals())
    wts = {n: given[n] for n in WEIGHTS}
    mom = {n: given["m_" + n] for n in WEIGHTS}
    var = {n: given["v_" + n] for n in WEIGHTS}

    (g_in,), later_shards = gather_weights([w_in[0]], [w_glu[0], w_kv[0], w_out[0]])
    loss_part, grad_x, grads = local_step(
        x, mem, loss_target, g_pre, g_in, w_pool[0], pool_scale, a_re[0], a_im[0], log_dt[0], b_re[0], b_im[0],
        c_re[0], c_im[0], d_skip, later_shards, g_mem, g_post)

    partial4 = [
        grads["w_in"],
        _chip_major(grads["w_glu"], 2 * SSM_W // N_CHIPS),
        grads["w_kv"].reshape(N_CHIPS, D_MODEL // N_CHIPS, 2 * ATT_W),
        grads["w_out"].reshape(N_CHIPS, D_MODEL // N_CHIPS, D_MODEL),
    ]
    loss_tile = jnp.broadcast_to(loss_part, (SUBLANES, LANES))
    sharded_g, packed_g = reduce_all(partial4, _pack([grads[n] for n in REPLICATED] + [loss_tile]))
    small_g = _unpack(packed_g, [wts[n].shape for n in REPLICATED] + [(SUBLANES, LANES)])
    loss = small_g[-1][0, 0]
    grad = dict(zip(SHARDED, [g[None] for g in sharded_g]))
    grad.update(zip(REPLICATED, small_g[:-1]))

    deltas, new_ms, new_vs = adamw_all([wts[n] for n in WEIGHTS], [grad[n] for n in WEIGHTS],
                                       [mom[n] for n in WEIGHTS], [var[n] for n in WEIGHTS])
    return (loss, grad_x, *[grad[n] for n in WEIGHTS], *deltas, *new_ms, *new_vs)
```

```python
import functools
import math

import jax
import jax.numpy as jnp
from jax import lax
from jax.experimental import pallas as pl
from jax.experimental.pallas import tpu as pltpu

F32 = jnp.float32
BF16 = jnp.bfloat16

D_MODEL = 1024
POOL_W = 384
SSM_W = 384
ATT_W = 256
POOL_GW = 96
POOL_WINDOWS = (2, 4, 8, 16)
POOL_PAD = 16
SSM_NG = 24
SSM_N = 64
SSM_GC = 16
N_CHIPS = 4
N_DEV = 8
W_IN_SHARD = 2 * D_MODEL // N_CHIPS
NST = SSM_NG * SSM_N
BLK_CH = 128
BLK_GROUPS = BLK_CH // SSM_GC
BLK_ST = BLK_GROUPS * SSM_N
N_BLK = SSM_W // BLK_CH
N_MEM = 256
MEM_HEADS = 4
MEM_HD = 64
EPS = 1e-6

ADAM_LR = 0.001
ADAM_B1 = 0.9
ADAM_B2 = 0.999
ADAM_EPS = 1e-08
ADAM_WD = 0.01
ADAM_STEP = 10

SUBLANES = 8
LANES = 128
V7X_VMEM_BYTES = 64 * 2**20
VMEM_LIMIT = V7X_VMEM_BYTES - 8 * 2**20
SCAN_COLS = 512
ROW_TILE = 256
IN_PROJ_ROWS = 1024
MESH = pl.DeviceIdType.MESH

NT = (((1,), (1,)), ((), ()))
TN = (((0,), (0,)), ((), ()))


def _params(*sem):
    return pltpu.CompilerParams(dimension_semantics=sem, vmem_limit_bytes=VMEM_LIMIT)


def _dot(a, b):
    return jnp.dot(a, b, preferred_element_type=F32)


def _dot_nt(a, b):
    return lax.dot_general(a, b, NT, preferred_element_type=F32)


def _dot_tn(a, b):
    return lax.dot_general(a, b, TN, preferred_element_type=F32)


def _rms_scale(v):
    return lax.rsqrt(jnp.mean(v * v, axis=-1, keepdims=True) + EPS)


def _adamw(w, g, m, v):
    m = ADAM_B1 * m + (1.0 - ADAM_B1) * g
    v = ADAM_B2 * v + (1.0 - ADAM_B2) * (g * g)
    m_hat = m / (1.0 - ADAM_B1 ** ADAM_STEP)
    v_hat = v / (1.0 - ADAM_B2 ** ADAM_STEP)
    delta = -ADAM_LR * (m_hat / (jnp.sqrt(v_hat) + ADAM_EPS) + ADAM_WD * w)
    return delta, m, v


def _phase_shard(j):
    cx, cy = lax.axis_index("x"), lax.axis_index("y")
    others = [2 * px + py for px, py in _other_chips(cx, cy)]
    own = 2 * cx + cy
    return jnp.where(j == 0, own, jnp.where(j == 1, others[0], jnp.where(j == 2, others[1], others[2])))


def in_proj(x2, g_pre, w_in_shard, later_shards):
    t = x2.shape[0]
    tm = min(IN_PROJ_ROWS, t)
    n_tiles = t // tm
    n = len(later_shards)

    def body(x_ref, g_ref, ws_ref, *rest):
        shard_refs = rest[:n]
        o_ref, w_out_ref, gathered = rest[n], rest[n + 1], rest[n + 2:2 * n + 2]
        w_buf, h_buf = rest[2 * n + 2], rest[2 * n + 3]
        cast_bufs = rest[2 * n + 4:3 * n + 4]
        local_sems, send_sems, recv_sems = rest[3 * n + 4:]
        j = pl.program_id(0)
        i = pl.program_id(1)
        cx, cy, cc = _place()
        k = 2 * cx + cy
        chips = _other_chips(cx, cy)

        half = w_in_shard.shape[0] // 2

        def w_copy(p, block, core_half, forward):
            px, py = chips[p]
            rows = w_buf.at[block, pl.ds(pl.multiple_of(core_half * half, 2 * SUBLANES), half)]
            sem = 3 + p if forward else p
            return pltpu.make_async_remote_copy(
                src_ref=rows, dst_ref=rows, send_sem=send_sems.at[sem], recv_sem=recv_sems.at[sem],
                device_id=(cx, cy, 1 - cc) if forward else (px, py, cc), device_id_type=MESH)

        def own(m):
            return pltpu.make_async_copy(cast_bufs[m], gathered[m].at[k], local_sems.at[m])

        def later_copy(m, p, block):
            px, py = chips[p]
            return pltpu.make_async_remote_copy(
                src_ref=cast_bufs[m], dst_ref=gathered[m].at[block], send_sem=send_sems.at[6 + 3 * m + p],
                recv_sem=recv_sems.at[6 + 3 * m + p], device_id=(px, py, cc), device_id_type=MESH)

        @pl.when((j == 0) & (i == 0))
        def _():
            w_buf[k] = ws_ref[...].astype(BF16)
            for p in range(3):
                w_copy(p, k, cc, False).start()
            for m in range(n):
                cast_bufs[m][...] = shard_refs[m][...].astype(BF16)
                own(m).start()
                for p in range(3):
                    later_copy(m, p, k).start()

        rows = pl.ds(pl.multiple_of(i * tm, tm), tm)

        @pl.when(j == 0)
        def _():
            x = x_ref[...]
            h_buf[rows, :] = (x * _rms_scale(x) * g_ref[...]).astype(BF16)

        for p in range(3):
            @pl.when((j == p + 1) & (i == 0))
            def _(p=p):
                px, py = chips[p]
                w_copy(p, 2 * px + py, cc, False).wait_recv()
                w_copy(p, 2 * px + py, cc, True).start()
                w_copy(p, 2 * px + py, 1 - cc, True).wait_recv()

        o_ref[...] = _dot(h_buf[rows, :], w_buf[_phase_shard(j)]).astype(BF16)

        @pl.when((j == N_CHIPS - 1) & (i == n_tiles - 1))
        def _():
            out_cp = pltpu.make_async_copy(w_buf, w_out_ref, local_sems.at[n])
            out_cp.start()
            for m in range(n):
                own(m).wait()
                for p in range(3):
                    px, py = chips[p]
                    later_copy(m, p, 2 * px + py).wait_recv()
            for p in range(3):
                px, py = chips[p]
                w_copy(p, k, cc, False).wait_send()
                w_copy(p, 2 * px + py, cc, True).wait_send()
            for m in range(n):
                for p in range(3):
                    later_copy(m, p, k).wait_send()
            out_cp.wait()

    whole = lambda a: pl.BlockSpec(a.shape, lambda j, i: (0,) * a.ndim)
    res = pl.pallas_call(
        body, grid=(N_CHIPS, n_tiles),
        in_specs=[pl.BlockSpec((tm, D_MODEL), lambda j, i: (jnp.where(j == 0, i, n_tiles - 1), 0)),
                  pl.BlockSpec((1, D_MODEL), lambda j, i: (0, 0)),
                  whole(w_in_shard)] + [whole(s) for s in later_shards],
        out_specs=[pl.BlockSpec((tm, W_IN_SHARD), lambda j, i: (i, _phase_shard(j)))] + [ANY_SPEC] * (n + 1),
        out_shape=([jax.ShapeDtypeStruct((t, 2 * D_MODEL), BF16),
                    jax.ShapeDtypeStruct((N_CHIPS,) + w_in_shard.shape, BF16)]
                   + [jax.ShapeDtypeStruct((N_CHIPS,) + s.shape, BF16) for s in later_shards]),
        scratch_shapes=([pltpu.VMEM((N_CHIPS,) + w_in_shard.shape, BF16), pltpu.VMEM((t, D_MODEL), BF16)]
                        + [pltpu.VMEM(s.shape, BF16) for s in later_shards]
                        + [pltpu.SemaphoreType.DMA((n + 1,)), pltpu.SemaphoreType.DMA((3 * n + 6,)),
                           pltpu.SemaphoreType.DMA((3 * n + 6,))]),
        compiler_params=_params("arbitrary", "arbitrary"), name="in_proj")(x2, g_pre, w_in_shard, *later_shards)
    return res[0], res[1], res[2:]


def _pool_lane_window(shape):
    ch = lax.broadcasted_iota(jnp.int32, shape, 1)
    return jnp.where(ch < POOL_GW, 2.0, jnp.where(ch < 2 * POOL_GW, 4.0, jnp.where(ch < 3 * POOL_GW, 8.0, 16.0)))


def _pool_select(win, s2, s4, s8, s16):
    return jnp.where(win == 2.0, s2, jnp.where(win == 4.0, s4, jnp.where(win == 8.0, s8, s16)))


def _pool_div_count(v, win):
    pos = (lax.broadcasted_iota(jnp.int32, (POOL_PAD, POOL_W), 0) + 1).astype(F32)
    head = v[:POOL_PAD] / jnp.minimum(pos, win)
    return jnp.concatenate([head, v[POOL_PAD:] * (1.0 / win)], axis=0)


def _pool_diff(u, pad_ref, l):
    lo = POOL_PAD
    pad_ref[pl.ds(lo, l), :] = u
    s2 = u + pad_ref[pl.ds(lo - 1, l), :]
    pad_ref[pl.ds(lo, l), :] = s2
    s4 = s2 + pad_ref[pl.ds(lo - 2, l), :]
    pad_ref[pl.ds(lo, l), :] = s4
    s8 = s4 + pad_ref[pl.ds(lo - 4, l), :]
    pad_ref[pl.ds(lo, l), :] = s8
    s16 = s8 + pad_ref[pl.ds(lo - 8, l), :]
    win = _pool_lane_window((1, POOL_W))
    return _pool_div_count(_pool_select(win, s2, s4, s8, s16), win) - u, win


def pool_fwd(proj, w_blk, pool_scale, nb, l):
    def body(u_ref, w_ref, ps_ref, y_ref, pad_ref):
        pad_ref[pl.ds(0, POOL_PAD), :] = jnp.zeros((POOL_PAD, POOL_W), F32)
        d, _ = _pool_diff(u_ref[...].astype(F32), pad_ref, l)
        y_ref[...] = (_dot(d.astype(BF16), w_ref[...]) * ps_ref[...]).astype(BF16)

    return pl.pallas_call(
        body, grid=(nb,),
        in_specs=[pl.BlockSpec((l, POOL_W), lambda b: (b, 0)),
                  pl.BlockSpec((POOL_W, POOL_W), lambda b: (0, 0)),
                  pl.BlockSpec((1, POOL_W), lambda b: (0, 0))],
        out_specs=pl.BlockSpec((l, POOL_W), lambda b: (b, 0)),
        out_shape=jax.ShapeDtypeStruct((nb * l, POOL_W), BF16),
        scratch_shapes=[pltpu.VMEM((POOL_PAD + l, POOL_W), F32)],
        compiler_params=_params("arbitrary"), name="pool_fwd")(proj, w_blk, pool_scale)


def pool_bwd(proj, d_ycat, w_blk, pool_scale, nb, l):
    def body(u_ref, dy_ref, w_ref, ps_ref, du_ref, dw_ref, dps_ref, pad_ref, padb_ref):
        b = pl.program_id(0)

        @pl.when(b == 0)
        def _():
            dw_ref[...] = jnp.zeros_like(dw_ref)
            dps_ref[...] = jnp.zeros_like(dps_ref)

        pad_ref[pl.ds(0, POOL_PAD), :] = jnp.zeros((POOL_PAD, POOL_W), F32)
        padb_ref[pl.ds(l, POOL_PAD), :] = jnp.zeros((POOL_PAD, POOL_W), F32)
        d, win = _pool_diff(u_ref[...].astype(F32), pad_ref, l)
        db = d.astype(BF16)
        w = w_ref[...]
        dy = dy_ref[...].astype(F32)
        dps_ref[...] += jnp.sum(dy * _dot(db, w), axis=0, keepdims=True)
        dyo = (dy * ps_ref[...]).astype(BF16)
        dw_ref[...] += _dot_tn(db, dyo)
        dd = _dot_nt(dyo, w)
        e = _pool_div_count(dd, win)
        padb_ref[pl.ds(0, l), :] = e
        f2 = e + padb_ref[pl.ds(1, l), :]
        padb_ref[pl.ds(0, l), :] = f2
        f4 = f2 + padb_ref[pl.ds(2, l), :]
        padb_ref[pl.ds(0, l), :] = f4
        f8 = f4 + padb_ref[pl.ds(4, l), :]
        padb_ref[pl.ds(0, l), :] = f8
        f16 = f8 + padb_ref[pl.ds(8, l), :]
        du_ref[...] = (_pool_select(win, f2, f4, f8, f16) - dd).astype(BF16)

    return pl.pallas_call(
        body, grid=(nb,),
        in_specs=[pl.BlockSpec((l, POOL_W), lambda b: (b, 0)),
                  pl.BlockSpec((l, POOL_W), lambda b: (b, 0)),
                  pl.BlockSpec((POOL_W, POOL_W), lambda b: (0, 0)),
                  pl.BlockSpec((1, POOL_W), lambda b: (0, 0))],
        out_specs=[pl.BlockSpec((l, POOL_W), lambda b: (b, 0)),
                   pl.BlockSpec((POOL_W, POOL_W), lambda b: (0, 0)),
                   pl.BlockSpec((1, POOL_W), lambda b: (0, 0))],
        out_shape=[jax.ShapeDtypeStruct((nb * l, POOL_W), BF16),
                   jax.ShapeDtypeStruct((POOL_W, POOL_W), F32),
                   jax.ShapeDtypeStruct((1, POOL_W), F32)],
        scratch_shapes=[pltpu.VMEM((POOL_PAD + l, POOL_W), F32), pltpu.VMEM((POOL_PAD + l, POOL_W), F32)],
        compiler_params=_params("arbitrary"), name="pool_bwd")(proj, d_ycat, w_blk, pool_scale)


def _discretise(a_re, a_im, ldt, b_re, b_im):
    dt = jnp.exp(ldt)
    mag = jnp.exp(a_re * dt)
    th = a_im * dt
    lr = mag * jnp.cos(th)
    li = mag * jnp.sin(th)
    nr = lr - 1.0
    den = a_re * a_re + a_im * a_im
    fr = (nr * a_re + li * a_im) / den
    fi = (li * a_re - nr * a_im) / den
    return lr, li, fr * b_re - fi * b_im, fr * b_im + fi * b_re


def _cmul(ar, ai, br, bi):
    return ar * br - ai * bi, ar * bi + ai * br


def s5_scan_consts(a_re_row, a_im_row, ldt_row):
    def body(ar_ref, ai_ref, ld_ref, f_ref, b_ref):
        dt = jnp.exp(ld_ref[...])
        mag = jnp.exp(ar_ref[...] * dt)
        th = ai_ref[...] * dt
        lr = mag * jnp.cos(th)
        li = mag * jnp.sin(th)
        r = lax.broadcasted_iota(jnp.int32, (SUBLANES, NST), 0)
        for out_ref, sign, rev in ((f_ref, 1.0, False), (b_ref, -1.0, True)):
            p = [(lr, sign * li)]
            for _ in range(SUBLANES - 1):
                p.append(_cmul(p[-1][0], p[-1][1], lr, sign * li))
            for idx, s in enumerate((1, 2, 4)):
                inside = (r < SUBLANES - s) if rev else (r >= s)
                out_ref[pl.ds(2 * idx * SUBLANES, SUBLANES), :] = jnp.where(inside, p[s - 1][0], 0.0)
                out_ref[pl.ds((2 * idx + 1) * SUBLANES, SUBLANES), :] = jnp.where(inside, p[s - 1][1], 0.0)
            pr = jnp.zeros((SUBLANES, NST), F32)
            pi = jnp.zeros((SUBLANES, NST), F32)
            for row in range(SUBLANES):
                power = (SUBLANES - row) if rev else (row + 1)
                pr = jnp.where(r == row, p[power - 1][0], pr)
                pi = jnp.where(r == row, p[power - 1][1], pi)
            out_ref[pl.ds(6 * SUBLANES, SUBLANES), :] = pr
            out_ref[pl.ds(7 * SUBLANES, SUBLANES), :] = pi

    shape = jax.ShapeDtypeStruct((8 * SUBLANES, NST), F32)
    return pl.pallas_call(body, out_shape=[shape, shape], name="s5_scan_consts")(a_re_row, a_im_row, ldt_row)


def s5_bbar(a_re_row, a_im_row, ldt_row, b_re_t, b_im_t):
    def body(ar, ai, ld, br, bi, o_re, o_im):
        _, _, bbr, bbi = _discretise(ar[...], ai[...], ld[...], br[...], bi[...])
        o_re[...] = bbr
        o_im[...] = bbi

    shape = jax.ShapeDtypeStruct((SSM_GC, NST), F32)
    return pl.pallas_call(body, out_shape=[shape, shape], name="s5_bbar")(a_re_row, a_im_row, ldt_row, b_re_t, b_im_t)


def s5_param_bwd(a_re_row, a_im_row, ldt_row, b_re_t, b_im_t, d_lr, d_li, d_bbr, d_bbi):
    def body(ar, ai, ld, br, bi, g_lr, g_li, g_br, g_bi, o_ar, o_ai, o_ld, o_br, o_bi):
        _, vjp = jax.vjp(_discretise, ar[...], ai[...], ld[...], br[...], bi[...])
        d_ar, d_ai, d_ld, d_br, d_bi = vjp((g_lr[...], g_li[...], g_br[...], g_bi[...]))
        o_ar[...] = d_ar
        o_ai[...] = d_ai
        state = lax.broadcasted_iota(jnp.int32, (NST, LANES), 0)
        lane = lax.broadcasted_iota(jnp.int32, (NST, LANES), 1)
        in_group = jnp.where((state >= lane * SSM_N) & (state < (lane + 1) * SSM_N), 1.0, 0.0)
        o_ld[...] = jnp.dot(d_ld, in_group, precision=lax.Precision.HIGHEST, preferred_element_type=F32)
        o_br[...] = d_br
        o_bi[...] = d_bi

    row = jax.ShapeDtypeStruct((1, NST), F32)
    mat = jax.ShapeDtypeStruct((SSM_GC, NST), F32)
    grp = jax.ShapeDtypeStruct((1, LANES), F32)
    return pl.pallas_call(body, out_shape=[row, row, grp, mat, mat], name="s5_param_bwd")(
        a_re_row, a_im_row, ldt_row, b_re_t, b_im_t, d_lr, d_li, d_bbr, d_bbi)


def _scan_tiles(buf_ref, cst_ref, carry_ref, rows, reverse, reset, h_ref=None, acc_ref=None):
    n_tiles = rows // SUBLANES
    shifts = ((1, 0), (2, 2), (4, 4))
    row_id = lax.broadcasted_iota(jnp.int32, (SUBLANES, SCAN_COLS), 0)
    cols = [(j * SCAN_COLS, NST + j * SCAN_COLS) for j in range(NST // SCAN_COLS)]
    edge = 0 if reverse else SUBLANES - 1
    carry = jnp.where(reset, 0.0, carry_ref[...])
    accs = [(acc_ref[:, pl.ds(o_re, SCAN_COLS)], acc_ref[:, pl.ds(o_im, SCAN_COLS)]) for o_re, o_im in cols] if reverse else None
    for i in range(n_tiles):
        start = ((n_tiles - 1 - i) if reverse else i) * SUBLANES
        rws = pl.ds(start, SUBLANES)
        for j, (o_re, o_im) in enumerate(cols):
            c_re = pl.ds(o_re, SCAN_COLS)
            c_im = pl.ds(o_im, SCAN_COLS)
            xr = buf_ref[rws, c_re]
            xi = buf_ref[rws, c_im]
            for s, base in shifts:
                amount = (SUBLANES - s) if reverse else s
                sr = pltpu.roll(xr, amount, 0)
                si = pltpu.roll(xi, amount, 0)
                mr = cst_ref[pl.ds(base * SUBLANES, SUBLANES), c_re]
                mi = cst_ref[pl.ds((base + 1) * SUBLANES, SUBLANES), c_re]
                xr, xi = xr + (mr * sr - mi * si), xi + (mr * si + mi * sr)
            pr = cst_ref[pl.ds(6 * SUBLANES, SUBLANES), c_re]
            pi = cst_ref[pl.ds(7 * SUBLANES, SUBLANES), c_re]
            cr = carry[:, o_re:o_re + SCAN_COLS]
            ci = carry[:, o_im:o_im + SCAN_COLS]
            xr, xi = xr + (pr * cr - pi * ci), xi + (pr * ci + pi * cr)
            buf_ref[rws, c_re] = xr
            buf_ref[rws, c_im] = xi
            if reverse:
                gnr = jnp.where(row_id == SUBLANES - 1, cr, pltpu.roll(xr, SUBLANES - 1, 0))
                gni = jnp.where(row_id == SUBLANES - 1, ci, pltpu.roll(xi, SUBLANES - 1, 0))
                hr = h_ref[rws, c_re]
                hi = h_ref[rws, c_im]
                accs[j] = (accs[j][0] + (gnr * hr + gni * hi), accs[j][1] + (gni * hr - gnr * hi))
        carry = jnp.broadcast_to(buf_ref[pl.ds(start + edge, 1), :], (SUBLANES, 2 * NST))
    carry_ref[...] = carry
    if reverse:
        for (o_re, o_im), (a_re_, a_im_) in zip(cols, accs):
            acc_ref[:, pl.ds(o_re, SCAN_COLS)] = a_re_
            acc_ref[:, pl.ds(o_im, SCAN_COLS)] = a_im_


def _state_block(v, m):
    return jnp.concatenate([v[:, m * BLK_ST:(m + 1) * BLK_ST], v[:, NST + m * BLK_ST:NST + (m + 1) * BLK_ST]], axis=1)


def _put_state_block(buf_ref, m, val):
    buf_ref[:, pl.ds(m * BLK_ST, BLK_ST)] = val[:, :BLK_ST]
    buf_ref[:, pl.ds(NST + m * BLK_ST, BLK_ST)] = val[:, BLK_ST:]


def s5_fwd(proj, b_m, c_m, cst, d_skip, w_glu, nb, l):
    tt = min(256, l)
    nt = l // tt
    n_chunks = nb * nt

    def body(u_ref, b_ref, c_ref, cst_ref, ds_ref, wg_ref, y_ref, h_ref, yp_ref, z_ref,
             buf0, buf1, ub0, ub1, carry_ref):
        s = pl.program_id(0)

        @pl.when(s == 0)
        def _():
            for r in (buf0, buf1, ub0, ub1, carry_ref):
                r[...] = jnp.zeros_like(r)

        def step(p_buf, p_u, q_buf):
            h = p_buf[...].astype(BF16)
            h_ref[...] = h
            ypre = (jnp.concatenate([_dot(_state_block(h, m), c_ref[m]) for m in range(N_BLK)], axis=1)
                    + ds_ref[...] * p_u[...])
            yp_ref[...] = ypre
            z = _dot(jax.nn.gelu(ypre).astype(BF16), wg_ref[...])
            z_ref[...] = z
            y_ref[...] = (z[:, :SSM_W] * jax.nn.sigmoid(z[:, SSM_W:])).astype(BF16)
            ub = u_ref[...]
            p_u[...] = ub.astype(F32)
            for m in range(N_BLK):
                _put_state_block(p_buf, m, _dot(ub[:, m * BLK_CH:(m + 1) * BLK_CH], b_ref[m]))
            _scan_tiles(q_buf, cst_ref, carry_ref, tt, False, lax.rem(s + nt - 1, nt) == 0)

        @pl.when(lax.rem(s, 2) == 0)
        def _():
            step(buf0, ub0, buf1)

        @pl.when(lax.rem(s, 2) == 1)
        def _():
            step(buf1, ub1, buf0)

    row_in = lambda s: (jnp.minimum(s, n_chunks - 1), 1)
    row_out = lambda s: (jnp.maximum(s - 2, 0), 0)
    const = lambda s: (0, 0)
    const3 = lambda s: (0, 0, 0)
    return pl.pallas_call(
        body, grid=(n_chunks + 2,),
        in_specs=[pl.BlockSpec((tt, SSM_W), row_in),
                  pl.BlockSpec((N_BLK, BLK_CH, 2 * BLK_ST), const3),
                  pl.BlockSpec((N_BLK, 2 * BLK_ST, BLK_CH), const3),
                  pl.BlockSpec((8 * SUBLANES, NST), const),
                  pl.BlockSpec((1, SSM_W), const),
                  pl.BlockSpec((SSM_W, 2 * SSM_W), const)],
        out_specs=[pl.BlockSpec((tt, SSM_W), row_out),
                   pl.BlockSpec((tt, 2 * NST), row_out),
                   pl.BlockSpec((tt, SSM_W), row_out),
                   pl.BlockSpec((tt, 2 * SSM_W), row_out)],
        out_shape=[jax.ShapeDtypeStruct((nb * l, SSM_W), BF16),
                   jax.ShapeDtypeStruct((nb * l, 2 * NST), BF16),
                   jax.ShapeDtypeStruct((nb * l, SSM_W), F32),
                   jax.ShapeDtypeStruct((nb * l, 2 * SSM_W), F32)],
        scratch_shapes=[pltpu.VMEM((tt, 2 * NST), F32), pltpu.VMEM((tt, 2 * NST), F32),
                        pltpu.VMEM((tt, SSM_W), F32), pltpu.VMEM((tt, SSM_W), F32),
                        pltpu.VMEM((SUBLANES, 2 * NST), F32)],
        compiler_params=_params("arbitrary"), name="s5_fwd")(proj, b_m, c_m, cst, d_skip, w_glu)


def s5_bwd(proj, d_ycat, h, ypre, z, b_m, c_m, cst_rev, d_skip, w_glu, nb, l):
    tt = min(256, l)
    nt = l // tt
    n_chunks = nb * nt

    def body(u_ref, dy_ref, h_ref, yp_ref, z_ref, b_ref, c_ref, cst_ref, ds_ref, wg_ref,
             du_ref, dwg_ref, dds_ref, dcc_ref, dbb_ref, dlam_ref,
             buf0, buf1, hb0, hb1, ub0, ub1, dyp0, dyp1, carry_ref, dc_ref, db_ref):
        s = pl.program_id(0)

        @pl.when(s == 0)
        def _():
            for r in (buf0, buf1, hb0, hb1, ub0, ub1, dyp0, dyp1, carry_ref, dc_ref, db_ref, dwg_ref, dds_ref, dlam_ref):
                r[...] = jnp.zeros_like(r)

        def step(p_buf, p_h, p_u, p_dyp, q_buf, q_h):
            g = p_buf[...].astype(BF16)
            ub_done = p_u[...].astype(BF16)
            du = []
            for m in range(N_BLK):
                g_m = _state_block(g, m)
                db_ref[m] += _dot_tn(g_m, ub_done[:, m * BLK_CH:(m + 1) * BLK_CH])
                du.append(_dot_nt(g_m, b_ref[m]))
            du_ref[...] = (jnp.concatenate(du, axis=1) + ds_ref[...] * p_dyp[...]).astype(BF16)
            u = u_ref[...].astype(F32)
            ypre = yp_ref[...]
            z = z_ref[...]
            z1 = z[:, :SSM_W]
            sg = jax.nn.sigmoid(z[:, SSM_W:])
            dy = dy_ref[...].astype(F32) * jnp.where(s < n_chunks, 1.0, 0.0)
            dz = jnp.concatenate([dy * sg, dy * z1 * sg * (1.0 - sg)], axis=1).astype(BF16)
            yg, gelu_vjp = jax.vjp(jax.nn.gelu, ypre)
            dwg_ref[...] += _dot_tn(yg.astype(BF16), dz)
            dypre = gelu_vjp(_dot_nt(dz, wg_ref[...]))[0]
            dds_ref[...] += jnp.sum(dypre * u, axis=0, keepdims=True)
            dyb = dypre.astype(BF16)
            hb = h_ref[...]
            p_h[...] = hb.astype(F32)
            p_u[...] = u
            p_dyp[...] = dypre
            for m in range(N_BLK):
                dy_m = dyb[:, m * BLK_CH:(m + 1) * BLK_CH]
                dc_ref[m] += _dot_tn(_state_block(hb, m), dy_m)
                _put_state_block(p_buf, m, _dot_nt(dy_m, c_ref[m]))
            _scan_tiles(q_buf, cst_ref, carry_ref, tt, True, lax.rem(s + nt - 1, nt) == 0, h_ref=q_h, acc_ref=dlam_ref)

        @pl.when(lax.rem(s, 2) == 0)
        def _():
            step(buf0, hb0, ub0, dyp0, buf1, hb1)

        @pl.when(lax.rem(s, 2) == 1)
        def _():
            step(buf1, hb1, ub1, dyp1, buf0, hb0)

        @pl.when(s == n_chunks + 1)
        def _():
            dlam_ref[...] = jnp.broadcast_to(jnp.sum(dlam_ref[...], axis=0, keepdims=True), dlam_ref.shape)
            for acc_ref, out_ref in ((dc_ref, dcc_ref), (db_ref, dbb_ref)):
                for m in range(N_BLK):
                    for ri in range(2):
                        for gl in range(BLK_GROUPS):
                            out_ref[ri, pl.ds((m * BLK_GROUPS + gl) * SSM_N, SSM_N), :] = acc_ref[
                                m, pl.ds(ri * BLK_ST + gl * SSM_N, SSM_N), pl.ds(gl * SSM_GC, SSM_GC)]

    def chunk_rows(o):
        return lax.div(o, nt) * nt + (nt - 1 - lax.rem(o, nt))

    def rrow(col):
        return lambda s: (chunk_rows(jnp.minimum(s, n_chunks - 1)), col)

    const = lambda s: (0, 0)
    const3 = lambda s: (0, 0, 0)
    state_buf = pltpu.VMEM((tt, 2 * NST), F32)
    chan_buf = pltpu.VMEM((tt, SSM_W), F32)
    return pl.pallas_call(
        body, grid=(n_chunks + 2,),
        in_specs=[pl.BlockSpec((tt, SSM_W), rrow(1)),
                  pl.BlockSpec((tt, SSM_W), rrow(1)),
                  pl.BlockSpec((tt, 2 * NST), rrow(0)),
                  pl.BlockSpec((tt, SSM_W), rrow(0)),
                  pl.BlockSpec((tt, 2 * SSM_W), rrow(0)),
                  pl.BlockSpec((N_BLK, BLK_CH, 2 * BLK_ST), const3),
                  pl.BlockSpec((N_BLK, 2 * BLK_ST, BLK_CH), const3),
                  pl.BlockSpec((8 * SUBLANES, NST), const),
                  pl.BlockSpec((1, SSM_W), const),
                  pl.BlockSpec((SSM_W, 2 * SSM_W), const)],
        out_specs=[pl.BlockSpec((tt, SSM_W), lambda s: (chunk_rows(jnp.maximum(s - 2, 0)), 0)),
                   pl.BlockSpec((SSM_W, 2 * SSM_W), const),
                   pl.BlockSpec((1, SSM_W), const),
                   pl.BlockSpec((2, NST, SSM_GC), const3),
                   pl.BlockSpec((2, NST, SSM_GC), const3),
                   pl.BlockSpec((SUBLANES, 2 * NST), const)],
        out_shape=[jax.ShapeDtypeStruct((nb * l, SSM_W), BF16),
                   jax.ShapeDtypeStruct((SSM_W, 2 * SSM_W), F32),
                   jax.ShapeDtypeStruct((1, SSM_W), F32),
                   jax.ShapeDtypeStruct((2, NST, SSM_GC), F32),
                   jax.ShapeDtypeStruct((2, NST, SSM_GC), F32),
                   jax.ShapeDtypeStruct((SUBLANES, 2 * NST), F32)],
        scratch_shapes=[state_buf, state_buf, state_buf, state_buf, chan_buf, chan_buf, chan_buf, chan_buf,
                        pltpu.VMEM((SUBLANES, 2 * NST), F32),
                        pltpu.VMEM((N_BLK, 2 * BLK_ST, BLK_CH), F32), pltpu.VMEM((N_BLK, 2 * BLK_ST, BLK_CH), F32)],
        compiler_params=_params("arbitrary"), name="s5_bwd")(
            proj, d_ycat, h, ypre, z, b_m, c_m, cst_rev, d_skip, w_glu)


def _head_mask(hh):
    lane = lax.broadcasted_iota(jnp.int32, (1, ATT_W), 1)
    return (lane >= hh * MEM_HD) & (lane < (hh + 1) * MEM_HD)


def _mem_kv(mem_ref, gm_ref, wkv_ref):
    m = mem_ref[0]
    mh = m * _rms_scale(m)
    mb = (mh * gm_ref[...]).astype(BF16)
    kv = _dot(mb, wkv_ref[...])
    return mh, mb, kv[:, :ATT_W].astype(BF16), kv[:, ATT_W:].astype(BF16)


def _stack_heads(dst_ref, a):
    for hh in range(MEM_HEADS):
        dst_ref[pl.ds(hh * N_MEM, N_MEM), :] = jnp.where(_head_mask(hh), a, jnp.zeros_like(a))


def _fold_heads(a):
    out = jnp.zeros((N_MEM, ATT_W), a.dtype)
    for hh in range(MEM_HEADS):
        out = out + jnp.where(_head_mask(hh), a[hh * N_MEM:(hh + 1) * N_MEM], 0.0)
    return out


def _softmax_heads(s):
    parts = []
    for hh in range(MEM_HEADS):
        sh = s[:, hh * N_MEM:(hh + 1) * N_MEM]
        e = jnp.exp(sh - jnp.max(sh, axis=-1, keepdims=True))
        parts.append(e / jnp.sum(e, axis=-1, keepdims=True))
    return jnp.concatenate(parts, axis=1)


def attn_fwd(proj, mem, g_mem, w_kv, nb, l):
    tq = min(512, l)
    nq = l // tq
    scale = MEM_HD ** -0.5

    def body(q_ref, mem_ref, gm_ref, wkv_ref, o_ref, k_s, v_s):
        @pl.when(pl.program_id(1) == 0)
        def _():
            _, _, k, v = _mem_kv(mem_ref, gm_ref, wkv_ref)
            _stack_heads(k_s, k)
            _stack_heads(v_s, v)

        p = _softmax_heads(_dot_nt(q_ref[...], k_s[...]) * scale)
        o_ref[...] = _dot(p.astype(BF16), v_s[...]).astype(BF16)

    const = lambda b, t: (0, 0)
    stacked = pltpu.VMEM((MEM_HEADS * N_MEM, ATT_W), BF16)
    return pl.pallas_call(
        body, grid=(nb, nq),
        in_specs=[pl.BlockSpec((tq, ATT_W), lambda b, t: (b * nq + t, 3)),
                  pl.BlockSpec((1, N_MEM, D_MODEL), lambda b, t: (b, 0, 0)),
                  pl.BlockSpec((1, D_MODEL), const),
                  pl.BlockSpec((D_MODEL, 2 * ATT_W), const)],
        out_specs=pl.BlockSpec((tq, ATT_W), lambda b, t: (b * nq + t, 0)),
        out_shape=jax.ShapeDtypeStruct((nb * l, ATT_W), BF16),
        scratch_shapes=[stacked, stacked],
        compiler_params=_params("arbitrary", "arbitrary"), name="attn_fwd")(proj, mem, g_mem, w_kv)


def attn_bwd(proj, d_ycat, mem, g_mem, w_kv, nb, l):
    tq = min(512, l)
    nq = l // tq
    scale = MEM_HD ** -0.5

    def body(q_ref, do_ref, mem_ref, gm_ref, wkv_ref, dq_ref, dwkv_ref, dgm_ref, k_s, v_s, mb_s, dk_s, dv_s):
        b = pl.program_id(0)
        t = pl.program_id(1)

        @pl.when((b == 0) & (t == 0))
        def _():
            dwkv_ref[...] = jnp.zeros_like(dwkv_ref)
            dgm_ref[...] = jnp.zeros_like(dgm_ref)

        @pl.when(t == 0)
        def _():
            _, mb, k, v = _mem_kv(mem_ref, gm_ref, wkv_ref)
            _stack_heads(k_s, k)
            _stack_heads(v_s, v)
            mb_s[...] = mb
            dk_s[...] = jnp.zeros_like(dk_s)
            dv_s[...] = jnp.zeros_like(dv_s)

        q = q_ref[...]
        do = do_ref[...]
        k = k_s[...]
        p = _softmax_heads(_dot_nt(q, k) * scale)
        dp = _dot_nt(do, v_s[...])
        ds = []
        for hh in range(MEM_HEADS):
            blk = slice(hh * N_MEM, (hh + 1) * N_MEM)
            ds.append(p[:, blk] * (dp[:, blk] - jnp.sum(dp[:, blk] * p[:, blk], axis=-1, keepdims=True)) * scale)
        ds = jnp.concatenate(ds, axis=1).astype(BF16)
        dq_ref[...] = _dot(ds, k).astype(BF16)
        dk_s[...] += _dot_tn(ds, q)
        dv_s[...] += _dot_tn(p.astype(BF16), do)

        @pl.when(t == nq - 1)
        def _():
            dkv = jnp.concatenate([_fold_heads(dk_s[...]), _fold_heads(dv_s[...])], axis=1).astype(BF16)
            dwkv_ref[...] += _dot_tn(mb_s[...], dkv)
            m = mem_ref[0]
            dgm_ref[...] += jnp.sum(_dot_nt(dkv, wkv_ref[...]) * (m * _rms_scale(m)), axis=0, keepdims=True)

    const = lambda b, t: (0, 0)
    return pl.pallas_call(
        body, grid=(nb, nq),
        in_specs=[pl.BlockSpec((tq, ATT_W), lambda b, t: (b * nq + t, 3)),
                  pl.BlockSpec((tq, ATT_W), lambda b, t: (b * nq + t, 3)),
                  pl.BlockSpec((1, N_MEM, D_MODEL), lambda b, t: (b, 0, 0)),
                  pl.BlockSpec((1, D_MODEL), const),
                  pl.BlockSpec((D_MODEL, 2 * ATT_W), const)],
        out_specs=[pl.BlockSpec((tq, ATT_W), lambda b, t: (b * nq + t, 0)),
                   pl.BlockSpec((D_MODEL, 2 * ATT_W), const),
                   pl.BlockSpec((1, D_MODEL), const)],
        out_shape=[jax.ShapeDtypeStruct((nb * l, ATT_W), BF16),
                   jax.ShapeDtypeStruct((D_MODEL, 2 * ATT_W), F32),
                   jax.ShapeDtypeStruct((1, D_MODEL), F32)],
        scratch_shapes=[pltpu.VMEM((MEM_HEADS * N_MEM, ATT_W), BF16), pltpu.VMEM((MEM_HEADS * N_MEM, ATT_W), BF16),
                        pltpu.VMEM((N_MEM, D_MODEL), BF16),
                        pltpu.VMEM((MEM_HEADS * N_MEM, ATT_W), F32), pltpu.VMEM((MEM_HEADS * N_MEM, ATT_W), F32)],
        compiler_params=_params("arbitrary", "arbitrary"), name="attn_bwd")(proj, d_ycat, mem, g_mem, w_kv)


def tail(y_pool, y_ssm, y_att, proj, x2, target, w_out, g_post):
    t = x2.shape[0]
    tm = min(ROW_TILE, t)

    def body(yp_ref, ys_ref, ya_ref, gate_ref, x_ref, tg_ref, w_ref, gp_ref,
             dz_ref, dyc_ref, dgate_ref, dw_ref, dgp_ref, loss_ref):
        @pl.when(pl.program_id(0) == 0)
        def _():
            dw_ref[...] = jnp.zeros_like(dw_ref)
            dgp_ref[...] = jnp.zeros_like(dgp_ref)
            loss_ref[...] = jnp.zeros_like(loss_ref)

        ycat = jnp.concatenate([yp_ref[...], ys_ref[...], ya_ref[...]], axis=1).astype(F32)
        gate = gate_ref[...].astype(F32)
        sg = jax.nn.sigmoid(gate)
        silu = gate * sg
        yb = (ycat * silu).astype(BF16)
        w = w_ref[...]
        out = _dot(yb, w)
        r2 = _rms_scale(out)
        oh = out * r2
        gp = gp_ref[...]
        err = (x_ref[...] + oh * gp) - tg_ref[...]
        loss_ref[...] += 0.5 * jnp.sum(jnp.mean(err * err, axis=-1, keepdims=True), axis=0, keepdims=True)
        dz = err * (1.0 / D_MODEL)
        dz_ref[...] = dz.astype(BF16)
        dgp_ref[...] += jnp.sum(dz * oh, axis=0, keepdims=True)
        dn = dz * gp
        dout = (r2 * (dn - oh * jnp.mean(dn * oh, axis=-1, keepdims=True))).astype(BF16)
        dw_ref[...] += _dot_tn(yb, dout)
        dy = _dot_nt(dout, w)
        dyc_ref[...] = (dy * silu).astype(BF16)
        dgate_ref[...] = (dy * ycat * (sg * (1.0 + gate * (1.0 - sg)))).astype(BF16)

    row = lambda i: (i, 0)
    const = lambda i: (0, 0)
    full = jax.ShapeDtypeStruct((t, D_MODEL), BF16)
    return pl.pallas_call(
        body, grid=(t // tm,),
        in_specs=[pl.BlockSpec((tm, POOL_W), row),
                  pl.BlockSpec((tm, SSM_W), row),
                  pl.BlockSpec((tm, ATT_W), row),
                  pl.BlockSpec((tm, D_MODEL), lambda i: (i, 1)),
                  pl.BlockSpec((tm, D_MODEL), row),
                  pl.BlockSpec((tm, D_MODEL), row),
                  pl.BlockSpec((D_MODEL, D_MODEL), const),
                  pl.BlockSpec((1, D_MODEL), const)],
        out_specs=[pl.BlockSpec((tm, D_MODEL), row),
                   pl.BlockSpec((tm, D_MODEL), row),
                   pl.BlockSpec((tm, D_MODEL), row),
                   pl.BlockSpec((D_MODEL, D_MODEL), const),
                   pl.BlockSpec((1, D_MODEL), const),
                   pl.BlockSpec((1, LANES), const)],
        out_shape=[full, full, full,
                   jax.ShapeDtypeStruct((D_MODEL, D_MODEL), F32),
                   jax.ShapeDtypeStruct((1, D_MODEL), F32),
                   jax.ShapeDtypeStruct((1, LANES), F32)],
        compiler_params=_params("arbitrary"), name="tail")(y_pool, y_ssm, y_att, proj, x2, target, w_out, g_post)


def in_proj_bwd(du_pool, du_ssm, dq, d_gate, x2, dz, g_pre, w_in):
    t = x2.shape[0]
    tm = min(ROW_TILE, t)
    shard = W_IN_SHARD

    def body(dup_ref, dus_ref, dq_ref, dg_ref, x_ref, dz_ref, g_ref, w_ref, gx_ref, dw_ref, dgp_ref):
        @pl.when(pl.program_id(0) == 0)
        def _():
            dw_ref[...] = jnp.zeros_like(dw_ref)
            dgp_ref[...] = jnp.zeros_like(dgp_ref)

        dproj = jnp.concatenate([dup_ref[...], dus_ref[...], dq_ref[...], dg_ref[...]], axis=1)
        x = x_ref[...]
        r1 = _rms_scale(x)
        xh = x * r1
        g = g_ref[...]
        hb = (xh * g).astype(BF16)
        dh = jnp.zeros((tm, D_MODEL), F32)
        for k in range(N_CHIPS):
            dp_k = dproj[:, k * shard:(k + 1) * shard]
            dw_ref[k] += _dot_tn(hb, dp_k)
            dh = dh + _dot_nt(dp_k, w_ref[k])
        dgp_ref[...] += jnp.sum(dh * xh, axis=0, keepdims=True)
        dn = dh * g
        gx_ref[...] = dz_ref[...].astype(F32) + r1 * (dn - xh * jnp.mean(dn * xh, axis=-1, keepdims=True))

    row = lambda i: (i, 0)
    const = lambda i: (0, 0)
    return pl.pallas_call(
        body, grid=(t // tm,),
        in_specs=[pl.BlockSpec((tm, POOL_W), row),
                  pl.BlockSpec((tm, SSM_W), row),
                  pl.BlockSpec((tm, ATT_W), row),
                  pl.BlockSpec((tm, D_MODEL), row),
                  pl.BlockSpec((tm, D_MODEL), row),
                  pl.BlockSpec((tm, D_MODEL), row),
                  pl.BlockSpec((1, D_MODEL), const),
                  pl.BlockSpec((N_CHIPS, D_MODEL, shard), lambda i: (0, 0, 0))],
        out_specs=[pl.BlockSpec((tm, D_MODEL), row),
                   pl.BlockSpec((N_CHIPS, D_MODEL, shard), lambda i: (0, 0, 0)),
                   pl.BlockSpec((1, D_MODEL), const)],
        out_shape=[jax.ShapeDtypeStruct((t, D_MODEL), F32),
                   jax.ShapeDtypeStruct((N_CHIPS, D_MODEL, shard), F32),
                   jax.ShapeDtypeStruct((1, D_MODEL), F32)],
        compiler_params=_params("arbitrary"), name="in_proj_bwd")(du_pool, du_ssm, dq, d_gate, x2, dz, g_pre, w_in)


def _block_diag(blocks):
    g, r, c = blocks.shape
    eye = jnp.eye(g, dtype=blocks.dtype)
    return jnp.einsum("grc,gh->grhc", blocks, eye).reshape(g * r, g * c)


def _block_diag_extract(mat, g, r, c):
    eye = jnp.eye(g, dtype=mat.dtype)
    return jnp.einsum("grhc,gh->grc", mat.reshape(g, r, g, c), eye)


def local_step(x, mem, target, g_pre, w_in_shard, w_pool, pool_scale, a_re, a_im, log_dt, b_re, b_im, c_re, c_im,
               d_skip, later_shards, g_mem, g_post):
    nb, l, _ = x.shape
    x2 = x.reshape(nb * l, D_MODEL)
    tg2 = target.reshape(nb * l, D_MODEL)

    rowv = lambda a: a.reshape(1, NST)
    lanes_last = lambda b: b.transpose(2, 0, 1).reshape(SSM_GC, NST)
    ldt_row = rowv(jnp.broadcast_to(log_dt.reshape(SSM_NG, 1), (SSM_NG, SSM_N)))
    b_re_t = lanes_last(b_re)
    b_im_t = lanes_last(b_im)
    cst_f, cst_b = s5_scan_consts(rowv(a_re), rowv(a_im), ldt_row)
    bbr, bbi = s5_bbar(rowv(a_re), rowv(a_im), ldt_row, b_re_t, b_im_t)
    eye = jnp.eye(BLK_GROUPS, dtype=F32)
    blk_in = lambda bb: jnp.einsum("cmgn,gh->mhcgn", bb.reshape(SSM_GC, N_BLK, BLK_GROUPS, SSM_N), eye).reshape(
        N_BLK, BLK_CH, BLK_ST)
    blk_out = lambda cc: jnp.einsum("mgcn,gh->mgnhc", cc.reshape(N_BLK, BLK_GROUPS, SSM_GC, SSM_N), eye).reshape(
        N_BLK, BLK_ST, BLK_CH)
    b_m = jnp.concatenate([blk_in(bbr), blk_in(bbi)], axis=2).astype(BF16)
    c_m = jnp.concatenate([blk_out(c_re), -blk_out(c_im)], axis=1).astype(BF16)
    w_pool_blk = _block_diag(w_pool.reshape(4, POOL_GW, POOL_GW)).astype(BF16)

    proj, w_in, (g_glu, g_kv, g_out) = in_proj(x2, g_pre, w_in_shard, later_shards)
    w_glu = _from_chip_major(g_glu)
    w_kv = g_kv.reshape(D_MODEL, 2 * ATT_W)
    w_out = g_out.reshape(D_MODEL, D_MODEL)
    y_pool = pool_fwd(proj, w_pool_blk, pool_scale, nb, l)
    y_ssm, h, ypre, z = s5_fwd(proj, b_m, c_m, cst_f, d_skip, w_glu, nb, l)
    y_att = attn_fwd(proj, mem, g_mem, w_kv, nb, l)
    dz, d_ycat, d_gate, d_w_out, d_g_post, loss = tail(y_pool, y_ssm, y_att, proj, x2, tg2, w_out, g_post)
    du_pool, d_wp_blk, d_pool_scale = pool_bwd(proj, d_ycat, w_pool_blk, pool_scale, nb, l)
    du_ssm, d_w_glu, d_d_skip, d_cc, d_bb, d_lam = s5_bwd(
        proj, d_ycat, h, ypre, z, b_m, c_m, cst_b, d_skip, w_glu, nb, l)
    dq, d_w_kv, d_g_mem = attn_bwd(proj, d_ycat, mem, g_mem, w_kv, nb, l)
    grad_x, d_w_in, d_g_pre = in_proj_bwd(du_pool, du_ssm, dq, d_gate, x2, dz, g_pre, w_in)

    d_c = d_cc[0].reshape(SSM_NG, SSM_N, SSM_GC)
    d_ci = d_cc[1].reshape(SSM_NG, SSM_N, SSM_GC)
    d_lam_row = d_lam[0]
    d_ar, d_ai, d_ld, d_br, d_bi = s5_param_bwd(
        rowv(a_re), rowv(a_im), ldt_row, b_re_t, b_im_t,
        d_lam_row[:NST].reshape(1, NST), d_lam_row[NST:].reshape(1, NST), d_bb[0].T, d_bb[1].T)
    from_lanes_last = lambda b: b.reshape(SSM_GC, SSM_NG, SSM_N).transpose(1, 2, 0).reshape(1, SSM_NG, SSM_N, SSM_GC)

    grads = {
        "g_pre": d_g_pre,
        "w_in": d_w_in,
        "w_pool": _block_diag_extract(d_wp_blk, 4, POOL_GW, POOL_GW).reshape(1, 4, POOL_GW, POOL_GW),
        "pool_scale": d_pool_scale,
        "a_re": d_ar.reshape(1, SSM_NG, SSM_N),
        "a_im": d_ai.reshape(1, SSM_NG, SSM_N),
        "log_dt": d_ld[:, :SSM_NG],
        "b_re": from_lanes_last(d_br),
        "b_im": from_lanes_last(d_bi),
        "c_re": d_c.transpose(0, 2, 1).reshape(1, SSM_NG, SSM_GC, SSM_N),
        "c_im": (-d_ci).transpose(0, 2, 1).reshape(1, SSM_NG, SSM_GC, SSM_N),
        "d_skip": d_d_skip,
        "w_glu": d_w_glu,
        "g_mem": d_g_mem,
        "w_kv": d_w_kv,
        "w_out": d_w_out,
        "g_post": d_g_post,
    }
    return loss, grad_x.reshape(nb, l, D_MODEL), grads


VMEM_SPEC = pl.BlockSpec(memory_space=pltpu.VMEM)
ANY_SPEC = pl.BlockSpec(memory_space=pl.ANY)


def _place():
    return lax.axis_index("x"), lax.axis_index("y"), lax.axis_index("c")


def _other_chips(x, y):
    return [(1 - x, y), (x, 1 - y), (1 - x, 1 - y)]


SEMS_PER_ITEM = 5


def reduce_all(g4s, packed):
    n = len(g4s)
    dims = [g.shape[1:] for g in g4s]
    pr = packed.shape[0]

    def body(*refs):
        g_refs, p_ref = refs[:n], refs[n]
        outs, op = refs[n + 1:2 * n + 1], refs[2 * n + 1]
        scr = refs[2 * n + 2:]
        mine, theirs, sendb, recvb = scr[0:n], scr[n:2 * n], scr[2 * n:3 * n], scr[3 * n:4 * n]
        p_theirs, gat, lsems, send_sems, recv_sems = scr[4 * n:]
        x, y, c = _place()
        k = 2 * x + y
        chips = _other_chips(x, y)
        sib = (x, y, 1 - c)

        def halves(rows):
            hr = rows // 2
            return (pl.ds(pl.multiple_of(c * hr, SUBLANES), hr), pl.ds(pl.multiple_of((1 - c) * hr, SUBLANES), hr))

        def remote(src, dst, sem, to):
            return pltpu.make_async_remote_copy(
                src_ref=src, dst_ref=dst, send_sem=send_sems.at[sem], recv_sem=recv_sems.at[sem],
                device_id=to, device_id_type=MESH)

        loads, started = [], []
        for i in range(n):
            my_rows, sib_rows = halves(dims[i][0])
            ld = pltpu.make_async_copy(g_refs[i].at[:, my_rows, :], mine[i], lsems.at[i])
            ld.start()
            loads.append(ld)
            sw = remote(g_refs[i].at[:, sib_rows, :], theirs[i], SEMS_PER_ITEM * i, sib)
            sw.start()
            started.append(sw)
        p_my, p_sib = halves(pr)
        p_swap = remote(p_ref.at[p_sib, :], p_theirs, SEMS_PER_ITEM * n, sib)
        p_swap.start()
        started.append(p_swap)

        p_swap.wait_recv()
        gat[k] = p_ref[p_my, :] + p_theirs[...]
        for j, (px, py) in enumerate(chips):
            cp = remote(gat.at[k], gat.at[k], SEMS_PER_ITEM * n + 1 + j, (px, py, c))
            cp.start()
            started.append(cp)
        ici = []
        for i in range(n):
            loads[i].wait()
            started[i].wait_recv()
            for j, (px, py) in enumerate(chips):
                kp = 2 * px + py
                sendb[i][j] = (mine[i][kp] + theirs[i][kp]).astype(BF16)
                cp = remote(sendb[i].at[j], recvb[i].at[j], SEMS_PER_ITEM * i + 1 + j, (px, py, c))
                cp.start()
                ici.append(cp)

        for j, (px, py) in enumerate(chips):
            kp = 2 * px + py
            remote(gat.at[kp], gat.at[kp], SEMS_PER_ITEM * n + 1 + j, (px, py, c)).wait_recv()
        op[p_my, :] = ((gat[0] + gat[1]) + gat[2]) + gat[3]
        last = remote(op.at[p_my, :], op.at[p_my, :], SEMS_PER_ITEM * n + 4, sib)
        last.start()
        started.append(last)
        for i in range(n):
            my_rows, _ = halves(dims[i][0])
            half = mine[i][k] + theirs[i][k]
            for j in range(3):
                ici[3 * i + j].wait_recv()
                half = half + recvb[i][j].astype(F32)
            outs[i][my_rows, :] = half
            last = remote(outs[i].at[my_rows, :], outs[i].at[my_rows, :], SEMS_PER_ITEM * i + 4, sib)
            last.start()
            started.append(last)
        remote(op.at[p_sib, :], op.at[p_sib, :], SEMS_PER_ITEM * n + 4, sib).wait_recv()
        for i in range(n):
            _, sib_rows = halves(dims[i][0])
            remote(outs[i].at[sib_rows, :], outs[i].at[sib_rows, :], SEMS_PER_ITEM * i + 4, sib).wait_recv()
        for cp in started + ici:
            cp.wait_send()

    n_sems = SEMS_PER_ITEM * (n + 1)
    scratch = ([pltpu.VMEM((N_CHIPS, r // 2, cd), F32) for r, cd in dims] * 2
               + [pltpu.VMEM((3, r // 2, cd), BF16) for r, cd in dims] * 2
               + [pltpu.VMEM((pr // 2, LANES), F32), pltpu.VMEM((N_CHIPS, pr // 2, LANES), F32),
                  pltpu.SemaphoreType.DMA((n,)), pltpu.SemaphoreType.DMA((n_sems,)), pltpu.SemaphoreType.DMA((n_sems,))])
    res = pl.pallas_call(
        body,
        out_shape=[jax.ShapeDtypeStruct(d, F32) for d in dims] + [jax.ShapeDtypeStruct(packed.shape, F32)],
        in_specs=[ANY_SPEC] * n + [VMEM_SPEC], out_specs=[VMEM_SPEC] * (n + 1),
        scratch_shapes=scratch,
        compiler_params=pltpu.CompilerParams(vmem_limit_bytes=VMEM_LIMIT),
        name="reduce_all")(*g4s, packed)
    return res[:n], res[n]


def adamw_all(ws, gs, ms, vs):
    n = len(ws)

    def body(*refs):
        w, g, m, v = refs[:n], refs[n:2 * n], refs[2 * n:3 * n], refs[3 * n:4 * n]
        od, om, ov = refs[4 * n:5 * n], refs[5 * n:6 * n], refs[6 * n:]
        for i in range(n):
            od[i][...], om[i][...], ov[i][...] = _adamw(w[i][...], g[i][...], m[i][...], v[i][...])

    shapes = [jax.ShapeDtypeStruct(a.shape, F32) for a in ws]
    res = pl.pallas_call(
        body, out_shape=shapes * 3, in_specs=[VMEM_SPEC] * (4 * n), out_specs=[VMEM_SPEC] * (3 * n),
        compiler_params=pltpu.CompilerParams(vmem_limit_bytes=VMEM_LIMIT),
        name="adamw_all")(*ws, *gs, *ms, *vs)
    return res[:n], res[n:2 * n], res[2 * n:]


WEIGHTS = ("g_pre", "w_in", "w_pool", "pool_scale", "a_re", "a_im", "log_dt", "b_re", "b_im", "c_re", "c_im", "d_skip",
           "w_glu", "g_mem", "w_kv", "w_out", "g_post")
SHARDED = ("w_in", "w_glu", "w_kv", "w_out")
REPLICATED = tuple(n for n in WEIGHTS if n not in SHARDED)
PACK_TILE = SUBLANES * LANES


def _pack(arrays):
    parts = []
    for a in arrays:
        flat = a.reshape(-1)
        parts.append(jnp.pad(flat, (0, -flat.shape[0] % PACK_TILE)).reshape(-1, LANES))
    rows = sum(p.shape[0] for p in parts)
    if rows % (2 * SUBLANES):
        parts.append(jnp.zeros((SUBLANES, LANES), F32))
    return jnp.concatenate(parts, axis=0)


def _unpack(packed, shapes):
    out, row = [], 0
    for shp in shapes:
        size = math.prod(shp)
        rows = -(-size // PACK_TILE) * SUBLANES
        out.append(packed[row:row + rows].reshape(-1)[:size].reshape(shp))
        row += rows
    return out


def _chip_major(a, n_cols):
    return a.reshape(a.shape[0], N_CHIPS, n_cols).transpose(1, 0, 2)


def _from_chip_major(a):
    return a.transpose(1, 0, 2).reshape(a.shape[1], N_CHIPS * a.shape[2])


def kernel(x, mem, g_pre, w_in, w_pool, pool_scale, a_re, a_im, log_dt, b_re, b_im, c_re, c_im, d_skip, w_glu, g_mem, w_kv, w_out, g_post, loss_target, m_g_pre, m_w_in, m_w_pool, m_pool_scale, m_a_re, m_a_im, m_log_dt, m_b_re, m_b_im, m_c_re, m_c_im, m_d_skip, m_w_glu, m_g_mem, m_w_kv, m_w_out, m_g_post, v_g_pre, v_w_in, v_w_pool, v_pool_scale, v_a_re, v_a_im, v_log_dt, v_b_re, v_b_im, v_c_re, v_c_im, v_d_skip, v_w_glu, v_g_mem, v_w_kv, v_w_out, v_g_post):
    given = dict(locals())
    wts = {n: given[n] for n in WEIGHTS}
    mom = {n: given["m_" + n] for n in WEIGHTS}
    var = {n: given["v_" + n] for n in WEIGHTS}

    loss_part, grad_x, grads = local_step(
        x, mem, loss_target, g_pre, w_in[0], w_pool[0], pool_scale, a_re[0], a_im[0], log_dt[0], b_re[0], b_im[0],
        c_re[0], c_im[0], d_skip, [w_glu[0], w_kv[0], w_out[0]], g_mem, g_post)

    partial4 = [
        grads["w_in"],
        _chip_major(grads["w_glu"], 2 * SSM_W // N_CHIPS),
        grads["w_kv"].reshape(N_CHIPS, D_MODEL // N_CHIPS, 2 * ATT_W),
        grads["w_out"].reshape(N_CHIPS, D_MODEL // N_CHIPS, D_MODEL),
    ]
    loss_tile = jnp.broadcast_to(loss_part, (SUBLANES, LANES))
    sharded_g, packed_g = reduce_all(partial4, _pack([grads[n] for n in REPLICATED] + [loss_tile]))
    small_g = _unpack(packed_g, [wts[n].shape for n in REPLICATED] + [(SUBLANES, LANES)])
    loss = small_g[-1][0, 0]
    grad = dict(zip(SHARDED, [g[None] for g in sharded_g]))
    grad.update(zip(REPLICATED, small_g[:-1]))

    deltas, new_ms, new_vs = adamw_all([wts[n] for n in WEIGHTS], [grad[n] for n in WEIGHTS],
                                       [mom[n] for n in WEIGHTS], [var[n] for n in WEIGHTS])
    return (loss, grad_x, *[grad[n] for n in WEIGHTS], *deltas, *new_ms, *new_vs)
```

```python
import functools
import math

import jax
import jax.numpy as jnp
from jax import lax
from jax.experimental import pallas as pl
from jax.experimental.pallas import tpu as pltpu

F32 = jnp.float32
BF16 = jnp.bfloat16

D_MODEL = 1024
POOL_W = 384
SSM_W = 384
ATT_W = 256
POOL_GW = 96
POOL_WINDOWS = (2, 4, 8, 16)
POOL_PAD = 16
SSM_NG = 24
SSM_N = 64
SSM_GC = 16
N_CHIPS = 4
N_DEV = 8
W_IN_SHARD = 2 * D_MODEL // N_CHIPS
NST = SSM_NG * SSM_N
BLK_CH = 128
BLK_GROUPS = BLK_CH // SSM_GC
BLK_ST = BLK_GROUPS * SSM_N
N_BLK = SSM_W // BLK_CH
N_MEM = 256
MEM_HEADS = 4
MEM_HD = 64
EPS = 1e-6

ADAM_LR = 0.001
ADAM_B1 = 0.9
ADAM_B2 = 0.999
ADAM_EPS = 1e-08
ADAM_WD = 0.01
ADAM_STEP = 10

SUBLANES = 8
LANES = 128
V7X_VMEM_BYTES = 64 * 2**20
VMEM_LIMIT = V7X_VMEM_BYTES - 8 * 2**20
SCAN_COLS = 512
ROW_TILE = 256
IN_PROJ_ROWS = 1024
MESH = pl.DeviceIdType.MESH

NT = (((1,), (1,)), ((), ()))
TN = (((0,), (0,)), ((), ()))


def _params(*sem):
    return pltpu.CompilerParams(dimension_semantics=sem, vmem_limit_bytes=VMEM_LIMIT)


def _dot(a, b):
    return jnp.dot(a, b, preferred_element_type=F32)


def _dot_nt(a, b):
    return lax.dot_general(a, b, NT, preferred_element_type=F32)


def _dot_tn(a, b):
    return lax.dot_general(a, b, TN, preferred_element_type=F32)


def _rms_scale(v):
    return lax.rsqrt(jnp.mean(v * v, axis=-1, keepdims=True) + EPS)


def _adamw(w, g, m, v):
    m = ADAM_B1 * m + (1.0 - ADAM_B1) * g
    v = ADAM_B2 * v + (1.0 - ADAM_B2) * (g * g)
    m_hat = m / (1.0 - ADAM_B1 ** ADAM_STEP)
    v_hat = v / (1.0 - ADAM_B2 ** ADAM_STEP)
    delta = -ADAM_LR * (m_hat / (jnp.sqrt(v_hat) + ADAM_EPS) + ADAM_WD * w)
    return delta, m, v


def _phase_shard(j):
    cx, cy = lax.axis_index("x"), lax.axis_index("y")
    others = [2 * px + py for px, py in _other_chips(cx, cy)]
    own = 2 * cx + cy
    return jnp.where(j == 0, own, jnp.where(j == 1, others[0], jnp.where(j == 2, others[1], others[2])))


def in_proj(x2, g_pre, w_in_shard, later_shards):
    t = x2.shape[0]
    tm = min(IN_PROJ_ROWS, t)
    n_tiles = t // tm
    n = len(later_shards)

    def body(x_ref, g_ref, ws_ref, *rest):
        shard_refs = rest[:n]
        o_ref, w_out_ref, gathered = rest[n], rest[n + 1], rest[n + 2:2 * n + 2]
        w_buf, h_buf = rest[2 * n + 2], rest[2 * n + 3]
        cast_bufs = rest[2 * n + 4:3 * n + 4]
        local_sems, send_sems, recv_sems = rest[3 * n + 4:]
        j = pl.program_id(0)
        i = pl.program_id(1)
        cx, cy, cc = _place()
        k = 2 * cx + cy
        chips = _other_chips(cx, cy)

        half = w_in_shard.shape[0] // 2

        def w_copy(p, block, core_half, forward):
            px, py = chips[p]
            rows = w_buf.at[block, pl.ds(pl.multiple_of(core_half * half, 2 * SUBLANES), half)]
            sem = 3 + p if forward else p
            return pltpu.make_async_remote_copy(
                src_ref=rows, dst_ref=rows, send_sem=send_sems.at[sem], recv_sem=recv_sems.at[sem],
                device_id=(cx, cy, 1 - cc) if forward else (px, py, cc), device_id_type=MESH)

        def own(m):
            return pltpu.make_async_copy(cast_bufs[m], gathered[m].at[k], local_sems.at[m])

        def later_copy(m, p, block, core_half, forward):
            px, py = chips[p]
            hr = later_shards[m].shape[0] // 2
            rows = pl.ds(pl.multiple_of(core_half * hr, 2 * SUBLANES), hr)
            dst = gathered[m].at[block, rows]
            sem = 6 + 6 * m + (3 + p if forward else p)
            return pltpu.make_async_remote_copy(
                src_ref=dst if forward else cast_bufs[m].at[rows], dst_ref=dst, send_sem=send_sems.at[sem],
                recv_sem=recv_sems.at[sem], device_id=(cx, cy, 1 - cc) if forward else (px, py, cc),
                device_id_type=MESH)

        @pl.when((j == 0) & (i == 0))
        def _():
            w_buf[k] = ws_ref[...].astype(BF16)
            for p in range(3):
                w_copy(p, k, cc, False).start()
            for m in range(n):
                cast_bufs[m][...] = shard_refs[m][...].astype(BF16)
                own(m).start()
                for p in range(3):
                    later_copy(m, p, k, cc, False).start()

        rows = pl.ds(pl.multiple_of(i * tm, tm), tm)

        @pl.when(j == 0)
        def _():
            x = x_ref[...]
            h_buf[rows, :] = (x * _rms_scale(x) * g_ref[...]).astype(BF16)

        for p in range(3):
            @pl.when((j == p + 1) & (i == 0))
            def _(p=p):
                px, py = chips[p]
                w_copy(p, 2 * px + py, cc, False).wait_recv()
                w_copy(p, 2 * px + py, cc, True).start()
                w_copy(p, 2 * px + py, 1 - cc, True).wait_recv()

        o_ref[...] = _dot(h_buf[rows, :], w_buf[_phase_shard(j)]).astype(BF16)

        @pl.when((j == N_CHIPS - 1) & (i == n_tiles - 1))
        def _():
            out_cp = pltpu.make_async_copy(w_buf, w_out_ref, local_sems.at[n])
            out_cp.start()
            for m in range(n):
                own(m).wait()
                for p in range(3):
                    px, py = chips[p]
                    later_copy(m, p, 2 * px + py, cc, False).wait_recv()
                    later_copy(m, p, 2 * px + py, cc, True).start()
            for m in range(n):
                for p in range(3):
                    px, py = chips[p]
                    later_copy(m, p, 2 * px + py, 1 - cc, True).wait_recv()
            for p in range(3):
                px, py = chips[p]
                w_copy(p, k, cc, False).wait_send()
                w_copy(p, 2 * px + py, cc, True).wait_send()
            for m in range(n):
                for p in range(3):
                    px, py = chips[p]
                    later_copy(m, p, k, cc, False).wait_send()
                    later_copy(m, p, 2 * px + py, cc, True).wait_send()
            out_cp.wait()

    whole = lambda a: pl.BlockSpec(a.shape, lambda j, i: (0,) * a.ndim)
    res = pl.pallas_call(
        body, grid=(N_CHIPS, n_tiles),
        in_specs=[pl.BlockSpec((tm, D_MODEL), lambda j, i: (jnp.where(j == 0, i, n_tiles - 1), 0)),
                  pl.BlockSpec((1, D_MODEL), lambda j, i: (0, 0)),
                  whole(w_in_shard)] + [whole(s) for s in later_shards],
        out_specs=[pl.BlockSpec((tm, W_IN_SHARD), lambda j, i: (i, _phase_shard(j)))] + [ANY_SPEC] * (n + 1),
        out_shape=([jax.ShapeDtypeStruct((t, 2 * D_MODEL), BF16),
                    jax.ShapeDtypeStruct((N_CHIPS,) + w_in_shard.shape, BF16)]
                   + [jax.ShapeDtypeStruct((N_CHIPS,) + s.shape, BF16) for s in later_shards]),
        scratch_shapes=([pltpu.VMEM((N_CHIPS,) + w_in_shard.shape, BF16), pltpu.VMEM((t, D_MODEL), BF16)]
                        + [pltpu.VMEM(s.shape, BF16) for s in later_shards]
                        + [pltpu.SemaphoreType.DMA((n + 1,)), pltpu.SemaphoreType.DMA((6 * n + 6,)),
                           pltpu.SemaphoreType.DMA((6 * n + 6,))]),
        compiler_params=_params("arbitrary", "arbitrary"), name="in_proj")(x2, g_pre, w_in_shard, *later_shards)
    return res[0], res[1], res[2:]


def _pool_lane_window(shape):
    ch = lax.broadcasted_iota(jnp.int32, shape, 1)
    return jnp.where(ch < POOL_GW, 2.0, jnp.where(ch < 2 * POOL_GW, 4.0, jnp.where(ch < 3 * POOL_GW, 8.0, 16.0)))


def _pool_select(win, s2, s4, s8, s16):
    return jnp.where(win == 2.0, s2, jnp.where(win == 4.0, s4, jnp.where(win == 8.0, s8, s16)))


def _pool_div_count(v, win):
    pos = (lax.broadcasted_iota(jnp.int32, (POOL_PAD, POOL_W), 0) + 1).astype(F32)
    head = v[:POOL_PAD] / jnp.minimum(pos, win)
    return jnp.concatenate([head, v[POOL_PAD:] * (1.0 / win)], axis=0)


def _pool_diff(u, pad_ref, l):
    lo = POOL_PAD
    pad_ref[pl.ds(lo, l), :] = u
    s2 = u + pad_ref[pl.ds(lo - 1, l), :]
    pad_ref[pl.ds(lo, l), :] = s2
    s4 = s2 + pad_ref[pl.ds(lo - 2, l), :]
    pad_ref[pl.ds(lo, l), :] = s4
    s8 = s4 + pad_ref[pl.ds(lo - 4, l), :]
    pad_ref[pl.ds(lo, l), :] = s8
    s16 = s8 + pad_ref[pl.ds(lo - 8, l), :]
    win = _pool_lane_window((1, POOL_W))
    return _pool_div_count(_pool_select(win, s2, s4, s8, s16), win) - u, win


def pool_fwd(proj, w_blk, pool_scale, nb, l):
    def body(u_ref, w_ref, ps_ref, y_ref, pad_ref):
        pad_ref[pl.ds(0, POOL_PAD), :] = jnp.zeros((POOL_PAD, POOL_W), F32)
        d, _ = _pool_diff(u_ref[...].astype(F32), pad_ref, l)
        y_ref[...] = (_dot(d.astype(BF16), w_ref[...]) * ps_ref[...]).astype(BF16)

    return pl.pallas_call(
        body, grid=(nb,),
        in_specs=[pl.BlockSpec((l, POOL_W), lambda b: (b, 0)),
                  pl.BlockSpec((POOL_W, POOL_W), lambda b: (0, 0)),
                  pl.BlockSpec((1, POOL_W), lambda b: (0, 0))],
        out_specs=pl.BlockSpec((l, POOL_W), lambda b: (b, 0)),
        out_shape=jax.ShapeDtypeStruct((nb * l, POOL_W), BF16),
        scratch_shapes=[pltpu.VMEM((POOL_PAD + l, POOL_W), F32)],
        compiler_params=_params("arbitrary"), name="pool_fwd")(proj, w_blk, pool_scale)


def pool_bwd(proj, d_ycat, w_blk, pool_scale, nb, l):
    def body(u_ref, dy_ref, w_ref, ps_ref, du_ref, dw_ref, dps_ref, pad_ref, padb_ref):
        b = pl.program_id(0)

        @pl.when(b == 0)
        def _():
            dw_ref[...] = jnp.zeros_like(dw_ref)
            dps_ref[...] = jnp.zeros_like(dps_ref)

        pad_ref[pl.ds(0, POOL_PAD), :] = jnp.zeros((POOL_PAD, POOL_W), F32)
        padb_ref[pl.ds(l, POOL_PAD), :] = jnp.zeros((POOL_PAD, POOL_W), F32)
        d, win = _pool_diff(u_ref[...].astype(F32), pad_ref, l)
        db = d.astype(BF16)
        w = w_ref[...]
        dy = dy_ref[...].astype(F32)
        dps_ref[...] += jnp.sum(dy * _dot(db, w), axis=0, keepdims=True)
        dyo = (dy * ps_ref[...]).astype(BF16)
        dw_ref[...] += _dot_tn(db, dyo)
        dd = _dot_nt(dyo, w)
        e = _pool_div_count(dd, win)
        padb_ref[pl.ds(0, l), :] = e
        f2 = e + padb_ref[pl.ds(1, l), :]
        padb_ref[pl.ds(0, l), :] = f2
        f4 = f2 + padb_ref[pl.ds(2, l), :]
        padb_ref[pl.ds(0, l), :] = f4
        f8 = f4 + padb_ref[pl.ds(4, l), :]
        padb_ref[pl.ds(0, l), :] = f8
        f16 = f8 + padb_ref[pl.ds(8, l), :]
        du_ref[...] = (_pool_select(win, f2, f4, f8, f16) - dd).astype(BF16)

    return pl.pallas_call(
        body, grid=(nb,),
        in_specs=[pl.BlockSpec((l, POOL_W), lambda b: (b, 0)),
                  pl.BlockSpec((l, POOL_W), lambda b: (b, 0)),
                  pl.BlockSpec((POOL_W, POOL_W), lambda b: (0, 0)),
                  pl.BlockSpec((1, POOL_W), lambda b: (0, 0))],
        out_specs=[pl.BlockSpec((l, POOL_W), lambda b: (b, 0)),
                   pl.BlockSpec((POOL_W, POOL_W), lambda b: (0, 0)),
                   pl.BlockSpec((1, POOL_W), lambda b: (0, 0))],
        out_shape=[jax.ShapeDtypeStruct((nb * l, POOL_W), BF16),
                   jax.ShapeDtypeStruct((POOL_W, POOL_W), F32),
                   jax.ShapeDtypeStruct((1, POOL_W), F32)],
        scratch_shapes=[pltpu.VMEM((POOL_PAD + l, POOL_W), F32), pltpu.VMEM((POOL_PAD + l, POOL_W), F32)],
        compiler_params=_params("arbitrary"), name="pool_bwd")(proj, d_ycat, w_blk, pool_scale)


def _discretise(a_re, a_im, ldt, b_re, b_im):
    dt = jnp.exp(ldt)
    mag = jnp.exp(a_re * dt)
    th = a_im * dt
    lr = mag * jnp.cos(th)
    li = mag * jnp.sin(th)
    nr = lr - 1.0
    den = a_re * a_re + a_im * a_im
    fr = (nr * a_re + li * a_im) / den
    fi = (li * a_re - nr * a_im) / den
    return lr, li, fr * b_re - fi * b_im, fr * b_im + fi * b_re


def _cmul(ar, ai, br, bi):
    return ar * br - ai * bi, ar * bi + ai * br


def s5_scan_consts(a_re_row, a_im_row, ldt_row):
    def body(ar_ref, ai_ref, ld_ref, f_ref, b_ref):
        dt = jnp.exp(ld_ref[...])
        mag = jnp.exp(ar_ref[...] * dt)
        th = ai_ref[...] * dt
        lr = mag * jnp.cos(th)
        li = mag * jnp.sin(th)
        r = lax.broadcasted_iota(jnp.int32, (SUBLANES, NST), 0)
        for out_ref, sign, rev in ((f_ref, 1.0, False), (b_ref, -1.0, True)):
            p = [(lr, sign * li)]
            for _ in range(SUBLANES - 1):
                p.append(_cmul(p[-1][0], p[-1][1], lr, sign * li))
            for idx, s in enumerate((1, 2, 4)):
                inside = (r < SUBLANES - s) if rev else (r >= s)
                out_ref[pl.ds(2 * idx * SUBLANES, SUBLANES), :] = jnp.where(inside, p[s - 1][0], 0.0)
                out_ref[pl.ds((2 * idx + 1) * SUBLANES, SUBLANES), :] = jnp.where(inside, p[s - 1][1], 0.0)
            pr = jnp.zeros((SUBLANES, NST), F32)
            pi = jnp.zeros((SUBLANES, NST), F32)
            for row in range(SUBLANES):
                power = (SUBLANES - row) if rev else (row + 1)
                pr = jnp.where(r == row, p[power - 1][0], pr)
                pi = jnp.where(r == row, p[power - 1][1], pi)
            out_ref[pl.ds(6 * SUBLANES, SUBLANES), :] = pr
            out_ref[pl.ds(7 * SUBLANES, SUBLANES), :] = pi

    shape = jax.ShapeDtypeStruct((8 * SUBLANES, NST), F32)
    return pl.pallas_call(body, out_shape=[shape, shape], name="s5_scan_consts")(a_re_row, a_im_row, ldt_row)


def s5_bbar(a_re_row, a_im_row, ldt_row, b_re_t, b_im_t):
    def body(ar, ai, ld, br, bi, o_re, o_im):
        _, _, bbr, bbi = _discretise(ar[...], ai[...], ld[...], br[...], bi[...])
        o_re[...] = bbr
        o_im[...] = bbi

    shape = jax.ShapeDtypeStruct((SSM_GC, NST), F32)
    return pl.pallas_call(body, out_shape=[shape, shape], name="s5_bbar")(a_re_row, a_im_row, ldt_row, b_re_t, b_im_t)


def s5_param_bwd(a_re_row, a_im_row, ldt_row, b_re_t, b_im_t, d_lr, d_li, d_bbr, d_bbi):
    def body(ar, ai, ld, br, bi, g_lr, g_li, g_br, g_bi, o_ar, o_ai, o_ld, o_br, o_bi):
        _, vjp = jax.vjp(_discretise, ar[...], ai[...], ld[...], br[...], bi[...])
        d_ar, d_ai, d_ld, d_br, d_bi = vjp((g_lr[...], g_li[...], g_br[...], g_bi[...]))
        o_ar[...] = d_ar
        o_ai[...] = d_ai
        state = lax.broadcasted_iota(jnp.int32, (NST, LANES), 0)
        lane = lax.broadcasted_iota(jnp.int32, (NST, LANES), 1)
        in_group = jnp.where((state >= lane * SSM_N) & (state < (lane + 1) * SSM_N), 1.0, 0.0)
        o_ld[...] = jnp.dot(d_ld, in_group, precision=lax.Precision.HIGHEST, preferred_element_type=F32)
        o_br[...] = d_br
        o_bi[...] = d_bi

    row = jax.ShapeDtypeStruct((1, NST), F32)
    mat = jax.ShapeDtypeStruct((SSM_GC, NST), F32)
    grp = jax.ShapeDtypeStruct((1, LANES), F32)
    return pl.pallas_call(body, out_shape=[row, row, grp, mat, mat], name="s5_param_bwd")(
        a_re_row, a_im_row, ldt_row, b_re_t, b_im_t, d_lr, d_li, d_bbr, d_bbi)


def _scan_tiles(buf_ref, cst_ref, carry_ref, rows, reverse, reset, h_ref=None, acc_ref=None):
    n_tiles = rows // SUBLANES
    shifts = ((1, 0), (2, 2), (4, 4))
    row_id = lax.broadcasted_iota(jnp.int32, (SUBLANES, SCAN_COLS), 0)
    cols = [(j * SCAN_COLS, NST + j * SCAN_COLS) for j in range(NST // SCAN_COLS)]
    edge = 0 if reverse else SUBLANES - 1
    carry = jnp.where(reset, 0.0, carry_ref[...])
    accs = [(acc_ref[:, pl.ds(o_re, SCAN_COLS)], acc_ref[:, pl.ds(o_im, SCAN_COLS)]) for o_re, o_im in cols] if reverse else None
    for i in range(n_tiles):
        start = ((n_tiles - 1 - i) if reverse else i) * SUBLANES
        rws = pl.ds(start, SUBLANES)
        for j, (o_re, o_im) in enumerate(cols):
            c_re = pl.ds(o_re, SCAN_COLS)
            c_im = pl.ds(o_im, SCAN_COLS)
            xr = buf_ref[rws, c_re]
            xi = buf_ref[rws, c_im]
            for s, base in shifts:
                amount = (SUBLANES - s) if reverse else s
                sr = pltpu.roll(xr, amount, 0)
                si = pltpu.roll(xi, amount, 0)
                mr = cst_ref[pl.ds(base * SUBLANES, SUBLANES), c_re]
                mi = cst_ref[pl.ds((base + 1) * SUBLANES, SUBLANES), c_re]
                xr, xi = xr + (mr * sr - mi * si), xi + (mr * si + mi * sr)
            pr = cst_ref[pl.ds(6 * SUBLANES, SUBLANES), c_re]
            pi = cst_ref[pl.ds(7 * SUBLANES, SUBLANES), c_re]
            cr = carry[:, o_re:o_re + SCAN_COLS]
            ci = carry[:, o_im:o_im + SCAN_COLS]
            xr, xi = xr + (pr * cr - pi * ci), xi + (pr * ci + pi * cr)
            buf_ref[rws, c_re] = xr
            buf_ref[rws, c_im] = xi
            if reverse:
                gnr = jnp.where(row_id == SUBLANES - 1, cr, pltpu.roll(xr, SUBLANES - 1, 0))
                gni = jnp.where(row_id == SUBLANES - 1, ci, pltpu.roll(xi, SUBLANES - 1, 0))
                hr = h_ref[rws, c_re]
                hi = h_ref[rws, c_im]
                accs[j] = (accs[j][0] + (gnr * hr + gni * hi), accs[j][1] + (gni * hr - gnr * hi))
        carry = jnp.broadcast_to(buf_ref[pl.ds(start + edge, 1), :], (SUBLANES, 2 * NST))
    carry_ref[...] = carry
    if reverse:
        for (o_re, o_im), (a_re_, a_im_) in zip(cols, accs):
            acc_ref[:, pl.ds(o_re, SCAN_COLS)] = a_re_
            acc_ref[:, pl.ds(o_im, SCAN_COLS)] = a_im_


def _state_block(v, m):
    return jnp.concatenate([v[:, m * BLK_ST:(m + 1) * BLK_ST], v[:, NST + m * BLK_ST:NST + (m + 1) * BLK_ST]], axis=1)


def _put_state_block(buf_ref, m, val):
    buf_ref[:, pl.ds(m * BLK_ST, BLK_ST)] = val[:, :BLK_ST]
    buf_ref[:, pl.ds(NST + m * BLK_ST, BLK_ST)] = val[:, BLK_ST:]


def s5_fwd(proj, b_m, c_m, cst, d_skip, w_glu, nb, l):
    tt = min(256, l)
    nt = l // tt
    n_chunks = nb * nt

    def body(u_ref, b_ref, c_ref, cst_ref, ds_ref, wg_ref, y_ref, h_ref, yp_ref, z_ref,
             buf0, buf1, ub0, ub1, carry_ref):
        s = pl.program_id(0)

        @pl.when(s == 0)
        def _():
            for r in (buf0, buf1, ub0, ub1, carry_ref):
                r[...] = jnp.zeros_like(r)

        def step(p_buf, p_u, q_buf):
            h = p_buf[...].astype(BF16)
            h_ref[...] = h
            ypre = (jnp.concatenate([_dot(_state_block(h, m), c_ref[m]) for m in range(N_BLK)], axis=1)
                    + ds_ref[...] * p_u[...])
            yp_ref[...] = ypre
            z = _dot(jax.nn.gelu(ypre).astype(BF16), wg_ref[...])
            z_ref[...] = z
            y_ref[...] = (z[:, :SSM_W] * jax.nn.sigmoid(z[:, SSM_W:])).astype(BF16)
            ub = u_ref[...]
            p_u[...] = ub.astype(F32)
            for m in range(N_BLK):
                _put_state_block(p_buf, m, _dot(ub[:, m * BLK_CH:(m + 1) * BLK_CH], b_ref[m]))
            _scan_tiles(q_buf, cst_ref, carry_ref, tt, False, lax.rem(s + nt - 1, nt) == 0)

        @pl.when(lax.rem(s, 2) == 0)
        def _():
            step(buf0, ub0, buf1)

        @pl.when(lax.rem(s, 2) == 1)
        def _():
            step(buf1, ub1, buf0)

    row_in = lambda s: (jnp.minimum(s, n_chunks - 1), 1)
    row_out = lambda s: (jnp.maximum(s - 2, 0), 0)
    const = lambda s: (0, 0)
    const3 = lambda s: (0, 0, 0)
    return pl.pallas_call(
        body, grid=(n_chunks + 2,),
        in_specs=[pl.BlockSpec((tt, SSM_W), row_in),
                  pl.BlockSpec((N_BLK, BLK_CH, 2 * BLK_ST), const3),
                  pl.BlockSpec((N_BLK, 2 * BLK_ST, BLK_CH), const3),
                  pl.BlockSpec((8 * SUBLANES, NST), const),
                  pl.BlockSpec((1, SSM_W), const),
                  pl.BlockSpec((SSM_W, 2 * SSM_W), const)],
        out_specs=[pl.BlockSpec((tt, SSM_W), row_out),
                   pl.BlockSpec((tt, 2 * NST), row_out),
                   pl.BlockSpec((tt, SSM_W), row_out),
                   pl.BlockSpec((tt, 2 * SSM_W), row_out)],
        out_shape=[jax.ShapeDtypeStruct((nb * l, SSM_W), BF16),
                   jax.ShapeDtypeStruct((nb * l, 2 * NST), BF16),
                   jax.ShapeDtypeStruct((nb * l, SSM_W), F32),
                   jax.ShapeDtypeStruct((nb * l, 2 * SSM_W), F32)],
        scratch_shapes=[pltpu.VMEM((tt, 2 * NST), F32), pltpu.VMEM((tt, 2 * NST), F32),
                        pltpu.VMEM((tt, SSM_W), F32), pltpu.VMEM((tt, SSM_W), F32),
                        pltpu.VMEM((SUBLANES, 2 * NST), F32)],
        compiler_params=_params("arbitrary"), name="s5_fwd")(proj, b_m, c_m, cst, d_skip, w_glu)


def s5_bwd(proj, d_ycat, h, ypre, z, b_m, c_m, cst_rev, d_skip, w_glu, nb, l):
    tt = min(256, l)
    nt = l // tt
    n_chunks = nb * nt

    def body(u_ref, dy_ref, h_ref, yp_ref, z_ref, b_ref, c_ref, cst_ref, ds_ref, wg_ref,
             du_ref, dwg_ref, dds_ref, dcc_ref, dbb_ref, dlam_ref,
             buf0, buf1, hb0, hb1, ub0, ub1, dyp0, dyp1, carry_ref, dc_ref, db_ref):
        s = pl.program_id(0)

        @pl.when(s == 0)
        def _():
            for r in (buf0, buf1, hb0, hb1, ub0, ub1, dyp0, dyp1, carry_ref, dc_ref, db_ref, dwg_ref, dds_ref, dlam_ref):
                r[...] = jnp.zeros_like(r)

        def step(p_buf, p_h, p_u, p_dyp, q_buf, q_h):
            g = p_buf[...].astype(BF16)
            ub_done = p_u[...].astype(BF16)
            du = []
            for m in range(N_BLK):
                g_m = _state_block(g, m)
                db_ref[m] += _dot_tn(g_m, ub_done[:, m * BLK_CH:(m + 1) * BLK_CH])
                du.append(_dot_nt(g_m, b_ref[m]))
            du_ref[...] = (jnp.concatenate(du, axis=1) + ds_ref[...] * p_dyp[...]).astype(BF16)
            u = u_ref[...].astype(F32)
            ypre = yp_ref[...]
            z = z_ref[...]
            z1 = z[:, :SSM_W]
            sg = jax.nn.sigmoid(z[:, SSM_W:])
            dy = dy_ref[...].astype(F32) * jnp.where(s < n_chunks, 1.0, 0.0)
            dz = jnp.concatenate([dy * sg, dy * z1 * sg * (1.0 - sg)], axis=1).astype(BF16)
            yg, gelu_vjp = jax.vjp(jax.nn.gelu, ypre)
            dwg_ref[...] += _dot_tn(yg.astype(BF16), dz)
            dypre = gelu_vjp(_dot_nt(dz, wg_ref[...]))[0]
            dds_ref[...] += jnp.sum(dypre * u, axis=0, keepdims=True)
            dyb = dypre.astype(BF16)
            hb = h_ref[...]
            p_h[...] = hb.astype(F32)
            p_u[...] = u
            p_dyp[...] = dypre
            for m in range(N_BLK):
                dy_m = dyb[:, m * BLK_CH:(m + 1) * BLK_CH]
                dc_ref[m] += _dot_tn(_state_block(hb, m), dy_m)
                _put_state_block(p_buf, m, _dot_nt(dy_m, c_ref[m]))
            _scan_tiles(q_buf, cst_ref, carry_ref, tt, True, lax.rem(s + nt - 1, nt) == 0, h_ref=q_h, acc_ref=dlam_ref)

        @pl.when(lax.rem(s, 2) == 0)
        def _():
            step(buf0, hb0, ub0, dyp0, buf1, hb1)

        @pl.when(lax.rem(s, 2) == 1)
        def _():
            step(buf1, hb1, ub1, dyp1, buf0, hb0)

        @pl.when(s == n_chunks + 1)
        def _():
            dlam_ref[...] = jnp.broadcast_to(jnp.sum(dlam_ref[...], axis=0, keepdims=True), dlam_ref.shape)
            for acc_ref, out_ref in ((dc_ref, dcc_ref), (db_ref, dbb_ref)):
                for m in range(N_BLK):
                    for ri in range(2):
                        for gl in range(BLK_GROUPS):
                            out_ref[ri, pl.ds((m * BLK_GROUPS + gl) * SSM_N, SSM_N), :] = acc_ref[
                                m, pl.ds(ri * BLK_ST + gl * SSM_N, SSM_N), pl.ds(gl * SSM_GC, SSM_GC)]

    def chunk_rows(o):
        return lax.div(o, nt) * nt + (nt - 1 - lax.rem(o, nt))

    def rrow(col):
        return lambda s: (chunk_rows(jnp.minimum(s, n_chunks - 1)), col)

    const = lambda s: (0, 0)
    const3 = lambda s: (0, 0, 0)
    state_buf = pltpu.VMEM((tt, 2 * NST), F32)
    chan_buf = pltpu.VMEM((tt, SSM_W), F32)
    return pl.pallas_call(
        body, grid=(n_chunks + 2,),
        in_specs=[pl.BlockSpec((tt, SSM_W), rrow(1)),
                  pl.BlockSpec((tt, SSM_W), rrow(1)),
                  pl.BlockSpec((tt, 2 * NST), rrow(0)),
                  pl.BlockSpec((tt, SSM_W), rrow(0)),
                  pl.BlockSpec((tt, 2 * SSM_W), rrow(0)),
                  pl.BlockSpec((N_BLK, BLK_CH, 2 * BLK_ST), const3),
                  pl.BlockSpec((N_BLK, 2 * BLK_ST, BLK_CH), const3),
                  pl.BlockSpec((8 * SUBLANES, NST), const),
                  pl.BlockSpec((1, SSM_W), const),
                  pl.BlockSpec((SSM_W, 2 * SSM_W), const)],
        out_specs=[pl.BlockSpec((tt, SSM_W), lambda s: (chunk_rows(jnp.maximum(s - 2, 0)), 0)),
                   pl.BlockSpec((SSM_W, 2 * SSM_W), const),
                   pl.BlockSpec((1, SSM_W), const),
                   pl.BlockSpec((2, NST, SSM_GC), const3),
                   pl.BlockSpec((2, NST, SSM_GC), const3),
                   pl.BlockSpec((SUBLANES, 2 * NST), const)],
        out_shape=[jax.ShapeDtypeStruct((nb * l, SSM_W), BF16),
                   jax.ShapeDtypeStruct((SSM_W, 2 * SSM_W), F32),
                   jax.ShapeDtypeStruct((1, SSM_W), F32),
                   jax.ShapeDtypeStruct((2, NST, SSM_GC), F32),
                   jax.ShapeDtypeStruct((2, NST, SSM_GC), F32),
                   jax.ShapeDtypeStruct((SUBLANES, 2 * NST), F32)],
        scratch_shapes=[state_buf, state_buf, state_buf, state_buf, chan_buf, chan_buf, chan_buf, chan_buf,
                        pltpu.VMEM((SUBLANES, 2 * NST), F32),
                        pltpu.VMEM((N_BLK, 2 * BLK_ST, BLK_CH), F32), pltpu.VMEM((N_BLK, 2 * BLK_ST, BLK_CH), F32)],
        compiler_params=_params("arbitrary"), name="s5_bwd")(
            proj, d_ycat, h, ypre, z, b_m, c_m, cst_rev, d_skip, w_glu)


def _head_mask(hh):
    lane = lax.broadcasted_iota(jnp.int32, (1, ATT_W), 1)
    return (lane >= hh * MEM_HD) & (lane < (hh + 1) * MEM_HD)


def _mem_kv(mem_ref, gm_ref, wkv_ref):
    m = mem_ref[0]
    mh = m * _rms_scale(m)
    mb = (mh * gm_ref[...]).astype(BF16)
    kv = _dot(mb, wkv_ref[...])
    return mh, mb, kv[:, :ATT_W].astype(BF16), kv[:, ATT_W:].astype(BF16)


def _stack_heads(dst_ref, a):
    for hh in range(MEM_HEADS):
        dst_ref[pl.ds(hh * N_MEM, N_MEM), :] = jnp.where(_head_mask(hh), a, jnp.zeros_like(a))


def _fold_heads(a):
    out = jnp.zeros((N_MEM, ATT_W), a.dtype)
    for hh in range(MEM_HEADS):
        out = out + jnp.where(_head_mask(hh), a[hh * N_MEM:(hh + 1) * N_MEM], 0.0)
    return out


def _softmax_heads(s):
    parts = []
    for hh in range(MEM_HEADS):
        sh = s[:, hh * N_MEM:(hh + 1) * N_MEM]
        e = jnp.exp(sh - jnp.max(sh, axis=-1, keepdims=True))
        parts.append(e / jnp.sum(e, axis=-1, keepdims=True))
    return jnp.concatenate(parts, axis=1)


def attn_fwd(proj, mem, g_mem, w_kv, nb, l):
    tq = min(512, l)
    nq = l // tq
    scale = MEM_HD ** -0.5

    def body(q_ref, mem_ref, gm_ref, wkv_ref, o_ref, k_s, v_s):
        @pl.when(pl.program_id(1) == 0)
        def _():
            _, _, k, v = _mem_kv(mem_ref, gm_ref, wkv_ref)
            _stack_heads(k_s, k)
            _stack_heads(v_s, v)

        p = _softmax_heads(_dot_nt(q_ref[...], k_s[...]) * scale)
        o_ref[...] = _dot(p.astype(BF16), v_s[...]).astype(BF16)

    const = lambda b, t: (0, 0)
    stacked = pltpu.VMEM((MEM_HEADS * N_MEM, ATT_W), BF16)
    return pl.pallas_call(
        body, grid=(nb, nq),
        in_specs=[pl.BlockSpec((tq, ATT_W), lambda b, t: (b * nq + t, 3)),
                  pl.BlockSpec((1, N_MEM, D_MODEL), lambda b, t: (b, 0, 0)),
                  pl.BlockSpec((1, D_MODEL), const),
                  pl.BlockSpec((D_MODEL, 2 * ATT_W), const)],
        out_specs=pl.BlockSpec((tq, ATT_W), lambda b, t: (b * nq + t, 0)),
        out_shape=jax.ShapeDtypeStruct((nb * l, ATT_W), BF16),
        scratch_shapes=[stacked, stacked],
        compiler_params=_params("arbitrary", "arbitrary"), name="attn_fwd")(proj, mem, g_mem, w_kv)


def attn_bwd(proj, d_ycat, mem, g_mem, w_kv, nb, l):
    tq = min(512, l)
    nq = l // tq
    scale = MEM_HD ** -0.5

    def body(q_ref, do_ref, mem_ref, gm_ref, wkv_ref, dq_ref, dwkv_ref, dgm_ref, k_s, v_s, mb_s, dk_s, dv_s):
        b = pl.program_id(0)
        t = pl.program_id(1)

        @pl.when((b == 0) & (t == 0))
        def _():
            dwkv_ref[...] = jnp.zeros_like(dwkv_ref)
            dgm_ref[...] = jnp.zeros_like(dgm_ref)

        @pl.when(t == 0)
        def _():
            _, mb, k, v = _mem_kv(mem_ref, gm_ref, wkv_ref)
            _stack_heads(k_s, k)
            _stack_heads(v_s, v)
            mb_s[...] = mb
            dk_s[...] = jnp.zeros_like(dk_s)
            dv_s[...] = jnp.zeros_like(dv_s)

        q = q_ref[...]
        do = do_ref[...]
        k = k_s[...]
        p = _softmax_heads(_dot_nt(q, k) * scale)
        dp = _dot_nt(do, v_s[...])
        ds = []
        for hh in range(MEM_HEADS):
            blk = slice(hh * N_MEM, (hh + 1) * N_MEM)
            ds.append(p[:, blk] * (dp[:, blk] - jnp.sum(dp[:, blk] * p[:, blk], axis=-1, keepdims=True)) * scale)
        ds = jnp.concatenate(ds, axis=1).astype(BF16)
        dq_ref[...] = _dot(ds, k).astype(BF16)
        dk_s[...] += _dot_tn(ds, q)
        dv_s[...] += _dot_tn(p.astype(BF16), do)

        @pl.when(t == nq - 1)
        def _():
            dkv = jnp.concatenate([_fold_heads(dk_s[...]), _fold_heads(dv_s[...])], axis=1).astype(BF16)
            dwkv_ref[...] += _dot_tn(mb_s[...], dkv)
            m = mem_ref[0]
            dgm_ref[...] += jnp.sum(_dot_nt(dkv, wkv_ref[...]) * (m * _rms_scale(m)), axis=0, keepdims=True)

    const = lambda b, t: (0, 0)
    return pl.pallas_call(
        body, grid=(nb, nq),
        in_specs=[pl.BlockSpec((tq, ATT_W), lambda b, t: (b * nq + t, 3)),
                  pl.BlockSpec((tq, ATT_W), lambda b, t: (b * nq + t, 3)),
                  pl.BlockSpec((1, N_MEM, D_MODEL), lambda b, t: (b, 0, 0)),
                  pl.BlockSpec((1, D_MODEL), const),
                  pl.BlockSpec((D_MODEL, 2 * ATT_W), const)],
        out_specs=[pl.BlockSpec((tq, ATT_W), lambda b, t: (b * nq + t, 0)),
                   pl.BlockSpec((D_MODEL, 2 * ATT_W), const),
                   pl.BlockSpec((1, D_MODEL), const)],
        out_shape=[jax.ShapeDtypeStruct((nb * l, ATT_W), BF16),
                   jax.ShapeDtypeStruct((D_MODEL, 2 * ATT_W), F32),
                   jax.ShapeDtypeStruct((1, D_MODEL), F32)],
        scratch_shapes=[pltpu.VMEM((MEM_HEADS * N_MEM, ATT_W), BF16), pltpu.VMEM((MEM_HEADS * N_MEM, ATT_W), BF16),
                        pltpu.VMEM((N_MEM, D_MODEL), BF16),
                        pltpu.VMEM((MEM_HEADS * N_MEM, ATT_W), F32), pltpu.VMEM((MEM_HEADS * N_MEM, ATT_W), F32)],
        compiler_params=_params("arbitrary", "arbitrary"), name="attn_bwd")(proj, d_ycat, mem, g_mem, w_kv)


def tail(y_pool, y_ssm, y_att, proj, x2, target, w_out, g_post):
    t = x2.shape[0]
    tm = min(ROW_TILE, t)

    def body(yp_ref, ys_ref, ya_ref, gate_ref, x_ref, tg_ref, w_ref, gp_ref,
             dz_ref, dyc_ref, dgate_ref, dw_ref, dgp_ref, loss_ref):
        @pl.when(pl.program_id(0) == 0)
        def _():
            dw_ref[...] = jnp.zeros_like(dw_ref)
            dgp_ref[...] = jnp.zeros_like(dgp_ref)
            loss_ref[...] = jnp.zeros_like(loss_ref)

        ycat = jnp.concatenate([yp_ref[...], ys_ref[...], ya_ref[...]], axis=1).astype(F32)
        gate = gate_ref[...].astype(F32)
        sg = jax.nn.sigmoid(gate)
        silu = gate * sg
        yb = (ycat * silu).astype(BF16)
        w = w_ref[...]
        out = _dot(yb, w)
        r2 = _rms_scale(out)
        oh = out * r2
        gp = gp_ref[...]
        err = (x_ref[...] + oh * gp) - tg_ref[...]
        loss_ref[...] += 0.5 * jnp.sum(jnp.mean(err * err, axis=-1, keepdims=True), axis=0, keepdims=True)
        dz = err * (1.0 / D_MODEL)
        dz_ref[...] = dz.astype(BF16)
        dgp_ref[...] += jnp.sum(dz * oh, axis=0, keepdims=True)
        dn = dz * gp
        dout = (r2 * (dn - oh * jnp.mean(dn * oh, axis=-1, keepdims=True))).astype(BF16)
        dw_ref[...] += _dot_tn(yb, dout)
        dy = _dot_nt(dout, w)
        dyc_ref[...] = (dy * silu).astype(BF16)
        dgate_ref[...] = (dy * ycat * (sg * (1.0 + gate * (1.0 - sg)))).astype(BF16)

    row = lambda i: (i, 0)
    const = lambda i: (0, 0)
    full = jax.ShapeDtypeStruct((t, D_MODEL), BF16)
    return pl.pallas_call(
        body, grid=(t // tm,),
        in_specs=[pl.BlockSpec((tm, POOL_W), row),
                  pl.BlockSpec((tm, SSM_W), row),
                  pl.BlockSpec((tm, ATT_W), row),
                  pl.BlockSpec((tm, D_MODEL), lambda i: (i, 1)),
                  pl.BlockSpec((tm, D_MODEL), row),
                  pl.BlockSpec((tm, D_MODEL), row),
                  pl.BlockSpec((D_MODEL, D_MODEL), const),
                  pl.BlockSpec((1, D_MODEL), const)],
        out_specs=[pl.BlockSpec((tm, D_MODEL), row),
                   pl.BlockSpec((tm, D_MODEL), row),
                   pl.BlockSpec((tm, D_MODEL), row),
                   pl.BlockSpec((D_MODEL, D_MODEL), const),
                   pl.BlockSpec((1, D_MODEL), const),
                   pl.BlockSpec((1, LANES), const)],
        out_shape=[full, full, full,
                   jax.ShapeDtypeStruct((D_MODEL, D_MODEL), F32),
                   jax.ShapeDtypeStruct((1, D_MODEL), F32),
                   jax.ShapeDtypeStruct((1, LANES), F32)],
        compiler_params=_params("arbitrary"), name="tail")(y_pool, y_ssm, y_att, proj, x2, target, w_out, g_post)


def in_proj_bwd(du_pool, du_ssm, dq, d_gate, x2, dz, g_pre, w_in):
    t = x2.shape[0]
    tm = min(ROW_TILE, t)
    shard = W_IN_SHARD

    def body(dup_ref, dus_ref, dq_ref, dg_ref, x_ref, dz_ref, g_ref, w_ref, gx_ref, dw_ref, dgp_ref):
        @pl.when(pl.program_id(0) == 0)
        def _():
            dw_ref[...] = jnp.zeros_like(dw_ref)
            dgp_ref[...] = jnp.zeros_like(dgp_ref)

        dproj = jnp.concatenate([dup_ref[...], dus_ref[...], dq_ref[...], dg_ref[...]], axis=1)
        x = x_ref[...]
        r1 = _rms_scale(x)
        xh = x * r1
        g = g_ref[...]
        hb = (xh * g).astype(BF16)
        dh = jnp.zeros((tm, D_MODEL), F32)
        for k in range(N_CHIPS):
            dp_k = dproj[:, k * shard:(k + 1) * shard]
            dw_ref[k] += _dot_tn(hb, dp_k)
            dh = dh + _dot_nt(dp_k, w_ref[k])
        dgp_ref[...] += jnp.sum(dh * xh, axis=0, keepdims=True)
        dn = dh * g
        gx_ref[...] = dz_ref[...].astype(F32) + r1 * (dn - xh * jnp.mean(dn * xh, axis=-1, keepdims=True))

    row = lambda i: (i, 0)
    const = lambda i: (0, 0)
    return pl.pallas_call(
        body, grid=(t // tm,),
        in_specs=[pl.BlockSpec((tm, POOL_W), row),
                  pl.BlockSpec((tm, SSM_W), row),
                  pl.BlockSpec((tm, ATT_W), row),
                  pl.BlockSpec((tm, D_MODEL), row),
                  pl.BlockSpec((tm, D_MODEL), row),
                  pl.BlockSpec((tm, D_MODEL), row),
                  pl.BlockSpec((1, D_MODEL), const),
                  pl.BlockSpec((N_CHIPS, D_MODEL, shard), lambda i: (0, 0, 0))],
        out_specs=[pl.BlockSpec((tm, D_MODEL), row),
                   pl.BlockSpec((N_CHIPS, D_MODEL, shard), lambda i: (0, 0, 0)),
                   pl.BlockSpec((1, D_MODEL), const)],
        out_shape=[jax.ShapeDtypeStruct((t, D_MODEL), F32),
                   jax.ShapeDtypeStruct((N_CHIPS, D_MODEL, shard), F32),
                   jax.ShapeDtypeStruct((1, D_MODEL), F32)],
        compiler_params=_params("arbitrary"), name="in_proj_bwd")(du_pool, du_ssm, dq, d_gate, x2, dz, g_pre, w_in)


def _block_diag(blocks):
    g, r, c = blocks.shape
    eye = jnp.eye(g, dtype=blocks.dtype)
    return jnp.einsum("grc,gh->grhc", blocks, eye).reshape(g * r, g * c)


def _block_diag_extract(mat, g, r, c):
    eye = jnp.eye(g, dtype=mat.dtype)
    return jnp.einsum("grhc,gh->grc", mat.reshape(g, r, g, c), eye)


def local_step(x, mem, target, g_pre, w_in_shard, w_pool, pool_scale, a_re, a_im, log_dt, b_re, b_im, c_re, c_im,
               d_skip, later_shards, g_mem, g_post):
    nb, l, _ = x.shape
    x2 = x.reshape(nb * l, D_MODEL)
    tg2 = target.reshape(nb * l, D_MODEL)

    rowv = lambda a: a.reshape(1, NST)
    lanes_last = lambda b: b.transpose(2, 0, 1).reshape(SSM_GC, NST)
    ldt_row = rowv(jnp.broadcast_to(log_dt.reshape(SSM_NG, 1), (SSM_NG, SSM_N)))
    b_re_t = lanes_last(b_re)
    b_im_t = lanes_last(b_im)
    cst_f, cst_b = s5_scan_consts(rowv(a_re), rowv(a_im), ldt_row)
    bbr, bbi = s5_bbar(rowv(a_re), rowv(a_im), ldt_row, b_re_t, b_im_t)
    eye = jnp.eye(BLK_GROUPS, dtype=F32)
    blk_in = lambda bb: jnp.einsum("cmgn,gh->mhcgn", bb.reshape(SSM_GC, N_BLK, BLK_GROUPS, SSM_N), eye).reshape(
        N_BLK, BLK_CH, BLK_ST)
    blk_out = lambda cc: jnp.einsum("mgcn,gh->mgnhc", cc.reshape(N_BLK, BLK_GROUPS, SSM_GC, SSM_N), eye).reshape(
        N_BLK, BLK_ST, BLK_CH)
    b_m = jnp.concatenate([blk_in(bbr), blk_in(bbi)], axis=2).astype(BF16)
    c_m = jnp.concatenate([blk_out(c_re), -blk_out(c_im)], axis=1).astype(BF16)
    w_pool_blk = _block_diag(w_pool.reshape(4, POOL_GW, POOL_GW)).astype(BF16)

    proj, w_in, (g_glu, g_kv, g_out) = in_proj(x2, g_pre, w_in_shard, later_shards)
    w_glu = _from_chip_major(g_glu)
    w_kv = g_kv.reshape(D_MODEL, 2 * ATT_W)
    w_out = g_out.reshape(D_MODEL, D_MODEL)
    y_pool = pool_fwd(proj, w_pool_blk, pool_scale, nb, l)
    y_ssm, h, ypre, z = s5_fwd(proj, b_m, c_m, cst_f, d_skip, w_glu, nb, l)
    y_att = attn_fwd(proj, mem, g_mem, w_kv, nb, l)
    dz, d_ycat, d_gate, d_w_out, d_g_post, loss = tail(y_pool, y_ssm, y_att, proj, x2, tg2, w_out, g_post)
    du_pool, d_wp_blk, d_pool_scale = pool_bwd(proj, d_ycat, w_pool_blk, pool_scale, nb, l)
    du_ssm, d_w_glu, d_d_skip, d_cc, d_bb, d_lam = s5_bwd(
        proj, d_ycat, h, ypre, z, b_m, c_m, cst_b, d_skip, w_glu, nb, l)
    dq, d_w_kv, d_g_mem = attn_bwd(proj, d_ycat, mem, g_mem, w_kv, nb, l)
    grad_x, d_w_in, d_g_pre = in_proj_bwd(du_pool, du_ssm, dq, d_gate, x2, dz, g_pre, w_in)

    d_c = d_cc[0].reshape(SSM_NG, SSM_N, SSM_GC)
    d_ci = d_cc[1].reshape(SSM_NG, SSM_N, SSM_GC)
    d_lam_row = d_lam[0]
    d_ar, d_ai, d_ld, d_br, d_bi = s5_param_bwd(
        rowv(a_re), rowv(a_im), ldt_row, b_re_t, b_im_t,
        d_lam_row[:NST].reshape(1, NST), d_lam_row[NST:].reshape(1, NST), d_bb[0].T, d_bb[1].T)
    from_lanes_last = lambda b: b.reshape(SSM_GC, SSM_NG, SSM_N).transpose(1, 2, 0).reshape(1, SSM_NG, SSM_N, SSM_GC)

    grads = {
        "g_pre": d_g_pre,
        "w_in": d_w_in,
        "w_pool": _block_diag_extract(d_wp_blk, 4, POOL_GW, POOL_GW).reshape(1, 4, POOL_GW, POOL_GW),
        "pool_scale": d_pool_scale,
        "a_re": d_ar.reshape(1, SSM_NG, SSM_N),
        "a_im": d_ai.reshape(1, SSM_NG, SSM_N),
        "log_dt": d_ld[:, :SSM_NG],
        "b_re": from_lanes_last(d_br),
        "b_im": from_lanes_last(d_bi),
        "c_re": d_c.transpose(0, 2, 1).reshape(1, SSM_NG, SSM_GC, SSM_N),
        "c_im": (-d_ci).transpose(0, 2, 1).reshape(1, SSM_NG, SSM_GC, SSM_N),
        "d_skip": d_d_skip,
        "w_glu": d_w_glu,
        "g_mem": d_g_mem,
        "w_kv": d_w_kv,
        "w_out": d_w_out,
        "g_post": d_g_post,
    }
    return loss, grad_x.reshape(nb, l, D_MODEL), grads


VMEM_SPEC = pl.BlockSpec(memory_space=pltpu.VMEM)
ANY_SPEC = pl.BlockSpec(memory_space=pl.ANY)


def _place():
    return lax.axis_index("x"), lax.axis_index("y"), lax.axis_index("c")


def _other_chips(x, y):
    return [(1 - x, y), (x, 1 - y), (1 - x, 1 - y)]


SEMS_PER_ITEM = 5


def reduce_all(g4s, packed):
    n = len(g4s)
    dims = [g.shape[1:] for g in g4s]
    pr = packed.shape[0]

    def body(*refs):
        g_refs, p_ref = refs[:n], refs[n]
        outs, op = refs[n + 1:2 * n + 1], refs[2 * n + 1]
        scr = refs[2 * n + 2:]
        mine, theirs, sendb, recvb = scr[0:n], scr[n:2 * n], scr[2 * n:3 * n], scr[3 * n:4 * n]
        p_theirs, gat, lsems, send_sems, recv_sems = scr[4 * n:]
        x, y, c = _place()
        k = 2 * x + y
        chips = _other_chips(x, y)
        sib = (x, y, 1 - c)

        def halves(rows):
            hr = rows // 2
            return (pl.ds(pl.multiple_of(c * hr, SUBLANES), hr), pl.ds(pl.multiple_of((1 - c) * hr, SUBLANES), hr))

        def remote(src, dst, sem, to):
            return pltpu.make_async_remote_copy(
                src_ref=src, dst_ref=dst, send_sem=send_sems.at[sem], recv_sem=recv_sems.at[sem],
                device_id=to, device_id_type=MESH)

        loads, started = [], []
        for i in range(n):
            my_rows, sib_rows = halves(dims[i][0])
            ld = pltpu.make_async_copy(g_refs[i].at[:, my_rows, :], mine[i], lsems.at[i])
            ld.start()
            loads.append(ld)
            sw = remote(g_refs[i].at[:, sib_rows, :], theirs[i], SEMS_PER_ITEM * i, sib)
            sw.start()
            started.append(sw)
        p_my, p_sib = halves(pr)
        p_swap = remote(p_ref.at[p_sib, :], p_theirs, SEMS_PER_ITEM * n, sib)
        p_swap.start()
        started.append(p_swap)

        p_swap.wait_recv()
        gat[k] = p_ref[p_my, :] + p_theirs[...]
        for j, (px, py) in enumerate(chips):
            cp = remote(gat.at[k], gat.at[k], SEMS_PER_ITEM * n + 1 + j, (px, py, c))
            cp.start()
            started.append(cp)
        ici = []
        for i in range(n):
            loads[i].wait()
            started[i].wait_recv()
            for j, (px, py) in enumerate(chips):
                kp = 2 * px + py
                sendb[i][j] = (mine[i][kp] + theirs[i][kp]).astype(BF16)
                cp = remote(sendb[i].at[j], recvb[i].at[j], SEMS_PER_ITEM * i + 1 + j, (px, py, c))
                cp.start()
                ici.append(cp)

        for j, (px, py) in enumerate(chips):
            kp = 2 * px + py
            remote(gat.at[kp], gat.at[kp], SEMS_PER_ITEM * n + 1 + j, (px, py, c)).wait_recv()
        op[p_my, :] = ((gat[0] + gat[1]) + gat[2]) + gat[3]
        last = remote(op.at[p_my, :], op.at[p_my, :], SEMS_PER_ITEM * n + 4, sib)
        last.start()
        started.append(last)
        for i in range(n):
            my_rows, _ = halves(dims[i][0])
            half = mine[i][k] + theirs[i][k]
            for j in range(3):
                ici[3 * i + j].wait_recv()
                half = half + recvb[i][j].astype(F32)
            outs[i][my_rows, :] = half
            last = remote(outs[i].at[my_rows, :], outs[i].at[my_rows, :], SEMS_PER_ITEM * i + 4, sib)
            last.start()
            started.append(last)
        remote(op.at[p_sib, :], op.at[p_sib, :], SEMS_PER_ITEM * n + 4, sib).wait_recv()
        for i in range(n):
            _, sib_rows = halves(dims[i][0])
            remote(outs[i].at[sib_rows, :], outs[i].at[sib_rows, :], SEMS_PER_ITEM * i + 4, sib).wait_recv()
        for cp in started + ici:
            cp.wait_send()

    n_sems = SEMS_PER_ITEM * (n + 1)
    scratch = ([pltpu.VMEM((N_CHIPS, r // 2, cd), F32) for r, cd in dims] * 2
               + [pltpu.VMEM((3, r // 2, cd), BF16) for r, cd in dims] * 2
               + [pltpu.VMEM((pr // 2, LANES), F32), pltpu.VMEM((N_CHIPS, pr // 2, LANES), F32),
                  pltpu.SemaphoreType.DMA((n,)), pltpu.SemaphoreType.DMA((n_sems,)), pltpu.SemaphoreType.DMA((n_sems,))])
    res = pl.pallas_call(
        body,
        out_shape=[jax.ShapeDtypeStruct(d, F32) for d in dims] + [jax.ShapeDtypeStruct(packed.shape, F32)],
        in_specs=[ANY_SPEC] * n + [VMEM_SPEC], out_specs=[VMEM_SPEC] * (n + 1),
        scratch_shapes=scratch,
        compiler_params=pltpu.CompilerParams(vmem_limit_bytes=VMEM_LIMIT),
        name="reduce_all")(*g4s, packed)
    return res[:n], res[n]


def adamw_all(ws, gs, ms, vs):
    n = len(ws)

    def body(*refs):
        w, g, m, v = refs[:n], refs[n:2 * n], refs[2 * n:3 * n], refs[3 * n:4 * n]
        od, om, ov = refs[4 * n:5 * n], refs[5 * n:6 * n], refs[6 * n:]
        for i in range(n):
            od[i][...], om[i][...], ov[i][...] = _adamw(w[i][...], g[i][...], m[i][...], v[i][...])

    shapes = [jax.ShapeDtypeStruct(a.shape, F32) for a in ws]
    res = pl.pallas_call(
        body, out_shape=shapes * 3, in_specs=[VMEM_SPEC] * (4 * n), out_specs=[VMEM_SPEC] * (3 * n),
        compiler_params=pltpu.CompilerParams(vmem_limit_bytes=VMEM_LIMIT),
        name="adamw_all")(*ws, *gs, *ms, *vs)
    return res[:n], res[n:2 * n], res[2 * n:]


WEIGHTS = ("g_pre", "w_in", "w_pool", "pool_scale", "a_re", "a_im", "log_dt", "b_re", "b_im", "c_re", "c_im", "d_skip",
           "w_glu", "g_mem", "w_kv", "w_out", "g_post")
SHARDED = ("w_in", "w_glu", "w_kv", "w_out")
REPLICATED = tuple(n for n in WEIGHTS if n not in SHARDED)
PACK_TILE = SUBLANES * LANES


def _pack(arrays):
    parts = []
    for a in arrays:
        flat = a.reshape(-1)
        parts.append(jnp.pad(flat, (0, -flat.shape[0] % PACK_TILE)).reshape(-1, LANES))
    rows = sum(p.shape[0] for p in parts)
    if rows % (2 * SUBLANES):
        parts.append(jnp.zeros((SUBLANES, LANES), F32))
    return jnp.concatenate(parts, axis=0)


def _unpack(packed, shapes):
    out, row = [], 0
    for shp in shapes:
        size = math.prod(shp)
        rows = -(-size // PACK_TILE) * SUBLANES
        out.append(packed[row:row + rows].reshape(-1)[:size].reshape(shp))
        row += rows
    return out


def _chip_major(a, n_cols):
    return a.reshape(a.shape[0], N_CHIPS, n_cols).transpose(1, 0, 2)


def _from_chip_major(a):
    return a.transpose(1, 0, 2).reshape(a.shape[1], N_CHIPS * a.shape[2])


def kernel(x, mem, g_pre, w_in, w_pool, pool_scale, a_re, a_im, log_dt, b_re, b_im, c_re, c_im, d_skip, w_glu, g_mem, w_kv, w_out, g_post, loss_target, m_g_pre, m_w_in, m_w_pool, m_pool_scale, m_a_re, m_a_im, m_log_dt, m_b_re, m_b_im, m_c_re, m_c_im, m_d_skip, m_w_glu, m_g_mem, m_w_kv, m_w_out, m_g_post, v_g_pre, v_w_in, v_w_pool, v_pool_scale, v_a_re, v_a_im, v_log_dt, v_b_re, v_b_im, v_c_re, v_c_im, v_d_skip, v_w_glu, v_g_mem, v_w_kv, v_w_out, v_g_post):
    given = dict(locals())
    wts = {n: given[n] for n in WEIGHTS}
    mom = {n: given["m_" + n] for n in WEIGHTS}
    var = {n: given["v_" + n] for n in WEIGHTS}

    loss_part, grad_x, grads = local_step(
        x, mem, loss_target, g_pre, w_in[0], w_pool[0], pool_scale, a_re[0], a_im[0], log_dt[0], b_re[0], b_im[0],
        c_re[0], c_im[0], d_skip, [w_glu[0], w_kv[0], w_out[0]], g_mem, g_post)

    partial4 = [
        grads["w_in"],
        _chip_major(grads["w_glu"], 2 * SSM_W // N_CHIPS),
        grads["w_kv"].reshape(N_CHIPS, D_MODEL // N_CHIPS, 2 * ATT_W),
        grads["w_out"].reshape(N_CHIPS, D_MODEL // N_CHIPS, D_MODEL),
    ]
    loss_tile = jnp.broadcast_to(loss_part, (SUBLANES, LANES))
    sharded_g, packed_g = reduce_all(partial4, _pack([grads[n] for n in REPLICATED] + [loss_tile]))
    small_g = _unpack(packed_g, [wts[n].shape for n in REPLICATED] + [(SUBLANES, LANES)])
    loss = small_g[-1][0, 0]
    grad = dict(zip(SHARDED, [g[None] for g in sharded_g]))
    grad.update(zip(REPLICATED, small_g[:-1]))

    deltas, new_ms, new_vs = adamw_all([wts[n] for n in WEIGHTS], [grad[n] for n in WEIGHTS],
                                       [mom[n] for n in WEIGHTS], [var[n] for n in WEIGHTS])
    return (loss, grad_x, *[grad[n] for n in WEIGHTS], *deltas, *new_ms, *new_vs)
```

```python
import functools
import math

import jax
import jax.numpy as jnp
from jax import lax
from jax.experimental import pallas as pl
from jax.experimental.pallas import tpu as pltpu

F32 = jnp.float32
BF16 = jnp.bfloat16

D_MODEL = 1024
POOL_W = 384
SSM_W = 384
ATT_W = 256
POOL_GW = 96
POOL_WINDOWS = (2, 4, 8, 16)
POOL_PAD = 16
SSM_NG = 24
SSM_N = 64
SSM_GC = 16
N_CHIPS = 4
N_DEV = 8
W_IN_SHARD = 2 * D_MODEL // N_CHIPS
NST = SSM_NG * SSM_N
BLK_CH = 128
BLK_GROUPS = BLK_CH // SSM_GC
BLK_ST = BLK_GROUPS * SSM_N
N_BLK = SSM_W // BLK_CH
N_MEM = 256
MEM_HEADS = 4
MEM_HD = 64
EPS = 1e-6

ADAM_LR = 0.001
ADAM_B1 = 0.9
ADAM_B2 = 0.999
ADAM_EPS = 1e-08
ADAM_WD = 0.01
ADAM_STEP = 10

SUBLANES = 8
LANES = 128
V7X_VMEM_BYTES = 64 * 2**20
VMEM_LIMIT = V7X_VMEM_BYTES - 8 * 2**20
SCAN_COLS = 512
ROW_TILE = 256
IN_PROJ_ROWS = 1024
MESH = pl.DeviceIdType.MESH

NT = (((1,), (1,)), ((), ()))
TN = (((0,), (0,)), ((), ()))


def _params(*sem):
    return pltpu.CompilerParams(dimension_semantics=sem, vmem_limit_bytes=VMEM_LIMIT)


def _dot(a, b):
    return jnp.dot(a, b, preferred_element_type=F32)


def _dot_nt(a, b):
    return lax.dot_general(a, b, NT, preferred_element_type=F32)


def _dot_tn(a, b):
    return lax.dot_general(a, b, TN, preferred_element_type=F32)


def _rms_scale(v):
    return lax.rsqrt(jnp.mean(v * v, axis=-1, keepdims=True) + EPS)


def _adamw(w, g, m, v):
    m = ADAM_B1 * m + (1.0 - ADAM_B1) * g
    v = ADAM_B2 * v + (1.0 - ADAM_B2) * (g * g)
    m_hat = m / (1.0 - ADAM_B1 ** ADAM_STEP)
    v_hat = v / (1.0 - ADAM_B2 ** ADAM_STEP)
    delta = -ADAM_LR * (m_hat / (jnp.sqrt(v_hat) + ADAM_EPS) + ADAM_WD * w)
    return delta, m, v


def _phase_shard(j):
    cx, cy = lax.axis_index("x"), lax.axis_index("y")
    others = [2 * px + py for px, py in _other_chips(cx, cy)]
    own = 2 * cx + cy
    return jnp.where(j == 0, own, jnp.where(j == 1, others[0], jnp.where(j == 2, others[1], others[2])))


def in_proj(x2, g_pre, w_in_shard, later_shards):
    t = x2.shape[0]
    tm = min(IN_PROJ_ROWS, t)
    n_tiles = t // tm
    n = len(later_shards)

    def body(x_ref, g_ref, ws_ref, *rest):
        shard_refs = rest[:n]
        o_ref, w_out_ref, gathered = rest[n], rest[n + 1], rest[n + 2:2 * n + 2]
        w_buf, h_buf = rest[2 * n + 2], rest[2 * n + 3]
        cast_bufs = rest[2 * n + 4:3 * n + 4]
        local_sems, send_sems, recv_sems = rest[3 * n + 4:]
        j = pl.program_id(0)
        i = pl.program_id(1)
        cx, cy, cc = _place()
        k = 2 * cx + cy
        chips = _other_chips(cx, cy)

        half = w_in_shard.shape[0] // 2

        def w_copy(p, block, core_half, forward):
            px, py = chips[p]
            rows = w_buf.at[block, pl.ds(pl.multiple_of(core_half * half, 2 * SUBLANES), half)]
            sem = 3 + p if forward else p
            return pltpu.make_async_remote_copy(
                src_ref=rows, dst_ref=rows, send_sem=send_sems.at[sem], recv_sem=recv_sems.at[sem],
                device_id=(cx, cy, 1 - cc) if forward else (px, py, cc), device_id_type=MESH)

        def own(m):
            return pltpu.make_async_copy(cast_bufs[m], gathered[m].at[k], local_sems.at[m])

        def later_copy(m, p, block, core_half, forward):
            px, py = chips[p]
            hr = later_shards[m].shape[0] // 2
            rows = pl.ds(pl.multiple_of(core_half * hr, 2 * SUBLANES), hr)
            dst = gathered[m].at[block, rows]
            sem = 6 + 6 * m + (3 + p if forward else p)
            return pltpu.make_async_remote_copy(
                src_ref=dst if forward else cast_bufs[m].at[rows], dst_ref=dst, send_sem=send_sems.at[sem],
                recv_sem=recv_sems.at[sem], device_id=(cx, cy, 1 - cc) if forward else (px, py, cc),
                device_id_type=MESH)

        @pl.when((j == 0) & (i == 0))
        def _():
            w_buf[k] = ws_ref[...].astype(BF16)
            for p in range(3):
                w_copy(p, k, cc, False).start()
            for m in range(n):
                cast_bufs[m][...] = shard_refs[m][...].astype(BF16)
                own(m).start()
                for p in range(3):
                    later_copy(m, p, k, cc, False).start()

        rows = pl.ds(pl.multiple_of(i * tm, tm), tm)

        @pl.when(j == 0)
        def _():
            x = x_ref[...]
            h_buf[rows, :] = (x * _rms_scale(x) * g_ref[...]).astype(BF16)

        for p in range(3):
            @pl.when((j == p + 1) & (i == 0))
            def _(p=p):
                px, py = chips[p]
                w_copy(p, 2 * px + py, cc, False).wait_recv()
                w_copy(p, 2 * px + py, cc, True).start()
                w_copy(p, 2 * px + py, 1 - cc, True).wait_recv()

        o_ref[...] = _dot(h_buf[rows, :], w_buf[_phase_shard(j)]).astype(BF16)

        @pl.when((j == N_CHIPS - 1) & (i == n_tiles - 1))
        def _():
            out_cp = pltpu.make_async_copy(w_buf, w_out_ref, local_sems.at[n])
            out_cp.start()
            for m in range(n):
                own(m).wait()
                for p in range(3):
                    px, py = chips[p]
                    later_copy(m, p, 2 * px + py, cc, False).wait_recv()
                    later_copy(m, p, 2 * px + py, cc, True).start()
            for m in range(n):
                for p in range(3):
                    px, py = chips[p]
                    later_copy(m, p, 2 * px + py, 1 - cc, True).wait_recv()
            for p in range(3):
                px, py = chips[p]
                w_copy(p, k, cc, False).wait_send()
                w_copy(p, 2 * px + py, cc, True).wait_send()
            for m in range(n):
                for p in range(3):
                    px, py = chips[p]
                    later_copy(m, p, k, cc, False).wait_send()
                    later_copy(m, p, 2 * px + py, cc, True).wait_send()
            out_cp.wait()

    whole = lambda a: pl.BlockSpec(a.shape, lambda j, i: (0,) * a.ndim)
    res = pl.pallas_call(
        body, grid=(N_CHIPS, n_tiles),
        in_specs=[pl.BlockSpec((tm, D_MODEL), lambda j, i: (jnp.where(j == 0, i, n_tiles - 1), 0)),
                  pl.BlockSpec((1, D_MODEL), lambda j, i: (0, 0)),
                  whole(w_in_shard)] + [whole(s) for s in later_shards],
        out_specs=[pl.BlockSpec((tm, W_IN_SHARD), lambda j, i: (i, _phase_shard(j)))] + [ANY_SPEC] * (n + 1),
        out_shape=([jax.ShapeDtypeStruct((t, 2 * D_MODEL), BF16),
                    jax.ShapeDtypeStruct((N_CHIPS,) + w_in_shard.shape, BF16)]
                   + [jax.ShapeDtypeStruct((N_CHIPS,) + s.shape, BF16) for s in later_shards]),
        scratch_shapes=([pltpu.VMEM((N_CHIPS,) + w_in_shard.shape, BF16), pltpu.VMEM((t, D_MODEL), BF16)]
                        + [pltpu.VMEM(s.shape, BF16) for s in later_shards]
                        + [pltpu.SemaphoreType.DMA((n + 1,)), pltpu.SemaphoreType.DMA((6 * n + 6,)),
                           pltpu.SemaphoreType.DMA((6 * n + 6,))]),
        compiler_params=_params("arbitrary", "arbitrary"), name="in_proj")(x2, g_pre, w_in_shard, *later_shards)
    return res[0], res[1], res[2:]


def _pool_lane_window(shape):
    ch = lax.broadcasted_iota(jnp.int32, shape, 1)
    return jnp.where(ch < POOL_GW, 2.0, jnp.where(ch < 2 * POOL_GW, 4.0, jnp.where(ch < 3 * POOL_GW, 8.0, 16.0)))


def _pool_select(win, s2, s4, s8, s16):
    return jnp.where(win == 2.0, s2, jnp.where(win == 4.0, s4, jnp.where(win == 8.0, s8, s16)))


def _pool_div_count(v, win):
    pos = (lax.broadcasted_iota(jnp.int32, (POOL_PAD, POOL_W), 0) + 1).astype(F32)
    head = v[:POOL_PAD] / jnp.minimum(pos, win)
    return jnp.concatenate([head, v[POOL_PAD:] * (1.0 / win)], axis=0)


def _pool_diff(u, pad_ref, l):
    lo = POOL_PAD
    pad_ref[pl.ds(lo, l), :] = u
    s2 = u + pad_ref[pl.ds(lo - 1, l), :]
    pad_ref[pl.ds(lo, l), :] = s2
    s4 = s2 + pad_ref[pl.ds(lo - 2, l), :]
    pad_ref[pl.ds(lo, l), :] = s4
    s8 = s4 + pad_ref[pl.ds(lo - 4, l), :]
    pad_ref[pl.ds(lo, l), :] = s8
    s16 = s8 + pad_ref[pl.ds(lo - 8, l), :]
    win = _pool_lane_window((1, POOL_W))
    return _pool_div_count(_pool_select(win, s2, s4, s8, s16), win) - u, win


def pool_fwd(proj, w_blk, pool_scale, nb, l):
    def body(u_ref, w_ref, ps_ref, y_ref, pad_ref):
        pad_ref[pl.ds(0, POOL_PAD), :] = jnp.zeros((POOL_PAD, POOL_W), F32)
        d, _ = _pool_diff(u_ref[...].astype(F32), pad_ref, l)
        y_ref[...] = (_dot(d.astype(BF16), w_ref[...]) * ps_ref[...]).astype(BF16)

    return pl.pallas_call(
        body, grid=(nb,),
        in_specs=[pl.BlockSpec((l, POOL_W), lambda b: (b, 0)),
                  pl.BlockSpec((POOL_W, POOL_W), lambda b: (0, 0)),
                  pl.BlockSpec((1, POOL_W), lambda b: (0, 0))],
        out_specs=pl.BlockSpec((l, POOL_W), lambda b: (b, 0)),
        out_shape=jax.ShapeDtypeStruct((nb * l, POOL_W), BF16),
        scratch_shapes=[pltpu.VMEM((POOL_PAD + l, POOL_W), F32)],
        compiler_params=_params("arbitrary"), name="pool_fwd")(proj, w_blk, pool_scale)


def pool_bwd(proj, d_ycat, w_blk, pool_scale, nb, l):
    def body(u_ref, dy_ref, w_ref, ps_ref, du_ref, dw_ref, dps_ref, pad_ref, padb_ref):
        b = pl.program_id(0)

        @pl.when(b == 0)
        def _():
            dw_ref[...] = jnp.zeros_like(dw_ref)
            dps_ref[...] = jnp.zeros_like(dps_ref)

        pad_ref[pl.ds(0, POOL_PAD), :] = jnp.zeros((POOL_PAD, POOL_W), F32)
        padb_ref[pl.ds(l, POOL_PAD), :] = jnp.zeros((POOL_PAD, POOL_W), F32)
        d, win = _pool_diff(u_ref[...].astype(F32), pad_ref, l)
        db = d.astype(BF16)
        w = w_ref[...]
        dy = dy_ref[...].astype(F32)
        dps_ref[...] += jnp.sum(dy * _dot(db, w), axis=0, keepdims=True)
        dyo = (dy * ps_ref[...]).astype(BF16)
        dw_ref[...] += _dot_tn(db, dyo)
        dd = _dot_nt(dyo, w)
        e = _pool_div_count(dd, win)
        padb_ref[pl.ds(0, l), :] = e
        f2 = e + padb_ref[pl.ds(1, l), :]
        padb_ref[pl.ds(0, l), :] = f2
        f4 = f2 + padb_ref[pl.ds(2, l), :]
        padb_ref[pl.ds(0, l), :] = f4
        f8 = f4 + padb_ref[pl.ds(4, l), :]
        padb_ref[pl.ds(0, l), :] = f8
        f16 = f8 + padb_ref[pl.ds(8, l), :]
        du_ref[...] = (_pool_select(win, f2, f4, f8, f16) - dd).astype(BF16)

    return pl.pallas_call(
        body, grid=(nb,),
        in_specs=[pl.BlockSpec((l, POOL_W), lambda b: (b, 0)),
                  pl.BlockSpec((l, POOL_W), lambda b: (b, 0)),
                  pl.BlockSpec((POOL_W, POOL_W), lambda b: (0, 0)),
                  pl.BlockSpec((1, POOL_W), lambda b: (0, 0))],
        out_specs=[pl.BlockSpec((l, POOL_W), lambda b: (b, 0)),
                   pl.BlockSpec((POOL_W, POOL_W), lambda b: (0, 0)),
                   pl.BlockSpec((1, POOL_W), lambda b: (0, 0))],
        out_shape=[jax.ShapeDtypeStruct((nb * l, POOL_W), BF16),
                   jax.ShapeDtypeStruct((POOL_W, POOL_W), F32),
                   jax.ShapeDtypeStruct((1, POOL_W), F32)],
        scratch_shapes=[pltpu.VMEM((POOL_PAD + l, POOL_W), F32), pltpu.VMEM((POOL_PAD + l, POOL_W), F32)],
        compiler_params=_params("arbitrary"), name="pool_bwd")(proj, d_ycat, w_blk, pool_scale)


def _discretise(a_re, a_im, ldt, b_re, b_im):
    dt = jnp.exp(ldt)
    mag = jnp.exp(a_re * dt)
    th = a_im * dt
    lr = mag * jnp.cos(th)
    li = mag * jnp.sin(th)
    nr = lr - 1.0
    den = a_re * a_re + a_im * a_im
    fr = (nr * a_re + li * a_im) / den
    fi = (li * a_re - nr * a_im) / den
    return lr, li, fr * b_re - fi * b_im, fr * b_im + fi * b_re


def _cmul(ar, ai, br, bi):
    return ar * br - ai * bi, ar * bi + ai * br


def s5_scan_consts(a_re_row, a_im_row, ldt_row):
    def body(ar_ref, ai_ref, ld_ref, f_ref, b_ref):
        dt = jnp.exp(ld_ref[...])
        mag = jnp.exp(ar_ref[...] * dt)
        th = ai_ref[...] * dt
        lr = mag * jnp.cos(th)
        li = mag * jnp.sin(th)
        r = lax.broadcasted_iota(jnp.int32, (SUBLANES, NST), 0)
        for out_ref, sign, rev in ((f_ref, 1.0, False), (b_ref, -1.0, True)):
            p = [(lr, sign * li)]
            for _ in range(SUBLANES - 1):
                p.append(_cmul(p[-1][0], p[-1][1], lr, sign * li))
            for idx, s in enumerate((1, 2, 4)):
                inside = (r < SUBLANES - s) if rev else (r >= s)
                out_ref[pl.ds(2 * idx * SUBLANES, SUBLANES), :] = jnp.where(inside, p[s - 1][0], 0.0)
                out_ref[pl.ds((2 * idx + 1) * SUBLANES, SUBLANES), :] = jnp.where(inside, p[s - 1][1], 0.0)
            pr = jnp.zeros((SUBLANES, NST), F32)
            pi = jnp.zeros((SUBLANES, NST), F32)
            for row in range(SUBLANES):
                power = (SUBLANES - row) if rev else (row + 1)
                pr = jnp.where(r == row, p[power - 1][0], pr)
                pi = jnp.where(r == row, p[power - 1][1], pi)
            out_ref[pl.ds(6 * SUBLANES, SUBLANES), :] = pr
            out_ref[pl.ds(7 * SUBLANES, SUBLANES), :] = pi

    shape = jax.ShapeDtypeStruct((8 * SUBLANES, NST), F32)
    return pl.pallas_call(body, out_shape=[shape, shape], name="s5_scan_consts")(a_re_row, a_im_row, ldt_row)


def s5_bbar(a_re_row, a_im_row, ldt_row, b_re_t, b_im_t):
    def body(ar, ai, ld, br, bi, o_re, o_im):
        _, _, bbr, bbi = _discretise(ar[...], ai[...], ld[...], br[...], bi[...])
        o_re[...] = bbr
        o_im[...] = bbi

    shape = jax.ShapeDtypeStruct((SSM_GC, NST), F32)
    return pl.pallas_call(body, out_shape=[shape, shape], name="s5_bbar")(a_re_row, a_im_row, ldt_row, b_re_t, b_im_t)


def s5_param_bwd(a_re_row, a_im_row, ldt_row, b_re_t, b_im_t, d_lr, d_li, d_bbr, d_bbi):
    def body(ar, ai, ld, br, bi, g_lr, g_li, g_br, g_bi, o_ar, o_ai, o_ld, o_br, o_bi):
        _, vjp = jax.vjp(_discretise, ar[...], ai[...], ld[...], br[...], bi[...])
        d_ar, d_ai, d_ld, d_br, d_bi = vjp((g_lr[...], g_li[...], g_br[...], g_bi[...]))
        o_ar[...] = d_ar
        o_ai[...] = d_ai
        state = lax.broadcasted_iota(jnp.int32, (NST, LANES), 0)
        lane = lax.broadcasted_iota(jnp.int32, (NST, LANES), 1)
        in_group = jnp.where((state >= lane * SSM_N) & (state < (lane + 1) * SSM_N), 1.0, 0.0)
        o_ld[...] = jnp.dot(d_ld, in_group, precision=lax.Precision.HIGHEST, preferred_element_type=F32)
        o_br[...] = d_br
        o_bi[...] = d_bi

    row = jax.ShapeDtypeStruct((1, NST), F32)
    mat = jax.ShapeDtypeStruct((SSM_GC, NST), F32)
    grp = jax.ShapeDtypeStruct((1, LANES), F32)
    return pl.pallas_call(body, out_shape=[row, row, grp, mat, mat], name="s5_param_bwd")(
        a_re_row, a_im_row, ldt_row, b_re_t, b_im_t, d_lr, d_li, d_bbr, d_bbi)


def _scan_tiles(buf_ref, cst_ref, carry_ref, rows, reverse, reset, h_ref=None, acc_ref=None):
    n_tiles = rows // SUBLANES
    shifts = ((1, 0), (2, 2), (4, 4))
    row_id = lax.broadcasted_iota(jnp.int32, (SUBLANES, SCAN_COLS), 0)
    cols = [(j * SCAN_COLS, NST + j * SCAN_COLS) for j in range(NST // SCAN_COLS)]
    edge = 0 if reverse else SUBLANES - 1
    carry = jnp.where(reset, 0.0, carry_ref[...])
    accs = [(acc_ref[:, pl.ds(o_re, SCAN_COLS)], acc_ref[:, pl.ds(o_im, SCAN_COLS)]) for o_re, o_im in cols] if reverse else None
    for i in range(n_tiles):
        start = ((n_tiles - 1 - i) if reverse else i) * SUBLANES
        rws = pl.ds(start, SUBLANES)
        for j, (o_re, o_im) in enumerate(cols):
            c_re = pl.ds(o_re, SCAN_COLS)
            c_im = pl.ds(o_im, SCAN_COLS)
            xr = buf_ref[rws, c_re]
            xi = buf_ref[rws, c_im]
            for s, base in shifts:
                amount = (SUBLANES - s) if reverse else s
                sr = pltpu.roll(xr, amount, 0)
                si = pltpu.roll(xi, amount, 0)
                mr = cst_ref[pl.ds(base * SUBLANES, SUBLANES), c_re]
                mi = cst_ref[pl.ds((base + 1) * SUBLANES, SUBLANES), c_re]
                xr, xi = xr + (mr * sr - mi * si), xi + (mr * si + mi * sr)
            pr = cst_ref[pl.ds(6 * SUBLANES, SUBLANES), c_re]
            pi = cst_ref[pl.ds(7 * SUBLANES, SUBLANES), c_re]
            cr = carry[:, o_re:o_re + SCAN_COLS]
            ci = carry[:, o_im:o_im + SCAN_COLS]
            xr, xi = xr + (pr * cr - pi * ci), xi + (pr * ci + pi * cr)
            buf_ref[rws, c_re] = xr
            buf_ref[rws, c_im] = xi
            if reverse:
                gnr = jnp.where(row_id == SUBLANES - 1, cr, pltpu.roll(xr, SUBLANES - 1, 0))
                gni = jnp.where(row_id == SUBLANES - 1, ci, pltpu.roll(xi, SUBLANES - 1, 0))
                hr = h_ref[rws, c_re]
                hi = h_ref[rws, c_im]
                accs[j] = (accs[j][0] + (gnr * hr + gni * hi), accs[j][1] + (gni * hr - gnr * hi))
        carry = jnp.broadcast_to(buf_ref[pl.ds(start + edge, 1), :], (SUBLANES, 2 * NST))
    carry_ref[...] = carry
    if reverse:
        for (o_re, o_im), (a_re_, a_im_) in zip(cols, accs):
            acc_ref[:, pl.ds(o_re, SCAN_COLS)] = a_re_
            acc_ref[:, pl.ds(o_im, SCAN_COLS)] = a_im_


def _state_block(v, m):
    return jnp.concatenate([v[:, m * BLK_ST:(m + 1) * BLK_ST], v[:, NST + m * BLK_ST:NST + (m + 1) * BLK_ST]], axis=1)


def _put_state_block(buf_ref, m, val):
    buf_ref[:, pl.ds(m * BLK_ST, BLK_ST)] = val[:, :BLK_ST]
    buf_ref[:, pl.ds(NST + m * BLK_ST, BLK_ST)] = val[:, BLK_ST:]


def s5_fwd(proj, b_m, c_m, cst, d_skip, w_glu, nb, l):
    tt = min(256, l)
    nt = l // tt
    n_chunks = nb * nt

    def body(u_ref, b_ref, c_ref, cst_ref, ds_ref, wg_ref, y_ref, h_ref, yp_ref, z_ref,
             buf0, buf1, ub0, ub1, carry_ref):
        s = pl.program_id(0)

        @pl.when(s == 0)
        def _():
            for r in (buf0, buf1, ub0, ub1, carry_ref):
                r[...] = jnp.zeros_like(r)

        def step(p_buf, p_u, q_buf):
            h = p_buf[...].astype(BF16)
            h_ref[...] = h
            ypre = (jnp.concatenate([_dot(_state_block(h, m), c_ref[m]) for m in range(N_BLK)], axis=1)
                    + ds_ref[...] * p_u[...])
            yp_ref[...] = ypre
            z = _dot(jax.nn.gelu(ypre).astype(BF16), wg_ref[...])
            z_ref[...] = z
            y_ref[...] = (z[:, :SSM_W] * jax.nn.sigmoid(z[:, SSM_W:])).astype(BF16)
            ub = u_ref[...]
            p_u[...] = ub.astype(F32)
            for m in range(N_BLK):
                _put_state_block(p_buf, m, _dot(ub[:, m * BLK_CH:(m + 1) * BLK_CH], b_ref[m]))
            _scan_tiles(q_buf, cst_ref, carry_ref, tt, False, lax.rem(s + nt - 1, nt) == 0)

        @pl.when(lax.rem(s, 2) == 0)
        def _():
            step(buf0, ub0, buf1)

        @pl.when(lax.rem(s, 2) == 1)
        def _():
            step(buf1, ub1, buf0)

    row_in = lambda s: (jnp.minimum(s, n_chunks - 1), 1)
    row_out = lambda s: (jnp.maximum(s - 2, 0), 0)
    const = lambda s: (0, 0)
    const3 = lambda s: (0, 0, 0)
    return pl.pallas_call(
        body, grid=(n_chunks + 2,),
        in_specs=[pl.BlockSpec((tt, SSM_W), row_in),
                  pl.BlockSpec((N_BLK, BLK_CH, 2 * BLK_ST), const3),
                  pl.BlockSpec((N_BLK, 2 * BLK_ST, BLK_CH), const3),
                  pl.BlockSpec((8 * SUBLANES, NST), const),
                  pl.BlockSpec((1, SSM_W), const),
                  pl.BlockSpec((SSM_W, 2 * SSM_W), const)],
        out_specs=[pl.BlockSpec((tt, SSM_W), row_out),
                   pl.BlockSpec((tt, 2 * NST), row_out),
                   pl.BlockSpec((tt, SSM_W), row_out),
                   pl.BlockSpec((tt, 2 * SSM_W), row_out)],
        out_shape=[jax.ShapeDtypeStruct((nb * l, SSM_W), BF16),
                   jax.ShapeDtypeStruct((nb * l, 2 * NST), BF16),
                   jax.ShapeDtypeStruct((nb * l, SSM_W), F32),
                   jax.ShapeDtypeStruct((nb * l, 2 * SSM_W), F32)],
        scratch_shapes=[pltpu.VMEM((tt, 2 * NST), F32), pltpu.VMEM((tt, 2 * NST), F32),
                        pltpu.VMEM((tt, SSM_W), F32), pltpu.VMEM((tt, SSM_W), F32),
                        pltpu.VMEM((SUBLANES, 2 * NST), F32)],
        compiler_params=_params("arbitrary"), name="s5_fwd")(proj, b_m, c_m, cst, d_skip, w_glu)


def s5_bwd(proj, d_ycat, h, ypre, z, b_m, c_m, cst_rev, d_skip, w_glu, nb, l):
    tt = min(256, l)
    nt = l // tt
    n_chunks = nb * nt

    def body(u_ref, dy_ref, h_ref, yp_ref, z_ref, b_ref, c_ref, cst_ref, ds_ref, wg_ref,
             du_ref, dwg_ref, dds_ref, dcc_ref, dbb_ref, dlam_ref,
             buf0, buf1, hb0, hb1, ub0, ub1, dyp0, dyp1, carry_ref, dc_ref, db_ref):
        s = pl.program_id(0)

        @pl.when(s == 0)
        def _():
            for r in (buf0, buf1, hb0, hb1, ub0, ub1, dyp0, dyp1, carry_ref, dc_ref, db_ref, dwg_ref, dds_ref, dlam_ref):
                r[...] = jnp.zeros_like(r)

        def step(p_buf, p_h, p_u, p_dyp, q_buf, q_h):
            g = p_buf[...].astype(BF16)
            ub_done = p_u[...].astype(BF16)
            du = []
            for m in range(N_BLK):
                g_m = _state_block(g, m)
                db_ref[m] += _dot_tn(g_m, ub_done[:, m * BLK_CH:(m + 1) * BLK_CH])
                du.append(_dot_nt(g_m, b_ref[m]))
            du_ref[...] = (jnp.concatenate(du, axis=1) + ds_ref[...] * p_dyp[...]).astype(BF16)
            u = u_ref[...].astype(F32)
            ypre = yp_ref[...]
            z = z_ref[...]
            z1 = z[:, :SSM_W]
            sg = jax.nn.sigmoid(z[:, SSM_W:])
            dy = dy_ref[...].astype(F32) * jnp.where(s < n_chunks, 1.0, 0.0)
            dz = jnp.concatenate([dy * sg, dy * z1 * sg * (1.0 - sg)], axis=1).astype(BF16)
            yg, gelu_vjp = jax.vjp(jax.nn.gelu, ypre)
            dwg_ref[...] += _dot_tn(yg.astype(BF16), dz)
            dypre = gelu_vjp(_dot_nt(dz, wg_ref[...]))[0]
            dds_ref[...] += jnp.sum(dypre * u, axis=0, keepdims=True)
            dyb = dypre.astype(BF16)
            hb = h_ref[...]
            p_h[...] = hb.astype(F32)
            p_u[...] = u
            p_dyp[...] = dypre
            for m in range(N_BLK):
                dy_m = dyb[:, m * BLK_CH:(m + 1) * BLK_CH]
                dc_ref[m] += _dot_tn(_state_block(hb, m), dy_m)
                _put_state_block(p_buf, m, _dot_nt(dy_m, c_ref[m]))
            _scan_tiles(q_buf, cst_ref, carry_ref, tt, True, lax.rem(s + nt - 1, nt) == 0, h_ref=q_h, acc_ref=dlam_ref)

        @pl.when(lax.rem(s, 2) == 0)
        def _():
            step(buf0, hb0, ub0, dyp0, buf1, hb1)

        @pl.when(lax.rem(s, 2) == 1)
        def _():
            step(buf1, hb1, ub1, dyp1, buf0, hb0)

        @pl.when(s == n_chunks + 1)
        def _():
            dlam_ref[...] = jnp.broadcast_to(jnp.sum(dlam_ref[...], axis=0, keepdims=True), dlam_ref.shape)
            for acc_ref, out_ref in ((dc_ref, dcc_ref), (db_ref, dbb_ref)):
                for m in range(N_BLK):
                    for ri in range(2):
                        for gl in range(BLK_GROUPS):
                            out_ref[ri, pl.ds((m * BLK_GROUPS + gl) * SSM_N, SSM_N), :] = acc_ref[
                                m, pl.ds(ri * BLK_ST + gl * SSM_N, SSM_N), pl.ds(gl * SSM_GC, SSM_GC)]

    def chunk_rows(o):
        return lax.div(o, nt) * nt + (nt - 1 - lax.rem(o, nt))

    def rrow(col):
        return lambda s: (chunk_rows(jnp.minimum(s, n_chunks - 1)), col)

    const = lambda s: (0, 0)
    const3 = lambda s: (0, 0, 0)
    state_buf = pltpu.VMEM((tt, 2 * NST), F32)
    chan_buf = pltpu.VMEM((tt, SSM_W), F32)
    return pl.pallas_call(
        body, grid=(n_chunks + 2,),
        in_specs=[pl.BlockSpec((tt, SSM_W), rrow(1)),
                  pl.BlockSpec((tt, SSM_W), rrow(1)),
                  pl.BlockSpec((tt, 2 * NST), rrow(0)),
                  pl.BlockSpec((tt, SSM_W), rrow(0)),
                  pl.BlockSpec((tt, 2 * SSM_W), rrow(0)),
                  pl.BlockSpec((N_BLK, BLK_CH, 2 * BLK_ST), const3),
                  pl.BlockSpec((N_BLK, 2 * BLK_ST, BLK_CH), const3),
                  pl.BlockSpec((8 * SUBLANES, NST), const),
                  pl.BlockSpec((1, SSM_W), const),
                  pl.BlockSpec((SSM_W, 2 * SSM_W), const)],
        out_specs=[pl.BlockSpec((tt, SSM_W), lambda s: (chunk_rows(jnp.maximum(s - 2, 0)), 0)),
                   pl.BlockSpec((SSM_W, 2 * SSM_W), const),
                   pl.BlockSpec((1, SSM_W), const),
                   pl.BlockSpec((2, NST, SSM_GC), const3),
                   pl.BlockSpec((2, NST, SSM_GC), const3),
                   pl.BlockSpec((SUBLANES, 2 * NST), const)],
        out_shape=[jax.ShapeDtypeStruct((nb * l, SSM_W), BF16),
                   jax.ShapeDtypeStruct((SSM_W, 2 * SSM_W), F32),
                   jax.ShapeDtypeStruct((1, SSM_W), F32),
                   jax.ShapeDtypeStruct((2, NST, SSM_GC), F32),
                   jax.ShapeDtypeStruct((2, NST, SSM_GC), F32),
                   jax.ShapeDtypeStruct((SUBLANES, 2 * NST), F32)],
        scratch_shapes=[state_buf, state_buf, state_buf, state_buf, chan_buf, chan_buf, chan_buf, chan_buf,
                        pltpu.VMEM((SUBLANES, 2 * NST), F32),
                        pltpu.VMEM((N_BLK, 2 * BLK_ST, BLK_CH), F32), pltpu.VMEM((N_BLK, 2 * BLK_ST, BLK_CH), F32)],
        compiler_params=_params("arbitrary"), name="s5_bwd")(
            proj, d_ycat, h, ypre, z, b_m, c_m, cst_rev, d_skip, w_glu)


def _head_mask(hh):
    lane = lax.broadcasted_iota(jnp.int32, (1, ATT_W), 1)
    return (lane >= hh * MEM_HD) & (lane < (hh + 1) * MEM_HD)


def _mem_kv(mem_ref, gm_ref, wkv_ref):
    m = mem_ref[0]
    mh = m * _rms_scale(m)
    mb = (mh * gm_ref[...]).astype(BF16)
    kv = _dot(mb, wkv_ref[...])
    return mh, mb, kv[:, :ATT_W].astype(BF16), kv[:, ATT_W:].astype(BF16)


def _stack_heads(dst_ref, a):
    for hh in range(MEM_HEADS):
        dst_ref[pl.ds(hh * N_MEM, N_MEM), :] = jnp.where(_head_mask(hh), a, jnp.zeros_like(a))


def _fold_heads(a):
    out = jnp.zeros((N_MEM, ATT_W), a.dtype)
    for hh in range(MEM_HEADS):
        out = out + jnp.where(_head_mask(hh), a[hh * N_MEM:(hh + 1) * N_MEM], 0.0)
    return out


def _softmax_heads(s):
    parts = []
    for hh in range(MEM_HEADS):
        sh = s[:, hh * N_MEM:(hh + 1) * N_MEM]
        e = jnp.exp(sh - jnp.max(sh, axis=-1, keepdims=True))
        parts.append(e / jnp.sum(e, axis=-1, keepdims=True))
    return jnp.concatenate(parts, axis=1)


def attn_fwd(proj, mem, g_mem, w_kv, nb, l):
    tq = min(512, l)
    nq = l // tq
    scale = MEM_HD ** -0.5

    def body(q_ref, mem_ref, gm_ref, wkv_ref, o_ref, k_s, v_s):
        @pl.when(pl.program_id(1) == 0)
        def _():
            _, _, k, v = _mem_kv(mem_ref, gm_ref, wkv_ref)
            _stack_heads(k_s, k)
            _stack_heads(v_s, v)

        p = _softmax_heads(_dot_nt(q_ref[...], k_s[...]) * scale)
        o_ref[...] = _dot(p.astype(BF16), v_s[...]).astype(BF16)

    const = lambda b, t: (0, 0)
    stacked = pltpu.VMEM((MEM_HEADS * N_MEM, ATT_W), BF16)
    return pl.pallas_call(
        body, grid=(nb, nq),
        in_specs=[pl.BlockSpec((tq, ATT_W), lambda b, t: (b * nq + t, 3)),
                  pl.BlockSpec((1, N_MEM, D_MODEL), lambda b, t: (b, 0, 0)),
                  pl.BlockSpec((1, D_MODEL), const),
                  pl.BlockSpec((D_MODEL, 2 * ATT_W), const)],
        out_specs=pl.BlockSpec((tq, ATT_W), lambda b, t: (b * nq + t, 0)),
        out_shape=jax.ShapeDtypeStruct((nb * l, ATT_W), BF16),
        scratch_shapes=[stacked, stacked],
        compiler_params=_params("arbitrary", "arbitrary"), name="attn_fwd")(proj, mem, g_mem, w_kv)


def attn_bwd(proj, d_ycat, mem, g_mem, w_kv, nb, l):
    tq = min(512, l)
    nq = l // tq
    scale = MEM_HD ** -0.5

    def body(q_ref, do_ref, mem_ref, gm_ref, wkv_ref, dq_ref, dwkv_ref, dgm_ref, k_s, v_s, mb_s, dk_s, dv_s):
        b = pl.program_id(0)
        t = pl.program_id(1)

        @pl.when((b == 0) & (t == 0))
        def _():
            dwkv_ref[...] = jnp.zeros_like(dwkv_ref)
            dgm_ref[...] = jnp.zeros_like(dgm_ref)

        @pl.when(t == 0)
        def _():
            _, mb, k, v = _mem_kv(mem_ref, gm_ref, wkv_ref)
            _stack_heads(k_s, k)
            _stack_heads(v_s, v)
            mb_s[...] = mb
            dk_s[...] = jnp.zeros_like(dk_s)
            dv_s[...] = jnp.zeros_like(dv_s)

        q = q_ref[...]
        do = do_ref[...]
        k = k_s[...]
        p = _softmax_heads(_dot_nt(q, k) * scale)
        dp = _dot_nt(do, v_s[...])
        ds = []
        for hh in range(MEM_HEADS):
            blk = slice(hh * N_MEM, (hh + 1) * N_MEM)
            ds.append(p[:, blk] * (dp[:, blk] - jnp.sum(dp[:, blk] * p[:, blk], axis=-1, keepdims=True)) * scale)
        ds = jnp.concatenate(ds, axis=1).astype(BF16)
        dq_ref[...] = _dot(ds, k).astype(BF16)
        dk_s[...] += _dot_tn(ds, q)
        dv_s[...] += _dot_tn(p.astype(BF16), do)

        @pl.when(t == nq - 1)
        def _():
            dkv = jnp.concatenate([_fold_heads(dk_s[...]), _fold_heads(dv_s[...])], axis=1).astype(BF16)
            dwkv_ref[...] += _dot_tn(mb_s[...], dkv)
            m = mem_ref[0]
            dgm_ref[...] += jnp.sum(_dot_nt(dkv, wkv_ref[...]) * (m * _rms_scale(m)), axis=0, keepdims=True)

    const = lambda b, t: (0, 0)
    return pl.pallas_call(
        body, grid=(nb, nq),
        in_specs=[pl.BlockSpec((tq, ATT_W), lambda b, t: (b * nq + t, 3)),
                  pl.BlockSpec((tq, ATT_W), lambda b, t: (b * nq + t, 3)),
                  pl.BlockSpec((1, N_MEM, D_MODEL), lambda b, t: (b, 0, 0)),
                  pl.BlockSpec((1, D_MODEL), const),
                  pl.BlockSpec((D_MODEL, 2 * ATT_W), const)],
        out_specs=[pl.BlockSpec((tq, ATT_W), lambda b, t: (b * nq + t, 0)),
                   pl.BlockSpec((D_MODEL, 2 * ATT_W), const),
                   pl.BlockSpec((1, D_MODEL), const)],
        out_shape=[jax.ShapeDtypeStruct((nb * l, ATT_W), BF16),
                   jax.ShapeDtypeStruct((D_MODEL, 2 * ATT_W), F32),
                   jax.ShapeDtypeStruct((1, D_MODEL), F32)],
        scratch_shapes=[pltpu.VMEM((MEM_HEADS * N_MEM, ATT_W), BF16), pltpu.VMEM((MEM_HEADS * N_MEM, ATT_W), BF16),
                        pltpu.VMEM((N_MEM, D_MODEL), BF16),
                        pltpu.VMEM((MEM_HEADS * N_MEM, ATT_W), F32), pltpu.VMEM((MEM_HEADS * N_MEM, ATT_W), F32)],
        compiler_params=_params("arbitrary", "arbitrary"), name="attn_bwd")(proj, d_ycat, mem, g_mem, w_kv)


def tail(y_pool, y_ssm, y_att, proj, x2, target, w_out, g_post):
    t = x2.shape[0]
    tm = min(ROW_TILE, t)

    def body(yp_ref, ys_ref, ya_ref, gate_ref, x_ref, tg_ref, w_ref, gp_ref,
             dz_ref, dyc_ref, dgate_ref, dw_ref, dgp_ref, loss_ref):
        @pl.when(pl.program_id(0) == 0)
        def _():
            dw_ref[...] = jnp.zeros_like(dw_ref)
            dgp_ref[...] = jnp.zeros_like(dgp_ref)
            loss_ref[...] = jnp.zeros_like(loss_ref)

        ycat = jnp.concatenate([yp_ref[...], ys_ref[...], ya_ref[...]], axis=1).astype(F32)
        gate = gate_ref[...].astype(F32)
        sg = jax.nn.sigmoid(gate)
        silu = gate * sg
        yb = (ycat * silu).astype(BF16)
        w = w_ref[...]
        out = _dot(yb, w)
        r2 = _rms_scale(out)
        oh = out * r2
        gp = gp_ref[...]
        err = (x_ref[...] + oh * gp) - tg_ref[...]
        loss_ref[...] += 0.5 * jnp.sum(jnp.mean(err * err, axis=-1, keepdims=True), axis=0, keepdims=True)
        dz = err * (1.0 / D_MODEL)
        dz_ref[...] = dz.astype(BF16)
        dgp_ref[...] += jnp.sum(dz * oh, axis=0, keepdims=True)
        dn = dz * gp
        dout = (r2 * (dn - oh * jnp.mean(dn * oh, axis=-1, keepdims=True))).astype(BF16)
        dw_ref[...] += _dot_tn(yb, dout)
        dy = _dot_nt(dout, w)
        dyc_ref[...] = (dy * silu).astype(BF16)
        dgate_ref[...] = (dy * ycat * (sg * (1.0 + gate * (1.0 - sg)))).astype(BF16)

    row = lambda i: (i, 0)
    const = lambda i: (0, 0)
    full = jax.ShapeDtypeStruct((t, D_MODEL), BF16)
    return pl.pallas_call(
        body, grid=(t // tm,),
        in_specs=[pl.BlockSpec((tm, POOL_W), row),
                  pl.BlockSpec((tm, SSM_W), row),
                  pl.BlockSpec((tm, ATT_W), row),
                  pl.BlockSpec((tm, D_MODEL), lambda i: (i, 1)),
                  pl.BlockSpec((tm, D_MODEL), row),
                  pl.BlockSpec((tm, D_MODEL), row),
                  pl.BlockSpec((D_MODEL, D_MODEL), const),
                  pl.BlockSpec((1, D_MODEL), const)],
        out_specs=[pl.BlockSpec((tm, D_MODEL), row),
                   pl.BlockSpec((tm, D_MODEL), row),
                   pl.BlockSpec((tm, D_MODEL), row),
                   pl.BlockSpec((D_MODEL, D_MODEL), const),
                   pl.BlockSpec((1, D_MODEL), const),
                   pl.BlockSpec((1, LANES), const)],
        out_shape=[full, full, full,
                   jax.ShapeDtypeStruct((D_MODEL, D_MODEL), F32),
                   jax.ShapeDtypeStruct((1, D_MODEL), F32),
                   jax.ShapeDtypeStruct((1, LANES), F32)],
        compiler_params=_params("arbitrary"), name="tail")(y_pool, y_ssm, y_att, proj, x2, target, w_out, g_post)


def in_proj_bwd(du_pool, du_ssm, dq, d_gate, x2, dz, g_pre, w_in):
    t = x2.shape[0]
    tm = min(ROW_TILE, t)
    shard = W_IN_SHARD

    def body(dup_ref, dus_ref, dq_ref, dg_ref, x_ref, dz_ref, g_ref, w_ref, gx_ref, dw_ref, dgp_ref):
        @pl.when(pl.program_id(0) == 0)
        def _():
            dw_ref[...] = jnp.zeros_like(dw_ref)
            dgp_ref[...] = jnp.zeros_like(dgp_ref)

        dproj = jnp.concatenate([dup_ref[...], dus_ref[...], dq_ref[...], dg_ref[...]], axis=1)
        x = x_ref[...]
        r1 = _rms_scale(x)
        xh = x * r1
        g = g_ref[...]
        hb = (xh * g).astype(BF16)
        dh = jnp.zeros((tm, D_MODEL), F32)
        for k in range(N_CHIPS):
            dp_k = dproj[:, k * shard:(k + 1) * shard]
            dw_ref[k] += _dot_tn(hb, dp_k)
            dh = dh + _dot_nt(dp_k, w_ref[k])
        dgp_ref[...] += jnp.sum(dh * xh, axis=0, keepdims=True)
        dn = dh * g
        gx_ref[...] = dz_ref[...].astype(F32) + r1 * (dn - xh * jnp.mean(dn * xh, axis=-1, keepdims=True))

    row = lambda i: (i, 0)
    const = lambda i: (0, 0)
    return pl.pallas_call(
        body, grid=(t // tm,),
        in_specs=[pl.BlockSpec((tm, POOL_W), row),
                  pl.BlockSpec((tm, SSM_W), row),
                  pl.BlockSpec((tm, ATT_W), row),
                  pl.BlockSpec((tm, D_MODEL), row),
                  pl.BlockSpec((tm, D_MODEL), row),
                  pl.BlockSpec((tm, D_MODEL), row),
                  pl.BlockSpec((1, D_MODEL), const),
                  pl.BlockSpec((N_CHIPS, D_MODEL, shard), lambda i: (0, 0, 0))],
        out_specs=[pl.BlockSpec((tm, D_MODEL), row),
                   pl.BlockSpec((N_CHIPS, D_MODEL, shard), lambda i: (0, 0, 0)),
                   pl.BlockSpec((1, D_MODEL), const)],
        out_shape=[jax.ShapeDtypeStruct((t, D_MODEL), F32),
                   jax.ShapeDtypeStruct((N_CHIPS, D_MODEL, shard), F32),
                   jax.ShapeDtypeStruct((1, D_MODEL), F32)],
        compiler_params=_params("arbitrary"), name="in_proj_bwd")(du_pool, du_ssm, dq, d_gate, x2, dz, g_pre, w_in)


def _block_diag(blocks):
    g, r, c = blocks.shape
    eye = jnp.eye(g, dtype=blocks.dtype)
    return jnp.einsum("grc,gh->grhc", blocks, eye).reshape(g * r, g * c)


def _block_diag_extract(mat, g, r, c):
    eye = jnp.eye(g, dtype=mat.dtype)
    return jnp.einsum("grhc,gh->grc", mat.reshape(g, r, g, c), eye)


def local_step(x, mem, target, g_pre, w_in_shard, w_pool, pool_scale, a_re, a_im, log_dt, b_re, b_im, c_re, c_im,
               d_skip, later_shards, g_mem, g_post):
    nb, l, _ = x.shape
    x2 = x.reshape(nb * l, D_MODEL)
    tg2 = target.reshape(nb * l, D_MODEL)

    rowv = lambda a: a.reshape(1, NST)
    lanes_last = lambda b: b.transpose(2, 0, 1).reshape(SSM_GC, NST)
    ldt_row = rowv(jnp.broadcast_to(log_dt.reshape(SSM_NG, 1), (SSM_NG, SSM_N)))
    b_re_t = lanes_last(b_re)
    b_im_t = lanes_last(b_im)
    cst_f, cst_b = s5_scan_consts(rowv(a_re), rowv(a_im), ldt_row)
    bbr, bbi = s5_bbar(rowv(a_re), rowv(a_im), ldt_row, b_re_t, b_im_t)
    eye = jnp.eye(BLK_GROUPS, dtype=F32)
    blk_in = lambda bb: jnp.einsum("cmgn,gh->mhcgn", bb.reshape(SSM_GC, N_BLK, BLK_GROUPS, SSM_N), eye).reshape(
        N_BLK, BLK_CH, BLK_ST)
    blk_out = lambda cc: jnp.einsum("mgcn,gh->mgnhc", cc.reshape(N_BLK, BLK_GROUPS, SSM_GC, SSM_N), eye).reshape(
        N_BLK, BLK_ST, BLK_CH)
    b_m = jnp.concatenate([blk_in(bbr), blk_in(bbi)], axis=2).astype(BF16)
    c_m = jnp.concatenate([blk_out(c_re), -blk_out(c_im)], axis=1).astype(BF16)
    w_pool_blk = _block_diag(w_pool.reshape(4, POOL_GW, POOL_GW)).astype(BF16)

    proj, w_in, (g_glu, g_kv, g_out) = in_proj(x2, g_pre, w_in_shard, later_shards)
    w_glu = g_glu.transpose(2, 0, 1).reshape(SSM_W, 2 * SSM_W)
    w_kv = g_kv.reshape(D_MODEL, 2 * ATT_W)
    w_out = g_out.reshape(D_MODEL, D_MODEL)
    y_pool = pool_fwd(proj, w_pool_blk, pool_scale, nb, l)
    y_ssm, h, ypre, z = s5_fwd(proj, b_m, c_m, cst_f, d_skip, w_glu, nb, l)
    y_att = attn_fwd(proj, mem, g_mem, w_kv, nb, l)
    dz, d_ycat, d_gate, d_w_out, d_g_post, loss = tail(y_pool, y_ssm, y_att, proj, x2, tg2, w_out, g_post)
    du_pool, d_wp_blk, d_pool_scale = pool_bwd(proj, d_ycat, w_pool_blk, pool_scale, nb, l)
    du_ssm, d_w_glu, d_d_skip, d_cc, d_bb, d_lam = s5_bwd(
        proj, d_ycat, h, ypre, z, b_m, c_m, cst_b, d_skip, w_glu, nb, l)
    dq, d_w_kv, d_g_mem = attn_bwd(proj, d_ycat, mem, g_mem, w_kv, nb, l)
    grad_x, d_w_in, d_g_pre = in_proj_bwd(du_pool, du_ssm, dq, d_gate, x2, dz, g_pre, w_in)

    d_c = d_cc[0].reshape(SSM_NG, SSM_N, SSM_GC)
    d_ci = d_cc[1].reshape(SSM_NG, SSM_N, SSM_GC)
    d_lam_row = d_lam[0]
    d_ar, d_ai, d_ld, d_br, d_bi = s5_param_bwd(
        rowv(a_re), rowv(a_im), ldt_row, b_re_t, b_im_t,
        d_lam_row[:NST].reshape(1, NST), d_lam_row[NST:].reshape(1, NST), d_bb[0].T, d_bb[1].T)
    from_lanes_last = lambda b: b.reshape(SSM_GC, SSM_NG, SSM_N).transpose(1, 0, 2).reshape(1, SSM_NG, SSM_GC, SSM_N)

    grads = {
        "g_pre": d_g_pre,
        "w_in": d_w_in,
        "w_pool": _block_diag_extract(d_wp_blk, 4, POOL_GW, POOL_GW).reshape(1, 4, POOL_GW, POOL_GW),
        "pool_scale": d_pool_scale,
        "a_re": d_ar.reshape(1, SSM_NG, SSM_N),
        "a_im": d_ai.reshape(1, SSM_NG, SSM_N),
        "log_dt": d_ld[:, :SSM_NG],
        "b_re": from_lanes_last(d_br),
        "b_im": from_lanes_last(d_bi),
        "c_re": d_c.transpose(0, 2, 1).reshape(1, SSM_NG, SSM_GC, SSM_N),
        "c_im": (-d_ci).transpose(0, 2, 1).reshape(1, SSM_NG, SSM_GC, SSM_N),
        "d_skip": d_d_skip,
        "w_glu": d_w_glu,
        "g_mem": d_g_mem,
        "w_kv": d_w_kv,
        "w_out": d_w_out,
        "g_post": d_g_post,
    }
    return loss, grad_x.reshape(nb, l, D_MODEL), grads


VMEM_SPEC = pl.BlockSpec(memory_space=pltpu.VMEM)
ANY_SPEC = pl.BlockSpec(memory_space=pl.ANY)


def _place():
    return lax.axis_index("x"), lax.axis_index("y"), lax.axis_index("c")


def _other_chips(x, y):
    return [(1 - x, y), (x, 1 - y), (1 - x, 1 - y)]


SEMS_PER_ITEM = 5


def reduce_all(g4s, packed):
    n = len(g4s)
    dims = [g.shape[1:] for g in g4s]
    pr = packed.shape[0]

    def body(*refs):
        g_refs, p_ref = refs[:n], refs[n]
        outs, op = refs[n + 1:2 * n + 1], refs[2 * n + 1]
        scr = refs[2 * n + 2:]
        mine, theirs, sendb, recvb = scr[0:n], scr[n:2 * n], scr[2 * n:3 * n], scr[3 * n:4 * n]
        p_theirs, gat, lsems, send_sems, recv_sems = scr[4 * n:]
        x, y, c = _place()
        k = 2 * x + y
        chips = _other_chips(x, y)
        sib = (x, y, 1 - c)

        def halves(rows):
            hr = rows // 2
            return (pl.ds(pl.multiple_of(c * hr, SUBLANES), hr), pl.ds(pl.multiple_of((1 - c) * hr, SUBLANES), hr))

        def remote(src, dst, sem, to):
            return pltpu.make_async_remote_copy(
                src_ref=src, dst_ref=dst, send_sem=send_sems.at[sem], recv_sem=recv_sems.at[sem],
                device_id=to, device_id_type=MESH)

        loads, started = [], []
        for i in range(n):
            my_rows, sib_rows = halves(dims[i][0])
            ld = pltpu.make_async_copy(g_refs[i].at[:, my_rows, :], mine[i], lsems.at[i])
            ld.start()
            loads.append(ld)
            sw = remote(g_refs[i].at[:, sib_rows, :], theirs[i], SEMS_PER_ITEM * i, sib)
            sw.start()
            started.append(sw)
        p_my, p_sib = halves(pr)
        p_swap = remote(p_ref.at[p_sib, :], p_theirs, SEMS_PER_ITEM * n, sib)
        p_swap.start()
        started.append(p_swap)

        p_swap.wait_recv()
        gat[k] = p_ref[p_my, :] + p_theirs[...]
        for j, (px, py) in enumerate(chips):
            cp = remote(gat.at[k], gat.at[k], SEMS_PER_ITEM * n + 1 + j, (px, py, c))
            cp.start()
            started.append(cp)
        ici = []
        for i in range(n):
            loads[i].wait()
            started[i].wait_recv()
            for j, (px, py) in enumerate(chips):
                kp = 2 * px + py
                sendb[i][j] = (mine[i][kp] + theirs[i][kp]).astype(BF16)
                cp = remote(sendb[i].at[j], recvb[i].at[j], SEMS_PER_ITEM * i + 1 + j, (px, py, c))
                cp.start()
                ici.append(cp)

        for j, (px, py) in enumerate(chips):
            kp = 2 * px + py
            remote(gat.at[kp], gat.at[kp], SEMS_PER_ITEM * n + 1 + j, (px, py, c)).wait_recv()
        op[p_my, :] = ((gat[0] + gat[1]) + gat[2]) + gat[3]
        last = remote(op.at[p_my, :], op.at[p_my, :], SEMS_PER_ITEM * n + 4, sib)
        last.start()
        started.append(last)
        for i in range(n):
            my_rows, _ = halves(dims[i][0])
            half = mine[i][k] + theirs[i][k]
            for j in range(3):
                ici[3 * i + j].wait_recv()
                half = half + recvb[i][j].astype(F32)
            outs[i][my_rows, :] = half
            last = remote(outs[i].at[my_rows, :], outs[i].at[my_rows, :], SEMS_PER_ITEM * i + 4, sib)
            last.start()
            started.append(last)
        remote(op.at[p_sib, :], op.at[p_sib, :], SEMS_PER_ITEM * n + 4, sib).wait_recv()
        for i in range(n):
            _, sib_rows = halves(dims[i][0])
            remote(outs[i].at[sib_rows, :], outs[i].at[sib_rows, :], SEMS_PER_ITEM * i + 4, sib).wait_recv()
        for cp in started + ici:
            cp.wait_send()

    n_sems = SEMS_PER_ITEM * (n + 1)
    scratch = ([pltpu.VMEM((N_CHIPS, r // 2, cd), F32) for r, cd in dims] * 2
               + [pltpu.VMEM((3, r // 2, cd), BF16) for r, cd in dims] * 2
               + [pltpu.VMEM((pr // 2, LANES), F32), pltpu.VMEM((N_CHIPS, pr // 2, LANES), F32),
                  pltpu.SemaphoreType.DMA((n,)), pltpu.SemaphoreType.DMA((n_sems,)), pltpu.SemaphoreType.DMA((n_sems,))])
    res = pl.pallas_call(
        body,
        out_shape=[jax.ShapeDtypeStruct(d, F32) for d in dims] + [jax.ShapeDtypeStruct(packed.shape, F32)],
        in_specs=[ANY_SPEC] * n + [VMEM_SPEC], out_specs=[VMEM_SPEC] * (n + 1),
        scratch_shapes=scratch,
        compiler_params=pltpu.CompilerParams(vmem_limit_bytes=VMEM_LIMIT),
        name="reduce_all")(*g4s, packed)
    return res[:n], res[n]


def adamw_all(ws, gs, ms, vs):
    n = len(ws)

    def body(*refs):
        w, g, m, v = refs[:n], refs[n:2 * n], refs[2 * n:3 * n], refs[3 * n:4 * n]
        od, om, ov = refs[4 * n:5 * n], refs[5 * n:6 * n], refs[6 * n:]
        for i in range(n):
            od[i][...], om[i][...], ov[i][...] = _adamw(w[i][...], g[i][...], m[i][...], v[i][...])

    shapes = [jax.ShapeDtypeStruct(a.shape, F32) for a in ws]
    res = pl.pallas_call(
        body, out_shape=shapes * 3, in_specs=[VMEM_SPEC] * (4 * n), out_specs=[VMEM_SPEC] * (3 * n),
        compiler_params=pltpu.CompilerParams(vmem_limit_bytes=VMEM_LIMIT),
        name="adamw_all")(*ws, *gs, *ms, *vs)
    return res[:n], res[n:2 * n], res[2 * n:]


WEIGHTS = ("g_pre", "w_in", "w_pool", "pool_scale", "a_re", "a_im", "log_dt", "b_re", "b_im", "c_re", "c_im", "d_skip",
           "w_glu", "g_mem", "w_kv", "w_out", "g_post")
SHARDED = ("w_in", "w_glu", "w_kv", "w_out")
SWAPPED = ("b_re", "b_im", "w_glu")
REPLICATED = tuple(n for n in WEIGHTS if n not in SHARDED)
PACK_TILE = SUBLANES * LANES


def _pack(arrays):
    parts = []
    for a in arrays:
        flat = a.reshape(-1)
        parts.append(jnp.pad(flat, (0, -flat.shape[0] % PACK_TILE)).reshape(-1, LANES))
    rows = sum(p.shape[0] for p in parts)
    if rows % (2 * SUBLANES):
        parts.append(jnp.zeros((SUBLANES, LANES), F32))
    return jnp.concatenate(parts, axis=0)


def _unpack(packed, shapes):
    out, row = [], 0
    for shp in shapes:
        size = math.prod(shp)
        rows = -(-size // PACK_TILE) * SUBLANES
        out.append(packed[row:row + rows].reshape(-1)[:size].reshape(shp))
        row += rows
    return out


def kernel(x, mem, g_pre, w_in, w_pool, pool_scale, a_re, a_im, log_dt, b_re, b_im, c_re, c_im, d_skip, w_glu, g_mem, w_kv, w_out, g_post, loss_target, m_g_pre, m_w_in, m_w_pool, m_pool_scale, m_a_re, m_a_im, m_log_dt, m_b_re, m_b_im, m_c_re, m_c_im, m_d_skip, m_w_glu, m_g_mem, m_w_kv, m_w_out, m_g_post, v_g_pre, v_w_in, v_w_pool, v_pool_scale, v_a_re, v_a_im, v_log_dt, v_b_re, v_b_im, v_c_re, v_c_im, v_d_skip, v_w_glu, v_g_mem, v_w_kv, v_w_out, v_g_post):
    given = dict(locals())
    wts = {n: given[n] for n in WEIGHTS}
    mom = {n: given["m_" + n] for n in WEIGHTS}
    var = {n: given["v_" + n] for n in WEIGHTS}

    swap = lambda a: jnp.swapaxes(a, -1, -2)
    view = lambda n, a: swap(a) if n in SWAPPED else a

    loss_part, grad_x, grads = local_step(
        x, mem, loss_target, g_pre, w_in[0], w_pool[0], pool_scale, a_re[0], a_im[0], log_dt[0], b_re[0], b_im[0],
        c_re[0], c_im[0], d_skip, [swap(w_glu)[0], w_kv[0], w_out[0]], g_mem, g_post)

    glu_cols = 2 * SSM_W // N_CHIPS
    partial4 = [
        grads["w_in"],
        grads["w_glu"].reshape(SSM_W, N_CHIPS, glu_cols).transpose(1, 2, 0),
        grads["w_kv"].reshape(N_CHIPS, D_MODEL // N_CHIPS, 2 * ATT_W),
        grads["w_out"].reshape(N_CHIPS, D_MODEL // N_CHIPS, D_MODEL),
    ]
    loss_tile = jnp.broadcast_to(loss_part, (SUBLANES, LANES))
    sharded_g, packed_g = reduce_all(partial4, _pack([grads[n] for n in REPLICATED] + [loss_tile]))
    small_g = _unpack(packed_g, [view(n, wts[n]).shape for n in REPLICATED] + [(SUBLANES, LANES)])
    loss = small_g[-1][0, 0]
    grad = dict(zip(SHARDED, [g[None] for g in sharded_g]))
    grad.update(zip(REPLICATED, small_g[:-1]))

    deltas, new_ms, new_vs = adamw_all([view(n, wts[n]) for n in WEIGHTS], [grad[n] for n in WEIGHTS],
                                       [view(n, mom[n]) for n in WEIGHTS], [view(n, var[n]) for n in WEIGHTS])
    back = lambda arrs: [view(n, a) for n, a in zip(WEIGHTS, arrs)]
    return (loss, grad_x, *back([grad[n] for n in WEIGHTS]), *back(deltas), *back(new_ms), *back(new_vs))
```

```python
import functools
import math

import jax
import jax.numpy as jnp
from jax import lax
from jax.experimental import pallas as pl
from jax.experimental.pallas import tpu as pltpu

F32 = jnp.float32
BF16 = jnp.bfloat16

D_MODEL = 1024
POOL_W = 384
SSM_W = 384
ATT_W = 256
POOL_GW = 96
POOL_WINDOWS = (2, 4, 8, 16)
POOL_PAD = 16
SSM_NG = 24
SSM_N = 64
SSM_GC = 16
N_CHIPS = 4
N_DEV = 8
W_IN_SHARD = 2 * D_MODEL // N_CHIPS
NST = SSM_NG * SSM_N
BLK_CH = 128
BLK_GROUPS = BLK_CH // SSM_GC
BLK_ST = BLK_GROUPS * SSM_N
N_BLK = SSM_W // BLK_CH
N_MEM = 256
MEM_HEADS = 4
MEM_HD = 64
EPS = 1e-6

ADAM_LR = 0.001
ADAM_B1 = 0.9
ADAM_B2 = 0.999
ADAM_EPS = 1e-08
ADAM_WD = 0.01
ADAM_STEP = 10

SUBLANES = 8
LANES = 128
V7X_VMEM_BYTES = 64 * 2**20
VMEM_LIMIT = V7X_VMEM_BYTES - 8 * 2**20
SCAN_COLS = 512
ROW_TILE = 256
IN_PROJ_ROWS = 2048
ATTN_ROWS = 1024
MESH = pl.DeviceIdType.MESH

NT = (((1,), (1,)), ((), ()))
TN = (((0,), (0,)), ((), ()))


def _params(*sem):
    return pltpu.CompilerParams(dimension_semantics=sem, vmem_limit_bytes=VMEM_LIMIT)


def _dot(a, b):
    return jnp.dot(a, b, preferred_element_type=F32)


def _dot_nt(a, b):
    return lax.dot_general(a, b, NT, preferred_element_type=F32)


def _dot_tn(a, b):
    return lax.dot_general(a, b, TN, preferred_element_type=F32)


def _rms_scale(v):
    return lax.rsqrt(jnp.mean(v * v, axis=-1, keepdims=True) + EPS)


def _adamw(w, g, m, v):
    m = ADAM_B1 * m + (1.0 - ADAM_B1) * g
    v = ADAM_B2 * v + (1.0 - ADAM_B2) * (g * g)
    m_hat = m / (1.0 - ADAM_B1 ** ADAM_STEP)
    v_hat = v / (1.0 - ADAM_B2 ** ADAM_STEP)
    delta = -ADAM_LR * (m_hat / (jnp.sqrt(v_hat) + ADAM_EPS) + ADAM_WD * w)
    return delta, m, v


def _phase_shard(j):
    cx, cy = lax.axis_index("x"), lax.axis_index("y")
    others = [2 * px + py for px, py in _other_chips(cx, cy)]
    own = 2 * cx + cy
    return jnp.where(j == 0, own, jnp.where(j == 1, others[0], jnp.where(j == 2, others[1], others[2])))


def in_proj(x2, g_pre, w_in_shard, later_shards):
    t = x2.shape[0]
    tm = min(IN_PROJ_ROWS, t)
    n_tiles = t // tm
    n = len(later_shards)

    def body(x_ref, g_ref, ws_ref, *rest):
        shard_refs = rest[:n]
        o_ref, w_out_ref, gathered = rest[n], rest[n + 1], rest[n + 2:2 * n + 2]
        w_buf, h_buf = rest[2 * n + 2], rest[2 * n + 3]
        cast_bufs = rest[2 * n + 4:3 * n + 4]
        local_sems, send_sems, recv_sems = rest[3 * n + 4:]
        j = pl.program_id(0)
        i = pl.program_id(1)
        cx, cy, cc = _place()
        k = 2 * cx + cy
        chips = _other_chips(cx, cy)

        half = w_in_shard.shape[0] // 2

        def w_copy(p, block, core_half, forward):
            px, py = chips[p]
            rows = w_buf.at[block, pl.ds(pl.multiple_of(core_half * half, 2 * SUBLANES), half)]
            sem = 3 + p if forward else p
            return pltpu.make_async_remote_copy(
                src_ref=rows, dst_ref=rows, send_sem=send_sems.at[sem], recv_sem=recv_sems.at[sem],
                device_id=(cx, cy, 1 - cc) if forward else (px, py, cc), device_id_type=MESH)

        def own(m):
            return pltpu.make_async_copy(cast_bufs[m], gathered[m].at[k], local_sems.at[m])

        def later_copy(m, p, block, core_half, forward):
            px, py = chips[p]
            hr = later_shards[m].shape[0] // 2
            rows = pl.ds(pl.multiple_of(core_half * hr, 2 * SUBLANES), hr)
            dst = gathered[m].at[block, rows]
            sem = 6 + 6 * m + (3 + p if forward else p)
            return pltpu.make_async_remote_copy(
                src_ref=dst if forward else cast_bufs[m].at[rows], dst_ref=dst, send_sem=send_sems.at[sem],
                recv_sem=recv_sems.at[sem], device_id=(cx, cy, 1 - cc) if forward else (px, py, cc),
                device_id_type=MESH)

        @pl.when((j == 0) & (i == 0))
        def _():
            w_buf[k] = ws_ref[...].astype(BF16)
            for p in range(3):
                w_copy(p, k, cc, False).start()
            for m in range(n):
                cast_bufs[m][...] = shard_refs[m][...].astype(BF16)
                own(m).start()
                for p in range(3):
                    later_copy(m, p, k, cc, False).start()

        rows = pl.ds(pl.multiple_of(i * tm, tm), tm)

        @pl.when(j == 0)
        def _():
            x = x_ref[...]
            h_buf[rows, :] = (x * _rms_scale(x) * g_ref[...]).astype(BF16)

        for p in range(3):
            @pl.when((j == p + 1) & (i == 0))
            def _(p=p):
                px, py = chips[p]
                w_copy(p, 2 * px + py, cc, False).wait_recv()
                w_copy(p, 2 * px + py, cc, True).start()
                w_copy(p, 2 * px + py, 1 - cc, True).wait_recv()

        o_ref[...] = _dot(h_buf[rows, :], w_buf[_phase_shard(j)]).astype(BF16)

        @pl.when((j == N_CHIPS - 1) & (i == n_tiles - 1))
        def _():
            out_cp = pltpu.make_async_copy(w_buf, w_out_ref, local_sems.at[n])
            out_cp.start()
            for m in range(n):
                own(m).wait()
                for p in range(3):
                    px, py = chips[p]
                    later_copy(m, p, 2 * px + py, cc, False).wait_recv()
                    later_copy(m, p, 2 * px + py, cc, True).start()
            for m in range(n):
                for p in range(3):
                    px, py = chips[p]
                    later_copy(m, p, 2 * px + py, 1 - cc, True).wait_recv()
            for p in range(3):
                px, py = chips[p]
                w_copy(p, k, cc, False).wait_send()
                w_copy(p, 2 * px + py, cc, True).wait_send()
            for m in range(n):
                for p in range(3):
                    px, py = chips[p]
                    later_copy(m, p, k, cc, False).wait_send()
                    later_copy(m, p, 2 * px + py, cc, True).wait_send()
            out_cp.wait()

    whole = lambda a: pl.BlockSpec(a.shape, lambda j, i: (0,) * a.ndim)
    res = pl.pallas_call(
        body, grid=(N_CHIPS, n_tiles),
        in_specs=[pl.BlockSpec((tm, D_MODEL), lambda j, i: (jnp.where(j == 0, i, n_tiles - 1), 0)),
                  pl.BlockSpec((1, D_MODEL), lambda j, i: (0, 0)),
                  whole(w_in_shard)] + [whole(s) for s in later_shards],
        out_specs=[pl.BlockSpec((tm, W_IN_SHARD), lambda j, i: (i, _phase_shard(j)))] + [ANY_SPEC] * (n + 1),
        out_shape=([jax.ShapeDtypeStruct((t, 2 * D_MODEL), BF16),
                    jax.ShapeDtypeStruct((N_CHIPS,) + w_in_shard.shape, BF16)]
                   + [jax.ShapeDtypeStruct((N_CHIPS,) + s.shape, BF16) for s in later_shards]),
        scratch_shapes=([pltpu.VMEM((N_CHIPS,) + w_in_shard.shape, BF16), pltpu.VMEM((t, D_MODEL), BF16)]
                        + [pltpu.VMEM(s.shape, BF16) for s in later_shards]
                        + [pltpu.SemaphoreType.DMA((n + 1,)), pltpu.SemaphoreType.DMA((6 * n + 6,)),
                           pltpu.SemaphoreType.DMA((6 * n + 6,))]),
        compiler_params=_params("arbitrary", "arbitrary"), name="in_proj")(x2, g_pre, w_in_shard, *later_shards)
    return res[0], res[1], res[2:]


def _pool_lane_window(shape):
    ch = lax.broadcasted_iota(jnp.int32, shape, 1)
    return jnp.where(ch < POOL_GW, 2.0, jnp.where(ch < 2 * POOL_GW, 4.0, jnp.where(ch < 3 * POOL_GW, 8.0, 16.0)))


def _pool_select(win, s2, s4, s8, s16):
    return jnp.where(win == 2.0, s2, jnp.where(win == 4.0, s4, jnp.where(win == 8.0, s8, s16)))


def _pool_div_count(v, win):
    pos = (lax.broadcasted_iota(jnp.int32, (POOL_PAD, POOL_W), 0) + 1).astype(F32)
    head = v[:POOL_PAD] / jnp.minimum(pos, win)
    return jnp.concatenate([head, v[POOL_PAD:] * (1.0 / win)], axis=0)


def _pool_diff(u, pad_ref, l):
    lo = POOL_PAD
    pad_ref[pl.ds(lo, l), :] = u
    s2 = u + pad_ref[pl.ds(lo - 1, l), :]
    pad_ref[pl.ds(lo, l), :] = s2
    s4 = s2 + pad_ref[pl.ds(lo - 2, l), :]
    pad_ref[pl.ds(lo, l), :] = s4
    s8 = s4 + pad_ref[pl.ds(lo - 4, l), :]
    pad_ref[pl.ds(lo, l), :] = s8
    s16 = s8 + pad_ref[pl.ds(lo - 8, l), :]
    win = _pool_lane_window((1, POOL_W))
    return _pool_div_count(_pool_select(win, s2, s4, s8, s16), win) - u, win


def pool_fwd(proj, w_blk, pool_scale, nb, l):
    def body(u_ref, w_ref, ps_ref, y_ref, pad_ref):
        pad_ref[pl.ds(0, POOL_PAD), :] = jnp.zeros((POOL_PAD, POOL_W), F32)
        d, _ = _pool_diff(u_ref[...].astype(F32), pad_ref, l)
        y_ref[...] = (_dot(d.astype(BF16), w_ref[...]) * ps_ref[...]).astype(BF16)

    return pl.pallas_call(
        body, grid=(nb,),
        in_specs=[pl.BlockSpec((l, POOL_W), lambda b: (b, 0)),
                  pl.BlockSpec((POOL_W, POOL_W), lambda b: (0, 0)),
                  pl.BlockSpec((1, POOL_W), lambda b: (0, 0))],
        out_specs=pl.BlockSpec((l, POOL_W), lambda b: (b, 0)),
        out_shape=jax.ShapeDtypeStruct((nb * l, POOL_W), BF16),
        scratch_shapes=[pltpu.VMEM((POOL_PAD + l, POOL_W), F32)],
        compiler_params=_params("arbitrary"), name="pool_fwd")(proj, w_blk, pool_scale)


def pool_bwd(proj, d_ycat, w_blk, pool_scale, nb, l):
    def body(u_ref, dy_ref, w_ref, ps_ref, du_ref, dw_ref, dps_ref, pad_ref, padb_ref):
        b = pl.program_id(0)

        @pl.when(b == 0)
        def _():
            dw_ref[...] = jnp.zeros_like(dw_ref)
            dps_ref[...] = jnp.zeros_like(dps_ref)

        pad_ref[pl.ds(0, POOL_PAD), :] = jnp.zeros((POOL_PAD, POOL_W), F32)
        padb_ref[pl.ds(l, POOL_PAD), :] = jnp.zeros((POOL_PAD, POOL_W), F32)
        d, win = _pool_diff(u_ref[...].astype(F32), pad_ref, l)
        db = d.astype(BF16)
        w = w_ref[...]
        dy = dy_ref[...].astype(F32)
        dps_ref[...] += jnp.sum(dy * _dot(db, w), axis=0, keepdims=True)
        dyo = (dy * ps_ref[...]).astype(BF16)
        dw_ref[...] += _dot_tn(db, dyo)
        dd = _dot_nt(dyo, w)
        e = _pool_div_count(dd, win)
        padb_ref[pl.ds(0, l), :] = e
        f2 = e + padb_ref[pl.ds(1, l), :]
        padb_ref[pl.ds(0, l), :] = f2
        f4 = f2 + padb_ref[pl.ds(2, l), :]
        padb_ref[pl.ds(0, l), :] = f4
        f8 = f4 + padb_ref[pl.ds(4, l), :]
        padb_ref[pl.ds(0, l), :] = f8
        f16 = f8 + padb_ref[pl.ds(8, l), :]
        du_ref[...] = (_pool_select(win, f2, f4, f8, f16) - dd).astype(BF16)

    return pl.pallas_call(
        body, grid=(nb,),
        in_specs=[pl.BlockSpec((l, POOL_W), lambda b: (b, 0)),
                  pl.BlockSpec((l, POOL_W), lambda b: (b, 0)),
                  pl.BlockSpec((POOL_W, POOL_W), lambda b: (0, 0)),
                  pl.BlockSpec((1, POOL_W), lambda b: (0, 0))],
        out_specs=[pl.BlockSpec((l, POOL_W), lambda b: (b, 0)),
                   pl.BlockSpec((POOL_W, POOL_W), lambda b: (0, 0)),
                   pl.BlockSpec((1, POOL_W), lambda b: (0, 0))],
        out_shape=[jax.ShapeDtypeStruct((nb * l, POOL_W), BF16),
                   jax.ShapeDtypeStruct((POOL_W, POOL_W), F32),
                   jax.ShapeDtypeStruct((1, POOL_W), F32)],
        scratch_shapes=[pltpu.VMEM((POOL_PAD + l, POOL_W), F32), pltpu.VMEM((POOL_PAD + l, POOL_W), F32)],
        compiler_params=_params("arbitrary"), name="pool_bwd")(proj, d_ycat, w_blk, pool_scale)


def _discretise(a_re, a_im, ldt, b_re, b_im):
    dt = jnp.exp(ldt)
    mag = jnp.exp(a_re * dt)
    th = a_im * dt
    lr = mag * jnp.cos(th)
    li = mag * jnp.sin(th)
    nr = lr - 1.0
    den = a_re * a_re + a_im * a_im
    fr = (nr * a_re + li * a_im) / den
    fi = (li * a_re - nr * a_im) / den
    return lr, li, fr * b_re - fi * b_im, fr * b_im + fi * b_re


def _cmul(ar, ai, br, bi):
    return ar * br - ai * bi, ar * bi + ai * br


def s5_scan_consts(a_re_row, a_im_row, ldt_row):
    def body(ar_ref, ai_ref, ld_ref, f_ref, b_ref):
        dt = jnp.exp(ld_ref[...])
        mag = jnp.exp(ar_ref[...] * dt)
        th = ai_ref[...] * dt
        lr = mag * jnp.cos(th)
        li = mag * jnp.sin(th)
        r = lax.broadcasted_iota(jnp.int32, (SUBLANES, NST), 0)
        for out_ref, sign, rev in ((f_ref, 1.0, False), (b_ref, -1.0, True)):
            p = [(lr, sign * li)]
            for _ in range(SUBLANES - 1):
                p.append(_cmul(p[-1][0], p[-1][1], lr, sign * li))
            for idx, s in enumerate((1, 2, 4)):
                inside = (r < SUBLANES - s) if rev else (r >= s)
                out_ref[pl.ds(2 * idx * SUBLANES, SUBLANES), :] = jnp.where(inside, p[s - 1][0], 0.0)
                out_ref[pl.ds((2 * idx + 1) * SUBLANES, SUBLANES), :] = jnp.where(inside, p[s - 1][1], 0.0)
            pr = jnp.zeros((SUBLANES, NST), F32)
            pi = jnp.zeros((SUBLANES, NST), F32)
            for row in range(SUBLANES):
                power = (SUBLANES - row) if rev else (row + 1)
                pr = jnp.where(r == row, p[power - 1][0], pr)
                pi = jnp.where(r == row, p[power - 1][1], pi)
            out_ref[pl.ds(6 * SUBLANES, SUBLANES), :] = pr
            out_ref[pl.ds(7 * SUBLANES, SUBLANES), :] = pi

    shape = jax.ShapeDtypeStruct((8 * SUBLANES, NST), F32)
    return pl.pallas_call(body, out_shape=[shape, shape], name="s5_scan_consts")(a_re_row, a_im_row, ldt_row)


def s5_bbar(a_re_row, a_im_row, ldt_row, b_re_t, b_im_t):
    def body(ar, ai, ld, br, bi, o_re, o_im):
        _, _, bbr, bbi = _discretise(ar[...], ai[...], ld[...], br[...], bi[...])
        o_re[...] = bbr
        o_im[...] = bbi

    shape = jax.ShapeDtypeStruct((SSM_GC, NST), F32)
    return pl.pallas_call(body, out_shape=[shape, shape], name="s5_bbar")(a_re_row, a_im_row, ldt_row, b_re_t, b_im_t)


def s5_param_bwd(a_re_row, a_im_row, ldt_row, b_re_t, b_im_t, d_lr, d_li, d_bbr, d_bbi):
    def body(ar, ai, ld, br, bi, g_lr, g_li, g_br, g_bi, o_ar, o_ai, o_ld, o_br, o_bi):
        _, vjp = jax.vjp(_discretise, ar[...], ai[...], ld[...], br[...], bi[...])
        d_ar, d_ai, d_ld, d_br, d_bi = vjp((g_lr[...], g_li[...], g_br[...], g_bi[...]))
        o_ar[...] = d_ar
        o_ai[...] = d_ai
        state = lax.broadcasted_iota(jnp.int32, (NST, LANES), 0)
        lane = lax.broadcasted_iota(jnp.int32, (NST, LANES), 1)
        in_group = jnp.where((state >= lane * SSM_N) & (state < (lane + 1) * SSM_N), 1.0, 0.0)
        o_ld[...] = jnp.dot(d_ld, in_group, precision=lax.Precision.HIGHEST, preferred_element_type=F32)
        o_br[...] = d_br
        o_bi[...] = d_bi

    row = jax.ShapeDtypeStruct((1, NST), F32)
    mat = jax.ShapeDtypeStruct((SSM_GC, NST), F32)
    grp = jax.ShapeDtypeStruct((1, LANES), F32)
    return pl.pallas_call(body, out_shape=[row, row, grp, mat, mat], name="s5_param_bwd")(
        a_re_row, a_im_row, ldt_row, b_re_t, b_im_t, d_lr, d_li, d_bbr, d_bbi)


def _scan_tiles(buf_ref, cst_ref, carry_ref, rows, reverse, reset, h_ref=None, acc_ref=None):
    n_tiles = rows // SUBLANES
    shifts = ((1, 0), (2, 2), (4, 4))
    row_id = lax.broadcasted_iota(jnp.int32, (SUBLANES, SCAN_COLS), 0)
    cols = [(j * SCAN_COLS, NST + j * SCAN_COLS) for j in range(NST // SCAN_COLS)]
    edge = 0 if reverse else SUBLANES - 1
    carry = jnp.where(reset, 0.0, carry_ref[...])
    accs = [(acc_ref[:, pl.ds(o_re, SCAN_COLS)], acc_ref[:, pl.ds(o_im, SCAN_COLS)]) for o_re, o_im in cols] if reverse else None
    for i in range(n_tiles):
        start = ((n_tiles - 1 - i) if reverse else i) * SUBLANES
        rws = pl.ds(start, SUBLANES)
        for j, (o_re, o_im) in enumerate(cols):
            c_re = pl.ds(o_re, SCAN_COLS)
            c_im = pl.ds(o_im, SCAN_COLS)
            xr = buf_ref[rws, c_re]
            xi = buf_ref[rws, c_im]
            for s, base in shifts:
                amount = (SUBLANES - s) if reverse else s
                sr = pltpu.roll(xr, amount, 0)
                si = pltpu.roll(xi, amount, 0)
                mr = cst_ref[pl.ds(base * SUBLANES, SUBLANES), c_re]
                mi = cst_ref[pl.ds((base + 1) * SUBLANES, SUBLANES), c_re]
                xr, xi = xr + (mr * sr - mi * si), xi + (mr * si + mi * sr)
            pr = cst_ref[pl.ds(6 * SUBLANES, SUBLANES), c_re]
            pi = cst_ref[pl.ds(7 * SUBLANES, SUBLANES), c_re]
            cr = carry[:, o_re:o_re + SCAN_COLS]
            ci = carry[:, o_im:o_im + SCAN_COLS]
            xr, xi = xr + (pr * cr - pi * ci), xi + (pr * ci + pi * cr)
            buf_ref[rws, c_re] = xr
            buf_ref[rws, c_im] = xi
            if reverse:
                gnr = jnp.where(row_id == SUBLANES - 1, cr, pltpu.roll(xr, SUBLANES - 1, 0))
                gni = jnp.where(row_id == SUBLANES - 1, ci, pltpu.roll(xi, SUBLANES - 1, 0))
                hr = h_ref[rws, c_re]
                hi = h_ref[rws, c_im]
                accs[j] = (accs[j][0] + (gnr * hr + gni * hi), accs[j][1] + (gni * hr - gnr * hi))
        carry = jnp.broadcast_to(buf_ref[pl.ds(start + edge, 1), :], (SUBLANES, 2 * NST))
    carry_ref[...] = carry
    if reverse:
        for (o_re, o_im), (a_re_, a_im_) in zip(cols, accs):
            acc_ref[:, pl.ds(o_re, SCAN_COLS)] = a_re_
            acc_ref[:, pl.ds(o_im, SCAN_COLS)] = a_im_


def _state_block(v, m):
    return jnp.concatenate([v[:, m * BLK_ST:(m + 1) * BLK_ST], v[:, NST + m * BLK_ST:NST + (m + 1) * BLK_ST]], axis=1)


def _put_state_block(buf_ref, m, val):
    buf_ref[:, pl.ds(m * BLK_ST, BLK_ST)] = val[:, :BLK_ST]
    buf_ref[:, pl.ds(NST + m * BLK_ST, BLK_ST)] = val[:, BLK_ST:]


def s5_fwd(proj, b_m, c_m, cst, d_skip, w_glu, nb, l):
    tt = min(256, l)
    nt = l // tt
    n_chunks = nb * nt

    def body(u_ref, b_ref, c_ref, cst_ref, ds_ref, wg_ref, y_ref, h_ref, yp_ref, z_ref,
             buf0, buf1, ub0, ub1, carry_ref):
        s = pl.program_id(0)

        @pl.when(s == 0)
        def _():
            for r in (buf0, buf1, ub0, ub1, carry_ref):
                r[...] = jnp.zeros_like(r)

        def step(p_buf, p_u, q_buf):
            h = p_buf[...].astype(BF16)
            h_ref[...] = h
            ypre = (jnp.concatenate([_dot(_state_block(h, m), c_ref[m]) for m in range(N_BLK)], axis=1)
                    + ds_ref[...] * p_u[...])
            yp_ref[...] = ypre
            z = _dot(jax.nn.gelu(ypre).astype(BF16), wg_ref[...])
            z_ref[...] = z
            y_ref[...] = (z[:, :SSM_W] * jax.nn.sigmoid(z[:, SSM_W:])).astype(BF16)
            ub = u_ref[...]
            p_u[...] = ub.astype(F32)
            for m in range(N_BLK):
                _put_state_block(p_buf, m, _dot(ub[:, m * BLK_CH:(m + 1) * BLK_CH], b_ref[m]))
            _scan_tiles(q_buf, cst_ref, carry_ref, tt, False, lax.rem(s + nt - 1, nt) == 0)

        @pl.when(lax.rem(s, 2) == 0)
        def _():
            step(buf0, ub0, buf1)

        @pl.when(lax.rem(s, 2) == 1)
        def _():
            step(buf1, ub1, buf0)

    row_in = lambda s: (jnp.minimum(s, n_chunks - 1), 1)
    row_out = lambda s: (jnp.maximum(s - 2, 0), 0)
    const = lambda s: (0, 0)
    const3 = lambda s: (0, 0, 0)
    return pl.pallas_call(
        body, grid=(n_chunks + 2,),
        in_specs=[pl.BlockSpec((tt, SSM_W), row_in),
                  pl.BlockSpec((N_BLK, BLK_CH, 2 * BLK_ST), const3),
                  pl.BlockSpec((N_BLK, 2 * BLK_ST, BLK_CH), const3),
                  pl.BlockSpec((8 * SUBLANES, NST), const),
                  pl.BlockSpec((1, SSM_W), const),
                  pl.BlockSpec((SSM_W, 2 * SSM_W), const)],
        out_specs=[pl.BlockSpec((tt, SSM_W), row_out),
                   pl.BlockSpec((tt, 2 * NST), row_out),
                   pl.BlockSpec((tt, SSM_W), row_out),
                   pl.BlockSpec((tt, 2 * SSM_W), row_out)],
        out_shape=[jax.ShapeDtypeStruct((nb * l, SSM_W), BF16),
                   jax.ShapeDtypeStruct((nb * l, 2 * NST), BF16),
                   jax.ShapeDtypeStruct((nb * l, SSM_W), F32),
                   jax.ShapeDtypeStruct((nb * l, 2 * SSM_W), F32)],
        scratch_shapes=[pltpu.VMEM((tt, 2 * NST), F32), pltpu.VMEM((tt, 2 * NST), F32),
                        pltpu.VMEM((tt, SSM_W), F32), pltpu.VMEM((tt, SSM_W), F32),
                        pltpu.VMEM((SUBLANES, 2 * NST), F32)],
        compiler_params=_params("arbitrary"), name="s5_fwd")(proj, b_m, c_m, cst, d_skip, w_glu)


def s5_bwd(proj, d_ycat, h, ypre, z, b_m, c_m, cst_rev, d_skip, w_glu, nb, l):
    tt = min(256, l)
    nt = l // tt
    n_chunks = nb * nt

    def body(u_ref, dy_ref, h_ref, yp_ref, z_ref, b_ref, c_ref, cst_ref, ds_ref, wg_ref,
             du_ref, dwg_ref, dds_ref, dcc_ref, dbb_ref, dlam_ref,
             buf0, buf1, hb0, hb1, ub0, ub1, dyp0, dyp1, carry_ref, dc_ref, db_ref):
        s = pl.program_id(0)

        @pl.when(s == 0)
        def _():
            for r in (buf0, buf1, hb0, hb1, ub0, ub1, dyp0, dyp1, carry_ref, dc_ref, db_ref, dwg_ref, dds_ref, dlam_ref):
                r[...] = jnp.zeros_like(r)

        def step(p_buf, p_h, p_u, p_dyp, q_buf, q_h):
            g = p_buf[...].astype(BF16)
            ub_done = p_u[...].astype(BF16)
            du = []
            for m in range(N_BLK):
                g_m = _state_block(g, m)
                db_ref[m] += _dot_tn(g_m, ub_done[:, m * BLK_CH:(m + 1) * BLK_CH])
                du.append(_dot_nt(g_m, b_ref[m]))
            du_ref[...] = (jnp.concatenate(du, axis=1) + ds_ref[...] * p_dyp[...]).astype(BF16)
            u = u_ref[...].astype(F32)
            ypre = yp_ref[...]
            z = z_ref[...]
            z1 = z[:, :SSM_W]
            sg = jax.nn.sigmoid(z[:, SSM_W:])
            dy = dy_ref[...].astype(F32) * jnp.where(s < n_chunks, 1.0, 0.0)
            dz = jnp.concatenate([dy * sg, dy * z1 * sg * (1.0 - sg)], axis=1).astype(BF16)
            yg, gelu_vjp = jax.vjp(jax.nn.gelu, ypre)
            dwg_ref[...] += _dot_tn(yg.astype(BF16), dz)
            dypre = gelu_vjp(_dot_nt(dz, wg_ref[...]))[0]
            dds_ref[...] += jnp.sum(dypre * u, axis=0, keepdims=True)
            dyb = dypre.astype(BF16)
            hb = h_ref[...]
            p_h[...] = hb.astype(F32)
            p_u[...] = u
            p_dyp[...] = dypre
            for m in range(N_BLK):
                dy_m = dyb[:, m * BLK_CH:(m + 1) * BLK_CH]
                dc_ref[m] += _dot_tn(_state_block(hb, m), dy_m)
                _put_state_block(p_buf, m, _dot_nt(dy_m, c_ref[m]))
            _scan_tiles(q_buf, cst_ref, carry_ref, tt, True, lax.rem(s + nt - 1, nt) == 0, h_ref=q_h, acc_ref=dlam_ref)

        @pl.when(lax.rem(s, 2) == 0)
        def _():
            step(buf0, hb0, ub0, dyp0, buf1, hb1)

        @pl.when(lax.rem(s, 2) == 1)
        def _():
            step(buf1, hb1, ub1, dyp1, buf0, hb0)

        @pl.when(s == n_chunks + 1)
        def _():
            dlam_ref[...] = jnp.broadcast_to(jnp.sum(dlam_ref[...], axis=0, keepdims=True), dlam_ref.shape)
            for acc_ref, out_ref in ((dc_ref, dcc_ref), (db_ref, dbb_ref)):
                for m in range(N_BLK):
                    for ri in range(2):
                        for gl in range(BLK_GROUPS):
                            out_ref[ri, pl.ds((m * BLK_GROUPS + gl) * SSM_N, SSM_N), :] = acc_ref[
                                m, pl.ds(ri * BLK_ST + gl * SSM_N, SSM_N), pl.ds(gl * SSM_GC, SSM_GC)]

    def chunk_rows(o):
        return lax.div(o, nt) * nt + (nt - 1 - lax.rem(o, nt))

    def rrow(col):
        return lambda s: (chunk_rows(jnp.minimum(s, n_chunks - 1)), col)

    const = lambda s: (0, 0)
    const3 = lambda s: (0, 0, 0)
    state_buf = pltpu.VMEM((tt, 2 * NST), F32)
    chan_buf = pltpu.VMEM((tt, SSM_W), F32)
    return pl.pallas_call(
        body, grid=(n_chunks + 2,),
        in_specs=[pl.BlockSpec((tt, SSM_W), rrow(1)),
                  pl.BlockSpec((tt, SSM_W), rrow(1)),
                  pl.BlockSpec((tt, 2 * NST), rrow(0)),
                  pl.BlockSpec((tt, SSM_W), rrow(0)),
                  pl.BlockSpec((tt, 2 * SSM_W), rrow(0)),
                  pl.BlockSpec((N_BLK, BLK_CH, 2 * BLK_ST), const3),
                  pl.BlockSpec((N_BLK, 2 * BLK_ST, BLK_CH), const3),
                  pl.BlockSpec((8 * SUBLANES, NST), const),
                  pl.BlockSpec((1, SSM_W), const),
                  pl.BlockSpec((SSM_W, 2 * SSM_W), const)],
        out_specs=[pl.BlockSpec((tt, SSM_W), lambda s: (chunk_rows(jnp.maximum(s - 2, 0)), 0)),
                   pl.BlockSpec((SSM_W, 2 * SSM_W), const),
                   pl.BlockSpec((1, SSM_W), const),
                   pl.BlockSpec((2, NST, SSM_GC), const3),
                   pl.BlockSpec((2, NST, SSM_GC), const3),
                   pl.BlockSpec((SUBLANES, 2 * NST), const)],
        out_shape=[jax.ShapeDtypeStruct((nb * l, SSM_W), BF16),
                   jax.ShapeDtypeStruct((SSM_W, 2 * SSM_W), F32),
                   jax.ShapeDtypeStruct((1, SSM_W), F32),
                   jax.ShapeDtypeStruct((2, NST, SSM_GC), F32),
                   jax.ShapeDtypeStruct((2, NST, SSM_GC), F32),
                   jax.ShapeDtypeStruct((SUBLANES, 2 * NST), F32)],
        scratch_shapes=[state_buf, state_buf, state_buf, state_buf, chan_buf, chan_buf, chan_buf, chan_buf,
                        pltpu.VMEM((SUBLANES, 2 * NST), F32),
                        pltpu.VMEM((N_BLK, 2 * BLK_ST, BLK_CH), F32), pltpu.VMEM((N_BLK, 2 * BLK_ST, BLK_CH), F32)],
        compiler_params=_params("arbitrary"), name="s5_bwd")(
            proj, d_ycat, h, ypre, z, b_m, c_m, cst_rev, d_skip, w_glu)


def _head_mask(hh):
    lane = lax.broadcasted_iota(jnp.int32, (1, ATT_W), 1)
    return (lane >= hh * MEM_HD) & (lane < (hh + 1) * MEM_HD)


def _mem_kv(mem_ref, gm_ref, wkv_ref):
    m = mem_ref[0]
    mh = m * _rms_scale(m)
    mb = (mh * gm_ref[...]).astype(BF16)
    kv = _dot(mb, wkv_ref[...])
    return mh, mb, kv[:, :ATT_W].astype(BF16), kv[:, ATT_W:].astype(BF16)


def _stack_heads(dst_ref, a):
    for hh in range(MEM_HEADS):
        dst_ref[pl.ds(hh * N_MEM, N_MEM), :] = jnp.where(_head_mask(hh), a, jnp.zeros_like(a))


def _fold_heads(a):
    out = jnp.zeros((N_MEM, ATT_W), a.dtype)
    for hh in range(MEM_HEADS):
        out = out + jnp.where(_head_mask(hh), a[hh * N_MEM:(hh + 1) * N_MEM], 0.0)
    return out


def _softmax_heads(s):
    parts = []
    for hh in range(MEM_HEADS):
        sh = s[:, hh * N_MEM:(hh + 1) * N_MEM]
        e = jnp.exp(sh - jnp.max(sh, axis=-1, keepdims=True))
        parts.append(e / jnp.sum(e, axis=-1, keepdims=True))
    return jnp.concatenate(parts, axis=1)


def attn_fwd(proj, mem, g_mem, w_kv, nb, l):
    tq = min(ATTN_ROWS, l)
    nq = l // tq
    scale = MEM_HD ** -0.5

    def body(q_ref, mem_ref, gm_ref, wkv_ref, o_ref, k_s, v_s):
        @pl.when(pl.program_id(1) == 0)
        def _():
            _, _, k, v = _mem_kv(mem_ref, gm_ref, wkv_ref)
            _stack_heads(k_s, k)
            _stack_heads(v_s, v)

        p = _softmax_heads(_dot_nt(q_ref[...], k_s[...]) * scale)
        o_ref[...] = _dot(p.astype(BF16), v_s[...]).astype(BF16)

    const = lambda b, t: (0, 0)
    stacked = pltpu.VMEM((MEM_HEADS * N_MEM, ATT_W), BF16)
    return pl.pallas_call(
        body, grid=(nb, nq),
        in_specs=[pl.BlockSpec((tq, ATT_W), lambda b, t: (b * nq + t, 3)),
                  pl.BlockSpec((1, N_MEM, D_MODEL), lambda b, t: (b, 0, 0)),
                  pl.BlockSpec((1, D_MODEL), const),
                  pl.BlockSpec((D_MODEL, 2 * ATT_W), const)],
        out_specs=pl.BlockSpec((tq, ATT_W), lambda b, t: (b * nq + t, 0)),
        out_shape=jax.ShapeDtypeStruct((nb * l, ATT_W), BF16),
        scratch_shapes=[stacked, stacked],
        compiler_params=_params("arbitrary", "arbitrary"), name="attn_fwd")(proj, mem, g_mem, w_kv)


def attn_bwd(proj, d_ycat, mem, g_mem, w_kv, nb, l):
    tq = min(ATTN_ROWS, l)
    nq = l // tq
    scale = MEM_HD ** -0.5

    def body(q_ref, do_ref, mem_ref, gm_ref, wkv_ref, dq_ref, dwkv_ref, dgm_ref, k_s, v_s, mb_s, dk_s, dv_s):
        b = pl.program_id(0)
        t = pl.program_id(1)

        @pl.when((b == 0) & (t == 0))
        def _():
            dwkv_ref[...] = jnp.zeros_like(dwkv_ref)
            dgm_ref[...] = jnp.zeros_like(dgm_ref)

        @pl.when(t == 0)
        def _():
            _, mb, k, v = _mem_kv(mem_ref, gm_ref, wkv_ref)
            _stack_heads(k_s, k)
            _stack_heads(v_s, v)
            mb_s[...] = mb
            dk_s[...] = jnp.zeros_like(dk_s)
            dv_s[...] = jnp.zeros_like(dv_s)

        q = q_ref[...]
        do = do_ref[...]
        k = k_s[...]
        p = _softmax_heads(_dot_nt(q, k) * scale)
        dp = _dot_nt(do, v_s[...])
        ds = []
        for hh in range(MEM_HEADS):
            blk = slice(hh * N_MEM, (hh + 1) * N_MEM)
            ds.append(p[:, blk] * (dp[:, blk] - jnp.sum(dp[:, blk] * p[:, blk], axis=-1, keepdims=True)) * scale)
        ds = jnp.concatenate(ds, axis=1).astype(BF16)
        dq_ref[...] = _dot(ds, k).astype(BF16)
        dk_s[...] += _dot_tn(ds, q)
        dv_s[...] += _dot_tn(p.astype(BF16), do)

        @pl.when(t == nq - 1)
        def _():
            dkv = jnp.concatenate([_fold_heads(dk_s[...]), _fold_heads(dv_s[...])], axis=1).astype(BF16)
            dwkv_ref[...] += _dot_tn(mb_s[...], dkv)
            m = mem_ref[0]
            dgm_ref[...] += jnp.sum(_dot_nt(dkv, wkv_ref[...]) * (m * _rms_scale(m)), axis=0, keepdims=True)

    const = lambda b, t: (0, 0)
    return pl.pallas_call(
        body, grid=(nb, nq),
        in_specs=[pl.BlockSpec((tq, ATT_W), lambda b, t: (b * nq + t, 3)),
                  pl.BlockSpec((tq, ATT_W), lambda b, t: (b * nq + t, 3)),
                  pl.BlockSpec((1, N_MEM, D_MODEL), lambda b, t: (b, 0, 0)),
                  pl.BlockSpec((1, D_MODEL), const),
                  pl.BlockSpec((D_MODEL, 2 * ATT_W), const)],
        out_specs=[pl.BlockSpec((tq, ATT_W), lambda b, t: (b * nq + t, 0)),
                   pl.BlockSpec((D_MODEL, 2 * ATT_W), const),
                   pl.BlockSpec((1, D_MODEL), const)],
        out_shape=[jax.ShapeDtypeStruct((nb * l, ATT_W), BF16),
                   jax.ShapeDtypeStruct((D_MODEL, 2 * ATT_W), F32),
                   jax.ShapeDtypeStruct((1, D_MODEL), F32)],
        scratch_shapes=[pltpu.VMEM((MEM_HEADS * N_MEM, ATT_W), BF16), pltpu.VMEM((MEM_HEADS * N_MEM, ATT_W), BF16),
                        pltpu.VMEM((N_MEM, D_MODEL), BF16),
                        pltpu.VMEM((MEM_HEADS * N_MEM, ATT_W), F32), pltpu.VMEM((MEM_HEADS * N_MEM, ATT_W), F32)],
        compiler_params=_params("arbitrary", "arbitrary"), name="attn_bwd")(proj, d_ycat, mem, g_mem, w_kv)


def tail(y_pool, y_ssm, y_att, proj, x2, target, w_out, g_post):
    t = x2.shape[0]
    tm = min(ROW_TILE, t)

    def body(yp_ref, ys_ref, ya_ref, gate_ref, x_ref, tg_ref, w_ref, gp_ref,
             dz_ref, dyc_ref, dgate_ref, dw_ref, dgp_ref, loss_ref):
        @pl.when(pl.program_id(0) == 0)
        def _():
            dw_ref[...] = jnp.zeros_like(dw_ref)
            dgp_ref[...] = jnp.zeros_like(dgp_ref)
            loss_ref[...] = jnp.zeros_like(loss_ref)

        ycat = jnp.concatenate([yp_ref[...], ys_ref[...], ya_ref[...]], axis=1).astype(F32)
        gate = gate_ref[...].astype(F32)
        sg = jax.nn.sigmoid(gate)
        silu = gate * sg
        yb = (ycat * silu).astype(BF16)
        w = w_ref[...]
        out = _dot(yb, w)
        r2 = _rms_scale(out)
        oh = out * r2
        gp = gp_ref[...]
        err = (x_ref[...] + oh * gp) - tg_ref[...]
        loss_ref[...] += 0.5 * jnp.sum(jnp.mean(err * err, axis=-1, keepdims=True), axis=0, keepdims=True)
        dz = err * (1.0 / D_MODEL)
        dz_ref[...] = dz.astype(BF16)
        dgp_ref[...] += jnp.sum(dz * oh, axis=0, keepdims=True)
        dn = dz * gp
        dout = (r2 * (dn - oh * jnp.mean(dn * oh, axis=-1, keepdims=True))).astype(BF16)
        dw_ref[...] += _dot_tn(yb, dout)
        dy = _dot_nt(dout, w)
        dyc_ref[...] = (dy * silu).astype(BF16)
        dgate_ref[...] = (dy * ycat * (sg * (1.0 + gate * (1.0 - sg)))).astype(BF16)

    row = lambda i: (i, 0)
    const = lambda i: (0, 0)
    full = jax.ShapeDtypeStruct((t, D_MODEL), BF16)
    return pl.pallas_call(
        body, grid=(t // tm,),
        in_specs=[pl.BlockSpec((tm, POOL_W), row),
                  pl.BlockSpec((tm, SSM_W), row),
                  pl.BlockSpec((tm, ATT_W), row),
                  pl.BlockSpec((tm, D_MODEL), lambda i: (i, 1)),
                  pl.BlockSpec((tm, D_MODEL), row),
                  pl.BlockSpec((tm, D_MODEL), row),
                  pl.BlockSpec((D_MODEL, D_MODEL), const),
                  pl.BlockSpec((1, D_MODEL), const)],
        out_specs=[pl.BlockSpec((tm, D_MODEL), row),
                   pl.BlockSpec((tm, D_MODEL), row),
                   pl.BlockSpec((tm, D_MODEL), row),
                   pl.BlockSpec((D_MODEL, D_MODEL), const),
                   pl.BlockSpec((1, D_MODEL), const),
                   pl.BlockSpec((1, LANES), const)],
        out_shape=[full, full, full,
                   jax.ShapeDtypeStruct((D_MODEL, D_MODEL), F32),
                   jax.ShapeDtypeStruct((1, D_MODEL), F32),
                   jax.ShapeDtypeStruct((1, LANES), F32)],
        compiler_params=_params("arbitrary"), name="tail")(y_pool, y_ssm, y_att, proj, x2, target, w_out, g_post)


def in_proj_bwd(du_pool, du_ssm, dq, d_gate, x2, dz, g_pre, w_in):
    t = x2.shape[0]
    tm = min(ROW_TILE, t)
    shard = W_IN_SHARD

    def body(dup_ref, dus_ref, dq_ref, dg_ref, x_ref, dz_ref, g_ref, w_ref, gx_ref, dw_ref, dgp_ref):
        @pl.when(pl.program_id(0) == 0)
        def _():
            dw_ref[...] = jnp.zeros_like(dw_ref)
            dgp_ref[...] = jnp.zeros_like(dgp_ref)

        dproj = jnp.concatenate([dup_ref[...], dus_ref[...], dq_ref[...], dg_ref[...]], axis=1)
        x = x_ref[...]
        r1 = _rms_scale(x)
        xh = x * r1
        g = g_ref[...]
        hb = (xh * g).astype(BF16)
        dh = jnp.zeros((tm, D_MODEL), F32)
        for k in range(N_CHIPS):
            dp_k = dproj[:, k * shard:(k + 1) * shard]
            dw_ref[k] += _dot_tn(hb, dp_k)
            dh = dh + _dot_nt(dp_k, w_ref[k])
        dgp_ref[...] += jnp.sum(dh * xh, axis=0, keepdims=True)
        dn = dh * g
        gx_ref[...] = dz_ref[...].astype(F32) + r1 * (dn - xh * jnp.mean(dn * xh, axis=-1, keepdims=True))

    row = lambda i: (i, 0)
    const = lambda i: (0, 0)
    return pl.pallas_call(
        body, grid=(t // tm,),
        in_specs=[pl.BlockSpec((tm, POOL_W), row),
                  pl.BlockSpec((tm, SSM_W), row),
                  pl.BlockSpec((tm, ATT_W), row),
                  pl.BlockSpec((tm, D_MODEL), row),
                  pl.BlockSpec((tm, D_MODEL), row),
                  pl.BlockSpec((tm, D_MODEL), row),
                  pl.BlockSpec((1, D_MODEL), const),
                  pl.BlockSpec((N_CHIPS, D_MODEL, shard), lambda i: (0, 0, 0))],
        out_specs=[pl.BlockSpec((tm, D_MODEL), row),
                   pl.BlockSpec((N_CHIPS, D_MODEL, shard), lambda i: (0, 0, 0)),
                   pl.BlockSpec((1, D_MODEL), const)],
        out_shape=[jax.ShapeDtypeStruct((t, D_MODEL), F32),
                   jax.ShapeDtypeStruct((N_CHIPS, D_MODEL, shard), F32),
                   jax.ShapeDtypeStruct((1, D_MODEL), F32)],
        compiler_params=_params("arbitrary"), name="in_proj_bwd")(du_pool, du_ssm, dq, d_gate, x2, dz, g_pre, w_in)


def _block_diag(blocks):
    g, r, c = blocks.shape
    eye = jnp.eye(g, dtype=blocks.dtype)
    return jnp.einsum("grc,gh->grhc", blocks, eye).reshape(g * r, g * c)


def _block_diag_extract(mat, g, r, c):
    eye = jnp.eye(g, dtype=mat.dtype)
    return jnp.einsum("grhc,gh->grc", mat.reshape(g, r, g, c), eye)


def local_step(x, mem, target, g_pre, w_in_shard, w_pool, pool_scale, a_re, a_im, log_dt, b_re, b_im, c_re, c_im,
               d_skip, later_shards, g_mem, g_post):
    nb, l, _ = x.shape
    x2 = x.reshape(nb * l, D_MODEL)
    tg2 = target.reshape(nb * l, D_MODEL)

    rowv = lambda a: a.reshape(1, NST)
    lanes_last = lambda b: b.transpose(2, 0, 1).reshape(SSM_GC, NST)
    ldt_row = rowv(jnp.broadcast_to(log_dt.reshape(SSM_NG, 1), (SSM_NG, SSM_N)))
    b_re_t = lanes_last(b_re)
    b_im_t = lanes_last(b_im)
    cst_f, cst_b = s5_scan_consts(rowv(a_re), rowv(a_im), ldt_row)
    bbr, bbi = s5_bbar(rowv(a_re), rowv(a_im), ldt_row, b_re_t, b_im_t)
    eye = jnp.eye(BLK_GROUPS, dtype=F32)
    blk_in = lambda bb: jnp.einsum("cmgn,gh->mhcgn", bb.reshape(SSM_GC, N_BLK, BLK_GROUPS, SSM_N), eye).reshape(
        N_BLK, BLK_CH, BLK_ST)
    blk_out = lambda cc: jnp.einsum("mgcn,gh->mgnhc", cc.reshape(N_BLK, BLK_GROUPS, SSM_GC, SSM_N), eye).reshape(
        N_BLK, BLK_ST, BLK_CH)
    b_m = jnp.concatenate([blk_in(bbr), blk_in(bbi)], axis=2).astype(BF16)
    c_m = jnp.concatenate([blk_out(c_re), -blk_out(c_im)], axis=1).astype(BF16)
    w_pool_blk = _block_diag(w_pool.reshape(4, POOL_GW, POOL_GW)).astype(BF16)

    proj, w_in, (g_glu, g_kv, g_out) = in_proj(x2, g_pre, w_in_shard, later_shards)
    w_glu = g_glu.transpose(2, 0, 1).reshape(SSM_W, 2 * SSM_W)
    w_kv = g_kv.reshape(D_MODEL, 2 * ATT_W)
    w_out = g_out.reshape(D_MODEL, D_MODEL)
    y_pool = pool_fwd(proj, w_pool_blk, pool_scale, nb, l)
    y_ssm, h, ypre, z = s5_fwd(proj, b_m, c_m, cst_f, d_skip, w_glu, nb, l)
    y_att = attn_fwd(proj, mem, g_mem, w_kv, nb, l)
    dz, d_ycat, d_gate, d_w_out, d_g_post, loss = tail(y_pool, y_ssm, y_att, proj, x2, tg2, w_out, g_post)
    du_pool, d_wp_blk, d_pool_scale = pool_bwd(proj, d_ycat, w_pool_blk, pool_scale, nb, l)
    du_ssm, d_w_glu, d_d_skip, d_cc, d_bb, d_lam = s5_bwd(
        proj, d_ycat, h, ypre, z, b_m, c_m, cst_b, d_skip, w_glu, nb, l)
    dq, d_w_kv, d_g_mem = attn_bwd(proj, d_ycat, mem, g_mem, w_kv, nb, l)
    grad_x, d_w_in, d_g_pre = in_proj_bwd(du_pool, du_ssm, dq, d_gate, x2, dz, g_pre, w_in)

    d_c = d_cc[0].reshape(SSM_NG, SSM_N, SSM_GC)
    d_ci = d_cc[1].reshape(SSM_NG, SSM_N, SSM_GC)
    d_lam_row = d_lam[0]
    d_ar, d_ai, d_ld, d_br, d_bi = s5_param_bwd(
        rowv(a_re), rowv(a_im), ldt_row, b_re_t, b_im_t,
        d_lam_row[:NST].reshape(1, NST), d_lam_row[NST:].reshape(1, NST), d_bb[0].T, d_bb[1].T)
    from_lanes_last = lambda b: b.reshape(SSM_GC, SSM_NG, SSM_N).transpose(1, 0, 2).reshape(1, SSM_NG, SSM_GC, SSM_N)

    grads = {
        "g_pre": d_g_pre,
        "w_in": d_w_in,
        "w_pool": _block_diag_extract(d_wp_blk, 4, POOL_GW, POOL_GW).reshape(1, 4, POOL_GW, POOL_GW),
        "pool_scale": d_pool_scale,
        "a_re": d_ar.reshape(1, SSM_NG, SSM_N),
        "a_im": d_ai.reshape(1, SSM_NG, SSM_N),
        "log_dt": d_ld[:, :SSM_NG],
        "b_re": from_lanes_last(d_br),
        "b_im": from_lanes_last(d_bi),
        "c_re": d_c.transpose(0, 2, 1).reshape(1, SSM_NG, SSM_GC, SSM_N),
        "c_im": (-d_ci).transpose(0, 2, 1).reshape(1, SSM_NG, SSM_GC, SSM_N),
        "d_skip": d_d_skip,
        "w_glu": d_w_glu,
        "g_mem": d_g_mem,
        "w_kv": d_w_kv,
        "w_out": d_w_out,
        "g_post": d_g_post,
    }
    return loss, grad_x.reshape(nb, l, D_MODEL), grads


VMEM_SPEC = pl.BlockSpec(memory_space=pltpu.VMEM)
ANY_SPEC = pl.BlockSpec(memory_space=pl.ANY)


def _place():
    return lax.axis_index("x"), lax.axis_index("y"), lax.axis_index("c")


def _other_chips(x, y):
    return [(1 - x, y), (x, 1 - y), (1 - x, 1 - y)]


SEMS_PER_ITEM = 5


def reduce_all(g4s, packed):
    n = len(g4s)
    dims = [g.shape[1:] for g in g4s]
    pr = packed.shape[0]

    def body(*refs):
        g_refs, p_ref = refs[:n], refs[n]
        outs, op = refs[n + 1:2 * n + 1], refs[2 * n + 1]
        scr = refs[2 * n + 2:]
        mine, theirs, sendb, recvb = scr[0:n], scr[n:2 * n], scr[2 * n:3 * n], scr[3 * n:4 * n]
        p_theirs, gat, lsems, send_sems, recv_sems = scr[4 * n:]
        x, y, c = _place()
        k = 2 * x + y
        chips = _other_chips(x, y)
        sib = (x, y, 1 - c)

        def halves(rows):
            hr = rows // 2
            return (pl.ds(pl.multiple_of(c * hr, SUBLANES), hr), pl.ds(pl.multiple_of((1 - c) * hr, SUBLANES), hr))

        def remote(src, dst, sem, to):
            return pltpu.make_async_remote_copy(
                src_ref=src, dst_ref=dst, send_sem=send_sems.at[sem], recv_sem=recv_sems.at[sem],
                device_id=to, device_id_type=MESH)

        loads, started = [], []
        for i in range(n):
            my_rows, sib_rows = halves(dims[i][0])
            ld = pltpu.make_async_copy(g_refs[i].at[:, my_rows, :], mine[i], lsems.at[i])
            ld.start()
            loads.append(ld)
            sw = remote(g_refs[i].at[:, sib_rows, :], theirs[i], SEMS_PER_ITEM * i, sib)
            sw.start()
            started.append(sw)
        p_my, p_sib = halves(pr)
        p_swap = remote(p_ref.at[p_sib, :], p_theirs, SEMS_PER_ITEM * n, sib)
        p_swap.start()
        started.append(p_swap)

        p_swap.wait_recv()
        gat[k] = p_ref[p_my, :] + p_theirs[...]
        for j, (px, py) in enumerate(chips):
            cp = remote(gat.at[k], gat.at[k], SEMS_PER_ITEM * n + 1 + j, (px, py, c))
            cp.start()
            started.append(cp)
        ici = []
        for i in range(n):
            loads[i].wait()
            started[i].wait_recv()
            for j, (px, py) in enumerate(chips):
                kp = 2 * px + py
                sendb[i][j] = (mine[i][kp] + theirs[i][kp]).astype(BF16)
                cp = remote(sendb[i].at[j], recvb[i].at[j], SEMS_PER_ITEM * i + 1 + j, (px, py, c))
                cp.start()
                ici.append(cp)

        for j, (px, py) in enumerate(chips):
            kp = 2 * px + py
            remote(gat.at[kp], gat.at[kp], SEMS_PER_ITEM * n + 1 + j, (px, py, c)).wait_recv()
        op[p_my, :] = ((gat[0] + gat[1]) + gat[2]) + gat[3]
        last = remote(op.at[p_my, :], op.at[p_my, :], SEMS_PER_ITEM * n + 4, sib)
        last.start()
        started.append(last)
        for i in range(n):
            my_rows, _ = halves(dims[i][0])
            half = mine[i][k] + theirs[i][k]
            for j in range(3):
                ici[3 * i + j].wait_recv()
                half = half + recvb[i][j].astype(F32)
            outs[i][my_rows, :] = half
            last = remote(outs[i].at[my_rows, :], outs[i].at[my_rows, :], SEMS_PER_ITEM * i + 4, sib)
            last.start()
            started.append(last)
        remote(op.at[p_sib, :], op.at[p_sib, :], SEMS_PER_ITEM * n + 4, sib).wait_recv()
        for i in range(n):
            _, sib_rows = halves(dims[i][0])
            remote(outs[i].at[sib_rows, :], outs[i].at[sib_rows, :], SEMS_PER_ITEM * i + 4, sib).wait_recv()
        for cp in started + ici:
            cp.wait_send()

    n_sems = SEMS_PER_ITEM * (n + 1)
    scratch = ([pltpu.VMEM((N_CHIPS, r // 2, cd), F32) for r, cd in dims] * 2
               + [pltpu.VMEM((3, r // 2, cd), BF16) for r, cd in dims] * 2
               + [pltpu.VMEM((pr // 2, LANES), F32), pltpu.VMEM((N_CHIPS, pr // 2, LANES), F32),
                  pltpu.SemaphoreType.DMA((n,)), pltpu.SemaphoreType.DMA((n_sems,)), pltpu.SemaphoreType.DMA((n_sems,))])
    res = pl.pallas_call(
        body,
        out_shape=[jax.ShapeDtypeStruct(d, F32) for d in dims] + [jax.ShapeDtypeStruct(packed.shape, F32)],
        in_specs=[ANY_SPEC] * n + [VMEM_SPEC], out_specs=[VMEM_SPEC] * (n + 1),
        scratch_shapes=scratch,
        compiler_params=pltpu.CompilerParams(vmem_limit_bytes=VMEM_LIMIT),
        name="reduce_all")(*g4s, packed)
    return res[:n], res[n]


def adamw_all(ws, gs, ms, vs):
    n = len(ws)

    def body(*refs):
        w, g, m, v = refs[:n], refs[n:2 * n], refs[2 * n:3 * n], refs[3 * n:4 * n]
        od, om, ov = refs[4 * n:5 * n], refs[5 * n:6 * n], refs[6 * n:]
        for i in range(n):
            od[i][...], om[i][...], ov[i][...] = _adamw(w[i][...], g[i][...], m[i][...], v[i][...])

    shapes = [jax.ShapeDtypeStruct(a.shape, F32) for a in ws]
    res = pl.pallas_call(
        body, out_shape=shapes * 3, in_specs=[VMEM_SPEC] * (4 * n), out_specs=[VMEM_SPEC] * (3 * n),
        compiler_params=pltpu.CompilerParams(vmem_limit_bytes=VMEM_LIMIT),
        name="adamw_all")(*ws, *gs, *ms, *vs)
    return res[:n], res[n:2 * n], res[2 * n:]


WEIGHTS = ("g_pre", "w_in", "w_pool", "pool_scale", "a_re", "a_im", "log_dt", "b_re", "b_im", "c_re", "c_im", "d_skip",
           "w_glu", "g_mem", "w_kv", "w_out", "g_post")
SHARDED = ("w_in", "w_glu", "w_kv", "w_out")
SWAPPED = ("b_re", "b_im", "w_glu")
REPLICATED = tuple(n for n in WEIGHTS if n not in SHARDED)
PACK_TILE = SUBLANES * LANES


def _pack(arrays):
    parts = []
    for a in arrays:
        flat = a.reshape(-1)
        parts.append(jnp.pad(flat, (0, -flat.shape[0] % PACK_TILE)).reshape(-1, LANES))
    rows = sum(p.shape[0] for p in parts)
    if rows % (2 * SUBLANES):
        parts.append(jnp.zeros((SUBLANES, LANES), F32))
    return jnp.concatenate(parts, axis=0)


def _unpack(packed, shapes):
    out, row = [], 0
    for shp in shapes:
        size = math.prod(shp)
        rows = -(-size // PACK_TILE) * SUBLANES
        out.append(packed[row:row + rows].reshape(-1)[:size].reshape(shp))
        row += rows
    return out


def kernel(x, mem, g_pre, w_in, w_pool, pool_scale, a_re, a_im, log_dt, b_re, b_im, c_re, c_im, d_skip, w_glu, g_mem, w_kv, w_out, g_post, loss_target, m_g_pre, m_w_in, m_w_pool, m_pool_scale, m_a_re, m_a_im, m_log_dt, m_b_re, m_b_im, m_c_re, m_c_im, m_d_skip, m_w_glu, m_g_mem, m_w_kv, m_w_out, m_g_post, v_g_pre, v_w_in, v_w_pool, v_pool_scale, v_a_re, v_a_im, v_log_dt, v_b_re, v_b_im, v_c_re, v_c_im, v_d_skip, v_w_glu, v_g_mem, v_w_kv, v_w_out, v_g_post):
    given = dict(locals())
    wts = {n: given[n] for n in WEIGHTS}
    mom = {n: given["m_" + n] for n in WEIGHTS}
    var = {n: given["v_" + n] for n in WEIGHTS}

    swap = lambda a: jnp.swapaxes(a, -1, -2)
    view = lambda n, a: swap(a) if n in SWAPPED else a

    loss_part, grad_x, grads = local_step(
        x, mem, loss_target, g_pre, w_in[0], w_pool[0], pool_scale, a_re[0], a_im[0], log_dt[0], b_re[0], b_im[0],
        c_re[0], c_im[0], d_skip, [swap(w_glu)[0], w_kv[0], w_out[0]], g_mem, g_post)

    glu_cols = 2 * SSM_W // N_CHIPS
    partial4 = [
        grads["w_in"],
        grads["w_glu"].reshape(SSM_W, N_CHIPS, glu_cols).transpose(1, 2, 0),
        grads["w_kv"].reshape(N_CHIPS, D_MODEL // N_CHIPS, 2 * ATT_W),
        grads["w_out"].reshape(N_CHIPS, D_MODEL // N_CHIPS, D_MODEL),
    ]
    loss_tile = jnp.broadcast_to(loss_part, (SUBLANES, LANES))
    sharded_g, packed_g = reduce_all(partial4, _pack([grads[n] for n in REPLICATED] + [loss_tile]))
    small_g = _unpack(packed_g, [view(n, wts[n]).shape for n in REPLICATED] + [(SUBLANES, LANES)])
    loss = small_g[-1][0, 0]
    grad = dict(zip(SHARDED, [g[None] for g in sharded_g]))
    grad.update(zip(REPLICATED, small_g[:-1]))

    deltas, new_ms, new_vs = adamw_all([view(n, wts[n]) for n in WEIGHTS], [grad[n] for n in WEIGHTS],
                                       [view(n, mom[n]) for n in WEIGHTS], [view(n, var[n]) for n in WEIGHTS])
    back = lambda arrs: [view(n, a) for n, a in zip(WEIGHTS, arrs)]
    return (loss, grad_x, *back([grad[n] for n in WEIGHTS]), *back(deltas), *back(new_ms), *back(new_vs))
```

```python
import functools
import math

import jax
import jax.numpy as jnp
from jax import lax
from jax.experimental import pallas as pl
from jax.experimental.pallas import tpu as pltpu

F32 = jnp.float32
BF16 = jnp.bfloat16

D_MODEL = 1024
POOL_W = 384
SSM_W = 384
ATT_W = 256
POOL_GW = 96
POOL_WINDOWS = (2, 4, 8, 16)
POOL_PAD = 16
SSM_NG = 24
SSM_N = 64
SSM_GC = 16
N_CHIPS = 4
N_DEV = 8
W_IN_SHARD = 2 * D_MODEL // N_CHIPS
NST = SSM_NG * SSM_N
BLK_CH = 128
BLK_GROUPS = BLK_CH // SSM_GC
BLK_ST = BLK_GROUPS * SSM_N
N_BLK = SSM_W // BLK_CH
N_MEM = 256
MEM_HEADS = 4
MEM_HD = 64
EPS = 1e-6

ADAM_LR = 0.001
ADAM_B1 = 0.9
ADAM_B2 = 0.999
ADAM_EPS = 1e-08
ADAM_WD = 0.01
ADAM_STEP = 10

SUBLANES = 8
LANES = 128
V7X_VMEM_BYTES = 64 * 2**20
VMEM_LIMIT = V7X_VMEM_BYTES - 8 * 2**20
SCAN_COLS = 512
ROW_TILE = 512
IN_PROJ_ROWS = 2048
ATTN_ROWS = 1024
MESH = pl.DeviceIdType.MESH

NT = (((1,), (1,)), ((), ()))
TN = (((0,), (0,)), ((), ()))


def _params(*sem):
    return pltpu.CompilerParams(dimension_semantics=sem, vmem_limit_bytes=VMEM_LIMIT)


def _dot(a, b):
    return jnp.dot(a, b, preferred_element_type=F32)


def _dot_nt(a, b):
    return lax.dot_general(a, b, NT, preferred_element_type=F32)


def _dot_tn(a, b):
    return lax.dot_general(a, b, TN, preferred_element_type=F32)


def _rms_scale(v):
    return lax.rsqrt(jnp.mean(v * v, axis=-1, keepdims=True) + EPS)


def _adamw(w, g, m, v):
    m = ADAM_B1 * m + (1.0 - ADAM_B1) * g
    v = ADAM_B2 * v + (1.0 - ADAM_B2) * (g * g)
    m_hat = m / (1.0 - ADAM_B1 ** ADAM_STEP)
    v_hat = v / (1.0 - ADAM_B2 ** ADAM_STEP)
    delta = -ADAM_LR * (m_hat / (jnp.sqrt(v_hat) + ADAM_EPS) + ADAM_WD * w)
    return delta, m, v


def _phase_shard(j):
    cx, cy = lax.axis_index("x"), lax.axis_index("y")
    others = [2 * px + py for px, py in _other_chips(cx, cy)]
    own = 2 * cx + cy
    return jnp.where(j == 0, own, jnp.where(j == 1, others[0], jnp.where(j == 2, others[1], others[2])))


def in_proj(x2, g_pre, w_in_shard, later_shards):
    t = x2.shape[0]
    tm = min(IN_PROJ_ROWS, t)
    n_tiles = t // tm
    n = len(later_shards)

    def body(x_ref, g_ref, ws_ref, *rest):
        shard_refs = rest[:n]
        o_ref, w_out_ref, gathered = rest[n], rest[n + 1], rest[n + 2:2 * n + 2]
        w_buf, h_buf = rest[2 * n + 2], rest[2 * n + 3]
        cast_bufs = rest[2 * n + 4:3 * n + 4]
        local_sems, send_sems, recv_sems = rest[3 * n + 4:]
        j = pl.program_id(0)
        i = pl.program_id(1)
        cx, cy, cc = _place()
        k = 2 * cx + cy
        chips = _other_chips(cx, cy)

        half = w_in_shard.shape[0] // 2

        def w_copy(p, block, core_half, forward):
            px, py = chips[p]
            rows = w_buf.at[block, pl.ds(pl.multiple_of(core_half * half, 2 * SUBLANES), half)]
            sem = 3 + p if forward else p
            return pltpu.make_async_remote_copy(
                src_ref=rows, dst_ref=rows, send_sem=send_sems.at[sem], recv_sem=recv_sems.at[sem],
                device_id=(cx, cy, 1 - cc) if forward else (px, py, cc), device_id_type=MESH)

        def own(m):
            return pltpu.make_async_copy(cast_bufs[m], gathered[m].at[k], local_sems.at[m])

        def later_copy(m, p, block, core_half, forward):
            px, py = chips[p]
            hr = later_shards[m].shape[0] // 2
            rows = pl.ds(pl.multiple_of(core_half * hr, 2 * SUBLANES), hr)
            dst = gathered[m].at[block, rows]
            sem = 6 + 6 * m + (3 + p if forward else p)
            return pltpu.make_async_remote_copy(
                src_ref=dst if forward else cast_bufs[m].at[rows], dst_ref=dst, send_sem=send_sems.at[sem],
                recv_sem=recv_sems.at[sem], device_id=(cx, cy, 1 - cc) if forward else (px, py, cc),
                device_id_type=MESH)

        @pl.when((j == 0) & (i == 0))
        def _():
            w_buf[k] = ws_ref[...].astype(BF16)
            for p in range(3):
                w_copy(p, k, cc, False).start()
            for m in range(n):
                cast_bufs[m][...] = shard_refs[m][...].astype(BF16)
                own(m).start()
                for p in range(3):
                    later_copy(m, p, k, cc, False).start()

        rows = pl.ds(pl.multiple_of(i * tm, tm), tm)

        @pl.when(j == 0)
        def _():
            x = x_ref[...]
            h_buf[rows, :] = (x * _rms_scale(x) * g_ref[...]).astype(BF16)

        for p in range(3):
            @pl.when((j == p + 1) & (i == 0))
            def _(p=p):
                px, py = chips[p]
                w_copy(p, 2 * px + py, cc, False).wait_recv()
                w_copy(p, 2 * px + py, cc, True).start()
                w_copy(p, 2 * px + py, 1 - cc, True).wait_recv()

        o_ref[...] = _dot(h_buf[rows, :], w_buf[_phase_shard(j)]).astype(BF16)

        @pl.when((j == N_CHIPS - 1) & (i == n_tiles - 1))
        def _():
            out_cp = pltpu.make_async_copy(w_buf, w_out_ref, local_sems.at[n])
            out_cp.start()
            for m in range(n):
                own(m).wait()
                for p in range(3):
                    px, py = chips[p]
                    later_copy(m, p, 2 * px + py, cc, False).wait_recv()
                    later_copy(m, p, 2 * px + py, cc, True).start()
            for m in range(n):
                for p in range(3):
                    px, py = chips[p]
                    later_copy(m, p, 2 * px + py, 1 - cc, True).wait_recv()
            for p in range(3):
                px, py = chips[p]
                w_copy(p, k, cc, False).wait_send()
                w_copy(p, 2 * px + py, cc, True).wait_send()
            for m in range(n):
                for p in range(3):
                    px, py = chips[p]
                    later_copy(m, p, k, cc, False).wait_send()
                    later_copy(m, p, 2 * px + py, cc, True).wait_send()
            out_cp.wait()

    whole = lambda a: pl.BlockSpec(a.shape, lambda j, i: (0,) * a.ndim)
    res = pl.pallas_call(
        body, grid=(N_CHIPS, n_tiles),
        in_specs=[pl.BlockSpec((tm, D_MODEL), lambda j, i: (jnp.where(j == 0, i, n_tiles - 1), 0)),
                  pl.BlockSpec((1, D_MODEL), lambda j, i: (0, 0)),
                  whole(w_in_shard)] + [whole(s) for s in later_shards],
        out_specs=[pl.BlockSpec((tm, W_IN_SHARD), lambda j, i: (i, _phase_shard(j)))] + [ANY_SPEC] * (n + 1),
        out_shape=([jax.ShapeDtypeStruct((t, 2 * D_MODEL), BF16),
                    jax.ShapeDtypeStruct((N_CHIPS,) + w_in_shard.shape, BF16)]
                   + [jax.ShapeDtypeStruct((N_CHIPS,) + s.shape, BF16) for s in later_shards]),
        scratch_shapes=([pltpu.VMEM((N_CHIPS,) + w_in_shard.shape, BF16), pltpu.VMEM((t, D_MODEL), BF16)]
                        + [pltpu.VMEM(s.shape, BF16) for s in later_shards]
                        + [pltpu.SemaphoreType.DMA((n + 1,)), pltpu.SemaphoreType.DMA((6 * n + 6,)),
                           pltpu.SemaphoreType.DMA((6 * n + 6,))]),
        compiler_params=_params("arbitrary", "arbitrary"), name="in_proj")(x2, g_pre, w_in_shard, *later_shards)
    return res[0], res[1], res[2:]


def _pool_lane_window(shape):
    ch = lax.broadcasted_iota(jnp.int32, shape, 1)
    return jnp.where(ch < POOL_GW, 2.0, jnp.where(ch < 2 * POOL_GW, 4.0, jnp.where(ch < 3 * POOL_GW, 8.0, 16.0)))


def _pool_select(win, s2, s4, s8, s16):
    return jnp.where(win == 2.0, s2, jnp.where(win == 4.0, s4, jnp.where(win == 8.0, s8, s16)))


def _pool_div_count(v, win):
    pos = (lax.broadcasted_iota(jnp.int32, (POOL_PAD, POOL_W), 0) + 1).astype(F32)
    head = v[:POOL_PAD] / jnp.minimum(pos, win)
    return jnp.concatenate([head, v[POOL_PAD:] * (1.0 / win)], axis=0)


def _pool_diff(u, pad_ref, l):
    lo = POOL_PAD
    pad_ref[pl.ds(lo, l), :] = u
    s2 = u + pad_ref[pl.ds(lo - 1, l), :]
    pad_ref[pl.ds(lo, l), :] = s2
    s4 = s2 + pad_ref[pl.ds(lo - 2, l), :]
    pad_ref[pl.ds(lo, l), :] = s4
    s8 = s4 + pad_ref[pl.ds(lo - 4, l), :]
    pad_ref[pl.ds(lo, l), :] = s8
    s16 = s8 + pad_ref[pl.ds(lo - 8, l), :]
    win = _pool_lane_window((1, POOL_W))
    return _pool_div_count(_pool_select(win, s2, s4, s8, s16), win) - u, win


def pool_fwd(proj, w_blk, pool_scale, nb, l):
    def body(u_ref, w_ref, ps_ref, y_ref, pad_ref):
        pad_ref[pl.ds(0, POOL_PAD), :] = jnp.zeros((POOL_PAD, POOL_W), F32)
        d, _ = _pool_diff(u_ref[...].astype(F32), pad_ref, l)
        y_ref[...] = (_dot(d.astype(BF16), w_ref[...]) * ps_ref[...]).astype(BF16)

    return pl.pallas_call(
        body, grid=(nb,),
        in_specs=[pl.BlockSpec((l, POOL_W), lambda b: (b, 0)),
                  pl.BlockSpec((POOL_W, POOL_W), lambda b: (0, 0)),
                  pl.BlockSpec((1, POOL_W), lambda b: (0, 0))],
        out_specs=pl.BlockSpec((l, POOL_W), lambda b: (b, 0)),
        out_shape=jax.ShapeDtypeStruct((nb * l, POOL_W), BF16),
        scratch_shapes=[pltpu.VMEM((POOL_PAD + l, POOL_W), F32)],
        compiler_params=_params("arbitrary"), name="pool_fwd")(proj, w_blk, pool_scale)


def pool_bwd(proj, d_ycat, w_blk, pool_scale, nb, l):
    def body(u_ref, dy_ref, w_ref, ps_ref, du_ref, dw_ref, dps_ref, pad_ref, padb_ref):
        b = pl.program_id(0)

        @pl.when(b == 0)
        def _():
            dw_ref[...] = jnp.zeros_like(dw_ref)
            dps_ref[...] = jnp.zeros_like(dps_ref)

        pad_ref[pl.ds(0, POOL_PAD), :] = jnp.zeros((POOL_PAD, POOL_W), F32)
        padb_ref[pl.ds(l, POOL_PAD), :] = jnp.zeros((POOL_PAD, POOL_W), F32)
        d, win = _pool_diff(u_ref[...].astype(F32), pad_ref, l)
        db = d.astype(BF16)
        w = w_ref[...]
        dy = dy_ref[...].astype(F32)
        dps_ref[...] += jnp.sum(dy * _dot(db, w), axis=0, keepdims=True)
        dyo = (dy * ps_ref[...]).astype(BF16)
        dw_ref[...] += _dot_tn(db, dyo)
        dd = _dot_nt(dyo, w)
        e = _pool_div_count(dd, win)
        padb_ref[pl.ds(0, l), :] = e
        f2 = e + padb_ref[pl.ds(1, l), :]
        padb_ref[pl.ds(0, l), :] = f2
        f4 = f2 + padb_ref[pl.ds(2, l), :]
        padb_ref[pl.ds(0, l), :] = f4
        f8 = f4 + padb_ref[pl.ds(4, l), :]
        padb_ref[pl.ds(0, l), :] = f8
        f16 = f8 + padb_ref[pl.ds(8, l), :]
        du_ref[...] = (_pool_select(win, f2, f4, f8, f16) - dd).astype(BF16)

    return pl.pallas_call(
        body, grid=(nb,),
        in_specs=[pl.BlockSpec((l, POOL_W), lambda b: (b, 0)),
                  pl.BlockSpec((l, POOL_W), lambda b: (b, 0)),
                  pl.BlockSpec((POOL_W, POOL_W), lambda b: (0, 0)),
                  pl.BlockSpec((1, POOL_W), lambda b: (0, 0))],
        out_specs=[pl.BlockSpec((l, POOL_W), lambda b: (b, 0)),
                   pl.BlockSpec((POOL_W, POOL_W), lambda b: (0, 0)),
                   pl.BlockSpec((1, POOL_W), lambda b: (0, 0))],
        out_shape=[jax.ShapeDtypeStruct((nb * l, POOL_W), BF16),
                   jax.ShapeDtypeStruct((POOL_W, POOL_W), F32),
                   jax.ShapeDtypeStruct((1, POOL_W), F32)],
        scratch_shapes=[pltpu.VMEM((POOL_PAD + l, POOL_W), F32), pltpu.VMEM((POOL_PAD + l, POOL_W), F32)],
        compiler_params=_params("arbitrary"), name="pool_bwd")(proj, d_ycat, w_blk, pool_scale)


def _discretise(a_re, a_im, ldt, b_re, b_im):
    dt = jnp.exp(ldt)
    mag = jnp.exp(a_re * dt)
    th = a_im * dt
    lr = mag * jnp.cos(th)
    li = mag * jnp.sin(th)
    nr = lr - 1.0
    den = a_re * a_re + a_im * a_im
    fr = (nr * a_re + li * a_im) / den
    fi = (li * a_re - nr * a_im) / den
    return lr, li, fr * b_re - fi * b_im, fr * b_im + fi * b_re


def _cmul(ar, ai, br, bi):
    return ar * br - ai * bi, ar * bi + ai * br


def s5_scan_consts(a_re_row, a_im_row, ldt_row):
    def body(ar_ref, ai_ref, ld_ref, f_ref, b_ref):
        dt = jnp.exp(ld_ref[...])
        mag = jnp.exp(ar_ref[...] * dt)
        th = ai_ref[...] * dt
        lr = mag * jnp.cos(th)
        li = mag * jnp.sin(th)
        r = lax.broadcasted_iota(jnp.int32, (SUBLANES, NST), 0)
        for out_ref, sign, rev in ((f_ref, 1.0, False), (b_ref, -1.0, True)):
            p = [(lr, sign * li)]
            for _ in range(SUBLANES - 1):
                p.append(_cmul(p[-1][0], p[-1][1], lr, sign * li))
            for idx, s in enumerate((1, 2, 4)):
                inside = (r < SUBLANES - s) if rev else (r >= s)
                out_ref[pl.ds(2 * idx * SUBLANES, SUBLANES), :] = jnp.where(inside, p[s - 1][0], 0.0)
                out_ref[pl.ds((2 * idx + 1) * SUBLANES, SUBLANES), :] = jnp.where(inside, p[s - 1][1], 0.0)
            pr = jnp.zeros((SUBLANES, NST), F32)
            pi = jnp.zeros((SUBLANES, NST), F32)
            for row in range(SUBLANES):
                power = (SUBLANES - row) if rev else (row + 1)
                pr = jnp.where(r == row, p[power - 1][0], pr)
                pi = jnp.where(r == row, p[power - 1][1], pi)
            out_ref[pl.ds(6 * SUBLANES, SUBLANES), :] = pr
            out_ref[pl.ds(7 * SUBLANES, SUBLANES), :] = pi

    shape = jax.ShapeDtypeStruct((8 * SUBLANES, NST), F32)
    return pl.pallas_call(body, out_shape=[shape, shape], name="s5_scan_consts")(a_re_row, a_im_row, ldt_row)


def s5_bbar(a_re_row, a_im_row, ldt_row, b_re_t, b_im_t):
    def body(ar, ai, ld, br, bi, o_re, o_im):
        _, _, bbr, bbi = _discretise(ar[...], ai[...], ld[...], br[...], bi[...])
        o_re[...] = bbr
        o_im[...] = bbi

    shape = jax.ShapeDtypeStruct((SSM_GC, NST), F32)
    return pl.pallas_call(body, out_shape=[shape, shape], name="s5_bbar")(a_re_row, a_im_row, ldt_row, b_re_t, b_im_t)


def s5_param_bwd(a_re_row, a_im_row, ldt_row, b_re_t, b_im_t, d_lr, d_li, d_bbr, d_bbi):
    def body(ar, ai, ld, br, bi, g_lr, g_li, g_br, g_bi, o_ar, o_ai, o_ld, o_br, o_bi):
        _, vjp = jax.vjp(_discretise, ar[...], ai[...], ld[...], br[...], bi[...])
        d_ar, d_ai, d_ld, d_br, d_bi = vjp((g_lr[...], g_li[...], g_br[...], g_bi[...]))
        o_ar[...] = d_ar
        o_ai[...] = d_ai
        state = lax.broadcasted_iota(jnp.int32, (NST, LANES), 0)
        lane = lax.broadcasted_iota(jnp.int32, (NST, LANES), 1)
        in_group = jnp.where((state >= lane * SSM_N) & (state < (lane + 1) * SSM_N), 1.0, 0.0)
        o_ld[...] = jnp.dot(d_ld, in_group, precision=lax.Precision.HIGHEST, preferred_element_type=F32)
        o_br[...] = d_br
        o_bi[...] = d_bi

    row = jax.ShapeDtypeStruct((1, NST), F32)
    mat = jax.ShapeDtypeStruct((SSM_GC, NST), F32)
    grp = jax.ShapeDtypeStruct((1, LANES), F32)
    return pl.pallas_call(body, out_shape=[row, row, grp, mat, mat], name="s5_param_bwd")(
        a_re_row, a_im_row, ldt_row, b_re_t, b_im_t, d_lr, d_li, d_bbr, d_bbi)


def _scan_tiles(buf_ref, cst_ref, carry_ref, rows, reverse, reset, h_ref=None, acc_ref=None):
    n_tiles = rows // SUBLANES
    shifts = ((1, 0), (2, 2), (4, 4))
    row_id = lax.broadcasted_iota(jnp.int32, (SUBLANES, SCAN_COLS), 0)
    cols = [(j * SCAN_COLS, NST + j * SCAN_COLS) for j in range(NST // SCAN_COLS)]
    edge = 0 if reverse else SUBLANES - 1
    carry = jnp.where(reset, 0.0, carry_ref[...])
    accs = [(acc_ref[:, pl.ds(o_re, SCAN_COLS)], acc_ref[:, pl.ds(o_im, SCAN_COLS)]) for o_re, o_im in cols] if reverse else None
    for i in range(n_tiles):
        start = ((n_tiles - 1 - i) if reverse else i) * SUBLANES
        rws = pl.ds(start, SUBLANES)
        for j, (o_re, o_im) in enumerate(cols):
            c_re = pl.ds(o_re, SCAN_COLS)
            c_im = pl.ds(o_im, SCAN_COLS)
            xr = buf_ref[rws, c_re]
            xi = buf_ref[rws, c_im]
            for s, base in shifts:
                amount = (SUBLANES - s) if reverse else s
                sr = pltpu.roll(xr, amount, 0)
                si = pltpu.roll(xi, amount, 0)
                mr = cst_ref[pl.ds(base * SUBLANES, SUBLANES), c_re]
                mi = cst_ref[pl.ds((base + 1) * SUBLANES, SUBLANES), c_re]
                xr, xi = xr + (mr * sr - mi * si), xi + (mr * si + mi * sr)
            pr = cst_ref[pl.ds(6 * SUBLANES, SUBLANES), c_re]
            pi = cst_ref[pl.ds(7 * SUBLANES, SUBLANES), c_re]
            cr = carry[:, o_re:o_re + SCAN_COLS]
            ci = carry[:, o_im:o_im + SCAN_COLS]
            xr, xi = xr + (pr * cr - pi * ci), xi + (pr * ci + pi * cr)
            buf_ref[rws, c_re] = xr
            buf_ref[rws, c_im] = xi
            if reverse:
                gnr = jnp.where(row_id == SUBLANES - 1, cr, pltpu.roll(xr, SUBLANES - 1, 0))
                gni = jnp.where(row_id == SUBLANES - 1, ci, pltpu.roll(xi, SUBLANES - 1, 0))
                hr = h_ref[rws, c_re]
                hi = h_ref[rws, c_im]
                accs[j] = (accs[j][0] + (gnr * hr + gni * hi), accs[j][1] + (gni * hr - gnr * hi))
        carry = jnp.broadcast_to(buf_ref[pl.ds(start + edge, 1), :], (SUBLANES, 2 * NST))
    carry_ref[...] = carry
    if reverse:
        for (o_re, o_im), (a_re_, a_im_) in zip(cols, accs):
            acc_ref[:, pl.ds(o_re, SCAN_COLS)] = a_re_
            acc_ref[:, pl.ds(o_im, SCAN_COLS)] = a_im_


def _state_block(v, m):
    return jnp.concatenate([v[:, m * BLK_ST:(m + 1) * BLK_ST], v[:, NST + m * BLK_ST:NST + (m + 1) * BLK_ST]], axis=1)


def _put_state_block(buf_ref, m, val):
    buf_ref[:, pl.ds(m * BLK_ST, BLK_ST)] = val[:, :BLK_ST]
    buf_ref[:, pl.ds(NST + m * BLK_ST, BLK_ST)] = val[:, BLK_ST:]


def s5_fwd(proj, b_m, c_m, cst, d_skip, w_glu, nb, l):
    tt = min(256, l)
    nt = l // tt
    n_chunks = nb * nt

    def body(u_ref, b_ref, c_ref, cst_ref, ds_ref, wg_ref, y_ref, h_ref, yp_ref, z_ref,
             buf0, buf1, ub0, ub1, carry_ref):
        s = pl.program_id(0)

        @pl.when(s == 0)
        def _():
            for r in (buf0, buf1, ub0, ub1, carry_ref):
                r[...] = jnp.zeros_like(r)

        def step(p_buf, p_u, q_buf):
            h = p_buf[...].astype(BF16)
            h_ref[...] = h
            ypre = (jnp.concatenate([_dot(_state_block(h, m), c_ref[m]) for m in range(N_BLK)], axis=1)
                    + ds_ref[...] * p_u[...])
            yp_ref[...] = ypre
            z = _dot(jax.nn.gelu(ypre).astype(BF16), wg_ref[...])
            z_ref[...] = z
            y_ref[...] = (z[:, :SSM_W] * jax.nn.sigmoid(z[:, SSM_W:])).astype(BF16)
            ub = u_ref[...]
            p_u[...] = ub.astype(F32)
            for m in range(N_BLK):
                _put_state_block(p_buf, m, _dot(ub[:, m * BLK_CH:(m + 1) * BLK_CH], b_ref[m]))
            _scan_tiles(q_buf, cst_ref, carry_ref, tt, False, lax.rem(s + nt - 1, nt) == 0)

        @pl.when(lax.rem(s, 2) == 0)
        def _():
            step(buf0, ub0, buf1)

        @pl.when(lax.rem(s, 2) == 1)
        def _():
            step(buf1, ub1, buf0)

    row_in = lambda s: (jnp.minimum(s, n_chunks - 1), 1)
    row_out = lambda s: (jnp.maximum(s - 2, 0), 0)
    const = lambda s: (0, 0)
    const3 = lambda s: (0, 0, 0)
    return pl.pallas_call(
        body, grid=(n_chunks + 2,),
        in_specs=[pl.BlockSpec((tt, SSM_W), row_in),
                  pl.BlockSpec((N_BLK, BLK_CH, 2 * BLK_ST), const3),
                  pl.BlockSpec((N_BLK, 2 * BLK_ST, BLK_CH), const3),
                  pl.BlockSpec((8 * SUBLANES, NST), const),
                  pl.BlockSpec((1, SSM_W), const),
                  pl.BlockSpec((SSM_W, 2 * SSM_W), const)],
        out_specs=[pl.BlockSpec((tt, SSM_W), row_out),
                   pl.BlockSpec((tt, 2 * NST), row_out),
                   pl.BlockSpec((tt, SSM_W), row_out),
                   pl.BlockSpec((tt, 2 * SSM_W), row_out)],
        out_shape=[jax.ShapeDtypeStruct((nb * l, SSM_W), BF16),
                   jax.ShapeDtypeStruct((nb * l, 2 * NST), BF16),
                   jax.ShapeDtypeStruct((nb * l, SSM_W), F32),
                   jax.ShapeDtypeStruct((nb * l, 2 * SSM_W), F32)],
        scratch_shapes=[pltpu.VMEM((tt, 2 * NST), F32), pltpu.VMEM((tt, 2 * NST), F32),
                        pltpu.VMEM((tt, SSM_W), F32), pltpu.VMEM((tt, SSM_W), F32),
                        pltpu.VMEM((SUBLANES, 2 * NST), F32)],
        compiler_params=_params("arbitrary"), name="s5_fwd")(proj, b_m, c_m, cst, d_skip, w_glu)


def s5_bwd(proj, d_ycat, h, ypre, z, b_m, c_m, cst_rev, d_skip, w_glu, nb, l):
    tt = min(256, l)
    nt = l // tt
    n_chunks = nb * nt

    def body(u_ref, dy_ref, h_ref, yp_ref, z_ref, b_ref, c_ref, cst_ref, ds_ref, wg_ref,
             du_ref, dwg_ref, dds_ref, dcc_ref, dbb_ref, dlam_ref,
             buf0, buf1, hb0, hb1, ub0, ub1, dyp0, dyp1, carry_ref, dc_ref, db_ref):
        s = pl.program_id(0)

        @pl.when(s == 0)
        def _():
            for r in (buf0, buf1, hb0, hb1, ub0, ub1, dyp0, dyp1, carry_ref, dc_ref, db_ref, dwg_ref, dds_ref, dlam_ref):
                r[...] = jnp.zeros_like(r)

        def step(p_buf, p_h, p_u, p_dyp, q_buf, q_h):
            g = p_buf[...].astype(BF16)
            ub_done = p_u[...].astype(BF16)
            du = []
            for m in range(N_BLK):
                g_m = _state_block(g, m)
                db_ref[m] += _dot_tn(g_m, ub_done[:, m * BLK_CH:(m + 1) * BLK_CH])
                du.append(_dot_nt(g_m, b_ref[m]))
            du_ref[...] = (jnp.concatenate(du, axis=1) + ds_ref[...] * p_dyp[...]).astype(BF16)
            u = u_ref[...].astype(F32)
            ypre = yp_ref[...]
            z = z_ref[...]
            z1 = z[:, :SSM_W]
            sg = jax.nn.sigmoid(z[:, SSM_W:])
            dy = dy_ref[...].astype(F32) * jnp.where(s < n_chunks, 1.0, 0.0)
            dz = jnp.concatenate([dy * sg, dy * z1 * sg * (1.0 - sg)], axis=1).astype(BF16)
            yg, gelu_vjp = jax.vjp(jax.nn.gelu, ypre)
            dwg_ref[...] += _dot_tn(yg.astype(BF16), dz)
            dypre = gelu_vjp(_dot_nt(dz, wg_ref[...]))[0]
            dds_ref[...] += jnp.sum(dypre * u, axis=0, keepdims=True)
            dyb = dypre.astype(BF16)
            hb = h_ref[...]
            p_h[...] = hb.astype(F32)
            p_u[...] = u
            p_dyp[...] = dypre
            for m in range(N_BLK):
                dy_m = dyb[:, m * BLK_CH:(m + 1) * BLK_CH]
                dc_ref[m] += _dot_tn(_state_block(hb, m), dy_m)
                _put_state_block(p_buf, m, _dot_nt(dy_m, c_ref[m]))
            _scan_tiles(q_buf, cst_ref, carry_ref, tt, True, lax.rem(s + nt - 1, nt) == 0, h_ref=q_h, acc_ref=dlam_ref)

        @pl.when(lax.rem(s, 2) == 0)
        def _():
            step(buf0, hb0, ub0, dyp0, buf1, hb1)

        @pl.when(lax.rem(s, 2) == 1)
        def _():
            step(buf1, hb1, ub1, dyp1, buf0, hb0)

        @pl.when(s == n_chunks + 1)
        def _():
            dlam_ref[...] = jnp.broadcast_to(jnp.sum(dlam_ref[...], axis=0, keepdims=True), dlam_ref.shape)
            for acc_ref, out_ref in ((dc_ref, dcc_ref), (db_ref, dbb_ref)):
                for m in range(N_BLK):
                    for ri in range(2):
                        for gl in range(BLK_GROUPS):
                            out_ref[ri, pl.ds((m * BLK_GROUPS + gl) * SSM_N, SSM_N), :] = acc_ref[
                                m, pl.ds(ri * BLK_ST + gl * SSM_N, SSM_N), pl.ds(gl * SSM_GC, SSM_GC)]

    def chunk_rows(o):
        return lax.div(o, nt) * nt + (nt - 1 - lax.rem(o, nt))

    def rrow(col):
        return lambda s: (chunk_rows(jnp.minimum(s, n_chunks - 1)), col)

    const = lambda s: (0, 0)
    const3 = lambda s: (0, 0, 0)
    state_buf = pltpu.VMEM((tt, 2 * NST), F32)
    chan_buf = pltpu.VMEM((tt, SSM_W), F32)
    return pl.pallas_call(
        body, grid=(n_chunks + 2,),
        in_specs=[pl.BlockSpec((tt, SSM_W), rrow(1)),
                  pl.BlockSpec((tt, SSM_W), rrow(1)),
                  pl.BlockSpec((tt, 2 * NST), rrow(0)),
                  pl.BlockSpec((tt, SSM_W), rrow(0)),
                  pl.BlockSpec((tt, 2 * SSM_W), rrow(0)),
                  pl.BlockSpec((N_BLK, BLK_CH, 2 * BLK_ST), const3),
                  pl.BlockSpec((N_BLK, 2 * BLK_ST, BLK_CH), const3),
                  pl.BlockSpec((8 * SUBLANES, NST), const),
                  pl.BlockSpec((1, SSM_W), const),
                  pl.BlockSpec((SSM_W, 2 * SSM_W), const)],
        out_specs=[pl.BlockSpec((tt, SSM_W), lambda s: (chunk_rows(jnp.maximum(s - 2, 0)), 0)),
                   pl.BlockSpec((SSM_W, 2 * SSM_W), const),
                   pl.BlockSpec((1, SSM_W), const),
                   pl.BlockSpec((2, NST, SSM_GC), const3),
                   pl.BlockSpec((2, NST, SSM_GC), const3),
                   pl.BlockSpec((SUBLANES, 2 * NST), const)],
        out_shape=[jax.ShapeDtypeStruct((nb * l, SSM_W), BF16),
                   jax.ShapeDtypeStruct((SSM_W, 2 * SSM_W), F32),
                   jax.ShapeDtypeStruct((1, SSM_W), F32),
                   jax.ShapeDtypeStruct((2, NST, SSM_GC), F32),
                   jax.ShapeDtypeStruct((2, NST, SSM_GC), F32),
                   jax.ShapeDtypeStruct((SUBLANES, 2 * NST), F32)],
        scratch_shapes=[state_buf, state_buf, state_buf, state_buf, chan_buf, chan_buf, chan_buf, chan_buf,
                        pltpu.VMEM((SUBLANES, 2 * NST), F32),
                        pltpu.VMEM((N_BLK, 2 * BLK_ST, BLK_CH), F32), pltpu.VMEM((N_BLK, 2 * BLK_ST, BLK_CH), F32)],
        compiler_params=_params("arbitrary"), name="s5_bwd")(
            proj, d_ycat, h, ypre, z, b_m, c_m, cst_rev, d_skip, w_glu)


def _head_mask(hh):
    lane = lax.broadcasted_iota(jnp.int32, (1, ATT_W), 1)
    return (lane >= hh * MEM_HD) & (lane < (hh + 1) * MEM_HD)


def _mem_kv(mem_ref, gm_ref, wkv_ref):
    m = mem_ref[0]
    mh = m * _rms_scale(m)
    mb = (mh * gm_ref[...]).astype(BF16)
    kv = _dot(mb, wkv_ref[...])
    return mh, mb, kv[:, :ATT_W].astype(BF16), kv[:, ATT_W:].astype(BF16)


def _stack_heads(dst_ref, a):
    for hh in range(MEM_HEADS):
        dst_ref[pl.ds(hh * N_MEM, N_MEM), :] = jnp.where(_head_mask(hh), a, jnp.zeros_like(a))


def _fold_heads(a):
    out = jnp.zeros((N_MEM, ATT_W), a.dtype)
    for hh in range(MEM_HEADS):
        out = out + jnp.where(_head_mask(hh), a[hh * N_MEM:(hh + 1) * N_MEM], 0.0)
    return out


def _softmax_heads(s):
    parts = []
    for hh in range(MEM_HEADS):
        sh = s[:, hh * N_MEM:(hh + 1) * N_MEM]
        e = jnp.exp(sh - jnp.max(sh, axis=-1, keepdims=True))
        parts.append(e / jnp.sum(e, axis=-1, keepdims=True))
    return jnp.concatenate(parts, axis=1)


def attn_fwd(proj, mem, g_mem, w_kv, nb, l):
    tq = min(ATTN_ROWS, l)
    nq = l // tq
    scale = MEM_HD ** -0.5

    def body(q_ref, mem_ref, gm_ref, wkv_ref, o_ref, k_s, v_s):
        @pl.when(pl.program_id(1) == 0)
        def _():
            _, _, k, v = _mem_kv(mem_ref, gm_ref, wkv_ref)
            _stack_heads(k_s, k)
            _stack_heads(v_s, v)

        p = _softmax_heads(_dot_nt(q_ref[...], k_s[...]) * scale)
        o_ref[...] = _dot(p.astype(BF16), v_s[...]).astype(BF16)

    const = lambda b, t: (0, 0)
    stacked = pltpu.VMEM((MEM_HEADS * N_MEM, ATT_W), BF16)
    return pl.pallas_call(
        body, grid=(nb, nq),
        in_specs=[pl.BlockSpec((tq, ATT_W), lambda b, t: (b * nq + t, 3)),
                  pl.BlockSpec((1, N_MEM, D_MODEL), lambda b, t: (b, 0, 0)),
                  pl.BlockSpec((1, D_MODEL), const),
                  pl.BlockSpec((D_MODEL, 2 * ATT_W), const)],
        out_specs=pl.BlockSpec((tq, ATT_W), lambda b, t: (b * nq + t, 0)),
        out_shape=jax.ShapeDtypeStruct((nb * l, ATT_W), BF16),
        scratch_shapes=[stacked, stacked],
        compiler_params=_params("arbitrary", "arbitrary"), name="attn_fwd")(proj, mem, g_mem, w_kv)


def attn_bwd(proj, d_ycat, mem, g_mem, w_kv, nb, l):
    tq = min(ATTN_ROWS, l)
    nq = l // tq
    scale = MEM_HD ** -0.5

    def body(q_ref, do_ref, mem_ref, gm_ref, wkv_ref, dq_ref, dwkv_ref, dgm_ref, k_s, v_s, mb_s, dk_s, dv_s):
        b = pl.program_id(0)
        t = pl.program_id(1)

        @pl.when((b == 0) & (t == 0))
        def _():
            dwkv_ref[...] = jnp.zeros_like(dwkv_ref)
            dgm_ref[...] = jnp.zeros_like(dgm_ref)

        @pl.when(t == 0)
        def _():
            _, mb, k, v = _mem_kv(mem_ref, gm_ref, wkv_ref)
            _stack_heads(k_s, k)
            _stack_heads(v_s, v)
            mb_s[...] = mb
            dk_s[...] = jnp.zeros_like(dk_s)
            dv_s[...] = jnp.zeros_like(dv_s)

        q = q_ref[...]
        do = do_ref[...]
        k = k_s[...]
        p = _softmax_heads(_dot_nt(q, k) * scale)
        dp = _dot_nt(do, v_s[...])
        ds = []
        for hh in range(MEM_HEADS):
            blk = slice(hh * N_MEM, (hh + 1) * N_MEM)
            ds.append(p[:, blk] * (dp[:, blk] - jnp.sum(dp[:, blk] * p[:, blk], axis=-1, keepdims=True)) * scale)
        ds = jnp.concatenate(ds, axis=1).astype(BF16)
        dq_ref[...] = _dot(ds, k).astype(BF16)
        dk_s[...] += _dot_tn(ds, q)
        dv_s[...] += _dot_tn(p.astype(BF16), do)

        @pl.when(t == nq - 1)
        def _():
            dkv = jnp.concatenate([_fold_heads(dk_s[...]), _fold_heads(dv_s[...])], axis=1).astype(BF16)
            dwkv_ref[...] += _dot_tn(mb_s[...], dkv)
            m = mem_ref[0]
            dgm_ref[...] += jnp.sum(_dot_nt(dkv, wkv_ref[...]) * (m * _rms_scale(m)), axis=0, keepdims=True)

    const = lambda b, t: (0, 0)
    return pl.pallas_call(
        body, grid=(nb, nq),
        in_specs=[pl.BlockSpec((tq, ATT_W), lambda b, t: (b * nq + t, 3)),
                  pl.BlockSpec((tq, ATT_W), lambda b, t: (b * nq + t, 3)),
                  pl.BlockSpec((1, N_MEM, D_MODEL), lambda b, t: (b, 0, 0)),
                  pl.BlockSpec((1, D_MODEL), const),
                  pl.BlockSpec((D_MODEL, 2 * ATT_W), const)],
        out_specs=[pl.BlockSpec((tq, ATT_W), lambda b, t: (b * nq + t, 0)),
                   pl.BlockSpec((D_MODEL, 2 * ATT_W), const),
                   pl.BlockSpec((1, D_MODEL), const)],
        out_shape=[jax.ShapeDtypeStruct((nb * l, ATT_W), BF16),
                   jax.ShapeDtypeStruct((D_MODEL, 2 * ATT_W), F32),
                   jax.ShapeDtypeStruct((1, D_MODEL), F32)],
        scratch_shapes=[pltpu.VMEM((MEM_HEADS * N_MEM, ATT_W), BF16), pltpu.VMEM((MEM_HEADS * N_MEM, ATT_W), BF16),
                        pltpu.VMEM((N_MEM, D_MODEL), BF16),
                        pltpu.VMEM((MEM_HEADS * N_MEM, ATT_W), F32), pltpu.VMEM((MEM_HEADS * N_MEM, ATT_W), F32)],
        compiler_params=_params("arbitrary", "arbitrary"), name="attn_bwd")(proj, d_ycat, mem, g_mem, w_kv)


def tail(y_pool, y_ssm, y_att, proj, x2, target, w_out, g_post):
    t = x2.shape[0]
    tm = min(ROW_TILE, t)

    def body(yp_ref, ys_ref, ya_ref, gate_ref, x_ref, tg_ref, w_ref, gp_ref,
             dz_ref, dyc_ref, dgate_ref, dw_ref, dgp_ref, loss_ref):
        @pl.when(pl.program_id(0) == 0)
        def _():
            dw_ref[...] = jnp.zeros_like(dw_ref)
            dgp_ref[...] = jnp.zeros_like(dgp_ref)
            loss_ref[...] = jnp.zeros_like(loss_ref)

        ycat = jnp.concatenate([yp_ref[...], ys_ref[...], ya_ref[...]], axis=1).astype(F32)
        gate = gate_ref[...].astype(F32)
        sg = jax.nn.sigmoid(gate)
        silu = gate * sg
        yb = (ycat * silu).astype(BF16)
        w = w_ref[...]
        out = _dot(yb, w)
        r2 = _rms_scale(out)
        oh = out * r2
        gp = gp_ref[...]
        err = (x_ref[...] + oh * gp) - tg_ref[...]
        loss_ref[...] += 0.5 * jnp.sum(jnp.mean(err * err, axis=-1, keepdims=True), axis=0, keepdims=True)
        dz = err * (1.0 / D_MODEL)
        dz_ref[...] = dz.astype(BF16)
        dgp_ref[...] += jnp.sum(dz * oh, axis=0, keepdims=True)
        dn = dz * gp
        dout = (r2 * (dn - oh * jnp.mean(dn * oh, axis=-1, keepdims=True))).astype(BF16)
        dw_ref[...] += _dot_tn(yb, dout)
        dy = _dot_nt(dout, w)
        dyc_ref[...] = (dy * silu).astype(BF16)
        dgate_ref[...] = (dy * ycat * (sg * (1.0 + gate * (1.0 - sg)))).astype(BF16)

    row = lambda i: (i, 0)
    const = lambda i: (0, 0)
    full = jax.ShapeDtypeStruct((t, D_MODEL), BF16)
    return pl.pallas_call(
        body, grid=(t // tm,),
        in_specs=[pl.BlockSpec((tm, POOL_W), row),
                  pl.BlockSpec((tm, SSM_W), row),
                  pl.BlockSpec((tm, ATT_W), row),
                  pl.BlockSpec((tm, D_MODEL), lambda i: (i, 1)),
                  pl.BlockSpec((tm, D_MODEL), row),
                  pl.BlockSpec((tm, D_MODEL), row),
                  pl.BlockSpec((D_MODEL, D_MODEL), const),
                  pl.BlockSpec((1, D_MODEL), const)],
        out_specs=[pl.BlockSpec((tm, D_MODEL), row),
                   pl.BlockSpec((tm, D_MODEL), row),
                   pl.BlockSpec((tm, D_MODEL), row),
                   pl.BlockSpec((D_MODEL, D_MODEL), const),
                   pl.BlockSpec((1, D_MODEL), const),
                   pl.BlockSpec((1, LANES), const)],
        out_shape=[full, full, full,
                   jax.ShapeDtypeStruct((D_MODEL, D_MODEL), F32),
                   jax.ShapeDtypeStruct((1, D_MODEL), F32),
                   jax.ShapeDtypeStruct((1, LANES), F32)],
        compiler_params=_params("arbitrary"), name="tail")(y_pool, y_ssm, y_att, proj, x2, target, w_out, g_post)


def in_proj_bwd(du_pool, du_ssm, dq, d_gate, x2, dz, g_pre, w_in):
    t = x2.shape[0]
    tm = min(ROW_TILE, t)
    shard = W_IN_SHARD

    def body(dup_ref, dus_ref, dq_ref, dg_ref, x_ref, dz_ref, g_ref, w_ref, gx_ref, dw_ref, dgp_ref):
        @pl.when(pl.program_id(0) == 0)
        def _():
            dw_ref[...] = jnp.zeros_like(dw_ref)
            dgp_ref[...] = jnp.zeros_like(dgp_ref)

        dproj = jnp.concatenate([dup_ref[...], dus_ref[...], dq_ref[...], dg_ref[...]], axis=1)
        x = x_ref[...]
        r1 = _rms_scale(x)
        xh = x * r1
        g = g_ref[...]
        hb = (xh * g).astype(BF16)
        dh = jnp.zeros((tm, D_MODEL), F32)
        for k in range(N_CHIPS):
            dp_k = dproj[:, k * shard:(k + 1) * shard]
            dw_ref[k] += _dot_tn(hb, dp_k)
            dh = dh + _dot_nt(dp_k, w_ref[k])
        dgp_ref[...] += jnp.sum(dh * xh, axis=0, keepdims=True)
        dn = dh * g
        gx_ref[...] = dz_ref[...].astype(F32) + r1 * (dn - xh * jnp.mean(dn * xh, axis=-1, keepdims=True))

    row = lambda i: (i, 0)
    const = lambda i: (0, 0)
    return pl.pallas_call(
        body, grid=(t // tm,),
        in_specs=[pl.BlockSpec((tm, POOL_W), row),
                  pl.BlockSpec((tm, SSM_W), row),
                  pl.BlockSpec((tm, ATT_W), row),
                  pl.BlockSpec((tm, D_MODEL), row),
                  pl.BlockSpec((tm, D_MODEL), row),
                  pl.BlockSpec((tm, D_MODEL), row),
                  pl.BlockSpec((1, D_MODEL), const),
                  pl.BlockSpec((N_CHIPS, D_MODEL, shard), lambda i: (0, 0, 0))],
        out_specs=[pl.BlockSpec((tm, D_MODEL), row),
                   pl.BlockSpec((N_CHIPS, D_MODEL, shard), lambda i: (0, 0, 0)),
                   pl.BlockSpec((1, D_MODEL), const)],
        out_shape=[jax.ShapeDtypeStruct((t, D_MODEL), F32),
                   jax.ShapeDtypeStruct((N_CHIPS, D_MODEL, shard), F32),
                   jax.ShapeDtypeStruct((1, D_MODEL), F32)],
        compiler_params=_params("arbitrary"), name="in_proj_bwd")(du_pool, du_ssm, dq, d_gate, x2, dz, g_pre, w_in)


def _block_diag(blocks):
    g, r, c = blocks.shape
    eye = jnp.eye(g, dtype=blocks.dtype)
    return jnp.einsum("grc,gh->grhc", blocks, eye).reshape(g * r, g * c)


def _block_diag_extract(mat, g, r, c):
    eye = jnp.eye(g, dtype=mat.dtype)
    return jnp.einsum("grhc,gh->grc", mat.reshape(g, r, g, c), eye)


def local_step(x, mem, target, g_pre, w_in_shard, w_pool, pool_scale, a_re, a_im, log_dt, b_re, b_im, c_re, c_im,
               d_skip, later_shards, g_mem, g_post):
    nb, l, _ = x.shape
    x2 = x.reshape(nb * l, D_MODEL)
    tg2 = target.reshape(nb * l, D_MODEL)

    rowv = lambda a: a.reshape(1, NST)
    lanes_last = lambda b: b.transpose(2, 0, 1).reshape(SSM_GC, NST)
    ldt_row = rowv(jnp.broadcast_to(log_dt.reshape(SSM_NG, 1), (SSM_NG, SSM_N)))
    b_re_t = lanes_last(b_re)
    b_im_t = lanes_last(b_im)
    cst_f, cst_b = s5_scan_consts(rowv(a_re), rowv(a_im), ldt_row)
    bbr, bbi = s5_bbar(rowv(a_re), rowv(a_im), ldt_row, b_re_t, b_im_t)
    eye = jnp.eye(BLK_GROUPS, dtype=F32)
    blk_in = lambda bb: jnp.einsum("cmgn,gh->mhcgn", bb.reshape(SSM_GC, N_BLK, BLK_GROUPS, SSM_N), eye).reshape(
        N_BLK, BLK_CH, BLK_ST)
    blk_out = lambda cc: jnp.einsum("mgcn,gh->mgnhc", cc.reshape(N_BLK, BLK_GROUPS, SSM_GC, SSM_N), eye).reshape(
        N_BLK, BLK_ST, BLK_CH)
    b_m = jnp.concatenate([blk_in(bbr), blk_in(bbi)], axis=2).astype(BF16)
    c_m = jnp.concatenate([blk_out(c_re), -blk_out(c_im)], axis=1).astype(BF16)
    w_pool_blk = _block_diag(w_pool.reshape(4, POOL_GW, POOL_GW)).astype(BF16)

    proj, w_in, (g_glu, g_kv, g_out) = in_proj(x2, g_pre, w_in_shard, later_shards)
    w_glu = g_glu.transpose(2, 0, 1).reshape(SSM_W, 2 * SSM_W)
    w_kv = g_kv.reshape(D_MODEL, 2 * ATT_W)
    w_out = g_out.reshape(D_MODEL, D_MODEL)
    y_pool = pool_fwd(proj, w_pool_blk, pool_scale, nb, l)
    y_ssm, h, ypre, z = s5_fwd(proj, b_m, c_m, cst_f, d_skip, w_glu, nb, l)
    y_att = attn_fwd(proj, mem, g_mem, w_kv, nb, l)
    dz, d_ycat, d_gate, d_w_out, d_g_post, loss = tail(y_pool, y_ssm, y_att, proj, x2, tg2, w_out, g_post)
    du_pool, d_wp_blk, d_pool_scale = pool_bwd(proj, d_ycat, w_pool_blk, pool_scale, nb, l)
    du_ssm, d_w_glu, d_d_skip, d_cc, d_bb, d_lam = s5_bwd(
        proj, d_ycat, h, ypre, z, b_m, c_m, cst_b, d_skip, w_glu, nb, l)
    dq, d_w_kv, d_g_mem = attn_bwd(proj, d_ycat, mem, g_mem, w_kv, nb, l)
    grad_x, d_w_in, d_g_pre = in_proj_bwd(du_pool, du_ssm, dq, d_gate, x2, dz, g_pre, w_in)

    d_c = d_cc[0].reshape(SSM_NG, SSM_N, SSM_GC)
    d_ci = d_cc[1].reshape(SSM_NG, SSM_N, SSM_GC)
    d_lam_row = d_lam[0]
    d_ar, d_ai, d_ld, d_br, d_bi = s5_param_bwd(
        rowv(a_re), rowv(a_im), ldt_row, b_re_t, b_im_t,
        d_lam_row[:NST].reshape(1, NST), d_lam_row[NST:].reshape(1, NST), d_bb[0].T, d_bb[1].T)
    from_lanes_last = lambda b: b.reshape(SSM_GC, SSM_NG, SSM_N).transpose(1, 0, 2).reshape(1, SSM_NG, SSM_GC, SSM_N)

    grads = {
        "g_pre": d_g_pre,
        "w_in": d_w_in,
        "w_pool": _block_diag_extract(d_wp_blk, 4, POOL_GW, POOL_GW).reshape(1, 4, POOL_GW, POOL_GW),
        "pool_scale": d_pool_scale,
        "a_re": d_ar.reshape(1, SSM_NG, SSM_N),
        "a_im": d_ai.reshape(1, SSM_NG, SSM_N),
        "log_dt": d_ld[:, :SSM_NG],
        "b_re": from_lanes_last(d_br),
        "b_im": from_lanes_last(d_bi),
        "c_re": d_c.transpose(0, 2, 1).reshape(1, SSM_NG, SSM_GC, SSM_N),
        "c_im": (-d_ci).transpose(0, 2, 1).reshape(1, SSM_NG, SSM_GC, SSM_N),
        "d_skip": d_d_skip,
        "w_glu": d_w_glu,
        "g_mem": d_g_mem,
        "w_kv": d_w_kv,
        "w_out": d_w_out,
        "g_post": d_g_post,
    }
    return loss, grad_x.reshape(nb, l, D_MODEL), grads


VMEM_SPEC = pl.BlockSpec(memory_space=pltpu.VMEM)
ANY_SPEC = pl.BlockSpec(memory_space=pl.ANY)


def _place():
    return lax.axis_index("x"), lax.axis_index("y"), lax.axis_index("c")


def _other_chips(x, y):
    return [(1 - x, y), (x, 1 - y), (1 - x, 1 - y)]


SEMS_PER_ITEM = 5


def reduce_all(g4s, packed):
    n = len(g4s)
    dims = [g.shape[1:] for g in g4s]
    pr = packed.shape[0]

    def body(*refs):
        g_refs, p_ref = refs[:n], refs[n]
        outs, op = refs[n + 1:2 * n + 1], refs[2 * n + 1]
        scr = refs[2 * n + 2:]
        mine, theirs, sendb, recvb = scr[0:n], scr[n:2 * n], scr[2 * n:3 * n], scr[3 * n:4 * n]
        p_theirs, gat, lsems, send_sems, recv_sems = scr[4 * n:]
        x, y, c = _place()
        k = 2 * x + y
        chips = _other_chips(x, y)
        sib = (x, y, 1 - c)

        def halves(rows):
            hr = rows // 2
            return (pl.ds(pl.multiple_of(c * hr, SUBLANES), hr), pl.ds(pl.multiple_of((1 - c) * hr, SUBLANES), hr))

        def remote(src, dst, sem, to):
            return pltpu.make_async_remote_copy(
                src_ref=src, dst_ref=dst, send_sem=send_sems.at[sem], recv_sem=recv_sems.at[sem],
                device_id=to, device_id_type=MESH)

        loads, started = [], []
        for i in range(n):
            my_rows, sib_rows = halves(dims[i][0])
            ld = pltpu.make_async_copy(g_refs[i].at[:, my_rows, :], mine[i], lsems.at[i])
            ld.start()
            loads.append(ld)
            sw = remote(g_refs[i].at[:, sib_rows, :], theirs[i], SEMS_PER_ITEM * i, sib)
            sw.start()
            started.append(sw)
        p_my, p_sib = halves(pr)
        p_swap = remote(p_ref.at[p_sib, :], p_theirs, SEMS_PER_ITEM * n, sib)
        p_swap.start()
        started.append(p_swap)

        p_swap.wait_recv()
        gat[k] = p_ref[p_my, :] + p_theirs[...]
        for j, (px, py) in enumerate(chips):
            cp = remote(gat.at[k], gat.at[k], SEMS_PER_ITEM * n + 1 + j, (px, py, c))
            cp.start()
            started.append(cp)
        ici = []
        for i in range(n):
            loads[i].wait()
            started[i].wait_recv()
            for j, (px, py) in enumerate(chips):
                kp = 2 * px + py
                sendb[i][j] = (mine[i][kp] + theirs[i][kp]).astype(BF16)
                cp = remote(sendb[i].at[j], recvb[i].at[j], SEMS_PER_ITEM * i + 1 + j, (px, py, c))
                cp.start()
                ici.append(cp)

        for j, (px, py) in enumerate(chips):
            kp = 2 * px + py
            remote(gat.at[kp], gat.at[kp], SEMS_PER_ITEM * n + 1 + j, (px, py, c)).wait_recv()
        op[p_my, :] = ((gat[0] + gat[1]) + gat[2]) + gat[3]
        last = remote(op.at[p_my, :], op.at[p_my, :], SEMS_PER_ITEM * n + 4, sib)
        last.start()
        started.append(last)
        for i in range(n):
            my_rows, _ = halves(dims[i][0])
            half = mine[i][k] + theirs[i][k]
            for j in range(3):
                ici[3 * i + j].wait_recv()
                half = half + recvb[i][j].astype(F32)
            outs[i][my_rows, :] = half
            last = remote(outs[i].at[my_rows, :], outs[i].at[my_rows, :], SEMS_PER_ITEM * i + 4, sib)
            last.start()
            started.append(last)
        remote(op.at[p_sib, :], op.at[p_sib, :], SEMS_PER_ITEM * n + 4, sib).wait_recv()
        for i in range(n):
            _, sib_rows = halves(dims[i][0])
            remote(outs[i].at[sib_rows, :], outs[i].at[sib_rows, :], SEMS_PER_ITEM * i + 4, sib).wait_recv()
        for cp in started + ici:
            cp.wait_send()

    n_sems = SEMS_PER_ITEM * (n + 1)
    scratch = ([pltpu.VMEM((N_CHIPS, r // 2, cd), F32) for r, cd in dims] * 2
               + [pltpu.VMEM((3, r // 2, cd), BF16) for r, cd in dims] * 2
               + [pltpu.VMEM((pr // 2, LANES), F32), pltpu.VMEM((N_CHIPS, pr // 2, LANES), F32),
                  pltpu.SemaphoreType.DMA((n,)), pltpu.SemaphoreType.DMA((n_sems,)), pltpu.SemaphoreType.DMA((n_sems,))])
    res = pl.pallas_call(
        body,
        out_shape=[jax.ShapeDtypeStruct(d, F32) for d in dims] + [jax.ShapeDtypeStruct(packed.shape, F32)],
        in_specs=[ANY_SPEC] * n + [VMEM_SPEC], out_specs=[VMEM_SPEC] * (n + 1),
        scratch_shapes=scratch,
        compiler_params=pltpu.CompilerParams(vmem_limit_bytes=VMEM_LIMIT),
        name="reduce_all")(*g4s, packed)
    return res[:n], res[n]


def adamw_all(ws, gs, ms, vs):
    n = len(ws)

    def body(*refs):
        w, g, m, v = refs[:n], refs[n:2 * n], refs[2 * n:3 * n], refs[3 * n:4 * n]
        od, om, ov = refs[4 * n:5 * n], refs[5 * n:6 * n], refs[6 * n:]
        for i in range(n):
            od[i][...], om[i][...], ov[i][...] = _adamw(w[i][...], g[i][...], m[i][...], v[i][...])

    shapes = [jax.ShapeDtypeStruct(a.shape, F32) for a in ws]
    res = pl.pallas_call(
        body, out_shape=shapes * 3, in_specs=[VMEM_SPEC] * (4 * n), out_specs=[VMEM_SPEC] * (3 * n),
        compiler_params=pltpu.CompilerParams(vmem_limit_bytes=VMEM_LIMIT),
        name="adamw_all")(*ws, *gs, *ms, *vs)
    return res[:n], res[n:2 * n], res[2 * n:]


WEIGHTS = ("g_pre", "w_in", "w_pool", "pool_scale", "a_re", "a_im", "log_dt", "b_re", "b_im", "c_re", "c_im", "d_skip",
           "w_glu", "g_mem", "w_kv", "w_out", "g_post")
SHARDED = ("w_in", "w_glu", "w_kv", "w_out")
SWAPPED = ("b_re", "b_im", "w_glu")
REPLICATED = tuple(n for n in WEIGHTS if n not in SHARDED)
PACK_TILE = SUBLANES * LANES


def _pack(arrays):
    parts = []
    for a in arrays:
        flat = a.reshape(-1)
        parts.append(jnp.pad(flat, (0, -flat.shape[0] % PACK_TILE)).reshape(-1, LANES))
    rows = sum(p.shape[0] for p in parts)
    if rows % (2 * SUBLANES):
        parts.append(jnp.zeros((SUBLANES, LANES), F32))
    return jnp.concatenate(parts, axis=0)


def _unpack(packed, shapes):
    out, row = [], 0
    for shp in shapes:
        size = math.prod(shp)
        rows = -(-size // PACK_TILE) * SUBLANES
        out.append(packed[row:row + rows].reshape(-1)[:size].reshape(shp))
        row += rows
    return out


def kernel(x, mem, g_pre, w_in, w_pool, pool_scale, a_re, a_im, log_dt, b_re, b_im, c_re, c_im, d_skip, w_glu, g_mem, w_kv, w_out, g_post, loss_target, m_g_pre, m_w_in, m_w_pool, m_pool_scale, m_a_re, m_a_im, m_log_dt, m_b_re, m_b_im, m_c_re, m_c_im, m_d_skip, m_w_glu, m_g_mem, m_w_kv, m_w_out, m_g_post, v_g_pre, v_w_in, v_w_pool, v_pool_scale, v_a_re, v_a_im, v_log_dt, v_b_re, v_b_im, v_c_re, v_c_im, v_d_skip, v_w_glu, v_g_mem, v_w_kv, v_w_out, v_g_post):
    given = dict(locals())
    wts = {n: given[n] for n in WEIGHTS}
    mom = {n: given["m_" + n] for n in WEIGHTS}
    var = {n: given["v_" + n] for n in WEIGHTS}

    swap = lambda a: jnp.swapaxes(a, -1, -2)
    view = lambda n, a: swap(a) if n in SWAPPED else a

    loss_part, grad_x, grads = local_step(
        x, mem, loss_target, g_pre, w_in[0], w_pool[0], pool_scale, a_re[0], a_im[0], log_dt[0], b_re[0], b_im[0],
        c_re[0], c_im[0], d_skip, [swap(w_glu)[0], w_kv[0], w_out[0]], g_mem, g_post)

    glu_cols = 2 * SSM_W // N_CHIPS
    partial4 = [
        grads["w_in"],
        grads["w_glu"].reshape(SSM_W, N_CHIPS, glu_cols).transpose(1, 2, 0),
        grads["w_kv"].reshape(N_CHIPS, D_MODEL // N_CHIPS, 2 * ATT_W),
        grads["w_out"].reshape(N_CHIPS, D_MODEL // N_CHIPS, D_MODEL),
    ]
    loss_tile = jnp.broadcast_to(loss_part, (SUBLANES, LANES))
    sharded_g, packed_g = reduce_all(partial4, _pack([grads[n] for n in REPLICATED] + [loss_tile]))
    small_g = _unpack(packed_g, [view(n, wts[n]).shape for n in REPLICATED] + [(SUBLANES, LANES)])
    loss = small_g[-1][0, 0]
    grad = dict(zip(SHARDED, [g[None] for g in sharded_g]))
    grad.update(zip(REPLICATED, small_g[:-1]))

    deltas, new_ms, new_vs = adamw_all([view(n, wts[n]) for n in WEIGHTS], [grad[n] for n in WEIGHTS],
                                       [view(n, mom[n]) for n in WEIGHTS], [view(n, var[n]) for n in WEIGHTS])
    back = lambda arrs: [view(n, a) for n, a in zip(WEIGHTS, arrs)]
    return (loss, grad_x, *back([grad[n] for n in WEIGHTS]), *back(deltas), *back(new_ms), *back(new_vs))
```

```python
import math

import jax
import jax.numpy as jnp
from jax import lax
from jax.experimental import pallas as pl
from jax.experimental.pallas import tpu as pltpu

F32 = jnp.float32
BF16 = jnp.bfloat16

D_MODEL = 1024
POOL_W = 384
SSM_W = 384
ATT_W = 256
POOL_GW = 96
POOL_PAD = 16
SSM_NG = 24
SSM_N = 64
SSM_GC = 16
N_CHIPS = 4
W_IN_SHARD = 2 * D_MODEL // N_CHIPS
NST = SSM_NG * SSM_N
BLK_CH = 128
BLK_GROUPS = BLK_CH // SSM_GC
BLK_ST = BLK_GROUPS * SSM_N
N_BLK = SSM_W // BLK_CH
N_MEM = 256
MEM_HEADS = 4
MEM_HD = 64
EPS = 1e-6

ADAM_LR = 0.001
ADAM_B1 = 0.9
ADAM_B2 = 0.999
ADAM_EPS = 1e-08
ADAM_WD = 0.01
ADAM_STEP = 10

SUBLANES = 8
LANES = 128
V7X_VMEM_BYTES = 64 * 2**20
VMEM_LIMIT = V7X_VMEM_BYTES - 8 * 2**20
SCAN_COLS = 512
ROW_TILE = 512
IN_PROJ_ROWS = 2048
ATTN_ROWS = 2048
S5_CHUNK = 256
ADAMW_STEPS = 4
MESH = pl.DeviceIdType.MESH

NT = (((1,), (1,)), ((), ()))
TN = (((0,), (0,)), ((), ()))


def _params(*sem):
    return pltpu.CompilerParams(dimension_semantics=sem, vmem_limit_bytes=VMEM_LIMIT)


def _dot(a, b):
    return jnp.dot(a, b, preferred_element_type=F32)


def _dot_nt(a, b):
    return lax.dot_general(a, b, NT, preferred_element_type=F32)


def _dot_tn(a, b):
    return lax.dot_general(a, b, TN, preferred_element_type=F32)


def _rms_scale(v):
    return lax.rsqrt(jnp.mean(v * v, axis=-1, keepdims=True) + EPS)


def _adamw(w, g, m, v):
    m = ADAM_B1 * m + (1.0 - ADAM_B1) * g
    v = ADAM_B2 * v + (1.0 - ADAM_B2) * (g * g)
    m_hat = m / (1.0 - ADAM_B1 ** ADAM_STEP)
    v_hat = v / (1.0 - ADAM_B2 ** ADAM_STEP)
    delta = -ADAM_LR * (m_hat / (jnp.sqrt(v_hat) + ADAM_EPS) + ADAM_WD * w)
    return delta, m, v


def _phase_shard(j):
    cx, cy = lax.axis_index("x"), lax.axis_index("y")
    others = [2 * px + py for px, py in _other_chips(cx, cy)]
    own = 2 * cx + cy
    return jnp.where(j == 0, own, jnp.where(j == 1, others[0], jnp.where(j == 2, others[1], others[2])))


def in_proj(x2, g_pre, w_in_shard, later_shards):
    t = x2.shape[0]
    tm = min(IN_PROJ_ROWS, t)
    n_tiles = t // tm
    n = len(later_shards)

    def body(x_ref, g_ref, ws_ref, *rest):
        shard_refs = rest[:n]
        o_ref, w_out_ref, gathered = rest[n], rest[n + 1], rest[n + 2:2 * n + 2]
        w_buf, h_buf = rest[2 * n + 2], rest[2 * n + 3]
        cast_bufs = rest[2 * n + 4:3 * n + 4]
        local_sems, send_sems, recv_sems = rest[3 * n + 4:]
        j = pl.program_id(0)
        i = pl.program_id(1)
        cx, cy, cc = _place()
        k = 2 * cx + cy
        chips = _other_chips(cx, cy)

        half = w_in_shard.shape[0] // 2

        def w_copy(p, block, core_half, forward):
            px, py = chips[p]
            rows = w_buf.at[block, pl.ds(pl.multiple_of(core_half * half, 2 * SUBLANES), half)]
            sem = 3 + p if forward else p
            return pltpu.make_async_remote_copy(
                src_ref=rows, dst_ref=rows, send_sem=send_sems.at[sem], recv_sem=recv_sems.at[sem],
                device_id=(cx, cy, 1 - cc) if forward else (px, py, cc), device_id_type=MESH)

        def own(m):
            return pltpu.make_async_copy(cast_bufs[m], gathered[m].at[k], local_sems.at[m])

        def later_copy(m, p, block, core_half, forward):
            px, py = chips[p]
            hr = later_shards[m].shape[0] // 2
            rows = pl.ds(pl.multiple_of(core_half * hr, 2 * SUBLANES), hr)
            dst = gathered[m].at[block, rows]
            sem = 6 + 6 * m + (3 + p if forward else p)
            return pltpu.make_async_remote_copy(
                src_ref=dst if forward else cast_bufs[m].at[rows], dst_ref=dst, send_sem=send_sems.at[sem],
                recv_sem=recv_sems.at[sem], device_id=(cx, cy, 1 - cc) if forward else (px, py, cc),
                device_id_type=MESH)

        @pl.when((j == 0) & (i == 0))
        def _():
            w_buf[k] = ws_ref[...].astype(BF16)
            for p in range(3):
                w_copy(p, k, cc, False).start()
            for m in range(n):
                cast_bufs[m][...] = shard_refs[m][...].astype(BF16)
                own(m).start()
                for p in range(3):
                    later_copy(m, p, k, cc, False).start()

        rows = pl.ds(pl.multiple_of(i * tm, tm), tm)

        @pl.when(j == 0)
        def _():
            x = x_ref[...]
            h_buf[rows, :] = (x * _rms_scale(x) * g_ref[...]).astype(BF16)

        for p in range(3):
            @pl.when((j == p) & (i == n_tiles - 1))
            def _(p=p):
                px, py = chips[p]
                w_copy(p, 2 * px + py, cc, False).wait_recv()
                w_copy(p, 2 * px + py, cc, True).start()

            @pl.when((j == p + 1) & (i == 0))
            def _(p=p):
                px, py = chips[p]
                w_copy(p, 2 * px + py, 1 - cc, True).wait_recv()

        o_ref[...] = _dot(h_buf[rows, :], w_buf[_phase_shard(j)]).astype(BF16)

        out_cp = pltpu.make_async_copy(w_buf, w_out_ref, local_sems.at[n])

        @pl.when((j == N_CHIPS - 1) & (i == 0))
        def _():
            out_cp.start()
            for m in range(n):
                for p in range(3):
                    px, py = chips[p]
                    later_copy(m, p, 2 * px + py, cc, False).wait_recv()
                    later_copy(m, p, 2 * px + py, cc, True).start()

        @pl.when((j == N_CHIPS - 1) & (i == n_tiles - 1))
        def _():
            for m in range(n):
                own(m).wait()
            for m in range(n):
                for p in range(3):
                    px, py = chips[p]
                    later_copy(m, p, 2 * px + py, 1 - cc, True).wait_recv()
            for p in range(3):
                px, py = chips[p]
                w_copy(p, k, cc, False).wait_send()
                w_copy(p, 2 * px + py, cc, True).wait_send()
            for m in range(n):
                for p in range(3):
                    px, py = chips[p]
                    later_copy(m, p, k, cc, False).wait_send()
                    later_copy(m, p, 2 * px + py, cc, True).wait_send()
            out_cp.wait()

    whole = lambda a: pl.BlockSpec(a.shape, lambda j, i: (0,) * a.ndim)
    res = pl.pallas_call(
        body, grid=(N_CHIPS, n_tiles),
        in_specs=[pl.BlockSpec((tm, D_MODEL), lambda j, i: (jnp.where(j == 0, i, n_tiles - 1), 0)),
                  pl.BlockSpec((1, D_MODEL), lambda j, i: (0, 0)),
                  whole(w_in_shard)] + [whole(s) for s in later_shards],
        out_specs=[pl.BlockSpec((tm, W_IN_SHARD), lambda j, i: (i, _phase_shard(j)))] + [ANY_SPEC] * (n + 1),
        out_shape=([jax.ShapeDtypeStruct((t, 2 * D_MODEL), BF16),
                    jax.ShapeDtypeStruct((N_CHIPS,) + w_in_shard.shape, BF16)]
                   + [jax.ShapeDtypeStruct((N_CHIPS,) + s.shape, BF16) for s in later_shards]),
        scratch_shapes=([pltpu.VMEM((N_CHIPS,) + w_in_shard.shape, BF16), pltpu.VMEM((t, D_MODEL), BF16)]
                        + [pltpu.VMEM(s.shape, BF16) for s in later_shards]
                        + [pltpu.SemaphoreType.DMA((n + 1,)), pltpu.SemaphoreType.DMA((6 * n + 6,)),
                           pltpu.SemaphoreType.DMA((6 * n + 6,))]),
        compiler_params=_params("arbitrary", "arbitrary"), name="in_proj")(x2, g_pre, w_in_shard, *later_shards)
    return res[0], res[1], res[2:]


def _pool_lane_window(shape):
    ch = lax.broadcasted_iota(jnp.int32, shape, 1)
    return jnp.where(ch < POOL_GW, 2.0, jnp.where(ch < 2 * POOL_GW, 4.0, jnp.where(ch < 3 * POOL_GW, 8.0, 16.0)))


def _pool_select(win, s2, s4, s8, s16):
    return jnp.where(win == 2.0, s2, jnp.where(win == 4.0, s4, jnp.where(win == 8.0, s8, s16)))


def _pool_div_count(v, win):
    pos = (lax.broadcasted_iota(jnp.int32, (POOL_PAD, POOL_W), 0) + 1).astype(F32)
    head = v[:POOL_PAD] / jnp.minimum(pos, win)
    return jnp.concatenate([head, v[POOL_PAD:] * (1.0 / win)], axis=0)


def _pool_diff(u, pad_ref, l):
    lo = POOL_PAD
    pad_ref[pl.ds(lo, l), :] = u
    s2 = u + pad_ref[pl.ds(lo - 1, l), :]
    pad_ref[pl.ds(lo, l), :] = s2
    s4 = s2 + pad_ref[pl.ds(lo - 2, l), :]
    pad_ref[pl.ds(lo, l), :] = s4
    s8 = s4 + pad_ref[pl.ds(lo - 4, l), :]
    pad_ref[pl.ds(lo, l), :] = s8
    s16 = s8 + pad_ref[pl.ds(lo - 8, l), :]
    win = _pool_lane_window((1, POOL_W))
    return _pool_div_count(_pool_select(win, s2, s4, s8, s16), win) - u, win


def pool_fwd(proj, w_blk, pool_scale, nb, l):
    def body(u_ref, w_ref, ps_ref, y_ref, pad_ref):
        pad_ref[pl.ds(0, POOL_PAD), :] = jnp.zeros((POOL_PAD, POOL_W), F32)
        d, _ = _pool_diff(u_ref[...].astype(F32), pad_ref, l)
        y_ref[...] = (_dot(d.astype(BF16), w_ref[...]) * ps_ref[...]).astype(BF16)

    return pl.pallas_call(
        body, grid=(nb,),
        in_specs=[pl.BlockSpec((l, POOL_W), lambda b: (b, 0)),
                  pl.BlockSpec((POOL_W, POOL_W), lambda b: (0, 0)),
                  pl.BlockSpec((1, POOL_W), lambda b: (0, 0))],
        out_specs=pl.BlockSpec((l, POOL_W), lambda b: (b, 0)),
        out_shape=jax.ShapeDtypeStruct((nb * l, POOL_W), BF16),
        scratch_shapes=[pltpu.VMEM((POOL_PAD + l, POOL_W), F32)],
        compiler_params=_params("arbitrary"), name="pool_fwd")(proj, w_blk, pool_scale)


def pool_bwd(proj, d_ycat, w_blk, pool_scale, nb, l):
    def body(u_ref, dy_ref, w_ref, ps_ref, du_ref, dw_ref, dps_ref, pad_ref, padb_ref):
        b = pl.program_id(0)

        @pl.when(b == 0)
        def _():
            dw_ref[...] = jnp.zeros_like(dw_ref)
            dps_ref[...] = jnp.zeros_like(dps_ref)

        pad_ref[pl.ds(0, POOL_PAD), :] = jnp.zeros((POOL_PAD, POOL_W), F32)
        padb_ref[pl.ds(l, POOL_PAD), :] = jnp.zeros((POOL_PAD, POOL_W), F32)
        d, win = _pool_diff(u_ref[...].astype(F32), pad_ref, l)
        db = d.astype(BF16)
        w = w_ref[...]
        dy = dy_ref[...].astype(F32)
        dps_ref[...] += jnp.sum(dy * _dot(db, w), axis=0, keepdims=True)
        dyo = (dy * ps_ref[...]).astype(BF16)
        dw_ref[...] += _dot_tn(db, dyo)
        dd = _dot_nt(dyo, w)
        e = _pool_div_count(dd, win)
        padb_ref[pl.ds(0, l), :] = e
        f2 = e + padb_ref[pl.ds(1, l), :]
        padb_ref[pl.ds(0, l), :] = f2
        f4 = f2 + padb_ref[pl.ds(2, l), :]
        padb_ref[pl.ds(0, l), :] = f4
        f8 = f4 + padb_ref[pl.ds(4, l), :]
        padb_ref[pl.ds(0, l), :] = f8
        f16 = f8 + padb_ref[pl.ds(8, l), :]
        du_ref[...] = (_pool_select(win, f2, f4, f8, f16) - dd).astype(BF16)

    return pl.pallas_call(
        body, grid=(nb,),
        in_specs=[pl.BlockSpec((l, POOL_W), lambda b: (b, 0)),
                  pl.BlockSpec((l, POOL_W), lambda b: (b, 0)),
                  pl.BlockSpec((POOL_W, POOL_W), lambda b: (0, 0)),
                  pl.BlockSpec((1, POOL_W), lambda b: (0, 0))],
        out_specs=[pl.BlockSpec((l, POOL_W), lambda b: (b, 0)),
                   pl.BlockSpec((POOL_W, POOL_W), lambda b: (0, 0)),
                   pl.BlockSpec((1, POOL_W), lambda b: (0, 0))],
        out_shape=[jax.ShapeDtypeStruct((nb * l, POOL_W), BF16),
                   jax.ShapeDtypeStruct((POOL_W, POOL_W), F32),
                   jax.ShapeDtypeStruct((1, POOL_W), F32)],
        scratch_shapes=[pltpu.VMEM((POOL_PAD + l, POOL_W), F32), pltpu.VMEM((POOL_PAD + l, POOL_W), F32)],
        compiler_params=_params("arbitrary"), name="pool_bwd")(proj, d_ycat, w_blk, pool_scale)


def _discretise(a_re, a_im, ldt, b_re, b_im):
    dt = jnp.exp(ldt)
    mag = jnp.exp(a_re * dt)
    th = a_im * dt
    lr = mag * jnp.cos(th)
    li = mag * jnp.sin(th)
    nr = lr - 1.0
    den = a_re * a_re + a_im * a_im
    fr = (nr * a_re + li * a_im) / den
    fi = (li * a_re - nr * a_im) / den
    return lr, li, fr * b_re - fi * b_im, fr * b_im + fi * b_re


def _cmul(ar, ai, br, bi):
    return ar * br - ai * bi, ar * bi + ai * br


def s5_scan_consts(a_re_row, a_im_row, ldt_row):
    def body(ar_ref, ai_ref, ld_ref, f_ref, b_ref):
        dt = jnp.exp(ld_ref[...])
        mag = jnp.exp(ar_ref[...] * dt)
        th = ai_ref[...] * dt
        lr = mag * jnp.cos(th)
        li = mag * jnp.sin(th)
        r = lax.broadcasted_iota(jnp.int32, (SUBLANES, NST), 0)
        for out_ref, sign, rev in ((f_ref, 1.0, False), (b_ref, -1.0, True)):
            p = [(lr, sign * li)]
            for _ in range(SUBLANES - 1):
                p.append(_cmul(p[-1][0], p[-1][1], lr, sign * li))
            for idx, s in enumerate((1, 2, 4)):
                inside = (r < SUBLANES - s) if rev else (r >= s)
                out_ref[pl.ds(2 * idx * SUBLANES, SUBLANES), :] = jnp.where(inside, p[s - 1][0], 0.0)
                out_ref[pl.ds((2 * idx + 1) * SUBLANES, SUBLANES), :] = jnp.where(inside, p[s - 1][1], 0.0)
            pr = jnp.zeros((SUBLANES, NST), F32)
            pi = jnp.zeros((SUBLANES, NST), F32)
            for row in range(SUBLANES):
                power = (SUBLANES - row) if rev else (row + 1)
                pr = jnp.where(r == row, p[power - 1][0], pr)
                pi = jnp.where(r == row, p[power - 1][1], pi)
            out_ref[pl.ds(6 * SUBLANES, SUBLANES), :] = pr
            out_ref[pl.ds(7 * SUBLANES, SUBLANES), :] = pi

    shape = jax.ShapeDtypeStruct((8 * SUBLANES, NST), F32)
    return pl.pallas_call(body, out_shape=[shape, shape], name="s5_scan_consts")(a_re_row, a_im_row, ldt_row)


def s5_bbar(a_re_row, a_im_row, ldt_row, b_re_t, b_im_t):
    def body(ar, ai, ld, br, bi, o_re, o_im):
        _, _, bbr, bbi = _discretise(ar[...], ai[...], ld[...], br[...], bi[...])
        o_re[...] = bbr
        o_im[...] = bbi

    shape = jax.ShapeDtypeStruct((SSM_GC, NST), F32)
    return pl.pallas_call(body, out_shape=[shape, shape], name="s5_bbar")(a_re_row, a_im_row, ldt_row, b_re_t, b_im_t)


def s5_param_bwd(a_re_row, a_im_row, ldt_row, b_re_t, b_im_t, d_lr, d_li, d_bbr, d_bbi):
    def body(ar, ai, ld, br, bi, g_lr, g_li, g_br, g_bi, o_ar, o_ai, o_ld, o_br, o_bi):
        _, vjp = jax.vjp(_discretise, ar[...], ai[...], ld[...], br[...], bi[...])
        d_ar, d_ai, d_ld, d_br, d_bi = vjp((g_lr[...], g_li[...], g_br[...], g_bi[...]))
        o_ar[...] = d_ar
        o_ai[...] = d_ai
        state = lax.broadcasted_iota(jnp.int32, (NST, LANES), 0)
        lane = lax.broadcasted_iota(jnp.int32, (NST, LANES), 1)
        in_group = jnp.where((state >= lane * SSM_N) & (state < (lane + 1) * SSM_N), 1.0, 0.0)
        o_ld[...] = jnp.dot(d_ld, in_group, precision=lax.Precision.HIGHEST, preferred_element_type=F32)
        o_br[...] = d_br
        o_bi[...] = d_bi

    row = jax.ShapeDtypeStruct((1, NST), F32)
    mat = jax.ShapeDtypeStruct((SSM_GC, NST), F32)
    grp = jax.ShapeDtypeStruct((1, LANES), F32)
    return pl.pallas_call(body, out_shape=[row, row, grp, mat, mat], name="s5_param_bwd")(
        a_re_row, a_im_row, ldt_row, b_re_t, b_im_t, d_lr, d_li, d_bbr, d_bbi)


def _scan_tiles(buf_ref, cst_ref, carry_ref, rows, reverse, reset, h_ref=None, acc_ref=None):
    n_tiles = rows // SUBLANES
    shifts = ((1, 0), (2, 2), (4, 4))
    row_id = lax.broadcasted_iota(jnp.int32, (SUBLANES, SCAN_COLS), 0)
    cols = [(j * SCAN_COLS, NST + j * SCAN_COLS) for j in range(NST // SCAN_COLS)]
    edge = 0 if reverse else SUBLANES - 1
    carry = jnp.where(reset, 0.0, carry_ref[...])
    accs = [(acc_ref[:, pl.ds(o_re, SCAN_COLS)], acc_ref[:, pl.ds(o_im, SCAN_COLS)]) for o_re, o_im in cols] if reverse else None
    for i in range(n_tiles):
        start = ((n_tiles - 1 - i) if reverse else i) * SUBLANES
        rws = pl.ds(start, SUBLANES)
        for j, (o_re, o_im) in enumerate(cols):
            c_re = pl.ds(o_re, SCAN_COLS)
            c_im = pl.ds(o_im, SCAN_COLS)
            xr = buf_ref[rws, c_re]
            xi = buf_ref[rws, c_im]
            for s, base in shifts:
                amount = (SUBLANES - s) if reverse else s
                sr = pltpu.roll(xr, amount, 0)
                si = pltpu.roll(xi, amount, 0)
                mr = cst_ref[pl.ds(base * SUBLANES, SUBLANES), c_re]
                mi = cst_ref[pl.ds((base + 1) * SUBLANES, SUBLANES), c_re]
                xr, xi = xr + (mr * sr - mi * si), xi + (mr * si + mi * sr)
            pr = cst_ref[pl.ds(6 * SUBLANES, SUBLANES), c_re]
            pi = cst_ref[pl.ds(7 * SUBLANES, SUBLANES), c_re]
            cr = carry[:, o_re:o_re + SCAN_COLS]
            ci = carry[:, o_im:o_im + SCAN_COLS]
            xr, xi = xr + (pr * cr - pi * ci), xi + (pr * ci + pi * cr)
            buf_ref[rws, c_re] = xr
            buf_ref[rws, c_im] = xi
            if reverse:
                gnr = jnp.where(row_id == SUBLANES - 1, cr, pltpu.roll(xr, SUBLANES - 1, 0))
                gni = jnp.where(row_id == SUBLANES - 1, ci, pltpu.roll(xi, SUBLANES - 1, 0))
                hr = h_ref[rws, c_re]
                hi = h_ref[rws, c_im]
                accs[j] = (accs[j][0] + (gnr * hr + gni * hi), accs[j][1] + (gni * hr - gnr * hi))
        carry = jnp.broadcast_to(buf_ref[pl.ds(start + edge, 1), :], (SUBLANES, 2 * NST))
    carry_ref[...] = carry
    if reverse:
        for (o_re, o_im), (a_re_, a_im_) in zip(cols, accs):
            acc_ref[:, pl.ds(o_re, SCAN_COLS)] = a_re_
            acc_ref[:, pl.ds(o_im, SCAN_COLS)] = a_im_


def _state_block(v, m):
    return jnp.concatenate([v[:, m * BLK_ST:(m + 1) * BLK_ST], v[:, NST + m * BLK_ST:NST + (m + 1) * BLK_ST]], axis=1)


def _put_state_block(buf_ref, m, val):
    buf_ref[:, pl.ds(m * BLK_ST, BLK_ST)] = val[:, :BLK_ST]
    buf_ref[:, pl.ds(NST + m * BLK_ST, BLK_ST)] = val[:, BLK_ST:]


def s5_fwd(proj, b_m, c_m, cst, d_skip, w_glu, nb, l):
    tt = min(S5_CHUNK, l)
    nt = l // tt
    n_chunks = nb * nt

    def body(u_ref, b_ref, c_ref, cst_ref, ds_ref, wg_ref, y_ref, h_ref, yp_ref, z_ref,
             buf0, buf1, ub0, ub1, carry_ref):
        s = pl.program_id(0)

        @pl.when(s == 0)
        def _():
            for r in (buf0, buf1, ub0, ub1, carry_ref):
                r[...] = jnp.zeros_like(r)

        def step(p_buf, p_u, q_buf):
            h = p_buf[...].astype(BF16)
            h_ref[...] = h
            ypre = (jnp.concatenate([_dot(_state_block(h, m), c_ref[m]) for m in range(N_BLK)], axis=1)
                    + ds_ref[...] * p_u[...])
            yp_ref[...] = ypre
            z = _dot(jax.nn.gelu(ypre).astype(BF16), wg_ref[...])
            z_ref[...] = z
            y_ref[...] = (z[:, :SSM_W] * jax.nn.sigmoid(z[:, SSM_W:])).astype(BF16)
            ub = u_ref[...]
            p_u[...] = ub.astype(F32)
            for m in range(N_BLK):
                _put_state_block(p_buf, m, _dot(ub[:, m * BLK_CH:(m + 1) * BLK_CH], b_ref[m]))
            _scan_tiles(q_buf, cst_ref, carry_ref, tt, False, lax.rem(s + nt - 1, nt) == 0)

        @pl.when(lax.rem(s, 2) == 0)
        def _():
            step(buf0, ub0, buf1)

        @pl.when(lax.rem(s, 2) == 1)
        def _():
            step(buf1, ub1, buf0)

    row_in = lambda s: (jnp.minimum(s, n_chunks - 1), 1)
    row_out = lambda s: (jnp.maximum(s - 2, 0), 0)
    const = lambda s: (0, 0)
    const3 = lambda s: (0, 0, 0)
    return pl.pallas_call(
        body, grid=(n_chunks + 2,),
        in_specs=[pl.BlockSpec((tt, SSM_W), row_in),
                  pl.BlockSpec((N_BLK, BLK_CH, 2 * BLK_ST), const3),
                  pl.BlockSpec((N_BLK, 2 * BLK_ST, BLK_CH), const3),
                  pl.BlockSpec((8 * SUBLANES, NST), const),
                  pl.BlockSpec((1, SSM_W), const),
                  pl.BlockSpec((SSM_W, 2 * SSM_W), const)],
        out_specs=[pl.BlockSpec((tt, SSM_W), row_out),
                   pl.BlockSpec((tt, 2 * NST), row_out),
                   pl.BlockSpec((tt, SSM_W), row_out),
                   pl.BlockSpec((tt, 2 * SSM_W), row_out)],
        out_shape=[jax.ShapeDtypeStruct((nb * l, SSM_W), BF16),
                   jax.ShapeDtypeStruct((nb * l, 2 * NST), BF16),
                   jax.ShapeDtypeStruct((nb * l, SSM_W), F32),
                   jax.ShapeDtypeStruct((nb * l, 2 * SSM_W), F32)],
        scratch_shapes=[pltpu.VMEM((tt, 2 * NST), F32), pltpu.VMEM((tt, 2 * NST), F32),
                        pltpu.VMEM((tt, SSM_W), F32), pltpu.VMEM((tt, SSM_W), F32),
                        pltpu.VMEM((SUBLANES, 2 * NST), F32)],
        compiler_params=_params("arbitrary"), name="s5_fwd")(proj, b_m, c_m, cst, d_skip, w_glu)


def s5_bwd(proj, d_ycat, h, ypre, z, b_m, c_m, cst_rev, d_skip, w_glu, nb, l):
    tt = min(S5_CHUNK, l)
    nt = l // tt
    n_chunks = nb * nt

    def body(u_ref, dy_ref, h_ref, yp_ref, z_ref, b_ref, c_ref, cst_ref, ds_ref, wg_ref,
             du_ref, dwg_ref, dds_ref, dcc_ref, dbb_ref, dlam_ref,
             buf0, buf1, hb0, hb1, ub0, ub1, dyp0, dyp1, carry_ref, dc_ref, db_ref):
        s = pl.program_id(0)

        @pl.when(s == 0)
        def _():
            for r in (buf0, buf1, hb0, hb1, ub0, ub1, dyp0, dyp1, carry_ref, dc_ref, db_ref, dwg_ref, dds_ref, dlam_ref):
                r[...] = jnp.zeros_like(r)

        def step(p_buf, p_h, p_u, p_dyp, q_buf, q_h):
            g = p_buf[...].astype(BF16)
            ub_done = p_u[...].astype(BF16)
            du = []
            for m in range(N_BLK):
                g_m = _state_block(g, m)
                db_ref[m] += _dot_tn(g_m, ub_done[:, m * BLK_CH:(m + 1) * BLK_CH])
                du.append(_dot_nt(g_m, b_ref[m]))
            du_ref[...] = (jnp.concatenate(du, axis=1) + ds_ref[...] * p_dyp[...]).astype(BF16)
            u = u_ref[...].astype(F32)
            ypre = yp_ref[...]
            z = z_ref[...]
            z1 = z[:, :SSM_W]
            sg = jax.nn.sigmoid(z[:, SSM_W:])
            dy = dy_ref[...].astype(F32) * jnp.where(s < n_chunks, 1.0, 0.0)
            dz = jnp.concatenate([dy * sg, dy * z1 * sg * (1.0 - sg)], axis=1).astype(BF16)
            yg, gelu_vjp = jax.vjp(jax.nn.gelu, ypre)
            dwg_ref[...] += _dot_tn(yg.astype(BF16), dz)
            dypre = gelu_vjp(_dot_nt(dz, wg_ref[...]))[0]
            dds_ref[...] += jnp.sum(dypre * u, axis=0, keepdims=True)
            dyb = dypre.astype(BF16)
            hb = h_ref[...]
            p_h[...] = hb.astype(F32)
            p_u[...] = u
            p_dyp[...] = dypre
            for m in range(N_BLK):
                dy_m = dyb[:, m * BLK_CH:(m + 1) * BLK_CH]
                dc_ref[m] += _dot_tn(_state_block(hb, m), dy_m)
                _put_state_block(p_buf, m, _dot_nt(dy_m, c_ref[m]))
            _scan_tiles(q_buf, cst_ref, carry_ref, tt, True, lax.rem(s + nt - 1, nt) == 0, h_ref=q_h, acc_ref=dlam_ref)

        @pl.when(lax.rem(s, 2) == 0)
        def _():
            step(buf0, hb0, ub0, dyp0, buf1, hb1)

        @pl.when(lax.rem(s, 2) == 1)
        def _():
            step(buf1, hb1, ub1, dyp1, buf0, hb0)

        @pl.when(s == n_chunks + 1)
        def _():
            dlam_ref[...] = jnp.broadcast_to(jnp.sum(dlam_ref[...], axis=0, keepdims=True), dlam_ref.shape)
            for acc_ref, out_ref in ((dc_ref, dcc_ref), (db_ref, dbb_ref)):
                for m in range(N_BLK):
                    for ri in range(2):
                        for gl in range(BLK_GROUPS):
                            out_ref[ri, pl.ds((m * BLK_GROUPS + gl) * SSM_N, SSM_N), :] = acc_ref[
                                m, pl.ds(ri * BLK_ST + gl * SSM_N, SSM_N), pl.ds(gl * SSM_GC, SSM_GC)]

    def chunk_rows(o):
        return lax.div(o, nt) * nt + (nt - 1 - lax.rem(o, nt))

    def rrow(col):
        return lambda s: (chunk_rows(jnp.minimum(s, n_chunks - 1)), col)

    const = lambda s: (0, 0)
    const3 = lambda s: (0, 0, 0)
    state_buf = pltpu.VMEM((tt, 2 * NST), F32)
    chan_buf = pltpu.VMEM((tt, SSM_W), F32)
    return pl.pallas_call(
        body, grid=(n_chunks + 2,),
        in_specs=[pl.BlockSpec((tt, SSM_W), rrow(1)),
                  pl.BlockSpec((tt, SSM_W), rrow(1)),
                  pl.BlockSpec((tt, 2 * NST), rrow(0)),
                  pl.BlockSpec((tt, SSM_W), rrow(0)),
                  pl.BlockSpec((tt, 2 * SSM_W), rrow(0)),
                  pl.BlockSpec((N_BLK, BLK_CH, 2 * BLK_ST), const3),
                  pl.BlockSpec((N_BLK, 2 * BLK_ST, BLK_CH), const3),
                  pl.BlockSpec((8 * SUBLANES, NST), const),
                  pl.BlockSpec((1, SSM_W), const),
                  pl.BlockSpec((SSM_W, 2 * SSM_W), const)],
        out_specs=[pl.BlockSpec((tt, SSM_W), lambda s: (chunk_rows(jnp.maximum(s - 2, 0)), 0)),
                   pl.BlockSpec((SSM_W, 2 * SSM_W), const),
                   pl.BlockSpec((1, SSM_W), const),
                   pl.BlockSpec((2, NST, SSM_GC), const3),
                   pl.BlockSpec((2, NST, SSM_GC), const3),
                   pl.BlockSpec((SUBLANES, 2 * NST), const)],
        out_shape=[jax.ShapeDtypeStruct((nb * l, SSM_W), BF16),
                   jax.ShapeDtypeStruct((SSM_W, 2 * SSM_W), F32),
                   jax.ShapeDtypeStruct((1, SSM_W), F32),
                   jax.ShapeDtypeStruct((2, NST, SSM_GC), F32),
                   jax.ShapeDtypeStruct((2, NST, SSM_GC), F32),
                   jax.ShapeDtypeStruct((SUBLANES, 2 * NST), F32)],
        scratch_shapes=[state_buf, state_buf, state_buf, state_buf, chan_buf, chan_buf, chan_buf, chan_buf,
                        pltpu.VMEM((SUBLANES, 2 * NST), F32),
                        pltpu.VMEM((N_BLK, 2 * BLK_ST, BLK_CH), F32), pltpu.VMEM((N_BLK, 2 * BLK_ST, BLK_CH), F32)],
        compiler_params=_params("arbitrary"), name="s5_bwd")(
            proj, d_ycat, h, ypre, z, b_m, c_m, cst_rev, d_skip, w_glu)


def _head_mask(hh):
    lane = lax.broadcasted_iota(jnp.int32, (1, ATT_W), 1)
    return (lane >= hh * MEM_HD) & (lane < (hh + 1) * MEM_HD)


def _mem_kv(mem_ref, gm_ref, wkv_ref):
    m = mem_ref[0]
    mh = m * _rms_scale(m)
    mb = (mh * gm_ref[...]).astype(BF16)
    kv = _dot(mb, wkv_ref[...])
    return mh, mb, kv[:, :ATT_W].astype(BF16), kv[:, ATT_W:].astype(BF16)


def _stack_heads(dst_ref, a):
    for hh in range(MEM_HEADS):
        dst_ref[pl.ds(hh * N_MEM, N_MEM), :] = jnp.where(_head_mask(hh), a, jnp.zeros_like(a))


def _fold_heads(a):
    out = jnp.zeros((N_MEM, ATT_W), a.dtype)
    for hh in range(MEM_HEADS):
        out = out + jnp.where(_head_mask(hh), a[hh * N_MEM:(hh + 1) * N_MEM], 0.0)
    return out


def _softmax_heads(s):
    parts = []
    for hh in range(MEM_HEADS):
        sh = s[:, hh * N_MEM:(hh + 1) * N_MEM]
        e = jnp.exp(sh - jnp.max(sh, axis=-1, keepdims=True))
        parts.append(e / jnp.sum(e, axis=-1, keepdims=True))
    return jnp.concatenate(parts, axis=1)


def attn_fwd(proj, mem, g_mem, w_kv, nb, l):
    tq = min(ATTN_ROWS, l)
    nq = l // tq
    scale = MEM_HD ** -0.5

    def body(q_ref, mem_ref, gm_ref, wkv_ref, o_ref, k_s, v_s):
        @pl.when(pl.program_id(1) == 0)
        def _():
            _, _, k, v = _mem_kv(mem_ref, gm_ref, wkv_ref)
            _stack_heads(k_s, k)
            _stack_heads(v_s, v)

        p = _softmax_heads(_dot_nt(q_ref[...], k_s[...]) * scale)
        o_ref[...] = _dot(p.astype(BF16), v_s[...]).astype(BF16)

    const = lambda b, t: (0, 0)
    stacked = pltpu.VMEM((MEM_HEADS * N_MEM, ATT_W), BF16)
    return pl.pallas_call(
        body, grid=(nb, nq),
        in_specs=[pl.BlockSpec((tq, ATT_W), lambda b, t: (b * nq + t, 3)),
                  pl.BlockSpec((1, N_MEM, D_MODEL), lambda b, t: (b, 0, 0)),
                  pl.BlockSpec((1, D_MODEL), const),
                  pl.BlockSpec((D_MODEL, 2 * ATT_W), const)],
        out_specs=pl.BlockSpec((tq, ATT_W), lambda b, t: (b * nq + t, 0)),
        out_shape=jax.ShapeDtypeStruct((nb * l, ATT_W), BF16),
        scratch_shapes=[stacked, stacked],
        compiler_params=_params("arbitrary", "arbitrary"), name="attn_fwd")(proj, mem, g_mem, w_kv)


def attn_bwd(proj, d_ycat, mem, g_mem, w_kv, nb, l):
    tq = min(ATTN_ROWS, l)
    nq = l // tq
    scale = MEM_HD ** -0.5

    def body(q_ref, do_ref, mem_ref, gm_ref, wkv_ref, dq_ref, dwkv_ref, dgm_ref, k_s, v_s, mb_s, dk_s, dv_s):
        b = pl.program_id(0)
        t = pl.program_id(1)

        @pl.when((b == 0) & (t == 0))
        def _():
            dwkv_ref[...] = jnp.zeros_like(dwkv_ref)
            dgm_ref[...] = jnp.zeros_like(dgm_ref)

        @pl.when(t == 0)
        def _():
            _, mb, k, v = _mem_kv(mem_ref, gm_ref, wkv_ref)
            _stack_heads(k_s, k)
            _stack_heads(v_s, v)
            mb_s[...] = mb
            dk_s[...] = jnp.zeros_like(dk_s)
            dv_s[...] = jnp.zeros_like(dv_s)

        q = q_ref[...]
        do = do_ref[...]
        k = k_s[...]
        p = _softmax_heads(_dot_nt(q, k) * scale)
        dp = _dot_nt(do, v_s[...])
        ds = []
        for hh in range(MEM_HEADS):
            blk = slice(hh * N_MEM, (hh + 1) * N_MEM)
            ds.append(p[:, blk] * (dp[:, blk] - jnp.sum(dp[:, blk] * p[:, blk], axis=-1, keepdims=True)) * scale)
        ds = jnp.concatenate(ds, axis=1).astype(BF16)
        dq_ref[...] = _dot(ds, k).astype(BF16)
        dk_s[...] += _dot_tn(ds, q)
        dv_s[...] += _dot_tn(p.astype(BF16), do)

        @pl.when(t == nq - 1)
        def _():
            dkv = jnp.concatenate([_fold_heads(dk_s[...]), _fold_heads(dv_s[...])], axis=1).astype(BF16)
            dwkv_ref[...] += _dot_tn(mb_s[...], dkv)
            m = mem_ref[0]
            dgm_ref[...] += jnp.sum(_dot_nt(dkv, wkv_ref[...]) * (m * _rms_scale(m)), axis=0, keepdims=True)

    const = lambda b, t: (0, 0)
    return pl.pallas_call(
        body, grid=(nb, nq),
        in_specs=[pl.BlockSpec((tq, ATT_W), lambda b, t: (b * nq + t, 3)),
                  pl.BlockSpec((tq, ATT_W), lambda b, t: (b * nq + t, 3)),
                  pl.BlockSpec((1, N_MEM, D_MODEL), lambda b, t: (b, 0, 0)),
                  pl.BlockSpec((1, D_MODEL), const),
                  pl.BlockSpec((D_MODEL, 2 * ATT_W), const)],
        out_specs=[pl.BlockSpec((tq, ATT_W), lambda b, t: (b * nq + t, 0)),
                   pl.BlockSpec((D_MODEL, 2 * ATT_W), const),
                   pl.BlockSpec((1, D_MODEL), const)],
        out_shape=[jax.ShapeDtypeStruct((nb * l, ATT_W), BF16),
                   jax.ShapeDtypeStruct((D_MODEL, 2 * ATT_W), F32),
                   jax.ShapeDtypeStruct((1, D_MODEL), F32)],
        scratch_shapes=[pltpu.VMEM((MEM_HEADS * N_MEM, ATT_W), BF16), pltpu.VMEM((MEM_HEADS * N_MEM, ATT_W), BF16),
                        pltpu.VMEM((N_MEM, D_MODEL), BF16),
                        pltpu.VMEM((MEM_HEADS * N_MEM, ATT_W), F32), pltpu.VMEM((MEM_HEADS * N_MEM, ATT_W), F32)],
        compiler_params=_params("arbitrary", "arbitrary"), name="attn_bwd")(proj, d_ycat, mem, g_mem, w_kv)


def tail(y_pool, y_ssm, y_att, proj, x2, target, w_out, g_post):
    t = x2.shape[0]
    tm = min(ROW_TILE, t)

    def body(yp_ref, ys_ref, ya_ref, gate_ref, x_ref, tg_ref, w_ref, gp_ref,
             dz_ref, dyc_ref, dgate_ref, dw_ref, dgp_ref, loss_ref):
        @pl.when(pl.program_id(0) == 0)
        def _():
            dw_ref[...] = jnp.zeros_like(dw_ref)
            dgp_ref[...] = jnp.zeros_like(dgp_ref)
            loss_ref[...] = jnp.zeros_like(loss_ref)

        ycat = jnp.concatenate([yp_ref[...], ys_ref[...], ya_ref[...]], axis=1).astype(F32)
        gate = gate_ref[...].astype(F32)
        sg = jax.nn.sigmoid(gate)
        silu = gate * sg
        yb = (ycat * silu).astype(BF16)
        w = w_ref[...]
        out = _dot(yb, w)
        r2 = _rms_scale(out)
        oh = out * r2
        gp = gp_ref[...]
        err = (x_ref[...] + oh * gp) - tg_ref[...]
        loss_ref[...] += 0.5 * jnp.sum(jnp.mean(err * err, axis=-1, keepdims=True), axis=0, keepdims=True)
        dz = err * (1.0 / D_MODEL)
        dz_ref[...] = dz.astype(BF16)
        dgp_ref[...] += jnp.sum(dz * oh, axis=0, keepdims=True)
        dn = dz * gp
        dout = (r2 * (dn - oh * jnp.mean(dn * oh, axis=-1, keepdims=True))).astype(BF16)
        dw_ref[...] += _dot_tn(yb, dout)
        dy = _dot_nt(dout, w)
        dyc_ref[...] = (dy * silu).astype(BF16)
        dgate_ref[...] = (dy * ycat * (sg * (1.0 + gate * (1.0 - sg)))).astype(BF16)

    row = lambda i: (i, 0)
    const = lambda i: (0, 0)
    full = jax.ShapeDtypeStruct((t, D_MODEL), BF16)
    return pl.pallas_call(
        body, grid=(t // tm,),
        in_specs=[pl.BlockSpec((tm, POOL_W), row),
                  pl.BlockSpec((tm, SSM_W), row),
                  pl.BlockSpec((tm, ATT_W), row),
                  pl.BlockSpec((tm, D_MODEL), lambda i: (i, 1)),
                  pl.BlockSpec((tm, D_MODEL), row),
                  pl.BlockSpec((tm, D_MODEL), row),
                  pl.BlockSpec((D_MODEL, D_MODEL), const),
                  pl.BlockSpec((1, D_MODEL), const)],
        out_specs=[pl.BlockSpec((tm, D_MODEL), row),
                   pl.BlockSpec((tm, D_MODEL), row),
                   pl.BlockSpec((tm, D_MODEL), row),
                   pl.BlockSpec((D_MODEL, D_MODEL), const),
                   pl.BlockSpec((1, D_MODEL), const),
                   pl.BlockSpec((1, LANES), const)],
        out_shape=[full, full, full,
                   jax.ShapeDtypeStruct((D_MODEL, D_MODEL), F32),
                   jax.ShapeDtypeStruct((1, D_MODEL), F32),
                   jax.ShapeDtypeStruct((1, LANES), F32)],
        compiler_params=_params("arbitrary"), name="tail")(y_pool, y_ssm, y_att, proj, x2, target, w_out, g_post)


def in_proj_bwd(du_pool, du_ssm, dq, d_gate, x2, dz, g_pre, w_in):
    t = x2.shape[0]
    tm = min(ROW_TILE, t)
    shard = W_IN_SHARD

    def body(dup_ref, dus_ref, dq_ref, dg_ref, x_ref, dz_ref, g_ref, w_ref, gx_ref, dw_ref, dgp_ref):
        @pl.when(pl.program_id(0) == 0)
        def _():
            dw_ref[...] = jnp.zeros_like(dw_ref)
            dgp_ref[...] = jnp.zeros_like(dgp_ref)

        dproj = jnp.concatenate([dup_ref[...], dus_ref[...], dq_ref[...], dg_ref[...]], axis=1)
        x = x_ref[...]
        r1 = _rms_scale(x)
        xh = x * r1
        g = g_ref[...]
        hb = (xh * g).astype(BF16)
        dh = jnp.zeros((tm, D_MODEL), F32)
        for k in range(N_CHIPS):
            dp_k = dproj[:, k * shard:(k + 1) * shard]
            dw_ref[k] += _dot_tn(hb, dp_k)
            dh = dh + _dot_nt(dp_k, w_ref[k])
        dgp_ref[...] += jnp.sum(dh * xh, axis=0, keepdims=True)
        dn = dh * g
        gx_ref[...] = dz_ref[...].astype(F32) + r1 * (dn - xh * jnp.mean(dn * xh, axis=-1, keepdims=True))

    row = lambda i: (i, 0)
    const = lambda i: (0, 0)
    return pl.pallas_call(
        body, grid=(t // tm,),
        in_specs=[pl.BlockSpec((tm, POOL_W), row),
                  pl.BlockSpec((tm, SSM_W), row),
                  pl.BlockSpec((tm, ATT_W), row),
                  pl.BlockSpec((tm, D_MODEL), row),
                  pl.BlockSpec((tm, D_MODEL), row),
                  pl.BlockSpec((tm, D_MODEL), row),
                  pl.BlockSpec((1, D_MODEL), const),
                  pl.BlockSpec((N_CHIPS, D_MODEL, shard), lambda i: (0, 0, 0))],
        out_specs=[pl.BlockSpec((tm, D_MODEL), row),
                   pl.BlockSpec((N_CHIPS, D_MODEL, shard), lambda i: (0, 0, 0)),
                   pl.BlockSpec((1, D_MODEL), const)],
        out_shape=[jax.ShapeDtypeStruct((t, D_MODEL), F32),
                   jax.ShapeDtypeStruct((N_CHIPS, D_MODEL, shard), F32),
                   jax.ShapeDtypeStruct((1, D_MODEL), F32)],
        compiler_params=_params("arbitrary"), name="in_proj_bwd")(du_pool, du_ssm, dq, d_gate, x2, dz, g_pre, w_in)


def _block_diag(blocks):
    g, r, c = blocks.shape
    eye = jnp.eye(g, dtype=blocks.dtype)
    return jnp.einsum("grc,gh->grhc", blocks, eye).reshape(g * r, g * c)


def _block_diag_extract(mat, g, r, c):
    eye = jnp.eye(g, dtype=mat.dtype)
    return jnp.einsum("grhc,gh->grc", mat.reshape(g, r, g, c), eye)


def local_step(x, mem, target, g_pre, w_in_shard, w_pool, pool_scale, a_re, a_im, log_dt, b_re, b_im, c_re, c_im,
               d_skip, later_shards, g_mem, g_post):
    nb, l, _ = x.shape
    x2 = x.reshape(nb * l, D_MODEL)
    tg2 = target.reshape(nb * l, D_MODEL)

    rowv = lambda a: a.reshape(1, NST)
    lanes_last = lambda b: b.transpose(2, 0, 1).reshape(SSM_GC, NST)
    ldt_row = rowv(jnp.broadcast_to(log_dt.reshape(SSM_NG, 1), (SSM_NG, SSM_N)))
    b_re_t = lanes_last(b_re)
    b_im_t = lanes_last(b_im)
    cst_f, cst_b = s5_scan_consts(rowv(a_re), rowv(a_im), ldt_row)
    bbr, bbi = s5_bbar(rowv(a_re), rowv(a_im), ldt_row, b_re_t, b_im_t)
    eye = jnp.eye(BLK_GROUPS, dtype=F32)
    blk_in = lambda bb: jnp.einsum("cmgn,gh->mhcgn", bb.reshape(SSM_GC, N_BLK, BLK_GROUPS, SSM_N), eye).reshape(
        N_BLK, BLK_CH, BLK_ST)
    blk_out = lambda cc: jnp.einsum("mgcn,gh->mgnhc", cc.reshape(N_BLK, BLK_GROUPS, SSM_GC, SSM_N), eye).reshape(
        N_BLK, BLK_ST, BLK_CH)
    b_m = jnp.concatenate([blk_in(bbr), blk_in(bbi)], axis=2).astype(BF16)
    c_m = jnp.concatenate([blk_out(c_re), -blk_out(c_im)], axis=1).astype(BF16)
    w_pool_blk = _block_diag(w_pool.reshape(4, POOL_GW, POOL_GW)).astype(BF16)

    proj, w_in, (g_glu, g_kv, g_out) = in_proj(x2, g_pre, w_in_shard, later_shards)
    w_glu = g_glu.transpose(2, 0, 1).reshape(SSM_W, 2 * SSM_W)
    w_kv = g_kv.reshape(D_MODEL, 2 * ATT_W)
    w_out = g_out.reshape(D_MODEL, D_MODEL)
    y_pool = pool_fwd(proj, w_pool_blk, pool_scale, nb, l)
    y_ssm, h, ypre, z = s5_fwd(proj, b_m, c_m, cst_f, d_skip, w_glu, nb, l)
    y_att = attn_fwd(proj, mem, g_mem, w_kv, nb, l)
    dz, d_ycat, d_gate, d_w_out, d_g_post, loss = tail(y_pool, y_ssm, y_att, proj, x2, tg2, w_out, g_post)
    du_pool, d_wp_blk, d_pool_scale = pool_bwd(proj, d_ycat, w_pool_blk, pool_scale, nb, l)
    du_ssm, d_w_glu, d_d_skip, d_cc, d_bb, d_lam = s5_bwd(
        proj, d_ycat, h, ypre, z, b_m, c_m, cst_b, d_skip, w_glu, nb, l)
    dq, d_w_kv, d_g_mem = attn_bwd(proj, d_ycat, mem, g_mem, w_kv, nb, l)
    grad_x, d_w_in, d_g_pre = in_proj_bwd(du_pool, du_ssm, dq, d_gate, x2, dz, g_pre, w_in)

    d_c = d_cc[0].reshape(SSM_NG, SSM_N, SSM_GC)
    d_ci = d_cc[1].reshape(SSM_NG, SSM_N, SSM_GC)
    d_lam_row = d_lam[0]
    d_ar, d_ai, d_ld, d_br, d_bi = s5_param_bwd(
        rowv(a_re), rowv(a_im), ldt_row, b_re_t, b_im_t,
        d_lam_row[:NST].reshape(1, NST), d_lam_row[NST:].reshape(1, NST), d_bb[0].T, d_bb[1].T)
    from_lanes_last = lambda b: b.reshape(SSM_GC, SSM_NG, SSM_N).transpose(1, 0, 2).reshape(1, SSM_NG, SSM_GC, SSM_N)

    grads = {
        "g_pre": d_g_pre,
        "w_in": d_w_in,
        "w_pool": _block_diag_extract(d_wp_blk, 4, POOL_GW, POOL_GW).reshape(1, 4, POOL_GW, POOL_GW),
        "pool_scale": d_pool_scale,
        "a_re": d_ar.reshape(1, SSM_NG, SSM_N),
        "a_im": d_ai.reshape(1, SSM_NG, SSM_N),
        "log_dt": d_ld[:, :SSM_NG],
        "b_re": from_lanes_last(d_br),
        "b_im": from_lanes_last(d_bi),
        "c_re": d_c.transpose(0, 2, 1).reshape(1, SSM_NG, SSM_GC, SSM_N),
        "c_im": (-d_ci).transpose(0, 2, 1).reshape(1, SSM_NG, SSM_GC, SSM_N),
        "d_skip": d_d_skip,
        "w_glu": d_w_glu,
        "g_mem": d_g_mem,
        "w_kv": d_w_kv,
        "w_out": d_w_out,
        "g_post": d_g_post,
    }
    return loss, grad_x.reshape(nb, l, D_MODEL), grads


VMEM_SPEC = pl.BlockSpec(memory_space=pltpu.VMEM)
ANY_SPEC = pl.BlockSpec(memory_space=pl.ANY)


def _place():
    return lax.axis_index("x"), lax.axis_index("y"), lax.axis_index("c")


def _other_chips(x, y):
    return [(1 - x, y), (x, 1 - y), (1 - x, 1 - y)]


SEMS_PER_ITEM = 5


def reduce_all(g4s, packed):
    n = len(g4s)
    dims = [g.shape[1:] for g in g4s]
    pr = packed.shape[0]

    def body(*refs):
        g_refs, p_ref = refs[:n], refs[n]
        outs, op = refs[n + 1:2 * n + 1], refs[2 * n + 1]
        scr = refs[2 * n + 2:]
        mine, theirs, sendb, recvb = scr[0:n], scr[n:2 * n], scr[2 * n:3 * n], scr[3 * n:4 * n]
        p_theirs, gat, lsems, send_sems, recv_sems = scr[4 * n:]
        x, y, c = _place()
        k = 2 * x + y
        chips = _other_chips(x, y)
        sib = (x, y, 1 - c)

        def halves(rows):
            hr = rows // 2
            return (pl.ds(pl.multiple_of(c * hr, SUBLANES), hr), pl.ds(pl.multiple_of((1 - c) * hr, SUBLANES), hr))

        def remote(src, dst, sem, to):
            return pltpu.make_async_remote_copy(
                src_ref=src, dst_ref=dst, send_sem=send_sems.at[sem], recv_sem=recv_sems.at[sem],
                device_id=to, device_id_type=MESH)

        loads, started = [], []
        for i in range(n):
            my_rows, sib_rows = halves(dims[i][0])
            ld = pltpu.make_async_copy(g_refs[i].at[:, my_rows, :], mine[i], lsems.at[i])
            ld.start()
            loads.append(ld)
            sw = remote(g_refs[i].at[:, sib_rows, :], theirs[i], SEMS_PER_ITEM * i, sib)
            sw.start()
            started.append(sw)
        p_my, p_sib = halves(pr)
        p_swap = remote(p_ref.at[p_sib, :], p_theirs, SEMS_PER_ITEM * n, sib)
        p_swap.start()
        started.append(p_swap)

        p_swap.wait_recv()
        gat[k] = p_ref[p_my, :] + p_theirs[...]
        for j, (px, py) in enumerate(chips):
            cp = remote(gat.at[k], gat.at[k], SEMS_PER_ITEM * n + 1 + j, (px, py, c))
            cp.start()
            started.append(cp)
        ici = []
        for i in range(n):
            loads[i].wait()
            started[i].wait_recv()
            for j, (px, py) in enumerate(chips):
                kp = 2 * px + py
                sendb[i][j] = (mine[i][kp] + theirs[i][kp]).astype(BF16)
                cp = remote(sendb[i].at[j], recvb[i].at[j], SEMS_PER_ITEM * i + 1 + j, (px, py, c))
                cp.start()
                ici.append(cp)

        for j, (px, py) in enumerate(chips):
            kp = 2 * px + py
            remote(gat.at[kp], gat.at[kp], SEMS_PER_ITEM * n + 1 + j, (px, py, c)).wait_recv()
        op[p_my, :] = ((gat[0] + gat[1]) + gat[2]) + gat[3]
        last = remote(op.at[p_my, :], op.at[p_my, :], SEMS_PER_ITEM * n + 4, sib)
        last.start()
        started.append(last)
        for i in range(n):
            my_rows, _ = halves(dims[i][0])
            half = mine[i][k] + theirs[i][k]
            for j in range(3):
                ici[3 * i + j].wait_recv()
                half = half + recvb[i][j].astype(F32)
            outs[i][my_rows, :] = half
            last = remote(outs[i].at[my_rows, :], outs[i].at[my_rows, :], SEMS_PER_ITEM * i + 4, sib)
            last.start()
            started.append(last)
        remote(op.at[p_sib, :], op.at[p_sib, :], SEMS_PER_ITEM * n + 4, sib).wait_recv()
        for i in range(n):
            _, sib_rows = halves(dims[i][0])
            remote(outs[i].at[sib_rows, :], outs[i].at[sib_rows, :], SEMS_PER_ITEM * i + 4, sib).wait_recv()
        for cp in started + ici:
            cp.wait_send()

    n_sems = SEMS_PER_ITEM * (n + 1)
    scratch = ([pltpu.VMEM((N_CHIPS, r // 2, cd), F32) for r, cd in dims] * 2
               + [pltpu.VMEM((3, r // 2, cd), BF16) for r, cd in dims] * 2
               + [pltpu.VMEM((pr // 2, LANES), F32), pltpu.VMEM((N_CHIPS, pr // 2, LANES), F32),
                  pltpu.SemaphoreType.DMA((n,)), pltpu.SemaphoreType.DMA((n_sems,)), pltpu.SemaphoreType.DMA((n_sems,))])
    res = pl.pallas_call(
        body,
        out_shape=[jax.ShapeDtypeStruct(d, F32) for d in dims] + [jax.ShapeDtypeStruct(packed.shape, F32)],
        in_specs=[ANY_SPEC] * n + [VMEM_SPEC], out_specs=[VMEM_SPEC] * (n + 1),
        scratch_shapes=scratch,
        compiler_params=pltpu.CompilerParams(vmem_limit_bytes=VMEM_LIMIT),
        name="reduce_all")(*g4s, packed)
    return res[:n], res[n]


def adamw_all(ws, gs, ms, vs):
    n = len(ws)
    chunked = [a.ndim == 3 and a.shape[0] == 1 and a.shape[1] % (ADAMW_STEPS * SUBLANES) == 0 for a in ws]

    def body(*refs):
        w, g, m, v = refs[:n], refs[n:2 * n], refs[2 * n:3 * n], refs[3 * n:4 * n]
        od, om, ov = refs[4 * n:5 * n], refs[5 * n:6 * n], refs[6 * n:]
        for i in range(n):
            if chunked[i]:
                od[i][...], om[i][...], ov[i][...] = _adamw(w[i][...], g[i][...], m[i][...], v[i][...])

        @pl.when(pl.program_id(0) == 0)
        def _():
            for i in range(n):
                if not chunked[i]:
                    od[i][...], om[i][...], ov[i][...] = _adamw(w[i][...], g[i][...], m[i][...], v[i][...])

    def spec(a, is_chunked):
        if is_chunked:
            return pl.BlockSpec((1, a.shape[1] // ADAMW_STEPS, a.shape[2]), lambda s: (0, s, 0))
        return pl.BlockSpec(a.shape, lambda s: (0,) * a.ndim)

    specs = [spec(a, c) for a, c in zip(ws, chunked)]
    shapes = [jax.ShapeDtypeStruct(a.shape, F32) for a in ws]
    res = pl.pallas_call(
        body, grid=(ADAMW_STEPS,), out_shape=shapes * 3, in_specs=specs * 4, out_specs=specs * 3,
        compiler_params=_params("arbitrary"), name="adamw_all")(*ws, *gs, *ms, *vs)
    return res[:n], res[n:2 * n], res[2 * n:]


WEIGHTS = ("g_pre", "w_in", "w_pool", "pool_scale", "a_re", "a_im", "log_dt", "b_re", "b_im", "c_re", "c_im", "d_skip",
           "w_glu", "g_mem", "w_kv", "w_out", "g_post")
SHARDED = ("w_in", "w_glu", "w_kv", "w_out")
SWAPPED = ("b_re", "b_im", "w_glu")
REPLICATED = tuple(n for n in WEIGHTS if n not in SHARDED)
PACK_TILE = SUBLANES * LANES


def _pack(arrays):
    parts = []
    for a in arrays:
        flat = a.reshape(-1)
        parts.append(jnp.pad(flat, (0, -flat.shape[0] % PACK_TILE)).reshape(-1, LANES))
    rows = sum(p.shape[0] for p in parts)
    if rows % (2 * SUBLANES):
        parts.append(jnp.zeros((SUBLANES, LANES), F32))
    return jnp.concatenate(parts, axis=0)


def _unpack(packed, shapes):
    out, row = [], 0
    for shp in shapes:
        size = math.prod(shp)
        rows = -(-size // PACK_TILE) * SUBLANES
        out.append(packed[row:row + rows].reshape(-1)[:size].reshape(shp))
        row += rows
    return out


def kernel(x, mem, g_pre, w_in, w_pool, pool_scale, a_re, a_im, log_dt, b_re, b_im, c_re, c_im, d_skip, w_glu, g_mem, w_kv, w_out, g_post, loss_target, m_g_pre, m_w_in, m_w_pool, m_pool_scale, m_a_re, m_a_im, m_log_dt, m_b_re, m_b_im, m_c_re, m_c_im, m_d_skip, m_w_glu, m_g_mem, m_w_kv, m_w_out, m_g_post, v_g_pre, v_w_in, v_w_pool, v_pool_scale, v_a_re, v_a_im, v_log_dt, v_b_re, v_b_im, v_c_re, v_c_im, v_d_skip, v_w_glu, v_g_mem, v_w_kv, v_w_out, v_g_post):
    given = dict(locals())
    wts = {n: given[n] for n in WEIGHTS}
    mom = {n: given["m_" + n] for n in WEIGHTS}
    var = {n: given["v_" + n] for n in WEIGHTS}

    swap = lambda a: jnp.swapaxes(a, -1, -2)
    view = lambda n, a: swap(a) if n in SWAPPED else a

    loss_part, grad_x, grads = local_step(
        x, mem, loss_target, g_pre, w_in[0], w_pool[0], pool_scale, a_re[0], a_im[0], log_dt[0], b_re[0], b_im[0],
        c_re[0], c_im[0], d_skip, [swap(w_glu)[0], w_kv[0], w_out[0]], g_mem, g_post)

    glu_cols = 2 * SSM_W // N_CHIPS
    partial4 = [
        grads["w_in"],
        grads["w_glu"].reshape(SSM_W, N_CHIPS, glu_cols).transpose(1, 2, 0),
        grads["w_kv"].reshape(N_CHIPS, D_MODEL // N_CHIPS, 2 * ATT_W),
        grads["w_out"].reshape(N_CHIPS, D_MODEL // N_CHIPS, D_MODEL),
    ]
    loss_tile = jnp.broadcast_to(loss_part, (SUBLANES, LANES))
    sharded_g, packed_g = reduce_all(partial4, _pack([grads[n] for n in REPLICATED] + [loss_tile]))
    small_g = _unpack(packed_g, [view(n, wts[n]).shape for n in REPLICATED] + [(SUBLANES, LANES)])
    loss = small_g[-1][0, 0]
    grad = dict(zip(SHARDED, [g[None] for g in sharded_g]))
    grad.update(zip(REPLICATED, small_g[:-1]))

    deltas, new_ms, new_vs = adamw_all([view(n, wts[n]) for n in WEIGHTS], [grad[n] for n in WEIGHTS],
                                       [view(n, mom[n]) for n in WEIGHTS], [view(n, var[n]) for n in WEIGHTS])
    back = lambda arrs: [view(n, a) for n, a in zip(WEIGHTS, arrs)]
    return (loss, grad_x, *back([grad[n] for n in WEIGHTS]), *back(deltas), *back(new_ms), *back(new_vs))
```

```python
import math

import jax
import jax.numpy as jnp
from jax import lax
from jax.experimental import pallas as pl
from jax.experimental.pallas import tpu as pltpu

F32 = jnp.float32
BF16 = jnp.bfloat16

D_MODEL = 1024
POOL_W = 384
SSM_W = 384
ATT_W = 256
POOL_GW = 96
POOL_PAD = 16
SSM_NG = 24
SSM_N = 64
SSM_GC = 16
N_CHIPS = 4
W_IN_SHARD = 2 * D_MODEL // N_CHIPS
NST = SSM_NG * SSM_N
BLK_CH = 128
BLK_GROUPS = BLK_CH // SSM_GC
BLK_ST = BLK_GROUPS * SSM_N
N_BLK = SSM_W // BLK_CH
N_MEM = 256
MEM_HEADS = 4
MEM_HD = 64
EPS = 1e-6

ADAM_LR = 0.001
ADAM_B1 = 0.9
ADAM_B2 = 0.999
ADAM_EPS = 1e-08
ADAM_WD = 0.01
ADAM_STEP = 10

SUBLANES = 8
LANES = 128
V7X_VMEM_BYTES = 64 * 2**20
VMEM_LIMIT = V7X_VMEM_BYTES - 8 * 2**20
SCAN_COLS = 512
ROW_TILE = 512
IN_PROJ_ROWS = 2048
ATTN_ROWS = 2048
S5_CHUNK = 256
ADAMW_STEPS = 4
MESH = pl.DeviceIdType.MESH

NT = (((1,), (1,)), ((), ()))
TN = (((0,), (0,)), ((), ()))


def _params(*sem):
    return pltpu.CompilerParams(dimension_semantics=sem, vmem_limit_bytes=VMEM_LIMIT)


def _dot(a, b):
    return jnp.dot(a, b, preferred_element_type=F32)


def _dot_nt(a, b):
    return lax.dot_general(a, b, NT, preferred_element_type=F32)


def _dot_tn(a, b):
    return lax.dot_general(a, b, TN, preferred_element_type=F32)


def _rms_scale(v):
    return lax.rsqrt(jnp.mean(v * v, axis=-1, keepdims=True) + EPS)


def _adamw(w, g, m, v):
    m = ADAM_B1 * m + (1.0 - ADAM_B1) * g
    v = ADAM_B2 * v + (1.0 - ADAM_B2) * (g * g)
    m_hat = m / (1.0 - ADAM_B1 ** ADAM_STEP)
    v_hat = v / (1.0 - ADAM_B2 ** ADAM_STEP)
    delta = -ADAM_LR * (m_hat / (jnp.sqrt(v_hat) + ADAM_EPS) + ADAM_WD * w)
    return delta, m, v


def _phase_shard(j):
    cx, cy = lax.axis_index("x"), lax.axis_index("y")
    others = [2 * px + py for px, py in _other_chips(cx, cy)]
    own = 2 * cx + cy
    return jnp.where(j == 0, own, jnp.where(j == 1, others[0], jnp.where(j == 2, others[1], others[2])))


def in_proj(x2, g_pre, w_in_shard, later_shards):
    t = x2.shape[0]
    tm = min(IN_PROJ_ROWS, t)
    n_tiles = t // tm
    n = len(later_shards)

    def body(x_ref, g_ref, ws_ref, *rest):
        shard_refs = rest[:n]
        o_ref, w_out_ref, gathered = rest[n], rest[n + 1], rest[n + 2:2 * n + 2]
        w_buf, h_buf = rest[2 * n + 2], rest[2 * n + 3]
        cast_bufs = rest[2 * n + 4:3 * n + 4]
        local_sems, send_sems, recv_sems = rest[3 * n + 4:]
        j = pl.program_id(0)
        i = pl.program_id(1)
        cx, cy, cc = _place()
        k = 2 * cx + cy
        chips = _other_chips(cx, cy)

        half = w_in_shard.shape[0] // 2

        def w_copy(p, block, core_half, forward):
            px, py = chips[p]
            rows = w_buf.at[block, pl.ds(pl.multiple_of(core_half * half, 2 * SUBLANES), half)]
            sem = 3 + p if forward else p
            return pltpu.make_async_remote_copy(
                src_ref=rows, dst_ref=rows, send_sem=send_sems.at[sem], recv_sem=recv_sems.at[sem],
                device_id=(cx, cy, 1 - cc) if forward else (px, py, cc), device_id_type=MESH)

        def own(m):
            return pltpu.make_async_copy(cast_bufs[m], gathered[m].at[k], local_sems.at[m])

        def later_copy(m, p, block, core_half, forward):
            px, py = chips[p]
            hr = later_shards[m].shape[0] // 2
            rows = pl.ds(pl.multiple_of(core_half * hr, 2 * SUBLANES), hr)
            dst = gathered[m].at[block, rows]
            sem = 6 + 6 * m + (3 + p if forward else p)
            return pltpu.make_async_remote_copy(
                src_ref=dst if forward else cast_bufs[m].at[rows], dst_ref=dst, send_sem=send_sems.at[sem],
                recv_sem=recv_sems.at[sem], device_id=(cx, cy, 1 - cc) if forward else (px, py, cc),
                device_id_type=MESH)

        @pl.when((j == 0) & (i == 0))
        def _():
            w_buf[k] = ws_ref[...].astype(BF16)
            for p in range(3):
                w_copy(p, k, cc, False).start()
            for m in range(n):
                cast_bufs[m][...] = shard_refs[m][...].astype(BF16)
                own(m).start()
                for p in range(3):
                    later_copy(m, p, k, cc, False).start()

        rows = pl.ds(pl.multiple_of(i * tm, tm), tm)

        @pl.when(j == 0)
        def _():
            x = x_ref[...]
            h_buf[rows, :] = (x * _rms_scale(x) * g_ref[...]).astype(BF16)

        for p in range(3):
            @pl.when((j == p) & (i == n_tiles - 1))
            def _(p=p):
                px, py = chips[p]
                w_copy(p, 2 * px + py, cc, False).wait_recv()
                w_copy(p, 2 * px + py, cc, True).start()

            @pl.when((j == p + 1) & (i == 0))
            def _(p=p):
                px, py = chips[p]
                w_copy(p, 2 * px + py, 1 - cc, True).wait_recv()

        o_ref[...] = _dot(h_buf[rows, :], w_buf[_phase_shard(j)]).astype(BF16)

        out_cp = pltpu.make_async_copy(w_buf, w_out_ref, local_sems.at[n])

        @pl.when((j == N_CHIPS - 1) & (i == 0))
        def _():
            out_cp.start()
            for m in range(n):
                for p in range(3):
                    px, py = chips[p]
                    later_copy(m, p, 2 * px + py, cc, False).wait_recv()
                    later_copy(m, p, 2 * px + py, cc, True).start()

        @pl.when((j == N_CHIPS - 1) & (i == n_tiles - 1))
        def _():
            for m in range(n):
                own(m).wait()
            for m in range(n):
                for p in range(3):
                    px, py = chips[p]
                    later_copy(m, p, 2 * px + py, 1 - cc, True).wait_recv()
            for p in range(3):
                px, py = chips[p]
                w_copy(p, k, cc, False).wait_send()
                w_copy(p, 2 * px + py, cc, True).wait_send()
            for m in range(n):
                for p in range(3):
                    px, py = chips[p]
                    later_copy(m, p, k, cc, False).wait_send()
                    later_copy(m, p, 2 * px + py, cc, True).wait_send()
            out_cp.wait()

    whole = lambda a: pl.BlockSpec(a.shape, lambda j, i: (0,) * a.ndim)
    res = pl.pallas_call(
        body, grid=(N_CHIPS, n_tiles),
        in_specs=[pl.BlockSpec((tm, D_MODEL), lambda j, i: (jnp.where(j == 0, i, n_tiles - 1), 0)),
                  pl.BlockSpec((1, D_MODEL), lambda j, i: (0, 0)),
                  whole(w_in_shard)] + [whole(s) for s in later_shards],
        out_specs=[pl.BlockSpec((tm, W_IN_SHARD), lambda j, i: (i, _phase_shard(j)))] + [ANY_SPEC] * (n + 1),
        out_shape=([jax.ShapeDtypeStruct((t, 2 * D_MODEL), BF16),
                    jax.ShapeDtypeStruct((N_CHIPS,) + w_in_shard.shape, BF16)]
                   + [jax.ShapeDtypeStruct((N_CHIPS,) + s.shape, BF16) for s in later_shards]),
        scratch_shapes=([pltpu.VMEM((N_CHIPS,) + w_in_shard.shape, BF16), pltpu.VMEM((t, D_MODEL), BF16)]
                        + [pltpu.VMEM(s.shape, BF16) for s in later_shards]
                        + [pltpu.SemaphoreType.DMA((n + 1,)), pltpu.SemaphoreType.DMA((6 * n + 6,)),
                           pltpu.SemaphoreType.DMA((6 * n + 6,))]),
        compiler_params=_params("arbitrary", "arbitrary"), name="in_proj")(x2, g_pre, w_in_shard, *later_shards)
    return res[0], res[1], res[2:]


def _pool_lane_window(shape):
    ch = lax.broadcasted_iota(jnp.int32, shape, 1)
    return jnp.where(ch < POOL_GW, 2.0, jnp.where(ch < 2 * POOL_GW, 4.0, jnp.where(ch < 3 * POOL_GW, 8.0, 16.0)))


def _pool_select(win, s2, s4, s8, s16):
    return jnp.where(win == 2.0, s2, jnp.where(win == 4.0, s4, jnp.where(win == 8.0, s8, s16)))


def _pool_div_count(v, win):
    pos = (lax.broadcasted_iota(jnp.int32, (POOL_PAD, POOL_W), 0) + 1).astype(F32)
    head = v[:POOL_PAD] / jnp.minimum(pos, win)
    return jnp.concatenate([head, v[POOL_PAD:] * (1.0 / win)], axis=0)


def _pool_diff(u, pad_ref, l):
    lo = POOL_PAD
    pad_ref[pl.ds(lo, l), :] = u
    s2 = u + pad_ref[pl.ds(lo - 1, l), :]
    pad_ref[pl.ds(lo, l), :] = s2
    s4 = s2 + pad_ref[pl.ds(lo - 2, l), :]
    pad_ref[pl.ds(lo, l), :] = s4
    s8 = s4 + pad_ref[pl.ds(lo - 4, l), :]
    pad_ref[pl.ds(lo, l), :] = s8
    s16 = s8 + pad_ref[pl.ds(lo - 8, l), :]
    win = _pool_lane_window((1, POOL_W))
    return _pool_div_count(_pool_select(win, s2, s4, s8, s16), win) - u, win


def pool_fwd(proj, w_blk, pool_scale, nb, l):
    def body(u_ref, w_ref, ps_ref, y_ref, pad_ref):
        pad_ref[pl.ds(0, POOL_PAD), :] = jnp.zeros((POOL_PAD, POOL_W), F32)
        d, _ = _pool_diff(u_ref[...].astype(F32), pad_ref, l)
        y_ref[...] = (_dot(d.astype(BF16), w_ref[...]) * ps_ref[...]).astype(BF16)

    return pl.pallas_call(
        body, grid=(nb,),
        in_specs=[pl.BlockSpec((l, POOL_W), lambda b: (b, 0)),
                  pl.BlockSpec((POOL_W, POOL_W), lambda b: (0, 0)),
                  pl.BlockSpec((1, POOL_W), lambda b: (0, 0))],
        out_specs=pl.BlockSpec((l, POOL_W), lambda b: (b, 0)),
        out_shape=jax.ShapeDtypeStruct((nb * l, POOL_W), BF16),
        scratch_shapes=[pltpu.VMEM((POOL_PAD + l, POOL_W), F32)],
        compiler_params=_params("arbitrary"), name="pool_fwd")(proj, w_blk, pool_scale)


def pool_bwd(proj, d_ycat, w_blk, pool_scale, nb, l):
    def body(u_ref, dy_ref, w_ref, ps_ref, du_ref, dw_ref, dps_ref, pad_ref, padb_ref):
        b = pl.program_id(0)

        @pl.when(b == 0)
        def _():
            dw_ref[...] = jnp.zeros_like(dw_ref)
            dps_ref[...] = jnp.zeros_like(dps_ref)

        pad_ref[pl.ds(0, POOL_PAD), :] = jnp.zeros((POOL_PAD, POOL_W), F32)
        padb_ref[pl.ds(l, POOL_PAD), :] = jnp.zeros((POOL_PAD, POOL_W), F32)
        d, win = _pool_diff(u_ref[...].astype(F32), pad_ref, l)
        db = d.astype(BF16)
        w = w_ref[...]
        dy = dy_ref[...].astype(F32)
        dps_ref[...] += jnp.sum(dy * _dot(db, w), axis=0, keepdims=True)
        dyo = (dy * ps_ref[...]).astype(BF16)
        dw_ref[...] += _dot_tn(db, dyo)
        dd = _dot_nt(dyo, w)
        e = _pool_div_count(dd, win)
        padb_ref[pl.ds(0, l), :] = e
        f2 = e + padb_ref[pl.ds(1, l), :]
        padb_ref[pl.ds(0, l), :] = f2
        f4 = f2 + padb_ref[pl.ds(2, l), :]
        padb_ref[pl.ds(0, l), :] = f4
        f8 = f4 + padb_ref[pl.ds(4, l), :]
        padb_ref[pl.ds(0, l), :] = f8
        f16 = f8 + padb_ref[pl.ds(8, l), :]
        du_ref[...] = (_pool_select(win, f2, f4, f8, f16) - dd).astype(BF16)

    return pl.pallas_call(
        body, grid=(nb,),
        in_specs=[pl.BlockSpec((l, POOL_W), lambda b: (b, 0)),
                  pl.BlockSpec((l, POOL_W), lambda b: (b, 0)),
                  pl.BlockSpec((POOL_W, POOL_W), lambda b: (0, 0)),
                  pl.BlockSpec((1, POOL_W), lambda b: (0, 0))],
        out_specs=[pl.BlockSpec((l, POOL_W), lambda b: (b, 0)),
                   pl.BlockSpec((POOL_W, POOL_W), lambda b: (0, 0)),
                   pl.BlockSpec((1, POOL_W), lambda b: (0, 0))],
        out_shape=[jax.ShapeDtypeStruct((nb * l, POOL_W), BF16),
                   jax.ShapeDtypeStruct((POOL_W, POOL_W), F32),
                   jax.ShapeDtypeStruct((1, POOL_W), F32)],
        scratch_shapes=[pltpu.VMEM((POOL_PAD + l, POOL_W), F32), pltpu.VMEM((POOL_PAD + l, POOL_W), F32)],
        compiler_params=_params("arbitrary"), name="pool_bwd")(proj, d_ycat, w_blk, pool_scale)


def _discretise(a_re, a_im, ldt, b_re, b_im):
    dt = jnp.exp(ldt)
    mag = jnp.exp(a_re * dt)
    th = a_im * dt
    lr = mag * jnp.cos(th)
    li = mag * jnp.sin(th)
    nr = lr - 1.0
    den = a_re * a_re + a_im * a_im
    fr = (nr * a_re + li * a_im) / den
    fi = (li * a_re - nr * a_im) / den
    return lr, li, fr * b_re - fi * b_im, fr * b_im + fi * b_re


def _cmul(ar, ai, br, bi):
    return ar * br - ai * bi, ar * bi + ai * br


def s5_scan_consts(a_re_row, a_im_row, ldt_row):
    def body(ar_ref, ai_ref, ld_ref, f_ref, b_ref):
        dt = jnp.exp(ld_ref[...])
        mag = jnp.exp(ar_ref[...] * dt)
        th = ai_ref[...] * dt
        lr = mag * jnp.cos(th)
        li = mag * jnp.sin(th)
        r = lax.broadcasted_iota(jnp.int32, (SUBLANES, NST), 0)
        for out_ref, sign, rev in ((f_ref, 1.0, False), (b_ref, -1.0, True)):
            p = [(lr, sign * li)]
            for _ in range(SUBLANES - 1):
                p.append(_cmul(p[-1][0], p[-1][1], lr, sign * li))
            for idx, s in enumerate((1, 2, 4)):
                inside = (r < SUBLANES - s) if rev else (r >= s)
                out_ref[pl.ds(2 * idx * SUBLANES, SUBLANES), :] = jnp.where(inside, p[s - 1][0], 0.0)
                out_ref[pl.ds((2 * idx + 1) * SUBLANES, SUBLANES), :] = jnp.where(inside, p[s - 1][1], 0.0)
            pr = jnp.zeros((SUBLANES, NST), F32)
            pi = jnp.zeros((SUBLANES, NST), F32)
            for row in range(SUBLANES):
                power = (SUBLANES - row) if rev else (row + 1)
                pr = jnp.where(r == row, p[power - 1][0], pr)
                pi = jnp.where(r == row, p[power - 1][1], pi)
            out_ref[pl.ds(6 * SUBLANES, SUBLANES), :] = pr
            out_ref[pl.ds(7 * SUBLANES, SUBLANES), :] = pi

    shape = jax.ShapeDtypeStruct((8 * SUBLANES, NST), F32)
    return pl.pallas_call(body, out_shape=[shape, shape], name="s5_scan_consts")(a_re_row, a_im_row, ldt_row)


def s5_bbar(a_re_row, a_im_row, ldt_row, b_re_t, b_im_t):
    def body(ar, ai, ld, br, bi, o_re, o_im):
        _, _, bbr, bbi = _discretise(ar[...], ai[...], ld[...], br[...], bi[...])
        o_re[...] = bbr
        o_im[...] = bbi

    shape = jax.ShapeDtypeStruct((SSM_GC, NST), F32)
    return pl.pallas_call(body, out_shape=[shape, shape], name="s5_bbar")(a_re_row, a_im_row, ldt_row, b_re_t, b_im_t)


def s5_param_bwd(a_re_row, a_im_row, ldt_row, b_re_t, b_im_t, d_lr, d_li, d_bbr, d_bbi):
    def body(ar, ai, ld, br, bi, g_lr, g_li, g_br, g_bi, o_ar, o_ai, o_ld, o_br, o_bi):
        _, vjp = jax.vjp(_discretise, ar[...], ai[...], ld[...], br[...], bi[...])
        d_ar, d_ai, d_ld, d_br, d_bi = vjp((g_lr[...], g_li[...], g_br[...], g_bi[...]))
        o_ar[...] = d_ar
        o_ai[...] = d_ai
        state = lax.broadcasted_iota(jnp.int32, (NST, LANES), 0)
        lane = lax.broadcasted_iota(jnp.int32, (NST, LANES), 1)
        in_group = jnp.where((state >= lane * SSM_N) & (state < (lane + 1) * SSM_N), 1.0, 0.0)
        o_ld[...] = jnp.dot(d_ld, in_group, precision=lax.Precision.HIGHEST, preferred_element_type=F32)
        o_br[...] = d_br
        o_bi[...] = d_bi

    row = jax.ShapeDtypeStruct((1, NST), F32)
    mat = jax.ShapeDtypeStruct((SSM_GC, NST), F32)
    grp = jax.ShapeDtypeStruct((1, LANES), F32)
    return pl.pallas_call(body, out_shape=[row, row, grp, mat, mat], name="s5_param_bwd")(
        a_re_row, a_im_row, ldt_row, b_re_t, b_im_t, d_lr, d_li, d_bbr, d_bbi)


def _scan_tiles(buf_ref, cst_ref, carry_ref, rows, reverse, reset, h_ref=None, acc_ref=None):
    n_tiles = rows // SUBLANES
    shifts = ((1, 0), (2, 2), (4, 4))
    row_id = lax.broadcasted_iota(jnp.int32, (SUBLANES, SCAN_COLS), 0)
    cols = [(j * SCAN_COLS, NST + j * SCAN_COLS) for j in range(NST // SCAN_COLS)]
    edge = 0 if reverse else SUBLANES - 1
    carry = jnp.where(reset, 0.0, carry_ref[...])
    accs = [(acc_ref[:, pl.ds(o_re, SCAN_COLS)], acc_ref[:, pl.ds(o_im, SCAN_COLS)]) for o_re, o_im in cols] if reverse else None
    for i in range(n_tiles):
        start = ((n_tiles - 1 - i) if reverse else i) * SUBLANES
        rws = pl.ds(start, SUBLANES)
        for j, (o_re, o_im) in enumerate(cols):
            c_re = pl.ds(o_re, SCAN_COLS)
            c_im = pl.ds(o_im, SCAN_COLS)
            xr = buf_ref[rws, c_re]
            xi = buf_ref[rws, c_im]
            for s, base in shifts:
                amount = (SUBLANES - s) if reverse else s
                sr = pltpu.roll(xr, amount, 0)
                si = pltpu.roll(xi, amount, 0)
                mr = cst_ref[pl.ds(base * SUBLANES, SUBLANES), c_re]
                mi = cst_ref[pl.ds((base + 1) * SUBLANES, SUBLANES), c_re]
                xr, xi = xr + (mr * sr - mi * si), xi + (mr * si + mi * sr)
            pr = cst_ref[pl.ds(6 * SUBLANES, SUBLANES), c_re]
            pi = cst_ref[pl.ds(7 * SUBLANES, SUBLANES), c_re]
            cr = carry[:, o_re:o_re + SCAN_COLS]
            ci = carry[:, o_im:o_im + SCAN_COLS]
            xr, xi = xr + (pr * cr - pi * ci), xi + (pr * ci + pi * cr)
            buf_ref[rws, c_re] = xr
            buf_ref[rws, c_im] = xi
            if reverse:
                gnr = jnp.where(row_id == SUBLANES - 1, cr, pltpu.roll(xr, SUBLANES - 1, 0))
                gni = jnp.where(row_id == SUBLANES - 1, ci, pltpu.roll(xi, SUBLANES - 1, 0))
                hr = h_ref[rws, c_re]
                hi = h_ref[rws, c_im]
                accs[j] = (accs[j][0] + (gnr * hr + gni * hi), accs[j][1] + (gni * hr - gnr * hi))
        carry = jnp.broadcast_to(buf_ref[pl.ds(start + edge, 1), :], (SUBLANES, 2 * NST))
    carry_ref[...] = carry
    if reverse:
        for (o_re, o_im), (a_re_, a_im_) in zip(cols, accs):
            acc_ref[:, pl.ds(o_re, SCAN_COLS)] = a_re_
            acc_ref[:, pl.ds(o_im, SCAN_COLS)] = a_im_


def _state_block(v, m):
    return jnp.concatenate([v[:, m * BLK_ST:(m + 1) * BLK_ST], v[:, NST + m * BLK_ST:NST + (m + 1) * BLK_ST]], axis=1)


def _put_state_block(buf_ref, m, val):
    buf_ref[:, pl.ds(m * BLK_ST, BLK_ST)] = val[:, :BLK_ST]
    buf_ref[:, pl.ds(NST + m * BLK_ST, BLK_ST)] = val[:, BLK_ST:]


def s5_fwd(proj, b_m, c_m, cst, d_skip, w_glu, nb, l):
    tt = min(S5_CHUNK, l)
    nt = l // tt
    n_chunks = nb * nt

    def body(u_ref, b_ref, c_ref, cst_ref, ds_ref, wg_ref, y_ref, h_ref, yp_ref, z_ref,
             buf0, buf1, ub0, ub1, carry_ref):
        s = pl.program_id(0)

        @pl.when(s == 0)
        def _():
            for r in (buf0, buf1, ub0, ub1, carry_ref):
                r[...] = jnp.zeros_like(r)

        def step(p_buf, p_u, q_buf):
            h = p_buf[...].astype(BF16)
            h_ref[...] = h
            ypre = (jnp.concatenate([_dot(_state_block(h, m), c_ref[m]) for m in range(N_BLK)], axis=1)
                    + ds_ref[...] * p_u[...])
            yp_ref[...] = ypre
            z = _dot(jax.nn.gelu(ypre).astype(BF16), wg_ref[...])
            z_ref[...] = z
            y_ref[...] = (z[:, :SSM_W] * jax.nn.sigmoid(z[:, SSM_W:])).astype(BF16)
            ub = u_ref[...]
            p_u[...] = ub.astype(F32)
            for m in range(N_BLK):
                _put_state_block(p_buf, m, _dot(ub[:, m * BLK_CH:(m + 1) * BLK_CH], b_ref[m]))
            _scan_tiles(q_buf, cst_ref, carry_ref, tt, False, lax.rem(s + nt - 1, nt) == 0)

        @pl.when(lax.rem(s, 2) == 0)
        def _():
            step(buf0, ub0, buf1)

        @pl.when(lax.rem(s, 2) == 1)
        def _():
            step(buf1, ub1, buf0)

    row_in = lambda s: (jnp.minimum(s, n_chunks - 1), 1)
    row_out = lambda s: (jnp.maximum(s - 2, 0), 0)
    const = lambda s: (0, 0)
    const3 = lambda s: (0, 0, 0)
    return pl.pallas_call(
        body, grid=(n_chunks + 2,),
        in_specs=[pl.BlockSpec((tt, SSM_W), row_in),
                  pl.BlockSpec((N_BLK, BLK_CH, 2 * BLK_ST), const3),
                  pl.BlockSpec((N_BLK, 2 * BLK_ST, BLK_CH), const3),
                  pl.BlockSpec((8 * SUBLANES, NST), const),
                  pl.BlockSpec((1, SSM_W), const),
                  pl.BlockSpec((SSM_W, 2 * SSM_W), const)],
        out_specs=[pl.BlockSpec((tt, SSM_W), row_out),
                   pl.BlockSpec((tt, 2 * NST), row_out),
                   pl.BlockSpec((tt, SSM_W), row_out),
                   pl.BlockSpec((tt, 2 * SSM_W), row_out)],
        out_shape=[jax.ShapeDtypeStruct((nb * l, SSM_W), BF16),
                   jax.ShapeDtypeStruct((nb * l, 2 * NST), BF16),
                   jax.ShapeDtypeStruct((nb * l, SSM_W), F32),
                   jax.ShapeDtypeStruct((nb * l, 2 * SSM_W), F32)],
        scratch_shapes=[pltpu.VMEM((tt, 2 * NST), F32), pltpu.VMEM((tt, 2 * NST), F32),
                        pltpu.VMEM((tt, SSM_W), F32), pltpu.VMEM((tt, SSM_W), F32),
                        pltpu.VMEM((SUBLANES, 2 * NST), F32)],
        compiler_params=_params("arbitrary"), name="s5_fwd")(proj, b_m, c_m, cst, d_skip, w_glu)


def s5_bwd(proj, d_ycat, h, ypre, z, b_m, c_m, cst_rev, d_skip, w_glu, nb, l):
    tt = min(S5_CHUNK, l)
    nt = l // tt
    n_chunks = nb * nt

    def body(u_ref, dy_ref, h_ref, yp_ref, z_ref, b_ref, c_ref, cst_ref, ds_ref, wg_ref,
             du_ref, dwg_ref, dds_ref, dcc_ref, dbb_ref, dlam_ref,
             buf0, buf1, hb0, hb1, ub0, ub1, dyp0, dyp1, carry_ref, dc_ref, db_ref):
        s = pl.program_id(0)

        @pl.when(s == 0)
        def _():
            for r in (buf0, buf1, hb0, hb1, ub0, ub1, dyp0, dyp1, carry_ref, dc_ref, db_ref, dwg_ref, dds_ref, dlam_ref):
                r[...] = jnp.zeros_like(r)

        def step(p_buf, p_h, p_u, p_dyp, q_buf, q_h):
            g = p_buf[...].astype(BF16)
            ub_done = p_u[...].astype(BF16)
            du = []
            for m in range(N_BLK):
                g_m = _state_block(g, m)
                db_ref[m] += _dot_tn(g_m, ub_done[:, m * BLK_CH:(m + 1) * BLK_CH])
                du.append(_dot_nt(g_m, b_ref[m]))
            du_ref[...] = (jnp.concatenate(du, axis=1) + ds_ref[...] * p_dyp[...]).astype(BF16)
            u = u_ref[...].astype(F32)
            ypre = yp_ref[...]
            z = z_ref[...]
            z1 = z[:, :SSM_W]
            sg = jax.nn.sigmoid(z[:, SSM_W:])
            dy = dy_ref[...].astype(F32) * jnp.where(s < n_chunks, 1.0, 0.0)
            dz = jnp.concatenate([dy * sg, dy * z1 * sg * (1.0 - sg)], axis=1).astype(BF16)
            yg, gelu_vjp = jax.vjp(jax.nn.gelu, ypre)
            dwg_ref[...] += _dot_tn(yg.astype(BF16), dz)
            dypre = gelu_vjp(_dot_nt(dz, wg_ref[...]))[0]
            dds_ref[...] += jnp.sum(dypre * u, axis=0, keepdims=True)
            dyb = dypre.astype(BF16)
            hb = h_ref[...]
            p_h[...] = hb.astype(F32)
            p_u[...] = u
            p_dyp[...] = dypre
            for m in range(N_BLK):
                dy_m = dyb[:, m * BLK_CH:(m + 1) * BLK_CH]
                dc_ref[m] += _dot_tn(_state_block(hb, m), dy_m)
                _put_state_block(p_buf, m, _dot_nt(dy_m, c_ref[m]))
            _scan_tiles(q_buf, cst_ref, carry_ref, tt, True, lax.rem(s + nt - 1, nt) == 0, h_ref=q_h, acc_ref=dlam_ref)

        @pl.when(lax.rem(s, 2) == 0)
        def _():
            step(buf0, hb0, ub0, dyp0, buf1, hb1)

        @pl.when(lax.rem(s, 2) == 1)
        def _():
            step(buf1, hb1, ub1, dyp1, buf0, hb0)

        @pl.when(s == n_chunks + 1)
        def _():
            dlam_ref[...] = jnp.broadcast_to(jnp.sum(dlam_ref[...], axis=0, keepdims=True), dlam_ref.shape)
            for acc_ref, out_ref in ((dc_ref, dcc_ref), (db_ref, dbb_ref)):
                for m in range(N_BLK):
                    for ri in range(2):
                        for gl in range(BLK_GROUPS):
                            out_ref[ri, pl.ds((m * BLK_GROUPS + gl) * SSM_N, SSM_N), :] = acc_ref[
                                m, pl.ds(ri * BLK_ST + gl * SSM_N, SSM_N), pl.ds(gl * SSM_GC, SSM_GC)]

    def chunk_rows(o):
        return lax.div(o, nt) * nt + (nt - 1 - lax.rem(o, nt))

    def rrow(col):
        return lambda s: (chunk_rows(jnp.minimum(s, n_chunks - 1)), col)

    const = lambda s: (0, 0)
    const3 = lambda s: (0, 0, 0)
    state_buf = pltpu.VMEM((tt, 2 * NST), F32)
    chan_buf = pltpu.VMEM((tt, SSM_W), F32)
    return pl.pallas_call(
        body, grid=(n_chunks + 2,),
        in_specs=[pl.BlockSpec((tt, SSM_W), rrow(1)),
                  pl.BlockSpec((tt, SSM_W), rrow(1)),
                  pl.BlockSpec((tt, 2 * NST), rrow(0)),
                  pl.BlockSpec((tt, SSM_W), rrow(0)),
                  pl.BlockSpec((tt, 2 * SSM_W), rrow(0)),
                  pl.BlockSpec((N_BLK, BLK_CH, 2 * BLK_ST), const3),
                  pl.BlockSpec((N_BLK, 2 * BLK_ST, BLK_CH), const3),
                  pl.BlockSpec((8 * SUBLANES, NST), const),
                  pl.BlockSpec((1, SSM_W), const),
                  pl.BlockSpec((SSM_W, 2 * SSM_W), const)],
        out_specs=[pl.BlockSpec((tt, SSM_W), lambda s: (chunk_rows(jnp.maximum(s - 2, 0)), 0)),
                   pl.BlockSpec((SSM_W, 2 * SSM_W), const),
                   pl.BlockSpec((1, SSM_W), const),
                   pl.BlockSpec((2, NST, SSM_GC), const3),
                   pl.BlockSpec((2, NST, SSM_GC), const3),
                   pl.BlockSpec((SUBLANES, 2 * NST), const)],
        out_shape=[jax.ShapeDtypeStruct((nb * l, SSM_W), BF16),
                   jax.ShapeDtypeStruct((SSM_W, 2 * SSM_W), F32),
                   jax.ShapeDtypeStruct((1, SSM_W), F32),
                   jax.ShapeDtypeStruct((2, NST, SSM_GC), F32),
                   jax.ShapeDtypeStruct((2, NST, SSM_GC), F32),
                   jax.ShapeDtypeStruct((SUBLANES, 2 * NST), F32)],
        scratch_shapes=[state_buf, state_buf, state_buf, state_buf, chan_buf, chan_buf, chan_buf, chan_buf,
                        pltpu.VMEM((SUBLANES, 2 * NST), F32),
                        pltpu.VMEM((N_BLK, 2 * BLK_ST, BLK_CH), F32), pltpu.VMEM((N_BLK, 2 * BLK_ST, BLK_CH), F32)],
        compiler_params=_params("arbitrary"), name="s5_bwd")(
            proj, d_ycat, h, ypre, z, b_m, c_m, cst_rev, d_skip, w_glu)


def _head_mask(hh):
    lane = lax.broadcasted_iota(jnp.int32, (1, ATT_W), 1)
    return (lane >= hh * MEM_HD) & (lane < (hh + 1) * MEM_HD)


def _mem_kv(mem_ref, gm_ref, wkv_ref):
    m = mem_ref[0]
    mh = m * _rms_scale(m)
    mb = (mh * gm_ref[...]).astype(BF16)
    kv = _dot(mb, wkv_ref[...])
    return mh, mb, kv[:, :ATT_W].astype(BF16), kv[:, ATT_W:].astype(BF16)


def _stack_heads(dst_ref, a):
    for hh in range(MEM_HEADS):
        dst_ref[pl.ds(hh * N_MEM, N_MEM), :] = jnp.where(_head_mask(hh), a, jnp.zeros_like(a))


def _fold_heads(a):
    out = jnp.zeros((N_MEM, ATT_W), a.dtype)
    for hh in range(MEM_HEADS):
        out = out + jnp.where(_head_mask(hh), a[hh * N_MEM:(hh + 1) * N_MEM], 0.0)
    return out


def _softmax_heads(s):
    parts = []
    for hh in range(MEM_HEADS):
        sh = s[:, hh * N_MEM:(hh + 1) * N_MEM]
        e = jnp.exp(sh - jnp.max(sh, axis=-1, keepdims=True))
        parts.append(e / jnp.sum(e, axis=-1, keepdims=True))
    return jnp.concatenate(parts, axis=1)


def attn_fwd(proj, mem, g_mem, w_kv, nb, l):
    tq = min(ATTN_ROWS, l)
    nq = l // tq
    scale = MEM_HD ** -0.5

    def body(q_ref, mem_ref, gm_ref, wkv_ref, o_ref, k_s, v_s):
        @pl.when(pl.program_id(1) == 0)
        def _():
            _, _, k, v = _mem_kv(mem_ref, gm_ref, wkv_ref)
            _stack_heads(k_s, k)
            _stack_heads(v_s, v)

        p = _softmax_heads(_dot_nt(q_ref[...], k_s[...]) * scale)
        o_ref[...] = _dot(p.astype(BF16), v_s[...]).astype(BF16)

    const = lambda b, t: (0, 0)
    stacked = pltpu.VMEM((MEM_HEADS * N_MEM, ATT_W), BF16)
    return pl.pallas_call(
        body, grid=(nb, nq),
        in_specs=[pl.BlockSpec((tq, ATT_W), lambda b, t: (b * nq + t, 3)),
                  pl.BlockSpec((1, N_MEM, D_MODEL), lambda b, t: (b, 0, 0)),
                  pl.BlockSpec((1, D_MODEL), const),
                  pl.BlockSpec((D_MODEL, 2 * ATT_W), const)],
        out_specs=pl.BlockSpec((tq, ATT_W), lambda b, t: (b * nq + t, 0)),
        out_shape=jax.ShapeDtypeStruct((nb * l, ATT_W), BF16),
        scratch_shapes=[stacked, stacked],
        compiler_params=_params("arbitrary", "arbitrary"), name="attn_fwd")(proj, mem, g_mem, w_kv)


def attn_bwd(proj, d_ycat, mem, g_mem, w_kv, nb, l):
    tq = min(ATTN_ROWS, l)
    nq = l // tq
    scale = MEM_HD ** -0.5

    def body(q_ref, do_ref, mem_ref, gm_ref, wkv_ref, dq_ref, dwkv_ref, dgm_ref, k_s, v_s, mb_s, dk_s, dv_s):
        b = pl.program_id(0)
        t = pl.program_id(1)

        @pl.when((b == 0) & (t == 0))
        def _():
            dwkv_ref[...] = jnp.zeros_like(dwkv_ref)
            dgm_ref[...] = jnp.zeros_like(dgm_ref)

        @pl.when(t == 0)
        def _():
            _, mb, k, v = _mem_kv(mem_ref, gm_ref, wkv_ref)
            _stack_heads(k_s, k)
            _stack_heads(v_s, v)
            mb_s[...] = mb
            dk_s[...] = jnp.zeros_like(dk_s)
            dv_s[...] = jnp.zeros_like(dv_s)

        q = q_ref[...]
        do = do_ref[...]
        k = k_s[...]
        p = _softmax_heads(_dot_nt(q, k) * scale)
        dp = _dot_nt(do, v_s[...])
        ds = []
        for hh in range(MEM_HEADS):
            blk = slice(hh * N_MEM, (hh + 1) * N_MEM)
            ds.append(p[:, blk] * (dp[:, blk] - jnp.sum(dp[:, blk] * p[:, blk], axis=-1, keepdims=True)) * scale)
        ds = jnp.concatenate(ds, axis=1).astype(BF16)
        dq_ref[...] = _dot(ds, k).astype(BF16)
        dk_s[...] += _dot_tn(ds, q)
        dv_s[...] += _dot_tn(p.astype(BF16), do)

        @pl.when(t == nq - 1)
        def _():
            dkv = jnp.concatenate([_fold_heads(dk_s[...]), _fold_heads(dv_s[...])], axis=1).astype(BF16)
            dwkv_ref[...] += _dot_tn(mb_s[...], dkv)
            m = mem_ref[0]
            dgm_ref[...] += jnp.sum(_dot_nt(dkv, wkv_ref[...]) * (m * _rms_scale(m)), axis=0, keepdims=True)

    const = lambda b, t: (0, 0)
    return pl.pallas_call(
        body, grid=(nb, nq),
        in_specs=[pl.BlockSpec((tq, ATT_W), lambda b, t: (b * nq + t, 3)),
                  pl.BlockSpec((tq, ATT_W), lambda b, t: (b * nq + t, 3)),
                  pl.BlockSpec((1, N_MEM, D_MODEL), lambda b, t: (b, 0, 0)),
                  pl.BlockSpec((1, D_MODEL), const),
                  pl.BlockSpec((D_MODEL, 2 * ATT_W), const)],
        out_specs=[pl.BlockSpec((tq, ATT_W), lambda b, t: (b * nq + t, 0)),
                   pl.BlockSpec((D_MODEL, 2 * ATT_W), const),
                   pl.BlockSpec((1, D_MODEL), const)],
        out_shape=[jax.ShapeDtypeStruct((nb * l, ATT_W), BF16),
                   jax.ShapeDtypeStruct((D_MODEL, 2 * ATT_W), F32),
                   jax.ShapeDtypeStruct((1, D_MODEL), F32)],
        scratch_shapes=[pltpu.VMEM((MEM_HEADS * N_MEM, ATT_W), BF16), pltpu.VMEM((MEM_HEADS * N_MEM, ATT_W), BF16),
                        pltpu.VMEM((N_MEM, D_MODEL), BF16),
                        pltpu.VMEM((MEM_HEADS * N_MEM, ATT_W), F32), pltpu.VMEM((MEM_HEADS * N_MEM, ATT_W), F32)],
        compiler_params=_params("arbitrary", "arbitrary"), name="attn_bwd")(proj, d_ycat, mem, g_mem, w_kv)


def tail(y_pool, y_ssm, y_att, proj, x2, target, w_out, g_post):
    t = x2.shape[0]
    tm = min(ROW_TILE, t)

    def body(yp_ref, ys_ref, ya_ref, gate_ref, x_ref, tg_ref, w_ref, gp_ref,
             dz_ref, dyc_ref, dgate_ref, dw_ref, dgp_ref, loss_ref):
        @pl.when(pl.program_id(0) == 0)
        def _():
            dw_ref[...] = jnp.zeros_like(dw_ref)
            dgp_ref[...] = jnp.zeros_like(dgp_ref)
            loss_ref[...] = jnp.zeros_like(loss_ref)

        ycat = jnp.concatenate([yp_ref[...], ys_ref[...], ya_ref[...]], axis=1).astype(F32)
        gate = gate_ref[...].astype(F32)
        sg = jax.nn.sigmoid(gate)
        silu = gate * sg
        yb = (ycat * silu).astype(BF16)
        w = w_ref[...]
        out = _dot(yb, w)
        r2 = _rms_scale(out)
        oh = out * r2
        gp = gp_ref[...]
        err = (x_ref[...] + oh * gp) - tg_ref[...]
        loss_ref[...] += 0.5 * jnp.sum(jnp.mean(err * err, axis=-1, keepdims=True), axis=0, keepdims=True)
        dz = err * (1.0 / D_MODEL)
        dz_ref[...] = dz.astype(BF16)
        dgp_ref[...] += jnp.sum(dz * oh, axis=0, keepdims=True)
        dn = dz * gp
        dout = (r2 * (dn - oh * jnp.mean(dn * oh, axis=-1, keepdims=True))).astype(BF16)
        dw_ref[...] += _dot_tn(yb, dout)
        dy = _dot_nt(dout, w)
        dyc_ref[...] = (dy * silu).astype(BF16)
        dgate_ref[...] = (dy * ycat * (sg * (1.0 + gate * (1.0 - sg)))).astype(BF16)

    row = lambda i: (i, 0)
    const = lambda i: (0, 0)
    full = jax.ShapeDtypeStruct((t, D_MODEL), BF16)
    return pl.pallas_call(
        body, grid=(t // tm,),
        in_specs=[pl.BlockSpec((tm, POOL_W), row),
                  pl.BlockSpec((tm, SSM_W), row),
                  pl.BlockSpec((tm, ATT_W), row),
                  pl.BlockSpec((tm, D_MODEL), lambda i: (i, 1)),
                  pl.BlockSpec((tm, D_MODEL), row),
                  pl.BlockSpec((tm, D_MODEL), row),
                  pl.BlockSpec((D_MODEL, D_MODEL), const),
                  pl.BlockSpec((1, D_MODEL), const)],
        out_specs=[pl.BlockSpec((tm, D_MODEL), row),
                   pl.BlockSpec((tm, D_MODEL), row),
                   pl.BlockSpec((tm, D_MODEL), row),
                   pl.BlockSpec((D_MODEL, D_MODEL), const),
                   pl.BlockSpec((1, D_MODEL), const),
                   pl.BlockSpec((1, LANES), const)],
        out_shape=[full, full, full,
                   jax.ShapeDtypeStruct((D_MODEL, D_MODEL), F32),
                   jax.ShapeDtypeStruct((1, D_MODEL), F32),
                   jax.ShapeDtypeStruct((1, LANES), F32)],
        compiler_params=_params("arbitrary"), name="tail")(y_pool, y_ssm, y_att, proj, x2, target, w_out, g_post)


def in_proj_bwd(du_pool, du_ssm, dq, d_gate, x2, dz, g_pre, w_in):
    t = x2.shape[0]
    tm = min(ROW_TILE, t)
    shard = W_IN_SHARD

    def body(dup_ref, dus_ref, dq_ref, dg_ref, x_ref, dz_ref, g_ref, w_ref, gx_ref, dw_ref, dgp_ref):
        @pl.when(pl.program_id(0) == 0)
        def _():
            dw_ref[...] = jnp.zeros_like(dw_ref)
            dgp_ref[...] = jnp.zeros_like(dgp_ref)

        dproj = jnp.concatenate([dup_ref[...], dus_ref[...], dq_ref[...], dg_ref[...]], axis=1)
        x = x_ref[...]
        r1 = _rms_scale(x)
        xh = x * r1
        g = g_ref[...]
        hb = (xh * g).astype(BF16)
        dh = jnp.zeros((tm, D_MODEL), F32)
        for k in range(N_CHIPS):
            dp_k = dproj[:, k * shard:(k + 1) * shard]
            dw_ref[k] += _dot_tn(hb, dp_k)
            dh = dh + _dot_nt(dp_k, w_ref[k])
        dgp_ref[...] += jnp.sum(dh * xh, axis=0, keepdims=True)
        dn = dh * g
        gx_ref[...] = dz_ref[...].astype(F32) + r1 * (dn - xh * jnp.mean(dn * xh, axis=-1, keepdims=True))

    row = lambda i: (i, 0)
    const = lambda i: (0, 0)
    return pl.pallas_call(
        body, grid=(t // tm,),
        in_specs=[pl.BlockSpec((tm, POOL_W), row),
                  pl.BlockSpec((tm, SSM_W), row),
                  pl.BlockSpec((tm, ATT_W), row),
                  pl.BlockSpec((tm, D_MODEL), row),
                  pl.BlockSpec((tm, D_MODEL), row),
                  pl.BlockSpec((tm, D_MODEL), row),
                  pl.BlockSpec((1, D_MODEL), const),
                  pl.BlockSpec((N_CHIPS, D_MODEL, shard), lambda i: (0, 0, 0))],
        out_specs=[pl.BlockSpec((tm, D_MODEL), row),
                   pl.BlockSpec((N_CHIPS, D_MODEL, shard), lambda i: (0, 0, 0)),
                   pl.BlockSpec((1, D_MODEL), const)],
        out_shape=[jax.ShapeDtypeStruct((t, D_MODEL), F32),
                   jax.ShapeDtypeStruct((N_CHIPS, D_MODEL, shard), F32),
                   jax.ShapeDtypeStruct((1, D_MODEL), F32)],
        compiler_params=_params("arbitrary"), name="in_proj_bwd")(du_pool, du_ssm, dq, d_gate, x2, dz, g_pre, w_in)


def _block_diag(blocks):
    g, r, c = blocks.shape
    eye = jnp.eye(g, dtype=blocks.dtype)
    return jnp.einsum("grc,gh->grhc", blocks, eye).reshape(g * r, g * c)


def _block_diag_extract(mat, g, r, c):
    eye = jnp.eye(g, dtype=mat.dtype)
    return jnp.einsum("grhc,gh->grc", mat.reshape(g, r, g, c), eye)


def local_step(x, mem, target, g_pre, w_in_shard, w_pool, pool_scale, a_re, a_im, log_dt, b_re, b_im, c_re, c_im,
               d_skip, later_shards, g_mem, g_post):
    nb, l, _ = x.shape
    x2 = x.reshape(nb * l, D_MODEL)
    tg2 = target.reshape(nb * l, D_MODEL)

    rowv = lambda a: a.reshape(1, NST)
    lanes_last = lambda b: b.transpose(2, 0, 1).reshape(SSM_GC, NST)
    ldt_row = rowv(jnp.broadcast_to(log_dt.reshape(SSM_NG, 1), (SSM_NG, SSM_N)))
    b_re_t = lanes_last(b_re)
    b_im_t = lanes_last(b_im)
    cst_f, cst_b = s5_scan_consts(rowv(a_re), rowv(a_im), ldt_row)
    bbr, bbi = s5_bbar(rowv(a_re), rowv(a_im), ldt_row, b_re_t, b_im_t)
    eye = jnp.eye(BLK_GROUPS, dtype=F32)
    blk_in = lambda bb: jnp.einsum("cmgn,gh->mhcgn", bb.reshape(SSM_GC, N_BLK, BLK_GROUPS, SSM_N), eye).reshape(
        N_BLK, BLK_CH, BLK_ST)
    blk_out = lambda cc: jnp.einsum("mgcn,gh->mgnhc", cc.reshape(N_BLK, BLK_GROUPS, SSM_GC, SSM_N), eye).reshape(
        N_BLK, BLK_ST, BLK_CH)
    b_m = jnp.concatenate([blk_in(bbr), blk_in(bbi)], axis=2).astype(BF16)
    c_m = jnp.concatenate([blk_out(c_re), -blk_out(c_im)], axis=1).astype(BF16)
    w_pool_blk = _block_diag(w_pool.reshape(4, POOL_GW, POOL_GW)).astype(BF16)

    proj, w_in, (g_glu, g_kv, g_out) = in_proj(x2, g_pre, w_in_shard, later_shards)
    w_glu = g_glu.transpose(2, 0, 1).reshape(SSM_W, 2 * SSM_W)
    w_kv = g_kv.reshape(D_MODEL, 2 * ATT_W)
    w_out = g_out.reshape(D_MODEL, D_MODEL)
    y_pool = pool_fwd(proj, w_pool_blk, pool_scale, nb, l)
    y_ssm, h, ypre, z = s5_fwd(proj, b_m, c_m, cst_f, d_skip, w_glu, nb, l)
    y_att = attn_fwd(proj, mem, g_mem, w_kv, nb, l)
    dz, d_ycat, d_gate, d_w_out, d_g_post, loss = tail(y_pool, y_ssm, y_att, proj, x2, tg2, w_out, g_post)
    du_pool, d_wp_blk, d_pool_scale = pool_bwd(proj, d_ycat, w_pool_blk, pool_scale, nb, l)
    du_ssm, d_w_glu, d_d_skip, d_cc, d_bb, d_lam = s5_bwd(
        proj, d_ycat, h, ypre, z, b_m, c_m, cst_b, d_skip, w_glu, nb, l)
    dq, d_w_kv, d_g_mem = attn_bwd(proj, d_ycat, mem, g_mem, w_kv, nb, l)
    grad_x, d_w_in, d_g_pre = in_proj_bwd(du_pool, du_ssm, dq, d_gate, x2, dz, g_pre, w_in)

    d_c = d_cc[0].reshape(SSM_NG, SSM_N, SSM_GC)
    d_ci = d_cc[1].reshape(SSM_NG, SSM_N, SSM_GC)
    d_lam_row = d_lam[0]
    d_ar, d_ai, d_ld, d_br, d_bi = s5_param_bwd(
        rowv(a_re), rowv(a_im), ldt_row, b_re_t, b_im_t,
        d_lam_row[:NST].reshape(1, NST), d_lam_row[NST:].reshape(1, NST), d_bb[0].T, d_bb[1].T)
    from_lanes_last = lambda b: b.reshape(SSM_GC, SSM_NG, SSM_N).transpose(1, 0, 2).reshape(1, SSM_NG, SSM_GC, SSM_N)

    grads = {
        "g_pre": d_g_pre,
        "w_in": d_w_in,
        "w_pool": _block_diag_extract(d_wp_blk, 4, POOL_GW, POOL_GW).reshape(1, 4, POOL_GW, POOL_GW),
        "pool_scale": d_pool_scale,
        "a_re": d_ar.reshape(1, SSM_NG, SSM_N),
        "a_im": d_ai.reshape(1, SSM_NG, SSM_N),
        "log_dt": d_ld[:, :SSM_NG],
        "b_re": from_lanes_last(d_br),
        "b_im": from_lanes_last(d_bi),
        "c_re": d_c.transpose(0, 2, 1).reshape(1, SSM_NG, SSM_GC, SSM_N),
        "c_im": (-d_ci).transpose(0, 2, 1).reshape(1, SSM_NG, SSM_GC, SSM_N),
        "d_skip": d_d_skip,
        "w_glu": d_w_glu,
        "g_mem": d_g_mem,
        "w_kv": d_w_kv,
        "w_out": d_w_out,
        "g_post": d_g_post,
    }
    return loss, grad_x.reshape(nb, l, D_MODEL), grads


VMEM_SPEC = pl.BlockSpec(memory_space=pltpu.VMEM)
ANY_SPEC = pl.BlockSpec(memory_space=pl.ANY)


def _place():
    return lax.axis_index("x"), lax.axis_index("y"), lax.axis_index("c")


def _other_chips(x, y):
    return [(1 - x, y), (x, 1 - y), (1 - x, 1 - y)]


SEMS_PER_ITEM = 5


def reduce_all(g4s, packs, pack_dtypes):
    n = len(g4s)
    dims = [g.shape[1:] for g in g4s]
    np_ = len(packs)
    prs = [p.shape[0] for p in packs]

    def body(*refs):
        g_refs, p_refs = refs[:n], refs[n:n + np_]
        outs, ops = refs[n + np_:2 * n + np_], refs[2 * n + np_:2 * (n + np_)]
        scr = refs[2 * (n + np_):]
        mine, theirs, sendb, recvb = scr[0:n], scr[n:2 * n], scr[2 * n:3 * n], scr[3 * n:4 * n]
        p_theirs, gats = scr[4 * n:4 * n + np_], scr[4 * n + np_:4 * n + 2 * np_]
        lsems, send_sems, recv_sems = scr[4 * n + 2 * np_:]
        x, y, c = _place()
        k = 2 * x + y
        chips = _other_chips(x, y)
        sib = (x, y, 1 - c)

        def halves(rows):
            hr = rows // 2
            return (pl.ds(pl.multiple_of(c * hr, SUBLANES), hr), pl.ds(pl.multiple_of((1 - c) * hr, SUBLANES), hr))

        def remote(src, dst, sem, to):
            return pltpu.make_async_remote_copy(
                src_ref=src, dst_ref=dst, send_sem=send_sems.at[sem], recv_sem=recv_sems.at[sem],
                device_id=to, device_id_type=MESH)

        loads, started = [], []
        for i in range(n):
            my_rows, sib_rows = halves(dims[i][0])
            ld = pltpu.make_async_copy(g_refs[i].at[:, my_rows, :], mine[i], lsems.at[i])
            ld.start()
            loads.append(ld)
            sw = remote(g_refs[i].at[:, sib_rows, :], theirs[i], SEMS_PER_ITEM * i, sib)
            sw.start()
            started.append(sw)
        p_swaps = []
        for q in range(np_):
            _, p_sib = halves(prs[q])
            sw = remote(p_refs[q].at[p_sib, :], p_theirs[q], SEMS_PER_ITEM * (n + q), sib)
            sw.start()
            p_swaps.append(sw)
        started += p_swaps

        for q in range(np_):
            p_my, _ = halves(prs[q])
            p_swaps[q].wait_recv()
            gats[q][k] = (p_refs[q][p_my, :] + p_theirs[q][...]).astype(gats[q].dtype)
            for j, (px, py) in enumerate(chips):
                cp = remote(gats[q].at[k], gats[q].at[k], SEMS_PER_ITEM * (n + q) + 1 + j, (px, py, c))
                cp.start()
                started.append(cp)
        ici = []
        for i in range(n):
            loads[i].wait()
            started[i].wait_recv()
            for j, (px, py) in enumerate(chips):
                kp = 2 * px + py
                sendb[i][j] = (mine[i][kp] + theirs[i][kp]).astype(BF16)
                cp = remote(sendb[i].at[j], recvb[i].at[j], SEMS_PER_ITEM * i + 1 + j, (px, py, c))
                cp.start()
                ici.append(cp)

        for q in range(np_):
            p_my, _ = halves(prs[q])
            gat, op = gats[q], ops[q]
            for j, (px, py) in enumerate(chips):
                kp = 2 * px + py
                remote(gat.at[kp], gat.at[kp], SEMS_PER_ITEM * (n + q) + 1 + j, (px, py, c)).wait_recv()
            op[p_my, :] = ((gat[0].astype(F32) + gat[1].astype(F32)) + gat[2].astype(F32)) + gat[3].astype(F32)
            last = remote(op.at[p_my, :], op.at[p_my, :], SEMS_PER_ITEM * (n + q) + 4, sib)
            last.start()
            started.append(last)
        for i in range(n):
            my_rows, _ = halves(dims[i][0])
            half = mine[i][k] + theirs[i][k]
            for j in range(3):
                ici[3 * i + j].wait_recv()
                half = half + recvb[i][j].astype(F32)
            outs[i][my_rows, :] = half
            last = remote(outs[i].at[my_rows, :], outs[i].at[my_rows, :], SEMS_PER_ITEM * i + 4, sib)
            last.start()
            started.append(last)
        for q in range(np_):
            _, p_sib = halves(prs[q])
            remote(ops[q].at[p_sib, :], ops[q].at[p_sib, :], SEMS_PER_ITEM * (n + q) + 4, sib).wait_recv()
        for i in range(n):
            _, sib_rows = halves(dims[i][0])
            remote(outs[i].at[sib_rows, :], outs[i].at[sib_rows, :], SEMS_PER_ITEM * i + 4, sib).wait_recv()
        for cp in started + ici:
            cp.wait_send()

    n_sems = SEMS_PER_ITEM * (n + np_)
    scratch = ([pltpu.VMEM((N_CHIPS, r // 2, cd), F32) for r, cd in dims] * 2
               + [pltpu.VMEM((3, r // 2, cd), BF16) for r, cd in dims] * 2
               + [pltpu.VMEM((pr // 2, LANES), F32) for pr in prs]
               + [pltpu.VMEM((N_CHIPS, pr // 2, LANES), dt) for pr, dt in zip(prs, pack_dtypes)]
               + [pltpu.SemaphoreType.DMA((n,)), pltpu.SemaphoreType.DMA((n_sems,)), pltpu.SemaphoreType.DMA((n_sems,))])
    res = pl.pallas_call(
        body,
        out_shape=[jax.ShapeDtypeStruct(d, F32) for d in dims] + [jax.ShapeDtypeStruct(p.shape, F32) for p in packs],
        in_specs=[ANY_SPEC] * n + [VMEM_SPEC] * np_, out_specs=[VMEM_SPEC] * (n + np_),
        scratch_shapes=scratch,
        compiler_params=pltpu.CompilerParams(vmem_limit_bytes=VMEM_LIMIT),
        name="reduce_all")(*g4s, *packs)
    return res[:n], res[n:]


def adamw_all(ws, gs, ms, vs):
    n = len(ws)
    chunked = [a.ndim == 3 and a.shape[0] == 1 and a.shape[1] % (ADAMW_STEPS * SUBLANES) == 0 for a in ws]

    def body(*refs):
        w, g, m, v = refs[:n], refs[n:2 * n], refs[2 * n:3 * n], refs[3 * n:4 * n]
        od, om, ov = refs[4 * n:5 * n], refs[5 * n:6 * n], refs[6 * n:]
        for i in range(n):
            if chunked[i]:
                od[i][...], om[i][...], ov[i][...] = _adamw(w[i][...], g[i][...], m[i][...], v[i][...])

        @pl.when(pl.program_id(0) == 0)
        def _():
            for i in range(n):
                if not chunked[i]:
                    od[i][...], om[i][...], ov[i][...] = _adamw(w[i][...], g[i][...], m[i][...], v[i][...])

    def spec(a, is_chunked):
        if is_chunked:
            return pl.BlockSpec((1, a.shape[1] // ADAMW_STEPS, a.shape[2]), lambda s: (0, s, 0))
        return pl.BlockSpec(a.shape, lambda s: (0,) * a.ndim)

    specs = [spec(a, c) for a, c in zip(ws, chunked)]
    shapes = [jax.ShapeDtypeStruct(a.shape, F32) for a in ws]
    res = pl.pallas_call(
        body, grid=(ADAMW_STEPS,), out_shape=shapes * 3, in_specs=specs * 4, out_specs=specs * 3,
        compiler_params=_params("arbitrary"), name="adamw_all")(*ws, *gs, *ms, *vs)
    return res[:n], res[n:2 * n], res[2 * n:]


WEIGHTS = ("g_pre", "w_in", "w_pool", "pool_scale", "a_re", "a_im", "log_dt", "b_re", "b_im", "c_re", "c_im", "d_skip",
           "w_glu", "g_mem", "w_kv", "w_out", "g_post")
SHARDED = ("w_in", "w_glu", "w_kv", "w_out")
SWAPPED = ("b_re", "b_im", "w_glu")
MATRICES = ("w_pool", "b_re", "b_im", "c_re", "c_im")
REPLICATED = tuple(n for n in WEIGHTS if n not in SHARDED)
PACK_TILE = SUBLANES * LANES


def _pack(arrays):
    parts = []
    for a in arrays:
        flat = a.reshape(-1)
        parts.append(jnp.pad(flat, (0, -flat.shape[0] % PACK_TILE)).reshape(-1, LANES))
    rows = sum(p.shape[0] for p in parts)
    if rows % (2 * SUBLANES):
        parts.append(jnp.zeros((SUBLANES, LANES), F32))
    return jnp.concatenate(parts, axis=0)


def _unpack(packed, shapes):
    out, row = [], 0
    for shp in shapes:
        size = math.prod(shp)
        rows = -(-size // PACK_TILE) * SUBLANES
        out.append(packed[row:row + rows].reshape(-1)[:size].reshape(shp))
        row += rows
    return out


def kernel(x, mem, g_pre, w_in, w_pool, pool_scale, a_re, a_im, log_dt, b_re, b_im, c_re, c_im, d_skip, w_glu, g_mem, w_kv, w_out, g_post, loss_target, m_g_pre, m_w_in, m_w_pool, m_pool_scale, m_a_re, m_a_im, m_log_dt, m_b_re, m_b_im, m_c_re, m_c_im, m_d_skip, m_w_glu, m_g_mem, m_w_kv, m_w_out, m_g_post, v_g_pre, v_w_in, v_w_pool, v_pool_scale, v_a_re, v_a_im, v_log_dt, v_b_re, v_b_im, v_c_re, v_c_im, v_d_skip, v_w_glu, v_g_mem, v_w_kv, v_w_out, v_g_post):
    given = dict(locals())
    wts = {n: given[n] for n in WEIGHTS}
    mom = {n: given["m_" + n] for n in WEIGHTS}
    var = {n: given["v_" + n] for n in WEIGHTS}

    swap = lambda a: jnp.swapaxes(a, -1, -2)
    view = lambda n, a: swap(a) if n in SWAPPED else a

    loss_part, grad_x, grads = local_step(
        x, mem, loss_target, g_pre, w_in[0], w_pool[0], pool_scale, a_re[0], a_im[0], log_dt[0], b_re[0], b_im[0],
        c_re[0], c_im[0], d_skip, [swap(w_glu)[0], w_kv[0], w_out[0]], g_mem, g_post)

    glu_cols = 2 * SSM_W // N_CHIPS
    partial4 = [
        grads["w_in"],
        grads["w_glu"].reshape(SSM_W, N_CHIPS, glu_cols).transpose(1, 2, 0),
        grads["w_kv"].reshape(N_CHIPS, D_MODEL // N_CHIPS, 2 * ATT_W),
        grads["w_out"].reshape(N_CHIPS, D_MODEL // N_CHIPS, D_MODEL),
    ]
    loss_tile = jnp.broadcast_to(loss_part, (SUBLANES, LANES))
    vectors = [n for n in REPLICATED if n not in MATRICES]
    sharded_g, (pack_v, pack_m) = reduce_all(
        partial4, [_pack([grads[n] for n in vectors] + [loss_tile]), _pack([grads[n] for n in MATRICES])], [F32, BF16])
    vec_g = _unpack(pack_v, [view(n, wts[n]).shape for n in vectors] + [(SUBLANES, LANES)])
    mat_g = _unpack(pack_m, [view(n, wts[n]).shape for n in MATRICES])
    loss = vec_g[-1][0, 0]
    grad = dict(zip(SHARDED, [g[None] for g in sharded_g]))
    grad.update(zip(vectors, vec_g[:-1]))
    grad.update(zip(MATRICES, mat_g))

    deltas, new_ms, new_vs = adamw_all([view(n, wts[n]) for n in WEIGHTS], [grad[n] for n in WEIGHTS],
                                       [view(n, mom[n]) for n in WEIGHTS], [view(n, var[n]) for n in WEIGHTS])
    back = lambda arrs: [view(n, a) for n, a in zip(WEIGHTS, arrs)]
    return (loss, grad_x, *back([grad[n] for n in WEIGHTS]), *back(deltas), *back(new_ms), *back(new_vs))
```

```python
import math

import jax
import jax.numpy as jnp
from jax import lax
from jax.experimental import pallas as pl
from jax.experimental.pallas import tpu as pltpu

F32 = jnp.float32
BF16 = jnp.bfloat16

D_MODEL = 1024
POOL_W = 384
SSM_W = 384
ATT_W = 256
POOL_GW = 96
POOL_PAD = 16
SSM_NG = 24
SSM_N = 64
SSM_GC = 16
N_CHIPS = 4
W_IN_SHARD = 2 * D_MODEL // N_CHIPS
NST = SSM_NG * SSM_N
BLK_CH = 128
BLK_GROUPS = BLK_CH // SSM_GC
BLK_ST = BLK_GROUPS * SSM_N
N_BLK = SSM_W // BLK_CH
N_MEM = 256
MEM_HEADS = 4
MEM_HD = 64
EPS = 1e-6

ADAM_LR = 0.001
ADAM_B1 = 0.9
ADAM_B2 = 0.999
ADAM_EPS = 1e-08
ADAM_WD = 0.01
ADAM_STEP = 10

SUBLANES = 8
LANES = 128
V7X_VMEM_BYTES = 64 * 2**20
VMEM_LIMIT = V7X_VMEM_BYTES - 8 * 2**20
SCAN_COLS = 512
ROW_TILE = 512
IN_PROJ_ROWS = 2048
ATTN_ROWS = 2048
S5_CHUNK = 256
ADAMW_STEPS = 4
MESH = pl.DeviceIdType.MESH

NT = (((1,), (1,)), ((), ()))
TN = (((0,), (0,)), ((), ()))


def _params(*sem):
    return pltpu.CompilerParams(dimension_semantics=sem, vmem_limit_bytes=VMEM_LIMIT)


def _dot(a, b):
    return jnp.dot(a, b, preferred_element_type=F32)


def _dot_nt(a, b):
    return lax.dot_general(a, b, NT, preferred_element_type=F32)


def _dot_tn(a, b):
    return lax.dot_general(a, b, TN, preferred_element_type=F32)


def _rms_scale(v):
    return lax.rsqrt(jnp.mean(v * v, axis=-1, keepdims=True) + EPS)


def _adamw(w, g, m, v):
    m = ADAM_B1 * m + (1.0 - ADAM_B1) * g
    v = ADAM_B2 * v + (1.0 - ADAM_B2) * (g * g)
    m_hat = m / (1.0 - ADAM_B1 ** ADAM_STEP)
    v_hat = v / (1.0 - ADAM_B2 ** ADAM_STEP)
    delta = -ADAM_LR * (m_hat / (jnp.sqrt(v_hat) + ADAM_EPS) + ADAM_WD * w)
    return delta, m, v


def _phase_shard(j):
    cx, cy = lax.axis_index("x"), lax.axis_index("y")
    others = [2 * px + py for px, py in _other_chips(cx, cy)]
    own = 2 * cx + cy
    return jnp.where(j == 0, own, jnp.where(j == 1, others[0], jnp.where(j == 2, others[1], others[2])))


def in_proj(x2, g_pre, w_in_shard, later_shards):
    t = x2.shape[0]
    tm = min(IN_PROJ_ROWS, t)
    n_tiles = t // tm
    n = len(later_shards)

    def body(x_ref, g_ref, ws_ref, *rest):
        shard_refs = rest[:n]
        o_ref, w_out_ref, gathered = rest[n], rest[n + 1], rest[n + 2:2 * n + 2]
        w_buf, h_buf = rest[2 * n + 2], rest[2 * n + 3]
        cast_bufs = rest[2 * n + 4:3 * n + 4]
        local_sems, send_sems, recv_sems = rest[3 * n + 4:]
        j = pl.program_id(0)
        i = pl.program_id(1)
        cx, cy, cc = _place()
        k = 2 * cx + cy
        chips = _other_chips(cx, cy)

        half = w_in_shard.shape[0] // 2

        def w_copy(p, block, core_half, forward):
            px, py = chips[p]
            rows = w_buf.at[block, pl.ds(pl.multiple_of(core_half * half, 2 * SUBLANES), half)]
            sem = 3 + p if forward else p
            return pltpu.make_async_remote_copy(
                src_ref=rows, dst_ref=rows, send_sem=send_sems.at[sem], recv_sem=recv_sems.at[sem],
                device_id=(cx, cy, 1 - cc) if forward else (px, py, cc), device_id_type=MESH)

        def own(m):
            return pltpu.make_async_copy(cast_bufs[m], gathered[m].at[k], local_sems.at[m])

        def later_copy(m, p, block, core_half, forward):
            px, py = chips[p]
            hr = later_shards[m].shape[0] // 2
            rows = pl.ds(pl.multiple_of(core_half * hr, 2 * SUBLANES), hr)
            dst = gathered[m].at[block, rows]
            sem = 6 + 6 * m + (3 + p if forward else p)
            return pltpu.make_async_remote_copy(
                src_ref=dst if forward else cast_bufs[m].at[rows], dst_ref=dst, send_sem=send_sems.at[sem],
                recv_sem=recv_sems.at[sem], device_id=(cx, cy, 1 - cc) if forward else (px, py, cc),
                device_id_type=MESH)

        @pl.when((j == 0) & (i == 0))
        def _():
            w_buf[k] = ws_ref[...].astype(BF16)
            for p in range(3):
                w_copy(p, k, cc, False).start()
            for m in range(n):
                cast_bufs[m][...] = shard_refs[m][...].astype(BF16)
                own(m).start()
                for p in range(3):
                    later_copy(m, p, k, cc, False).start()

        rows = pl.ds(pl.multiple_of(i * tm, tm), tm)

        @pl.when(j == 0)
        def _():
            x = x_ref[...]
            h_buf[rows, :] = (x * _rms_scale(x) * g_ref[...]).astype(BF16)

        for p in range(3):
            @pl.when((j == p) & (i == n_tiles - 1))
            def _(p=p):
                px, py = chips[p]
                w_copy(p, 2 * px + py, cc, False).wait_recv()
                w_copy(p, 2 * px + py, cc, True).start()

            @pl.when((j == p + 1) & (i == 0))
            def _(p=p):
                px, py = chips[p]
                w_copy(p, 2 * px + py, 1 - cc, True).wait_recv()

        o_ref[...] = _dot(h_buf[rows, :], w_buf[_phase_shard(j)]).astype(BF16)

        out_cp = pltpu.make_async_copy(w_buf, w_out_ref, local_sems.at[n])

        @pl.when((j == N_CHIPS - 1) & (i == 0))
        def _():
            out_cp.start()
            for m in range(n):
                for p in range(3):
                    px, py = chips[p]
                    later_copy(m, p, 2 * px + py, cc, False).wait_recv()
                    later_copy(m, p, 2 * px + py, cc, True).start()

        @pl.when((j == N_CHIPS - 1) & (i == n_tiles - 1))
        def _():
            for m in range(n):
                own(m).wait()
            for m in range(n):
                for p in range(3):
                    px, py = chips[p]
                    later_copy(m, p, 2 * px + py, 1 - cc, True).wait_recv()
            for p in range(3):
                px, py = chips[p]
                w_copy(p, k, cc, False).wait_send()
                w_copy(p, 2 * px + py, cc, True).wait_send()
            for m in range(n):
                for p in range(3):
                    px, py = chips[p]
                    later_copy(m, p, k, cc, False).wait_send()
                    later_copy(m, p, 2 * px + py, cc, True).wait_send()
            out_cp.wait()

    whole = lambda a: pl.BlockSpec(a.shape, lambda j, i: (0,) * a.ndim)
    res = pl.pallas_call(
        body, grid=(N_CHIPS, n_tiles),
        in_specs=[pl.BlockSpec((tm, D_MODEL), lambda j, i: (jnp.where(j == 0, i, n_tiles - 1), 0)),
                  pl.BlockSpec((1, D_MODEL), lambda j, i: (0, 0)),
                  whole(w_in_shard)] + [whole(s) for s in later_shards],
        out_specs=[pl.BlockSpec((tm, W_IN_SHARD), lambda j, i: (i, _phase_shard(j)))] + [ANY_SPEC] * (n + 1),
        out_shape=([jax.ShapeDtypeStruct((t, 2 * D_MODEL), BF16),
                    jax.ShapeDtypeStruct((N_CHIPS,) + w_in_shard.shape, BF16)]
                   + [jax.ShapeDtypeStruct((N_CHIPS,) + s.shape, BF16) for s in later_shards]),
        scratch_shapes=([pltpu.VMEM((N_CHIPS,) + w_in_shard.shape, BF16), pltpu.VMEM((t, D_MODEL), BF16)]
                        + [pltpu.VMEM(s.shape, BF16) for s in later_shards]
                        + [pltpu.SemaphoreType.DMA((n + 1,)), pltpu.SemaphoreType.DMA((6 * n + 6,)),
                           pltpu.SemaphoreType.DMA((6 * n + 6,))]),
        compiler_params=_params("arbitrary", "arbitrary"), name="in_proj")(x2, g_pre, w_in_shard, *later_shards)
    return res[0], res[1], res[2:]


def _pool_lane_window(shape):
    ch = lax.broadcasted_iota(jnp.int32, shape, 1)
    return jnp.where(ch < POOL_GW, 2.0, jnp.where(ch < 2 * POOL_GW, 4.0, jnp.where(ch < 3 * POOL_GW, 8.0, 16.0)))


def _pool_select(win, s2, s4, s8, s16):
    return jnp.where(win == 2.0, s2, jnp.where(win == 4.0, s4, jnp.where(win == 8.0, s8, s16)))


def _pool_div_count(v, win):
    pos = (lax.broadcasted_iota(jnp.int32, (POOL_PAD, POOL_W), 0) + 1).astype(F32)
    head = v[:POOL_PAD] / jnp.minimum(pos, win)
    return jnp.concatenate([head, v[POOL_PAD:] * (1.0 / win)], axis=0)


def _pool_diff(u, pad_ref, l):
    lo = POOL_PAD
    pad_ref[pl.ds(lo, l), :] = u
    s2 = u + pad_ref[pl.ds(lo - 1, l), :]
    pad_ref[pl.ds(lo, l), :] = s2
    s4 = s2 + pad_ref[pl.ds(lo - 2, l), :]
    pad_ref[pl.ds(lo, l), :] = s4
    s8 = s4 + pad_ref[pl.ds(lo - 4, l), :]
    pad_ref[pl.ds(lo, l), :] = s8
    s16 = s8 + pad_ref[pl.ds(lo - 8, l), :]
    win = _pool_lane_window((1, POOL_W))
    return _pool_div_count(_pool_select(win, s2, s4, s8, s16), win) - u, win


def pool_fwd(proj, w_blk, pool_scale, nb, l):
    def body(u_ref, w_ref, ps_ref, y_ref, d_ref, pad_ref):
        pad_ref[pl.ds(0, POOL_PAD), :] = jnp.zeros((POOL_PAD, POOL_W), F32)
        d, _ = _pool_diff(u_ref[...].astype(F32), pad_ref, l)
        db = d.astype(BF16)
        d_ref[...] = db
        y_ref[...] = (_dot(db, w_ref[...]) * ps_ref[...]).astype(BF16)

    rows = pl.BlockSpec((l, POOL_W), lambda b: (b, 0))
    act = jax.ShapeDtypeStruct((nb * l, POOL_W), BF16)
    return pl.pallas_call(
        body, grid=(nb,),
        in_specs=[rows,
                  pl.BlockSpec((POOL_W, POOL_W), lambda b: (0, 0)),
                  pl.BlockSpec((1, POOL_W), lambda b: (0, 0))],
        out_specs=[rows, rows],
        out_shape=[act, act],
        scratch_shapes=[pltpu.VMEM((POOL_PAD + l, POOL_W), F32)],
        compiler_params=_params("arbitrary"), name="pool_fwd")(proj, w_blk, pool_scale)


def pool_bwd(d_b, d_ycat, w_blk, pool_scale, nb, l):
    def body(d_ref, dy_ref, w_ref, ps_ref, du_ref, dw_ref, dps_ref, padb_ref):
        b = pl.program_id(0)

        @pl.when(b == 0)
        def _():
            dw_ref[...] = jnp.zeros_like(dw_ref)
            dps_ref[...] = jnp.zeros_like(dps_ref)

        padb_ref[pl.ds(l, POOL_PAD), :] = jnp.zeros((POOL_PAD, POOL_W), F32)
        win = _pool_lane_window((1, POOL_W))
        db = d_ref[...]
        w = w_ref[...]
        dy = dy_ref[...].astype(F32)
        dps_ref[...] += jnp.sum(dy * _dot(db, w), axis=0, keepdims=True)
        dyo = (dy * ps_ref[...]).astype(BF16)
        dw_ref[...] += _dot_tn(db, dyo)
        dd = _dot_nt(dyo, w)
        e = _pool_div_count(dd, win)
        padb_ref[pl.ds(0, l), :] = e
        f2 = e + padb_ref[pl.ds(1, l), :]
        padb_ref[pl.ds(0, l), :] = f2
        f4 = f2 + padb_ref[pl.ds(2, l), :]
        padb_ref[pl.ds(0, l), :] = f4
        f8 = f4 + padb_ref[pl.ds(4, l), :]
        padb_ref[pl.ds(0, l), :] = f8
        f16 = f8 + padb_ref[pl.ds(8, l), :]
        du_ref[...] = (_pool_select(win, f2, f4, f8, f16) - dd).astype(BF16)

    return pl.pallas_call(
        body, grid=(nb,),
        in_specs=[pl.BlockSpec((l, POOL_W), lambda b: (b, 0)),
                  pl.BlockSpec((l, POOL_W), lambda b: (b, 0)),
                  pl.BlockSpec((POOL_W, POOL_W), lambda b: (0, 0)),
                  pl.BlockSpec((1, POOL_W), lambda b: (0, 0))],
        out_specs=[pl.BlockSpec((l, POOL_W), lambda b: (b, 0)),
                   pl.BlockSpec((POOL_W, POOL_W), lambda b: (0, 0)),
                   pl.BlockSpec((1, POOL_W), lambda b: (0, 0))],
        out_shape=[jax.ShapeDtypeStruct((nb * l, POOL_W), BF16),
                   jax.ShapeDtypeStruct((POOL_W, POOL_W), F32),
                   jax.ShapeDtypeStruct((1, POOL_W), F32)],
        scratch_shapes=[pltpu.VMEM((POOL_PAD + l, POOL_W), F32)],
        compiler_params=_params("arbitrary"), name="pool_bwd")(d_b, d_ycat, w_blk, pool_scale)


def _discretise(a_re, a_im, ldt, b_re, b_im):
    dt = jnp.exp(ldt)
    mag = jnp.exp(a_re * dt)
    th = a_im * dt
    lr = mag * jnp.cos(th)
    li = mag * jnp.sin(th)
    nr = lr - 1.0
    den = a_re * a_re + a_im * a_im
    fr = (nr * a_re + li * a_im) / den
    fi = (li * a_re - nr * a_im) / den
    return lr, li, fr * b_re - fi * b_im, fr * b_im + fi * b_re


def _cmul(ar, ai, br, bi):
    return ar * br - ai * bi, ar * bi + ai * br


def s5_scan_consts(a_re_row, a_im_row, ldt_row):
    def body(ar_ref, ai_ref, ld_ref, f_ref, b_ref):
        dt = jnp.exp(ld_ref[...])
        mag = jnp.exp(ar_ref[...] * dt)
        th = ai_ref[...] * dt
        lr = mag * jnp.cos(th)
        li = mag * jnp.sin(th)
        r = lax.broadcasted_iota(jnp.int32, (SUBLANES, NST), 0)
        for out_ref, sign, rev in ((f_ref, 1.0, False), (b_ref, -1.0, True)):
            p = [(lr, sign * li)]
            for _ in range(SUBLANES - 1):
                p.append(_cmul(p[-1][0], p[-1][1], lr, sign * li))
            for idx, s in enumerate((1, 2, 4)):
                inside = (r < SUBLANES - s) if rev else (r >= s)
                out_ref[pl.ds(2 * idx * SUBLANES, SUBLANES), :] = jnp.where(inside, p[s - 1][0], 0.0)
                out_ref[pl.ds((2 * idx + 1) * SUBLANES, SUBLANES), :] = jnp.where(inside, p[s - 1][1], 0.0)
            pr = jnp.zeros((SUBLANES, NST), F32)
            pi = jnp.zeros((SUBLANES, NST), F32)
            for row in range(SUBLANES):
                power = (SUBLANES - row) if rev else (row + 1)
                pr = jnp.where(r == row, p[power - 1][0], pr)
                pi = jnp.where(r == row, p[power - 1][1], pi)
            out_ref[pl.ds(6 * SUBLANES, SUBLANES), :] = pr
            out_ref[pl.ds(7 * SUBLANES, SUBLANES), :] = pi

    shape = jax.ShapeDtypeStruct((8 * SUBLANES, NST), F32)
    return pl.pallas_call(body, out_shape=[shape, shape], name="s5_scan_consts")(a_re_row, a_im_row, ldt_row)


def s5_bbar(a_re_row, a_im_row, ldt_row, b_re_t, b_im_t):
    def body(ar, ai, ld, br, bi, o_re, o_im):
        _, _, bbr, bbi = _discretise(ar[...], ai[...], ld[...], br[...], bi[...])
        o_re[...] = bbr
        o_im[...] = bbi

    shape = jax.ShapeDtypeStruct((SSM_GC, NST), F32)
    return pl.pallas_call(body, out_shape=[shape, shape], name="s5_bbar")(a_re_row, a_im_row, ldt_row, b_re_t, b_im_t)


def s5_param_bwd(a_re_row, a_im_row, ldt_row, b_re_t, b_im_t, d_lr, d_li, d_bbr, d_bbi):
    def body(ar, ai, ld, br, bi, g_lr, g_li, g_br, g_bi, o_ar, o_ai, o_ld, o_br, o_bi):
        _, vjp = jax.vjp(_discretise, ar[...], ai[...], ld[...], br[...], bi[...])
        d_ar, d_ai, d_ld, d_br, d_bi = vjp((g_lr[...], g_li[...], g_br[...], g_bi[...]))
        o_ar[...] = d_ar
        o_ai[...] = d_ai
        state = lax.broadcasted_iota(jnp.int32, (NST, LANES), 0)
        lane = lax.broadcasted_iota(jnp.int32, (NST, LANES), 1)
        in_group = jnp.where((state >= lane * SSM_N) & (state < (lane + 1) * SSM_N), 1.0, 0.0)
        o_ld[...] = jnp.dot(d_ld, in_group, precision=lax.Precision.HIGHEST, preferred_element_type=F32)
        o_br[...] = d_br
        o_bi[...] = d_bi

    row = jax.ShapeDtypeStruct((1, NST), F32)
    mat = jax.ShapeDtypeStruct((SSM_GC, NST), F32)
    grp = jax.ShapeDtypeStruct((1, LANES), F32)
    return pl.pallas_call(body, out_shape=[row, row, grp, mat, mat], name="s5_param_bwd")(
        a_re_row, a_im_row, ldt_row, b_re_t, b_im_t, d_lr, d_li, d_bbr, d_bbi)


def _scan_tiles(buf_ref, cst_ref, carry_ref, rows, reverse, reset, h_ref=None, acc_ref=None):
    n_tiles = rows // SUBLANES
    shifts = ((1, 0), (2, 2), (4, 4))
    row_id = lax.broadcasted_iota(jnp.int32, (SUBLANES, SCAN_COLS), 0)
    cols = [(j * SCAN_COLS, NST + j * SCAN_COLS) for j in range(NST // SCAN_COLS)]
    edge = 0 if reverse else SUBLANES - 1
    carry = jnp.where(reset, 0.0, carry_ref[...])
    accs = [(acc_ref[:, pl.ds(o_re, SCAN_COLS)], acc_ref[:, pl.ds(o_im, SCAN_COLS)]) for o_re, o_im in cols] if reverse else None
    for i in range(n_tiles):
        start = ((n_tiles - 1 - i) if reverse else i) * SUBLANES
        rws = pl.ds(start, SUBLANES)
        for j, (o_re, o_im) in enumerate(cols):
            c_re = pl.ds(o_re, SCAN_COLS)
            c_im = pl.ds(o_im, SCAN_COLS)
            xr = buf_ref[rws, c_re]
            xi = buf_ref[rws, c_im]
            for s, base in shifts:
                amount = (SUBLANES - s) if reverse else s
                sr = pltpu.roll(xr, amount, 0)
                si = pltpu.roll(xi, amount, 0)
                mr = cst_ref[pl.ds(base * SUBLANES, SUBLANES), c_re]
                mi = cst_ref[pl.ds((base + 1) * SUBLANES, SUBLANES), c_re]
                xr, xi = xr + (mr * sr - mi * si), xi + (mr * si + mi * sr)
            pr = cst_ref[pl.ds(6 * SUBLANES, SUBLANES), c_re]
            pi = cst_ref[pl.ds(7 * SUBLANES, SUBLANES), c_re]
            cr = carry[:, o_re:o_re + SCAN_COLS]
            ci = carry[:, o_im:o_im + SCAN_COLS]
            xr, xi = xr + (pr * cr - pi * ci), xi + (pr * ci + pi * cr)
            buf_ref[rws, c_re] = xr
            buf_ref[rws, c_im] = xi
            if reverse:
                gnr = jnp.where(row_id == SUBLANES - 1, cr, pltpu.roll(xr, SUBLANES - 1, 0))
                gni = jnp.where(row_id == SUBLANES - 1, ci, pltpu.roll(xi, SUBLANES - 1, 0))
                hr = h_ref[rws, c_re]
                hi = h_ref[rws, c_im]
                accs[j] = (accs[j][0] + (gnr * hr + gni * hi), accs[j][1] + (gni * hr - gnr * hi))
        carry = jnp.broadcast_to(buf_ref[pl.ds(start + edge, 1), :], (SUBLANES, 2 * NST))
    carry_ref[...] = carry
    if reverse:
        for (o_re, o_im), (a_re_, a_im_) in zip(cols, accs):
            acc_ref[:, pl.ds(o_re, SCAN_COLS)] = a_re_
            acc_ref[:, pl.ds(o_im, SCAN_COLS)] = a_im_


def _state_block(v, m):
    return jnp.concatenate([v[:, m * BLK_ST:(m + 1) * BLK_ST], v[:, NST + m * BLK_ST:NST + (m + 1) * BLK_ST]], axis=1)


def _put_state_block(buf_ref, m, val):
    buf_ref[:, pl.ds(m * BLK_ST, BLK_ST)] = val[:, :BLK_ST]
    buf_ref[:, pl.ds(NST + m * BLK_ST, BLK_ST)] = val[:, BLK_ST:]


def s5_fwd(proj, b_m, c_m, cst, d_skip, w_glu, nb, l):
    tt = min(S5_CHUNK, l)
    nt = l // tt
    n_chunks = nb * nt

    def body(u_ref, b_ref, c_ref, cst_ref, ds_ref, wg_ref, y_ref, h_ref, yp_ref, z_ref,
             buf0, buf1, ub0, ub1, carry_ref):
        s = pl.program_id(0)

        @pl.when(s == 0)
        def _():
            for r in (buf0, buf1, ub0, ub1, carry_ref):
                r[...] = jnp.zeros_like(r)

        def step(p_buf, p_u, q_buf):
            h = p_buf[...].astype(BF16)
            h_ref[...] = h
            ypre = (jnp.concatenate([_dot(_state_block(h, m), c_ref[m]) for m in range(N_BLK)], axis=1)
                    + ds_ref[...] * p_u[...])
            yp_ref[...] = ypre
            z = _dot(jax.nn.gelu(ypre).astype(BF16), wg_ref[...])
            z_ref[...] = z
            y_ref[...] = (z[:, :SSM_W] * jax.nn.sigmoid(z[:, SSM_W:])).astype(BF16)
            ub = u_ref[...]
            p_u[...] = ub.astype(F32)
            for m in range(N_BLK):
                _put_state_block(p_buf, m, _dot(ub[:, m * BLK_CH:(m + 1) * BLK_CH], b_ref[m]))
            _scan_tiles(q_buf, cst_ref, carry_ref, tt, False, lax.rem(s + nt - 1, nt) == 0)

        @pl.when(lax.rem(s, 2) == 0)
        def _():
            step(buf0, ub0, buf1)

        @pl.when(lax.rem(s, 2) == 1)
        def _():
            step(buf1, ub1, buf0)

    row_in = lambda s: (jnp.minimum(s, n_chunks - 1), 1)
    row_out = lambda s: (jnp.maximum(s - 2, 0), 0)
    const = lambda s: (0, 0)
    const3 = lambda s: (0, 0, 0)
    return pl.pallas_call(
        body, grid=(n_chunks + 2,),
        in_specs=[pl.BlockSpec((tt, SSM_W), row_in),
                  pl.BlockSpec((N_BLK, BLK_CH, 2 * BLK_ST), const3),
                  pl.BlockSpec((N_BLK, 2 * BLK_ST, BLK_CH), const3),
                  pl.BlockSpec((8 * SUBLANES, NST), const),
                  pl.BlockSpec((1, SSM_W), const),
                  pl.BlockSpec((SSM_W, 2 * SSM_W), const)],
        out_specs=[pl.BlockSpec((tt, SSM_W), row_out),
                   pl.BlockSpec((tt, 2 * NST), row_out),
                   pl.BlockSpec((tt, SSM_W), row_out),
                   pl.BlockSpec((tt, 2 * SSM_W), row_out)],
        out_shape=[jax.ShapeDtypeStruct((nb * l, SSM_W), BF16),
                   jax.ShapeDtypeStruct((nb * l, 2 * NST), BF16),
                   jax.ShapeDtypeStruct((nb * l, SSM_W), F32),
                   jax.ShapeDtypeStruct((nb * l, 2 * SSM_W), F32)],
        scratch_shapes=[pltpu.VMEM((tt, 2 * NST), F32), pltpu.VMEM((tt, 2 * NST), F32),
                        pltpu.VMEM((tt, SSM_W), F32), pltpu.VMEM((tt, SSM_W), F32),
                        pltpu.VMEM((SUBLANES, 2 * NST), F32)],
        compiler_params=_params("arbitrary"), name="s5_fwd")(proj, b_m, c_m, cst, d_skip, w_glu)


def s5_bwd(proj, d_ycat, h, ypre, z, b_m, c_m, cst_rev, d_skip, w_glu, nb, l):
    tt = min(S5_CHUNK, l)
    nt = l // tt
    n_chunks = nb * nt

    def body(u_ref, dy_ref, h_ref, yp_ref, z_ref, b_ref, c_ref, cst_ref, ds_ref, wg_ref,
             du_ref, dwg_ref, dds_ref, dcc_ref, dbb_ref, dlam_ref,
             buf0, buf1, hb0, hb1, ub0, ub1, dyp0, dyp1, carry_ref, dc_ref, db_ref):
        s = pl.program_id(0)

        @pl.when(s == 0)
        def _():
            for r in (buf0, buf1, hb0, hb1, ub0, ub1, dyp0, dyp1, carry_ref, dc_ref, db_ref, dwg_ref, dds_ref, dlam_ref):
                r[...] = jnp.zeros_like(r)

        def step(p_buf, p_h, p_u, p_dyp, q_buf, q_h):
            g = p_buf[...].astype(BF16)
            ub_done = p_u[...].astype(BF16)
            du = []
            for m in range(N_BLK):
                g_m = _state_block(g, m)
                db_ref[m] += _dot_tn(g_m, ub_done[:, m * BLK_CH:(m + 1) * BLK_CH])
                du.append(_dot_nt(g_m, b_ref[m]))
            du_ref[...] = (jnp.concatenate(du, axis=1) + ds_ref[...] * p_dyp[...]).astype(BF16)
            u = u_ref[...].astype(F32)
            ypre = yp_ref[...]
            z = z_ref[...]
            z1 = z[:, :SSM_W]
            sg = jax.nn.sigmoid(z[:, SSM_W:])
            dy = dy_ref[...].astype(F32) * jnp.where(s < n_chunks, 1.0, 0.0)
            dz = jnp.concatenate([dy * sg, dy * z1 * sg * (1.0 - sg)], axis=1).astype(BF16)
            yg, gelu_vjp = jax.vjp(jax.nn.gelu, ypre)
            dwg_ref[...] += _dot_tn(yg.astype(BF16), dz)
            dypre = gelu_vjp(_dot_nt(dz, wg_ref[...]))[0]
            dds_ref[...] += jnp.sum(dypre * u, axis=0, keepdims=True)
            dyb = dypre.astype(BF16)
            hb = h_ref[...]
            p_h[...] = hb.astype(F32)
            p_u[...] = u
            p_dyp[...] = dypre
            for m in range(N_BLK):
                dy_m = dyb[:, m * BLK_CH:(m + 1) * BLK_CH]
                dc_ref[m] += _dot_tn(_state_block(hb, m), dy_m)
                _put_state_block(p_buf, m, _dot_nt(dy_m, c_ref[m]))
            _scan_tiles(q_buf, cst_ref, carry_ref, tt, True, lax.rem(s + nt - 1, nt) == 0, h_ref=q_h, acc_ref=dlam_ref)

        @pl.when(lax.rem(s, 2) == 0)
        def _():
            step(buf0, hb0, ub0, dyp0, buf1, hb1)

        @pl.when(lax.rem(s, 2) == 1)
        def _():
            step(buf1, hb1, ub1, dyp1, buf0, hb0)

        @pl.when(s == n_chunks + 1)
        def _():
            dlam_ref[...] = jnp.broadcast_to(jnp.sum(dlam_ref[...], axis=0, keepdims=True), dlam_ref.shape)
            for acc_ref, out_ref in ((dc_ref, dcc_ref), (db_ref, dbb_ref)):
                for m in range(N_BLK):
                    for ri in range(2):
                        for gl in range(BLK_GROUPS):
                            out_ref[ri, pl.ds((m * BLK_GROUPS + gl) * SSM_N, SSM_N), :] = acc_ref[
                                m, pl.ds(ri * BLK_ST + gl * SSM_N, SSM_N), pl.ds(gl * SSM_GC, SSM_GC)]

    def chunk_rows(o):
        return lax.div(o, nt) * nt + (nt - 1 - lax.rem(o, nt))

    def rrow(col):
        return lambda s: (chunk_rows(jnp.minimum(s, n_chunks - 1)), col)

    const = lambda s: (0, 0)
    const3 = lambda s: (0, 0, 0)
    state_buf = pltpu.VMEM((tt, 2 * NST), F32)
    chan_buf = pltpu.VMEM((tt, SSM_W), F32)
    return pl.pallas_call(
        body, grid=(n_chunks + 2,),
        in_specs=[pl.BlockSpec((tt, SSM_W), rrow(1)),
                  pl.BlockSpec((tt, SSM_W), rrow(1)),
                  pl.BlockSpec((tt, 2 * NST), rrow(0)),
                  pl.BlockSpec((tt, SSM_W), rrow(0)),
                  pl.BlockSpec((tt, 2 * SSM_W), rrow(0)),
                  pl.BlockSpec((N_BLK, BLK_CH, 2 * BLK_ST), const3),
                  pl.BlockSpec((N_BLK, 2 * BLK_ST, BLK_CH), const3),
                  pl.BlockSpec((8 * SUBLANES, NST), const),
                  pl.BlockSpec((1, SSM_W), const),
                  pl.BlockSpec((SSM_W, 2 * SSM_W), const)],
        out_specs=[pl.BlockSpec((tt, SSM_W), lambda s: (chunk_rows(jnp.maximum(s - 2, 0)), 0)),
                   pl.BlockSpec((SSM_W, 2 * SSM_W), const),
                   pl.BlockSpec((1, SSM_W), const),
                   pl.BlockSpec((2, NST, SSM_GC), const3),
                   pl.BlockSpec((2, NST, SSM_GC), const3),
                   pl.BlockSpec((SUBLANES, 2 * NST), const)],
        out_shape=[jax.ShapeDtypeStruct((nb * l, SSM_W), BF16),
                   jax.ShapeDtypeStruct((SSM_W, 2 * SSM_W), F32),
                   jax.ShapeDtypeStruct((1, SSM_W), F32),
                   jax.ShapeDtypeStruct((2, NST, SSM_GC), F32),
                   jax.ShapeDtypeStruct((2, NST, SSM_GC), F32),
                   jax.ShapeDtypeStruct((SUBLANES, 2 * NST), F32)],
        scratch_shapes=[state_buf, state_buf, state_buf, state_buf, chan_buf, chan_buf, chan_buf, chan_buf,
                        pltpu.VMEM((SUBLANES, 2 * NST), F32),
                        pltpu.VMEM((N_BLK, 2 * BLK_ST, BLK_CH), F32), pltpu.VMEM((N_BLK, 2 * BLK_ST, BLK_CH), F32)],
        compiler_params=_params("arbitrary"), name="s5_bwd")(
            proj, d_ycat, h, ypre, z, b_m, c_m, cst_rev, d_skip, w_glu)


def _head_mask(hh):
    lane = lax.broadcasted_iota(jnp.int32, (1, ATT_W), 1)
    return (lane >= hh * MEM_HD) & (lane < (hh + 1) * MEM_HD)


def _mem_kv(mem_ref, gm_ref, wkv_ref):
    m = mem_ref[0]
    mh = m * _rms_scale(m)
    mb = (mh * gm_ref[...]).astype(BF16)
    kv = _dot(mb, wkv_ref[...])
    return mh, mb, kv[:, :ATT_W].astype(BF16), kv[:, ATT_W:].astype(BF16)


def _stack_heads(dst_ref, a):
    for hh in range(MEM_HEADS):
        dst_ref[pl.ds(hh * N_MEM, N_MEM), :] = jnp.where(_head_mask(hh), a, jnp.zeros_like(a))


def _fold_heads(a):
    out = jnp.zeros((N_MEM, ATT_W), a.dtype)
    for hh in range(MEM_HEADS):
        out = out + jnp.where(_head_mask(hh), a[hh * N_MEM:(hh + 1) * N_MEM], 0.0)
    return out


def _softmax_heads(s):
    parts = []
    for hh in range(MEM_HEADS):
        sh = s[:, hh * N_MEM:(hh + 1) * N_MEM]
        e = jnp.exp(sh - jnp.max(sh, axis=-1, keepdims=True))
        parts.append(e / jnp.sum(e, axis=-1, keepdims=True))
    return jnp.concatenate(parts, axis=1)


def attn_fwd(proj, mem, g_mem, w_kv, nb, l):
    tq = min(ATTN_ROWS, l)
    nq = l // tq
    scale = MEM_HD ** -0.5

    def body(q_ref, mem_ref, gm_ref, wkv_ref, o_ref, k_s, v_s):
        @pl.when(pl.program_id(1) == 0)
        def _():
            _, _, k, v = _mem_kv(mem_ref, gm_ref, wkv_ref)
            _stack_heads(k_s, k)
            _stack_heads(v_s, v)

        p = _softmax_heads(_dot_nt(q_ref[...], k_s[...]) * scale)
        o_ref[...] = _dot(p.astype(BF16), v_s[...]).astype(BF16)

    const = lambda b, t: (0, 0)
    stacked = pltpu.VMEM((MEM_HEADS * N_MEM, ATT_W), BF16)
    return pl.pallas_call(
        body, grid=(nb, nq),
        in_specs=[pl.BlockSpec((tq, ATT_W), lambda b, t: (b * nq + t, 3)),
                  pl.BlockSpec((1, N_MEM, D_MODEL), lambda b, t: (b, 0, 0)),
                  pl.BlockSpec((1, D_MODEL), const),
                  pl.BlockSpec((D_MODEL, 2 * ATT_W), const)],
        out_specs=pl.BlockSpec((tq, ATT_W), lambda b, t: (b * nq + t, 0)),
        out_shape=jax.ShapeDtypeStruct((nb * l, ATT_W), BF16),
        scratch_shapes=[stacked, stacked],
        compiler_params=_params("arbitrary", "arbitrary"), name="attn_fwd")(proj, mem, g_mem, w_kv)


def attn_bwd(proj, d_ycat, mem, g_mem, w_kv, nb, l):
    tq = min(ATTN_ROWS, l)
    nq = l // tq
    scale = MEM_HD ** -0.5

    def body(q_ref, do_ref, mem_ref, gm_ref, wkv_ref, dq_ref, dwkv_ref, dgm_ref, k_s, v_s, mb_s, dk_s, dv_s):
        b = pl.program_id(0)
        t = pl.program_id(1)

        @pl.when((b == 0) & (t == 0))
        def _():
            dwkv_ref[...] = jnp.zeros_like(dwkv_ref)
            dgm_ref[...] = jnp.zeros_like(dgm_ref)

        @pl.when(t == 0)
        def _():
            _, mb, k, v = _mem_kv(mem_ref, gm_ref, wkv_ref)
            _stack_heads(k_s, k)
            _stack_heads(v_s, v)
            mb_s[...] = mb
            dk_s[...] = jnp.zeros_like(dk_s)
            dv_s[...] = jnp.zeros_like(dv_s)

        q = q_ref[...]
        do = do_ref[...]
        k = k_s[...]
        p = _softmax_heads(_dot_nt(q, k) * scale)
        dp = _dot_nt(do, v_s[...])
        ds = []
        for hh in range(MEM_HEADS):
            blk = slice(hh * N_MEM, (hh + 1) * N_MEM)
            ds.append(p[:, blk] * (dp[:, blk] - jnp.sum(dp[:, blk] * p[:, blk], axis=-1, keepdims=True)) * scale)
        ds = jnp.concatenate(ds, axis=1).astype(BF16)
        dq_ref[...] = _dot(ds, k).astype(BF16)
        dk_s[...] += _dot_tn(ds, q)
        dv_s[...] += _dot_tn(p.astype(BF16), do)

        @pl.when(t == nq - 1)
        def _():
            dkv = jnp.concatenate([_fold_heads(dk_s[...]), _fold_heads(dv_s[...])], axis=1).astype(BF16)
            dwkv_ref[...] += _dot_tn(mb_s[...], dkv)
            m = mem_ref[0]
            dgm_ref[...] += jnp.sum(_dot_nt(dkv, wkv_ref[...]) * (m * _rms_scale(m)), axis=0, keepdims=True)

    const = lambda b, t: (0, 0)
    return pl.pallas_call(
        body, grid=(nb, nq),
        in_specs=[pl.BlockSpec((tq, ATT_W), lambda b, t: (b * nq + t, 3)),
                  pl.BlockSpec((tq, ATT_W), lambda b, t: (b * nq + t, 3)),
                  pl.BlockSpec((1, N_MEM, D_MODEL), lambda b, t: (b, 0, 0)),
                  pl.BlockSpec((1, D_MODEL), const),
                  pl.BlockSpec((D_MODEL, 2 * ATT_W), const)],
        out_specs=[pl.BlockSpec((tq, ATT_W), lambda b, t: (b * nq + t, 0)),
                   pl.BlockSpec((D_MODEL, 2 * ATT_W), const),
                   pl.BlockSpec((1, D_MODEL), const)],
        out_shape=[jax.ShapeDtypeStruct((nb * l, ATT_W), BF16),
                   jax.ShapeDtypeStruct((D_MODEL, 2 * ATT_W), F32),
                   jax.ShapeDtypeStruct((1, D_MODEL), F32)],
        scratch_shapes=[pltpu.VMEM((MEM_HEADS * N_MEM, ATT_W), BF16), pltpu.VMEM((MEM_HEADS * N_MEM, ATT_W), BF16),
                        pltpu.VMEM((N_MEM, D_MODEL), BF16),
                        pltpu.VMEM((MEM_HEADS * N_MEM, ATT_W), F32), pltpu.VMEM((MEM_HEADS * N_MEM, ATT_W), F32)],
        compiler_params=_params("arbitrary", "arbitrary"), name="attn_bwd")(proj, d_ycat, mem, g_mem, w_kv)


def tail(y_pool, y_ssm, y_att, proj, x2, target, w_out, g_post):
    t = x2.shape[0]
    tm = min(ROW_TILE, t)

    def body(yp_ref, ys_ref, ya_ref, gate_ref, x_ref, tg_ref, w_ref, gp_ref,
             dz_ref, dyc_ref, dgate_ref, dw_ref, dgp_ref, loss_ref):
        @pl.when(pl.program_id(0) == 0)
        def _():
            dw_ref[...] = jnp.zeros_like(dw_ref)
            dgp_ref[...] = jnp.zeros_like(dgp_ref)
            loss_ref[...] = jnp.zeros_like(loss_ref)

        ycat = jnp.concatenate([yp_ref[...], ys_ref[...], ya_ref[...]], axis=1).astype(F32)
        gate = gate_ref[...].astype(F32)
        sg = jax.nn.sigmoid(gate)
        silu = gate * sg
        yb = (ycat * silu).astype(BF16)
        w = w_ref[...]
        out = _dot(yb, w)
        r2 = _rms_scale(out)
        oh = out * r2
        gp = gp_ref[...]
        err = (x_ref[...] + oh * gp) - tg_ref[...]
        loss_ref[...] += 0.5 * jnp.sum(jnp.mean(err * err, axis=-1, keepdims=True), axis=0, keepdims=True)
        dz = err * (1.0 / D_MODEL)
        dz_ref[...] = dz.astype(BF16)
        dgp_ref[...] += jnp.sum(dz * oh, axis=0, keepdims=True)
        dn = dz * gp
        dout = (r2 * (dn - oh * jnp.mean(dn * oh, axis=-1, keepdims=True))).astype(BF16)
        dw_ref[...] += _dot_tn(yb, dout)
        dy = _dot_nt(dout, w)
        dyc_ref[...] = (dy * silu).astype(BF16)
        dgate_ref[...] = (dy * ycat * (sg * (1.0 + gate * (1.0 - sg)))).astype(BF16)

    row = lambda i: (i, 0)
    const = lambda i: (0, 0)
    full = jax.ShapeDtypeStruct((t, D_MODEL), BF16)
    return pl.pallas_call(
        body, grid=(t // tm,),
        in_specs=[pl.BlockSpec((tm, POOL_W), row),
                  pl.BlockSpec((tm, SSM_W), row),
                  pl.BlockSpec((tm, ATT_W), row),
                  pl.BlockSpec((tm, D_MODEL), lambda i: (i, 1)),
                  pl.BlockSpec((tm, D_MODEL), row),
                  pl.BlockSpec((tm, D_MODEL), row),
                  pl.BlockSpec((D_MODEL, D_MODEL), const),
                  pl.BlockSpec((1, D_MODEL), const)],
        out_specs=[pl.BlockSpec((tm, D_MODEL), row),
                   pl.BlockSpec((tm, D_MODEL), row),
                   pl.BlockSpec((tm, D_MODEL), row),
                   pl.BlockSpec((D_MODEL, D_MODEL), const),
                   pl.BlockSpec((1, D_MODEL), const),
                   pl.BlockSpec((1, LANES), const)],
        out_shape=[full, full, full,
                   jax.ShapeDtypeStruct((D_MODEL, D_MODEL), F32),
                   jax.ShapeDtypeStruct((1, D_MODEL), F32),
                   jax.ShapeDtypeStruct((1, LANES), F32)],
        compiler_params=_params("arbitrary"), name="tail")(y_pool, y_ssm, y_att, proj, x2, target, w_out, g_post)


def in_proj_bwd(du_pool, du_ssm, dq, d_gate, x2, dz, g_pre, w_in):
    t = x2.shape[0]
    tm = min(ROW_TILE, t)
    shard = W_IN_SHARD

    def body(dup_ref, dus_ref, dq_ref, dg_ref, x_ref, dz_ref, g_ref, w_ref, gx_ref, dw_ref, dgp_ref):
        @pl.when(pl.program_id(0) == 0)
        def _():
            dw_ref[...] = jnp.zeros_like(dw_ref)
            dgp_ref[...] = jnp.zeros_like(dgp_ref)

        dproj = jnp.concatenate([dup_ref[...], dus_ref[...], dq_ref[...], dg_ref[...]], axis=1)
        x = x_ref[...]
        r1 = _rms_scale(x)
        xh = x * r1
        g = g_ref[...]
        hb = (xh * g).astype(BF16)
        dh = jnp.zeros((tm, D_MODEL), F32)
        for k in range(N_CHIPS):
            dp_k = dproj[:, k * shard:(k + 1) * shard]
            dw_ref[k] += _dot_tn(hb, dp_k)
            dh = dh + _dot_nt(dp_k, w_ref[k])
        dgp_ref[...] += jnp.sum(dh * xh, axis=0, keepdims=True)
        dn = dh * g
        gx_ref[...] = dz_ref[...].astype(F32) + r1 * (dn - xh * jnp.mean(dn * xh, axis=-1, keepdims=True))

    row = lambda i: (i, 0)
    const = lambda i: (0, 0)
    return pl.pallas_call(
        body, grid=(t // tm,),
        in_specs=[pl.BlockSpec((tm, POOL_W), row),
                  pl.BlockSpec((tm, SSM_W), row),
                  pl.BlockSpec((tm, ATT_W), row),
                  pl.BlockSpec((tm, D_MODEL), row),
                  pl.BlockSpec((tm, D_MODEL), row),
                  pl.BlockSpec((tm, D_MODEL), row),
                  pl.BlockSpec((1, D_MODEL), const),
                  pl.BlockSpec((N_CHIPS, D_MODEL, shard), lambda i: (0, 0, 0))],
        out_specs=[pl.BlockSpec((tm, D_MODEL), row),
                   pl.BlockSpec((N_CHIPS, D_MODEL, shard), lambda i: (0, 0, 0)),
                   pl.BlockSpec((1, D_MODEL), const)],
        out_shape=[jax.ShapeDtypeStruct((t, D_MODEL), F32),
                   jax.ShapeDtypeStruct((N_CHIPS, D_MODEL, shard), F32),
                   jax.ShapeDtypeStruct((1, D_MODEL), F32)],
        compiler_params=_params("arbitrary"), name="in_proj_bwd")(du_pool, du_ssm, dq, d_gate, x2, dz, g_pre, w_in)


def _block_diag(blocks):
    g, r, c = blocks.shape
    eye = jnp.eye(g, dtype=blocks.dtype)
    return jnp.einsum("grc,gh->grhc", blocks, eye).reshape(g * r, g * c)


def _block_diag_extract(mat, g, r, c):
    eye = jnp.eye(g, dtype=mat.dtype)
    return jnp.einsum("grhc,gh->grc", mat.reshape(g, r, g, c), eye)


def local_step(x, mem, target, g_pre, w_in_shard, w_pool, pool_scale, a_re, a_im, log_dt, b_re, b_im, c_re, c_im,
               d_skip, later_shards, g_mem, g_post):
    nb, l, _ = x.shape
    x2 = x.reshape(nb * l, D_MODEL)
    tg2 = target.reshape(nb * l, D_MODEL)

    rowv = lambda a: a.reshape(1, NST)
    lanes_last = lambda b: b.transpose(2, 0, 1).reshape(SSM_GC, NST)
    ldt_row = rowv(jnp.broadcast_to(log_dt.reshape(SSM_NG, 1), (SSM_NG, SSM_N)))
    b_re_t = lanes_last(b_re)
    b_im_t = lanes_last(b_im)
    cst_f, cst_b = s5_scan_consts(rowv(a_re), rowv(a_im), ldt_row)
    bbr, bbi = s5_bbar(rowv(a_re), rowv(a_im), ldt_row, b_re_t, b_im_t)
    eye = jnp.eye(BLK_GROUPS, dtype=F32)
    blk_in = lambda bb: jnp.einsum("cmgn,gh->mhcgn", bb.reshape(SSM_GC, N_BLK, BLK_GROUPS, SSM_N), eye).reshape(
        N_BLK, BLK_CH, BLK_ST)
    blk_out = lambda cc: jnp.einsum("mgcn,gh->mgnhc", cc.reshape(N_BLK, BLK_GROUPS, SSM_GC, SSM_N), eye).reshape(
        N_BLK, BLK_ST, BLK_CH)
    b_m = jnp.concatenate([blk_in(bbr), blk_in(bbi)], axis=2).astype(BF16)
    c_m = jnp.concatenate([blk_out(c_re), -blk_out(c_im)], axis=1).astype(BF16)
    w_pool_blk = _block_diag(w_pool.reshape(4, POOL_GW, POOL_GW)).astype(BF16)

    proj, w_in, (g_glu, g_kv, g_out) = in_proj(x2, g_pre, w_in_shard, later_shards)
    w_glu = g_glu.transpose(2, 0, 1).reshape(SSM_W, 2 * SSM_W)
    w_kv = g_kv.reshape(D_MODEL, 2 * ATT_W)
    w_out = g_out.reshape(D_MODEL, D_MODEL)
    y_pool, pool_d = pool_fwd(proj, w_pool_blk, pool_scale, nb, l)
    y_ssm, h, ypre, z = s5_fwd(proj, b_m, c_m, cst_f, d_skip, w_glu, nb, l)
    y_att = attn_fwd(proj, mem, g_mem, w_kv, nb, l)
    dz, d_ycat, d_gate, d_w_out, d_g_post, loss = tail(y_pool, y_ssm, y_att, proj, x2, tg2, w_out, g_post)
    du_pool, d_wp_blk, d_pool_scale = pool_bwd(pool_d, d_ycat, w_pool_blk, pool_scale, nb, l)
    du_ssm, d_w_glu, d_d_skip, d_cc, d_bb, d_lam = s5_bwd(
        proj, d_ycat, h, ypre, z, b_m, c_m, cst_b, d_skip, w_glu, nb, l)
    dq, d_w_kv, d_g_mem = attn_bwd(proj, d_ycat, mem, g_mem, w_kv, nb, l)
    grad_x, d_w_in, d_g_pre = in_proj_bwd(du_pool, du_ssm, dq, d_gate, x2, dz, g_pre, w_in)

    d_c = d_cc[0].reshape(SSM_NG, SSM_N, SSM_GC)
    d_ci = d_cc[1].reshape(SSM_NG, SSM_N, SSM_GC)
    d_lam_row = d_lam[0]
    d_ar, d_ai, d_ld, d_br, d_bi = s5_param_bwd(
        rowv(a_re), rowv(a_im), ldt_row, b_re_t, b_im_t,
        d_lam_row[:NST].reshape(1, NST), d_lam_row[NST:].reshape(1, NST), d_bb[0].T, d_bb[1].T)
    from_lanes_last = lambda b: b.reshape(SSM_GC, SSM_NG, SSM_N).transpose(1, 0, 2).reshape(1, SSM_NG, SSM_GC, SSM_N)

    grads = {
        "g_pre": d_g_pre,
        "w_in": d_w_in,
        "w_pool": _block_diag_extract(d_wp_blk, 4, POOL_GW, POOL_GW).reshape(1, 4, POOL_GW, POOL_GW),
        "pool_scale": d_pool_scale,
        "a_re": d_ar.reshape(1, SSM_NG, SSM_N),
        "a_im": d_ai.reshape(1, SSM_NG, SSM_N),
        "log_dt": d_ld[:, :SSM_NG],
        "b_re": from_lanes_last(d_br),
        "b_im": from_lanes_last(d_bi),
        "c_re": d_c.transpose(0, 2, 1).reshape(1, SSM_NG, SSM_GC, SSM_N),
        "c_im": (-d_ci).transpose(0, 2, 1).reshape(1, SSM_NG, SSM_GC, SSM_N),
        "d_skip": d_d_skip,
        "w_glu": d_w_glu,
        "g_mem": d_g_mem,
        "w_kv": d_w_kv,
        "w_out": d_w_out,
        "g_post": d_g_post,
    }
    return loss, grad_x.reshape(nb, l, D_MODEL), grads


VMEM_SPEC = pl.BlockSpec(memory_space=pltpu.VMEM)
ANY_SPEC = pl.BlockSpec(memory_space=pl.ANY)


def _place():
    return lax.axis_index("x"), lax.axis_index("y"), lax.axis_index("c")


def _other_chips(x, y):
    return [(1 - x, y), (x, 1 - y), (1 - x, 1 - y)]


SEMS_PER_ITEM = 5


def reduce_all(g4s, packs, pack_dtypes):
    n = len(g4s)
    dims = [g.shape[1:] for g in g4s]
    np_ = len(packs)
    prs = [p.shape[0] for p in packs]

    def body(*refs):
        g_refs, p_refs = refs[:n], refs[n:n + np_]
        outs, ops = refs[n + np_:2 * n + np_], refs[2 * n + np_:2 * (n + np_)]
        scr = refs[2 * (n + np_):]
        mine, theirs, sendb, recvb = scr[0:n], scr[n:2 * n], scr[2 * n:3 * n], scr[3 * n:4 * n]
        p_theirs, gats = scr[4 * n:4 * n + np_], scr[4 * n + np_:4 * n + 2 * np_]
        lsems, send_sems, recv_sems = scr[4 * n + 2 * np_:]
        x, y, c = _place()
        k = 2 * x + y
        chips = _other_chips(x, y)
        sib = (x, y, 1 - c)

        def halves(rows):
            hr = rows // 2
            return (pl.ds(pl.multiple_of(c * hr, SUBLANES), hr), pl.ds(pl.multiple_of((1 - c) * hr, SUBLANES), hr))

        def remote(src, dst, sem, to):
            return pltpu.make_async_remote_copy(
                src_ref=src, dst_ref=dst, send_sem=send_sems.at[sem], recv_sem=recv_sems.at[sem],
                device_id=to, device_id_type=MESH)

        loads, started = [], []
        for i in range(n):
            my_rows, sib_rows = halves(dims[i][0])
            ld = pltpu.make_async_copy(g_refs[i].at[:, my_rows, :], mine[i], lsems.at[i])
            ld.start()
            loads.append(ld)
            sw = remote(g_refs[i].at[:, sib_rows, :], theirs[i], SEMS_PER_ITEM * i, sib)
            sw.start()
            started.append(sw)
        p_swaps = []
        for q in range(np_):
            _, p_sib = halves(prs[q])
            sw = remote(p_refs[q].at[p_sib, :], p_theirs[q], SEMS_PER_ITEM * (n + q), sib)
            sw.start()
            p_swaps.append(sw)
        started += p_swaps

        for q in range(np_):
            p_my, _ = halves(prs[q])
            p_swaps[q].wait_recv()
            gats[q][k] = (p_refs[q][p_my, :] + p_theirs[q][...]).astype(gats[q].dtype)
            for j, (px, py) in enumerate(chips):
                cp = remote(gats[q].at[k], gats[q].at[k], SEMS_PER_ITEM * (n + q) + 1 + j, (px, py, c))
                cp.start()
                started.append(cp)
        ici = []
        for i in range(n):
            loads[i].wait()
            started[i].wait_recv()
            for j, (px, py) in enumerate(chips):
                kp = 2 * px + py
                sendb[i][j] = (mine[i][kp] + theirs[i][kp]).astype(BF16)
                cp = remote(sendb[i].at[j], recvb[i].at[j], SEMS_PER_ITEM * i + 1 + j, (px, py, c))
                cp.start()
                ici.append(cp)

        for q in range(np_):
            p_my, _ = halves(prs[q])
            gat, op = gats[q], ops[q]
            for j, (px, py) in enumerate(chips):
                kp = 2 * px + py
                remote(gat.at[kp], gat.at[kp], SEMS_PER_ITEM * (n + q) + 1 + j, (px, py, c)).wait_recv()
            op[p_my, :] = ((gat[0].astype(F32) + gat[1].astype(F32)) + gat[2].astype(F32)) + gat[3].astype(F32)
            last = remote(op.at[p_my, :], op.at[p_my, :], SEMS_PER_ITEM * (n + q) + 4, sib)
            last.start()
            started.append(last)
        for i in range(n):
            my_rows, _ = halves(dims[i][0])
            half = mine[i][k] + theirs[i][k]
            for j in range(3):
                ici[3 * i + j].wait_recv()
                half = half + recvb[i][j].astype(F32)
            outs[i][my_rows, :] = half
            last = remote(outs[i].at[my_rows, :], outs[i].at[my_rows, :], SEMS_PER_ITEM * i + 4, sib)
            last.start()
            started.append(last)
        for q in range(np_):
            _, p_sib = halves(prs[q])
            remote(ops[q].at[p_sib, :], ops[q].at[p_sib, :], SEMS_PER_ITEM * (n + q) + 4, sib).wait_recv()
        for i in range(n):
            _, sib_rows = halves(dims[i][0])
            remote(outs[i].at[sib_rows, :], outs[i].at[sib_rows, :], SEMS_PER_ITEM * i + 4, sib).wait_recv()
        for cp in started + ici:
            cp.wait_send()

    n_sems = SEMS_PER_ITEM * (n + np_)
    scratch = ([pltpu.VMEM((N_CHIPS, r // 2, cd), F32) for r, cd in dims] * 2
               + [pltpu.VMEM((3, r // 2, cd), BF16) for r, cd in dims] * 2
               + [pltpu.VMEM((pr // 2, LANES), F32) for pr in prs]
               + [pltpu.VMEM((N_CHIPS, pr // 2, LANES), dt) for pr, dt in zip(prs, pack_dtypes)]
               + [pltpu.SemaphoreType.DMA((n,)), pltpu.SemaphoreType.DMA((n_sems,)), pltpu.SemaphoreType.DMA((n_sems,))])
    res = pl.pallas_call(
        body,
        out_shape=[jax.ShapeDtypeStruct(d, F32) for d in dims] + [jax.ShapeDtypeStruct(p.shape, F32) for p in packs],
        in_specs=[ANY_SPEC] * n + [VMEM_SPEC] * np_, out_specs=[VMEM_SPEC] * (n + np_),
        scratch_shapes=scratch,
        compiler_params=pltpu.CompilerParams(vmem_limit_bytes=VMEM_LIMIT),
        name="reduce_all")(*g4s, *packs)
    return res[:n], res[n:]


def adamw_all(ws, gs, ms, vs):
    n = len(ws)
    chunked = [a.ndim == 3 and a.shape[0] == 1 and a.shape[1] % (ADAMW_STEPS * SUBLANES) == 0 for a in ws]

    def body(*refs):
        w, g, m, v = refs[:n], refs[n:2 * n], refs[2 * n:3 * n], refs[3 * n:4 * n]
        od, om, ov = refs[4 * n:5 * n], refs[5 * n:6 * n], refs[6 * n:]
        for i in range(n):
            if chunked[i]:
                od[i][...], om[i][...], ov[i][...] = _adamw(w[i][...], g[i][...], m[i][...], v[i][...])

        @pl.when(pl.program_id(0) == 0)
        def _():
            for i in range(n):
                if not chunked[i]:
                    od[i][...], om[i][...], ov[i][...] = _adamw(w[i][...], g[i][...], m[i][...], v[i][...])

    def spec(a, is_chunked):
        if is_chunked:
            return pl.BlockSpec((1, a.shape[1] // ADAMW_STEPS, a.shape[2]), lambda s: (0, s, 0))
        return pl.BlockSpec(a.shape, lambda s: (0,) * a.ndim)

    specs = [spec(a, c) for a, c in zip(ws, chunked)]
    shapes = [jax.ShapeDtypeStruct(a.shape, F32) for a in ws]
    res = pl.pallas_call(
        body, grid=(ADAMW_STEPS,), out_shape=shapes * 3, in_specs=specs * 4, out_specs=specs * 3,
        compiler_params=_params("arbitrary"), name="adamw_all")(*ws, *gs, *ms, *vs)
    return res[:n], res[n:2 * n], res[2 * n:]


WEIGHTS = ("g_pre", "w_in", "w_pool", "pool_scale", "a_re", "a_im", "log_dt", "b_re", "b_im", "c_re", "c_im", "d_skip",
           "w_glu", "g_mem", "w_kv", "w_out", "g_post")
SHARDED = ("w_in", "w_glu", "w_kv", "w_out")
SWAPPED = ("b_re", "b_im", "w_glu")
MATRICES = ("w_pool", "b_re", "b_im", "c_re", "c_im")
REPLICATED = tuple(n for n in WEIGHTS if n not in SHARDED)
PACK_TILE = SUBLANES * LANES


def _pack(arrays):
    parts = []
    for a in arrays:
        flat = a.reshape(-1)
        parts.append(jnp.pad(flat, (0, -flat.shape[0] % PACK_TILE)).reshape(-1, LANES))
    rows = sum(p.shape[0] for p in parts)
    if rows % (2 * SUBLANES):
        parts.append(jnp.zeros((SUBLANES, LANES), F32))
    return jnp.concatenate(parts, axis=0)


def _unpack(packed, shapes):
    out, row = [], 0
    for shp in shapes:
        size = math.prod(shp)
        rows = -(-size // PACK_TILE) * SUBLANES
        out.append(packed[row:row + rows].reshape(-1)[:size].reshape(shp))
        row += rows
    return out


def kernel(x, mem, g_pre, w_in, w_pool, pool_scale, a_re, a_im, log_dt, b_re, b_im, c_re, c_im, d_skip, w_glu, g_mem, w_kv, w_out, g_post, loss_target, m_g_pre, m_w_in, m_w_pool, m_pool_scale, m_a_re, m_a_im, m_log_dt, m_b_re, m_b_im, m_c_re, m_c_im, m_d_skip, m_w_glu, m_g_mem, m_w_kv, m_w_out, m_g_post, v_g_pre, v_w_in, v_w_pool, v_pool_scale, v_a_re, v_a_im, v_log_dt, v_b_re, v_b_im, v_c_re, v_c_im, v_d_skip, v_w_glu, v_g_mem, v_w_kv, v_w_out, v_g_post):
    given = dict(locals())
    wts = {n: given[n] for n in WEIGHTS}
    mom = {n: given["m_" + n] for n in WEIGHTS}
    var = {n: given["v_" + n] for n in WEIGHTS}

    swap = lambda a: jnp.swapaxes(a, -1, -2)
    view = lambda n, a: swap(a) if n in SWAPPED else a

    loss_part, grad_x, grads = local_step(
        x, mem, loss_target, g_pre, w_in[0], w_pool[0], pool_scale, a_re[0], a_im[0], log_dt[0], b_re[0], b_im[0],
        c_re[0], c_im[0], d_skip, [swap(w_glu)[0], w_kv[0], w_out[0]], g_mem, g_post)

    glu_cols = 2 * SSM_W // N_CHIPS
    partial4 = [
        grads["w_in"],
        grads["w_glu"].reshape(SSM_W, N_CHIPS, glu_cols).transpose(1, 2, 0),
        grads["w_kv"].reshape(N_CHIPS, D_MODEL // N_CHIPS, 2 * ATT_W),
        grads["w_out"].reshape(N_CHIPS, D_MODEL // N_CHIPS, D_MODEL),
    ]
    loss_tile = jnp.broadcast_to(loss_part, (SUBLANES, LANES))
    vectors = [n for n in REPLICATED if n not in MATRICES]
    sharded_g, (pack_v, pack_m) = reduce_all(
        partial4, [_pack([grads[n] for n in vectors] + [loss_tile]), _pack([grads[n] for n in MATRICES])], [F32, BF16])
    vec_g = _unpack(pack_v, [view(n, wts[n]).shape for n in vectors] + [(SUBLANES, LANES)])
    mat_g = _unpack(pack_m, [view(n, wts[n]).shape for n in MATRICES])
    loss = vec_g[-1][0, 0]
    grad = dict(zip(SHARDED, [g[None] for g in sharded_g]))
    grad.update(zip(vectors, vec_g[:-1]))
    grad.update(zip(MATRICES, mat_g))

    deltas, new_ms, new_vs = adamw_all([view(n, wts[n]) for n in WEIGHTS], [grad[n] for n in WEIGHTS],
                                       [view(n, mom[n]) for n in WEIGHTS], [view(n, var[n]) for n in WEIGHTS])
    back = lambda arrs: [view(n, a) for n, a in zip(WEIGHTS, arrs)]
    return (loss, grad_x, *back([grad[n] for n in WEIGHTS]), *back(deltas), *back(new_ms), *back(new_vs))
```

```python
import math

import jax
import jax.numpy as jnp
from jax import lax
from jax.experimental import pallas as pl
from jax.experimental.pallas import tpu as pltpu

F32 = jnp.float32
BF16 = jnp.bfloat16

D_MODEL = 1024
POOL_W = 384
SSM_W = 384
ATT_W = 256
POOL_GW = 96
POOL_PAD = 16
SSM_NG = 24
SSM_N = 64
SSM_GC = 16
N_CHIPS = 4
W_IN_SHARD = 2 * D_MODEL // N_CHIPS
NST = SSM_NG * SSM_N
BLK_CH = 128
BLK_GROUPS = BLK_CH // SSM_GC
BLK_ST = BLK_GROUPS * SSM_N
N_BLK = SSM_W // BLK_CH
N_MEM = 256
MEM_HEADS = 4
MEM_HD = 64
EPS = 1e-6

ADAM_LR = 0.001
ADAM_B1 = 0.9
ADAM_B2 = 0.999
ADAM_EPS = 1e-08
ADAM_WD = 0.01
ADAM_STEP = 10

SUBLANES = 8
LANES = 128
V7X_VMEM_BYTES = 64 * 2**20
VMEM_LIMIT = V7X_VMEM_BYTES - 8 * 2**20
SCAN_COLS = 512
ROW_TILE = 512
IN_PROJ_ROWS = 2048
ATTN_ROWS = 2048
S5_CHUNK = 256
ADAMW_STEPS = 4
MESH = pl.DeviceIdType.MESH

NT = (((1,), (1,)), ((), ()))
TN = (((0,), (0,)), ((), ()))


def _params(*sem):
    return pltpu.CompilerParams(dimension_semantics=sem, vmem_limit_bytes=VMEM_LIMIT)


def _dot(a, b):
    return jnp.dot(a, b, preferred_element_type=F32)


def _dot_nt(a, b):
    return lax.dot_general(a, b, NT, preferred_element_type=F32)


def _dot_tn(a, b):
    return lax.dot_general(a, b, TN, preferred_element_type=F32)


def _rms_scale(v):
    return lax.rsqrt(jnp.mean(v * v, axis=-1, keepdims=True) + EPS)


def _adamw(w, g, m, v):
    m = ADAM_B1 * m + (1.0 - ADAM_B1) * g
    v = ADAM_B2 * v + (1.0 - ADAM_B2) * (g * g)
    m_hat = m / (1.0 - ADAM_B1 ** ADAM_STEP)
    v_hat = v / (1.0 - ADAM_B2 ** ADAM_STEP)
    delta = -ADAM_LR * (m_hat / (jnp.sqrt(v_hat) + ADAM_EPS) + ADAM_WD * w)
    return delta, m, v


def _phase_shard(j):
    cx, cy = lax.axis_index("x"), lax.axis_index("y")
    others = [2 * px + py for px, py in _other_chips(cx, cy)]
    own = 2 * cx + cy
    return jnp.where(j == 0, own, jnp.where(j == 1, others[0], jnp.where(j == 2, others[1], others[2])))


def in_proj(x2, g_pre, w_in_shard, later_shards):
    t = x2.shape[0]
    tm = min(IN_PROJ_ROWS, t)
    n_tiles = t // tm
    n = len(later_shards)

    def body(x_ref, g_ref, ws_ref, *rest):
        shard_refs = rest[:n]
        o_ref, w_out_ref, gathered = rest[n], rest[n + 1], rest[n + 2:2 * n + 2]
        w_buf, h_buf = rest[2 * n + 2], rest[2 * n + 3]
        cast_bufs = rest[2 * n + 4:3 * n + 4]
        local_sems, send_sems, recv_sems = rest[3 * n + 4:]
        j = pl.program_id(0)
        i = pl.program_id(1)
        cx, cy, cc = _place()
        k = 2 * cx + cy
        chips = _other_chips(cx, cy)

        half = w_in_shard.shape[0] // 2

        def w_copy(p, block, core_half, forward):
            px, py = chips[p]
            rows = w_buf.at[block, pl.ds(pl.multiple_of(core_half * half, 2 * SUBLANES), half)]
            sem = 3 + p if forward else p
            return pltpu.make_async_remote_copy(
                src_ref=rows, dst_ref=rows, send_sem=send_sems.at[sem], recv_sem=recv_sems.at[sem],
                device_id=(cx, cy, 1 - cc) if forward else (px, py, cc), device_id_type=MESH)

        def own(m):
            return pltpu.make_async_copy(cast_bufs[m], gathered[m].at[k], local_sems.at[m])

        def later_copy(m, p, block, core_half, forward):
            px, py = chips[p]
            hr = later_shards[m].shape[0] // 2
            rows = pl.ds(pl.multiple_of(core_half * hr, 2 * SUBLANES), hr)
            dst = gathered[m].at[block, rows]
            sem = 6 + 6 * m + (3 + p if forward else p)
            return pltpu.make_async_remote_copy(
                src_ref=dst if forward else cast_bufs[m].at[rows], dst_ref=dst, send_sem=send_sems.at[sem],
                recv_sem=recv_sems.at[sem], device_id=(cx, cy, 1 - cc) if forward else (px, py, cc),
                device_id_type=MESH)

        @pl.when((j == 0) & (i == 0))
        def _():
            w_buf[k] = ws_ref[...].astype(BF16)
            for p in range(3):
                w_copy(p, k, cc, False).start()
            for m in range(n):
                cast_bufs[m][...] = shard_refs[m][...].astype(BF16)
                own(m).start()
                for p in range(3):
                    later_copy(m, p, k, cc, False).start()

        rows = pl.ds(pl.multiple_of(i * tm, tm), tm)

        @pl.when(j == 0)
        def _():
            x = x_ref[...]
            h_buf[rows, :] = (x * _rms_scale(x) * g_ref[...]).astype(BF16)

        for p in range(3):
            @pl.when((j == p) & (i == n_tiles - 1))
            def _(p=p):
                px, py = chips[p]
                w_copy(p, 2 * px + py, cc, False).wait_recv()
                w_copy(p, 2 * px + py, cc, True).start()

            @pl.when((j == p + 1) & (i == 0))
            def _(p=p):
                px, py = chips[p]
                w_copy(p, 2 * px + py, 1 - cc, True).wait_recv()

        o_ref[...] = _dot(h_buf[rows, :], w_buf[_phase_shard(j)]).astype(BF16)

        out_cp = pltpu.make_async_copy(w_buf, w_out_ref, local_sems.at[n])

        @pl.when((j == N_CHIPS - 1) & (i == 0))
        def _():
            out_cp.start()
            for m in range(n):
                for p in range(3):
                    px, py = chips[p]
                    later_copy(m, p, 2 * px + py, cc, False).wait_recv()
                    later_copy(m, p, 2 * px + py, cc, True).start()

        @pl.when((j == N_CHIPS - 1) & (i == n_tiles - 1))
        def _():
            for m in range(n):
                own(m).wait()
            for m in range(n):
                for p in range(3):
                    px, py = chips[p]
                    later_copy(m, p, 2 * px + py, 1 - cc, True).wait_recv()
            for p in range(3):
                px, py = chips[p]
                w_copy(p, k, cc, False).wait_send()
                w_copy(p, 2 * px + py, cc, True).wait_send()
            for m in range(n):
                for p in range(3):
                    px, py = chips[p]
                    later_copy(m, p, k, cc, False).wait_send()
                    later_copy(m, p, 2 * px + py, cc, True).wait_send()
            out_cp.wait()

    whole = lambda a: pl.BlockSpec(a.shape, lambda j, i: (0,) * a.ndim)
    res = pl.pallas_call(
        body, grid=(N_CHIPS, n_tiles),
        in_specs=[pl.BlockSpec((tm, D_MODEL), lambda j, i: (jnp.where(j == 0, i, n_tiles - 1), 0)),
                  pl.BlockSpec((1, D_MODEL), lambda j, i: (0, 0)),
                  whole(w_in_shard)] + [whole(s) for s in later_shards],
        out_specs=[pl.BlockSpec((tm, W_IN_SHARD), lambda j, i: (i, _phase_shard(j)))] + [ANY_SPEC] * (n + 1),
        out_shape=([jax.ShapeDtypeStruct((t, 2 * D_MODEL), BF16),
                    jax.ShapeDtypeStruct((N_CHIPS,) + w_in_shard.shape, BF16)]
                   + [jax.ShapeDtypeStruct((N_CHIPS,) + s.shape, BF16) for s in later_shards]),
        scratch_shapes=([pltpu.VMEM((N_CHIPS,) + w_in_shard.shape, BF16), pltpu.VMEM((t, D_MODEL), BF16)]
                        + [pltpu.VMEM(s.shape, BF16) for s in later_shards]
                        + [pltpu.SemaphoreType.DMA((n + 1,)), pltpu.SemaphoreType.DMA((6 * n + 6,)),
                           pltpu.SemaphoreType.DMA((6 * n + 6,))]),
        compiler_params=_params("arbitrary", "arbitrary"), name="in_proj")(x2, g_pre, w_in_shard, *later_shards)
    return res[0], res[1], res[2:]


def _pool_lane_window(shape):
    ch = lax.broadcasted_iota(jnp.int32, shape, 1)
    return jnp.where(ch < POOL_GW, 2.0, jnp.where(ch < 2 * POOL_GW, 4.0, jnp.where(ch < 3 * POOL_GW, 8.0, 16.0)))


def _pool_select(win, s2, s4, s8, s16):
    return jnp.where(win == 2.0, s2, jnp.where(win == 4.0, s4, jnp.where(win == 8.0, s8, s16)))


def _pool_div_count(v, win):
    pos = (lax.broadcasted_iota(jnp.int32, (POOL_PAD, POOL_W), 0) + 1).astype(F32)
    head = v[:POOL_PAD] / jnp.minimum(pos, win)
    return jnp.concatenate([head, v[POOL_PAD:] * (1.0 / win)], axis=0)


def _pool_diff(u, pad_ref, l):
    lo = POOL_PAD
    pad_ref[pl.ds(lo, l), :] = u
    s2 = u + pad_ref[pl.ds(lo - 1, l), :]
    pad_ref[pl.ds(lo, l), :] = s2
    s4 = s2 + pad_ref[pl.ds(lo - 2, l), :]
    pad_ref[pl.ds(lo, l), :] = s4
    s8 = s4 + pad_ref[pl.ds(lo - 4, l), :]
    pad_ref[pl.ds(lo, l), :] = s8
    s16 = s8 + pad_ref[pl.ds(lo - 8, l), :]
    win = _pool_lane_window((1, POOL_W))
    return _pool_div_count(_pool_select(win, s2, s4, s8, s16), win) - u, win


def pool_fwd(proj, w_blk, pool_scale, nb, l):
    def body(u_ref, w_ref, ps_ref, y_ref, d_ref, pad_ref):
        pad_ref[pl.ds(0, POOL_PAD), :] = jnp.zeros((POOL_PAD, POOL_W), F32)
        d, _ = _pool_diff(u_ref[...].astype(F32), pad_ref, l)
        db = d.astype(BF16)
        d_ref[...] = db
        y_ref[...] = (_dot(db, w_ref[...]) * ps_ref[...]).astype(BF16)

    rows = pl.BlockSpec((l, POOL_W), lambda b: (b, 0))
    act = jax.ShapeDtypeStruct((nb * l, POOL_W), BF16)
    return pl.pallas_call(
        body, grid=(nb,),
        in_specs=[rows,
                  pl.BlockSpec((POOL_W, POOL_W), lambda b: (0, 0)),
                  pl.BlockSpec((1, POOL_W), lambda b: (0, 0))],
        out_specs=[rows, rows],
        out_shape=[act, act],
        scratch_shapes=[pltpu.VMEM((POOL_PAD + l, POOL_W), F32)],
        compiler_params=_params("arbitrary"), name="pool_fwd")(proj, w_blk, pool_scale)


def pool_bwd(d_b, d_ycat, w_blk, pool_scale, nb, l):
    def body(d_ref, dy_ref, w_ref, ps_ref, du_ref, dw_ref, dps_ref, padb_ref):
        b = pl.program_id(0)

        @pl.when(b == 0)
        def _():
            dw_ref[...] = jnp.zeros_like(dw_ref)
            dps_ref[...] = jnp.zeros_like(dps_ref)

        padb_ref[pl.ds(l, POOL_PAD), :] = jnp.zeros((POOL_PAD, POOL_W), F32)
        win = _pool_lane_window((1, POOL_W))
        db = d_ref[...]
        w = w_ref[...]
        dy = dy_ref[...].astype(F32)
        dps_ref[...] += jnp.sum(dy * _dot(db, w), axis=0, keepdims=True)
        dyo = (dy * ps_ref[...]).astype(BF16)
        dw_ref[...] += _dot_tn(db, dyo)
        dd = _dot_nt(dyo, w)
        e = _pool_div_count(dd, win)
        padb_ref[pl.ds(0, l), :] = e
        f2 = e + padb_ref[pl.ds(1, l), :]
        padb_ref[pl.ds(0, l), :] = f2
        f4 = f2 + padb_ref[pl.ds(2, l), :]
        padb_ref[pl.ds(0, l), :] = f4
        f8 = f4 + padb_ref[pl.ds(4, l), :]
        padb_ref[pl.ds(0, l), :] = f8
        f16 = f8 + padb_ref[pl.ds(8, l), :]
        du_ref[...] = (_pool_select(win, f2, f4, f8, f16) - dd).astype(BF16)

    return pl.pallas_call(
        body, grid=(nb,),
        in_specs=[pl.BlockSpec((l, POOL_W), lambda b: (b, 0)),
                  pl.BlockSpec((l, POOL_W), lambda b: (b, 0)),
                  pl.BlockSpec((POOL_W, POOL_W), lambda b: (0, 0)),
                  pl.BlockSpec((1, POOL_W), lambda b: (0, 0))],
        out_specs=[pl.BlockSpec((l, POOL_W), lambda b: (b, 0)),
                   pl.BlockSpec((POOL_W, POOL_W), lambda b: (0, 0)),
                   pl.BlockSpec((1, POOL_W), lambda b: (0, 0))],
        out_shape=[jax.ShapeDtypeStruct((nb * l, POOL_W), BF16),
                   jax.ShapeDtypeStruct((POOL_W, POOL_W), F32),
                   jax.ShapeDtypeStruct((1, POOL_W), F32)],
        scratch_shapes=[pltpu.VMEM((POOL_PAD + l, POOL_W), F32)],
        compiler_params=_params("arbitrary"), name="pool_bwd")(d_b, d_ycat, w_blk, pool_scale)


def _discretise(a_re, a_im, ldt, b_re, b_im):
    dt = jnp.exp(ldt)
    mag = jnp.exp(a_re * dt)
    th = a_im * dt
    lr = mag * jnp.cos(th)
    li = mag * jnp.sin(th)
    nr = lr - 1.0
    den = a_re * a_re + a_im * a_im
    fr = (nr * a_re + li * a_im) / den
    fi = (li * a_re - nr * a_im) / den
    return lr, li, fr * b_re - fi * b_im, fr * b_im + fi * b_re


def _cmul(ar, ai, br, bi):
    return ar * br - ai * bi, ar * bi + ai * br


def s5_scan_consts(a_re_row, a_im_row, ldt_row):
    def body(ar_ref, ai_ref, ld_ref, f_ref, b_ref):
        dt = jnp.exp(ld_ref[...])
        mag = jnp.exp(ar_ref[...] * dt)
        th = ai_ref[...] * dt
        lr = mag * jnp.cos(th)
        li = mag * jnp.sin(th)
        r = lax.broadcasted_iota(jnp.int32, (SUBLANES, NST), 0)
        for out_ref, sign, rev in ((f_ref, 1.0, False), (b_ref, -1.0, True)):
            p = [(lr, sign * li)]
            for _ in range(SUBLANES - 1):
                p.append(_cmul(p[-1][0], p[-1][1], lr, sign * li))
            for idx, s in enumerate((1, 2, 4)):
                inside = (r < SUBLANES - s) if rev else (r >= s)
                out_ref[pl.ds(2 * idx * SUBLANES, SUBLANES), :] = jnp.where(inside, p[s - 1][0], 0.0)
                out_ref[pl.ds((2 * idx + 1) * SUBLANES, SUBLANES), :] = jnp.where(inside, p[s - 1][1], 0.0)
            pr = jnp.zeros((SUBLANES, NST), F32)
            pi = jnp.zeros((SUBLANES, NST), F32)
            for row in range(SUBLANES):
                power = (SUBLANES - row) if rev else (row + 1)
                pr = jnp.where(r == row, p[power - 1][0], pr)
                pi = jnp.where(r == row, p[power - 1][1], pi)
            out_ref[pl.ds(6 * SUBLANES, SUBLANES), :] = pr
            out_ref[pl.ds(7 * SUBLANES, SUBLANES), :] = pi

    shape = jax.ShapeDtypeStruct((8 * SUBLANES, NST), F32)
    return pl.pallas_call(body, out_shape=[shape, shape], name="s5_scan_consts")(a_re_row, a_im_row, ldt_row)


def s5_bbar(a_re_row, a_im_row, ldt_row, b_re_t, b_im_t):
    def body(ar, ai, ld, br, bi, o_re, o_im):
        _, _, bbr, bbi = _discretise(ar[...], ai[...], ld[...], br[...], bi[...])
        o_re[...] = bbr
        o_im[...] = bbi

    shape = jax.ShapeDtypeStruct((SSM_GC, NST), F32)
    return pl.pallas_call(body, out_shape=[shape, shape], name="s5_bbar")(a_re_row, a_im_row, ldt_row, b_re_t, b_im_t)


def s5_param_bwd(a_re_row, a_im_row, ldt_row, b_re_t, b_im_t, d_lr, d_li, d_bbr, d_bbi):
    def body(ar, ai, ld, br, bi, g_lr, g_li, g_br, g_bi, o_ar, o_ai, o_ld, o_br, o_bi):
        _, vjp = jax.vjp(_discretise, ar[...], ai[...], ld[...], br[...], bi[...])
        d_ar, d_ai, d_ld, d_br, d_bi = vjp((g_lr[...], g_li[...], g_br[...], g_bi[...]))
        o_ar[...] = d_ar
        o_ai[...] = d_ai
        state = lax.broadcasted_iota(jnp.int32, (NST, LANES), 0)
        lane = lax.broadcasted_iota(jnp.int32, (NST, LANES), 1)
        in_group = jnp.where((state >= lane * SSM_N) & (state < (lane + 1) * SSM_N), 1.0, 0.0)
        o_ld[...] = jnp.dot(d_ld, in_group, precision=lax.Precision.HIGHEST, preferred_element_type=F32)
        o_br[...] = d_br
        o_bi[...] = d_bi

    row = jax.ShapeDtypeStruct((1, NST), F32)
    mat = jax.ShapeDtypeStruct((SSM_GC, NST), F32)
    grp = jax.ShapeDtypeStruct((1, LANES), F32)
    return pl.pallas_call(body, out_shape=[row, row, grp, mat, mat], name="s5_param_bwd")(
        a_re_row, a_im_row, ldt_row, b_re_t, b_im_t, d_lr, d_li, d_bbr, d_bbi)


def _scan_tiles(buf_ref, cst_ref, carry_ref, rows, reverse, reset, h_ref=None, acc_ref=None):
    n_tiles = rows // SUBLANES
    shifts = ((1, 0), (2, 2), (4, 4))
    row_id = lax.broadcasted_iota(jnp.int32, (SUBLANES, SCAN_COLS), 0)
    cols = [(j * SCAN_COLS, NST + j * SCAN_COLS) for j in range(NST // SCAN_COLS)]
    edge = 0 if reverse else SUBLANES - 1
    carry = jnp.where(reset, 0.0, carry_ref[...])
    accs = [(acc_ref[:, pl.ds(o_re, SCAN_COLS)], acc_ref[:, pl.ds(o_im, SCAN_COLS)]) for o_re, o_im in cols] if reverse else None
    for i in range(n_tiles):
        start = ((n_tiles - 1 - i) if reverse else i) * SUBLANES
        rws = pl.ds(start, SUBLANES)
        for j, (o_re, o_im) in enumerate(cols):
            c_re = pl.ds(o_re, SCAN_COLS)
            c_im = pl.ds(o_im, SCAN_COLS)
            xr = buf_ref[rws, c_re]
            xi = buf_ref[rws, c_im]
            for s, base in shifts:
                amount = (SUBLANES - s) if reverse else s
                sr = pltpu.roll(xr, amount, 0)
                si = pltpu.roll(xi, amount, 0)
                mr = cst_ref[pl.ds(base * SUBLANES, SUBLANES), c_re]
                mi = cst_ref[pl.ds((base + 1) * SUBLANES, SUBLANES), c_re]
                xr, xi = xr + (mr * sr - mi * si), xi + (mr * si + mi * sr)
            pr = cst_ref[pl.ds(6 * SUBLANES, SUBLANES), c_re]
            pi = cst_ref[pl.ds(7 * SUBLANES, SUBLANES), c_re]
            cr = carry[:, o_re:o_re + SCAN_COLS]
            ci = carry[:, o_im:o_im + SCAN_COLS]
            xr, xi = xr + (pr * cr - pi * ci), xi + (pr * ci + pi * cr)
            buf_ref[rws, c_re] = xr
            buf_ref[rws, c_im] = xi
            if reverse:
                gnr = jnp.where(row_id == SUBLANES - 1, cr, pltpu.roll(xr, SUBLANES - 1, 0))
                gni = jnp.where(row_id == SUBLANES - 1, ci, pltpu.roll(xi, SUBLANES - 1, 0))
                hr = h_ref[rws, c_re]
                hi = h_ref[rws, c_im]
                accs[j] = (accs[j][0] + (gnr * hr + gni * hi), accs[j][1] + (gni * hr - gnr * hi))
        carry = jnp.broadcast_to(buf_ref[pl.ds(start + edge, 1), :], (SUBLANES, 2 * NST))
    carry_ref[...] = carry
    if reverse:
        for (o_re, o_im), (a_re_, a_im_) in zip(cols, accs):
            acc_ref[:, pl.ds(o_re, SCAN_COLS)] = a_re_
            acc_ref[:, pl.ds(o_im, SCAN_COLS)] = a_im_


def _state_block(v, m):
    return jnp.concatenate([v[:, m * BLK_ST:(m + 1) * BLK_ST], v[:, NST + m * BLK_ST:NST + (m + 1) * BLK_ST]], axis=1)


def _put_state_block(buf_ref, m, val):
    buf_ref[:, pl.ds(m * BLK_ST, BLK_ST)] = val[:, :BLK_ST]
    buf_ref[:, pl.ds(NST + m * BLK_ST, BLK_ST)] = val[:, BLK_ST:]


def s5_fwd(proj, b_m, c_m, cst, d_skip, w_glu, nb, l):
    tt = min(S5_CHUNK, l)
    nt = l // tt
    n_chunks = nb * nt

    def body(u_ref, b_ref, c_ref, cst_ref, ds_ref, wg_ref, y_ref, h_ref, yp_ref, z_ref,
             buf0, buf1, ub0, ub1, carry_ref):
        s = pl.program_id(0)

        @pl.when(s == 0)
        def _():
            for r in (buf0, buf1, ub0, ub1, carry_ref):
                r[...] = jnp.zeros_like(r)

        def step(p_buf, p_u, q_buf):
            h = p_buf[...].astype(BF16)
            h_ref[...] = h
            ypre = (jnp.concatenate([_dot(_state_block(h, m), c_ref[m]) for m in range(N_BLK)], axis=1)
                    + ds_ref[...] * p_u[...])
            yp_ref[...] = ypre
            z = _dot(jax.nn.gelu(ypre).astype(BF16), wg_ref[...])
            z_ref[...] = z
            y_ref[...] = (z[:, :SSM_W] * jax.nn.sigmoid(z[:, SSM_W:])).astype(BF16)
            ub = u_ref[...]
            p_u[...] = ub.astype(F32)
            for m in range(N_BLK):
                _put_state_block(p_buf, m, _dot(ub[:, m * BLK_CH:(m + 1) * BLK_CH], b_ref[m]))
            _scan_tiles(q_buf, cst_ref, carry_ref, tt, False, lax.rem(s + nt - 1, nt) == 0)

        @pl.when(lax.rem(s, 2) == 0)
        def _():
            step(buf0, ub0, buf1)

        @pl.when(lax.rem(s, 2) == 1)
        def _():
            step(buf1, ub1, buf0)

    row_in = lambda s: (jnp.minimum(s, n_chunks - 1), 1)
    row_out = lambda s: (jnp.maximum(s - 2, 0), 0)
    const = lambda s: (0, 0)
    const3 = lambda s: (0, 0, 0)
    return pl.pallas_call(
        body, grid=(n_chunks + 2,),
        in_specs=[pl.BlockSpec((tt, SSM_W), row_in),
                  pl.BlockSpec((N_BLK, BLK_CH, 2 * BLK_ST), const3),
                  pl.BlockSpec((N_BLK, 2 * BLK_ST, BLK_CH), const3),
                  pl.BlockSpec((8 * SUBLANES, NST), const),
                  pl.BlockSpec((1, SSM_W), const),
                  pl.BlockSpec((SSM_W, 2 * SSM_W), const)],
        out_specs=[pl.BlockSpec((tt, SSM_W), row_out),
                   pl.BlockSpec((tt, 2 * NST), row_out),
                   pl.BlockSpec((tt, SSM_W), row_out),
                   pl.BlockSpec((tt, 2 * SSM_W), row_out)],
        out_shape=[jax.ShapeDtypeStruct((nb * l, SSM_W), BF16),
                   jax.ShapeDtypeStruct((nb * l, 2 * NST), BF16),
                   jax.ShapeDtypeStruct((nb * l, SSM_W), F32),
                   jax.ShapeDtypeStruct((nb * l, 2 * SSM_W), F32)],
        scratch_shapes=[pltpu.VMEM((tt, 2 * NST), F32), pltpu.VMEM((tt, 2 * NST), F32),
                        pltpu.VMEM((tt, SSM_W), F32), pltpu.VMEM((tt, SSM_W), F32),
                        pltpu.VMEM((SUBLANES, 2 * NST), F32)],
        compiler_params=_params("arbitrary"), name="s5_fwd")(proj, b_m, c_m, cst, d_skip, w_glu)


def s5_bwd(proj, d_ycat, h, ypre, z, b_m, c_m, cst_rev, d_skip, w_glu, nb, l):
    tt = min(S5_CHUNK, l)
    nt = l // tt
    n_chunks = nb * nt

    def body(u_ref, dy_ref, h_ref, yp_ref, z_ref, b_ref, c_ref, cst_ref, ds_ref, wg_ref,
             du_ref, dwg_ref, dds_ref, dcc_ref, dbb_ref, dlam_ref,
             buf0, buf1, hb0, hb1, ub0, ub1, dyp0, dyp1, carry_ref, dc_ref, db_ref):
        s = pl.program_id(0)

        @pl.when(s == 0)
        def _():
            for r in (buf0, buf1, hb0, hb1, ub0, ub1, dyp0, dyp1, carry_ref, dc_ref, db_ref, dwg_ref, dds_ref, dlam_ref):
                r[...] = jnp.zeros_like(r)

        def step(p_buf, p_h, p_u, p_dyp, q_buf, q_h):
            g = p_buf[...].astype(BF16)
            ub_done = p_u[...].astype(BF16)
            du = []
            for m in range(N_BLK):
                g_m = _state_block(g, m)
                db_ref[m] += _dot_tn(g_m, ub_done[:, m * BLK_CH:(m + 1) * BLK_CH])
                du.append(_dot_nt(g_m, b_ref[m]))
            du_ref[...] = (jnp.concatenate(du, axis=1) + ds_ref[...] * p_dyp[...]).astype(BF16)
            u = u_ref[...].astype(F32)
            ypre = yp_ref[...]
            z = z_ref[...]
            z1 = z[:, :SSM_W]
            sg = jax.nn.sigmoid(z[:, SSM_W:])
            dy = dy_ref[...].astype(F32) * jnp.where(s < n_chunks, 1.0, 0.0)
            dz = jnp.concatenate([dy * sg, dy * z1 * sg * (1.0 - sg)], axis=1).astype(BF16)
            yg, gelu_vjp = jax.vjp(jax.nn.gelu, ypre)
            dwg_ref[...] += _dot_tn(yg.astype(BF16), dz)
            dypre = gelu_vjp(_dot_nt(dz, wg_ref[...]))[0]
            dds_ref[...] += jnp.sum(dypre * u, axis=0, keepdims=True)
            dyb = dypre.astype(BF16)
            hb = h_ref[...]
            p_h[...] = hb.astype(F32)
            p_u[...] = u
            p_dyp[...] = dypre
            for m in range(N_BLK):
                dy_m = dyb[:, m * BLK_CH:(m + 1) * BLK_CH]
                dc_ref[m] += _dot_tn(_state_block(hb, m), dy_m)
                _put_state_block(p_buf, m, _dot_nt(dy_m, c_ref[m]))
            _scan_tiles(q_buf, cst_ref, carry_ref, tt, True, lax.rem(s + nt - 1, nt) == 0, h_ref=q_h, acc_ref=dlam_ref)

        @pl.when(lax.rem(s, 2) == 0)
        def _():
            step(buf0, hb0, ub0, dyp0, buf1, hb1)

        @pl.when(lax.rem(s, 2) == 1)
        def _():
            step(buf1, hb1, ub1, dyp1, buf0, hb0)

        @pl.when(s == n_chunks + 1)
        def _():
            dlam_ref[...] = jnp.broadcast_to(jnp.sum(dlam_ref[...], axis=0, keepdims=True), dlam_ref.shape)
            for acc_ref, out_ref in ((dc_ref, dcc_ref), (db_ref, dbb_ref)):
                for m in range(N_BLK):
                    for ri in range(2):
                        for gl in range(BLK_GROUPS):
                            out_ref[ri, pl.ds((m * BLK_GROUPS + gl) * SSM_N, SSM_N), :] = acc_ref[
                                m, pl.ds(ri * BLK_ST + gl * SSM_N, SSM_N), pl.ds(gl * SSM_GC, SSM_GC)]

    def chunk_rows(o):
        return lax.div(o, nt) * nt + (nt - 1 - lax.rem(o, nt))

    def rrow(col):
        return lambda s: (chunk_rows(jnp.minimum(s, n_chunks - 1)), col)

    const = lambda s: (0, 0)
    const3 = lambda s: (0, 0, 0)
    state_buf = pltpu.VMEM((tt, 2 * NST), F32)
    chan_buf = pltpu.VMEM((tt, SSM_W), F32)
    return pl.pallas_call(
        body, grid=(n_chunks + 2,),
        in_specs=[pl.BlockSpec((tt, SSM_W), rrow(1)),
                  pl.BlockSpec((tt, SSM_W), rrow(1)),
                  pl.BlockSpec((tt, 2 * NST), rrow(0)),
                  pl.BlockSpec((tt, SSM_W), rrow(0)),
                  pl.BlockSpec((tt, 2 * SSM_W), rrow(0)),
                  pl.BlockSpec((N_BLK, BLK_CH, 2 * BLK_ST), const3),
                  pl.BlockSpec((N_BLK, 2 * BLK_ST, BLK_CH), const3),
                  pl.BlockSpec((8 * SUBLANES, NST), const),
                  pl.BlockSpec((1, SSM_W), const),
                  pl.BlockSpec((SSM_W, 2 * SSM_W), const)],
        out_specs=[pl.BlockSpec((tt, SSM_W), lambda s: (chunk_rows(jnp.maximum(s - 2, 0)), 0)),
                   pl.BlockSpec((SSM_W, 2 * SSM_W), const),
                   pl.BlockSpec((1, SSM_W), const),
                   pl.BlockSpec((2, NST, SSM_GC), const3),
                   pl.BlockSpec((2, NST, SSM_GC), const3),
                   pl.BlockSpec((SUBLANES, 2 * NST), const)],
        out_shape=[jax.ShapeDtypeStruct((nb * l, SSM_W), BF16),
                   jax.ShapeDtypeStruct((SSM_W, 2 * SSM_W), F32),
                   jax.ShapeDtypeStruct((1, SSM_W), F32),
                   jax.ShapeDtypeStruct((2, NST, SSM_GC), F32),
                   jax.ShapeDtypeStruct((2, NST, SSM_GC), F32),
                   jax.ShapeDtypeStruct((SUBLANES, 2 * NST), F32)],
        scratch_shapes=[state_buf, state_buf, state_buf, state_buf, chan_buf, chan_buf, chan_buf, chan_buf,
                        pltpu.VMEM((SUBLANES, 2 * NST), F32),
                        pltpu.VMEM((N_BLK, 2 * BLK_ST, BLK_CH), F32), pltpu.VMEM((N_BLK, 2 * BLK_ST, BLK_CH), F32)],
        compiler_params=_params("arbitrary"), name="s5_bwd")(
            proj, d_ycat, h, ypre, z, b_m, c_m, cst_rev, d_skip, w_glu)


def _head_mask(hh):
    lane = lax.broadcasted_iota(jnp.int32, (1, ATT_W), 1)
    return (lane >= hh * MEM_HD) & (lane < (hh + 1) * MEM_HD)


def _mem_kv(mem_ref, gm_ref, wkv_ref):
    m = mem_ref[0]
    mh = m * _rms_scale(m)
    mb = (mh * gm_ref[...]).astype(BF16)
    kv = _dot(mb, wkv_ref[...])
    return mh, mb, kv[:, :ATT_W].astype(BF16), kv[:, ATT_W:].astype(BF16)


def _stack_heads(dst_ref, a):
    for hh in range(MEM_HEADS):
        dst_ref[pl.ds(hh * N_MEM, N_MEM), :] = jnp.where(_head_mask(hh), a, jnp.zeros_like(a))


def _fold_heads(a):
    out = jnp.zeros((N_MEM, ATT_W), a.dtype)
    for hh in range(MEM_HEADS):
        out = out + jnp.where(_head_mask(hh), a[hh * N_MEM:(hh + 1) * N_MEM], 0.0)
    return out


def _softmax_heads(s):
    parts = []
    for hh in range(MEM_HEADS):
        sh = s[:, hh * N_MEM:(hh + 1) * N_MEM]
        e = jnp.exp(sh - jnp.max(sh, axis=-1, keepdims=True))
        parts.append(e / jnp.sum(e, axis=-1, keepdims=True))
    return jnp.concatenate(parts, axis=1)


def attn_fwd(proj, mem, g_mem, w_kv, nb, l):
    tq = min(ATTN_ROWS, l)
    nq = l // tq
    scale = MEM_HD ** -0.5

    def body(q_ref, mem_ref, gm_ref, wkv_ref, o_ref, p_ref, k_s, v_s):
        @pl.when(pl.program_id(1) == 0)
        def _():
            _, _, k, v = _mem_kv(mem_ref, gm_ref, wkv_ref)
            _stack_heads(k_s, k)
            _stack_heads(v_s, v)

        p = _softmax_heads(_dot_nt(q_ref[...], k_s[...]) * scale).astype(BF16)
        p_ref[...] = p
        o_ref[...] = _dot(p, v_s[...]).astype(BF16)

    const = lambda b, t: (0, 0)
    stacked = pltpu.VMEM((MEM_HEADS * N_MEM, ATT_W), BF16)
    return pl.pallas_call(
        body, grid=(nb, nq),
        in_specs=[pl.BlockSpec((tq, ATT_W), lambda b, t: (b * nq + t, 3)),
                  pl.BlockSpec((1, N_MEM, D_MODEL), lambda b, t: (b, 0, 0)),
                  pl.BlockSpec((1, D_MODEL), const),
                  pl.BlockSpec((D_MODEL, 2 * ATT_W), const)],
        out_specs=[pl.BlockSpec((tq, ATT_W), lambda b, t: (b * nq + t, 0)),
                   pl.BlockSpec((tq, MEM_HEADS * N_MEM), lambda b, t: (b * nq + t, 0))],
        out_shape=[jax.ShapeDtypeStruct((nb * l, ATT_W), BF16),
                   jax.ShapeDtypeStruct((nb * l, MEM_HEADS * N_MEM), BF16)],
        scratch_shapes=[stacked, stacked],
        compiler_params=_params("arbitrary", "arbitrary"), name="attn_fwd")(proj, mem, g_mem, w_kv)


def attn_bwd(proj, d_ycat, p_att, mem, g_mem, w_kv, nb, l):
    tq = min(ATTN_ROWS, l)
    nq = l // tq
    scale = MEM_HD ** -0.5

    def body(q_ref, do_ref, p_ref, mem_ref, gm_ref, wkv_ref, dq_ref, dwkv_ref, dgm_ref, k_s, v_s, mb_s, dk_s, dv_s):
        b = pl.program_id(0)
        t = pl.program_id(1)

        @pl.when((b == 0) & (t == 0))
        def _():
            dwkv_ref[...] = jnp.zeros_like(dwkv_ref)
            dgm_ref[...] = jnp.zeros_like(dgm_ref)

        @pl.when(t == 0)
        def _():
            _, mb, k, v = _mem_kv(mem_ref, gm_ref, wkv_ref)
            _stack_heads(k_s, k)
            _stack_heads(v_s, v)
            mb_s[...] = mb
            dk_s[...] = jnp.zeros_like(dk_s)
            dv_s[...] = jnp.zeros_like(dv_s)

        q = q_ref[...]
        do = do_ref[...]
        k = k_s[...]
        pb = p_ref[...]
        p = pb.astype(F32)
        dp = _dot_nt(do, v_s[...])
        ds = []
        for hh in range(MEM_HEADS):
            blk = slice(hh * N_MEM, (hh + 1) * N_MEM)
            ds.append(p[:, blk] * (dp[:, blk] - jnp.sum(dp[:, blk] * p[:, blk], axis=-1, keepdims=True)) * scale)
        ds = jnp.concatenate(ds, axis=1).astype(BF16)
        dq_ref[...] = _dot(ds, k).astype(BF16)
        dk_s[...] += _dot_tn(ds, q)
        dv_s[...] += _dot_tn(pb, do)

        @pl.when(t == nq - 1)
        def _():
            dkv = jnp.concatenate([_fold_heads(dk_s[...]), _fold_heads(dv_s[...])], axis=1).astype(BF16)
            dwkv_ref[...] += _dot_tn(mb_s[...], dkv)
            m = mem_ref[0]
            dgm_ref[...] += jnp.sum(_dot_nt(dkv, wkv_ref[...]) * (m * _rms_scale(m)), axis=0, keepdims=True)

    const = lambda b, t: (0, 0)
    return pl.pallas_call(
        body, grid=(nb, nq),
        in_specs=[pl.BlockSpec((tq, ATT_W), lambda b, t: (b * nq + t, 3)),
                  pl.BlockSpec((tq, ATT_W), lambda b, t: (b * nq + t, 3)),
                  pl.BlockSpec((tq, MEM_HEADS * N_MEM), lambda b, t: (b * nq + t, 0)),
                  pl.BlockSpec((1, N_MEM, D_MODEL), lambda b, t: (b, 0, 0)),
                  pl.BlockSpec((1, D_MODEL), const),
                  pl.BlockSpec((D_MODEL, 2 * ATT_W), const)],
        out_specs=[pl.BlockSpec((tq, ATT_W), lambda b, t: (b * nq + t, 0)),
                   pl.BlockSpec((D_MODEL, 2 * ATT_W), const),
                   pl.BlockSpec((1, D_MODEL), const)],
        out_shape=[jax.ShapeDtypeStruct((nb * l, ATT_W), BF16),
                   jax.ShapeDtypeStruct((D_MODEL, 2 * ATT_W), F32),
                   jax.ShapeDtypeStruct((1, D_MODEL), F32)],
        scratch_shapes=[pltpu.VMEM((MEM_HEADS * N_MEM, ATT_W), BF16), pltpu.VMEM((MEM_HEADS * N_MEM, ATT_W), BF16),
                        pltpu.VMEM((N_MEM, D_MODEL), BF16),
                        pltpu.VMEM((MEM_HEADS * N_MEM, ATT_W), F32), pltpu.VMEM((MEM_HEADS * N_MEM, ATT_W), F32)],
        compiler_params=_params("arbitrary", "arbitrary"), name="attn_bwd")(proj, d_ycat, p_att, mem, g_mem, w_kv)


def tail(y_pool, y_ssm, y_att, proj, x2, target, w_out, g_post):
    t = x2.shape[0]
    tm = min(ROW_TILE, t)

    def body(yp_ref, ys_ref, ya_ref, gate_ref, x_ref, tg_ref, w_ref, gp_ref,
             dz_ref, dyc_ref, dgate_ref, dw_ref, dgp_ref, loss_ref):
        @pl.when(pl.program_id(0) == 0)
        def _():
            dw_ref[...] = jnp.zeros_like(dw_ref)
            dgp_ref[...] = jnp.zeros_like(dgp_ref)
            loss_ref[...] = jnp.zeros_like(loss_ref)

        ycat = jnp.concatenate([yp_ref[...], ys_ref[...], ya_ref[...]], axis=1).astype(F32)
        gate = gate_ref[...].astype(F32)
        sg = jax.nn.sigmoid(gate)
        silu = gate * sg
        yb = (ycat * silu).astype(BF16)
        w = w_ref[...]
        out = _dot(yb, w)
        r2 = _rms_scale(out)
        oh = out * r2
        gp = gp_ref[...]
        err = (x_ref[...] + oh * gp) - tg_ref[...]
        loss_ref[...] += 0.5 * jnp.sum(jnp.mean(err * err, axis=-1, keepdims=True), axis=0, keepdims=True)
        dz = err * (1.0 / D_MODEL)
        dz_ref[...] = dz.astype(BF16)
        dgp_ref[...] += jnp.sum(dz * oh, axis=0, keepdims=True)
        dn = dz * gp
        dout = (r2 * (dn - oh * jnp.mean(dn * oh, axis=-1, keepdims=True))).astype(BF16)
        dw_ref[...] += _dot_tn(yb, dout)
        dy = _dot_nt(dout, w)
        dyc_ref[...] = (dy * silu).astype(BF16)
        dgate_ref[...] = (dy * ycat * (sg * (1.0 + gate * (1.0 - sg)))).astype(BF16)

    row = lambda i: (i, 0)
    const = lambda i: (0, 0)
    full = jax.ShapeDtypeStruct((t, D_MODEL), BF16)
    return pl.pallas_call(
        body, grid=(t // tm,),
        in_specs=[pl.BlockSpec((tm, POOL_W), row),
                  pl.BlockSpec((tm, SSM_W), row),
                  pl.BlockSpec((tm, ATT_W), row),
                  pl.BlockSpec((tm, D_MODEL), lambda i: (i, 1)),
                  pl.BlockSpec((tm, D_MODEL), row),
                  pl.BlockSpec((tm, D_MODEL), row),
                  pl.BlockSpec((D_MODEL, D_MODEL), const),
                  pl.BlockSpec((1, D_MODEL), const)],
        out_specs=[pl.BlockSpec((tm, D_MODEL), row),
                   pl.BlockSpec((tm, D_MODEL), row),
                   pl.BlockSpec((tm, D_MODEL), row),
                   pl.BlockSpec((D_MODEL, D_MODEL), const),
                   pl.BlockSpec((1, D_MODEL), const),
                   pl.BlockSpec((1, LANES), const)],
        out_shape=[full, full, full,
                   jax.ShapeDtypeStruct((D_MODEL, D_MODEL), F32),
                   jax.ShapeDtypeStruct((1, D_MODEL), F32),
                   jax.ShapeDtypeStruct((1, LANES), F32)],
        compiler_params=_params("arbitrary"), name="tail")(y_pool, y_ssm, y_att, proj, x2, target, w_out, g_post)


def in_proj_bwd(du_pool, du_ssm, dq, d_gate, x2, dz, g_pre, w_in):
    t = x2.shape[0]
    tm = min(ROW_TILE, t)
    shard = W_IN_SHARD

    def body(dup_ref, dus_ref, dq_ref, dg_ref, x_ref, dz_ref, g_ref, w_ref, gx_ref, dw_ref, dgp_ref):
        @pl.when(pl.program_id(0) == 0)
        def _():
            dw_ref[...] = jnp.zeros_like(dw_ref)
            dgp_ref[...] = jnp.zeros_like(dgp_ref)

        dproj = jnp.concatenate([dup_ref[...], dus_ref[...], dq_ref[...], dg_ref[...]], axis=1)
        x = x_ref[...]
        r1 = _rms_scale(x)
        xh = x * r1
        g = g_ref[...]
        hb = (xh * g).astype(BF16)
        dh = jnp.zeros((tm, D_MODEL), F32)
        for k in range(N_CHIPS):
            dp_k = dproj[:, k * shard:(k + 1) * shard]
            dw_ref[k] += _dot_tn(hb, dp_k)
            dh = dh + _dot_nt(dp_k, w_ref[k])
        dgp_ref[...] += jnp.sum(dh * xh, axis=0, keepdims=True)
        dn = dh * g
        gx_ref[...] = dz_ref[...].astype(F32) + r1 * (dn - xh * jnp.mean(dn * xh, axis=-1, keepdims=True))

    row = lambda i: (i, 0)
    const = lambda i: (0, 0)
    return pl.pallas_call(
        body, grid=(t // tm,),
        in_specs=[pl.BlockSpec((tm, POOL_W), row),
                  pl.BlockSpec((tm, SSM_W), row),
                  pl.BlockSpec((tm, ATT_W), row),
                  pl.BlockSpec((tm, D_MODEL), row),
                  pl.BlockSpec((tm, D_MODEL), row),
                  pl.BlockSpec((tm, D_MODEL), row),
                  pl.BlockSpec((1, D_MODEL), const),
                  pl.BlockSpec((N_CHIPS, D_MODEL, shard), lambda i: (0, 0, 0))],
        out_specs=[pl.BlockSpec((tm, D_MODEL), row),
                   pl.BlockSpec((N_CHIPS, D_MODEL, shard), lambda i: (0, 0, 0)),
                   pl.BlockSpec((1, D_MODEL), const)],
        out_shape=[jax.ShapeDtypeStruct((t, D_MODEL), F32),
                   jax.ShapeDtypeStruct((N_CHIPS, D_MODEL, shard), F32),
                   jax.ShapeDtypeStruct((1, D_MODEL), F32)],
        compiler_params=_params("arbitrary"), name="in_proj_bwd")(du_pool, du_ssm, dq, d_gate, x2, dz, g_pre, w_in)


def _block_diag(blocks):
    g, r, c = blocks.shape
    eye = jnp.eye(g, dtype=blocks.dtype)
    return jnp.einsum("grc,gh->grhc", blocks, eye).reshape(g * r, g * c)


def _block_diag_extract(mat, g, r, c):
    eye = jnp.eye(g, dtype=mat.dtype)
    return jnp.einsum("grhc,gh->grc", mat.reshape(g, r, g, c), eye)


def local_step(x, mem, target, g_pre, w_in_shard, w_pool, pool_scale, a_re, a_im, log_dt, b_re, b_im, c_re, c_im,
               d_skip, later_shards, g_mem, g_post):
    nb, l, _ = x.shape
    x2 = x.reshape(nb * l, D_MODEL)
    tg2 = target.reshape(nb * l, D_MODEL)

    rowv = lambda a: a.reshape(1, NST)
    lanes_last = lambda b: b.transpose(2, 0, 1).reshape(SSM_GC, NST)
    ldt_row = rowv(jnp.broadcast_to(log_dt.reshape(SSM_NG, 1), (SSM_NG, SSM_N)))
    b_re_t = lanes_last(b_re)
    b_im_t = lanes_last(b_im)
    cst_f, cst_b = s5_scan_consts(rowv(a_re), rowv(a_im), ldt_row)
    bbr, bbi = s5_bbar(rowv(a_re), rowv(a_im), ldt_row, b_re_t, b_im_t)
    eye = jnp.eye(BLK_GROUPS, dtype=F32)
    blk_in = lambda bb: jnp.einsum("cmgn,gh->mhcgn", bb.reshape(SSM_GC, N_BLK, BLK_GROUPS, SSM_N), eye).reshape(
        N_BLK, BLK_CH, BLK_ST)
    blk_out = lambda cc: jnp.einsum("mgcn,gh->mgnhc", cc.reshape(N_BLK, BLK_GROUPS, SSM_GC, SSM_N), eye).reshape(
        N_BLK, BLK_ST, BLK_CH)
    b_m = jnp.concatenate([blk_in(bbr), blk_in(bbi)], axis=2).astype(BF16)
    c_m = jnp.concatenate([blk_out(c_re), -blk_out(c_im)], axis=1).astype(BF16)
    w_pool_blk = _block_diag(w_pool.reshape(4, POOL_GW, POOL_GW)).astype(BF16)

    proj, w_in, (g_glu, g_kv, g_out) = in_proj(x2, g_pre, w_in_shard, later_shards)
    w_glu = g_glu.transpose(2, 0, 1).reshape(SSM_W, 2 * SSM_W)
    w_kv = g_kv.reshape(D_MODEL, 2 * ATT_W)
    w_out = g_out.reshape(D_MODEL, D_MODEL)
    y_pool, pool_d = pool_fwd(proj, w_pool_blk, pool_scale, nb, l)
    y_ssm, h, ypre, z = s5_fwd(proj, b_m, c_m, cst_f, d_skip, w_glu, nb, l)
    y_att, p_att = attn_fwd(proj, mem, g_mem, w_kv, nb, l)
    dz, d_ycat, d_gate, d_w_out, d_g_post, loss = tail(y_pool, y_ssm, y_att, proj, x2, tg2, w_out, g_post)
    du_pool, d_wp_blk, d_pool_scale = pool_bwd(pool_d, d_ycat, w_pool_blk, pool_scale, nb, l)
    du_ssm, d_w_glu, d_d_skip, d_cc, d_bb, d_lam = s5_bwd(
        proj, d_ycat, h, ypre, z, b_m, c_m, cst_b, d_skip, w_glu, nb, l)
    dq, d_w_kv, d_g_mem = attn_bwd(proj, d_ycat, p_att, mem, g_mem, w_kv, nb, l)
    grad_x, d_w_in, d_g_pre = in_proj_bwd(du_pool, du_ssm, dq, d_gate, x2, dz, g_pre, w_in)

    d_c = d_cc[0].reshape(SSM_NG, SSM_N, SSM_GC)
    d_ci = d_cc[1].reshape(SSM_NG, SSM_N, SSM_GC)
    d_lam_row = d_lam[0]
    d_ar, d_ai, d_ld, d_br, d_bi = s5_param_bwd(
        rowv(a_re), rowv(a_im), ldt_row, b_re_t, b_im_t,
        d_lam_row[:NST].reshape(1, NST), d_lam_row[NST:].reshape(1, NST), d_bb[0].T, d_bb[1].T)
    from_lanes_last = lambda b: b.reshape(SSM_GC, SSM_NG, SSM_N).transpose(1, 0, 2).reshape(1, SSM_NG, SSM_GC, SSM_N)

    grads = {
        "g_pre": d_g_pre,
        "w_in": d_w_in,
        "w_pool": _block_diag_extract(d_wp_blk, 4, POOL_GW, POOL_GW).reshape(1, 4, POOL_GW, POOL_GW),
        "pool_scale": d_pool_scale,
        "a_re": d_ar.reshape(1, SSM_NG, SSM_N),
        "a_im": d_ai.reshape(1, SSM_NG, SSM_N),
        "log_dt": d_ld[:, :SSM_NG],
        "b_re": from_lanes_last(d_br),
        "b_im": from_lanes_last(d_bi),
        "c_re": d_c.transpose(0, 2, 1).reshape(1, SSM_NG, SSM_GC, SSM_N),
        "c_im": (-d_ci).transpose(0, 2, 1).reshape(1, SSM_NG, SSM_GC, SSM_N),
        "d_skip": d_d_skip,
        "w_glu": d_w_glu,
        "g_mem": d_g_mem,
        "w_kv": d_w_kv,
        "w_out": d_w_out,
        "g_post": d_g_post,
    }
    return loss, grad_x.reshape(nb, l, D_MODEL), grads


VMEM_SPEC = pl.BlockSpec(memory_space=pltpu.VMEM)
ANY_SPEC = pl.BlockSpec(memory_space=pl.ANY)


def _place():
    return lax.axis_index("x"), lax.axis_index("y"), lax.axis_index("c")


def _other_chips(x, y):
    return [(1 - x, y), (x, 1 - y), (1 - x, 1 - y)]


SEMS_PER_ITEM = 5


def reduce_all(g4s, packs, pack_dtypes):
    n = len(g4s)
    dims = [g.shape[1:] for g in g4s]
    np_ = len(packs)
    prs = [p.shape[0] for p in packs]

    def body(*refs):
        g_refs, p_refs = refs[:n], refs[n:n + np_]
        outs, ops = refs[n + np_:2 * n + np_], refs[2 * n + np_:2 * (n + np_)]
        scr = refs[2 * (n + np_):]
        mine, theirs, sendb, recvb = scr[0:n], scr[n:2 * n], scr[2 * n:3 * n], scr[3 * n:4 * n]
        p_theirs, gats = scr[4 * n:4 * n + np_], scr[4 * n + np_:4 * n + 2 * np_]
        lsems, send_sems, recv_sems = scr[4 * n + 2 * np_:]
        x, y, c = _place()
        k = 2 * x + y
        chips = _other_chips(x, y)
        sib = (x, y, 1 - c)

        def halves(rows):
            hr = rows // 2
            return (pl.ds(pl.multiple_of(c * hr, SUBLANES), hr), pl.ds(pl.multiple_of((1 - c) * hr, SUBLANES), hr))

        def remote(src, dst, sem, to):
            return pltpu.make_async_remote_copy(
                src_ref=src, dst_ref=dst, send_sem=send_sems.at[sem], recv_sem=recv_sems.at[sem],
                device_id=to, device_id_type=MESH)

        loads, started = [], []
        for i in range(n):
            my_rows, sib_rows = halves(dims[i][0])
            ld = pltpu.make_async_copy(g_refs[i].at[:, my_rows, :], mine[i], lsems.at[i])
            ld.start()
            loads.append(ld)
            sw = remote(g_refs[i].at[:, sib_rows, :], theirs[i], SEMS_PER_ITEM * i, sib)
            sw.start()
            started.append(sw)
        p_swaps = []
        for q in range(np_):
            _, p_sib = halves(prs[q])
            sw = remote(p_refs[q].at[p_sib, :], p_theirs[q], SEMS_PER_ITEM * (n + q), sib)
            sw.start()
            p_swaps.append(sw)
        started += p_swaps

        for q in range(np_):
            p_my, _ = halves(prs[q])
            p_swaps[q].wait_recv()
            gats[q][k] = (p_refs[q][p_my, :] + p_theirs[q][...]).astype(gats[q].dtype)
            for j, (px, py) in enumerate(chips):
                cp = remote(gats[q].at[k], gats[q].at[k], SEMS_PER_ITEM * (n + q) + 1 + j, (px, py, c))
                cp.start()
                started.append(cp)
        ici = []
        for i in range(n):
            loads[i].wait()
            started[i].wait_recv()
            for j, (px, py) in enumerate(chips):
                kp = 2 * px + py
                sendb[i][j] = (mine[i][kp] + theirs[i][kp]).astype(BF16)
                cp = remote(sendb[i].at[j], recvb[i].at[j], SEMS_PER_ITEM * i + 1 + j, (px, py, c))
                cp.start()
                ici.append(cp)

        for q in range(np_):
            p_my, _ = halves(prs[q])
            gat, op = gats[q], ops[q]
            for j, (px, py) in enumerate(chips):
                kp = 2 * px + py
                remote(gat.at[kp], gat.at[kp], SEMS_PER_ITEM * (n + q) + 1 + j, (px, py, c)).wait_recv()
            op[p_my, :] = ((gat[0].astype(F32) + gat[1].astype(F32)) + gat[2].astype(F32)) + gat[3].astype(F32)
            last = remote(op.at[p_my, :], op.at[p_my, :], SEMS_PER_ITEM * (n + q) + 4, sib)
            last.start()
            started.append(last)
        for i in range(n):
            my_rows, _ = halves(dims[i][0])
            half = mine[i][k] + theirs[i][k]
            for j in range(3):
                ici[3 * i + j].wait_recv()
                half = half + recvb[i][j].astype(F32)
            outs[i][my_rows, :] = half
            last = remote(outs[i].at[my_rows, :], outs[i].at[my_rows, :], SEMS_PER_ITEM * i + 4, sib)
            last.start()
            started.append(last)
        for q in range(np_):
            _, p_sib = halves(prs[q])
            remote(ops[q].at[p_sib, :], ops[q].at[p_sib, :], SEMS_PER_ITEM * (n + q) + 4, sib).wait_recv()
        for i in range(n):
            _, sib_rows = halves(dims[i][0])
            remote(outs[i].at[sib_rows, :], outs[i].at[sib_rows, :], SEMS_PER_ITEM * i + 4, sib).wait_recv()
        for cp in started + ici:
            cp.wait_send()

    n_sems = SEMS_PER_ITEM * (n + np_)
    scratch = ([pltpu.VMEM((N_CHIPS, r // 2, cd), F32) for r, cd in dims] * 2
               + [pltpu.VMEM((3, r // 2, cd), BF16) for r, cd in dims] * 2
               + [pltpu.VMEM((pr // 2, LANES), F32) for pr in prs]
               + [pltpu.VMEM((N_CHIPS, pr // 2, LANES), dt) for pr, dt in zip(prs, pack_dtypes)]
               + [pltpu.SemaphoreType.DMA((n,)), pltpu.SemaphoreType.DMA((n_sems,)), pltpu.SemaphoreType.DMA((n_sems,))])
    res = pl.pallas_call(
        body,
        out_shape=[jax.ShapeDtypeStruct(d, F32) for d in dims] + [jax.ShapeDtypeStruct(p.shape, F32) for p in packs],
        in_specs=[ANY_SPEC] * n + [VMEM_SPEC] * np_, out_specs=[VMEM_SPEC] * (n + np_),
        scratch_shapes=scratch,
        compiler_params=pltpu.CompilerParams(vmem_limit_bytes=VMEM_LIMIT),
        name="reduce_all")(*g4s, *packs)
    return res[:n], res[n:]


def adamw_all(ws, gs, ms, vs):
    n = len(ws)
    chunked = [a.ndim == 3 and a.shape[0] == 1 and a.shape[1] % (ADAMW_STEPS * SUBLANES) == 0 for a in ws]

    def body(*refs):
        w, g, m, v = refs[:n], refs[n:2 * n], refs[2 * n:3 * n], refs[3 * n:4 * n]
        od, om, ov = refs[4 * n:5 * n], refs[5 * n:6 * n], refs[6 * n:]
        for i in range(n):
            if chunked[i]:
                od[i][...], om[i][...], ov[i][...] = _adamw(w[i][...], g[i][...], m[i][...], v[i][...])

        @pl.when(pl.program_id(0) == 0)
        def _():
            for i in range(n):
                if not chunked[i]:
                    od[i][...], om[i][...], ov[i][...] = _adamw(w[i][...], g[i][...], m[i][...], v[i][...])

    def spec(a, is_chunked):
        if is_chunked:
            return pl.BlockSpec((1, a.shape[1] // ADAMW_STEPS, a.shape[2]), lambda s: (0, s, 0))
        return pl.BlockSpec(a.shape, lambda s: (0,) * a.ndim)

    specs = [spec(a, c) for a, c in zip(ws, chunked)]
    shapes = [jax.ShapeDtypeStruct(a.shape, F32) for a in ws]
    res = pl.pallas_call(
        body, grid=(ADAMW_STEPS,), out_shape=shapes * 3, in_specs=specs * 4, out_specs=specs * 3,
        compiler_params=_params("arbitrary"), name="adamw_all")(*ws, *gs, *ms, *vs)
    return res[:n], res[n:2 * n], res[2 * n:]


WEIGHTS = ("g_pre", "w_in", "w_pool", "pool_scale", "a_re", "a_im", "log_dt", "b_re", "b_im", "c_re", "c_im", "d_skip",
           "w_glu", "g_mem", "w_kv", "w_out", "g_post")
SHARDED = ("w_in", "w_glu", "w_kv", "w_out")
SWAPPED = ("b_re", "b_im", "w_glu")
MATRICES = ("w_pool", "b_re", "b_im", "c_re", "c_im")
REPLICATED = tuple(n for n in WEIGHTS if n not in SHARDED)
PACK_TILE = SUBLANES * LANES


def _pack(arrays):
    parts = []
    for a in arrays:
        flat = a.reshape(-1)
        parts.append(jnp.pad(flat, (0, -flat.shape[0] % PACK_TILE)).reshape(-1, LANES))
    rows = sum(p.shape[0] for p in parts)
    if rows % (2 * SUBLANES):
        parts.append(jnp.zeros((SUBLANES, LANES), F32))
    return jnp.concatenate(parts, axis=0)


def _unpack(packed, shapes):
    out, row = [], 0
    for shp in shapes:
        size = math.prod(shp)
        rows = -(-size // PACK_TILE) * SUBLANES
        out.append(packed[row:row + rows].reshape(-1)[:size].reshape(shp))
        row += rows
    return out


def kernel(x, mem, g_pre, w_in, w_pool, pool_scale, a_re, a_im, log_dt, b_re, b_im, c_re, c_im, d_skip, w_glu, g_mem, w_kv, w_out, g_post, loss_target, m_g_pre, m_w_in, m_w_pool, m_pool_scale, m_a_re, m_a_im, m_log_dt, m_b_re, m_b_im, m_c_re, m_c_im, m_d_skip, m_w_glu, m_g_mem, m_w_kv, m_w_out, m_g_post, v_g_pre, v_w_in, v_w_pool, v_pool_scale, v_a_re, v_a_im, v_log_dt, v_b_re, v_b_im, v_c_re, v_c_im, v_d_skip, v_w_glu, v_g_mem, v_w_kv, v_w_out, v_g_post):
    given = dict(locals())
    wts = {n: given[n] for n in WEIGHTS}
    mom = {n: given["m_" + n] for n in WEIGHTS}
    var = {n: given["v_" + n] for n in WEIGHTS}

    swap = lambda a: jnp.swapaxes(a, -1, -2)
    view = lambda n, a: swap(a) if n in SWAPPED else a

    loss_part, grad_x, grads = local_step(
        x, mem, loss_target, g_pre, w_in[0], w_pool[0], pool_scale, a_re[0], a_im[0], log_dt[0], b_re[0], b_im[0],
        c_re[0], c_im[0], d_skip, [swap(w_glu)[0], w_kv[0], w_out[0]], g_mem, g_post)

    glu_cols = 2 * SSM_W // N_CHIPS
    partial4 = [
        grads["w_in"],
        grads["w_glu"].reshape(SSM_W, N_CHIPS, glu_cols).transpose(1, 2, 0),
        grads["w_kv"].reshape(N_CHIPS, D_MODEL // N_CHIPS, 2 * ATT_W),
        grads["w_out"].reshape(N_CHIPS, D_MODEL // N_CHIPS, D_MODEL),
    ]
    loss_tile = jnp.broadcast_to(loss_part, (SUBLANES, LANES))
    vectors = [n for n in REPLICATED if n not in MATRICES]
    sharded_g, (pack_v, pack_m) = reduce_all(
        partial4, [_pack([grads[n] for n in vectors] + [loss_tile]), _pack([grads[n] for n in MATRICES])], [F32, BF16])
    vec_g = _unpack(pack_v, [view(n, wts[n]).shape for n in vectors] + [(SUBLANES, LANES)])
    mat_g = _unpack(pack_m, [view(n, wts[n]).shape for n in MATRICES])
    loss = vec_g[-1][0, 0]
    grad = dict(zip(SHARDED, [g[None] for g in sharded_g]))
    grad.update(zip(vectors, vec_g[:-1]))
    grad.update(zip(MATRICES, mat_g))

    deltas, new_ms, new_vs = adamw_all([view(n, wts[n]) for n in WEIGHTS], [grad[n] for n in WEIGHTS],
                                       [view(n, mom[n]) for n in WEIGHTS], [view(n, var[n]) for n in WEIGHTS])
    back = lambda arrs: [view(n, a) for n, a in zip(WEIGHTS, arrs)]
    return (loss, grad_x, *back([grad[n] for n in WEIGHTS]), *back(deltas), *back(new_ms), *back(new_vs))
```

```python
import math

import jax
import jax.numpy as jnp
from jax import lax
from jax.experimental import pallas as pl
from jax.experimental.pallas import tpu as pltpu

F32 = jnp.float32
BF16 = jnp.bfloat16

D_MODEL = 1024
POOL_W = 384
SSM_W = 384
ATT_W = 256
POOL_GW = 96
POOL_PAD = 16
SSM_NG = 24
SSM_N = 64
SSM_GC = 16
N_CHIPS = 4
W_IN_SHARD = 2 * D_MODEL // N_CHIPS
NST = SSM_NG * SSM_N
BLK_CH = 128
BLK_GROUPS = BLK_CH // SSM_GC
BLK_ST = BLK_GROUPS * SSM_N
N_BLK = SSM_W // BLK_CH
N_MEM = 256
MEM_HEADS = 4
MEM_HD = 64
EPS = 1e-6

ADAM_LR = 0.001
ADAM_B1 = 0.9
ADAM_B2 = 0.999
ADAM_EPS = 1e-08
ADAM_WD = 0.01
ADAM_STEP = 10

SUBLANES = 8
LANES = 128
V7X_VMEM_BYTES = 64 * 2**20
VMEM_LIMIT = V7X_VMEM_BYTES - 8 * 2**20
SCAN_COLS = 512
ROW_TILE = 512
IN_PROJ_ROWS = 2048
ATTN_ROWS = 2048
S5_CHUNK = 256
ADAMW_STEPS = 4
MESH = pl.DeviceIdType.MESH

NT = (((1,), (1,)), ((), ()))
TN = (((0,), (0,)), ((), ()))


def _params(*sem):
    return pltpu.CompilerParams(dimension_semantics=sem, vmem_limit_bytes=VMEM_LIMIT)


def _dot(a, b):
    return jnp.dot(a, b, preferred_element_type=F32)


def _dot_nt(a, b):
    return lax.dot_general(a, b, NT, preferred_element_type=F32)


def _dot_tn(a, b):
    return lax.dot_general(a, b, TN, preferred_element_type=F32)


def _rms_scale(v):
    return lax.rsqrt(jnp.mean(v * v, axis=-1, keepdims=True) + EPS)


def _adamw(w, g, m, v):
    m = ADAM_B1 * m + (1.0 - ADAM_B1) * g
    v = ADAM_B2 * v + (1.0 - ADAM_B2) * (g * g)
    m_hat = m / (1.0 - ADAM_B1 ** ADAM_STEP)
    v_hat = v / (1.0 - ADAM_B2 ** ADAM_STEP)
    delta = -ADAM_LR * (m_hat / (jnp.sqrt(v_hat) + ADAM_EPS) + ADAM_WD * w)
    return delta, m, v


def _phase_shard(j):
    cx, cy = lax.axis_index("x"), lax.axis_index("y")
    others = [2 * px + py for px, py in _other_chips(cx, cy)]
    own = 2 * cx + cy
    return jnp.where(j == 0, own, jnp.where(j == 1, others[0], jnp.where(j == 2, others[1], others[2])))


def in_proj(x2, g_pre, w_in_shard, later_shards):
    t = x2.shape[0]
    tm = min(IN_PROJ_ROWS, t)
    n_tiles = t // tm
    n = len(later_shards)

    def body(x_ref, g_ref, ws_ref, *rest):
        shard_refs = rest[:n]
        o_ref, w_out_ref, gathered = rest[n], rest[n + 1], rest[n + 2:2 * n + 2]
        w_buf, h_buf = rest[2 * n + 2], rest[2 * n + 3]
        cast_bufs = rest[2 * n + 4:3 * n + 4]
        local_sems, send_sems, recv_sems = rest[3 * n + 4:]
        j = pl.program_id(0)
        i = pl.program_id(1)
        cx, cy, cc = _place()
        k = 2 * cx + cy
        chips = _other_chips(cx, cy)

        half = w_in_shard.shape[0] // 2

        def w_copy(p, block, core_half, forward):
            px, py = chips[p]
            rows = w_buf.at[block, pl.ds(pl.multiple_of(core_half * half, 2 * SUBLANES), half)]
            sem = 3 + p if forward else p
            return pltpu.make_async_remote_copy(
                src_ref=rows, dst_ref=rows, send_sem=send_sems.at[sem], recv_sem=recv_sems.at[sem],
                device_id=(cx, cy, 1 - cc) if forward else (px, py, cc), device_id_type=MESH)

        def own(m):
            return pltpu.make_async_copy(cast_bufs[m], gathered[m].at[k], local_sems.at[m])

        def later_copy(m, p, block, core_half, forward):
            px, py = chips[p]
            hr = later_shards[m].shape[0] // 2
            rows = pl.ds(pl.multiple_of(core_half * hr, 2 * SUBLANES), hr)
            dst = gathered[m].at[block, rows]
            sem = 6 + 6 * m + (3 + p if forward else p)
            return pltpu.make_async_remote_copy(
                src_ref=dst if forward else cast_bufs[m].at[rows], dst_ref=dst, send_sem=send_sems.at[sem],
                recv_sem=recv_sems.at[sem], device_id=(cx, cy, 1 - cc) if forward else (px, py, cc),
                device_id_type=MESH)

        @pl.when((j == 0) & (i == 0))
        def _():
            w_buf[k] = ws_ref[...].astype(BF16)
            for p in range(3):
                w_copy(p, k, cc, False).start()
            for m in range(n):
                cast_bufs[m][...] = shard_refs[m][...].astype(BF16)
                own(m).start()
                for p in range(3):
                    later_copy(m, p, k, cc, False).start()

        rows = pl.ds(pl.multiple_of(i * tm, tm), tm)

        @pl.when(j == 0)
        def _():
            x = x_ref[...]
            h_buf[rows, :] = (x * _rms_scale(x) * g_ref[...]).astype(BF16)

        for p in range(3):
            @pl.when((j == p) & (i == n_tiles - 1))
            def _(p=p):
                px, py = chips[p]
                w_copy(p, 2 * px + py, cc, False).wait_recv()
                w_copy(p, 2 * px + py, cc, True).start()

            @pl.when((j == p + 1) & (i == 0))
            def _(p=p):
                px, py = chips[p]
                w_copy(p, 2 * px + py, 1 - cc, True).wait_recv()

        o_ref[...] = _dot(h_buf[rows, :], w_buf[_phase_shard(j)]).astype(BF16)

        out_cp = pltpu.make_async_copy(w_buf, w_out_ref, local_sems.at[n])

        @pl.when((j == N_CHIPS - 1) & (i == 0))
        def _():
            out_cp.start()
            for m in range(n):
                for p in range(3):
                    px, py = chips[p]
                    later_copy(m, p, 2 * px + py, cc, False).wait_recv()
                    later_copy(m, p, 2 * px + py, cc, True).start()

        @pl.when((j == N_CHIPS - 1) & (i == n_tiles - 1))
        def _():
            for m in range(n):
                own(m).wait()
            for m in range(n):
                for p in range(3):
                    px, py = chips[p]
                    later_copy(m, p, 2 * px + py, 1 - cc, True).wait_recv()
            for p in range(3):
                px, py = chips[p]
                w_copy(p, k, cc, False).wait_send()
                w_copy(p, 2 * px + py, cc, True).wait_send()
            for m in range(n):
                for p in range(3):
                    px, py = chips[p]
                    later_copy(m, p, k, cc, False).wait_send()
                    later_copy(m, p, 2 * px + py, cc, True).wait_send()
            out_cp.wait()

    whole = lambda a: pl.BlockSpec(a.shape, lambda j, i: (0,) * a.ndim)
    res = pl.pallas_call(
        body, grid=(N_CHIPS, n_tiles),
        in_specs=[pl.BlockSpec((tm, D_MODEL), lambda j, i: (jnp.where(j == 0, i, n_tiles - 1), 0)),
                  pl.BlockSpec((1, D_MODEL), lambda j, i: (0, 0)),
                  whole(w_in_shard)] + [whole(s) for s in later_shards],
        out_specs=[pl.BlockSpec((tm, W_IN_SHARD), lambda j, i: (i, _phase_shard(j)))] + [ANY_SPEC] * (n + 1),
        out_shape=([jax.ShapeDtypeStruct((t, 2 * D_MODEL), BF16),
                    jax.ShapeDtypeStruct((N_CHIPS,) + w_in_shard.shape, BF16)]
                   + [jax.ShapeDtypeStruct((N_CHIPS,) + s.shape, BF16) for s in later_shards]),
        scratch_shapes=([pltpu.VMEM((N_CHIPS,) + w_in_shard.shape, BF16), pltpu.VMEM((t, D_MODEL), BF16)]
                        + [pltpu.VMEM(s.shape, BF16) for s in later_shards]
                        + [pltpu.SemaphoreType.DMA((n + 1,)), pltpu.SemaphoreType.DMA((6 * n + 6,)),
                           pltpu.SemaphoreType.DMA((6 * n + 6,))]),
        compiler_params=_params("arbitrary", "arbitrary"), name="in_proj")(x2, g_pre, w_in_shard, *later_shards)
    return res[0], res[1], res[2:]


def _pool_lane_window(shape):
    ch = lax.broadcasted_iota(jnp.int32, shape, 1)
    return jnp.where(ch < POOL_GW, 2.0, jnp.where(ch < 2 * POOL_GW, 4.0, jnp.where(ch < 3 * POOL_GW, 8.0, 16.0)))


def _pool_select(win, s2, s4, s8, s16):
    return jnp.where(win == 2.0, s2, jnp.where(win == 4.0, s4, jnp.where(win == 8.0, s8, s16)))


def _pool_div_count(v, win):
    pos = (lax.broadcasted_iota(jnp.int32, (POOL_PAD, POOL_W), 0) + 1).astype(F32)
    head = v[:POOL_PAD] / jnp.minimum(pos, win)
    return jnp.concatenate([head, v[POOL_PAD:] * (1.0 / win)], axis=0)


def _pool_diff(u, pad_ref, l):
    lo = POOL_PAD
    pad_ref[pl.ds(lo, l), :] = u
    s2 = u + pad_ref[pl.ds(lo - 1, l), :]
    pad_ref[pl.ds(lo, l), :] = s2
    s4 = s2 + pad_ref[pl.ds(lo - 2, l), :]
    pad_ref[pl.ds(lo, l), :] = s4
    s8 = s4 + pad_ref[pl.ds(lo - 4, l), :]
    pad_ref[pl.ds(lo, l), :] = s8
    s16 = s8 + pad_ref[pl.ds(lo - 8, l), :]
    win = _pool_lane_window((1, POOL_W))
    return _pool_div_count(_pool_select(win, s2, s4, s8, s16), win) - u, win


def pool_fwd(proj, w_blk, pool_scale, nb, l):
    def body(u_ref, w_ref, ps_ref, y_ref, d_ref, pad_ref):
        pad_ref[pl.ds(0, POOL_PAD), :] = jnp.zeros((POOL_PAD, POOL_W), F32)
        d, _ = _pool_diff(u_ref[...].astype(F32), pad_ref, l)
        db = d.astype(BF16)
        d_ref[...] = db
        y_ref[...] = (_dot(db, w_ref[...]) * ps_ref[...]).astype(BF16)

    rows = pl.BlockSpec((l, POOL_W), lambda b: (b, 0))
    act = jax.ShapeDtypeStruct((nb * l, POOL_W), BF16)
    return pl.pallas_call(
        body, grid=(nb,),
        in_specs=[rows,
                  pl.BlockSpec((POOL_W, POOL_W), lambda b: (0, 0)),
                  pl.BlockSpec((1, POOL_W), lambda b: (0, 0))],
        out_specs=[rows, rows],
        out_shape=[act, act],
        scratch_shapes=[pltpu.VMEM((POOL_PAD + l, POOL_W), F32)],
        compiler_params=_params("arbitrary"), name="pool_fwd")(proj, w_blk, pool_scale)


def pool_bwd(d_b, d_ycat, w_blk, pool_scale, nb, l):
    def body(d_ref, dy_ref, w_ref, ps_ref, du_ref, dw_ref, dps_ref, padb_ref):
        b = pl.program_id(0)

        @pl.when(b == 0)
        def _():
            dw_ref[...] = jnp.zeros_like(dw_ref)
            dps_ref[...] = jnp.zeros_like(dps_ref)

        padb_ref[pl.ds(l, POOL_PAD), :] = jnp.zeros((POOL_PAD, POOL_W), F32)
        win = _pool_lane_window((1, POOL_W))
        db = d_ref[...]
        w = w_ref[...]
        dy = dy_ref[...].astype(F32)
        dps_ref[...] += jnp.sum(dy * _dot(db, w), axis=0, keepdims=True)
        dyo = (dy * ps_ref[...]).astype(BF16)
        dw_ref[...] += _dot_tn(db, dyo)
        dd = _dot_nt(dyo, w)
        e = _pool_div_count(dd, win)
        padb_ref[pl.ds(0, l), :] = e
        f2 = e + padb_ref[pl.ds(1, l), :]
        padb_ref[pl.ds(0, l), :] = f2
        f4 = f2 + padb_ref[pl.ds(2, l), :]
        padb_ref[pl.ds(0, l), :] = f4
        f8 = f4 + padb_ref[pl.ds(4, l), :]
        padb_ref[pl.ds(0, l), :] = f8
        f16 = f8 + padb_ref[pl.ds(8, l), :]
        du_ref[...] = (_pool_select(win, f2, f4, f8, f16) - dd).astype(BF16)

    return pl.pallas_call(
        body, grid=(nb,),
        in_specs=[pl.BlockSpec((l, POOL_W), lambda b: (b, 0)),
                  pl.BlockSpec((l, POOL_W), lambda b: (b, 0)),
                  pl.BlockSpec((POOL_W, POOL_W), lambda b: (0, 0)),
                  pl.BlockSpec((1, POOL_W), lambda b: (0, 0))],
        out_specs=[pl.BlockSpec((l, POOL_W), lambda b: (b, 0)),
                   pl.BlockSpec((POOL_W, POOL_W), lambda b: (0, 0)),
                   pl.BlockSpec((1, POOL_W), lambda b: (0, 0))],
        out_shape=[jax.ShapeDtypeStruct((nb * l, POOL_W), BF16),
                   jax.ShapeDtypeStruct((POOL_W, POOL_W), F32),
                   jax.ShapeDtypeStruct((1, POOL_W), F32)],
        scratch_shapes=[pltpu.VMEM((POOL_PAD + l, POOL_W), F32)],
        compiler_params=_params("arbitrary"), name="pool_bwd")(d_b, d_ycat, w_blk, pool_scale)


def _discretise(a_re, a_im, ldt, b_re, b_im):
    dt = jnp.exp(ldt)
    mag = jnp.exp(a_re * dt)
    th = a_im * dt
    lr = mag * jnp.cos(th)
    li = mag * jnp.sin(th)
    nr = lr - 1.0
    den = a_re * a_re + a_im * a_im
    fr = (nr * a_re + li * a_im) / den
    fi = (li * a_re - nr * a_im) / den
    return lr, li, fr * b_re - fi * b_im, fr * b_im + fi * b_re


def _cmul(ar, ai, br, bi):
    return ar * br - ai * bi, ar * bi + ai * br


def s5_discretise(a_re_row, a_im_row, ldt_row, b_re_t, b_im_t):
    def body(ar_ref, ai_ref, ld_ref, br_ref, bi_ref, f_ref, b_ref, o_re, o_im):
        lr, li, bbr, bbi = _discretise(ar_ref[...], ai_ref[...], ld_ref[...], br_ref[...], bi_ref[...])
        o_re[...] = bbr
        o_im[...] = bbi
        r = lax.broadcasted_iota(jnp.int32, (SUBLANES, NST), 0)
        for out_ref, sign, rev in ((f_ref, 1.0, False), (b_ref, -1.0, True)):
            p = [(lr, sign * li)]
            for _ in range(SUBLANES - 1):
                p.append(_cmul(p[-1][0], p[-1][1], lr, sign * li))
            for idx, s in enumerate((1, 2, 4)):
                inside = (r < SUBLANES - s) if rev else (r >= s)
                out_ref[pl.ds(2 * idx * SUBLANES, SUBLANES), :] = jnp.where(inside, p[s - 1][0], 0.0)
                out_ref[pl.ds((2 * idx + 1) * SUBLANES, SUBLANES), :] = jnp.where(inside, p[s - 1][1], 0.0)
            pr = jnp.zeros((SUBLANES, NST), F32)
            pi = jnp.zeros((SUBLANES, NST), F32)
            for row in range(SUBLANES):
                power = (SUBLANES - row) if rev else (row + 1)
                pr = jnp.where(r == row, p[power - 1][0], pr)
                pi = jnp.where(r == row, p[power - 1][1], pi)
            out_ref[pl.ds(6 * SUBLANES, SUBLANES), :] = pr
            out_ref[pl.ds(7 * SUBLANES, SUBLANES), :] = pi

    cst = jax.ShapeDtypeStruct((8 * SUBLANES, NST), F32)
    bbar = jax.ShapeDtypeStruct((SSM_GC, NST), F32)
    return pl.pallas_call(body, out_shape=[cst, cst, bbar, bbar], name="s5_discretise")(
        a_re_row, a_im_row, ldt_row, b_re_t, b_im_t)


def s5_param_bwd(a_re_row, a_im_row, ldt_row, b_re_t, b_im_t, d_lr, d_li, d_bbr, d_bbi):
    def body(ar, ai, ld, br, bi, g_lr, g_li, g_br, g_bi, o_ar, o_ai, o_ld, o_br, o_bi):
        _, vjp = jax.vjp(_discretise, ar[...], ai[...], ld[...], br[...], bi[...])
        d_ar, d_ai, d_ld, d_br, d_bi = vjp((g_lr[...], g_li[...], g_br[...], g_bi[...]))
        o_ar[...] = d_ar
        o_ai[...] = d_ai
        state = lax.broadcasted_iota(jnp.int32, (NST, LANES), 0)
        lane = lax.broadcasted_iota(jnp.int32, (NST, LANES), 1)
        in_group = jnp.where((state >= lane * SSM_N) & (state < (lane + 1) * SSM_N), 1.0, 0.0)
        o_ld[...] = jnp.dot(d_ld, in_group, precision=lax.Precision.HIGHEST, preferred_element_type=F32)
        o_br[...] = d_br
        o_bi[...] = d_bi

    row = jax.ShapeDtypeStruct((1, NST), F32)
    mat = jax.ShapeDtypeStruct((SSM_GC, NST), F32)
    grp = jax.ShapeDtypeStruct((1, LANES), F32)
    return pl.pallas_call(body, out_shape=[row, row, grp, mat, mat], name="s5_param_bwd")(
        a_re_row, a_im_row, ldt_row, b_re_t, b_im_t, d_lr, d_li, d_bbr, d_bbi)


def _scan_tiles(buf_ref, cst_ref, carry_ref, rows, reverse, reset, h_ref=None, acc_ref=None):
    n_tiles = rows // SUBLANES
    shifts = ((1, 0), (2, 2), (4, 4))
    row_id = lax.broadcasted_iota(jnp.int32, (SUBLANES, SCAN_COLS), 0)
    cols = [(j * SCAN_COLS, NST + j * SCAN_COLS) for j in range(NST // SCAN_COLS)]
    edge = 0 if reverse else SUBLANES - 1
    carry = jnp.where(reset, 0.0, carry_ref[...])
    accs = [(acc_ref[:, pl.ds(o_re, SCAN_COLS)], acc_ref[:, pl.ds(o_im, SCAN_COLS)]) for o_re, o_im in cols] if reverse else None
    for i in range(n_tiles):
        start = ((n_tiles - 1 - i) if reverse else i) * SUBLANES
        rws = pl.ds(start, SUBLANES)
        for j, (o_re, o_im) in enumerate(cols):
            c_re = pl.ds(o_re, SCAN_COLS)
            c_im = pl.ds(o_im, SCAN_COLS)
            xr = buf_ref[rws, c_re]
            xi = buf_ref[rws, c_im]
            for s, base in shifts:
                amount = (SUBLANES - s) if reverse else s
                sr = pltpu.roll(xr, amount, 0)
                si = pltpu.roll(xi, amount, 0)
                mr = cst_ref[pl.ds(base * SUBLANES, SUBLANES), c_re]
                mi = cst_ref[pl.ds((base + 1) * SUBLANES, SUBLANES), c_re]
                xr, xi = xr + (mr * sr - mi * si), xi + (mr * si + mi * sr)
            pr = cst_ref[pl.ds(6 * SUBLANES, SUBLANES), c_re]
            pi = cst_ref[pl.ds(7 * SUBLANES, SUBLANES), c_re]
            cr = carry[:, o_re:o_re + SCAN_COLS]
            ci = carry[:, o_im:o_im + SCAN_COLS]
            xr, xi = xr + (pr * cr - pi * ci), xi + (pr * ci + pi * cr)
            buf_ref[rws, c_re] = xr
            buf_ref[rws, c_im] = xi
            if reverse:
                gnr = jnp.where(row_id == SUBLANES - 1, cr, pltpu.roll(xr, SUBLANES - 1, 0))
                gni = jnp.where(row_id == SUBLANES - 1, ci, pltpu.roll(xi, SUBLANES - 1, 0))
                hr = h_ref[rws, c_re]
                hi = h_ref[rws, c_im]
                accs[j] = (accs[j][0] + (gnr * hr + gni * hi), accs[j][1] + (gni * hr - gnr * hi))
        carry = jnp.broadcast_to(buf_ref[pl.ds(start + edge, 1), :], (SUBLANES, 2 * NST))
    carry_ref[...] = carry
    if reverse:
        for (o_re, o_im), (a_re_, a_im_) in zip(cols, accs):
            acc_ref[:, pl.ds(o_re, SCAN_COLS)] = a_re_
            acc_ref[:, pl.ds(o_im, SCAN_COLS)] = a_im_


def _state_block(v, m):
    return jnp.concatenate([v[:, m * BLK_ST:(m + 1) * BLK_ST], v[:, NST + m * BLK_ST:NST + (m + 1) * BLK_ST]], axis=1)


def _put_state_block(buf_ref, m, val):
    buf_ref[:, pl.ds(m * BLK_ST, BLK_ST)] = val[:, :BLK_ST]
    buf_ref[:, pl.ds(NST + m * BLK_ST, BLK_ST)] = val[:, BLK_ST:]


def s5_fwd(proj, b_m, c_m, cst, d_skip, w_glu, nb, l):
    tt = min(S5_CHUNK, l)
    nt = l // tt
    n_chunks = nb * nt

    def body(u_ref, b_ref, c_ref, cst_ref, ds_ref, wg_ref, y_ref, h_ref, yp_ref, z_ref,
             buf0, buf1, ub0, ub1, carry_ref):
        s = pl.program_id(0)

        @pl.when(s == 0)
        def _():
            for r in (buf0, buf1, ub0, ub1, carry_ref):
                r[...] = jnp.zeros_like(r)

        def step(p_buf, p_u, q_buf):
            h = p_buf[...].astype(BF16)
            h_ref[...] = h
            ypre = (jnp.concatenate([_dot(_state_block(h, m), c_ref[m]) for m in range(N_BLK)], axis=1)
                    + ds_ref[...] * p_u[...])
            yp_ref[...] = ypre
            z = _dot(jax.nn.gelu(ypre).astype(BF16), wg_ref[...])
            z_ref[...] = z
            y_ref[...] = (z[:, :SSM_W] * jax.nn.sigmoid(z[:, SSM_W:])).astype(BF16)
            ub = u_ref[...]
            p_u[...] = ub.astype(F32)
            for m in range(N_BLK):
                _put_state_block(p_buf, m, _dot(ub[:, m * BLK_CH:(m + 1) * BLK_CH], b_ref[m]))
            _scan_tiles(q_buf, cst_ref, carry_ref, tt, False, lax.rem(s + nt - 1, nt) == 0)

        @pl.when(lax.rem(s, 2) == 0)
        def _():
            step(buf0, ub0, buf1)

        @pl.when(lax.rem(s, 2) == 1)
        def _():
            step(buf1, ub1, buf0)

    row_in = lambda s: (jnp.minimum(s, n_chunks - 1), 1)
    row_out = lambda s: (jnp.maximum(s - 2, 0), 0)
    const = lambda s: (0, 0)
    const3 = lambda s: (0, 0, 0)
    return pl.pallas_call(
        body, grid=(n_chunks + 2,),
        in_specs=[pl.BlockSpec((tt, SSM_W), row_in),
                  pl.BlockSpec((N_BLK, BLK_CH, 2 * BLK_ST), const3),
                  pl.BlockSpec((N_BLK, 2 * BLK_ST, BLK_CH), const3),
                  pl.BlockSpec((8 * SUBLANES, NST), const),
                  pl.BlockSpec((1, SSM_W), const),
                  pl.BlockSpec((SSM_W, 2 * SSM_W), const)],
        out_specs=[pl.BlockSpec((tt, SSM_W), row_out),
                   pl.BlockSpec((tt, 2 * NST), row_out),
                   pl.BlockSpec((tt, SSM_W), row_out),
                   pl.BlockSpec((tt, 2 * SSM_W), row_out)],
        out_shape=[jax.ShapeDtypeStruct((nb * l, SSM_W), BF16),
                   jax.ShapeDtypeStruct((nb * l, 2 * NST), BF16),
                   jax.ShapeDtypeStruct((nb * l, SSM_W), F32),
                   jax.ShapeDtypeStruct((nb * l, 2 * SSM_W), F32)],
        scratch_shapes=[pltpu.VMEM((tt, 2 * NST), F32), pltpu.VMEM((tt, 2 * NST), F32),
                        pltpu.VMEM((tt, SSM_W), F32), pltpu.VMEM((tt, SSM_W), F32),
                        pltpu.VMEM((SUBLANES, 2 * NST), F32)],
        compiler_params=_params("arbitrary"), name="s5_fwd")(proj, b_m, c_m, cst, d_skip, w_glu)


def s5_bwd(proj, d_ycat, h, ypre, z, b_m, c_m, cst_rev, d_skip, w_glu, nb, l):
    tt = min(S5_CHUNK, l)
    nt = l // tt
    n_chunks = nb * nt

    def body(u_ref, dy_ref, h_ref, yp_ref, z_ref, b_ref, c_ref, cst_ref, ds_ref, wg_ref,
             du_ref, dwg_ref, dds_ref, dcc_ref, dbb_ref, dlam_ref,
             buf0, buf1, hb0, hb1, ub0, ub1, dyp0, dyp1, carry_ref, dc_ref, db_ref):
        s = pl.program_id(0)

        @pl.when(s == 0)
        def _():
            for r in (buf0, buf1, hb0, hb1, ub0, ub1, dyp0, dyp1, carry_ref, dc_ref, db_ref, dwg_ref, dds_ref, dlam_ref):
                r[...] = jnp.zeros_like(r)

        def step(p_buf, p_h, p_u, p_dyp, q_buf, q_h):
            g = p_buf[...].astype(BF16)
            ub_done = p_u[...].astype(BF16)
            du = []
            for m in range(N_BLK):
                g_m = _state_block(g, m)
                db_ref[m] += _dot_tn(g_m, ub_done[:, m * BLK_CH:(m + 1) * BLK_CH])
                du.append(_dot_nt(g_m, b_ref[m]))
            du_ref[...] = (jnp.concatenate(du, axis=1) + ds_ref[...] * p_dyp[...]).astype(BF16)
            u = u_ref[...].astype(F32)
            ypre = yp_ref[...]
            z = z_ref[...]
            z1 = z[:, :SSM_W]
            sg = jax.nn.sigmoid(z[:, SSM_W:])
            dy = dy_ref[...].astype(F32) * jnp.where(s < n_chunks, 1.0, 0.0)
            dz = jnp.concatenate([dy * sg, dy * z1 * sg * (1.0 - sg)], axis=1).astype(BF16)
            yg, gelu_vjp = jax.vjp(jax.nn.gelu, ypre)
            dwg_ref[...] += _dot_tn(yg.astype(BF16), dz)
            dypre = gelu_vjp(_dot_nt(dz, wg_ref[...]))[0]
            dds_ref[...] += jnp.sum(dypre * u, axis=0, keepdims=True)
            dyb = dypre.astype(BF16)
            hb = h_ref[...]
            p_h[...] = hb.astype(F32)
            p_u[...] = u
            p_dyp[...] = dypre
            for m in range(N_BLK):
                dy_m = dyb[:, m * BLK_CH:(m + 1) * BLK_CH]
                dc_ref[m] += _dot_tn(_state_block(hb, m), dy_m)
                _put_state_block(p_buf, m, _dot_nt(dy_m, c_ref[m]))
            _scan_tiles(q_buf, cst_ref, carry_ref, tt, True, lax.rem(s + nt - 1, nt) == 0, h_ref=q_h, acc_ref=dlam_ref)

        @pl.when(lax.rem(s, 2) == 0)
        def _():
            step(buf0, hb0, ub0, dyp0, buf1, hb1)

        @pl.when(lax.rem(s, 2) == 1)
        def _():
            step(buf1, hb1, ub1, dyp1, buf0, hb0)

        @pl.when(s == n_chunks + 1)
        def _():
            dlam_ref[...] = jnp.broadcast_to(jnp.sum(dlam_ref[...], axis=0, keepdims=True), dlam_ref.shape)
            for acc_ref, out_ref in ((dc_ref, dcc_ref), (db_ref, dbb_ref)):
                for m in range(N_BLK):
                    for ri in range(2):
                        for gl in range(BLK_GROUPS):
                            out_ref[ri, pl.ds((m * BLK_GROUPS + gl) * SSM_N, SSM_N), :] = acc_ref[
                                m, pl.ds(ri * BLK_ST + gl * SSM_N, SSM_N), pl.ds(gl * SSM_GC, SSM_GC)]

    def chunk_rows(o):
        return lax.div(o, nt) * nt + (nt - 1 - lax.rem(o, nt))

    def rrow(col):
        return lambda s: (chunk_rows(jnp.minimum(s, n_chunks - 1)), col)

    const = lambda s: (0, 0)
    const3 = lambda s: (0, 0, 0)
    state_buf = pltpu.VMEM((tt, 2 * NST), F32)
    chan_buf = pltpu.VMEM((tt, SSM_W), F32)
    return pl.pallas_call(
        body, grid=(n_chunks + 2,),
        in_specs=[pl.BlockSpec((tt, SSM_W), rrow(1)),
                  pl.BlockSpec((tt, SSM_W), rrow(1)),
                  pl.BlockSpec((tt, 2 * NST), rrow(0)),
                  pl.BlockSpec((tt, SSM_W), rrow(0)),
                  pl.BlockSpec((tt, 2 * SSM_W), rrow(0)),
                  pl.BlockSpec((N_BLK, BLK_CH, 2 * BLK_ST), const3),
                  pl.BlockSpec((N_BLK, 2 * BLK_ST, BLK_CH), const3),
                  pl.BlockSpec((8 * SUBLANES, NST), const),
                  pl.BlockSpec((1, SSM_W), const),
                  pl.BlockSpec((SSM_W, 2 * SSM_W), const)],
        out_specs=[pl.BlockSpec((tt, SSM_W), lambda s: (chunk_rows(jnp.maximum(s - 2, 0)), 0)),
                   pl.BlockSpec((SSM_W, 2 * SSM_W), const),
                   pl.BlockSpec((1, SSM_W), const),
                   pl.BlockSpec((2, NST, SSM_GC), const3),
                   pl.BlockSpec((2, NST, SSM_GC), const3),
                   pl.BlockSpec((SUBLANES, 2 * NST), const)],
        out_shape=[jax.ShapeDtypeStruct((nb * l, SSM_W), BF16),
                   jax.ShapeDtypeStruct((SSM_W, 2 * SSM_W), F32),
                   jax.ShapeDtypeStruct((1, SSM_W), F32),
                   jax.ShapeDtypeStruct((2, NST, SSM_GC), F32),
                   jax.ShapeDtypeStruct((2, NST, SSM_GC), F32),
                   jax.ShapeDtypeStruct((SUBLANES, 2 * NST), F32)],
        scratch_shapes=[state_buf, state_buf, state_buf, state_buf, chan_buf, chan_buf, chan_buf, chan_buf,
                        pltpu.VMEM((SUBLANES, 2 * NST), F32),
                        pltpu.VMEM((N_BLK, 2 * BLK_ST, BLK_CH), F32), pltpu.VMEM((N_BLK, 2 * BLK_ST, BLK_CH), F32)],
        compiler_params=_params("arbitrary"), name="s5_bwd")(
            proj, d_ycat, h, ypre, z, b_m, c_m, cst_rev, d_skip, w_glu)


def _head_mask(hh):
    lane = lax.broadcasted_iota(jnp.int32, (1, ATT_W), 1)
    return (lane >= hh * MEM_HD) & (lane < (hh + 1) * MEM_HD)


def _mem_kv(mem_ref, gm_ref, wkv_ref):
    m = mem_ref[0]
    mh = m * _rms_scale(m)
    mb = (mh * gm_ref[...]).astype(BF16)
    kv = _dot(mb, wkv_ref[...])
    return mh, mb, kv[:, :ATT_W].astype(BF16), kv[:, ATT_W:].astype(BF16)


def _stack_heads(dst_ref, a):
    for hh in range(MEM_HEADS):
        dst_ref[pl.ds(hh * N_MEM, N_MEM), :] = jnp.where(_head_mask(hh), a, jnp.zeros_like(a))


def _fold_heads(a):
    out = jnp.zeros((N_MEM, ATT_W), a.dtype)
    for hh in range(MEM_HEADS):
        out = out + jnp.where(_head_mask(hh), a[hh * N_MEM:(hh + 1) * N_MEM], 0.0)
    return out


def _softmax_heads(s):
    parts = []
    for hh in range(MEM_HEADS):
        sh = s[:, hh * N_MEM:(hh + 1) * N_MEM]
        e = jnp.exp(sh - jnp.max(sh, axis=-1, keepdims=True))
        parts.append(e / jnp.sum(e, axis=-1, keepdims=True))
    return jnp.concatenate(parts, axis=1)


def attn_fwd(proj, mem, g_mem, w_kv, nb, l):
    tq = min(ATTN_ROWS, l)
    nq = l // tq
    scale = MEM_HD ** -0.5

    def body(q_ref, mem_ref, gm_ref, wkv_ref, o_ref, p_ref, k_s, v_s):
        @pl.when(pl.program_id(1) == 0)
        def _():
            _, _, k, v = _mem_kv(mem_ref, gm_ref, wkv_ref)
            _stack_heads(k_s, k)
            _stack_heads(v_s, v)

        p = _softmax_heads(_dot_nt(q_ref[...], k_s[...]) * scale).astype(BF16)
        p_ref[...] = p
        o_ref[...] = _dot(p, v_s[...]).astype(BF16)

    const = lambda b, t: (0, 0)
    stacked = pltpu.VMEM((MEM_HEADS * N_MEM, ATT_W), BF16)
    return pl.pallas_call(
        body, grid=(nb, nq),
        in_specs=[pl.BlockSpec((tq, ATT_W), lambda b, t: (b * nq + t, 3)),
                  pl.BlockSpec((1, N_MEM, D_MODEL), lambda b, t: (b, 0, 0)),
                  pl.BlockSpec((1, D_MODEL), const),
                  pl.BlockSpec((D_MODEL, 2 * ATT_W), const)],
        out_specs=[pl.BlockSpec((tq, ATT_W), lambda b, t: (b * nq + t, 0)),
                   pl.BlockSpec((tq, MEM_HEADS * N_MEM), lambda b, t: (b * nq + t, 0))],
        out_shape=[jax.ShapeDtypeStruct((nb * l, ATT_W), BF16),
                   jax.ShapeDtypeStruct((nb * l, MEM_HEADS * N_MEM), BF16)],
        scratch_shapes=[stacked, stacked],
        compiler_params=_params("arbitrary", "arbitrary"), name="attn_fwd")(proj, mem, g_mem, w_kv)


def attn_bwd(proj, d_ycat, p_att, mem, g_mem, w_kv, nb, l):
    tq = min(ATTN_ROWS, l)
    nq = l // tq
    scale = MEM_HD ** -0.5

    def body(q_ref, do_ref, p_ref, mem_ref, gm_ref, wkv_ref, dq_ref, dwkv_ref, dgm_ref, k_s, v_s, mb_s, dk_s, dv_s):
        b = pl.program_id(0)
        t = pl.program_id(1)

        @pl.when((b == 0) & (t == 0))
        def _():
            dwkv_ref[...] = jnp.zeros_like(dwkv_ref)
            dgm_ref[...] = jnp.zeros_like(dgm_ref)

        @pl.when(t == 0)
        def _():
            _, mb, k, v = _mem_kv(mem_ref, gm_ref, wkv_ref)
            _stack_heads(k_s, k)
            _stack_heads(v_s, v)
            mb_s[...] = mb
            dk_s[...] = jnp.zeros_like(dk_s)
            dv_s[...] = jnp.zeros_like(dv_s)

        q = q_ref[...]
        do = do_ref[...]
        k = k_s[...]
        pb = p_ref[...]
        p = pb.astype(F32)
        dp = _dot_nt(do, v_s[...])
        ds = []
        for hh in range(MEM_HEADS):
            blk = slice(hh * N_MEM, (hh + 1) * N_MEM)
            ds.append(p[:, blk] * (dp[:, blk] - jnp.sum(dp[:, blk] * p[:, blk], axis=-1, keepdims=True)) * scale)
        ds = jnp.concatenate(ds, axis=1).astype(BF16)
        dq_ref[...] = _dot(ds, k).astype(BF16)
        dk_s[...] += _dot_tn(ds, q)
        dv_s[...] += _dot_tn(pb, do)

        @pl.when(t == nq - 1)
        def _():
            dkv = jnp.concatenate([_fold_heads(dk_s[...]), _fold_heads(dv_s[...])], axis=1).astype(BF16)
            dwkv_ref[...] += _dot_tn(mb_s[...], dkv)
            m = mem_ref[0]
            dgm_ref[...] += jnp.sum(_dot_nt(dkv, wkv_ref[...]) * (m * _rms_scale(m)), axis=0, keepdims=True)

    const = lambda b, t: (0, 0)
    return pl.pallas_call(
        body, grid=(nb, nq),
        in_specs=[pl.BlockSpec((tq, ATT_W), lambda b, t: (b * nq + t, 3)),
                  pl.BlockSpec((tq, ATT_W), lambda b, t: (b * nq + t, 3)),
                  pl.BlockSpec((tq, MEM_HEADS * N_MEM), lambda b, t: (b * nq + t, 0)),
                  pl.BlockSpec((1, N_MEM, D_MODEL), lambda b, t: (b, 0, 0)),
                  pl.BlockSpec((1, D_MODEL), const),
                  pl.BlockSpec((D_MODEL, 2 * ATT_W), const)],
        out_specs=[pl.BlockSpec((tq, ATT_W), lambda b, t: (b * nq + t, 0)),
                   pl.BlockSpec((D_MODEL, 2 * ATT_W), const),
                   pl.BlockSpec((1, D_MODEL), const)],
        out_shape=[jax.ShapeDtypeStruct((nb * l, ATT_W), BF16),
                   jax.ShapeDtypeStruct((D_MODEL, 2 * ATT_W), F32),
                   jax.ShapeDtypeStruct((1, D_MODEL), F32)],
        scratch_shapes=[pltpu.VMEM((MEM_HEADS * N_MEM, ATT_W), BF16), pltpu.VMEM((MEM_HEADS * N_MEM, ATT_W), BF16),
                        pltpu.VMEM((N_MEM, D_MODEL), BF16),
                        pltpu.VMEM((MEM_HEADS * N_MEM, ATT_W), F32), pltpu.VMEM((MEM_HEADS * N_MEM, ATT_W), F32)],
        compiler_params=_params("arbitrary", "arbitrary"), name="attn_bwd")(proj, d_ycat, p_att, mem, g_mem, w_kv)


def tail(y_pool, y_ssm, y_att, proj, x2, target, w_out, g_post):
    t = x2.shape[0]
    tm = min(ROW_TILE, t)

    def body(yp_ref, ys_ref, ya_ref, gate_ref, x_ref, tg_ref, w_ref, gp_ref,
             dz_ref, dyc_ref, dgate_ref, dw_ref, dgp_ref, loss_ref):
        @pl.when(pl.program_id(0) == 0)
        def _():
            dw_ref[...] = jnp.zeros_like(dw_ref)
            dgp_ref[...] = jnp.zeros_like(dgp_ref)
            loss_ref[...] = jnp.zeros_like(loss_ref)

        ycat = jnp.concatenate([yp_ref[...], ys_ref[...], ya_ref[...]], axis=1).astype(F32)
        gate = gate_ref[...].astype(F32)
        sg = jax.nn.sigmoid(gate)
        silu = gate * sg
        yb = (ycat * silu).astype(BF16)
        w = w_ref[...]
        out = _dot(yb, w)
        r2 = _rms_scale(out)
        oh = out * r2
        gp = gp_ref[...]
        err = (x_ref[...] + oh * gp) - tg_ref[...]
        loss_ref[...] += 0.5 * jnp.sum(jnp.mean(err * err, axis=-1, keepdims=True), axis=0, keepdims=True)
        dz = err * (1.0 / D_MODEL)
        dz_ref[...] = dz.astype(BF16)
        dgp_ref[...] += jnp.sum(dz * oh, axis=0, keepdims=True)
        dn = dz * gp
        dout = (r2 * (dn - oh * jnp.mean(dn * oh, axis=-1, keepdims=True))).astype(BF16)
        dw_ref[...] += _dot_tn(yb, dout)
        dy = _dot_nt(dout, w)
        dyc_ref[...] = (dy * silu).astype(BF16)
        dgate_ref[...] = (dy * ycat * (sg * (1.0 + gate * (1.0 - sg)))).astype(BF16)

    row = lambda i: (i, 0)
    const = lambda i: (0, 0)
    full = jax.ShapeDtypeStruct((t, D_MODEL), BF16)
    return pl.pallas_call(
        body, grid=(t // tm,),
        in_specs=[pl.BlockSpec((tm, POOL_W), row),
                  pl.BlockSpec((tm, SSM_W), row),
                  pl.BlockSpec((tm, ATT_W), row),
                  pl.BlockSpec((tm, D_MODEL), lambda i: (i, 1)),
                  pl.BlockSpec((tm, D_MODEL), row),
                  pl.BlockSpec((tm, D_MODEL), row),
                  pl.BlockSpec((D_MODEL, D_MODEL), const),
                  pl.BlockSpec((1, D_MODEL), const)],
        out_specs=[pl.BlockSpec((tm, D_MODEL), row),
                   pl.BlockSpec((tm, D_MODEL), row),
                   pl.BlockSpec((tm, D_MODEL), row),
                   pl.BlockSpec((D_MODEL, D_MODEL), const),
                   pl.BlockSpec((1, D_MODEL), const),
                   pl.BlockSpec((1, LANES), const)],
        out_shape=[full, full, full,
                   jax.ShapeDtypeStruct((D_MODEL, D_MODEL), F32),
                   jax.ShapeDtypeStruct((1, D_MODEL), F32),
                   jax.ShapeDtypeStruct((1, LANES), F32)],
        compiler_params=_params("arbitrary"), name="tail")(y_pool, y_ssm, y_att, proj, x2, target, w_out, g_post)


def in_proj_bwd(du_pool, du_ssm, dq, d_gate, x2, dz, g_pre, w_in):
    t = x2.shape[0]
    tm = min(ROW_TILE, t)
    shard = W_IN_SHARD

    def body(dup_ref, dus_ref, dq_ref, dg_ref, x_ref, dz_ref, g_ref, w_ref, gx_ref, dw_ref, dgp_ref):
        @pl.when(pl.program_id(0) == 0)
        def _():
            dw_ref[...] = jnp.zeros_like(dw_ref)
            dgp_ref[...] = jnp.zeros_like(dgp_ref)

        dproj = jnp.concatenate([dup_ref[...], dus_ref[...], dq_ref[...], dg_ref[...]], axis=1)
        x = x_ref[...]
        r1 = _rms_scale(x)
        xh = x * r1
        g = g_ref[...]
        hb = (xh * g).astype(BF16)
        dh = jnp.zeros((tm, D_MODEL), F32)
        for k in range(N_CHIPS):
            dp_k = dproj[:, k * shard:(k + 1) * shard]
            dw_ref[k] += _dot_tn(hb, dp_k)
            dh = dh + _dot_nt(dp_k, w_ref[k])
        dgp_ref[...] += jnp.sum(dh * xh, axis=0, keepdims=True)
        dn = dh * g
        gx_ref[...] = dz_ref[...].astype(F32) + r1 * (dn - xh * jnp.mean(dn * xh, axis=-1, keepdims=True))

    row = lambda i: (i, 0)
    const = lambda i: (0, 0)
    return pl.pallas_call(
        body, grid=(t // tm,),
        in_specs=[pl.BlockSpec((tm, POOL_W), row),
                  pl.BlockSpec((tm, SSM_W), row),
                  pl.BlockSpec((tm, ATT_W), row),
                  pl.BlockSpec((tm, D_MODEL), row),
                  pl.BlockSpec((tm, D_MODEL), row),
                  pl.BlockSpec((tm, D_MODEL), row),
                  pl.BlockSpec((1, D_MODEL), const),
                  pl.BlockSpec((N_CHIPS, D_MODEL, shard), lambda i: (0, 0, 0))],
        out_specs=[pl.BlockSpec((tm, D_MODEL), row),
                   pl.BlockSpec((N_CHIPS, D_MODEL, shard), lambda i: (0, 0, 0)),
                   pl.BlockSpec((1, D_MODEL), const)],
        out_shape=[jax.ShapeDtypeStruct((t, D_MODEL), F32),
                   jax.ShapeDtypeStruct((N_CHIPS, D_MODEL, shard), F32),
                   jax.ShapeDtypeStruct((1, D_MODEL), F32)],
        compiler_params=_params("arbitrary"), name="in_proj_bwd")(du_pool, du_ssm, dq, d_gate, x2, dz, g_pre, w_in)


def _block_diag(blocks):
    g, r, c = blocks.shape
    eye = jnp.eye(g, dtype=blocks.dtype)
    return jnp.einsum("grc,gh->grhc", blocks, eye).reshape(g * r, g * c)


def _block_diag_extract(mat, g, r, c):
    eye = jnp.eye(g, dtype=mat.dtype)
    return jnp.einsum("grhc,gh->grc", mat.reshape(g, r, g, c), eye)


def local_step(x, mem, target, g_pre, w_in_shard, w_pool, pool_scale, a_re, a_im, log_dt, b_re, b_im, c_re, c_im,
               d_skip, later_shards, g_mem, g_post):
    nb, l, _ = x.shape
    x2 = x.reshape(nb * l, D_MODEL)
    tg2 = target.reshape(nb * l, D_MODEL)

    rowv = lambda a: a.reshape(1, NST)
    lanes_last = lambda b: b.transpose(2, 0, 1).reshape(SSM_GC, NST)
    ldt_row = rowv(jnp.broadcast_to(log_dt.reshape(SSM_NG, 1), (SSM_NG, SSM_N)))
    b_re_t = lanes_last(b_re)
    b_im_t = lanes_last(b_im)
    cst_f, cst_b, bbr, bbi = s5_discretise(rowv(a_re), rowv(a_im), ldt_row, b_re_t, b_im_t)
    eye = jnp.eye(BLK_GROUPS, dtype=F32)
    blk_in = lambda bb: jnp.einsum("cmgn,gh->mhcgn", bb.reshape(SSM_GC, N_BLK, BLK_GROUPS, SSM_N), eye).reshape(
        N_BLK, BLK_CH, BLK_ST)
    blk_out = lambda cc: jnp.einsum("mgcn,gh->mgnhc", cc.reshape(N_BLK, BLK_GROUPS, SSM_GC, SSM_N), eye).reshape(
        N_BLK, BLK_ST, BLK_CH)
    b_m = jnp.concatenate([blk_in(bbr), blk_in(bbi)], axis=2).astype(BF16)
    c_m = jnp.concatenate([blk_out(c_re), -blk_out(c_im)], axis=1).astype(BF16)
    w_pool_blk = _block_diag(w_pool.reshape(4, POOL_GW, POOL_GW)).astype(BF16)

    proj, w_in, (g_glu, g_kv, g_out) = in_proj(x2, g_pre, w_in_shard, later_shards)
    w_glu = g_glu.transpose(2, 0, 1).reshape(SSM_W, 2 * SSM_W)
    w_kv = g_kv.reshape(D_MODEL, 2 * ATT_W)
    w_out = g_out.reshape(D_MODEL, D_MODEL)
    y_pool, pool_d = pool_fwd(proj, w_pool_blk, pool_scale, nb, l)
    y_ssm, h, ypre, z = s5_fwd(proj, b_m, c_m, cst_f, d_skip, w_glu, nb, l)
    y_att, p_att = attn_fwd(proj, mem, g_mem, w_kv, nb, l)
    dz, d_ycat, d_gate, d_w_out, d_g_post, loss = tail(y_pool, y_ssm, y_att, proj, x2, tg2, w_out, g_post)
    du_pool, d_wp_blk, d_pool_scale = pool_bwd(pool_d, d_ycat, w_pool_blk, pool_scale, nb, l)
    du_ssm, d_w_glu, d_d_skip, d_cc, d_bb, d_lam = s5_bwd(
        proj, d_ycat, h, ypre, z, b_m, c_m, cst_b, d_skip, w_glu, nb, l)
    dq, d_w_kv, d_g_mem = attn_bwd(proj, d_ycat, p_att, mem, g_mem, w_kv, nb, l)
    grad_x, d_w_in, d_g_pre = in_proj_bwd(du_pool, du_ssm, dq, d_gate, x2, dz, g_pre, w_in)

    d_c = d_cc[0].reshape(SSM_NG, SSM_N, SSM_GC)
    d_ci = d_cc[1].reshape(SSM_NG, SSM_N, SSM_GC)
    d_lam_row = d_lam[0]
    d_ar, d_ai, d_ld, d_br, d_bi = s5_param_bwd(
        rowv(a_re), rowv(a_im), ldt_row, b_re_t, b_im_t,
        d_lam_row[:NST].reshape(1, NST), d_lam_row[NST:].reshape(1, NST), d_bb[0].T, d_bb[1].T)
    from_lanes_last = lambda b: b.reshape(SSM_GC, SSM_NG, SSM_N).transpose(1, 0, 2).reshape(1, SSM_NG, SSM_GC, SSM_N)

    grads = {
        "g_pre": d_g_pre,
        "w_in": d_w_in,
        "w_pool": _block_diag_extract(d_wp_blk, 4, POOL_GW, POOL_GW).reshape(1, 4, POOL_GW, POOL_GW),
        "pool_scale": d_pool_scale,
        "a_re": d_ar.reshape(1, SSM_NG, SSM_N),
        "a_im": d_ai.reshape(1, SSM_NG, SSM_N),
        "log_dt": d_ld[:, :SSM_NG],
        "b_re": from_lanes_last(d_br),
        "b_im": from_lanes_last(d_bi),
        "c_re": d_c.transpose(0, 2, 1).reshape(1, SSM_NG, SSM_GC, SSM_N),
        "c_im": (-d_ci).transpose(0, 2, 1).reshape(1, SSM_NG, SSM_GC, SSM_N),
        "d_skip": d_d_skip,
        "w_glu": d_w_glu,
        "g_mem": d_g_mem,
        "w_kv": d_w_kv,
        "w_out": d_w_out,
        "g_post": d_g_post,
    }
    return loss, grad_x.reshape(nb, l, D_MODEL), grads


VMEM_SPEC = pl.BlockSpec(memory_space=pltpu.VMEM)
ANY_SPEC = pl.BlockSpec(memory_space=pl.ANY)


def _place():
    return lax.axis_index("x"), lax.axis_index("y"), lax.axis_index("c")


def _other_chips(x, y):
    return [(1 - x, y), (x, 1 - y), (1 - x, 1 - y)]


SEMS_PER_ITEM = 5


def reduce_all(g4s, packs, pack_dtypes):
    n = len(g4s)
    dims = [g.shape[1:] for g in g4s]
    np_ = len(packs)
    prs = [p.shape[0] for p in packs]

    def body(*refs):
        g_refs, p_refs = refs[:n], refs[n:n + np_]
        outs, ops = refs[n + np_:2 * n + np_], refs[2 * n + np_:2 * (n + np_)]
        scr = refs[2 * (n + np_):]
        mine, theirs, sendb, recvb = scr[0:n], scr[n:2 * n], scr[2 * n:3 * n], scr[3 * n:4 * n]
        p_theirs, gats = scr[4 * n:4 * n + np_], scr[4 * n + np_:4 * n + 2 * np_]
        lsems, send_sems, recv_sems = scr[4 * n + 2 * np_:]
        x, y, c = _place()
        k = 2 * x + y
        chips = _other_chips(x, y)
        sib = (x, y, 1 - c)

        def halves(rows):
            hr = rows // 2
            return (pl.ds(pl.multiple_of(c * hr, SUBLANES), hr), pl.ds(pl.multiple_of((1 - c) * hr, SUBLANES), hr))

        def remote(src, dst, sem, to):
            return pltpu.make_async_remote_copy(
                src_ref=src, dst_ref=dst, send_sem=send_sems.at[sem], recv_sem=recv_sems.at[sem],
                device_id=to, device_id_type=MESH)

        loads, started = [], []
        for i in range(n):
            my_rows, sib_rows = halves(dims[i][0])
            ld = pltpu.make_async_copy(g_refs[i].at[:, my_rows, :], mine[i], lsems.at[i])
            ld.start()
            loads.append(ld)
            sw = remote(g_refs[i].at[:, sib_rows, :], theirs[i], SEMS_PER_ITEM * i, sib)
            sw.start()
            started.append(sw)
        p_swaps = []
        for q in range(np_):
            _, p_sib = halves(prs[q])
            sw = remote(p_refs[q].at[p_sib, :], p_theirs[q], SEMS_PER_ITEM * (n + q), sib)
            sw.start()
            p_swaps.append(sw)
        started += p_swaps

        for q in range(np_):
            p_my, _ = halves(prs[q])
            p_swaps[q].wait_recv()
            gats[q][k] = (p_refs[q][p_my, :] + p_theirs[q][...]).astype(gats[q].dtype)
            for j, (px, py) in enumerate(chips):
                cp = remote(gats[q].at[k], gats[q].at[k], SEMS_PER_ITEM * (n + q) + 1 + j, (px, py, c))
                cp.start()
                started.append(cp)
        ici = []
        for i in range(n):
            loads[i].wait()
            started[i].wait_recv()
            for j, (px, py) in enumerate(chips):
                kp = 2 * px + py
                sendb[i][j] = (mine[i][kp] + theirs[i][kp]).astype(BF16)
                cp = remote(sendb[i].at[j], recvb[i].at[j], SEMS_PER_ITEM * i + 1 + j, (px, py, c))
                cp.start()
                ici.append(cp)

        for q in range(np_):
            p_my, _ = halves(prs[q])
            gat, op = gats[q], ops[q]
            for j, (px, py) in enumerate(chips):
                kp = 2 * px + py
                remote(gat.at[kp], gat.at[kp], SEMS_PER_ITEM * (n + q) + 1 + j, (px, py, c)).wait_recv()
            op[p_my, :] = ((gat[0].astype(F32) + gat[1].astype(F32)) + gat[2].astype(F32)) + gat[3].astype(F32)
            last = remote(op.at[p_my, :], op.at[p_my, :], SEMS_PER_ITEM * (n + q) + 4, sib)
            last.start()
            started.append(last)
        for i in range(n):
            my_rows, _ = halves(dims[i][0])
            half = mine[i][k] + theirs[i][k]
            for j in range(3):
                ici[3 * i + j].wait_recv()
                half = half + recvb[i][j].astype(F32)
            outs[i][my_rows, :] = half
            last = remote(outs[i].at[my_rows, :], outs[i].at[my_rows, :], SEMS_PER_ITEM * i + 4, sib)
            last.start()
            started.append(last)
        for q in range(np_):
            _, p_sib = halves(prs[q])
            remote(ops[q].at[p_sib, :], ops[q].at[p_sib, :], SEMS_PER_ITEM * (n + q) + 4, sib).wait_recv()
        for i in range(n):
            _, sib_rows = halves(dims[i][0])
            remote(outs[i].at[sib_rows, :], outs[i].at[sib_rows, :], SEMS_PER_ITEM * i + 4, sib).wait_recv()
        for cp in started + ici:
            cp.wait_send()

    n_sems = SEMS_PER_ITEM * (n + np_)
    scratch = ([pltpu.VMEM((N_CHIPS, r // 2, cd), F32) for r, cd in dims] * 2
               + [pltpu.VMEM((3, r // 2, cd), BF16) for r, cd in dims] * 2
               + [pltpu.VMEM((pr // 2, LANES), F32) for pr in prs]
               + [pltpu.VMEM((N_CHIPS, pr // 2, LANES), dt) for pr, dt in zip(prs, pack_dtypes)]
               + [pltpu.SemaphoreType.DMA((n,)), pltpu.SemaphoreType.DMA((n_sems,)), pltpu.SemaphoreType.DMA((n_sems,))])
    res = pl.pallas_call(
        body,
        out_shape=[jax.ShapeDtypeStruct(d, F32) for d in dims] + [jax.ShapeDtypeStruct(p.shape, F32) for p in packs],
        in_specs=[ANY_SPEC] * n + [VMEM_SPEC] * np_, out_specs=[VMEM_SPEC] * (n + np_),
        scratch_shapes=scratch,
        compiler_params=pltpu.CompilerParams(vmem_limit_bytes=VMEM_LIMIT),
        name="reduce_all")(*g4s, *packs)
    return res[:n], res[n:]


def adamw_all(ws, gs, ms, vs):
    n = len(ws)
    chunked = [a.ndim == 3 and a.shape[0] == 1 and a.shape[1] % (ADAMW_STEPS * SUBLANES) == 0 for a in ws]

    def body(*refs):
        w, g, m, v = refs[:n], refs[n:2 * n], refs[2 * n:3 * n], refs[3 * n:4 * n]
        od, om, ov = refs[4 * n:5 * n], refs[5 * n:6 * n], refs[6 * n:]
        for i in range(n):
            if chunked[i]:
                od[i][...], om[i][...], ov[i][...] = _adamw(w[i][...], g[i][...], m[i][...], v[i][...])

        @pl.when(pl.program_id(0) == 0)
        def _():
            for i in range(n):
                if not chunked[i]:
                    od[i][...], om[i][...], ov[i][...] = _adamw(w[i][...], g[i][...], m[i][...], v[i][...])

    def spec(a, is_chunked):
        if is_chunked:
            return pl.BlockSpec((1, a.shape[1] // ADAMW_STEPS, a.shape[2]), lambda s: (0, s, 0))
        return pl.BlockSpec(a.shape, lambda s: (0,) * a.ndim)

    specs = [spec(a, c) for a, c in zip(ws, chunked)]
    shapes = [jax.ShapeDtypeStruct(a.shape, F32) for a in ws]
    res = pl.pallas_call(
        body, grid=(ADAMW_STEPS,), out_shape=shapes * 3, in_specs=specs * 4, out_specs=specs * 3,
        compiler_params=_params("arbitrary"), name="adamw_all")(*ws, *gs, *ms, *vs)
    return res[:n], res[n:2 * n], res[2 * n:]


WEIGHTS = ("g_pre", "w_in", "w_pool", "pool_scale", "a_re", "a_im", "log_dt", "b_re", "b_im", "c_re", "c_im", "d_skip",
           "w_glu", "g_mem", "w_kv", "w_out", "g_post")
SHARDED = ("w_in", "w_glu", "w_kv", "w_out")
SWAPPED = ("b_re", "b_im", "w_glu")
MATRICES = ("w_pool", "b_re", "b_im", "c_re", "c_im")
REPLICATED = tuple(n for n in WEIGHTS if n not in SHARDED)
PACK_TILE = SUBLANES * LANES


def _pack(arrays):
    parts = []
    for a in arrays:
        flat = a.reshape(-1)
        parts.append(jnp.pad(flat, (0, -flat.shape[0] % PACK_TILE)).reshape(-1, LANES))
    rows = sum(p.shape[0] for p in parts)
    if rows % (2 * SUBLANES):
        parts.append(jnp.zeros((SUBLANES, LANES), F32))
    return jnp.concatenate(parts, axis=0)


def _unpack(packed, shapes):
    out, row = [], 0
    for shp in shapes:
        size = math.prod(shp)
        rows = -(-size // PACK_TILE) * SUBLANES
        out.append(packed[row:row + rows].reshape(-1)[:size].reshape(shp))
        row += rows
    return out


def kernel(x, mem, g_pre, w_in, w_pool, pool_scale, a_re, a_im, log_dt, b_re, b_im, c_re, c_im, d_skip, w_glu, g_mem, w_kv, w_out, g_post, loss_target, m_g_pre, m_w_in, m_w_pool, m_pool_scale, m_a_re, m_a_im, m_log_dt, m_b_re, m_b_im, m_c_re, m_c_im, m_d_skip, m_w_glu, m_g_mem, m_w_kv, m_w_out, m_g_post, v_g_pre, v_w_in, v_w_pool, v_pool_scale, v_a_re, v_a_im, v_log_dt, v_b_re, v_b_im, v_c_re, v_c_im, v_d_skip, v_w_glu, v_g_mem, v_w_kv, v_w_out, v_g_post):
    given = dict(locals())
    wts = {n: given[n] for n in WEIGHTS}
    mom = {n: given["m_" + n] for n in WEIGHTS}
    var = {n: given["v_" + n] for n in WEIGHTS}

    swap = lambda a: jnp.swapaxes(a, -1, -2)
    view = lambda n, a: swap(a) if n in SWAPPED else a

    loss_part, grad_x, grads = local_step(
        x, mem, loss_target, g_pre, w_in[0], w_pool[0], pool_scale, a_re[0], a_im[0], log_dt[0], b_re[0], b_im[0],
        c_re[0], c_im[0], d_skip, [swap(w_glu)[0], w_kv[0], w_out[0]], g_mem, g_post)

    glu_cols = 2 * SSM_W // N_CHIPS
    partial4 = [
        grads["w_in"],
        grads["w_glu"].reshape(SSM_W, N_CHIPS, glu_cols).transpose(1, 2, 0),
        grads["w_kv"].reshape(N_CHIPS, D_MODEL // N_CHIPS, 2 * ATT_W),
        grads["w_out"].reshape(N_CHIPS, D_MODEL // N_CHIPS, D_MODEL),
    ]
    loss_tile = jnp.broadcast_to(loss_part, (SUBLANES, LANES))
    vectors = [n for n in REPLICATED if n not in MATRICES]
    sharded_g, (pack_v, pack_m) = reduce_all(
        partial4, [_pack([grads[n] for n in vectors] + [loss_tile]), _pack([grads[n] for n in MATRICES])], [F32, BF16])
    vec_g = _unpack(pack_v, [view(n, wts[n]).shape for n in vectors] + [(SUBLANES, LANES)])
    mat_g = _unpack(pack_m, [view(n, wts[n]).shape for n in MATRICES])
    loss = vec_g[-1][0, 0]
    grad = dict(zip(SHARDED, [g[None] for g in sharded_g]))
    grad.update(zip(vectors, vec_g[:-1]))
    grad.update(zip(MATRICES, mat_g))

    deltas, new_ms, new_vs = adamw_all([view(n, wts[n]) for n in WEIGHTS], [grad[n] for n in WEIGHTS],
                                       [view(n, mom[n]) for n in WEIGHTS], [view(n, var[n]) for n in WEIGHTS])
    back = lambda arrs: [view(n, a) for n, a in zip(WEIGHTS, arrs)]
    return (loss, grad_x, *back([grad[n] for n in WEIGHTS]), *back(deltas), *back(new_ms), *back(new_vs))
```
